```python
import math
import jax, jax.numpy as jnp
from jax import lax
import numpy as np

D_MODEL = 1024
BATCH = 8
SEQ = 4096
DEPTH = 2

N_MIXERS = 2
N_POOL_LAYERS = (DEPTH + 1) // 2
N_SSM_LAYERS = DEPTH // 2
RMS_EPS = 1e-5

POOL_WINDOWS = (2, 4, 8, 16)
N_POOL_GROUPS = len(POOL_WINDOWS)
POOL_GROUP = D_MODEL // N_POOL_GROUPS

SSM_EXPAND = 2
D_INNER = SSM_EXPAND * D_MODEL
HEAD_DIM = 64
N_HEADS = D_INNER // HEAD_DIM
N_GROUPS = 4
HEADS_PER_GROUP = N_HEADS // N_GROUPS
D_STATE = 128
CONV_K = 4
CHUNK = 128
CONV_DIM = D_INNER + 2 * N_GROUPS * D_STATE
IN_PROJ_DIM = D_INNER + CONV_DIM + N_HEADS

D_FF = 4 * D_MODEL

kernel_name = "pool_ssd_interleaved_hybrid"


def rms_norm(x, g):
    xf = x.astype(jnp.float32)
    y = xf * lax.rsqrt(jnp.mean(xf * xf, axis=-1, keepdims=True) + RMS_EPS)
    return (y * g.astype(jnp.float32)).astype(x.dtype)


def pool_mixer(h, w_pool, b_pool, scale):
    b, s, d = h.shape
    hf = h.astype(jnp.float32)
    cs = jnp.pad(jnp.cumsum(hf, axis=1), ((0, 0), (1, 0), (0, 0)))
    t = jnp.arange(s)
    hg = hf.reshape(b, s, N_POOL_GROUPS, POOL_GROUP)
    csg = cs.reshape(b, s + 1, N_POOL_GROUPS, POOL_GROUP)
    pooled = []
    for gi, w in enumerate(POOL_WINDOWS):
        c = csg[:, :, gi]
        lower = c[:, jnp.maximum(t + 1 - w, 0)]
        count = jnp.minimum(t + 1, w).astype(jnp.float32)
        pooled.append((c[:, 1:] - lower) / count[None, :, None])
    mixed = jnp.stack(pooled, axis=2) - hg
    out = jnp.einsum("bsgi,gio->bsgo", mixed, w_pool.astype(jnp.float32)).reshape(b, s, d)
    out = (out + b_pool.astype(jnp.float32)) * scale.astype(jnp.float32)
    return out.astype(h.dtype)


def causal_dwconv(u, w, bias):
    s = u.shape[1]
    up = jnp.pad(u, ((0, 0), (CONV_K - 1, 0), (0, 0)))
    out = up[:, 0:s] * w[0]
    for k in range(1, CONV_K):
        out = out + up[:, k:k + s] * w[k]
    return out + bias


def ssd_scan(xs, dt, a, bm, cm):
    b, s, g, r, p = xs.shape
    n = bm.shape[-1]
    nc = s // CHUNK
    x = xs.astype(jnp.float32).reshape(b, nc, CHUNK, g, r, p)
    dtc = dt.astype(jnp.float32).reshape(b, nc, CHUNK, g, r)
    bc = bm.astype(jnp.float32).reshape(b, nc, CHUNK, g, n)
    cc = cm.astype(jnp.float32).reshape(b, nc, CHUNK, g, n)
    xdt = x * dtc[..., None]
    adt = jnp.moveaxis(dtc * a.astype(jnp.float32), 2, -1)
    a_cs = jnp.cumsum(adt, axis=-1)
    causal = jnp.tril(jnp.ones((CHUNK, CHUNK), dtype=bool))
    seg = a_cs[..., :, None] - a_cs[..., None, :]
    decay = jnp.exp(jnp.where(causal, seg, -jnp.inf))
    cb = jnp.einsum("bclgn,bcsgn->bcgls", cc, bc)
    y_diag = jnp.einsum("bcgls,bcgrls,bcsgrp->bclgrp", cb, decay, xdt)
    decay_to_end = jnp.exp(a_cs[..., -1:] - a_cs)
    states = jnp.einsum("bclgn,bcgrl,bclgrp->bcgrpn", bc, decay_to_end, xdt)
    chunk_decay = jnp.exp(a_cs[..., -1])

    def step(hstate, inp):
        st, dec = inp
        return hstate * dec[..., None, None] + st, hstate

    h0 = jnp.zeros((b, g, r, p, n), jnp.float32)
    _, prev = lax.scan(step, h0, (jnp.moveaxis(states, 1, 0), jnp.moveaxis(chunk_decay, 1, 0)))
    prev = jnp.moveaxis(prev, 0, 1)
    y_off = jnp.einsum("bclgn,bcgrpn,bcgrl->bclgrp", cc, prev, jnp.exp(a_cs))
    return (y_diag + y_off).reshape(b, s, g, r, p)


def ssd_mixer(h, w_in, conv_w, conv_b, dt_bias, a_log, d_skip, norm_g, w_out):
    b, s, _ = h.shape
    zxbcdt = h @ w_in
    z = zxbcdt[..., :D_INNER]
    xbc = zxbcdt[..., D_INNER:D_INNER + CONV_DIM]
    dt_raw = zxbcdt[..., D_INNER + CONV_DIM:]
    xbc = jax.nn.silu(causal_dwconv(xbc, conv_w, conv_b))
    xs = xbc[..., :D_INNER].reshape(b, s, N_GROUPS, HEADS_PER_GROUP, HEAD_DIM)
    bm = xbc[..., D_INNER:D_INNER + N_GROUPS * D_STATE].reshape(b, s, N_GROUPS, D_STATE)
    cm = xbc[..., D_INNER + N_GROUPS * D_STATE:].reshape(b, s, N_GROUPS, D_STATE)
    dt = jax.nn.softplus(dt_raw.astype(jnp.float32) + dt_bias.astype(jnp.float32))
    dt = dt.reshape(b, s, N_GROUPS, HEADS_PER_GROUP)
    a = -jnp.exp(a_log.astype(jnp.float32)).reshape(N_GROUPS, HEADS_PER_GROUP)
    y = ssd_scan(xs, dt, a, bm, cm)
    y = y + d_skip.astype(jnp.float32).reshape(N_GROUPS, HEADS_PER_GROUP)[..., None] * xs.astype(jnp.float32)
    y = y.reshape(b, s, D_INNER) * jax.nn.silu(z.astype(jnp.float32))
    yg = y.reshape(b, s, N_GROUPS, D_INNER // N_GROUPS)
    yg = yg * lax.rsqrt(jnp.mean(yg * yg, axis=-1, keepdims=True) + RMS_EPS)
    y = (yg.reshape(b, s, D_INNER) * norm_g.astype(jnp.float32)).astype(h.dtype)
    return y @ w_out


def sq_relu_mlp(h, w1, w2):
    u = jax.nn.relu(h @ w1)
    return (u * u) @ w2


def _fwd_setup_inputs(seed: int = 0) -> dict:
    key = jax.random.key(seed)
    ks = jax.random.split(key, 20)
    f32 = jnp.float32
    nrm = lambda k, shape, scale: jax.random.normal(k, shape, f32) * scale
    x = jax.random.normal(ks[0], (BATCH, SEQ, D_MODEL), f32)
    norm_mix_g = 1.0 + nrm(ks[1], (DEPTH, D_MODEL), 0.05)
    norm_mlp_g = 1.0 + nrm(ks[2], (DEPTH, D_MODEL), 0.05)
    pool_w = nrm(ks[3], (N_POOL_LAYERS, N_POOL_GROUPS, POOL_GROUP, POOL_GROUP), POOL_GROUP ** -0.5)
    pool_b = nrm(ks[4], (N_POOL_LAYERS, D_MODEL), 0.02)
    pool_scale = 0.5 + nrm(ks[5], (N_POOL_LAYERS, D_MODEL), 0.05)
    ssm_w_in = nrm(ks[6], (N_SSM_LAYERS, D_MODEL, IN_PROJ_DIM), D_MODEL ** -0.5)
    ssm_conv_w = nrm(ks[7], (N_SSM_LAYERS, CONV_K, CONV_DIM), CONV_K ** -0.5)
    ssm_conv_b = nrm(ks[8], (N_SSM_LAYERS, CONV_DIM), 0.02)
    dt0 = jnp.exp(jax.random.uniform(ks[9], (N_SSM_LAYERS, N_HEADS), f32)
                  * (math.log(0.1) - math.log(0.001)) + math.log(0.001))
    ssm_dt_bias = dt0 + jnp.log(-jnp.expm1(-dt0))
    ssm_a_log = jnp.log(jax.random.uniform(ks[10], (N_SSM_LAYERS, N_HEADS), f32, 1.0, 16.0))
    ssm_d = 1.0 + nrm(ks[11], (N_SSM_LAYERS, N_HEADS), 0.1)
    ssm_norm_g = 1.0 + nrm(ks[12], (N_SSM_LAYERS, D_INNER), 0.05)
    ssm_w_out = nrm(ks[13], (N_SSM_LAYERS, D_INNER, D_MODEL), D_INNER ** -0.5)
    mlp_w1 = nrm(ks[14], (DEPTH, D_MODEL, D_FF), D_MODEL ** -0.5)
    mlp_w2 = nrm(ks[15], (DEPTH, D_FF, D_MODEL), D_FF ** -0.5)
    final_g = 1.0 + nrm(ks[16], (D_MODEL,), 0.05)
    return {"x": x, "norm_mix_g": norm_mix_g, "norm_mlp_g": norm_mlp_g,
            "pool_w": pool_w, "pool_b": pool_b, "pool_scale": pool_scale,
            "ssm_w_in": ssm_w_in, "ssm_conv_w": ssm_conv_w, "ssm_conv_b": ssm_conv_b,
            "ssm_dt_bias": ssm_dt_bias, "ssm_a_log": ssm_a_log, "ssm_d": ssm_d,
            "ssm_norm_g": ssm_norm_g, "ssm_w_out": ssm_w_out,
            "mlp_w1": mlp_w1, "mlp_w2": mlp_w2, "final_g": final_g}


def _fwd_reference(x, norm_mix_g, norm_mlp_g, pool_w, pool_b, pool_scale,
              ssm_w_in, ssm_conv_w, ssm_conv_b, ssm_dt_bias, ssm_a_log, ssm_d,
              ssm_norm_g, ssm_w_out, mlp_w1, mlp_w2, final_g):
    h = x
    for i in range(DEPTH):
        hn = rms_norm(h, norm_mix_g[i])
        if i % N_MIXERS == 0:
            j = i // N_MIXERS
            mix = pool_mixer(hn, pool_w[j], pool_b[j], pool_scale[j])
        else:
            j = i // N_MIXERS
            mix = ssd_mixer(hn, ssm_w_in[j], ssm_conv_w[j], ssm_conv_b[j], ssm_dt_bias[j],
                            ssm_a_log[j], ssm_d[j], ssm_norm_g[j], ssm_w_out[j])
        h = h + mix
        h = h + sq_relu_mlp(rms_norm(h, norm_mlp_g[i]), mlp_w1[i], mlp_w2[i])
    return rms_norm(h, final_g)


import jax as _jax
import jax.numpy as _jnp

TWIN_FORMAT = 'train_step'
FWD_PARAMS = ['x', 'norm_mix_g', 'norm_mlp_g', 'pool_w', 'pool_b', 'pool_scale', 'ssm_w_in', 'ssm_conv_w', 'ssm_conv_b', 'ssm_dt_bias', 'ssm_a_log', 'ssm_d', 'ssm_norm_g', 'ssm_w_out', 'mlp_w1', 'mlp_w2', 'final_g']
TWIN_WEIGHTS = ['norm_mix_g', 'norm_mlp_g', 'pool_w', 'pool_b', 'pool_scale', 'ssm_w_in', 'ssm_conv_w', 'ssm_conv_b', 'ssm_dt_bias', 'ssm_a_log', 'ssm_d', 'ssm_norm_g', 'ssm_w_out', 'mlp_w1', 'mlp_w2', 'final_g']
TWIN_DIFF_INPUT = 'x'
TWIN_INPUTS = ['x', 'norm_mix_g', 'norm_mlp_g', 'pool_w', 'pool_b', 'pool_scale', 'ssm_w_in', 'ssm_conv_w', 'ssm_conv_b', 'ssm_dt_bias', 'ssm_a_log', 'ssm_d', 'ssm_norm_g', 'ssm_w_out', 'mlp_w1', 'mlp_w2', 'final_g', 'loss_target', 'm_norm_mix_g', 'm_norm_mlp_g', 'm_pool_w', 'm_pool_b', 'm_pool_scale', 'm_ssm_w_in', 'm_ssm_conv_w', 'm_ssm_conv_b', 'm_ssm_dt_bias', 'm_ssm_a_log', 'm_ssm_d', 'm_ssm_norm_g', 'm_ssm_w_out', 'm_mlp_w1', 'm_mlp_w2', 'm_final_g', 'v_norm_mix_g', 'v_norm_mlp_g', 'v_pool_w', 'v_pool_b', 'v_pool_scale', 'v_ssm_w_in', 'v_ssm_conv_w', 'v_ssm_conv_b', 'v_ssm_dt_bias', 'v_ssm_a_log', 'v_ssm_d', 'v_ssm_norm_g', 'v_ssm_w_out', 'v_mlp_w1', 'v_mlp_w2', 'v_final_g']
TWIN_OUTPUTS = ['loss', 'grad_x', 'grad_norm_mix_g', 'grad_norm_mlp_g', 'grad_pool_w', 'grad_pool_b', 'grad_pool_scale', 'grad_ssm_w_in', 'grad_ssm_conv_w', 'grad_ssm_conv_b', 'grad_ssm_dt_bias', 'grad_ssm_a_log', 'grad_ssm_d', 'grad_ssm_norm_g', 'grad_ssm_w_out', 'grad_mlp_w1', 'grad_mlp_w2', 'grad_final_g', 'delta_norm_mix_g', 'delta_norm_mlp_g', 'delta_pool_w', 'delta_pool_b', 'delta_pool_scale', 'delta_ssm_w_in', 'delta_ssm_conv_w', 'delta_ssm_conv_b', 'delta_ssm_dt_bias', 'delta_ssm_a_log', 'delta_ssm_d', 'delta_ssm_norm_g', 'delta_ssm_w_out', 'delta_mlp_w1', 'delta_mlp_w2', 'delta_final_g', 'new_m_norm_mix_g', 'new_m_norm_mlp_g', 'new_m_pool_w', 'new_m_pool_b', 'new_m_pool_scale', 'new_m_ssm_w_in', 'new_m_ssm_conv_w', 'new_m_ssm_conv_b', 'new_m_ssm_dt_bias', 'new_m_ssm_a_log', 'new_m_ssm_d', 'new_m_ssm_norm_g', 'new_m_ssm_w_out', 'new_m_mlp_w1', 'new_m_mlp_w2', 'new_m_final_g', 'new_v_norm_mix_g', 'new_v_norm_mlp_g', 'new_v_pool_w', 'new_v_pool_b', 'new_v_pool_scale', 'new_v_ssm_w_in', 'new_v_ssm_conv_w', 'new_v_ssm_conv_b', 'new_v_ssm_dt_bias', 'new_v_ssm_a_log', 'new_v_ssm_d', 'new_v_ssm_norm_g', 'new_v_ssm_w_out', 'new_v_mlp_w1', 'new_v_mlp_w2', 'new_v_final_g']
TWIN_LEAF_KINDS = {'loss': 'loss', 'grad_x': 'grad_x', 'grad_norm_mix_g': 'grad_w', 'grad_norm_mlp_g': 'grad_w', 'grad_pool_w': 'grad_w', 'grad_pool_b': 'grad_w', 'grad_pool_scale': 'grad_w', 'grad_ssm_w_in': 'grad_w', 'grad_ssm_conv_w': 'grad_w', 'grad_ssm_conv_b': 'grad_w', 'grad_ssm_dt_bias': 'grad_w', 'grad_ssm_a_log': 'grad_w', 'grad_ssm_d': 'grad_w', 'grad_ssm_norm_g': 'grad_w', 'grad_ssm_w_out': 'grad_w', 'grad_mlp_w1': 'grad_w', 'grad_mlp_w2': 'grad_w', 'grad_final_g': 'grad_w', 'delta_norm_mix_g': 'delta_w', 'delta_norm_mlp_g': 'delta_w', 'delta_pool_w': 'delta_w', 'delta_pool_b': 'delta_w', 'delta_pool_scale': 'delta_w', 'delta_ssm_w_in': 'delta_w', 'delta_ssm_conv_w': 'delta_w', 'delta_ssm_conv_b': 'delta_w', 'delta_ssm_dt_bias': 'delta_w', 'delta_ssm_a_log': 'delta_w', 'delta_ssm_d': 'delta_w', 'delta_ssm_norm_g': 'delta_w', 'delta_ssm_w_out': 'delta_w', 'delta_mlp_w1': 'delta_w', 'delta_mlp_w2': 'delta_w', 'delta_final_g': 'delta_w', 'new_m_norm_mix_g': 'new_m', 'new_m_norm_mlp_g': 'new_m', 'new_m_pool_w': 'new_m', 'new_m_pool_b': 'new_m', 'new_m_pool_scale': 'new_m', 'new_m_ssm_w_in': 'new_m', 'new_m_ssm_conv_w': 'new_m', 'new_m_ssm_conv_b': 'new_m', 'new_m_ssm_dt_bias': 'new_m', 'new_m_ssm_a_log': 'new_m', 'new_m_ssm_d': 'new_m', 'new_m_ssm_norm_g': 'new_m', 'new_m_ssm_w_out': 'new_m', 'new_m_mlp_w1': 'new_m', 'new_m_mlp_w2': 'new_m', 'new_m_final_g': 'new_m', 'new_v_norm_mix_g': 'new_v', 'new_v_norm_mlp_g': 'new_v', 'new_v_pool_w': 'new_v', 'new_v_pool_b': 'new_v', 'new_v_pool_scale': 'new_v', 'new_v_ssm_w_in': 'new_v', 'new_v_ssm_conv_w': 'new_v', 'new_v_ssm_conv_b': 'new_v', 'new_v_ssm_dt_bias': 'new_v', 'new_v_ssm_a_log': 'new_v', 'new_v_ssm_d': 'new_v', 'new_v_ssm_norm_g': 'new_v', 'new_v_ssm_w_out': 'new_v', 'new_v_mlp_w1': 'new_v', 'new_v_mlp_w2': 'new_v', 'new_v_final_g': 'new_v'}


def _forward(args):
    return _fwd_reference(*[args[k] for k in FWD_PARAMS])


def _output_shape():
    def fwd():
        inp = _fwd_setup_inputs(0)
        return _fwd_reference(*[inp[k] for k in FWD_PARAMS])
    out = _jax.eval_shape(fwd)
    return out.shape, out.dtype

N_MICROBATCH = 1
ADAM_LR = 0.001
ADAM_B1 = 0.9
ADAM_B2 = 0.999
ADAM_EPS = 1e-08
ADAM_WD = 0.01
ADAM_STEP = 10
PER_EXAMPLE_BATCH_AXIS = {'x': 0, 'loss_target': 0}
SHARED_INPUTS = []
_WEIGHT_DTYPES = {'norm_mix_g': _jnp.float32, 'norm_mlp_g': _jnp.float32, 'pool_w': _jnp.float32, 'pool_b': _jnp.float32, 'pool_scale': _jnp.float32, 'ssm_w_in': _jnp.float32, 'ssm_conv_w': _jnp.float32, 'ssm_conv_b': _jnp.float32, 'ssm_dt_bias': _jnp.float32, 'ssm_a_log': _jnp.float32, 'ssm_d': _jnp.float32, 'ssm_norm_g': _jnp.float32, 'ssm_w_out': _jnp.float32, 'mlp_w1': _jnp.float32, 'mlp_w2': _jnp.float32, 'final_g': _jnp.float32}
MOMENT_SCALE = {'norm_mix_g': 1.345133e-01, 'norm_mlp_g': 1.740716e-01, 'pool_w': 9.527643e-02, 'pool_b': 3.161681e-01, 'pool_scale': 2.686039e-01, 'ssm_w_in': 7.547216e-02, 'ssm_conv_w': 8.082230e-02, 'ssm_conv_b': 1.531471e-01, 'ssm_dt_bias': 1.375748e-01, 'ssm_a_log': 3.582930e-01, 'ssm_d': 3.953531e-01, 'ssm_norm_g': 9.426417e-02, 'ssm_w_out': 1.445772e-01, 'mlp_w1': 8.273635e-02, 'mlp_w2': 2.539186e-01, 'final_g': 3.261925e+01}


def _to_microbatches(a, axis):
    t = _jnp.moveaxis(a, axis, 0)
    t = t.reshape((N_MICROBATCH, t.shape[0] // N_MICROBATCH) + t.shape[1:])
    return _jnp.moveaxis(t, 1, axis + 1)


def setup_inputs(seed: int = 0) -> dict:
    inp = _fwd_setup_inputs(seed)
    key = _jax.random.fold_in(_jax.random.key(seed), 7919)
    shape, _ = _output_shape()
    out = dict(inp)
    out["loss_target"] = _jax.random.normal(_jax.random.fold_in(key, 0), shape, _jnp.float32)
    for i, name in enumerate(TWIN_WEIGHTS):
        w = inp[name].astype(_jnp.float32)
        if MOMENT_SCALE is None:
            s = _jnp.sqrt(_jnp.mean(_jnp.square(w)) + 1e-30)
        else:
            s = MOMENT_SCALE[name]
        km, kv = _jax.random.split(_jax.random.fold_in(key, i + 1))
        out[name] = w
        out["m_" + name] = s * _jax.random.normal(km, w.shape, _jnp.float32)
        out["v_" + name] = (s * s) * _jax.random.uniform(kv, w.shape, _jnp.float32, 0.5, 1.5)
    if N_MICROBATCH > 1:
        for name, axis in PER_EXAMPLE_BATCH_AXIS.items():
            out[name] = _to_microbatches(out[name], axis)
    return {'x': out['x'], 'norm_mix_g': out['norm_mix_g'], 'norm_mlp_g': out['norm_mlp_g'], 'pool_w': out['pool_w'], 'pool_b': out['pool_b'], 'pool_scale': out['pool_scale'], 'ssm_w_in': out['ssm_w_in'], 'ssm_conv_w': out['ssm_conv_w'], 'ssm_conv_b': out['ssm_conv_b'], 'ssm_dt_bias': out['ssm_dt_bias'], 'ssm_a_log': out['ssm_a_log'], 'ssm_d': out['ssm_d'], 'ssm_norm_g': out['ssm_norm_g'], 'ssm_w_out': out['ssm_w_out'], 'mlp_w1': out['mlp_w1'], 'mlp_w2': out['mlp_w2'], 'final_g': out['final_g'], 'loss_target': out['loss_target'], 'm_norm_mix_g': out['m_norm_mix_g'], 'm_norm_mlp_g': out['m_norm_mlp_g'], 'm_pool_w': out['m_pool_w'], 'm_pool_b': out['m_pool_b'], 'm_pool_scale': out['m_pool_scale'], 'm_ssm_w_in': out['m_ssm_w_in'], 'm_ssm_conv_w': out['m_ssm_conv_w'], 'm_ssm_conv_b': out['m_ssm_conv_b'], 'm_ssm_dt_bias': out['m_ssm_dt_bias'], 'm_ssm_a_log': out['m_ssm_a_log'], 'm_ssm_d': out['m_ssm_d'], 'm_ssm_norm_g': out['m_ssm_norm_g'], 'm_ssm_w_out': out['m_ssm_w_out'], 'm_mlp_w1': out['m_mlp_w1'], 'm_mlp_w2': out['m_mlp_w2'], 'm_final_g': out['m_final_g'], 'v_norm_mix_g': out['v_norm_mix_g'], 'v_norm_mlp_g': out['v_norm_mlp_g'], 'v_pool_w': out['v_pool_w'], 'v_pool_b': out['v_pool_b'], 'v_pool_scale': out['v_pool_scale'], 'v_ssm_w_in': out['v_ssm_w_in'], 'v_ssm_conv_w': out['v_ssm_conv_w'], 'v_ssm_conv_b': out['v_ssm_conv_b'], 'v_ssm_dt_bias': out['v_ssm_dt_bias'], 'v_ssm_a_log': out['v_ssm_a_log'], 'v_ssm_d': out['v_ssm_d'], 'v_ssm_norm_g': out['v_ssm_norm_g'], 'v_ssm_w_out': out['v_ssm_w_out'], 'v_mlp_w1': out['v_mlp_w1'], 'v_mlp_w2': out['v_mlp_w2'], 'v_final_g': out['v_final_g']}


def _loss(weights, diff, rest, loss_target):
    with _jax.named_scope("forward"):
        args = {**rest, TWIN_DIFF_INPUT: diff, **{k: w.astype(_WEIGHT_DTYPES[k]) for k, w in weights.items()}}
        y = _forward(args)
    with _jax.named_scope("loss_head"):
        err = _jnp.square(y.astype(_jnp.float32) - loss_target)
        return 0.5 * _jnp.sum(_jnp.mean(err, axis=-1)) if err.ndim else 0.5 * err


def _adamw(w, g, m, v):
    m = ADAM_B1 * m + (1.0 - ADAM_B1) * g
    v = ADAM_B2 * v + (1.0 - ADAM_B2) * _jnp.square(g)
    m_hat = m / (1.0 - ADAM_B1 ** ADAM_STEP)
    v_hat = v / (1.0 - ADAM_B2 ** ADAM_STEP)
    delta = -ADAM_LR * (m_hat / (_jnp.sqrt(v_hat) + ADAM_EPS) + ADAM_WD * w)
    return delta, m, v


def reference(x, norm_mix_g, norm_mlp_g, pool_w, pool_b, pool_scale, ssm_w_in, ssm_conv_w, ssm_conv_b, ssm_dt_bias, ssm_a_log, ssm_d, ssm_norm_g, ssm_w_out, mlp_w1, mlp_w2, final_g, loss_target, m_norm_mix_g, m_norm_mlp_g, m_pool_w, m_pool_b, m_pool_scale, m_ssm_w_in, m_ssm_conv_w, m_ssm_conv_b, m_ssm_dt_bias, m_ssm_a_log, m_ssm_d, m_ssm_norm_g, m_ssm_w_out, m_mlp_w1, m_mlp_w2, m_final_g, v_norm_mix_g, v_norm_mlp_g, v_pool_w, v_pool_b, v_pool_scale, v_ssm_w_in, v_ssm_conv_w, v_ssm_conv_b, v_ssm_dt_bias, v_ssm_a_log, v_ssm_d, v_ssm_norm_g, v_ssm_w_out, v_mlp_w1, v_mlp_w2, v_final_g):
    given = dict(x=x, norm_mix_g=norm_mix_g, norm_mlp_g=norm_mlp_g, pool_w=pool_w, pool_b=pool_b, pool_scale=pool_scale, ssm_w_in=ssm_w_in, ssm_conv_w=ssm_conv_w, ssm_conv_b=ssm_conv_b, ssm_dt_bias=ssm_dt_bias, ssm_a_log=ssm_a_log, ssm_d=ssm_d, ssm_norm_g=ssm_norm_g, ssm_w_out=ssm_w_out, mlp_w1=mlp_w1, mlp_w2=mlp_w2, final_g=final_g, loss_target=loss_target, m_norm_mix_g=m_norm_mix_g, m_norm_mlp_g=m_norm_mlp_g, m_pool_w=m_pool_w, m_pool_b=m_pool_b, m_pool_scale=m_pool_scale, m_ssm_w_in=m_ssm_w_in, m_ssm_conv_w=m_ssm_conv_w, m_ssm_conv_b=m_ssm_conv_b, m_ssm_dt_bias=m_ssm_dt_bias, m_ssm_a_log=m_ssm_a_log, m_ssm_d=m_ssm_d, m_ssm_norm_g=m_ssm_norm_g, m_ssm_w_out=m_ssm_w_out, m_mlp_w1=m_mlp_w1, m_mlp_w2=m_mlp_w2, m_final_g=m_final_g, v_norm_mix_g=v_norm_mix_g, v_norm_mlp_g=v_norm_mlp_g, v_pool_w=v_pool_w, v_pool_b=v_pool_b, v_pool_scale=v_pool_scale, v_ssm_w_in=v_ssm_w_in, v_ssm_conv_w=v_ssm_conv_w, v_ssm_conv_b=v_ssm_conv_b, v_ssm_dt_bias=v_ssm_dt_bias, v_ssm_a_log=v_ssm_a_log, v_ssm_d=v_ssm_d, v_ssm_norm_g=v_ssm_norm_g, v_ssm_w_out=v_ssm_w_out, v_mlp_w1=v_mlp_w1, v_mlp_w2=v_mlp_w2, v_final_g=v_final_g)
    weights = {n: given[n] for n in TWIN_WEIGHTS}
    shared = {n: given[n] for n in SHARED_INPUTS}
    per_example = {n: given[n] for n in ['x']}
    grad_fn = _jax.value_and_grad(_loss, argnums=(0, 1))

    def one_microbatch(ex, loss_target):
        ex = dict(ex)
        diff = ex.pop(TWIN_DIFF_INPUT)
        return grad_fn(weights, diff, {**shared, **ex}, loss_target)

    if N_MICROBATCH == 1:
        loss, (grad_w, grad_x) = one_microbatch(per_example, given["loss_target"])
    else:
        def body(carry, xs):
            loss_sum, grad_sum = carry
            l_k, (gw_k, gx_k) = one_microbatch(xs[0], xs[1])
            with _jax.named_scope("update"):
                return (loss_sum + l_k, _jax.tree.map(_jnp.add, grad_sum, gw_k)), gx_k

        init = (_jnp.zeros((), _jnp.float32), _jax.tree.map(_jnp.zeros_like, weights))
        (loss, grad_w), grad_x = _jax.lax.scan(body, init, (per_example, given["loss_target"]))
    with _jax.named_scope("update"):
        delta_w, new_m, new_v = {}, {}, {}
        for n in TWIN_WEIGHTS:
            delta_w[n], new_m[n], new_v[n] = _adamw(weights[n], grad_w[n], given["m_" + n], given["v_" + n])
    return (loss, grad_x, *[grad_w[n] for n in TWIN_WEIGHTS], *[delta_w[n] for n in TWIN_WEIGHTS],
            *[new_m[n] for n in TWIN_WEIGHTS], *[new_v[n] for n in TWIN_WEIGHTS])
```

```python
import functools

import jax
import jax.numpy as jnp
from jax import lax
from jax.experimental import pallas as pl
from jax.experimental.pallas import tpu as pltpu

F32 = jnp.float32
BF16 = jnp.bfloat16
MESH = pl.DeviceIdType.MESH

D_MODEL = 1024
RMS_EPS = 1e-5
POOL_WINDOWS = (2, 4, 8, 16)
POOL_GROUP = 256
POOL_HALO = 16
D_INNER = 2048
HEAD_DIM = 64
N_HEADS = 32
N_GROUPS = 4
HEADS_PER_GROUP = 8
D_STATE = 128
CONV_K = 4
CONV_HALO = 8
CHUNK = 128
CONV_DIM = 3072
IN_PROJ_DIM = 5152
D_FF = 4096
N_CHIPS = 4
N_DEV = 8

ADAM_LR = 0.001
ADAM_B1 = 0.9
ADAM_B2 = 0.999
ADAM_EPS = 1e-08
ADAM_WD = 0.01
ADAM_STEP = 10

VMEM_LIMIT = 56 * 1024 * 1024
NEG_INF = float("-inf")


def _pcall(body, **kw):
    return pl.pallas_call(body, **kw)


def _params(*sem):
    return pltpu.CompilerParams(dimension_semantics=sem, vmem_limit_bytes=VMEM_LIMIT)


def _sigmoid(v):
    return 1.0 / (1.0 + jnp.exp(-v))


def _row_spec(tb, d, nb=None, reverse=False):
    if reverse:
        return pl.BlockSpec((tb, d), lambda i: (nb - 1 - i, 0))
    return pl.BlockSpec((tb, d), lambda i: (i, 0))


def _const_spec(shape):
    return pl.BlockSpec(shape, lambda *_: tuple(0 for _ in shape))


_DIMS = {"nn": (((1,), (0,)), ((), ())),
         "nt": (((1,), (1,)), ((), ())),
         "tn": (((0,), (0,)), ((), ()))}


def _matmul(a, b, mode, *, name, out_dtype=F32, tm=1024, tn=1024, tk=512,
            a_relu2=False, add=None, relu2_grad_of=None, out_col_shards=1):
    if mode == "tn":
        k_dim, m_dim = a.shape
    else:
        m_dim, k_dim = a.shape
    n_dim = b.shape[0] if mode == "nt" else b.shape[1]
    tm, tn, tk = min(tm, m_dim), min(tn, n_dim), min(tk, k_dim)
    assert m_dim % tm == 0 and n_dim % tn == 0 and k_dim % tk == 0
    nk = k_dim // tk
    a_spec = (pl.BlockSpec((tk, tm), lambda i, j, k: (k, i)) if mode == "tn"
              else pl.BlockSpec((tm, tk), lambda i, j, k: (i, k)))
    b_spec = (pl.BlockSpec((tn, tk), lambda i, j, k: (j, k)) if mode == "nt"
              else pl.BlockSpec((tk, tn), lambda i, j, k: (k, j)))
    mn_spec = pl.BlockSpec((tm, tn), lambda i, j, k: (i, j))
    operands, in_specs = [a, b], [a_spec, b_spec]
    if relu2_grad_of is not None:
        operands.append(relu2_grad_of)
        in_specs.append(mn_spec)
    if add is not None:
        operands.append(add)
        in_specs.append(mn_spec)
    if out_col_shards == 1:
        out_shape = jax.ShapeDtypeStruct((m_dim, n_dim), out_dtype)
        out_spec = mn_spec
    else:
        n_shard = n_dim // out_col_shards
        assert n_shard % tn == 0
        per = n_shard // tn
        out_shape = jax.ShapeDtypeStruct((out_col_shards, m_dim, n_shard), out_dtype)
        out_spec = pl.BlockSpec((None, tm, tn), lambda i, j, k: (j // per, i, j % per))

    def body(*refs):
        a_ref, b_ref = refs[0], refs[1]
        o_ref, acc = refs[-2], refs[-1]
        k = pl.program_id(2)

        @pl.when(k == 0)
        def _():
            acc[...] = jnp.zeros_like(acc)

        av = a_ref[...]
        if a_relu2:
            av = jnp.maximum(av.astype(F32), 0.0)
            av = av * av
        acc[...] += lax.dot_general(av.astype(BF16), b_ref[...].astype(BF16), _DIMS[mode],
                                    preferred_element_type=F32)

        @pl.when(k == nk - 1)
        def _():
            r = acc[...]
            nxt = 2
            if relu2_grad_of is not None:
                r = r * (2.0 * jnp.maximum(refs[nxt][...].astype(F32), 0.0))
                nxt += 1
            if add is not None:
                r = r + refs[nxt][...]
            o_ref[...] = r.astype(out_dtype)

    return _pcall(
        body, name=name, out_shape=out_shape,
        grid=(m_dim // tm, n_dim // tn, nk),
        in_specs=in_specs, out_specs=out_spec,
        scratch_shapes=[pltpu.VMEM((tm, tn), F32)],
        compiler_params=_params("parallel", "parallel", "arbitrary"),
    )(*operands)


def _rms(x):
    return lax.rsqrt(jnp.mean(x * x, axis=-1, keepdims=True) + RMS_EPS)


def _rmsnorm_fwd(h, g, *, name, tb=512):
    t_dim, d = h.shape

    def body(h_ref, g_ref, o_ref):
        x = h_ref[...]
        o_ref[...] = (x * _rms(x) * g_ref[...]).astype(BF16)

    return _pcall(
        body, name=name, out_shape=jax.ShapeDtypeStruct((t_dim, d), BF16),
        grid=(t_dim // tb,), in_specs=[_row_spec(tb, d), _const_spec((1, d))],
        out_specs=_row_spec(tb, d), compiler_params=_params("parallel"),
    )(h, g)


def _rmsnorm_bwd(dy, h, g, dres, *, name, tb=512):
    t_dim, d = h.shape

    def body(dy_ref, h_ref, g_ref, dres_ref, dh_ref, dg_ref):
        @pl.when(pl.program_id(0) == 0)
        def _():
            dg_ref[...] = jnp.zeros_like(dg_ref)

        x = h_ref[...]
        r = _rms(x)
        xhat = x * r
        dyv = dy_ref[...]
        dxhat = dyv * g_ref[...]
        dh_ref[...] = dres_ref[...] + r * (dxhat - xhat * jnp.mean(dxhat * xhat, axis=-1, keepdims=True))
        dg_ref[0:1, :] += jnp.sum(dyv * xhat, axis=0, keepdims=True)

    return _pcall(
        body, name=name,
        out_shape=(jax.ShapeDtypeStruct((t_dim, d), F32), jax.ShapeDtypeStruct((8, d), F32)),
        grid=(t_dim // tb,),
        in_specs=[_row_spec(tb, d), _row_spec(tb, d), _const_spec((1, d)), _row_spec(tb, d)],
        out_specs=(_row_spec(tb, d), _const_spec((8, d))),
        compiler_params=_params("arbitrary"),
    )(dy, h, g, dres)


def _pool_mixed(ext, hn, t0, tb):
    t = t0 + lax.broadcasted_iota(jnp.int32, (tb, 1), 0)
    parts = []
    for gi, w in enumerate(POOL_WINDOWS):
        lanes = slice(gi * POOL_GROUP, (gi + 1) * POOL_GROUP)
        s = ext[:, lanes]
        k = 1
        while k < w:
            s = s + pltpu.roll(s, k, 0)
            k *= 2
        cnt = jnp.minimum(t + 1, w).astype(F32)
        parts.append(s[POOL_HALO:, :] / cnt - hn[:, lanes])
    return parts


def _pool_fwd(x, g, pw, pb, ps, *, tb=512):
    t_dim, d = x.shape

    def body(x_ref, g_ref, pw_ref, pb_ref, ps_ref, o_ref, ext_ref):
        i = pl.program_id(0)

        @pl.when(i == 0)
        def _():
            ext_ref[0:POOL_HALO, :] = jnp.zeros((POOL_HALO, d), F32)

        xv = x_ref[...]
        hn = xv * _rms(xv) * g_ref[...]
        ext_ref[POOL_HALO:, :] = hn
        mixed = _pool_mixed(ext_ref[...], hn, i * tb, tb)
        for gi in range(len(POOL_WINDOWS)):
            lanes = slice(gi * POOL_GROUP, (gi + 1) * POOL_GROUP)
            out = jnp.dot(mixed[gi].astype(BF16), pw_ref[gi], preferred_element_type=F32)
            o_ref[:, lanes] = xv[:, lanes] + (out + pb_ref[:, lanes]) * ps_ref[:, lanes]
        ext_ref[0:POOL_HALO, :] = hn[tb - POOL_HALO:, :]

    return _pcall(
        body, name="pool_fwd", out_shape=jax.ShapeDtypeStruct((t_dim, d), F32),
        grid=(t_dim // tb,),
        in_specs=[_row_spec(tb, d), _const_spec((1, d)), _const_spec((4, POOL_GROUP, POOL_GROUP)),
                  _const_spec((1, d)), _const_spec((1, d))],
        out_specs=_row_spec(tb, d),
        scratch_shapes=[pltpu.VMEM((POOL_HALO + tb, d), F32)],
        compiler_params=_params("arbitrary"),
    )(x, g, pw, pb, ps)


def _pool_bwd(x, g, pw, pb, ps, dh1, *, tb=512):
    t_dim, d = x.shape
    nb = t_dim // tb
    halo_per_block = tb // POOL_HALO

    def body(x_ref, xprev_ref, g_ref, pw_ref, pb_ref, ps_ref, dh1_ref,
             dx_ref, dpw_ref, small_ref, ext_ref, dext_ref):
        i = pl.program_id(0)
        blk = nb - 1 - i

        @pl.when(i == 0)
        def _():
            dpw_ref[...] = jnp.zeros_like(dpw_ref)
            small_ref[...] = jnp.zeros_like(small_ref)
            dext_ref[tb:, :] = jnp.zeros((POOL_HALO, d), F32)

        gv = g_ref[...]
        xv = x_ref[...]
        r = _rms(xv)
        xhat = xv * r
        hn = xhat * gv
        xp = xprev_ref[...]
        hprev = xp * _rms(xp) * gv * (blk > 0).astype(F32)
        ext_ref[0:POOL_HALO, :] = hprev
        ext_ref[POOL_HALO:, :] = hn
        mixed = _pool_mixed(ext_ref[...], hn, blk * tb, tb)

        dout = dh1_ref[...]
        t = blk * tb + lax.broadcasted_iota(jnp.int32, (tb, 1), 0)
        for gi, w in enumerate(POOL_WINDOWS):
            lanes = slice(gi * POOL_GROUP, (gi + 1) * POOL_GROUP)
            mb = mixed[gi].astype(BF16)
            pre = jnp.dot(mb, pw_ref[gi], preferred_element_type=F32) + pb_ref[:, lanes]
            dg_out = dout[:, lanes]
            small_ref[2:3, lanes] += jnp.sum(dg_out * pre, axis=0, keepdims=True)
            dpre = dg_out * ps_ref[:, lanes]
            small_ref[1:2, lanes] += jnp.sum(dpre, axis=0, keepdims=True)
            dpb16 = dpre.astype(BF16)
            dpw_ref[gi] += lax.dot_general(mb, dpb16, _DIMS["tn"], preferred_element_type=F32)
            dmixed = lax.dot_general(dpb16, pw_ref[gi], _DIMS["nt"], preferred_element_type=F32)
            cnt = jnp.minimum(t + 1, w).astype(F32)
            dq = dmixed / cnt
            dext_ref[0:tb, lanes] = dq
            s = dext_ref[:, lanes]
            k = 1
            while k < w:
                s = s + pltpu.roll(s, tb + POOL_HALO - k, 0)
                k *= 2
            dhn = s[0:tb, :] - dmixed
            dext_ref[tb:, lanes] = dq[0:POOL_HALO, :]
            small_ref[0:1, lanes] += jnp.sum(dhn * xhat[:, lanes], axis=0, keepdims=True)
            ext_ref[POOL_HALO:, lanes] = dhn * gv[:, lanes]
        dxhat = ext_ref[POOL_HALO:, :]
        dx_ref[...] = dout + r * (dxhat - xhat * jnp.mean(dxhat * xhat, axis=-1, keepdims=True))

    return _pcall(
        body, name="pool_bwd",
        out_shape=(jax.ShapeDtypeStruct((t_dim, d), F32),
                   jax.ShapeDtypeStruct((4, POOL_GROUP, POOL_GROUP), F32),
                   jax.ShapeDtypeStruct((8, d), F32)),
        grid=(nb,),
        in_specs=[_row_spec(tb, d, nb, True),
                  pl.BlockSpec((POOL_HALO, d),
                               lambda i: (jnp.maximum((nb - 1 - i) * halo_per_block - 1, 0), 0)),
                  _const_spec((1, d)), _const_spec((4, POOL_GROUP, POOL_GROUP)),
                  _const_spec((1, d)), _const_spec((1, d)), _row_spec(tb, d, nb, True)],
        out_specs=(_row_spec(tb, d, nb, True), _const_spec((4, POOL_GROUP, POOL_GROUP)),
                   _const_spec((8, d))),
        scratch_shapes=[pltpu.VMEM((POOL_HALO + tb, d), F32), pltpu.VMEM((tb + POOL_HALO, d), F32)],
        compiler_params=_params("arbitrary"),
    )(x, x, g, pw, pb, ps, dh1)


_CONV_CB = 1024


def _conv_pre(e, u, w, b, tb):
    acc = u * w[CONV_K - 1:CONV_K, :] + b
    for sh in range(1, CONV_K):
        acc = acc + pltpu.roll(e, sh, 0)[CONV_HALO:, :] * w[CONV_K - 1 - sh:CONV_K - sh, :]
    return acc


def _conv_fwd(u, w, b, *, tb=512):
    t_dim, c = u.shape
    cb = _CONV_CB

    def body(u_ref, w_ref, b_ref, o_ref, ext_ref):
        @pl.when(pl.program_id(1) == 0)
        def _():
            ext_ref[0:CONV_HALO, :] = jnp.zeros((CONV_HALO, cb), F32)

        uv = u_ref[...].astype(F32)
        ext_ref[CONV_HALO:, :] = uv
        v = _conv_pre(ext_ref[...], uv, w_ref[...], b_ref[...], tb)
        o_ref[...] = (v * _sigmoid(v)).astype(BF16)
        ext_ref[0:CONV_HALO, :] = uv[tb - CONV_HALO:, :]

    blk = pl.BlockSpec((tb, cb), lambda j, t: (t, j))
    return _pcall(
        body, name="conv_fwd", out_shape=jax.ShapeDtypeStruct((t_dim, c), BF16),
        grid=(c // cb, t_dim // tb),
        in_specs=[blk, pl.BlockSpec((CONV_K, cb), lambda j, t: (0, j)),
                  pl.BlockSpec((1, cb), lambda j, t: (0, j))],
        out_specs=blk,
        scratch_shapes=[pltpu.VMEM((CONV_HALO + tb, cb), F32)],
        compiler_params=_params("parallel", "arbitrary"),
    )(u, w, b)


def _conv_bwd_act(u, dxc, w, b, *, tb=512):
    t_dim, c = u.shape
    cb = _CONV_CB

    def body(u_ref, d_ref, w_ref, b_ref, dv_ref, dwb_ref, ext_ref):
        @pl.when(pl.program_id(1) == 0)
        def _():
            ext_ref[0:CONV_HALO, :] = jnp.zeros((CONV_HALO, cb), F32)
            dwb_ref[...] = jnp.zeros_like(dwb_ref)

        uv = u_ref[...].astype(F32)
        ext_ref[CONV_HALO:, :] = uv
        e = ext_ref[...]
        v = _conv_pre(e, uv, w_ref[...], b_ref[...], tb)
        sg = _sigmoid(v)
        dv = d_ref[...].astype(F32) * (sg * (1.0 + v * (1.0 - sg)))
        dv_ref[...] = dv.astype(BF16)
        dwb_ref[CONV_K:CONV_K + 1, :] += jnp.sum(dv, axis=0, keepdims=True)
        dwb_ref[CONV_K - 1:CONV_K, :] += jnp.sum(dv * uv, axis=0, keepdims=True)
        for sh in range(1, CONV_K):
            us = pltpu.roll(e, sh, 0)[CONV_HALO:, :]
            dwb_ref[CONV_K - 1 - sh:CONV_K - sh, :] += jnp.sum(dv * us, axis=0, keepdims=True)
        ext_ref[0:CONV_HALO, :] = uv[tb - CONV_HALO:, :]

    blk = pl.BlockSpec((tb, cb), lambda j, t: (t, j))
    return _pcall(
        body, name="conv_bwd_act",
        out_shape=(jax.ShapeDtypeStruct((t_dim, c), BF16), jax.ShapeDtypeStruct((8, c), F32)),
        grid=(c // cb, t_dim // tb),
        in_specs=[blk, blk, pl.BlockSpec((CONV_K, cb), lambda j, t: (0, j)),
                  pl.BlockSpec((1, cb), lambda j, t: (0, j))],
        out_specs=(blk, pl.BlockSpec((8, cb), lambda j, t: (0, j))),
        scratch_shapes=[pltpu.VMEM((CONV_HALO + tb, cb), F32)],
        compiler_params=_params("parallel", "arbitrary"),
    )(u, dxc, w, b)


def _conv_bwd_in(dv, w, *, tb=512):
    t_dim, c = dv.shape
    cb = _CONV_CB
    nb = t_dim // tb

    def body(dv_ref, w_ref, du_ref, ext_ref):
        @pl.when(pl.program_id(1) == 0)
        def _():
            ext_ref[tb:, :] = jnp.zeros((CONV_HALO, cb), F32)

        d = dv_ref[...].astype(F32)
        ext_ref[0:tb, :] = d
        e = ext_ref[...]
        wv = w_ref[...]
        acc = d * wv[CONV_K - 1:CONV_K, :]
        for sh in range(1, CONV_K):
            acc = acc + pltpu.roll(e, tb + CONV_HALO - sh, 0)[0:tb, :] * wv[CONV_K - 1 - sh:CONV_K - sh, :]
        du_ref[...] = acc.astype(BF16)
        ext_ref[tb:, :] = d[0:CONV_HALO, :]

    blk = pl.BlockSpec((tb, cb), lambda j, t: (nb - 1 - t, j))
    return _pcall(
        body, name="conv_bwd_in", out_shape=jax.ShapeDtypeStruct((t_dim, c), BF16),
        grid=(c // cb, nb),
        in_specs=[blk, pl.BlockSpec((CONV_K, cb), lambda j, t: (0, j))],
        out_specs=blk,
        scratch_shapes=[pltpu.VMEM((tb + CONV_HALO, cb), F32)],
        compiler_params=_params("parallel", "arbitrary"),
    )(dv, w)


def _softplus(v):
    e = jnp.exp(-jnp.abs(v))
    w = 1.0 + e
    log1p = jnp.where(w == 1.0, e, jnp.log(w) * e / jnp.where(w == 1.0, 1.0, w - 1.0))
    return jnp.maximum(v, 0.0) + log1p


def _cumsum_rows(v):
    row = lax.broadcasted_iota(jnp.int32, v.shape, 0)
    k = 1
    while k < CHUNK:
        v = v + jnp.where(row >= k, pltpu.roll(v, k, 0), 0.0)
        k *= 2
    return v


def _cumsum_lanes(v):
    col = lax.broadcasted_iota(jnp.int32, v.shape, 1)
    k = 1
    while k < CHUNK:
        v = v + jnp.where(col >= k, pltpu.roll(v, k, 1), 0.0)
        k *= 2
    return v


def _rev_cumsum_rows(v):
    row = lax.broadcasted_iota(jnp.int32, v.shape, 0)
    k = 1
    while k < CHUNK:
        v = v + jnp.where(row < CHUNK - k, pltpu.roll(v, CHUNK - k, 0), 0.0)
        k *= 2
    return v


def _ssd_decay_terms(dtr_ref, dtt_ref, bias_r, bias_c, alog_r, alog_c):
    dt_r = _softplus(dtr_ref[...] + bias_r)
    dt_c = _softplus(dtt_ref[...] + bias_c)
    a_r = -jnp.exp(alog_r)
    a_c = -jnp.exp(alog_c)
    acs_r = _cumsum_rows(dt_r * a_r)
    acs_c = _cumsum_lanes(dt_c * a_c)
    return dt_r, a_r, acs_r, acs_c


def _ssd_specs():
    chunk_rows = lambda w: pl.BlockSpec((CHUNK, w), lambda c: (c, 0))
    return chunk_rows


def _ssd_fwd(xc, dt_raw, dt_raw_t, bias_r, bias_c, alog_r, alog_c, dskip_r):
    t_dim = xc.shape[0]
    nc = t_dim // CHUNK

    def body(xc_ref, dtr_ref, dtt_ref, br_ref, bc_ref, ar_ref, ac_ref, dk_ref,
             y_ref, st_ref, state):
        @pl.when(pl.program_id(0) == 0)
        def _():
            state[...] = jnp.zeros_like(state)

        dt_r, _, acs_r, acs_c = _ssd_decay_terms(dtr_ref, dtt_ref, br_ref[...], bc_ref[...],
                                                 ar_ref[...], ac_ref[...])
        st_ref[0] = state[...]
        causal = (lax.broadcasted_iota(jnp.int32, (CHUNK, CHUNK), 0)
                  >= lax.broadcasted_iota(jnp.int32, (CHUNK, CHUNK), 1))
        dk = dk_ref[...]
        for g in range(N_GROUPS):
            bg = xc_ref[:, D_INNER + g * D_STATE:D_INNER + (g + 1) * D_STATE]
            cg = xc_ref[:, D_INNER + (N_GROUPS + g) * D_STATE:D_INNER + (N_GROUPS + g + 1) * D_STATE]
            cb = lax.dot_general(cg, bg, _DIMS["nt"], preferred_element_type=F32)
            for r_ in range(HEADS_PER_GROUP):
                h = g * HEADS_PER_GROUP + r_
                lanes = slice(h * HEAD_DIM, (h + 1) * HEAD_DIM)
                col = acs_r[:, h:h + 1]
                row = acs_c[h:h + 1, :]
                last = acs_r[CHUNK - 1:CHUNK, h:h + 1]
                lm = jnp.exp(jnp.where(causal, col - row, NEG_INF))
                xh = xc_ref[:, lanes].astype(F32)
                xdt = xh * dt_r[:, h:h + 1]
                ydiag = jnp.dot((cb * lm).astype(BF16), xdt.astype(BF16), preferred_element_type=F32)
                hprev = state[h]
                yoff = lax.dot_general(cg, hprev.astype(BF16), _DIMS["nt"],
                                       preferred_element_type=F32) * jnp.exp(col)
                y_ref[:, lanes] = ydiag + yoff + dk[:, h:h + 1] * xh
                s_new = lax.dot_general((xdt * jnp.exp(last - col)).astype(BF16), bg, _DIMS["tn"],
                                        preferred_element_type=F32)
                state[h] = hprev * jnp.exp(last) + s_new

    rows = _ssd_specs()
    return _pcall(
        body, name="ssd_fwd",
        out_shape=(jax.ShapeDtypeStruct((t_dim, D_INNER), F32),
                   jax.ShapeDtypeStruct((nc, N_HEADS, HEAD_DIM, D_STATE), F32)),
        grid=(nc,),
        in_specs=[rows(CONV_DIM), rows(128), pl.BlockSpec((N_HEADS, CHUNK), lambda c: (0, c)),
                  _const_spec((1, 128)), _const_spec((N_HEADS, 1)),
                  _const_spec((1, 128)), _const_spec((N_HEADS, 1)), _const_spec((1, 128))],
        out_specs=(rows(D_INNER),
                   pl.BlockSpec((1, N_HEADS, HEAD_DIM, D_STATE), lambda c: (c, 0, 0, 0))),
        scratch_shapes=[pltpu.VMEM((N_HEADS, HEAD_DIM, D_STATE), F32)],
        compiler_params=_params("arbitrary"),
    )(xc, dt_raw, dt_raw_t, bias_r, bias_c, alog_r, alog_c, dskip_r)


def _ssd_bwd(xc, dt_raw, dt_raw_t, bias_r, bias_c, alog_r, alog_c, dskip_r, states, dy):
    t_dim = xc.shape[0]
    nc = t_dim // CHUNK

    def body(xc_ref, dtr_ref, dtt_ref, br_ref, bc_ref, ar_ref, ac_ref, dk_ref, st_ref, dy_ref,
             dxc_ref, ddt_ref, small_ref, dstate):
        @pl.when(pl.program_id(0) == 0)
        def _():
            dstate[...] = jnp.zeros_like(dstate)
            small_ref[...] = jnp.zeros_like(small_ref)

        dt_r, a_r, acs_r, acs_c = _ssd_decay_terms(dtr_ref, dtt_ref, br_ref[...], bc_ref[...],
                                                   ar_ref[...], ac_ref[...])
        causal = (lax.broadcasted_iota(jnp.int32, (CHUNK, CHUNK), 0)
                  >= lax.broadcasted_iota(jnp.int32, (CHUNK, CHUNK), 1))
        lane_id = lax.broadcasted_iota(jnp.int32, (CHUNK, 128), 1)
        row_id = lax.broadcasted_iota(jnp.int32, (CHUNK, 128), 0)
        is_last_row = lax.broadcasted_iota(jnp.int32, (CHUNK, 1), 0) == CHUNK - 1
        dk = dk_ref[...]
        dacs_cols = jnp.zeros((CHUNK, 128), F32)
        dacs_rows = jnp.zeros((CHUNK, 128), F32)
        ddt_cols = jnp.zeros((CHUNK, 128), F32)
        dd_row = jnp.zeros((1, 128), F32)
        for g in range(N_GROUPS):
            b_lanes = slice(D_INNER + g * D_STATE, D_INNER + (g + 1) * D_STATE)
            c_lanes = slice(D_INNER + (N_GROUPS + g) * D_STATE, D_INNER + (N_GROUPS + g + 1) * D_STATE)
            bg = xc_ref[:, b_lanes]
            cg = xc_ref[:, c_lanes]
            cb = lax.dot_general(cg, bg, _DIMS["nt"], preferred_element_type=F32)
            dcb = jnp.zeros((CHUNK, CHUNK), F32)
            dc_acc = jnp.zeros((CHUNK, D_STATE), F32)
            db_acc = jnp.zeros((CHUNK, D_STATE), F32)
            for r_ in range(HEADS_PER_GROUP):
                h = g * HEADS_PER_GROUP + r_
                lanes = slice(h * HEAD_DIM, (h + 1) * HEAD_DIM)
                col = acs_r[:, h:h + 1]
                row = acs_c[h:h + 1, :]
                last = acs_r[CHUNK - 1:CHUNK, h:h + 1]
                lm = jnp.exp(jnp.where(causal, col - row, NEG_INF))
                m = cb * lm
                xh = xc_ref[:, lanes].astype(F32)
                dt_col = dt_r[:, h:h + 1]
                xdt = xh * dt_col
                xdt16 = xdt.astype(BF16)
                dyh = dy_ref[:, lanes]
                dy16 = dyh.astype(BF16)
                ea = jnp.exp(col)
                dte = jnp.exp(last - col)
                cd = jnp.exp(last)
                hprev = st_ref[0, h]
                hp16 = hprev.astype(BF16)
                dhn = dstate[h]
                dhn16 = dhn.astype(BF16)
                dye = dyh * ea
                dye16 = dye.astype(BF16)
                ch = lax.dot_general(cg, hp16, _DIMS["nt"], preferred_element_type=F32)
                dacs = jnp.sum(dye * ch, axis=1, keepdims=True)
                dc_acc = dc_acc + jnp.dot(dye16, hp16, preferred_element_type=F32)
                dhprev = dhn * cd + lax.dot_general(dye16, cg, _DIMS["tn"], preferred_element_type=F32)
                gmat = lax.dot_general(bg, dhn16, _DIMS["nt"], preferred_element_type=F32)
                dxdt = gmat * dte
                ddte = jnp.sum(gmat * xdt, axis=1, keepdims=True) * dte
                db_acc = db_acc + jnp.dot((xdt * dte).astype(BF16), dhn16, preferred_element_type=F32)
                dlast = jnp.sum(ddte, axis=0, keepdims=True) + cd * jnp.sum(
                    jnp.sum(dhn * hprev, axis=1, keepdims=True), axis=0, keepdims=True)
                dacs = dacs - ddte
                dxdt = dxdt + lax.dot_general(m.astype(BF16), dy16, _DIMS["tn"], preferred_element_type=F32)
                dm = lax.dot_general(dy16, xdt16, _DIMS["nt"], preferred_element_type=F32)
                dcb = dcb + dm * lm
                dseg = dm * m
                dacs = dacs + jnp.sum(dseg, axis=1, keepdims=True)
                dacs = dacs + jnp.where(is_last_row, dlast, 0.0)
                dacs_neg = jnp.sum(dseg, axis=0, keepdims=True)
                dxc_ref[:, lanes] = (dxdt * dt_col + dk[:, h:h + 1] * dyh).astype(BF16)
                ddt_h = jnp.sum(dxdt * xh, axis=1, keepdims=True)
                dd_h = jnp.sum(jnp.sum(dyh * xh, axis=1, keepdims=True), axis=0, keepdims=True)
                dacs_cols = jnp.where(lane_id == h, dacs, dacs_cols)
                ddt_cols = jnp.where(lane_id == h, ddt_h, ddt_cols)
                dacs_rows = jnp.where(row_id == h, dacs_neg, dacs_rows)
                dd_row = jnp.where(lane_id[0:1, :] == h, dd_h, dd_row)
                dstate[h] = dhprev
            dcb16 = dcb.astype(BF16)
            dxc_ref[:, c_lanes] = (dc_acc + jnp.dot(dcb16, bg, preferred_element_type=F32)).astype(BF16)
            dxc_ref[:, b_lanes] = (db_acc + lax.dot_general(dcb16, cg, _DIMS["tn"],
                                                           preferred_element_type=F32)).astype(BF16)
        dacs_all = dacs_cols - dacs_rows.T
        dadt = _rev_cumsum_rows(dacs_all)
        ddt = ddt_cols + dadt * a_r
        dalog = jnp.sum(dadt * dt_r, axis=0, keepdims=True) * a_r
        ddraw = ddt * _sigmoid(dtr_ref[...] + br_ref[...])
        ddraw = jnp.where(lane_id < N_HEADS, ddraw, 0.0)
        ddt_ref[...] = ddraw
        small_ref[0:1, :] += jnp.where(lane_id[0:1, :] < N_HEADS, dalog, 0.0)
        small_ref[1:2, :] += jnp.sum(ddraw, axis=0, keepdims=True)
        small_ref[2:3, :] += dd_row

    rev = lambda w: pl.BlockSpec((CHUNK, w), lambda c: (nc - 1 - c, 0))
    return _pcall(
        body, name="ssd_bwd",
        out_shape=(jax.ShapeDtypeStruct((t_dim, CONV_DIM), BF16),
                   jax.ShapeDtypeStruct((t_dim, 128), F32),
                   jax.ShapeDtypeStruct((8, 128), F32)),
        grid=(nc,),
        in_specs=[rev(CONV_DIM), rev(128), pl.BlockSpec((N_HEADS, CHUNK), lambda c: (0, nc - 1 - c)),
                  _const_spec((1, 128)), _const_spec((N_HEADS, 1)),
                  _const_spec((1, 128)), _const_spec((N_HEADS, 1)), _const_spec((1, 128)),
                  pl.BlockSpec((1, N_HEADS, HEAD_DIM, D_STATE), lambda c: (nc - 1 - c, 0, 0, 0)),
                  rev(D_INNER)],
        out_specs=(rev(CONV_DIM), rev(128), _const_spec((8, 128))),
        scratch_shapes=[pltpu.VMEM((N_HEADS, HEAD_DIM, D_STATE), F32)],
        compiler_params=_params("arbitrary"),
    )(xc, dt_raw, dt_raw_t, bias_r, bias_c, alog_r, alog_c, dskip_r, states, dy)


_GATE_GROUP = D_INNER // N_GROUPS


def _gate_fwd(y, z, g, *, tb=256):
    t_dim = y.shape[0]

    def body(y_ref, z_ref, g_ref, o_ref):
        for gi in range(N_GROUPS):
            lanes = slice(gi * _GATE_GROUP, (gi + 1) * _GATE_GROUP)
            zv = z_ref[:, lanes].astype(F32)
            wv = y_ref[:, lanes] * (zv * _sigmoid(zv))
            o_ref[:, lanes] = (wv * _rms(wv) * g_ref[:, lanes]).astype(BF16)

    return _pcall(
        body, name="gate_fwd", out_shape=jax.ShapeDtypeStruct((t_dim, D_INNER), BF16),
        grid=(t_dim // tb,),
        in_specs=[_row_spec(tb, D_INNER), _row_spec(tb, D_INNER), _const_spec((1, D_INNER))],
        out_specs=_row_spec(tb, D_INNER), compiler_params=_params("parallel"),
    )(y, z, g)


def _gate_bwd(dyn, y, z, g, *, tb=256):
    t_dim = y.shape[0]

    def body(d_ref, y_ref, z_ref, g_ref, dy_ref, dz_ref, dg_ref):
        @pl.when(pl.program_id(0) == 0)
        def _():
            dg_ref[...] = jnp.zeros_like(dg_ref)

        for gi in range(N_GROUPS):
            lanes = slice(gi * _GATE_GROUP, (gi + 1) * _GATE_GROUP)
            zv = z_ref[:, lanes].astype(F32)
            sg = _sigmoid(zv)
            sz = zv * sg
            yv = y_ref[:, lanes]
            wv = yv * sz
            r = _rms(wv)
            what = wv * r
            dv = d_ref[:, lanes]
            dwhat = dv * g_ref[:, lanes]
            dw = r * (dwhat - what * jnp.mean(dwhat * what, axis=-1, keepdims=True))
            dg_ref[0:1, lanes] += jnp.sum(dv * what, axis=0, keepdims=True)
            dy_ref[:, lanes] = dw * sz
            dz_ref[:, lanes] = (dw * yv * (sg * (1.0 + zv * (1.0 - sg)))).astype(BF16)

    return _pcall(
        body, name="gate_bwd",
        out_shape=(jax.ShapeDtypeStruct((t_dim, D_INNER), F32),
                   jax.ShapeDtypeStruct((t_dim, D_INNER), BF16),
                   jax.ShapeDtypeStruct((8, D_INNER), F32)),
        grid=(t_dim // tb,),
        in_specs=[_row_spec(tb, D_INNER), _row_spec(tb, D_INNER), _row_spec(tb, D_INNER),
                  _const_spec((1, D_INNER))],
        out_specs=(_row_spec(tb, D_INNER), _row_spec(tb, D_INNER), _const_spec((8, D_INNER))),
        compiler_params=_params("arbitrary"),
    )(dyn, y, z, g)


def _loss_head(h, g, target, *, tb=512):
    t_dim, d = h.shape

    def body(h_ref, g_ref, t_ref, dh_ref, small_ref):
        @pl.when(pl.program_id(0) == 0)
        def _():
            small_ref[...] = jnp.zeros_like(small_ref)

        x = h_ref[...]
        r = _rms(x)
        xhat = x * r
        gv = g_ref[...]
        err = xhat * gv - t_ref[...]
        small_ref[1:2, :] += (0.5 / d) * jnp.sum(err * err, axis=0, keepdims=True)
        dyv = err * (1.0 / d)
        dxhat = dyv * gv
        dh_ref[...] = r * (dxhat - xhat * jnp.mean(dxhat * xhat, axis=-1, keepdims=True))
        small_ref[0:1, :] += jnp.sum(dyv * xhat, axis=0, keepdims=True)

    return _pcall(
        body, name="loss_head",
        out_shape=(jax.ShapeDtypeStruct((t_dim, d), F32), jax.ShapeDtypeStruct((8, d), F32)),
        grid=(t_dim // tb,),
        in_specs=[_row_spec(tb, d), _const_spec((1, d)), _row_spec(tb, d)],
        out_specs=(_row_spec(tb, d), _const_spec((8, d))),
        compiler_params=_params("arbitrary"),
    )(h, g, target)


def _adamw(w, g, m, v, *, name):
    r_dim, c = w.shape
    tb = r_dim
    for cand in (512, 256, 128, 64, 32, 16, 8):
        if r_dim % cand == 0:
            tb = cand
            break
    c1 = 1.0 / (1.0 - ADAM_B1 ** ADAM_STEP)
    c2 = 1.0 / (1.0 - ADAM_B2 ** ADAM_STEP)

    def body(w_ref, g_ref, m_ref, v_ref, d_ref, mo_ref, vo_ref):
        gv = g_ref[...]
        mn = ADAM_B1 * m_ref[...] + (1.0 - ADAM_B1) * gv
        vn = ADAM_B2 * v_ref[...] + (1.0 - ADAM_B2) * (gv * gv)
        mo_ref[...] = mn
        vo_ref[...] = vn
        d_ref[...] = -ADAM_LR * ((mn * c1) / (jnp.sqrt(vn * c2) + ADAM_EPS) + ADAM_WD * w_ref[...])

    spec = _row_spec(tb, c)
    sds = jax.ShapeDtypeStruct((r_dim, c), F32)
    return _pcall(
        body, name=name, out_shape=(sds, sds, sds), grid=(r_dim // tb,),
        in_specs=[spec] * 4, out_specs=(spec,) * 3, compiler_params=_params("parallel"),
    )(w, g, m, v)


def _pair_sum(own, recv, *, name):
    s_dim, r_dim, c = own.shape
    tb = min(r_dim, 256)

    def body(a_ref, b_ref, o_ref, o16_ref):
        s = a_ref[...] + b_ref[...]
        o_ref[...] = s
        o16_ref[...] = s.astype(BF16)

    spec = pl.BlockSpec((1, tb, c), lambda s, i: (s, i, 0))
    return _pcall(
        body, name=name,
        out_shape=(jax.ShapeDtypeStruct(own.shape, F32), jax.ShapeDtypeStruct(own.shape, BF16)),
        grid=(s_dim, r_dim // tb), in_specs=[spec, spec], out_specs=(spec, spec),
        compiler_params=_params("parallel", "parallel"),
    )(own, recv)


def _chip_sum(own, recv, *, name):
    r_dim, c = own.shape
    tb = min(r_dim, 256)

    def body(a_ref, b_ref, o_ref):
        s = a_ref[...]
        for k in range(1, N_CHIPS):
            s = s + b_ref[k].astype(F32)
        o_ref[...] = s

    return _pcall(
        body, name=name, out_shape=jax.ShapeDtypeStruct((r_dim, c), F32),
        grid=(r_dim // tb,),
        in_specs=[_row_spec(tb, c), pl.BlockSpec((N_CHIPS, tb, c), lambda i: (0, i, 0))],
        out_specs=_row_spec(tb, c), compiler_params=_params("parallel"),
    )(own, recv)


def _position():
    return lax.axis_index("x"), lax.axis_index("y"), lax.axis_index("c")


def _chip_peer(x, y, k):
    return x ^ (k >> 1), y ^ (k & 1)


_ANY = pl.BlockSpec(memory_space=pl.ANY)


def _all_gather_weights(shards):
    n = len(shards)

    def body(*refs):
        srcs, outs = refs[:n], refs[n:2 * n]
        send_sems, recv_sems, local_sems = refs[2 * n:]
        x, y, c = _position()
        me = 2 * x + y
        copies = []
        for w in range(n):
            loc = pltpu.make_async_copy(srcs[w], outs[w].at[me], local_sems.at[w])
            loc.start()
            copies.append(loc)
        sends = []
        for w in range(n):
            for k in range(1, N_CHIPS):
                px, py = _chip_peer(x, y, k)
                cp = pltpu.make_async_remote_copy(
                    src_ref=srcs[w], dst_ref=outs[w].at[me],
                    send_sem=send_sems.at[w, k - 1], recv_sem=recv_sems.at[w, k - 1],
                    device_id=(px, py, c), device_id_type=MESH)
                cp.start()
                sends.append(cp)
        for w in range(n):
            for k in range(1, N_CHIPS):
                px, py = _chip_peer(x, y, k)
                pltpu.make_async_remote_copy(
                    src_ref=srcs[w], dst_ref=outs[w].at[2 * px + py],
                    send_sem=send_sems.at[w, k - 1], recv_sem=recv_sems.at[w, k - 1],
                    device_id=(px, py, c), device_id_type=MESH).wait_recv()
        for cp in sends:
            cp.wait_send()
        for cp in copies:
            cp.wait()

    return _pcall(
        body, name="gather_weights",
        out_shape=tuple(jax.ShapeDtypeStruct((N_CHIPS,) + s.shape, s.dtype) for s in shards),
        in_specs=[_ANY] * n, out_specs=(_ANY,) * n,
        scratch_shapes=[pltpu.SemaphoreType.DMA((n, N_CHIPS - 1)),
                        pltpu.SemaphoreType.DMA((n, N_CHIPS - 1)),
                        pltpu.SemaphoreType.DMA((n,))],
    )(*shards)


def _pair_exchange_halves(grads):
    n = len(grads)

    def body(*refs):
        srcs, outs = refs[:n], refs[n:2 * n]
        send_sems, recv_sems = refs[2 * n:]
        x, y, c = _position()
        sends = []
        for w in range(n):
            half = srcs[w].shape[1] // 2
            cp = pltpu.make_async_remote_copy(
                src_ref=srcs[w].at[:, pl.ds((1 - c) * half, half), :], dst_ref=outs[w],
                send_sem=send_sems.at[w], recv_sem=recv_sems.at[w],
                device_id=(x, y, 1 - c), device_id_type=MESH)
            cp.start()
            sends.append(cp)
        for cp in sends:
            cp.wait_recv()
        for cp in sends:
            cp.wait_send()

    return _pcall(
        body, name="pair_exchange_halves",
        out_shape=tuple(jax.ShapeDtypeStruct((g.shape[0], g.shape[1] // 2, g.shape[2]), F32) for g in grads),
        in_specs=[_ANY] * n, out_specs=(_ANY,) * n,
        scratch_shapes=[pltpu.SemaphoreType.DMA((n,)), pltpu.SemaphoreType.DMA((n,))],
    )(*grads)


def _chip_exchange(parts):
    n = len(parts)

    def body(*refs):
        srcs, outs = refs[:n], refs[n:2 * n]
        send_sems, recv_sems = refs[2 * n:]
        x, y, c = _position()
        sends = []
        for w in range(n):
            for k in range(1, N_CHIPS):
                px, py = _chip_peer(x, y, k)
                cp = pltpu.make_async_remote_copy(
                    src_ref=srcs[w].at[2 * px + py], dst_ref=outs[w].at[k],
                    send_sem=send_sems.at[w, k - 1], recv_sem=recv_sems.at[w, k - 1],
                    device_id=(px, py, c), device_id_type=MESH)
                cp.start()
                sends.append(cp)
        for cp in sends:
            cp.wait_recv()
        for cp in sends:
            cp.wait_send()

    return _pcall(
        body, name="chip_exchange",
        out_shape=tuple(jax.ShapeDtypeStruct(p.shape, BF16) for p in parts),
        in_specs=[_ANY] * n, out_specs=(_ANY,) * n,
        scratch_shapes=[pltpu.SemaphoreType.DMA((n, N_CHIPS - 1)),
                        pltpu.SemaphoreType.DMA((n, N_CHIPS - 1))],
    )(*parts)


def _pair_gather_halves(halves):
    n = len(halves)

    def body(*refs):
        srcs, outs = refs[:n], refs[n:2 * n]
        send_sems, recv_sems, local_sems = refs[2 * n:]
        x, y, c = _position()
        sends, copies = [], []
        for w in range(n):
            half = srcs[w].shape[0]
            loc = pltpu.make_async_copy(srcs[w], outs[w].at[pl.ds(c * half, half), :], local_sems.at[w])
            loc.start()
            copies.append(loc)
            cp = pltpu.make_async_remote_copy(
                src_ref=srcs[w], dst_ref=outs[w].at[pl.ds(c * half, half), :],
                send_sem=send_sems.at[w], recv_sem=recv_sems.at[w],
                device_id=(x, y, 1 - c), device_id_type=MESH)
            cp.start()
            sends.append(cp)
        for w in range(n):
            half = srcs[w].shape[0]
            pltpu.make_async_remote_copy(
                src_ref=srcs[w], dst_ref=outs[w].at[pl.ds((1 - c) * half, half), :],
                send_sem=send_sems.at[w], recv_sem=recv_sems.at[w],
                device_id=(x, y, 1 - c), device_id_type=MESH).wait_recv()
        for cp in sends:
            cp.wait_send()
        for cp in copies:
            cp.wait()

    return _pcall(
        body, name="pair_gather_halves",
        out_shape=tuple(jax.ShapeDtypeStruct((2 * h.shape[0], h.shape[1]), F32) for h in halves),
        in_specs=[_ANY] * n, out_specs=(_ANY,) * n,
        scratch_shapes=[pltpu.SemaphoreType.DMA((n,)), pltpu.SemaphoreType.DMA((n,)),
                        pltpu.SemaphoreType.DMA((n,))],
    )(*halves)


def _all_reduce_small(packed, *, name, sum_row0):
    r_dim, c = packed.shape

    def body(src_ref, out_ref, recv_ref, send_sems, recv_sems):
        x, y, c_ = _position()
        me = 4 * x + 2 * y + c_
        recv_ref[0] = src_ref[...]
        sends = []
        for k in range(1, N_DEV):
            peer = (x ^ (k >> 2), y ^ ((k >> 1) & 1), c_ ^ (k & 1))
            cp = pltpu.make_async_remote_copy(
                src_ref=src_ref, dst_ref=recv_ref.at[k],
                send_sem=send_sems.at[k - 1], recv_sem=recv_sems.at[k - 1],
                device_id=peer, device_id_type=MESH)
            cp.start()
            sends.append(cp)
        for cp in sends:
            cp.wait_recv()
        total = recv_ref[me]
        for d in range(1, N_DEV):
            total = total + recv_ref[d ^ me]
        if sum_row0:
            row0 = jnp.sum(total[0:1, :], axis=1, keepdims=True)
            rid = lax.broadcasted_iota(jnp.int32, total.shape, 0)
            total = jnp.where(rid == 0, row0, total)
        out_ref[...] = total
        for cp in sends:
            cp.wait_send()

    return _pcall(
        body, name=name, out_shape=jax.ShapeDtypeStruct((r_dim, c), F32),
        in_specs=[pl.BlockSpec(memory_space=pltpu.VMEM)],
        out_specs=pl.BlockSpec(memory_space=pltpu.VMEM),
        scratch_shapes=[pltpu.VMEM((N_DEV, r_dim, c), F32),
                        pltpu.SemaphoreType.DMA((N_DEV - 1,)), pltpu.SemaphoreType.DMA((N_DEV - 1,))],
    )(packed)


def _pad_lanes(v, width):
    return jnp.pad(v, ((0, 0), (0, width - v.shape[1])))


def _rows_1024(v):
    flat = v.reshape(-1)
    pad = (-flat.shape[0]) % D_MODEL
    return jnp.pad(flat, (0, pad)).reshape(-1, D_MODEL)


def _local_step(xs, target, pw, w_z, w_xbc, w_dt, wout, w1, w2, conv_w, conv_b, gate_g,
                norm_mix_g, norm_mlp_g, pool_b, pool_scale, ssm_dt_bias, ssm_a_log, ssm_d, final_g):
    bias_r = _pad_lanes(ssm_dt_bias, 128)
    alog_r = _pad_lanes(ssm_a_log, 128)
    dskip_r = _pad_lanes(ssm_d, 128)
    bias_c = ssm_dt_bias.reshape(N_HEADS, 1)
    alog_c = ssm_a_log.reshape(N_HEADS, 1)

    g_mix0, g_mix1 = norm_mix_g[0:1], norm_mix_g[1:2]
    g_mlp0, g_mlp1 = norm_mlp_g[0:1], norm_mlp_g[1:2]
    fg = final_g.reshape(1, D_MODEL)

    h1 = _pool_fwd(xs, g_mix0, pw, pool_b, pool_scale)
    hm0 = _rmsnorm_fwd(h1, g_mlp0, name="norm_mlp0")
    u0 = _matmul(hm0, w1[0], "nn", name="mlp0_up", out_dtype=BF16)
    h2 = _matmul(u0, w2[0], "nn", name="mlp0_down", a_relu2=True, add=h1)

    hn1 = _rmsnorm_fwd(h2, g_mix1, name="norm_mix1")
    z = _matmul(hn1, w_z, "nn", name="in_proj_z", out_dtype=BF16)
    xbc = _matmul(hn1, w_xbc, "nn", name="in_proj_xbc", out_dtype=BF16)
    dt_raw = _matmul(hn1, w_dt, "nn", name="in_proj_dt")
    dt_raw_t = dt_raw[:, :N_HEADS].T
    xc = _conv_fwd(xbc, conv_w, conv_b)
    y, states = _ssd_fwd(xc, dt_raw, dt_raw_t, bias_r, bias_c, alog_r, alog_c, dskip_r)
    yn = _gate_fwd(y, z, gate_g)
    h3 = _matmul(yn, wout, "nn", name="out_proj", add=h2)
    hm1 = _rmsnorm_fwd(h3, g_mlp1, name="norm_mlp1")
    u1 = _matmul(hm1, w1[1], "nn", name="mlp1_up", out_dtype=BF16)
    h4 = _matmul(u1, w2[1], "nn", name="mlp1_down", a_relu2=True, add=h3)

    dh4, small_final = _loss_head(h4, fg, target)

    def mlp_bwd(dh_out, h_in, hm, u, w1_i, w2_i, g_i, tag):
        du = _matmul(dh_out, w2_i, "nt", name=tag + "_du", out_dtype=BF16, relu2_grad_of=u)
        dw2 = _matmul(u, dh_out, "tn", name=tag + "_dw2", a_relu2=True)
        dhm = _matmul(du, w1_i, "nt", name=tag + "_dhm")
        dw1 = _matmul(hm, du, "tn", name=tag + "_dw1", out_col_shards=N_CHIPS)
        dh_in, dg = _rmsnorm_bwd(dhm, h_in, g_i, dh_out, name=tag + "_norm_bwd")
        return dh_in, dw1, dw2.reshape(N_CHIPS, D_FF // N_CHIPS, D_MODEL), dg

    dh3, dw1_1, dw2_1, dg_mlp1 = mlp_bwd(dh4, h3, hm1, u1, w1[1], w2[1], g_mlp1, "mlp1")

    dyn = _matmul(dh3, wout, "nt", name="out_proj_dyn")
    dwout = _matmul(yn, dh3, "tn", name="out_proj_dw").reshape(N_CHIPS, D_INNER // N_CHIPS, D_MODEL)
    dy, dz, dg_gate = _gate_bwd(dyn, y, z, gate_g)
    dxc, ddt_raw, small_ssd = _ssd_bwd(xc, dt_raw, dt_raw_t, bias_r, bias_c, alog_r, alog_c, dskip_r, states, dy)
    dv, dconv = _conv_bwd_act(xbc, dxc, conv_w, conv_b)
    dxbc = _conv_bwd_in(dv, conv_w)
    dhn1 = _matmul(ddt_raw, w_dt, "nt", name="in_proj_dt_dh")
    dhn1 = _matmul(dz, w_z, "nt", name="in_proj_z_dh", add=dhn1)
    dhn1 = _matmul(dxbc, w_xbc, "nt", name="in_proj_xbc_dh", add=dhn1)
    dw_z = _matmul(hn1, dz, "tn", name="in_proj_z_dw")
    dw_xbc = _matmul(hn1, dxbc, "tn", name="in_proj_xbc_dw")
    dw_dt = _matmul(hn1, ddt_raw, "tn", name="in_proj_dt_dw")
    dwin = jnp.concatenate([dw_z, dw_xbc, dw_dt[:, :N_HEADS]], axis=1)
    dwin = jnp.transpose(dwin.reshape(D_MODEL, N_CHIPS, IN_PROJ_DIM // N_CHIPS), (1, 0, 2))
    dh2, dg_mix1 = _rmsnorm_bwd(dhn1, h2, g_mix1, dh3, name="norm_mix1_bwd")

    dh1, dw1_0, dw2_0, dg_mlp0 = mlp_bwd(dh2, h1, hm0, u0, w1[0], w2[0], g_mlp0, "mlp0")
    dx, dpw, small_pool = _pool_bwd(xs, g_mix0, pw, pool_b, pool_scale, dh1)
    dpw = jnp.transpose(dpw.reshape(4, N_CHIPS, POOL_GROUP // N_CHIPS, POOL_GROUP), (1, 0, 2, 3))
    dpw = dpw.reshape(N_CHIPS, 4 * (POOL_GROUP // N_CHIPS), POOL_GROUP)

    big = [dpw, dwin, dwout, dw1_0, dw1_1, dw2_0, dw2_1]
    rows = [
        small_final[1:2],
        small_final[0:1],
        small_pool[0:1], dg_mix1[0:1],
        dg_mlp0[0:1], dg_mlp1[0:1],
        small_pool[1:2], small_pool[2:3],
        _pad_lanes(small_ssd[0:3], D_MODEL),
        _rows_1024(dg_gate[0:1]),
        _rows_1024(dconv[0:CONV_K]),
        _rows_1024(dconv[CONV_K:CONV_K + 1]),
    ]
    return dx, big, rows


def kernel(x, norm_mix_g, norm_mlp_g, pool_w, pool_b, pool_scale, ssm_w_in, ssm_conv_w, ssm_conv_b, ssm_dt_bias, ssm_a_log, ssm_d, ssm_norm_g, ssm_w_out, mlp_w1, mlp_w2, final_g, loss_target, m_norm_mix_g, m_norm_mlp_g, m_pool_w, m_pool_b, m_pool_scale, m_ssm_w_in, m_ssm_conv_w, m_ssm_conv_b, m_ssm_dt_bias, m_ssm_a_log, m_ssm_d, m_ssm_norm_g, m_ssm_w_out, m_mlp_w1, m_mlp_w2, m_final_g, v_norm_mix_g, v_norm_mlp_g, v_pool_w, v_pool_b, v_pool_scale, v_ssm_w_in, v_ssm_conv_w, v_ssm_conv_b, v_ssm_dt_bias, v_ssm_a_log, v_ssm_d, v_ssm_norm_g, v_ssm_w_out, v_mlp_w1, v_mlp_w2, v_final_g):
    t_dim = x.shape[1]
    xs = x[0]
    target = loss_target[0]
    my_x, my_y, my_c = _position()
    my_chip = 2 * my_x + my_y

    g_pool, g_win, g_wout, g_w1, g_w2 = _all_gather_weights([
        pool_w[0].astype(BF16),
        ssm_w_in[0].astype(BF16),
        ssm_w_out[0].astype(BF16),
        mlp_w1.astype(BF16),
        mlp_w2.astype(BF16),
    ])
    pw = jnp.transpose(g_pool, (1, 0, 2, 3)).reshape(4, POOL_GROUP, POOL_GROUP)
    win = jnp.transpose(g_win, (1, 0, 2)).reshape(D_MODEL, IN_PROJ_DIM)
    w_z, w_xbc = win[:, :D_INNER], win[:, D_INNER:D_INNER + CONV_DIM]
    w_dt = _pad_lanes(win[:, D_INNER + CONV_DIM:], 128)
    wout = g_wout.reshape(D_INNER, D_MODEL)
    w1 = [jnp.transpose(g_w1[:, i], (1, 0, 2)).reshape(D_MODEL, D_FF) for i in range(2)]
    w2 = [g_w2[:, i].reshape(D_FF, D_MODEL) for i in range(2)]
    conv_w = jnp.zeros((CONV_K, CONV_DIM), F32)
    conv_b = jnp.zeros((1, CONV_DIM), F32)
    gate_g = jnp.zeros((1, D_INNER), F32)
    conv_w = lax.dynamic_update_slice(conv_w, ssm_conv_w[0], (0, my_chip * (CONV_DIM // N_CHIPS)))
    conv_b = lax.dynamic_update_slice(conv_b, ssm_conv_b, (0, my_chip * (CONV_DIM // N_CHIPS)))
    gate_g = lax.dynamic_update_slice(gate_g, ssm_norm_g, (0, my_chip * (D_INNER // N_CHIPS)))
    vec_rows = jnp.concatenate([_rows_1024(conv_w), _rows_1024(conv_b), _rows_1024(gate_g)], axis=0)
    vec_rows = jnp.pad(vec_rows, ((0, (-vec_rows.shape[0]) % 8), (0, 0)))
    vec_rows = _all_reduce_small(vec_rows * 0.5, name="gather_vectors", sum_row0=False)
    conv_w = vec_rows[0:12].reshape(CONV_K, CONV_DIM)
    conv_b = vec_rows[12:15].reshape(1, CONV_DIM)
    gate_g = vec_rows[15:17].reshape(1, D_INNER)

    dx, big, rows = _local_step(xs, target, pw, w_z, w_xbc, w_dt, wout, w1, w2, conv_w, conv_b, gate_g,
                                norm_mix_g, norm_mlp_g, pool_b, pool_scale, ssm_dt_bias, ssm_a_log, ssm_d,
                                final_g)

    recv1 = _pair_exchange_halves(big)
    chip_f32, chip_b16 = [], []
    for i, (gfull, r1) in enumerate(zip(big, recv1)):
        half = gfull.shape[1] // 2
        mine = lax.dynamic_slice_in_dim(gfull, my_c * half, half, axis=1)
        s32, s16 = _pair_sum(mine, r1, name="pair_sum_%d" % i)
        chip_f32.append(lax.dynamic_index_in_dim(s32, my_chip, axis=0, keepdims=False))
        chip_b16.append(s16)
    recv2 = _chip_exchange(chip_b16)
    halves = [_chip_sum(o, r2, name="chip_sum_%d" % i) for i, (o, r2) in enumerate(zip(chip_f32, recv2))]
    g_pool_w, g_win_s, g_wout_s, g_w1_0, g_w1_1, g_w2_0, g_w2_1 = _pair_gather_halves(halves)

    small = jnp.concatenate(rows, axis=0)
    small = jnp.pad(small, ((0, (-small.shape[0]) % 8), (0, 0)))
    small = _all_reduce_small(small, name="all_reduce_small", sum_row0=True)
    loss = small[0, 0]
    g_final = small[1]
    g_norm_mix = small[2:4]
    g_norm_mlp = small[4:6]
    g_pool_b, g_pool_scale = small[6:7], small[7:8]
    g_alog, g_dtb, g_dsk = small[8:9, :N_HEADS], small[9:10, :N_HEADS], small[10:11, :N_HEADS]
    g_gate_full = small[11:13].reshape(1, D_INNER)
    g_convw_full = small[13:25].reshape(CONV_K, CONV_DIM)
    g_convb_full = small[25:28].reshape(1, CONV_DIM)
    g_gate = lax.dynamic_slice_in_dim(g_gate_full, my_chip * (D_INNER // N_CHIPS), D_INNER // N_CHIPS, axis=1)
    g_convw = lax.dynamic_slice_in_dim(g_convw_full, my_chip * (CONV_DIM // N_CHIPS), CONV_DIM // N_CHIPS, axis=1)
    g_convb = lax.dynamic_slice_in_dim(g_convb_full, my_chip * (CONV_DIM // N_CHIPS), CONV_DIM // N_CHIPS, axis=1)

    grads = {
        "norm_mix_g": g_norm_mix, "norm_mlp_g": g_norm_mlp,
        "pool_w": g_pool_w.reshape(pool_w.shape), "pool_b": g_pool_b, "pool_scale": g_pool_scale,
        "ssm_w_in": g_win_s.reshape(ssm_w_in.shape), "ssm_conv_w": g_convw.reshape(ssm_conv_w.shape),
        "ssm_conv_b": g_convb, "ssm_dt_bias": g_dtb, "ssm_a_log": g_alog, "ssm_d": g_dsk,
        "ssm_norm_g": g_gate, "ssm_w_out": g_wout_s.reshape(ssm_w_out.shape),
        "mlp_w1": jnp.stack([g_w1_0, g_w1_1]), "mlp_w2": jnp.stack([g_w2_0, g_w2_1]),
        "final_g": g_final,
    }
    weights = dict(norm_mix_g=norm_mix_g, norm_mlp_g=norm_mlp_g, pool_w=pool_w, pool_b=pool_b,
                   pool_scale=pool_scale, ssm_w_in=ssm_w_in, ssm_conv_w=ssm_conv_w, ssm_conv_b=ssm_conv_b,
                   ssm_dt_bias=ssm_dt_bias, ssm_a_log=ssm_a_log, ssm_d=ssm_d, ssm_norm_g=ssm_norm_g,
                   ssm_w_out=ssm_w_out, mlp_w1=mlp_w1, mlp_w2=mlp_w2, final_g=final_g)
    moms = dict(norm_mix_g=(m_norm_mix_g, v_norm_mix_g), norm_mlp_g=(m_norm_mlp_g, v_norm_mlp_g),
                pool_w=(m_pool_w, v_pool_w), pool_b=(m_pool_b, v_pool_b),
                pool_scale=(m_pool_scale, v_pool_scale), ssm_w_in=(m_ssm_w_in, v_ssm_w_in),
                ssm_conv_w=(m_ssm_conv_w, v_ssm_conv_w), ssm_conv_b=(m_ssm_conv_b, v_ssm_conv_b),
                ssm_dt_bias=(m_ssm_dt_bias, v_ssm_dt_bias), ssm_a_log=(m_ssm_a_log, v_ssm_a_log),
                ssm_d=(m_ssm_d, v_ssm_d), ssm_norm_g=(m_ssm_norm_g, v_ssm_norm_g),
                ssm_w_out=(m_ssm_w_out, v_ssm_w_out), mlp_w1=(m_mlp_w1, v_mlp_w1),
                mlp_w2=(m_mlp_w2, v_mlp_w2), final_g=(m_final_g, v_final_g))
    names = list(weights)
    big_names = ("pool_w", "ssm_w_in", "ssm_w_out", "mlp_w1", "mlp_w2")
    deltas, new_m, new_v = {}, {}, {}
    for nm in big_names:
        w = weights[nm]
        two_d = (-1, w.shape[-1])
        d_, m_, v_ = _adamw(w.reshape(two_d), grads[nm].reshape(two_d), moms[nm][0].reshape(two_d),
                            moms[nm][1].reshape(two_d), name="adamw_" + nm)
        deltas[nm], new_m[nm], new_v[nm] = d_.reshape(w.shape), m_.reshape(w.shape), v_.reshape(w.shape)
    small_names = [nm for nm in names if nm not in big_names]
    sizes = [weights[nm].size for nm in small_names]

    def pack(parts):
        flat = jnp.concatenate([p.reshape(-1) for p in parts])
        pad = (-flat.shape[0]) % (8 * D_MODEL)
        return jnp.pad(flat, (0, pad)).reshape(-1, D_MODEL)

    d_, m_, v_ = _adamw(pack([weights[nm] for nm in small_names]), pack([grads[nm] for nm in small_names]),
                        pack([moms[nm][0] for nm in small_names]), pack([moms[nm][1] for nm in small_names]),
                        name="adamw_small")
    off = 0
    for nm, sz in zip(small_names, sizes):
        shp = weights[nm].shape
        deltas[nm] = d_.reshape(-1)[off:off + sz].reshape(shp)
        new_m[nm] = m_.reshape(-1)[off:off + sz].reshape(shp)
        new_v[nm] = v_.reshape(-1)[off:off + sz].reshape(shp)
        off += sz

    grad_x = dx.reshape(x.shape)
    out_grads = [grads[nm].reshape(weights[nm].shape) for nm in names]
    return (loss, grad_x, *out_grads, *[deltas[nm] for nm in names],
            *[new_m[nm] for nm in names], *[new_v[nm] for nm in names])
```

```python
import functools

import jax
import jax.numpy as jnp
from jax import lax
from jax.experimental import pallas as pl
from jax.experimental.pallas import tpu as pltpu

F32 = jnp.float32
BF16 = jnp.bfloat16
MESH = pl.DeviceIdType.MESH

D_MODEL = 1024
RMS_EPS = 1e-5
POOL_WINDOWS = (2, 4, 8, 16)
POOL_GROUP = 256
POOL_HALO = 16
D_INNER = 2048
HEAD_DIM = 64
N_HEADS = 32
N_GROUPS = 4
HEADS_PER_GROUP = 8
D_STATE = 128
CONV_K = 4
CONV_HALO = 8
CHUNK = 128
CONV_DIM = 3072
IN_PROJ_DIM = 5152
D_FF = 4096
N_CHIPS = 4
N_DEV = 8

ADAM_LR = 0.001
ADAM_B1 = 0.9
ADAM_B2 = 0.999
ADAM_EPS = 1e-08
ADAM_WD = 0.01
ADAM_STEP = 10

VMEM_LIMIT = 56 * 1024 * 1024
NEG_INF = float("-inf")


def _pcall(body, **kw):
    return pl.pallas_call(body, **kw)


def _params(*sem):
    return pltpu.CompilerParams(dimension_semantics=sem, vmem_limit_bytes=VMEM_LIMIT)


def _sigmoid(v):
    return 1.0 / (1.0 + jnp.exp(-v))


def _row_spec(tb, d, nb=None, reverse=False):
    if reverse:
        return pl.BlockSpec((tb, d), lambda i: (nb - 1 - i, 0))
    return pl.BlockSpec((tb, d), lambda i: (i, 0))


def _const_spec(shape):
    return pl.BlockSpec(shape, lambda *_: tuple(0 for _ in shape))


_DIMS = {"nn": (((1,), (0,)), ((), ())),
         "nt": (((1,), (1,)), ((), ())),
         "tn": (((0,), (0,)), ((), ()))}


_MATMUL_VMEM_BUDGET = 40 * 1024 * 1024


def _matmul_tiles(m_dim, n_dim, k_dim, a_bytes, b_bytes, mn_bytes):
    tm, tn = min(m_dim, 1024), min(n_dim, 1024)
    while 2 * (tm * k_dim * a_bytes + tn * k_dim * b_bytes + tm * tn * mn_bytes) > _MATMUL_VMEM_BUDGET:
        if tm >= tn:
            tm //= 2
        else:
            tn //= 2
    return tm, tn


def _matmul(a, b, mode, *, name, out_dtype=F32, a_relu2=False, add=None, relu2_grad_of=None,
            out_col_shards=1):
    if mode == "tn":
        k_dim, m_dim = a.shape
    else:
        m_dim, k_dim = a.shape
    n_dim = b.shape[0] if mode == "nt" else b.shape[1]
    mn_bytes = jnp.dtype(out_dtype).itemsize
    if relu2_grad_of is not None:
        mn_bytes += relu2_grad_of.dtype.itemsize
    if add is not None:
        mn_bytes += add.dtype.itemsize
    tm, tn = _matmul_tiles(m_dim, n_dim, k_dim, a.dtype.itemsize, b.dtype.itemsize, mn_bytes)
    assert m_dim % tm == 0 and n_dim % tn == 0
    a_spec = (pl.BlockSpec((k_dim, tm), lambda i, j: (0, i)) if mode == "tn"
              else pl.BlockSpec((tm, k_dim), lambda i, j: (i, 0)))
    b_spec = (pl.BlockSpec((tn, k_dim), lambda i, j: (j, 0)) if mode == "nt"
              else pl.BlockSpec((k_dim, tn), lambda i, j: (0, j)))
    mn_spec = pl.BlockSpec((tm, tn), lambda i, j: (i, j))
    operands, in_specs = [a, b], [a_spec, b_spec]
    if relu2_grad_of is not None:
        operands.append(relu2_grad_of)
        in_specs.append(mn_spec)
    if add is not None:
        operands.append(add)
        in_specs.append(mn_spec)
    if out_col_shards == 1:
        out_shape = jax.ShapeDtypeStruct((m_dim, n_dim), out_dtype)
        out_spec = mn_spec
    else:
        n_shard = n_dim // out_col_shards
        assert n_shard % tn == 0
        per = n_shard // tn
        out_shape = jax.ShapeDtypeStruct((out_col_shards, m_dim, n_shard), out_dtype)
        out_spec = pl.BlockSpec((None, tm, tn), lambda i, j: (j // per, i, j % per))

    def body(*refs):
        a_ref, b_ref, o_ref = refs[0], refs[1], refs[-1]
        av = a_ref[...]
        if a_relu2:
            av = jnp.maximum(av, 0)
            av = av * av
        r = lax.dot_general(av.astype(BF16), b_ref[...].astype(BF16), _DIMS[mode],
                            preferred_element_type=F32)
        nxt = 2
        if relu2_grad_of is not None:
            r = r * (2.0 * jnp.maximum(refs[nxt][...].astype(F32), 0.0))
            nxt += 1
        if add is not None:
            r = r + refs[nxt][...]
        o_ref[...] = r.astype(out_dtype)

    return _pcall(
        body, name=name, out_shape=out_shape,
        grid=(m_dim // tm, n_dim // tn),
        in_specs=in_specs, out_specs=out_spec,
        compiler_params=_params("parallel", "parallel"),
    )(*operands)


def _rms(x):
    return lax.rsqrt(jnp.mean(x * x, axis=-1, keepdims=True) + RMS_EPS)


def _rmsnorm_fwd(h, g, *, name, tb=512):
    t_dim, d = h.shape

    def body(h_ref, g_ref, o_ref):
        x = h_ref[...]
        o_ref[...] = (x * _rms(x) * g_ref[...]).astype(BF16)

    return _pcall(
        body, name=name, out_shape=jax.ShapeDtypeStruct((t_dim, d), BF16),
        grid=(t_dim // tb,), in_specs=[_row_spec(tb, d), _const_spec((1, d))],
        out_specs=_row_spec(tb, d), compiler_params=_params("parallel"),
    )(h, g)


def _rmsnorm_bwd(dy, h, g, dres, *, name, tb=512):
    t_dim, d = h.shape

    def body(dy_ref, h_ref, g_ref, dres_ref, dh_ref, dg_ref):
        @pl.when(pl.program_id(0) == 0)
        def _():
            dg_ref[...] = jnp.zeros_like(dg_ref)

        x = h_ref[...]
        r = _rms(x)
        xhat = x * r
        dyv = dy_ref[...]
        dxhat = dyv * g_ref[...]
        dh_ref[...] = dres_ref[...] + r * (dxhat - xhat * jnp.mean(dxhat * xhat, axis=-1, keepdims=True))
        dg_ref[0:1, :] += jnp.sum(dyv * xhat, axis=0, keepdims=True)

    return _pcall(
        body, name=name,
        out_shape=(jax.ShapeDtypeStruct((t_dim, d), F32), jax.ShapeDtypeStruct((8, d), F32)),
        grid=(t_dim // tb,),
        in_specs=[_row_spec(tb, d), _row_spec(tb, d), _const_spec((1, d)), _row_spec(tb, d)],
        out_specs=(_row_spec(tb, d), _const_spec((8, d))),
        compiler_params=_params("arbitrary"),
    )(dy, h, g, dres)


def _pool_mixed(ext, hn, t0, tb):
    t = t0 + lax.broadcasted_iota(jnp.int32, (tb, 1), 0)
    parts = []
    for gi, w in enumerate(POOL_WINDOWS):
        lanes = slice(gi * POOL_GROUP, (gi + 1) * POOL_GROUP)
        s = ext[:, lanes]
        k = 1
        while k < w:
            s = s + pltpu.roll(s, k, 0)
            k *= 2
        cnt = jnp.minimum(t + 1, w).astype(F32)
        parts.append(s[POOL_HALO:, :] / cnt - hn[:, lanes])
    return parts


def _pool_fwd(x, g, pw, pb, ps, *, tb=512):
    t_dim, d = x.shape

    def body(x_ref, g_ref, pw_ref, pb_ref, ps_ref, o_ref, ext_ref):
        i = pl.program_id(0)

        @pl.when(i == 0)
        def _():
            ext_ref[0:POOL_HALO, :] = jnp.zeros((POOL_HALO, d), F32)

        xv = x_ref[...]
        hn = xv * _rms(xv) * g_ref[...]
        ext_ref[POOL_HALO:, :] = hn
        mixed = _pool_mixed(ext_ref[...], hn, i * tb, tb)
        for gi in range(len(POOL_WINDOWS)):
            lanes = slice(gi * POOL_GROUP, (gi + 1) * POOL_GROUP)
            out = jnp.dot(mixed[gi].astype(BF16), pw_ref[gi], preferred_element_type=F32)
            o_ref[:, lanes] = xv[:, lanes] + (out + pb_ref[:, lanes]) * ps_ref[:, lanes]
        ext_ref[0:POOL_HALO, :] = hn[tb - POOL_HALO:, :]

    return _pcall(
        body, name="pool_fwd", out_shape=jax.ShapeDtypeStruct((t_dim, d), F32),
        grid=(t_dim // tb,),
        in_specs=[_row_spec(tb, d), _const_spec((1, d)), _const_spec((4, POOL_GROUP, POOL_GROUP)),
                  _const_spec((1, d)), _const_spec((1, d))],
        out_specs=_row_spec(tb, d),
        scratch_shapes=[pltpu.VMEM((POOL_HALO + tb, d), F32)],
        compiler_params=_params("arbitrary"),
    )(x, g, pw, pb, ps)


def _pool_bwd(x, g, pw, pb, ps, dh1, *, tb=512):
    t_dim, d = x.shape
    nb = t_dim // tb
    halo_per_block = tb // POOL_HALO

    def body(x_ref, xprev_ref, g_ref, pw_ref, pb_ref, ps_ref, dh1_ref,
             dx_ref, dpw_ref, small_ref, ext_ref, dext_ref):
        i = pl.program_id(0)
        blk = nb - 1 - i

        @pl.when(i == 0)
        def _():
            dpw_ref[...] = jnp.zeros_like(dpw_ref)
            small_ref[...] = jnp.zeros_like(small_ref)
            dext_ref[tb:, :] = jnp.zeros((POOL_HALO, d), F32)

        gv = g_ref[...]
        xv = x_ref[...]
        r = _rms(xv)
        xhat = xv * r
        hn = xhat * gv
        xp = xprev_ref[...]
        hprev = xp * _rms(xp) * gv * (blk > 0).astype(F32)
        ext_ref[0:POOL_HALO, :] = hprev
        ext_ref[POOL_HALO:, :] = hn
        mixed = _pool_mixed(ext_ref[...], hn, blk * tb, tb)

        dout = dh1_ref[...]
        t = blk * tb + lax.broadcasted_iota(jnp.int32, (tb, 1), 0)
        for gi, w in enumerate(POOL_WINDOWS):
            lanes = slice(gi * POOL_GROUP, (gi + 1) * POOL_GROUP)
            mb = mixed[gi].astype(BF16)
            pre = jnp.dot(mb, pw_ref[gi], preferred_element_type=F32) + pb_ref[:, lanes]
            dg_out = dout[:, lanes]
            small_ref[2:3, lanes] += jnp.sum(dg_out * pre, axis=0, keepdims=True)
            dpre = dg_out * ps_ref[:, lanes]
            small_ref[1:2, lanes] += jnp.sum(dpre, axis=0, keepdims=True)
            dpb16 = dpre.astype(BF16)
            dpw_ref[gi] += lax.dot_general(mb, dpb16, _DIMS["tn"], preferred_element_type=F32)
            dmixed = lax.dot_general(dpb16, pw_ref[gi], _DIMS["nt"], preferred_element_type=F32)
            cnt = jnp.minimum(t + 1, w).astype(F32)
            dq = dmixed / cnt
            dext_ref[0:tb, lanes] = dq
            s = dext_ref[:, lanes]
            k = 1
            while k < w:
                s = s + pltpu.roll(s, tb + POOL_HALO - k, 0)
                k *= 2
            dhn = s[0:tb, :] - dmixed
            dext_ref[tb:, lanes] = dq[0:POOL_HALO, :]
            small_ref[0:1, lanes] += jnp.sum(dhn * xhat[:, lanes], axis=0, keepdims=True)
            ext_ref[POOL_HALO:, lanes] = dhn * gv[:, lanes]
        dxhat = ext_ref[POOL_HALO:, :]
        dx_ref[...] = dout + r * (dxhat - xhat * jnp.mean(dxhat * xhat, axis=-1, keepdims=True))

    return _pcall(
        body, name="pool_bwd",
        out_shape=(jax.ShapeDtypeStruct((t_dim, d), F32),
                   jax.ShapeDtypeStruct((4, POOL_GROUP, POOL_GROUP), F32),
                   jax.ShapeDtypeStruct((8, d), F32)),
        grid=(nb,),
        in_specs=[_row_spec(tb, d, nb, True),
                  pl.BlockSpec((POOL_HALO, d),
                               lambda i: (jnp.maximum((nb - 1 - i) * halo_per_block - 1, 0), 0)),
                  _const_spec((1, d)), _const_spec((4, POOL_GROUP, POOL_GROUP)),
                  _const_spec((1, d)), _const_spec((1, d)), _row_spec(tb, d, nb, True)],
        out_specs=(_row_spec(tb, d, nb, True), _const_spec((4, POOL_GROUP, POOL_GROUP)),
                   _const_spec((8, d))),
        scratch_shapes=[pltpu.VMEM((POOL_HALO + tb, d), F32), pltpu.VMEM((tb + POOL_HALO, d), F32)],
        compiler_params=_params("arbitrary"),
    )(x, x, g, pw, pb, ps, dh1)


_CONV_CB = 1024


def _conv_pre(e, u, w, b, tb):
    acc = u * w[CONV_K - 1:CONV_K, :] + b
    for sh in range(1, CONV_K):
        acc = acc + pltpu.roll(e, sh, 0)[CONV_HALO:, :] * w[CONV_K - 1 - sh:CONV_K - sh, :]
    return acc


def _conv_fwd(u, w, b, *, tb=512):
    t_dim, c = u.shape
    cb = _CONV_CB

    def body(u_ref, w_ref, b_ref, o_ref, ext_ref):
        @pl.when(pl.program_id(1) == 0)
        def _():
            ext_ref[0:CONV_HALO, :] = jnp.zeros((CONV_HALO, cb), F32)

        uv = u_ref[...].astype(F32)
        ext_ref[CONV_HALO:, :] = uv
        v = _conv_pre(ext_ref[...], uv, w_ref[...], b_ref[...], tb)
        o_ref[...] = (v * _sigmoid(v)).astype(BF16)
        ext_ref[0:CONV_HALO, :] = uv[tb - CONV_HALO:, :]

    blk = pl.BlockSpec((tb, cb), lambda j, t: (t, j))
    return _pcall(
        body, name="conv_fwd", out_shape=jax.ShapeDtypeStruct((t_dim, c), BF16),
        grid=(c // cb, t_dim // tb),
        in_specs=[blk, pl.BlockSpec((CONV_K, cb), lambda j, t: (0, j)),
                  pl.BlockSpec((1, cb), lambda j, t: (0, j))],
        out_specs=blk,
        scratch_shapes=[pltpu.VMEM((CONV_HALO + tb, cb), F32)],
        compiler_params=_params("parallel", "arbitrary"),
    )(u, w, b)


def _conv_bwd_act(u, dxc, w, b, *, tb=512):
    t_dim, c = u.shape
    cb = _CONV_CB

    def body(u_ref, d_ref, w_ref, b_ref, dv_ref, dwb_ref, ext_ref):
        @pl.when(pl.program_id(1) == 0)
        def _():
            ext_ref[0:CONV_HALO, :] = jnp.zeros((CONV_HALO, cb), F32)
            dwb_ref[...] = jnp.zeros_like(dwb_ref)

        uv = u_ref[...].astype(F32)
        ext_ref[CONV_HALO:, :] = uv
        e = ext_ref[...]
        v = _conv_pre(e, uv, w_ref[...], b_ref[...], tb)
        sg = _sigmoid(v)
        dv = d_ref[...].astype(F32) * (sg * (1.0 + v * (1.0 - sg)))
        dv_ref[...] = dv.astype(BF16)
        dwb_ref[CONV_K:CONV_K + 1, :] += jnp.sum(dv, axis=0, keepdims=True)
        dwb_ref[CONV_K - 1:CONV_K, :] += jnp.sum(dv * uv, axis=0, keepdims=True)
        for sh in range(1, CONV_K):
            us = pltpu.roll(e, sh, 0)[CONV_HALO:, :]
            dwb_ref[CONV_K - 1 - sh:CONV_K - sh, :] += jnp.sum(dv * us, axis=0, keepdims=True)
        ext_ref[0:CONV_HALO, :] = uv[tb - CONV_HALO:, :]

    blk = pl.BlockSpec((tb, cb), lambda j, t: (t, j))
    return _pcall(
        body, name="conv_bwd_act",
        out_shape=(jax.ShapeDtypeStruct((t_dim, c), BF16), jax.ShapeDtypeStruct((8, c), F32)),
        grid=(c // cb, t_dim // tb),
        in_specs=[blk, blk, pl.BlockSpec((CONV_K, cb), lambda j, t: (0, j)),
                  pl.BlockSpec((1, cb), lambda j, t: (0, j))],
        out_specs=(blk, pl.BlockSpec((8, cb), lambda j, t: (0, j))),
        scratch_shapes=[pltpu.VMEM((CONV_HALO + tb, cb), F32)],
        compiler_params=_params("parallel", "arbitrary"),
    )(u, dxc, w, b)


def _conv_bwd_in(dv, w, *, tb=512):
    t_dim, c = dv.shape
    cb = _CONV_CB
    nb = t_dim // tb

    def body(dv_ref, w_ref, du_ref, ext_ref):
        @pl.when(pl.program_id(1) == 0)
        def _():
            ext_ref[tb:, :] = jnp.zeros((CONV_HALO, cb), F32)

        d = dv_ref[...].astype(F32)
        ext_ref[0:tb, :] = d
        e = ext_ref[...]
        wv = w_ref[...]
        acc = d * wv[CONV_K - 1:CONV_K, :]
        for sh in range(1, CONV_K):
            acc = acc + pltpu.roll(e, tb + CONV_HALO - sh, 0)[0:tb, :] * wv[CONV_K - 1 - sh:CONV_K - sh, :]
        du_ref[...] = acc.astype(BF16)
        ext_ref[tb:, :] = d[0:CONV_HALO, :]

    blk = pl.BlockSpec((tb, cb), lambda j, t: (nb - 1 - t, j))
    return _pcall(
        body, name="conv_bwd_in", out_shape=jax.ShapeDtypeStruct((t_dim, c), BF16),
        grid=(c // cb, nb),
        in_specs=[blk, pl.BlockSpec((CONV_K, cb), lambda j, t: (0, j))],
        out_specs=blk,
        scratch_shapes=[pltpu.VMEM((tb + CONV_HALO, cb), F32)],
        compiler_params=_params("parallel", "arbitrary"),
    )(dv, w)


def _softplus(v):
    e = jnp.exp(-jnp.abs(v))
    w = 1.0 + e
    log1p = jnp.where(w == 1.0, e, jnp.log(w) * e / jnp.where(w == 1.0, 1.0, w - 1.0))
    return jnp.maximum(v, 0.0) + log1p


def _cumsum_rows(v):
    row = lax.broadcasted_iota(jnp.int32, v.shape, 0)
    k = 1
    while k < CHUNK:
        v = v + jnp.where(row >= k, pltpu.roll(v, k, 0), 0.0)
        k *= 2
    return v


def _cumsum_lanes(v):
    col = lax.broadcasted_iota(jnp.int32, v.shape, 1)
    k = 1
    while k < CHUNK:
        v = v + jnp.where(col >= k, pltpu.roll(v, k, 1), 0.0)
        k *= 2
    return v


def _rev_cumsum_rows(v):
    row = lax.broadcasted_iota(jnp.int32, v.shape, 0)
    k = 1
    while k < CHUNK:
        v = v + jnp.where(row < CHUNK - k, pltpu.roll(v, CHUNK - k, 0), 0.0)
        k *= 2
    return v


PAIR = 2 * HEAD_DIM
GROUP_LANES = HEADS_PER_GROUP * HEAD_DIM


def _head_lane_matrix():
    h = lax.broadcasted_iota(jnp.int32, (128, D_INNER), 0)
    j = lax.broadcasted_iota(jnp.int32, (128, D_INNER), 1)
    return (j // HEAD_DIM == h).astype(BF16)


def _split_bf16(v, pieces):
    out = []
    for _ in range(pieces):
        p = v.astype(BF16)
        out.append(p)
        v = v - p.astype(F32)
    return out


def _expand_heads(v, e):
    return sum(jnp.dot(p, e, preferred_element_type=F32) for p in _split_bf16(v, 3))


def _reduce_heads(v, et, pieces):
    return sum(jnp.dot(p, et, preferred_element_type=F32) for p in _split_bf16(v, pieces))


def _ssd_common(dtr_ref, dtt_ref, e_ref, bx, bc, ax, ac):
    dtx = _expand_heads(dtr_ref[...], e_ref[...])
    dt = _softplus(dtx + bx)
    a_x = -jnp.exp(ax)
    acs = _cumsum_rows(dt * a_x)
    acs_c = _cumsum_lanes(_softplus(dtt_ref[...] + bc) * (-jnp.exp(ac)))
    return dtx, dt, a_x, acs, acs_c


def _pair_decay(acs_slab, acs_c, h0, causal, left):
    other = pltpu.roll(acs_slab, HEAD_DIM, 1)
    col0 = jnp.where(left, acs_slab, other)
    col1 = jnp.where(left, other, acs_slab)
    l0 = jnp.exp(jnp.where(causal, col0 - acs_c[h0:h0 + 1, :], NEG_INF))
    l1 = jnp.exp(jnp.where(causal, col1 - acs_c[h0 + 1:h0 + 2, :], NEG_INF))
    return l0, l1


def _ssd_fwd(xc, dt_raw, dt_raw_t, bias_x, bias_c, alog_x, alog_c, dskip_x, e_mat):
    t_dim = xc.shape[0]
    nc = t_dim // CHUNK

    def body(xc_ref, dtr_ref, dtt_ref, bx_ref, bc_ref, ax_ref, ac_ref, dk_ref, e_ref,
             y_ref, st_ref, state):
        @pl.when(pl.program_id(0) == 0)
        def _():
            state[...] = jnp.zeros_like(state)

        _, dt, _, acs, acs_c = _ssd_common(dtr_ref, dtt_ref, e_ref, bx_ref[...], bc_ref[...],
                                           ax_ref[...], ac_ref[...])
        st_ref[0] = state[...]
        last = acs[CHUNK - 1:CHUNK, :]
        xs32 = xc_ref[:, 0:D_INNER].astype(F32)
        xdt = xs32 * dt
        xdt16 = xdt.astype(BF16)
        xdte16 = (xdt * jnp.exp(last - acs)).astype(BF16)
        ea = jnp.exp(acs)
        cd = jnp.exp(last)
        skip = dk_ref[...] * xs32
        causal = (lax.broadcasted_iota(jnp.int32, (CHUNK, CHUNK), 0)
                  >= lax.broadcasted_iota(jnp.int32, (CHUNK, CHUNK), 1))
        left = lax.broadcasted_iota(jnp.int32, (CHUNK, PAIR), 1) < HEAD_DIM
        for g in range(N_GROUPS):
            gl = slice(g * GROUP_LANES, (g + 1) * GROUP_LANES)
            bg = xc_ref[:, D_INNER + g * D_STATE:D_INNER + (g + 1) * D_STATE]
            cg = xc_ref[:, D_INNER + (N_GROUPS + g) * D_STATE:D_INNER + (N_GROUPS + g + 1) * D_STATE]
            cb = lax.dot_general(cg, bg, _DIMS["nt"], preferred_element_type=F32)
            hprev = state[:, gl]
            ch = jnp.dot(cg, hprev.astype(BF16), preferred_element_type=F32)
            for j in range(HEADS_PER_GROUP // 2):
                pl_ = slice(g * GROUP_LANES + j * PAIR, g * GROUP_LANES + (j + 1) * PAIR)
                h0 = g * HEADS_PER_GROUP + 2 * j
                l0, l1 = _pair_decay(acs[:, pl_], acs_c, h0, causal, left)
                lhs = jnp.concatenate([(cb * l0).astype(BF16), (cb * l1).astype(BF16)], axis=1)
                xp = xdt16[:, pl_]
                zero = jnp.zeros_like(xp)
                rhs = jnp.concatenate([jnp.where(left, xp, zero), jnp.where(left, zero, xp)], axis=0)
                ydiag = jnp.dot(lhs, rhs, preferred_element_type=F32)
                y_ref[:, pl_] = ydiag + ch[:, j * PAIR:(j + 1) * PAIR] * ea[:, pl_] + skip[:, pl_]
            s_new = lax.dot_general(bg, xdte16[:, gl], _DIMS["tn"], preferred_element_type=F32)
            state[:, gl] = hprev * cd[:, gl] + s_new

    rows = lambda w: pl.BlockSpec((CHUNK, w), lambda c: (c, 0))
    return _pcall(
        body, name="ssd_fwd",
        out_shape=(jax.ShapeDtypeStruct((t_dim, D_INNER), F32),
                   jax.ShapeDtypeStruct((nc, D_STATE, D_INNER), F32)),
        grid=(nc,),
        in_specs=[rows(CONV_DIM), rows(128), pl.BlockSpec((N_HEADS, CHUNK), lambda c: (0, c)),
                  _const_spec((1, D_INNER)), _const_spec((N_HEADS, 1)),
                  _const_spec((1, D_INNER)), _const_spec((N_HEADS, 1)), _const_spec((1, D_INNER)),
                  _const_spec((128, D_INNER))],
        out_specs=(rows(D_INNER), pl.BlockSpec((1, D_STATE, D_INNER), lambda c: (c, 0, 0))),
        scratch_shapes=[pltpu.VMEM((D_STATE, D_INNER), F32)],
        compiler_params=_params("arbitrary"),
    )(xc, dt_raw, dt_raw_t, bias_x, bias_c, alog_x, alog_c, dskip_x, e_mat)


def _ssd_bwd(xc, dt_raw, dt_raw_t, bias_x, bias_c, alog_x, alog_c, dskip_x, bias_r, alog_r,
             e_mat, et_mat, states, dy):
    t_dim = xc.shape[0]
    nc = t_dim // CHUNK

    def body(xc_ref, dtr_ref, dtt_ref, bx_ref, bc_ref, ax_ref, ac_ref, dk_ref, br_ref, ar_ref,
             e_ref, et_ref, st_ref, dy_ref,
             dxc_ref, ddt_ref, small_ref, dstate, dacs_ref, dxdt_ref, acc_x, acc_r):
        step = pl.program_id(0)

        @pl.when(step == 0)
        def _():
            dstate[...] = jnp.zeros_like(dstate)
            acc_x[...] = jnp.zeros_like(acc_x)
            acc_r[...] = jnp.zeros_like(acc_r)

        dtx, dt, a_x, acs, acs_c = _ssd_common(dtr_ref, dtt_ref, e_ref, bx_ref[...], bc_ref[...],
                                               ax_ref[...], ac_ref[...])
        last = acs[CHUNK - 1:CHUNK, :]
        xs32 = xc_ref[:, 0:D_INNER].astype(F32)
        xdt = xs32 * dt
        xdt16 = xdt.astype(BF16)
        dte = jnp.exp(last - acs)
        xdte = xdt * dte
        xdte16 = xdte.astype(BF16)
        cd = jnp.exp(last)
        dyv = dy_ref[...]
        dy16 = dyv.astype(BF16)
        dye = dyv * jnp.exp(acs)
        dye16 = dye.astype(BF16)
        causal = (lax.broadcasted_iota(jnp.int32, (CHUNK, CHUNK), 0)
                  >= lax.broadcasted_iota(jnp.int32, (CHUNK, CHUNK), 1))
        left = lax.broadcasted_iota(jnp.int32, (CHUNK, PAIR), 1) < HEAD_DIM
        lane_id = lax.broadcasted_iota(jnp.int32, (CHUNK, 128), 1)
        row_id = lax.broadcasted_iota(jnp.int32, (CHUNK, 128), 0)
        is_last_row = lax.broadcasted_iota(jnp.int32, (CHUNK, 1), 0) == CHUNK - 1
        dacs_cols = jnp.zeros((CHUNK, 128), F32)
        dacs_rows = jnp.zeros((CHUNK, 128), F32)
        for g in range(N_GROUPS):
            gl = slice(g * GROUP_LANES, (g + 1) * GROUP_LANES)
            b_lanes = slice(D_INNER + g * D_STATE, D_INNER + (g + 1) * D_STATE)
            c_lanes = slice(D_INNER + (N_GROUPS + g) * D_STATE, D_INNER + (N_GROUPS + g + 1) * D_STATE)
            bg = xc_ref[:, b_lanes]
            cg = xc_ref[:, c_lanes]
            cb = lax.dot_general(cg, bg, _DIMS["nt"], preferred_element_type=F32)
            hprev = st_ref[0, :, gl]
            hp16 = hprev.astype(BF16)
            dhn = dstate[:, gl]
            dhn16 = dhn.astype(BF16)
            ch = jnp.dot(cg, hp16, preferred_element_type=F32)
            gmat = jnp.dot(bg, dhn16, preferred_element_type=F32)
            gx = gmat * xdte[:, gl]
            dlast = jnp.sum(gx, axis=0, keepdims=True) + cd[:, gl] * jnp.sum(dhn * hprev, axis=0, keepdims=True)
            dacs_ref[:, gl] = dye[:, gl] * ch - gx + jnp.where(is_last_row, dlast, 0.0)
            dc_acc = lax.dot_general(dye16[:, gl], hp16, _DIMS["nt"], preferred_element_type=F32)
            db_acc = lax.dot_general(xdte16[:, gl], dhn16, _DIMS["nt"], preferred_element_type=F32)
            dstate[:, gl] = dhn * cd[:, gl] + lax.dot_general(cg, dye16[:, gl], _DIMS["tn"],
                                                             preferred_element_type=F32)
            dcb = jnp.zeros((CHUNK, CHUNK), F32)
            for j in range(HEADS_PER_GROUP // 2):
                pl_ = slice(g * GROUP_LANES + j * PAIR, g * GROUP_LANES + (j + 1) * PAIR)
                h0 = g * HEADS_PER_GROUP + 2 * j
                l0, l1 = _pair_decay(acs[:, pl_], acs_c, h0, causal, left)
                m0, m1 = cb * l0, cb * l1
                lhs = jnp.concatenate([m0.astype(BF16), m1.astype(BF16)], axis=1)
                dyp = dy16[:, pl_]
                zero = jnp.zeros_like(dyp)
                both = lax.dot_general(lhs, dyp, _DIMS["tn"], preferred_element_type=F32)
                dxdt_ref[:, pl_] = (jnp.where(left, both[0:CHUNK, :], both[CHUNK:, :])
                                    + gmat[:, j * PAIR:(j + 1) * PAIR] * dte[:, pl_])
                lhs2 = jnp.concatenate([jnp.where(left, dyp, zero), jnp.where(left, zero, dyp)], axis=0)
                dm = lax.dot_general(lhs2, xdt16[:, pl_], _DIMS["nt"], preferred_element_type=F32)
                dm0, dm1 = dm[0:CHUNK, :], dm[CHUNK:, :]
                dcb = dcb + dm0 * l0 + dm1 * l1
                ds0, ds1 = dm0 * m0, dm1 * m1
                dacs_cols = jnp.where(lane_id == h0, jnp.sum(ds0, axis=1, keepdims=True), dacs_cols)
                dacs_cols = jnp.where(lane_id == h0 + 1, jnp.sum(ds1, axis=1, keepdims=True), dacs_cols)
                dacs_rows = jnp.where(row_id == h0, jnp.sum(ds0, axis=0, keepdims=True), dacs_rows)
                dacs_rows = jnp.where(row_id == h0 + 1, jnp.sum(ds1, axis=0, keepdims=True), dacs_rows)
            dcb16 = dcb.astype(BF16)
            dxc_ref[:, c_lanes] = (dc_acc + jnp.dot(dcb16, bg, preferred_element_type=F32)).astype(BF16)
            dxc_ref[:, b_lanes] = (db_acc + lax.dot_general(dcb16, cg, _DIMS["tn"],
                                                           preferred_element_type=F32)).astype(BF16)
        dxdt = dxdt_ref[...]
        dxc_ref[:, 0:D_INNER] = (dxdt * dt + dk_ref[...] * dyv).astype(BF16)
        dadt = _rev_cumsum_rows(dacs_ref[...])
        ddraw_x = (dxdt * xs32 + dadt * a_x) * _sigmoid(dtx + bx_ref[...])
        acc_x[0:1, :] += jnp.sum(dadt * dt, axis=0, keepdims=True) * a_x
        acc_x[1:2, :] += jnp.sum(ddraw_x, axis=0, keepdims=True)
        acc_x[2:3, :] += jnp.sum(dyv * xs32, axis=0, keepdims=True)
        a_r = -jnp.exp(ar_ref[...])
        pre_r = dtr_ref[...] + br_ref[...]
        dadt_r = _rev_cumsum_rows(dacs_cols - dacs_rows.T)
        ddraw_r = jnp.where(lane_id < N_HEADS, dadt_r * a_r * _sigmoid(pre_r), 0.0)
        acc_r[0:1, :] += jnp.where(lane_id[0:1, :] < N_HEADS,
                                   jnp.sum(dadt_r * _softplus(pre_r), axis=0, keepdims=True) * a_r, 0.0)
        acc_r[1:2, :] += jnp.sum(ddraw_r, axis=0, keepdims=True)
        ddt_ref[...] = ddraw_r + _reduce_heads(ddraw_x, et_ref[...], 2)

        @pl.when(step == nc - 1)
        def _():
            small_ref[...] = acc_r[...] + _reduce_heads(acc_x[...], et_ref[...], 3)

    rev = lambda w: pl.BlockSpec((CHUNK, w), lambda c: (nc - 1 - c, 0))
    return _pcall(
        body, name="ssd_bwd",
        out_shape=(jax.ShapeDtypeStruct((t_dim, CONV_DIM), BF16),
                   jax.ShapeDtypeStruct((t_dim, 128), F32),
                   jax.ShapeDtypeStruct((8, 128), F32)),
        grid=(nc,),
        in_specs=[rev(CONV_DIM), rev(128), pl.BlockSpec((N_HEADS, CHUNK), lambda c: (0, nc - 1 - c)),
                  _const_spec((1, D_INNER)), _const_spec((N_HEADS, 1)),
                  _const_spec((1, D_INNER)), _const_spec((N_HEADS, 1)), _const_spec((1, D_INNER)),
                  _const_spec((1, 128)), _const_spec((1, 128)),
                  _const_spec((128, D_INNER)), _const_spec((D_INNER, 128)),
                  pl.BlockSpec((1, D_STATE, D_INNER), lambda c: (nc - 1 - c, 0, 0)),
                  rev(D_INNER)],
        out_specs=(rev(CONV_DIM), rev(128), _const_spec((8, 128))),
        scratch_shapes=[pltpu.VMEM((D_STATE, D_INNER), F32), pltpu.VMEM((CHUNK, D_INNER), F32),
                        pltpu.VMEM((CHUNK, D_INNER), F32), pltpu.VMEM((8, D_INNER), F32),
                        pltpu.VMEM((8, 128), F32)],
        compiler_params=_params("arbitrary"),
    )(xc, dt_raw, dt_raw_t, bias_x, bias_c, alog_x, alog_c, dskip_x, bias_r, alog_r,
      e_mat, et_mat, states, dy)


_GATE_GROUP = D_INNER // N_GROUPS


def _gate_fwd(y, z, g, *, tb=256):
    t_dim = y.shape[0]

    def body(y_ref, z_ref, g_ref, o_ref):
        for gi in range(N_GROUPS):
            lanes = slice(gi * _GATE_GROUP, (gi + 1) * _GATE_GROUP)
            zv = z_ref[:, lanes].astype(F32)
            wv = y_ref[:, lanes] * (zv * _sigmoid(zv))
            o_ref[:, lanes] = (wv * _rms(wv) * g_ref[:, lanes]).astype(BF16)

    return _pcall(
        body, name="gate_fwd", out_shape=jax.ShapeDtypeStruct((t_dim, D_INNER), BF16),
        grid=(t_dim // tb,),
        in_specs=[_row_spec(tb, D_INNER), _row_spec(tb, D_INNER), _const_spec((1, D_INNER))],
        out_specs=_row_spec(tb, D_INNER), compiler_params=_params("parallel"),
    )(y, z, g)


def _gate_bwd(dyn, y, z, g, *, tb=256):
    t_dim = y.shape[0]

    def body(d_ref, y_ref, z_ref, g_ref, dy_ref, dz_ref, dg_ref):
        @pl.when(pl.program_id(0) == 0)
        def _():
            dg_ref[...] = jnp.zeros_like(dg_ref)

        for gi in range(N_GROUPS):
            lanes = slice(gi * _GATE_GROUP, (gi + 1) * _GATE_GROUP)
            zv = z_ref[:, lanes].astype(F32)
            sg = _sigmoid(zv)
            sz = zv * sg
            yv = y_ref[:, lanes]
            wv = yv * sz
            r = _rms(wv)
            what = wv * r
            dv = d_ref[:, lanes]
            dwhat = dv * g_ref[:, lanes]
            dw = r * (dwhat - what * jnp.mean(dwhat * what, axis=-1, keepdims=True))
            dg_ref[0:1, lanes] += jnp.sum(dv * what, axis=0, keepdims=True)
            dy_ref[:, lanes] = dw * sz
            dz_ref[:, lanes] = (dw * yv * (sg * (1.0 + zv * (1.0 - sg)))).astype(BF16)

    return _pcall(
        body, name="gate_bwd",
        out_shape=(jax.ShapeDtypeStruct((t_dim, D_INNER), F32),
                   jax.ShapeDtypeStruct((t_dim, D_INNER), BF16),
                   jax.ShapeDtypeStruct((8, D_INNER), F32)),
        grid=(t_dim // tb,),
        in_specs=[_row_spec(tb, D_INNER), _row_spec(tb, D_INNER), _row_spec(tb, D_INNER),
                  _const_spec((1, D_INNER))],
        out_specs=(_row_spec(tb, D_INNER), _row_spec(tb, D_INNER), _const_spec((8, D_INNER))),
        compiler_params=_params("arbitrary"),
    )(dyn, y, z, g)


def _loss_head(h, g, target, *, tb=512):
    t_dim, d = h.shape

    def body(h_ref, g_ref, t_ref, dh_ref, small_ref):
        @pl.when(pl.program_id(0) == 0)
        def _():
            small_ref[...] = jnp.zeros_like(small_ref)

        x = h_ref[...]
        r = _rms(x)
        xhat = x * r
        gv = g_ref[...]
        err = xhat * gv - t_ref[...]
        small_ref[1:2, :] += (0.5 / d) * jnp.sum(err * err, axis=0, keepdims=True)
        dyv = err * (1.0 / d)
        dxhat = dyv * gv
        dh_ref[...] = r * (dxhat - xhat * jnp.mean(dxhat * xhat, axis=-1, keepdims=True))
        small_ref[0:1, :] += jnp.sum(dyv * xhat, axis=0, keepdims=True)

    return _pcall(
        body, name="loss_head",
        out_shape=(jax.ShapeDtypeStruct((t_dim, d), F32), jax.ShapeDtypeStruct((8, d), F32)),
        grid=(t_dim // tb,),
        in_specs=[_row_spec(tb, d), _const_spec((1, d)), _row_spec(tb, d)],
        out_specs=(_row_spec(tb, d), _const_spec((8, d))),
        compiler_params=_params("arbitrary"),
    )(h, g, target)


def _adamw(w, g, m, v, *, name):
    r_dim, c = w.shape
    tb = r_dim
    for cand in (512, 256, 128, 64, 32, 16, 8):
        if r_dim % cand == 0:
            tb = cand
            break
    c1 = 1.0 / (1.0 - ADAM_B1 ** ADAM_STEP)
    c2 = 1.0 / (1.0 - ADAM_B2 ** ADAM_STEP)

    def body(w_ref, g_ref, m_ref, v_ref, d_ref, mo_ref, vo_ref):
        gv = g_ref[...]
        mn = ADAM_B1 * m_ref[...] + (1.0 - ADAM_B1) * gv
        vn = ADAM_B2 * v_ref[...] + (1.0 - ADAM_B2) * (gv * gv)
        mo_ref[...] = mn
        vo_ref[...] = vn
        d_ref[...] = -ADAM_LR * ((mn * c1) / (jnp.sqrt(vn * c2) + ADAM_EPS) + ADAM_WD * w_ref[...])

    spec = _row_spec(tb, c)
    sds = jax.ShapeDtypeStruct((r_dim, c), F32)
    return _pcall(
        body, name=name, out_shape=(sds, sds, sds), grid=(r_dim // tb,),
        in_specs=[spec] * 4, out_specs=(spec,) * 3, compiler_params=_params("parallel"),
    )(w, g, m, v)


def _pair_sum(own, recv, *, name):
    s_dim, r_dim, c = own.shape
    tb = min(r_dim, 256)

    def body(a_ref, b_ref, o_ref, o16_ref):
        s = a_ref[...] + b_ref[...]
        o_ref[...] = s
        o16_ref[...] = s.astype(BF16)

    spec = pl.BlockSpec((1, tb, c), lambda s, i: (s, i, 0))
    return _pcall(
        body, name=name,
        out_shape=(jax.ShapeDtypeStruct(own.shape, F32), jax.ShapeDtypeStruct(own.shape, BF16)),
        grid=(s_dim, r_dim // tb), in_specs=[spec, spec], out_specs=(spec, spec),
        compiler_params=_params("parallel", "parallel"),
    )(own, recv)


def _chip_sum(own, recv, *, name):
    r_dim, c = own.shape
    tb = min(r_dim, 256)

    def body(a_ref, b_ref, o_ref):
        s = a_ref[...]
        for k in range(1, N_CHIPS):
            s = s + b_ref[k].astype(F32)
        o_ref[...] = s

    return _pcall(
        body, name=name, out_shape=jax.ShapeDtypeStruct((r_dim, c), F32),
        grid=(r_dim // tb,),
        in_specs=[_row_spec(tb, c), pl.BlockSpec((N_CHIPS, tb, c), lambda i: (0, i, 0))],
        out_specs=_row_spec(tb, c), compiler_params=_params("parallel"),
    )(own, recv)


def _position():
    return lax.axis_index("x"), lax.axis_index("y"), lax.axis_index("c")


def _chip_peer(x, y, k):
    return x ^ (k >> 1), y ^ (k & 1)


_ANY = pl.BlockSpec(memory_space=pl.ANY)


def _all_gather_weights(shards):
    n = len(shards)
    hops = N_CHIPS - 1

    def body(*refs):
        srcs, outs = refs[:n], refs[n:2 * n]
        send_sems, recv_sems, local_sems = refs[2 * n:]
        x, y, c = _position()
        me = 2 * x + y
        copies = []
        for w in range(n):
            loc = pltpu.make_async_copy(srcs[w], outs[w].at[me], local_sems.at[w])
            loc.start()
            copies.append(loc)

        def over_ici(w, k, chip, to):
            return pltpu.make_async_remote_copy(
                src_ref=srcs[w].at[c], dst_ref=outs[w].at[chip, c],
                send_sem=send_sems.at[w, k - 1], recv_sem=recv_sems.at[w, k - 1],
                device_id=to, device_id_type=MESH)

        def over_d2d(w, k, chip, half):
            return pltpu.make_async_remote_copy(
                src_ref=outs[w].at[chip, half], dst_ref=outs[w].at[chip, half],
                send_sem=send_sems.at[w, hops + k - 1], recv_sem=recv_sems.at[w, hops + k - 1],
                device_id=(x, y, 1 - c), device_id_type=MESH)

        sends = []
        for w in range(n):
            for k in range(1, N_CHIPS):
                px, py = _chip_peer(x, y, k)
                cp = over_ici(w, k, me, (px, py, c))
                cp.start()
                sends.append(cp)
        for w in range(n):
            for k in range(1, N_CHIPS):
                px, py = _chip_peer(x, y, k)
                over_ici(w, k, 2 * px + py, (px, py, c)).wait_recv()
                cp = over_d2d(w, k, 2 * px + py, c)
                cp.start()
                sends.append(cp)
        for w in range(n):
            for k in range(1, N_CHIPS):
                px, py = _chip_peer(x, y, k)
                over_d2d(w, k, 2 * px + py, 1 - c).wait_recv()
        for cp in sends:
            cp.wait_send()
        for cp in copies:
            cp.wait()

    return _pcall(
        body, name="gather_weights",
        out_shape=tuple(jax.ShapeDtypeStruct((N_CHIPS,) + s.shape, s.dtype) for s in shards),
        in_specs=[_ANY] * n, out_specs=(_ANY,) * n,
        scratch_shapes=[pltpu.SemaphoreType.DMA((n, 2 * hops)),
                        pltpu.SemaphoreType.DMA((n, 2 * hops)),
                        pltpu.SemaphoreType.DMA((n,))],
    )(*shards)


def _pair_exchange_halves(grads):
    n = len(grads)

    def body(*refs):
        srcs, outs = refs[:n], refs[n:2 * n]
        send_sems, recv_sems = refs[2 * n:]
        x, y, c = _position()
        sends = []
        for w in range(n):
            half = srcs[w].shape[1] // 2
            cp = pltpu.make_async_remote_copy(
                src_ref=srcs[w].at[:, pl.ds((1 - c) * half, half), :], dst_ref=outs[w],
                send_sem=send_sems.at[w], recv_sem=recv_sems.at[w],
                device_id=(x, y, 1 - c), device_id_type=MESH)
            cp.start()
            sends.append(cp)
        for cp in sends:
            cp.wait_recv()
        for cp in sends:
            cp.wait_send()

    return _pcall(
        body, name="pair_exchange_halves",
        out_shape=tuple(jax.ShapeDtypeStruct((g.shape[0], g.shape[1] // 2, g.shape[2]), F32) for g in grads),
        in_specs=[_ANY] * n, out_specs=(_ANY,) * n,
        scratch_shapes=[pltpu.SemaphoreType.DMA((n,)), pltpu.SemaphoreType.DMA((n,))],
    )(*grads)


def _chip_exchange(parts):
    n = len(parts)

    def body(*refs):
        srcs, outs = refs[:n], refs[n:2 * n]
        send_sems, recv_sems = refs[2 * n:]
        x, y, c = _position()
        sends = []
        for w in range(n):
            for k in range(1, N_CHIPS):
                px, py = _chip_peer(x, y, k)
                cp = pltpu.make_async_remote_copy(
                    src_ref=srcs[w].at[2 * px + py], dst_ref=outs[w].at[k],
                    send_sem=send_sems.at[w, k - 1], recv_sem=recv_sems.at[w, k - 1],
                    device_id=(px, py, c), device_id_type=MESH)
                cp.start()
                sends.append(cp)
        for cp in sends:
            cp.wait_recv()
        for cp in sends:
            cp.wait_send()

    return _pcall(
        body, name="chip_exchange",
        out_shape=tuple(jax.ShapeDtypeStruct(p.shape, BF16) for p in parts),
        in_specs=[_ANY] * n, out_specs=(_ANY,) * n,
        scratch_shapes=[pltpu.SemaphoreType.DMA((n, N_CHIPS - 1)),
                        pltpu.SemaphoreType.DMA((n, N_CHIPS - 1))],
    )(*parts)


def _pair_gather_halves(halves):
    n = len(halves)

    def body(*refs):
        srcs, outs = refs[:n], refs[n:2 * n]
        send_sems, recv_sems, local_sems = refs[2 * n:]
        x, y, c = _position()
        sends, copies = [], []
        for w in range(n):
            loc = pltpu.make_async_copy(srcs[w], outs[w].at[c], local_sems.at[w])
            loc.start()
            copies.append(loc)
            cp = pltpu.make_async_remote_copy(
                src_ref=srcs[w], dst_ref=outs[w].at[c],
                send_sem=send_sems.at[w], recv_sem=recv_sems.at[w],
                device_id=(x, y, 1 - c), device_id_type=MESH)
            cp.start()
            sends.append(cp)
        for w in range(n):
            pltpu.make_async_remote_copy(
                src_ref=srcs[w], dst_ref=outs[w].at[1 - c],
                send_sem=send_sems.at[w], recv_sem=recv_sems.at[w],
                device_id=(x, y, 1 - c), device_id_type=MESH).wait_recv()
        for cp in sends:
            cp.wait_send()
        for cp in copies:
            cp.wait()

    whole = _pcall(
        body, name="pair_gather_halves",
        out_shape=tuple(jax.ShapeDtypeStruct((2,) + h.shape, F32) for h in halves),
        in_specs=[_ANY] * n, out_specs=(_ANY,) * n,
        scratch_shapes=[pltpu.SemaphoreType.DMA((n,)), pltpu.SemaphoreType.DMA((n,)),
                        pltpu.SemaphoreType.DMA((n,))],
    )(*halves)
    return [w.reshape(2 * w.shape[1], w.shape[2]) for w in whole]


def _all_reduce_small(packed, *, name, sum_row0):
    r_dim, c = packed.shape

    def body(src_ref, out_ref, recv_ref, send_sems, recv_sems):
        x, y, c_ = _position()
        me = 4 * x + 2 * y + c_
        recv_ref[0] = src_ref[...]
        sends = []
        for k in range(1, N_DEV):
            peer = (x ^ (k >> 2), y ^ ((k >> 1) & 1), c_ ^ (k & 1))
            cp = pltpu.make_async_remote_copy(
                src_ref=src_ref, dst_ref=recv_ref.at[k],
                send_sem=send_sems.at[k - 1], recv_sem=recv_sems.at[k - 1],
                device_id=peer, device_id_type=MESH)
            cp.start()
            sends.append(cp)
        for cp in sends:
            cp.wait_recv()
        total = recv_ref[me]
        for d in range(1, N_DEV):
            total = total + recv_ref[d ^ me]
        if sum_row0:
            row0 = jnp.sum(total[0:1, :], axis=1, keepdims=True)
            rid = lax.broadcasted_iota(jnp.int32, total.shape, 0)
            total = jnp.where(rid == 0, row0, total)
        out_ref[...] = total
        for cp in sends:
            cp.wait_send()

    return _pcall(
        body, name=name, out_shape=jax.ShapeDtypeStruct((r_dim, c), F32),
        in_specs=[pl.BlockSpec(memory_space=pltpu.VMEM)],
        out_specs=pl.BlockSpec(memory_space=pltpu.VMEM),
        scratch_shapes=[pltpu.VMEM((N_DEV, r_dim, c), F32),
                        pltpu.SemaphoreType.DMA((N_DEV - 1,)), pltpu.SemaphoreType.DMA((N_DEV - 1,))],
    )(packed)


def _pad_lanes(v, width):
    return jnp.pad(v, ((0, 0), (0, width - v.shape[1])))


def _rows_1024(v):
    flat = v.reshape(-1)
    pad = (-flat.shape[0]) % D_MODEL
    return jnp.pad(flat, (0, pad)).reshape(-1, D_MODEL)


def _local_step(xs, target, pw, w_z, w_xbc, w_dt, wout, w1, w2, conv_w, conv_b, gate_g,
                norm_mix_g, norm_mlp_g, pool_b, pool_scale, ssm_dt_bias, ssm_a_log, ssm_d, final_g):
    bias_r = _pad_lanes(ssm_dt_bias, 128)
    alog_r = _pad_lanes(ssm_a_log, 128)
    bias_x = jnp.repeat(ssm_dt_bias, HEAD_DIM, axis=1)
    alog_x = jnp.repeat(ssm_a_log, HEAD_DIM, axis=1)
    dskip_x = jnp.repeat(ssm_d, HEAD_DIM, axis=1)
    bias_c = ssm_dt_bias.reshape(N_HEADS, 1)
    alog_c = ssm_a_log.reshape(N_HEADS, 1)
    e_mat = _head_lane_matrix()

    g_mix0, g_mix1 = norm_mix_g[0:1], norm_mix_g[1:2]
    g_mlp0, g_mlp1 = norm_mlp_g[0:1], norm_mlp_g[1:2]
    fg = final_g.reshape(1, D_MODEL)

    h1 = _pool_fwd(xs, g_mix0, pw, pool_b, pool_scale)
    hm0 = _rmsnorm_fwd(h1, g_mlp0, name="norm_mlp0")
    u0 = _matmul(hm0, w1[0], "nn", name="mlp0_up", out_dtype=BF16)
    h2 = _matmul(u0, w2[0], "nn", name="mlp0_down", a_relu2=True, add=h1)

    hn1 = _rmsnorm_fwd(h2, g_mix1, name="norm_mix1")
    z = _matmul(hn1, w_z, "nn", name="in_proj_z", out_dtype=BF16)
    xbc = _matmul(hn1, w_xbc, "nn", name="in_proj_xbc", out_dtype=BF16)
    dt_raw = _matmul(hn1, w_dt, "nn", name="in_proj_dt")
    dt_raw_t = dt_raw[:, :N_HEADS].T
    xc = _conv_fwd(xbc, conv_w, conv_b)
    y, states = _ssd_fwd(xc, dt_raw, dt_raw_t, bias_x, bias_c, alog_x, alog_c, dskip_x, e_mat)
    yn = _gate_fwd(y, z, gate_g)
    h3 = _matmul(yn, wout, "nn", name="out_proj", add=h2)
    hm1 = _rmsnorm_fwd(h3, g_mlp1, name="norm_mlp1")
    u1 = _matmul(hm1, w1[1], "nn", name="mlp1_up", out_dtype=BF16)
    h4 = _matmul(u1, w2[1], "nn", name="mlp1_down", a_relu2=True, add=h3)

    dh4, small_final = _loss_head(h4, fg, target)

    def mlp_bwd(dh_out, h_in, hm, u, w1_i, w2_i, g_i, tag):
        du = _matmul(dh_out, w2_i, "nt", name=tag + "_du", out_dtype=BF16, relu2_grad_of=u)
        dw2 = _matmul(u, dh_out, "tn", name=tag + "_dw2", a_relu2=True)
        dhm = _matmul(du, w1_i, "nt", name=tag + "_dhm")
        dw1 = _matmul(hm, du, "tn", name=tag + "_dw1", out_col_shards=N_CHIPS)
        dh_in, dg = _rmsnorm_bwd(dhm, h_in, g_i, dh_out, name=tag + "_norm_bwd")
        return dh_in, dw1, dw2.reshape(N_CHIPS, D_FF // N_CHIPS, D_MODEL), dg

    dh3, dw1_1, dw2_1, dg_mlp1 = mlp_bwd(dh4, h3, hm1, u1, w1[1], w2[1], g_mlp1, "mlp1")

    dyn = _matmul(dh3, wout, "nt", name="out_proj_dyn")
    dwout = _matmul(yn, dh3, "tn", name="out_proj_dw").reshape(N_CHIPS, D_INNER // N_CHIPS, D_MODEL)
    dy, dz, dg_gate = _gate_bwd(dyn, y, z, gate_g)
    dxc, ddt_raw, small_ssd = _ssd_bwd(xc, dt_raw, dt_raw_t, bias_x, bias_c, alog_x, alog_c, dskip_x,
                                       bias_r, alog_r, e_mat, e_mat.T, states, dy)
    dv, dconv = _conv_bwd_act(xbc, dxc, conv_w, conv_b)
    dxbc = _conv_bwd_in(dv, conv_w)
    dhn1 = _matmul(ddt_raw, w_dt, "nt", name="in_proj_dt_dh")
    dhn1 = _matmul(dz, w_z, "nt", name="in_proj_z_dh", add=dhn1)
    dhn1 = _matmul(dxbc, w_xbc, "nt", name="in_proj_xbc_dh", add=dhn1)
    dw_z = _matmul(hn1, dz, "tn", name="in_proj_z_dw")
    dw_xbc = _matmul(hn1, dxbc, "tn", name="in_proj_xbc_dw")
    dw_dt = _matmul(hn1, ddt_raw, "tn", name="in_proj_dt_dw")
    dwin = jnp.concatenate([dw_z, dw_xbc, dw_dt[:, :N_HEADS]], axis=1)
    dwin = jnp.transpose(dwin.reshape(D_MODEL, N_CHIPS, IN_PROJ_DIM // N_CHIPS), (1, 0, 2))
    dh2, dg_mix1 = _rmsnorm_bwd(dhn1, h2, g_mix1, dh3, name="norm_mix1_bwd")

    dh1, dw1_0, dw2_0, dg_mlp0 = mlp_bwd(dh2, h1, hm0, u0, w1[0], w2[0], g_mlp0, "mlp0")
    dx, dpw, small_pool = _pool_bwd(xs, g_mix0, pw, pool_b, pool_scale, dh1)
    dpw = jnp.transpose(dpw.reshape(4, N_CHIPS, POOL_GROUP // N_CHIPS, POOL_GROUP), (1, 0, 2, 3))
    dpw = dpw.reshape(N_CHIPS, 4 * (POOL_GROUP // N_CHIPS), POOL_GROUP)

    big = [dpw, dwin, dwout, dw1_0, dw1_1, dw2_0, dw2_1]
    rows = [
        small_final[1:2],
        small_final[0:1],
        small_pool[0:1], dg_mix1[0:1],
        dg_mlp0[0:1], dg_mlp1[0:1],
        small_pool[1:2], small_pool[2:3],
        _pad_lanes(small_ssd[0:3], D_MODEL),
        _rows_1024(dg_gate[0:1]),
        _rows_1024(dconv[0:CONV_K]),
        _rows_1024(dconv[CONV_K:CONV_K + 1]),
    ]
    return dx, big, rows


def kernel(x, norm_mix_g, norm_mlp_g, pool_w, pool_b, pool_scale, ssm_w_in, ssm_conv_w, ssm_conv_b, ssm_dt_bias, ssm_a_log, ssm_d, ssm_norm_g, ssm_w_out, mlp_w1, mlp_w2, final_g, loss_target, m_norm_mix_g, m_norm_mlp_g, m_pool_w, m_pool_b, m_pool_scale, m_ssm_w_in, m_ssm_conv_w, m_ssm_conv_b, m_ssm_dt_bias, m_ssm_a_log, m_ssm_d, m_ssm_norm_g, m_ssm_w_out, m_mlp_w1, m_mlp_w2, m_final_g, v_norm_mix_g, v_norm_mlp_g, v_pool_w, v_pool_b, v_pool_scale, v_ssm_w_in, v_ssm_conv_w, v_ssm_conv_b, v_ssm_dt_bias, v_ssm_a_log, v_ssm_d, v_ssm_norm_g, v_ssm_w_out, v_mlp_w1, v_mlp_w2, v_final_g):
    t_dim = x.shape[1]
    xs = x[0]
    target = loss_target[0]
    my_x, my_y, my_c = _position()
    my_chip = 2 * my_x + my_y

    g_pool, g_win, g_wout, g_w1, g_w2 = _all_gather_weights([
        pool_w[0].astype(BF16).reshape(2, 2, POOL_GROUP // N_CHIPS, POOL_GROUP),
        ssm_w_in[0].astype(BF16).reshape(2, D_MODEL // 2, IN_PROJ_DIM // N_CHIPS),
        ssm_w_out[0].astype(BF16).reshape(2, D_INNER // N_CHIPS // 2, D_MODEL),
        mlp_w1.astype(BF16),
        mlp_w2.astype(BF16),
    ])
    g_pool = g_pool.reshape(N_CHIPS, 4, POOL_GROUP // N_CHIPS, POOL_GROUP)
    g_win = g_win.reshape(N_CHIPS, D_MODEL, IN_PROJ_DIM // N_CHIPS)
    g_wout = g_wout.reshape(N_CHIPS, D_INNER // N_CHIPS, D_MODEL)
    pw = jnp.transpose(g_pool, (1, 0, 2, 3)).reshape(4, POOL_GROUP, POOL_GROUP)
    win = jnp.transpose(g_win, (1, 0, 2)).reshape(D_MODEL, IN_PROJ_DIM)
    w_z, w_xbc = win[:, :D_INNER], win[:, D_INNER:D_INNER + CONV_DIM]
    w_dt = _pad_lanes(win[:, D_INNER + CONV_DIM:], 128)
    wout = g_wout.reshape(D_INNER, D_MODEL)
    w1 = [jnp.transpose(g_w1[:, i], (1, 0, 2)).reshape(D_MODEL, D_FF) for i in range(2)]
    w2 = [g_w2[:, i].reshape(D_FF, D_MODEL) for i in range(2)]
    conv_w = jnp.zeros((CONV_K, CONV_DIM), F32)
    conv_b = jnp.zeros((1, CONV_DIM), F32)
    gate_g = jnp.zeros((1, D_INNER), F32)
    conv_w = lax.dynamic_update_slice(conv_w, ssm_conv_w[0], (0, my_chip * (CONV_DIM // N_CHIPS)))
    conv_b = lax.dynamic_update_slice(conv_b, ssm_conv_b, (0, my_chip * (CONV_DIM // N_CHIPS)))
    gate_g = lax.dynamic_update_slice(gate_g, ssm_norm_g, (0, my_chip * (D_INNER // N_CHIPS)))
    vec_rows = jnp.concatenate([_rows_1024(conv_w), _rows_1024(conv_b), _rows_1024(gate_g)], axis=0)
    vec_rows = jnp.pad(vec_rows, ((0, (-vec_rows.shape[0]) % 8), (0, 0)))
    vec_rows = _all_reduce_small(vec_rows * 0.5, name="gather_vectors", sum_row0=False)
    conv_w = vec_rows[0:12].reshape(CONV_K, CONV_DIM)
    conv_b = vec_rows[12:15].reshape(1, CONV_DIM)
    gate_g = vec_rows[15:17].reshape(1, D_INNER)

    dx, big, rows = _local_step(xs, target, pw, w_z, w_xbc, w_dt, wout, w1, w2, conv_w, conv_b, gate_g,
                                norm_mix_g, norm_mlp_g, pool_b, pool_scale, ssm_dt_bias, ssm_a_log, ssm_d,
                                final_g)

    recv1 = _pair_exchange_halves(big)
    chip_f32, chip_b16 = [], []
    for i, (gfull, r1) in enumerate(zip(big, recv1)):
        half = gfull.shape[1] // 2
        mine = lax.dynamic_slice_in_dim(gfull, my_c * half, half, axis=1)
        s32, s16 = _pair_sum(mine, r1, name="pair_sum_%d" % i)
        chip_f32.append(lax.dynamic_index_in_dim(s32, my_chip, axis=0, keepdims=False))
        chip_b16.append(s16)
    recv2 = _chip_exchange(chip_b16)
    halves = [_chip_sum(o, r2, name="chip_sum_%d" % i) for i, (o, r2) in enumerate(zip(chip_f32, recv2))]
    g_pool_w, g_win_s, g_wout_s, g_w1_0, g_w1_1, g_w2_0, g_w2_1 = _pair_gather_halves(halves)

    small = jnp.concatenate(rows, axis=0)
    small = jnp.pad(small, ((0, (-small.shape[0]) % 8), (0, 0)))
    small = _all_reduce_small(small, name="all_reduce_small", sum_row0=True)
    loss = small[0, 0]
    g_final = small[1]
    g_norm_mix = small[2:4]
    g_norm_mlp = small[4:6]
    g_pool_b, g_pool_scale = small[6:7], small[7:8]
    g_alog, g_dtb, g_dsk = small[8:9, :N_HEADS], small[9:10, :N_HEADS], small[10:11, :N_HEADS]
    g_gate_full = small[11:13].reshape(1, D_INNER)
    g_convw_full = small[13:25].reshape(CONV_K, CONV_DIM)
    g_convb_full = small[25:28].reshape(1, CONV_DIM)
    g_gate = lax.dynamic_slice_in_dim(g_gate_full, my_chip * (D_INNER // N_CHIPS), D_INNER // N_CHIPS, axis=1)
    g_convw = lax.dynamic_slice_in_dim(g_convw_full, my_chip * (CONV_DIM // N_CHIPS), CONV_DIM // N_CHIPS, axis=1)
    g_convb = lax.dynamic_slice_in_dim(g_convb_full, my_chip * (CONV_DIM // N_CHIPS), CONV_DIM // N_CHIPS, axis=1)

    grads = {
        "norm_mix_g": g_norm_mix, "norm_mlp_g": g_norm_mlp,
        "pool_w": g_pool_w.reshape(pool_w.shape), "pool_b": g_pool_b, "pool_scale": g_pool_scale,
        "ssm_w_in": g_win_s.reshape(ssm_w_in.shape), "ssm_conv_w": g_convw.reshape(ssm_conv_w.shape),
        "ssm_conv_b": g_convb, "ssm_dt_bias": g_dtb, "ssm_a_log": g_alog, "ssm_d": g_dsk,
        "ssm_norm_g": g_gate, "ssm_w_out": g_wout_s.reshape(ssm_w_out.shape),
        "mlp_w1": jnp.stack([g_w1_0, g_w1_1]), "mlp_w2": jnp.stack([g_w2_0, g_w2_1]),
        "final_g": g_final,
    }
    weights = dict(norm_mix_g=norm_mix_g, norm_mlp_g=norm_mlp_g, pool_w=pool_w, pool_b=pool_b,
                   pool_scale=pool_scale, ssm_w_in=ssm_w_in, ssm_conv_w=ssm_conv_w, ssm_conv_b=ssm_conv_b,
                   ssm_dt_bias=ssm_dt_bias, ssm_a_log=ssm_a_log, ssm_d=ssm_d, ssm_norm_g=ssm_norm_g,
                   ssm_w_out=ssm_w_out, mlp_w1=mlp_w1, mlp_w2=mlp_w2, final_g=final_g)
    moms = dict(norm_mix_g=(m_norm_mix_g, v_norm_mix_g), norm_mlp_g=(m_norm_mlp_g, v_norm_mlp_g),
                pool_w=(m_pool_w, v_pool_w), pool_b=(m_pool_b, v_pool_b),
                pool_scale=(m_pool_scale, v_pool_scale), ssm_w_in=(m_ssm_w_in, v_ssm_w_in),
                ssm_conv_w=(m_ssm_conv_w, v_ssm_conv_w), ssm_conv_b=(m_ssm_conv_b, v_ssm_conv_b),
                ssm_dt_bias=(m_ssm_dt_bias, v_ssm_dt_bias), ssm_a_log=(m_ssm_a_log, v_ssm_a_log),
                ssm_d=(m_ssm_d, v_ssm_d), ssm_norm_g=(m_ssm_norm_g, v_ssm_norm_g),
                ssm_w_out=(m_ssm_w_out, v_ssm_w_out), mlp_w1=(m_mlp_w1, v_mlp_w1),
                mlp_w2=(m_mlp_w2, v_mlp_w2), final_g=(m_final_g, v_final_g))
    names = list(weights)
    big_names = ("pool_w", "ssm_w_in", "ssm_w_out", "mlp_w1", "mlp_w2")
    deltas, new_m, new_v = {}, {}, {}
    for nm in big_names:
        w = weights[nm]
        two_d = (-1, w.shape[-1])
        d_, m_, v_ = _adamw(w.reshape(two_d), grads[nm].reshape(two_d), moms[nm][0].reshape(two_d),
                            moms[nm][1].reshape(two_d), name="adamw_" + nm)
        deltas[nm], new_m[nm], new_v[nm] = d_.reshape(w.shape), m_.reshape(w.shape), v_.reshape(w.shape)
    small_names = [nm for nm in names if nm not in big_names]
    sizes = [weights[nm].size for nm in small_names]

    def pack(parts):
        flat = jnp.concatenate([p.reshape(-1) for p in parts])
        pad = (-flat.shape[0]) % (8 * D_MODEL)
        return jnp.pad(flat, (0, pad)).reshape(-1, D_MODEL)

    d_, m_, v_ = _adamw(pack([weights[nm] for nm in small_names]), pack([grads[nm] for nm in small_names]),
                        pack([moms[nm][0] for nm in small_names]), pack([moms[nm][1] for nm in small_names]),
                        name="adamw_small")
    off = 0
    for nm, sz in zip(small_names, sizes):
        shp = weights[nm].shape
        deltas[nm] = d_.reshape(-1)[off:off + sz].reshape(shp)
        new_m[nm] = m_.reshape(-1)[off:off + sz].reshape(shp)
        new_v[nm] = v_.reshape(-1)[off:off + sz].reshape(shp)
        off += sz

    grad_x = dx.reshape(x.shape)
    out_grads = [grads[nm].reshape(weights[nm].shape) for nm in names]
    return (loss, grad_x, *out_grads, *[deltas[nm] for nm in names],
            *[new_m[nm] for nm in names], *[new_v[nm] for nm in names])
```

```python
import functools

import jax
import jax.numpy as jnp
from jax import lax
from jax.experimental import pallas as pl
from jax.experimental.pallas import tpu as pltpu

F32 = jnp.float32
BF16 = jnp.bfloat16
MESH = pl.DeviceIdType.MESH

D_MODEL = 1024
RMS_EPS = 1e-5
POOL_WINDOWS = (2, 4, 8, 16)
POOL_GROUP = 256
POOL_HALO = 16
D_INNER = 2048
HEAD_DIM = 64
N_HEADS = 32
N_GROUPS = 4
HEADS_PER_GROUP = 8
D_STATE = 128
CONV_K = 4
CONV_HALO = 8
CHUNK = 128
CONV_DIM = 3072
IN_PROJ_DIM = 5152
D_FF = 4096
N_CHIPS = 4
N_DEV = 8

ADAM_LR = 0.001
ADAM_B1 = 0.9
ADAM_B2 = 0.999
ADAM_EPS = 1e-08
ADAM_WD = 0.01
ADAM_STEP = 10

VMEM_LIMIT = 56 * 1024 * 1024
NEG_INF = float("-inf")


def _pcall(body, **kw):
    return pl.pallas_call(body, **kw)


def _params(*sem):
    return pltpu.CompilerParams(dimension_semantics=sem, vmem_limit_bytes=VMEM_LIMIT)


def _sigmoid(v):
    return 1.0 / (1.0 + jnp.exp(-v))


def _row_spec(tb, d, nb=None, reverse=False):
    if reverse:
        return pl.BlockSpec((tb, d), lambda i: (nb - 1 - i, 0))
    return pl.BlockSpec((tb, d), lambda i: (i, 0))


def _const_spec(shape):
    return pl.BlockSpec(shape, lambda *_: tuple(0 for _ in shape))


_DIMS = {"nn": (((1,), (0,)), ((), ())),
         "nt": (((1,), (1,)), ((), ())),
         "tn": (((0,), (0,)), ((), ()))}


_MATMUL_VMEM_BUDGET = 40 * 1024 * 1024


def _matmul_tiles(m_dim, n_dim, k_dim, a_bytes, b_bytes, mn_bytes):
    tm, tn = min(m_dim, 1024), min(n_dim, 1024)
    while 2 * (tm * k_dim * a_bytes + tn * k_dim * b_bytes + tm * tn * mn_bytes) > _MATMUL_VMEM_BUDGET:
        if tm >= tn:
            tm //= 2
        else:
            tn //= 2
    return tm, tn


def _matmul(a, b, mode, *, name, out_dtype=F32, a_relu2=False, add=None, relu2_grad_of=None,
            out_col_shards=1):
    if mode == "tn":
        k_dim, m_dim = a.shape
    else:
        m_dim, k_dim = a.shape
    n_dim = b.shape[0] if mode == "nt" else b.shape[1]
    mn_bytes = jnp.dtype(out_dtype).itemsize
    if relu2_grad_of is not None:
        mn_bytes += relu2_grad_of.dtype.itemsize
    if add is not None:
        mn_bytes += add.dtype.itemsize
    tm, tn = _matmul_tiles(m_dim, n_dim, k_dim, a.dtype.itemsize, b.dtype.itemsize, mn_bytes)
    assert m_dim % tm == 0 and n_dim % tn == 0
    a_spec = (pl.BlockSpec((k_dim, tm), lambda i, j: (0, i)) if mode == "tn"
              else pl.BlockSpec((tm, k_dim), lambda i, j: (i, 0)))
    b_spec = (pl.BlockSpec((tn, k_dim), lambda i, j: (j, 0)) if mode == "nt"
              else pl.BlockSpec((k_dim, tn), lambda i, j: (0, j)))
    mn_spec = pl.BlockSpec((tm, tn), lambda i, j: (i, j))
    operands, in_specs = [a, b], [a_spec, b_spec]
    if relu2_grad_of is not None:
        operands.append(relu2_grad_of)
        in_specs.append(mn_spec)
    if add is not None:
        operands.append(add)
        in_specs.append(mn_spec)
    if out_col_shards == 1:
        out_shape = jax.ShapeDtypeStruct((m_dim, n_dim), out_dtype)
        out_spec = mn_spec
    else:
        n_shard = n_dim // out_col_shards
        assert n_shard % tn == 0
        per = n_shard // tn
        out_shape = jax.ShapeDtypeStruct((out_col_shards, m_dim, n_shard), out_dtype)
        out_spec = pl.BlockSpec((None, tm, tn), lambda i, j: (j // per, i, j % per))

    def body(*refs):
        a_ref, b_ref, o_ref = refs[0], refs[1], refs[-1]
        av = a_ref[...]
        if a_relu2:
            av = jnp.maximum(av, 0)
            av = av * av
        r = lax.dot_general(av.astype(BF16), b_ref[...].astype(BF16), _DIMS[mode],
                            preferred_element_type=F32)
        nxt = 2
        if relu2_grad_of is not None:
            r = r * (2.0 * jnp.maximum(refs[nxt][...].astype(F32), 0.0))
            nxt += 1
        if add is not None:
            r = r + refs[nxt][...]
        o_ref[...] = r.astype(out_dtype)

    return _pcall(
        body, name=name, out_shape=out_shape,
        grid=(m_dim // tm, n_dim // tn),
        in_specs=in_specs, out_specs=out_spec,
        compiler_params=_params("parallel", "parallel"),
    )(*operands)


def _rms(x):
    return lax.rsqrt(jnp.mean(x * x, axis=-1, keepdims=True) + RMS_EPS)


def _rmsnorm_fwd(h, g, *, name, tb=512):
    t_dim, d = h.shape

    def body(h_ref, g_ref, o_ref):
        x = h_ref[...]
        o_ref[...] = (x * _rms(x) * g_ref[...]).astype(BF16)

    return _pcall(
        body, name=name, out_shape=jax.ShapeDtypeStruct((t_dim, d), BF16),
        grid=(t_dim // tb,), in_specs=[_row_spec(tb, d), _const_spec((1, d))],
        out_specs=_row_spec(tb, d), compiler_params=_params("parallel"),
    )(h, g)


def _rmsnorm_bwd(dy, h, g, dres, *, name, tb=512):
    t_dim, d = h.shape

    def body(dy_ref, h_ref, g_ref, dres_ref, dh_ref, dh16_ref, dg_ref):
        @pl.when(pl.program_id(0) == 0)
        def _():
            dg_ref[...] = jnp.zeros_like(dg_ref)

        x = h_ref[...]
        r = _rms(x)
        xhat = x * r
        dyv = dy_ref[...]
        dxhat = dyv * g_ref[...]
        dh = dres_ref[...] + r * (dxhat - xhat * jnp.mean(dxhat * xhat, axis=-1, keepdims=True))
        dh_ref[...] = dh
        dh16_ref[...] = dh.astype(BF16)
        dg_ref[0:1, :] += jnp.sum(dyv * xhat, axis=0, keepdims=True)

    return _pcall(
        body, name=name,
        out_shape=(jax.ShapeDtypeStruct((t_dim, d), F32), jax.ShapeDtypeStruct((t_dim, d), BF16),
                   jax.ShapeDtypeStruct((8, d), F32)),
        grid=(t_dim // tb,),
        in_specs=[_row_spec(tb, d), _row_spec(tb, d), _const_spec((1, d)), _row_spec(tb, d)],
        out_specs=(_row_spec(tb, d), _row_spec(tb, d), _const_spec((8, d))),
        compiler_params=_params("arbitrary"),
    )(dy, h, g, dres)


def _pool_mixed(ext, hn, t0, tb):
    t = t0 + lax.broadcasted_iota(jnp.int32, (tb, 1), 0)
    parts = []
    for gi, w in enumerate(POOL_WINDOWS):
        lanes = slice(gi * POOL_GROUP, (gi + 1) * POOL_GROUP)
        s = ext[:, lanes]
        k = 1
        while k < w:
            s = s + pltpu.roll(s, k, 0)
            k *= 2
        cnt = jnp.minimum(t + 1, w).astype(F32)
        parts.append(s[POOL_HALO:, :] / cnt - hn[:, lanes])
    return parts


def _pool_fwd(x, g, pw, pb, ps, *, tb=512):
    t_dim, d = x.shape

    def body(x_ref, g_ref, pw_ref, pb_ref, ps_ref, o_ref, ext_ref):
        i = pl.program_id(0)

        @pl.when(i == 0)
        def _():
            ext_ref[0:POOL_HALO, :] = jnp.zeros((POOL_HALO, d), F32)

        xv = x_ref[...]
        hn = xv * _rms(xv) * g_ref[...]
        ext_ref[POOL_HALO:, :] = hn
        mixed = _pool_mixed(ext_ref[...], hn, i * tb, tb)
        for gi in range(len(POOL_WINDOWS)):
            lanes = slice(gi * POOL_GROUP, (gi + 1) * POOL_GROUP)
            out = jnp.dot(mixed[gi].astype(BF16), pw_ref[gi], preferred_element_type=F32)
            o_ref[:, lanes] = xv[:, lanes] + (out + pb_ref[:, lanes]) * ps_ref[:, lanes]
        ext_ref[0:POOL_HALO, :] = hn[tb - POOL_HALO:, :]

    return _pcall(
        body, name="pool_fwd", out_shape=jax.ShapeDtypeStruct((t_dim, d), F32),
        grid=(t_dim // tb,),
        in_specs=[_row_spec(tb, d), _const_spec((1, d)), _const_spec((4, POOL_GROUP, POOL_GROUP)),
                  _const_spec((1, d)), _const_spec((1, d))],
        out_specs=_row_spec(tb, d),
        scratch_shapes=[pltpu.VMEM((POOL_HALO + tb, d), F32)],
        compiler_params=_params("arbitrary"),
    )(x, g, pw, pb, ps)


def _pool_bwd(x, g, pw, pb, ps, dh1, *, tb=512):
    t_dim, d = x.shape
    nb = t_dim // tb
    halo_per_block = tb // POOL_HALO

    def body(x_ref, xprev_ref, g_ref, pw_ref, pb_ref, ps_ref, dh1_ref,
             dx_ref, dpw_ref, small_ref, ext_ref, dext_ref):
        i = pl.program_id(0)
        blk = nb - 1 - i

        @pl.when(i == 0)
        def _():
            dpw_ref[...] = jnp.zeros_like(dpw_ref)
            small_ref[...] = jnp.zeros_like(small_ref)
            dext_ref[tb:, :] = jnp.zeros((POOL_HALO, d), F32)

        gv = g_ref[...]
        xv = x_ref[...]
        r = _rms(xv)
        xhat = xv * r
        hn = xhat * gv
        xp = xprev_ref[...]
        hprev = xp * _rms(xp) * gv * (blk > 0).astype(F32)
        ext_ref[0:POOL_HALO, :] = hprev
        ext_ref[POOL_HALO:, :] = hn
        mixed = _pool_mixed(ext_ref[...], hn, blk * tb, tb)

        dout = dh1_ref[...]
        t = blk * tb + lax.broadcasted_iota(jnp.int32, (tb, 1), 0)
        for gi, w in enumerate(POOL_WINDOWS):
            lanes = slice(gi * POOL_GROUP, (gi + 1) * POOL_GROUP)
            mb = mixed[gi].astype(BF16)
            pre = jnp.dot(mb, pw_ref[gi], preferred_element_type=F32) + pb_ref[:, lanes]
            dg_out = dout[:, lanes]
            small_ref[2:3, lanes] += jnp.sum(dg_out * pre, axis=0, keepdims=True)
            dpre = dg_out * ps_ref[:, lanes]
            small_ref[1:2, lanes] += jnp.sum(dpre, axis=0, keepdims=True)
            dpb16 = dpre.astype(BF16)
            dpw_ref[gi] += lax.dot_general(mb, dpb16, _DIMS["tn"], preferred_element_type=F32)
            dmixed = lax.dot_general(dpb16, pw_ref[gi], _DIMS["nt"], preferred_element_type=F32)
            cnt = jnp.minimum(t + 1, w).astype(F32)
            dq = dmixed / cnt
            dext_ref[0:tb, lanes] = dq
            s = dext_ref[:, lanes]
            k = 1
            while k < w:
                s = s + pltpu.roll(s, tb + POOL_HALO - k, 0)
                k *= 2
            dhn = s[0:tb, :] - dmixed
            dext_ref[tb:, lanes] = dq[0:POOL_HALO, :]
            small_ref[0:1, lanes] += jnp.sum(dhn * xhat[:, lanes], axis=0, keepdims=True)
            ext_ref[POOL_HALO:, lanes] = dhn * gv[:, lanes]
        dxhat = ext_ref[POOL_HALO:, :]
        dx_ref[...] = dout + r * (dxhat - xhat * jnp.mean(dxhat * xhat, axis=-1, keepdims=True))

    return _pcall(
        body, name="pool_bwd",
        out_shape=(jax.ShapeDtypeStruct((t_dim, d), F32),
                   jax.ShapeDtypeStruct((4, POOL_GROUP, POOL_GROUP), F32),
                   jax.ShapeDtypeStruct((8, d), F32)),
        grid=(nb,),
        in_specs=[_row_spec(tb, d, nb, True),
                  pl.BlockSpec((POOL_HALO, d),
                               lambda i: (jnp.maximum((nb - 1 - i) * halo_per_block - 1, 0), 0)),
                  _const_spec((1, d)), _const_spec((4, POOL_GROUP, POOL_GROUP)),
                  _const_spec((1, d)), _const_spec((1, d)), _row_spec(tb, d, nb, True)],
        out_specs=(_row_spec(tb, d, nb, True), _const_spec((4, POOL_GROUP, POOL_GROUP)),
                   _const_spec((8, d))),
        scratch_shapes=[pltpu.VMEM((POOL_HALO + tb, d), F32), pltpu.VMEM((tb + POOL_HALO, d), F32)],
        compiler_params=_params("arbitrary"),
    )(x, x, g, pw, pb, ps, dh1)


_CONV_CB = 1024


def _conv_pre(e, u, w, b, tb):
    acc = u * w[CONV_K - 1:CONV_K, :] + b
    for sh in range(1, CONV_K):
        acc = acc + pltpu.roll(e, sh, 0)[CONV_HALO:, :] * w[CONV_K - 1 - sh:CONV_K - sh, :]
    return acc


def _conv_fwd(u, w, b, *, tb=512):
    t_dim, c = u.shape
    cb = _CONV_CB

    def body(u_ref, w_ref, b_ref, o_ref, ext_ref):
        @pl.when(pl.program_id(1) == 0)
        def _():
            ext_ref[0:CONV_HALO, :] = jnp.zeros((CONV_HALO, cb), F32)

        uv = u_ref[...].astype(F32)
        ext_ref[CONV_HALO:, :] = uv
        v = _conv_pre(ext_ref[...], uv, w_ref[...], b_ref[...], tb)
        o_ref[...] = (v * _sigmoid(v)).astype(BF16)
        ext_ref[0:CONV_HALO, :] = uv[tb - CONV_HALO:, :]

    blk = pl.BlockSpec((tb, cb), lambda j, t: (t, j))
    return _pcall(
        body, name="conv_fwd", out_shape=jax.ShapeDtypeStruct((t_dim, c), BF16),
        grid=(c // cb, t_dim // tb),
        in_specs=[blk, pl.BlockSpec((CONV_K, cb), lambda j, t: (0, j)),
                  pl.BlockSpec((1, cb), lambda j, t: (0, j))],
        out_specs=blk,
        scratch_shapes=[pltpu.VMEM((CONV_HALO + tb, cb), F32)],
        compiler_params=_params("parallel", "arbitrary"),
    )(u, w, b)


def _conv_bwd_act(u, dxc, w, b, *, tb=512):
    t_dim, c = u.shape
    cb = _CONV_CB

    def body(u_ref, d_ref, w_ref, b_ref, dv_ref, dwb_ref, ext_ref):
        @pl.when(pl.program_id(1) == 0)
        def _():
            ext_ref[0:CONV_HALO, :] = jnp.zeros((CONV_HALO, cb), F32)
            dwb_ref[...] = jnp.zeros_like(dwb_ref)

        uv = u_ref[...].astype(F32)
        ext_ref[CONV_HALO:, :] = uv
        e = ext_ref[...]
        v = _conv_pre(e, uv, w_ref[...], b_ref[...], tb)
        sg = _sigmoid(v)
        dv = d_ref[...].astype(F32) * (sg * (1.0 + v * (1.0 - sg)))
        dv_ref[...] = dv.astype(BF16)
        dwb_ref[CONV_K:CONV_K + 1, :] += jnp.sum(dv, axis=0, keepdims=True)
        dwb_ref[CONV_K - 1:CONV_K, :] += jnp.sum(dv * uv, axis=0, keepdims=True)
        for sh in range(1, CONV_K):
            us = pltpu.roll(e, sh, 0)[CONV_HALO:, :]
            dwb_ref[CONV_K - 1 - sh:CONV_K - sh, :] += jnp.sum(dv * us, axis=0, keepdims=True)
        ext_ref[0:CONV_HALO, :] = uv[tb - CONV_HALO:, :]

    blk = pl.BlockSpec((tb, cb), lambda j, t: (t, j))
    return _pcall(
        body, name="conv_bwd_act",
        out_shape=(jax.ShapeDtypeStruct((t_dim, c), BF16), jax.ShapeDtypeStruct((8, c), F32)),
        grid=(c // cb, t_dim // tb),
        in_specs=[blk, blk, pl.BlockSpec((CONV_K, cb), lambda j, t: (0, j)),
                  pl.BlockSpec((1, cb), lambda j, t: (0, j))],
        out_specs=(blk, pl.BlockSpec((8, cb), lambda j, t: (0, j))),
        scratch_shapes=[pltpu.VMEM((CONV_HALO + tb, cb), F32)],
        compiler_params=_params("parallel", "arbitrary"),
    )(u, dxc, w, b)


def _conv_bwd_in(dv, w, *, tb=512):
    t_dim, c = dv.shape
    cb = _CONV_CB
    nb = t_dim // tb

    def body(dv_ref, w_ref, du_ref, ext_ref):
        @pl.when(pl.program_id(1) == 0)
        def _():
            ext_ref[tb:, :] = jnp.zeros((CONV_HALO, cb), F32)

        d = dv_ref[...].astype(F32)
        ext_ref[0:tb, :] = d
        e = ext_ref[...]
        wv = w_ref[...]
        acc = d * wv[CONV_K - 1:CONV_K, :]
        for sh in range(1, CONV_K):
            acc = acc + pltpu.roll(e, tb + CONV_HALO - sh, 0)[0:tb, :] * wv[CONV_K - 1 - sh:CONV_K - sh, :]
        du_ref[...] = acc.astype(BF16)
        ext_ref[tb:, :] = d[0:CONV_HALO, :]

    blk = pl.BlockSpec((tb, cb), lambda j, t: (nb - 1 - t, j))
    return _pcall(
        body, name="conv_bwd_in", out_shape=jax.ShapeDtypeStruct((t_dim, c), BF16),
        grid=(c // cb, nb),
        in_specs=[blk, pl.BlockSpec((CONV_K, cb), lambda j, t: (0, j))],
        out_specs=blk,
        scratch_shapes=[pltpu.VMEM((tb + CONV_HALO, cb), F32)],
        compiler_params=_params("parallel", "arbitrary"),
    )(dv, w)


def _softplus(v):
    e = jnp.exp(-jnp.abs(v))
    w = 1.0 + e
    log1p = jnp.where(w == 1.0, e, jnp.log(w) * e / jnp.where(w == 1.0, 1.0, w - 1.0))
    return jnp.maximum(v, 0.0) + log1p


def _cumsum_rows(v):
    row = lax.broadcasted_iota(jnp.int32, v.shape, 0)
    k = 1
    while k < CHUNK:
        v = v + jnp.where(row >= k, pltpu.roll(v, k, 0), 0.0)
        k *= 2
    return v


def _cumsum_lanes(v):
    col = lax.broadcasted_iota(jnp.int32, v.shape, 1)
    k = 1
    while k < CHUNK:
        v = v + jnp.where(col >= k, pltpu.roll(v, k, 1), 0.0)
        k *= 2
    return v


def _rev_cumsum_rows(v):
    row = lax.broadcasted_iota(jnp.int32, v.shape, 0)
    k = 1
    while k < CHUNK:
        v = v + jnp.where(row < CHUNK - k, pltpu.roll(v, CHUNK - k, 0), 0.0)
        k *= 2
    return v


PAIR = 2 * HEAD_DIM
GROUP_LANES = HEADS_PER_GROUP * HEAD_DIM


def _head_lane_matrix():
    h = lax.broadcasted_iota(jnp.int32, (128, D_INNER), 0)
    j = lax.broadcasted_iota(jnp.int32, (128, D_INNER), 1)
    return (j // HEAD_DIM == h).astype(BF16)


def _split_bf16(v, pieces):
    out = []
    for _ in range(pieces):
        p = v.astype(BF16)
        out.append(p)
        v = v - p.astype(F32)
    return out


def _expand_heads(v, e):
    return sum(jnp.dot(p, e, preferred_element_type=F32) for p in _split_bf16(v, 3))


def _reduce_heads(v, et, pieces):
    return sum(jnp.dot(p, et, preferred_element_type=F32) for p in _split_bf16(v, pieces))


def _ssd_common(dtr_ref, dtt_ref, e_ref, bx, bc, ax, ac):
    dtx = _expand_heads(dtr_ref[...], e_ref[...])
    dt = _softplus(dtx + bx)
    a_x = -jnp.exp(ax)
    acs = _cumsum_rows(dt * a_x)
    acs_c = _cumsum_lanes(_softplus(dtt_ref[...] + bc) * (-jnp.exp(ac)))
    return dtx, dt, a_x, acs, acs_c


def _pair_decay(acs_slab, acs_c, h0, causal, left):
    other = pltpu.roll(acs_slab, HEAD_DIM, 1)
    col0 = jnp.where(left, acs_slab, other)
    col1 = jnp.where(left, other, acs_slab)
    l0 = jnp.exp(jnp.where(causal, col0 - acs_c[h0:h0 + 1, :], NEG_INF))
    l1 = jnp.exp(jnp.where(causal, col1 - acs_c[h0 + 1:h0 + 2, :], NEG_INF))
    return l0, l1


def _ssd_fwd(xc, dt_raw, dt_raw_t, bias_x, bias_c, alog_x, alog_c, dskip_x, e_mat):
    t_dim = xc.shape[0]
    nc = t_dim // CHUNK

    def body(xc_ref, dtr_ref, dtt_ref, bx_ref, bc_ref, ax_ref, ac_ref, dk_ref, e_ref,
             y_ref, st_ref, state):
        @pl.when(pl.program_id(0) == 0)
        def _():
            state[...] = jnp.zeros_like(state)

        _, dt, _, acs, acs_c = _ssd_common(dtr_ref, dtt_ref, e_ref, bx_ref[...], bc_ref[...],
                                           ax_ref[...], ac_ref[...])
        st_ref[0] = state[...]
        last = acs[CHUNK - 1:CHUNK, :]
        xs32 = xc_ref[:, 0:D_INNER].astype(F32)
        xdt = xs32 * dt
        xdt16 = xdt.astype(BF16)
        xdte16 = (xdt * jnp.exp(last - acs)).astype(BF16)
        ea = jnp.exp(acs)
        cd = jnp.exp(last)
        skip = dk_ref[...] * xs32
        causal = (lax.broadcasted_iota(jnp.int32, (CHUNK, CHUNK), 0)
                  >= lax.broadcasted_iota(jnp.int32, (CHUNK, CHUNK), 1))
        left = lax.broadcasted_iota(jnp.int32, (CHUNK, PAIR), 1) < HEAD_DIM
        for g in range(N_GROUPS):
            gl = slice(g * GROUP_LANES, (g + 1) * GROUP_LANES)
            bg = xc_ref[:, D_INNER + g * D_STATE:D_INNER + (g + 1) * D_STATE]
            cg = xc_ref[:, D_INNER + (N_GROUPS + g) * D_STATE:D_INNER + (N_GROUPS + g + 1) * D_STATE]
            cb = lax.dot_general(cg, bg, _DIMS["nt"], preferred_element_type=F32)
            hprev = state[:, gl]
            ch = jnp.dot(cg, hprev.astype(BF16), preferred_element_type=F32)
            for j in range(HEADS_PER_GROUP // 2):
                pl_ = slice(g * GROUP_LANES + j * PAIR, g * GROUP_LANES + (j + 1) * PAIR)
                h0 = g * HEADS_PER_GROUP + 2 * j
                l0, l1 = _pair_decay(acs[:, pl_], acs_c, h0, causal, left)
                lhs = jnp.concatenate([(cb * l0).astype(BF16), (cb * l1).astype(BF16)], axis=1)
                xp = xdt16[:, pl_]
                zero = jnp.zeros_like(xp)
                rhs = jnp.concatenate([jnp.where(left, xp, zero), jnp.where(left, zero, xp)], axis=0)
                ydiag = jnp.dot(lhs, rhs, preferred_element_type=F32)
                y_ref[:, pl_] = ydiag + ch[:, j * PAIR:(j + 1) * PAIR] * ea[:, pl_] + skip[:, pl_]
            s_new = lax.dot_general(bg, xdte16[:, gl], _DIMS["tn"], preferred_element_type=F32)
            state[:, gl] = hprev * cd[:, gl] + s_new

    rows = lambda w: pl.BlockSpec((CHUNK, w), lambda c: (c, 0))
    return _pcall(
        body, name="ssd_fwd",
        out_shape=(jax.ShapeDtypeStruct((t_dim, D_INNER), F32),
                   jax.ShapeDtypeStruct((nc, D_STATE, D_INNER), F32)),
        grid=(nc,),
        in_specs=[rows(CONV_DIM), rows(128), pl.BlockSpec((N_HEADS, CHUNK), lambda c: (0, c)),
                  _const_spec((1, D_INNER)), _const_spec((N_HEADS, 1)),
                  _const_spec((1, D_INNER)), _const_spec((N_HEADS, 1)), _const_spec((1, D_INNER)),
                  _const_spec((128, D_INNER))],
        out_specs=(rows(D_INNER), pl.BlockSpec((1, D_STATE, D_INNER), lambda c: (c, 0, 0))),
        scratch_shapes=[pltpu.VMEM((D_STATE, D_INNER), F32)],
        compiler_params=_params("arbitrary"),
    )(xc, dt_raw, dt_raw_t, bias_x, bias_c, alog_x, alog_c, dskip_x, e_mat)


def _ssd_bwd(xc, dt_raw, dt_raw_t, bias_x, bias_c, alog_x, alog_c, dskip_x, bias_r, alog_r,
             e_mat, et_mat, states, dy):
    t_dim = xc.shape[0]
    nc = t_dim // CHUNK

    def body(xc_ref, dtr_ref, dtt_ref, bx_ref, bc_ref, ax_ref, ac_ref, dk_ref, br_ref, ar_ref,
             e_ref, et_ref, st_ref, dy_ref,
             dxc_ref, ddt_ref, small_ref, dstate, dacs_ref, dxdt_ref, acc_x, acc_r):
        step = pl.program_id(0)

        @pl.when(step == 0)
        def _():
            dstate[...] = jnp.zeros_like(dstate)
            acc_x[...] = jnp.zeros_like(acc_x)
            acc_r[...] = jnp.zeros_like(acc_r)

        dtx, dt, a_x, acs, acs_c = _ssd_common(dtr_ref, dtt_ref, e_ref, bx_ref[...], bc_ref[...],
                                               ax_ref[...], ac_ref[...])
        last = acs[CHUNK - 1:CHUNK, :]
        xs32 = xc_ref[:, 0:D_INNER].astype(F32)
        xdt = xs32 * dt
        xdt16 = xdt.astype(BF16)
        dte = jnp.exp(last - acs)
        xdte = xdt * dte
        xdte16 = xdte.astype(BF16)
        cd = jnp.exp(last)
        dyv = dy_ref[...]
        dy16 = dyv.astype(BF16)
        dye = dyv * jnp.exp(acs)
        dye16 = dye.astype(BF16)
        causal = (lax.broadcasted_iota(jnp.int32, (CHUNK, CHUNK), 0)
                  >= lax.broadcasted_iota(jnp.int32, (CHUNK, CHUNK), 1))
        left = lax.broadcasted_iota(jnp.int32, (CHUNK, PAIR), 1) < HEAD_DIM
        lane_id = lax.broadcasted_iota(jnp.int32, (CHUNK, 128), 1)
        row_id = lax.broadcasted_iota(jnp.int32, (CHUNK, 128), 0)
        is_last_row = lax.broadcasted_iota(jnp.int32, (CHUNK, 1), 0) == CHUNK - 1
        dacs_cols = jnp.zeros((CHUNK, 128), F32)
        dacs_rows = jnp.zeros((CHUNK, 128), F32)
        for g in range(N_GROUPS):
            gl = slice(g * GROUP_LANES, (g + 1) * GROUP_LANES)
            b_lanes = slice(D_INNER + g * D_STATE, D_INNER + (g + 1) * D_STATE)
            c_lanes = slice(D_INNER + (N_GROUPS + g) * D_STATE, D_INNER + (N_GROUPS + g + 1) * D_STATE)
            bg = xc_ref[:, b_lanes]
            cg = xc_ref[:, c_lanes]
            cb = lax.dot_general(cg, bg, _DIMS["nt"], preferred_element_type=F32)
            hprev = st_ref[0, :, gl]
            hp16 = hprev.astype(BF16)
            dhn = dstate[:, gl]
            dhn16 = dhn.astype(BF16)
            ch = jnp.dot(cg, hp16, preferred_element_type=F32)
            gmat = jnp.dot(bg, dhn16, preferred_element_type=F32)
            gx = gmat * xdte[:, gl]
            dlast = jnp.sum(gx, axis=0, keepdims=True) + cd[:, gl] * jnp.sum(dhn * hprev, axis=0, keepdims=True)
            dacs_ref[:, gl] = dye[:, gl] * ch - gx + jnp.where(is_last_row, dlast, 0.0)
            dc_acc = lax.dot_general(dye16[:, gl], hp16, _DIMS["nt"], preferred_element_type=F32)
            db_acc = lax.dot_general(xdte16[:, gl], dhn16, _DIMS["nt"], preferred_element_type=F32)
            dstate[:, gl] = dhn * cd[:, gl] + lax.dot_general(cg, dye16[:, gl], _DIMS["tn"],
                                                             preferred_element_type=F32)
            dcb = jnp.zeros((CHUNK, CHUNK), F32)
            for j in range(HEADS_PER_GROUP // 2):
                pl_ = slice(g * GROUP_LANES + j * PAIR, g * GROUP_LANES + (j + 1) * PAIR)
                h0 = g * HEADS_PER_GROUP + 2 * j
                l0, l1 = _pair_decay(acs[:, pl_], acs_c, h0, causal, left)
                m0, m1 = cb * l0, cb * l1
                lhs = jnp.concatenate([m0.astype(BF16), m1.astype(BF16)], axis=1)
                dyp = dy16[:, pl_]
                zero = jnp.zeros_like(dyp)
                both = lax.dot_general(lhs, dyp, _DIMS["tn"], preferred_element_type=F32)
                dxdt_ref[:, pl_] = (jnp.where(left, both[0:CHUNK, :], both[CHUNK:, :])
                                    + gmat[:, j * PAIR:(j + 1) * PAIR] * dte[:, pl_])
                lhs2 = jnp.concatenate([jnp.where(left, dyp, zero), jnp.where(left, zero, dyp)], axis=0)
                dm = lax.dot_general(lhs2, xdt16[:, pl_], _DIMS["nt"], preferred_element_type=F32)
                dm0, dm1 = dm[0:CHUNK, :], dm[CHUNK:, :]
                dcb = dcb + dm0 * l0 + dm1 * l1
                ds0, ds1 = dm0 * m0, dm1 * m1
                dacs_cols = jnp.where(lane_id == h0, jnp.sum(ds0, axis=1, keepdims=True), dacs_cols)
                dacs_cols = jnp.where(lane_id == h0 + 1, jnp.sum(ds1, axis=1, keepdims=True), dacs_cols)
                dacs_rows = jnp.where(row_id == h0, jnp.sum(ds0, axis=0, keepdims=True), dacs_rows)
                dacs_rows = jnp.where(row_id == h0 + 1, jnp.sum(ds1, axis=0, keepdims=True), dacs_rows)
            dcb16 = dcb.astype(BF16)
            dxc_ref[:, c_lanes] = (dc_acc + jnp.dot(dcb16, bg, preferred_element_type=F32)).astype(BF16)
            dxc_ref[:, b_lanes] = (db_acc + lax.dot_general(dcb16, cg, _DIMS["tn"],
                                                           preferred_element_type=F32)).astype(BF16)
        dxdt = dxdt_ref[...]
        dxc_ref[:, 0:D_INNER] = (dxdt * dt + dk_ref[...] * dyv).astype(BF16)
        dadt = _rev_cumsum_rows(dacs_ref[...])
        ddraw_x = (dxdt * xs32 + dadt * a_x) * _sigmoid(dtx + bx_ref[...])
        acc_x[0:1, :] += jnp.sum(dadt * dt, axis=0, keepdims=True) * a_x
        acc_x[1:2, :] += jnp.sum(ddraw_x, axis=0, keepdims=True)
        acc_x[2:3, :] += jnp.sum(dyv * xs32, axis=0, keepdims=True)
        a_r = -jnp.exp(ar_ref[...])
        pre_r = dtr_ref[...] + br_ref[...]
        dadt_r = _rev_cumsum_rows(dacs_cols - dacs_rows.T)
        ddraw_r = jnp.where(lane_id < N_HEADS, dadt_r * a_r * _sigmoid(pre_r), 0.0)
        acc_r[0:1, :] += jnp.where(lane_id[0:1, :] < N_HEADS,
                                   jnp.sum(dadt_r * _softplus(pre_r), axis=0, keepdims=True) * a_r, 0.0)
        acc_r[1:2, :] += jnp.sum(ddraw_r, axis=0, keepdims=True)
        ddt_ref[...] = ddraw_r + _reduce_heads(ddraw_x, et_ref[...], 2)

        @pl.when(step == nc - 1)
        def _():
            small_ref[...] = acc_r[...] + _reduce_heads(acc_x[...], et_ref[...], 3)

    rev = lambda w: pl.BlockSpec((CHUNK, w), lambda c: (nc - 1 - c, 0))
    return _pcall(
        body, name="ssd_bwd",
        out_shape=(jax.ShapeDtypeStruct((t_dim, CONV_DIM), BF16),
                   jax.ShapeDtypeStruct((t_dim, 128), F32),
                   jax.ShapeDtypeStruct((8, 128), F32)),
        grid=(nc,),
        in_specs=[rev(CONV_DIM), rev(128), pl.BlockSpec((N_HEADS, CHUNK), lambda c: (0, nc - 1 - c)),
                  _const_spec((1, D_INNER)), _const_spec((N_HEADS, 1)),
                  _const_spec((1, D_INNER)), _const_spec((N_HEADS, 1)), _const_spec((1, D_INNER)),
                  _const_spec((1, 128)), _const_spec((1, 128)),
                  _const_spec((128, D_INNER)), _const_spec((D_INNER, 128)),
                  pl.BlockSpec((1, D_STATE, D_INNER), lambda c: (nc - 1 - c, 0, 0)),
                  rev(D_INNER)],
        out_specs=(rev(CONV_DIM), rev(128), _const_spec((8, 128))),
        scratch_shapes=[pltpu.VMEM((D_STATE, D_INNER), F32), pltpu.VMEM((CHUNK, D_INNER), F32),
                        pltpu.VMEM((CHUNK, D_INNER), F32), pltpu.VMEM((8, D_INNER), F32),
                        pltpu.VMEM((8, 128), F32)],
        compiler_params=_params("arbitrary"),
    )(xc, dt_raw, dt_raw_t, bias_x, bias_c, alog_x, alog_c, dskip_x, bias_r, alog_r,
      e_mat, et_mat, states, dy)


_GATE_GROUP = D_INNER // N_GROUPS


def _gate_fwd(y, z, g, *, tb=256):
    t_dim = y.shape[0]

    def body(y_ref, z_ref, g_ref, o_ref):
        for gi in range(N_GROUPS):
            lanes = slice(gi * _GATE_GROUP, (gi + 1) * _GATE_GROUP)
            zv = z_ref[:, lanes].astype(F32)
            wv = y_ref[:, lanes] * (zv * _sigmoid(zv))
            o_ref[:, lanes] = (wv * _rms(wv) * g_ref[:, lanes]).astype(BF16)

    return _pcall(
        body, name="gate_fwd", out_shape=jax.ShapeDtypeStruct((t_dim, D_INNER), BF16),
        grid=(t_dim // tb,),
        in_specs=[_row_spec(tb, D_INNER), _row_spec(tb, D_INNER), _const_spec((1, D_INNER))],
        out_specs=_row_spec(tb, D_INNER), compiler_params=_params("parallel"),
    )(y, z, g)


def _gate_bwd(dyn, y, z, g, *, tb=256):
    t_dim = y.shape[0]

    def body(d_ref, y_ref, z_ref, g_ref, dy_ref, dz_ref, dg_ref):
        @pl.when(pl.program_id(0) == 0)
        def _():
            dg_ref[...] = jnp.zeros_like(dg_ref)

        for gi in range(N_GROUPS):
            lanes = slice(gi * _GATE_GROUP, (gi + 1) * _GATE_GROUP)
            zv = z_ref[:, lanes].astype(F32)
            sg = _sigmoid(zv)
            sz = zv * sg
            yv = y_ref[:, lanes]
            wv = yv * sz
            r = _rms(wv)
            what = wv * r
            dv = d_ref[:, lanes]
            dwhat = dv * g_ref[:, lanes]
            dw = r * (dwhat - what * jnp.mean(dwhat * what, axis=-1, keepdims=True))
            dg_ref[0:1, lanes] += jnp.sum(dv * what, axis=0, keepdims=True)
            dy_ref[:, lanes] = dw * sz
            dz_ref[:, lanes] = (dw * yv * (sg * (1.0 + zv * (1.0 - sg)))).astype(BF16)

    return _pcall(
        body, name="gate_bwd",
        out_shape=(jax.ShapeDtypeStruct((t_dim, D_INNER), F32),
                   jax.ShapeDtypeStruct((t_dim, D_INNER), BF16),
                   jax.ShapeDtypeStruct((8, D_INNER), F32)),
        grid=(t_dim // tb,),
        in_specs=[_row_spec(tb, D_INNER), _row_spec(tb, D_INNER), _row_spec(tb, D_INNER),
                  _const_spec((1, D_INNER))],
        out_specs=(_row_spec(tb, D_INNER), _row_spec(tb, D_INNER), _const_spec((8, D_INNER))),
        compiler_params=_params("arbitrary"),
    )(dyn, y, z, g)


def _loss_head(h, g, target, *, tb=512):
    t_dim, d = h.shape

    def body(h_ref, g_ref, t_ref, dh_ref, dh16_ref, small_ref):
        @pl.when(pl.program_id(0) == 0)
        def _():
            small_ref[...] = jnp.zeros_like(small_ref)

        x = h_ref[...]
        r = _rms(x)
        xhat = x * r
        gv = g_ref[...]
        err = xhat * gv - t_ref[...]
        small_ref[1:2, :] += (0.5 / d) * jnp.sum(err * err, axis=0, keepdims=True)
        dyv = err * (1.0 / d)
        dxhat = dyv * gv
        dh = r * (dxhat - xhat * jnp.mean(dxhat * xhat, axis=-1, keepdims=True))
        dh_ref[...] = dh
        dh16_ref[...] = dh.astype(BF16)
        small_ref[0:1, :] += jnp.sum(dyv * xhat, axis=0, keepdims=True)

    return _pcall(
        body, name="loss_head",
        out_shape=(jax.ShapeDtypeStruct((t_dim, d), F32), jax.ShapeDtypeStruct((t_dim, d), BF16),
                   jax.ShapeDtypeStruct((8, d), F32)),
        grid=(t_dim // tb,),
        in_specs=[_row_spec(tb, d), _const_spec((1, d)), _row_spec(tb, d)],
        out_specs=(_row_spec(tb, d), _row_spec(tb, d), _const_spec((8, d))),
        compiler_params=_params("arbitrary"),
    )(h, g, target)


def _adamw(w, g, m, v, *, name):
    r_dim, c = w.shape
    tb = r_dim
    for cand in (512, 256, 128, 64, 32, 16, 8):
        if r_dim % cand == 0:
            tb = cand
            break
    c1 = 1.0 / (1.0 - ADAM_B1 ** ADAM_STEP)
    c2 = 1.0 / (1.0 - ADAM_B2 ** ADAM_STEP)

    def body(w_ref, g_ref, m_ref, v_ref, d_ref, mo_ref, vo_ref):
        gv = g_ref[...]
        mn = ADAM_B1 * m_ref[...] + (1.0 - ADAM_B1) * gv
        vn = ADAM_B2 * v_ref[...] + (1.0 - ADAM_B2) * (gv * gv)
        mo_ref[...] = mn
        vo_ref[...] = vn
        d_ref[...] = -ADAM_LR * ((mn * c1) / (jnp.sqrt(vn * c2) + ADAM_EPS) + ADAM_WD * w_ref[...])

    spec = _row_spec(tb, c)
    sds = jax.ShapeDtypeStruct((r_dim, c), F32)
    return _pcall(
        body, name=name, out_shape=(sds, sds, sds), grid=(r_dim // tb,),
        in_specs=[spec] * 4, out_specs=(spec,) * 3, compiler_params=_params("parallel"),
    )(w, g, m, v)


def _pair_sum(grad, recv, place, *, name):
    s_dim, r_dim, c = grad.shape
    half = r_dim // 2
    tb = min(half, 256)
    per_half = half // tb

    def body(place_ref, a_ref, b_ref, o16_ref, o32_ref):
        s = a_ref[...] + b_ref[...]
        o16_ref[...] = s.astype(BF16)

        @pl.when(pl.program_id(1) == place_ref[1])
        def _():
            o32_ref[...] = s[0]

    grid_spec = pltpu.PrefetchScalarGridSpec(
        num_scalar_prefetch=1, grid=(per_half, s_dim),
        in_specs=[pl.BlockSpec((1, tb, c), lambda i, s, p: (s, p[0] * per_half + i, 0)),
                  pl.BlockSpec((1, tb, c), lambda i, s, p: (s, i, 0))],
        out_specs=(pl.BlockSpec((1, tb, c), lambda i, s, p: (s, i, 0)),
                   pl.BlockSpec((tb, c), lambda i, s, p: (i, 0))))
    return _pcall(
        body, name=name, grid_spec=grid_spec,
        out_shape=(jax.ShapeDtypeStruct((s_dim, half, c), BF16), jax.ShapeDtypeStruct((half, c), F32)),
        compiler_params=_params("parallel", "arbitrary"),
    )(place, grad, recv)


def _chip_sum(own, recv, *, name):
    r_dim, c = own.shape
    tb = min(r_dim, 256)

    def body(a_ref, b_ref, o_ref):
        s = a_ref[...]
        for k in range(1, N_CHIPS):
            s = s + b_ref[k].astype(F32)
        o_ref[...] = s

    return _pcall(
        body, name=name, out_shape=jax.ShapeDtypeStruct((r_dim, c), F32),
        grid=(r_dim // tb,),
        in_specs=[_row_spec(tb, c), pl.BlockSpec((N_CHIPS, tb, c), lambda i: (0, i, 0))],
        out_specs=_row_spec(tb, c), compiler_params=_params("parallel"),
    )(own, recv)


def _position():
    return lax.axis_index("x"), lax.axis_index("y"), lax.axis_index("c")


def _chip_peer(x, y, k):
    return x ^ (k >> 1), y ^ (k & 1)


_ANY = pl.BlockSpec(memory_space=pl.ANY)


def _all_gather_weights(shards):
    n = len(shards)
    hops = N_CHIPS - 1

    def body(*refs):
        srcs, outs = refs[:n], refs[n:2 * n]
        send_sems, recv_sems = refs[2 * n:]
        x, y, c = _position()
        me = 2 * x + y

        def over_ici(w, k, chip, to):
            return pltpu.make_async_remote_copy(
                src_ref=srcs[w].at[c], dst_ref=outs[w].at[chip, c],
                send_sem=send_sems.at[w, k - 1], recv_sem=recv_sems.at[w, k - 1],
                device_id=to, device_id_type=MESH)

        def over_d2d(w, k, chip, half):
            return pltpu.make_async_remote_copy(
                src_ref=outs[w].at[chip, half], dst_ref=outs[w].at[chip, half],
                send_sem=send_sems.at[w, hops + k - 1], recv_sem=recv_sems.at[w, hops + k - 1],
                device_id=(x, y, 1 - c), device_id_type=MESH)

        sends = []
        for w in range(n):
            for k in range(1, N_CHIPS):
                px, py = _chip_peer(x, y, k)
                cp = over_ici(w, k, me, (px, py, c))
                cp.start()
                sends.append(cp)
        for w in range(n):
            for k in range(1, N_CHIPS):
                px, py = _chip_peer(x, y, k)
                over_ici(w, k, 2 * px + py, (px, py, c)).wait_recv()
                cp = over_d2d(w, k, 2 * px + py, c)
                cp.start()
                sends.append(cp)
        for w in range(n):
            for k in range(1, N_CHIPS):
                px, py = _chip_peer(x, y, k)
                over_d2d(w, k, 2 * px + py, 1 - c).wait_recv()
        for cp in sends:
            cp.wait_send()

    return _pcall(
        body, name="gather_weights",
        out_shape=tuple(jax.ShapeDtypeStruct((N_CHIPS,) + s.shape, s.dtype) for s in shards),
        in_specs=[_ANY] * n, out_specs=(_ANY,) * n,
        scratch_shapes=[pltpu.SemaphoreType.DMA((n, 2 * hops)),
                        pltpu.SemaphoreType.DMA((n, 2 * hops))],
    )(*shards)


def _pair_exchange_halves(grads):
    n = len(grads)

    def body(*refs):
        srcs, outs = refs[:n], refs[n:2 * n]
        send_sems, recv_sems = refs[2 * n:]
        x, y, c = _position()
        sends = []
        for w in range(n):
            half = srcs[w].shape[1] // 2
            cp = pltpu.make_async_remote_copy(
                src_ref=srcs[w].at[:, pl.ds((1 - c) * half, half), :], dst_ref=outs[w],
                send_sem=send_sems.at[w], recv_sem=recv_sems.at[w],
                device_id=(x, y, 1 - c), device_id_type=MESH)
            cp.start()
            sends.append(cp)
        for cp in sends:
            cp.wait_recv()
        for cp in sends:
            cp.wait_send()

    return _pcall(
        body, name="pair_exchange_halves",
        out_shape=tuple(jax.ShapeDtypeStruct((g.shape[0], g.shape[1] // 2, g.shape[2]), F32) for g in grads),
        in_specs=[_ANY] * n, out_specs=(_ANY,) * n,
        scratch_shapes=[pltpu.SemaphoreType.DMA((n,)), pltpu.SemaphoreType.DMA((n,))],
    )(*grads)


def _chip_exchange(parts):
    n = len(parts)

    def body(*refs):
        srcs, outs = refs[:n], refs[n:2 * n]
        send_sems, recv_sems = refs[2 * n:]
        x, y, c = _position()
        sends = []
        for w in range(n):
            for k in range(1, N_CHIPS):
                px, py = _chip_peer(x, y, k)
                cp = pltpu.make_async_remote_copy(
                    src_ref=srcs[w].at[2 * px + py], dst_ref=outs[w].at[k],
                    send_sem=send_sems.at[w, k - 1], recv_sem=recv_sems.at[w, k - 1],
                    device_id=(px, py, c), device_id_type=MESH)
                cp.start()
                sends.append(cp)
        for cp in sends:
            cp.wait_recv()
        for cp in sends:
            cp.wait_send()

    return _pcall(
        body, name="chip_exchange",
        out_shape=tuple(jax.ShapeDtypeStruct(p.shape, BF16) for p in parts),
        in_specs=[_ANY] * n, out_specs=(_ANY,) * n,
        scratch_shapes=[pltpu.SemaphoreType.DMA((n, N_CHIPS - 1)),
                        pltpu.SemaphoreType.DMA((n, N_CHIPS - 1))],
    )(*parts)


def _pair_gather_halves(halves):
    n = len(halves)

    def body(*refs):
        srcs, outs = refs[:n], refs[n:2 * n]
        send_sems, recv_sems = refs[2 * n:]
        x, y, c = _position()
        sends = []
        for w in range(n):
            cp = pltpu.make_async_remote_copy(
                src_ref=srcs[w], dst_ref=outs[w],
                send_sem=send_sems.at[w], recv_sem=recv_sems.at[w],
                device_id=(x, y, 1 - c), device_id_type=MESH)
            cp.start()
            sends.append(cp)
        for cp in sends:
            cp.wait_recv()
        for cp in sends:
            cp.wait_send()

    theirs = _pcall(
        body, name="pair_gather_halves",
        out_shape=tuple(jax.ShapeDtypeStruct(h.shape, F32) for h in halves),
        in_specs=[_ANY] * n, out_specs=(_ANY,) * n,
        scratch_shapes=[pltpu.SemaphoreType.DMA((n,)), pltpu.SemaphoreType.DMA((n,))],
    )(*halves)
    my_c = lax.axis_index("c")
    whole = []
    for mine, other in zip(halves, theirs):
        both = jnp.stack([other, other])
        both = lax.dynamic_update_index_in_dim(both, mine, my_c, axis=0)
        whole.append(both.reshape(2 * mine.shape[0], mine.shape[1]))
    return whole


def _all_reduce_small(packed, *, name, sum_row0):
    r_dim, c = packed.shape

    def body(src_ref, out_ref, recv_ref, send_sems, recv_sems):
        x, y, c_ = _position()
        me = 4 * x + 2 * y + c_
        recv_ref[0] = src_ref[...]
        sends = []
        for k in range(1, N_DEV):
            peer = (x ^ (k >> 2), y ^ ((k >> 1) & 1), c_ ^ (k & 1))
            cp = pltpu.make_async_remote_copy(
                src_ref=src_ref, dst_ref=recv_ref.at[k],
                send_sem=send_sems.at[k - 1], recv_sem=recv_sems.at[k - 1],
                device_id=peer, device_id_type=MESH)
            cp.start()
            sends.append(cp)
        for cp in sends:
            cp.wait_recv()
        total = recv_ref[me]
        for d in range(1, N_DEV):
            total = total + recv_ref[d ^ me]
        if sum_row0:
            row0 = jnp.sum(total[0:1, :], axis=1, keepdims=True)
            rid = lax.broadcasted_iota(jnp.int32, total.shape, 0)
            total = jnp.where(rid == 0, row0, total)
        out_ref[...] = total
        for cp in sends:
            cp.wait_send()

    return _pcall(
        body, name=name, out_shape=jax.ShapeDtypeStruct((r_dim, c), F32),
        in_specs=[pl.BlockSpec(memory_space=pltpu.VMEM)],
        out_specs=pl.BlockSpec(memory_space=pltpu.VMEM),
        scratch_shapes=[pltpu.VMEM((N_DEV, r_dim, c), F32),
                        pltpu.SemaphoreType.DMA((N_DEV - 1,)), pltpu.SemaphoreType.DMA((N_DEV - 1,))],
    )(packed)


def _pad_lanes(v, width):
    return jnp.pad(v, ((0, 0), (0, width - v.shape[1])))


def _rows_1024(v):
    flat = v.reshape(-1)
    pad = (-flat.shape[0]) % D_MODEL
    return jnp.pad(flat, (0, pad)).reshape(-1, D_MODEL)


def _local_step(xs, target, pw, w_z, w_xbc, w_dt, wout, w1, w2, conv_w, conv_b, gate_g,
                norm_mix_g, norm_mlp_g, pool_b, pool_scale, ssm_dt_bias, ssm_a_log, ssm_d, final_g):
    bias_r = _pad_lanes(ssm_dt_bias, 128)
    alog_r = _pad_lanes(ssm_a_log, 128)
    bias_x = jnp.repeat(ssm_dt_bias, HEAD_DIM, axis=1)
    alog_x = jnp.repeat(ssm_a_log, HEAD_DIM, axis=1)
    dskip_x = jnp.repeat(ssm_d, HEAD_DIM, axis=1)
    bias_c = ssm_dt_bias.reshape(N_HEADS, 1)
    alog_c = ssm_a_log.reshape(N_HEADS, 1)
    e_mat = _head_lane_matrix()

    g_mix0, g_mix1 = norm_mix_g[0:1], norm_mix_g[1:2]
    g_mlp0, g_mlp1 = norm_mlp_g[0:1], norm_mlp_g[1:2]
    fg = final_g.reshape(1, D_MODEL)

    h1 = _pool_fwd(xs, g_mix0, pw, pool_b, pool_scale)
    hm0 = _rmsnorm_fwd(h1, g_mlp0, name="norm_mlp0")
    u0 = _matmul(hm0, w1[0], "nn", name="mlp0_up", out_dtype=BF16)
    h2 = _matmul(u0, w2[0], "nn", name="mlp0_down", a_relu2=True, add=h1)

    hn1 = _rmsnorm_fwd(h2, g_mix1, name="norm_mix1")
    z = _matmul(hn1, w_z, "nn", name="in_proj_z", out_dtype=BF16)
    xbc = _matmul(hn1, w_xbc, "nn", name="in_proj_xbc", out_dtype=BF16)
    dt_raw = _matmul(hn1, w_dt, "nn", name="in_proj_dt")
    dt_raw_t = dt_raw[:, :N_HEADS].T
    xc = _conv_fwd(xbc, conv_w, conv_b)
    y, states = _ssd_fwd(xc, dt_raw, dt_raw_t, bias_x, bias_c, alog_x, alog_c, dskip_x, e_mat)
    yn = _gate_fwd(y, z, gate_g)
    h3 = _matmul(yn, wout, "nn", name="out_proj", add=h2)
    hm1 = _rmsnorm_fwd(h3, g_mlp1, name="norm_mlp1")
    u1 = _matmul(hm1, w1[1], "nn", name="mlp1_up", out_dtype=BF16)
    h4 = _matmul(u1, w2[1], "nn", name="mlp1_down", a_relu2=True, add=h3)

    dh4, dh4_16, small_final = _loss_head(h4, fg, target)

    def mlp_bwd(dh_out, dh_out16, h_in, hm, u, w1_i, w2_i, g_i, tag):
        du = _matmul(dh_out16, w2_i, "nt", name=tag + "_du", out_dtype=BF16, relu2_grad_of=u)
        dw2 = _matmul(u, dh_out16, "tn", name=tag + "_dw2", a_relu2=True)
        dhm = _matmul(du, w1_i, "nt", name=tag + "_dhm")
        dw1 = _matmul(hm, du, "tn", name=tag + "_dw1", out_col_shards=N_CHIPS)
        dh_in, dh_in16, dg = _rmsnorm_bwd(dhm, h_in, g_i, dh_out, name=tag + "_norm_bwd")
        return dh_in, dh_in16, dw1, dw2.reshape(N_CHIPS, D_FF // N_CHIPS, D_MODEL), dg

    dh3, dh3_16, dw1_1, dw2_1, dg_mlp1 = mlp_bwd(dh4, dh4_16, h3, hm1, u1, w1[1], w2[1], g_mlp1, "mlp1")

    dyn = _matmul(dh3_16, wout, "nt", name="out_proj_dyn")
    dwout = _matmul(yn, dh3_16, "tn", name="out_proj_dw").reshape(N_CHIPS, D_INNER // N_CHIPS, D_MODEL)
    dy, dz, dg_gate = _gate_bwd(dyn, y, z, gate_g)
    dxc, ddt_raw, small_ssd = _ssd_bwd(xc, dt_raw, dt_raw_t, bias_x, bias_c, alog_x, alog_c, dskip_x,
                                       bias_r, alog_r, e_mat, e_mat.T, states, dy)
    dv, dconv = _conv_bwd_act(xbc, dxc, conv_w, conv_b)
    dxbc = _conv_bwd_in(dv, conv_w)
    dhn1 = _matmul(ddt_raw, w_dt, "nt", name="in_proj_dt_dh")
    dhn1 = _matmul(dz, w_z, "nt", name="in_proj_z_dh", add=dhn1)
    dhn1 = _matmul(dxbc, w_xbc, "nt", name="in_proj_xbc_dh", add=dhn1)
    dw_z = _matmul(hn1, dz, "tn", name="in_proj_z_dw")
    dw_xbc = _matmul(hn1, dxbc, "tn", name="in_proj_xbc_dw")
    dw_dt = _matmul(hn1, ddt_raw, "tn", name="in_proj_dt_dw")
    dwin = jnp.concatenate([dw_z, dw_xbc, dw_dt[:, :N_HEADS]], axis=1)
    dwin = jnp.transpose(dwin.reshape(D_MODEL, N_CHIPS, IN_PROJ_DIM // N_CHIPS), (1, 0, 2))
    dh2, dh2_16, dg_mix1 = _rmsnorm_bwd(dhn1, h2, g_mix1, dh3, name="norm_mix1_bwd")

    dh1, _, dw1_0, dw2_0, dg_mlp0 = mlp_bwd(dh2, dh2_16, h1, hm0, u0, w1[0], w2[0], g_mlp0, "mlp0")
    dx, dpw, small_pool = _pool_bwd(xs, g_mix0, pw, pool_b, pool_scale, dh1)
    dpw = jnp.transpose(dpw.reshape(4, N_CHIPS, POOL_GROUP // N_CHIPS, POOL_GROUP), (1, 0, 2, 3))
    dpw = dpw.reshape(N_CHIPS, 4 * (POOL_GROUP // N_CHIPS), POOL_GROUP)

    big = [dpw, dwin, dwout, dw1_0, dw1_1, dw2_0, dw2_1]
    rows = [
        small_final[1:2],
        small_final[0:1],
        small_pool[0:1], dg_mix1[0:1],
        dg_mlp0[0:1], dg_mlp1[0:1],
        small_pool[1:2], small_pool[2:3],
        _pad_lanes(small_ssd[0:3], D_MODEL),
        _rows_1024(dg_gate[0:1]),
        _rows_1024(dconv[0:CONV_K]),
        _rows_1024(dconv[CONV_K:CONV_K + 1]),
    ]
    return dx, big, rows


def kernel(x, norm_mix_g, norm_mlp_g, pool_w, pool_b, pool_scale, ssm_w_in, ssm_conv_w, ssm_conv_b, ssm_dt_bias, ssm_a_log, ssm_d, ssm_norm_g, ssm_w_out, mlp_w1, mlp_w2, final_g, loss_target, m_norm_mix_g, m_norm_mlp_g, m_pool_w, m_pool_b, m_pool_scale, m_ssm_w_in, m_ssm_conv_w, m_ssm_conv_b, m_ssm_dt_bias, m_ssm_a_log, m_ssm_d, m_ssm_norm_g, m_ssm_w_out, m_mlp_w1, m_mlp_w2, m_final_g, v_norm_mix_g, v_norm_mlp_g, v_pool_w, v_pool_b, v_pool_scale, v_ssm_w_in, v_ssm_conv_w, v_ssm_conv_b, v_ssm_dt_bias, v_ssm_a_log, v_ssm_d, v_ssm_norm_g, v_ssm_w_out, v_mlp_w1, v_mlp_w2, v_final_g):
    t_dim = x.shape[1]
    xs = x[0]
    target = loss_target[0]
    my_x, my_y, my_c = _position()
    my_chip = 2 * my_x + my_y

    own = [
        pool_w[0].astype(BF16).reshape(2, 2, POOL_GROUP // N_CHIPS, POOL_GROUP),
        ssm_w_in[0].astype(BF16).reshape(2, D_MODEL // 2, IN_PROJ_DIM // N_CHIPS),
        ssm_w_out[0].astype(BF16).reshape(2, D_INNER // N_CHIPS // 2, D_MODEL),
        mlp_w1.astype(BF16),
        mlp_w2.astype(BF16),
    ]
    gathered = _all_gather_weights(own)
    g_pool, g_win, g_wout, g_w1, g_w2 = [
        lax.dynamic_update_index_in_dim(g, o, my_chip, axis=0) for g, o in zip(gathered, own)]
    g_pool = g_pool.reshape(N_CHIPS, 4, POOL_GROUP // N_CHIPS, POOL_GROUP)
    g_win = g_win.reshape(N_CHIPS, D_MODEL, IN_PROJ_DIM // N_CHIPS)
    g_wout = g_wout.reshape(N_CHIPS, D_INNER // N_CHIPS, D_MODEL)
    pw = jnp.transpose(g_pool, (1, 0, 2, 3)).reshape(4, POOL_GROUP, POOL_GROUP)
    win = jnp.transpose(g_win, (1, 0, 2)).reshape(D_MODEL, IN_PROJ_DIM)
    w_z, w_xbc = win[:, :D_INNER], win[:, D_INNER:D_INNER + CONV_DIM]
    w_dt = _pad_lanes(win[:, D_INNER + CONV_DIM:], 128)
    wout = g_wout.reshape(D_INNER, D_MODEL)
    w1 = [jnp.transpose(g_w1[:, i], (1, 0, 2)).reshape(D_MODEL, D_FF) for i in range(2)]
    w2 = [g_w2[:, i].reshape(D_FF, D_MODEL) for i in range(2)]
    conv_w = jnp.zeros((CONV_K, CONV_DIM), F32)
    conv_b = jnp.zeros((1, CONV_DIM), F32)
    gate_g = jnp.zeros((1, D_INNER), F32)
    conv_w = lax.dynamic_update_slice(conv_w, ssm_conv_w[0], (0, my_chip * (CONV_DIM // N_CHIPS)))
    conv_b = lax.dynamic_update_slice(conv_b, ssm_conv_b, (0, my_chip * (CONV_DIM // N_CHIPS)))
    gate_g = lax.dynamic_update_slice(gate_g, ssm_norm_g, (0, my_chip * (D_INNER // N_CHIPS)))
    vec_rows = jnp.concatenate([_rows_1024(conv_w), _rows_1024(conv_b), _rows_1024(gate_g)], axis=0)
    vec_rows = jnp.pad(vec_rows, ((0, (-vec_rows.shape[0]) % 8), (0, 0)))
    vec_rows = _all_reduce_small(vec_rows * 0.5, name="gather_vectors", sum_row0=False)
    conv_w = vec_rows[0:12].reshape(CONV_K, CONV_DIM)
    conv_b = vec_rows[12:15].reshape(1, CONV_DIM)
    gate_g = vec_rows[15:17].reshape(1, D_INNER)

    dx, big, rows = _local_step(xs, target, pw, w_z, w_xbc, w_dt, wout, w1, w2, conv_w, conv_b, gate_g,
                                norm_mix_g, norm_mlp_g, pool_b, pool_scale, ssm_dt_bias, ssm_a_log, ssm_d,
                                final_g)

    recv1 = _pair_exchange_halves(big)
    place = jnp.stack([my_c, my_chip]).astype(jnp.int32)
    chip_f32, chip_b16 = [], []
    for i, (gfull, r1) in enumerate(zip(big, recv1)):
        s16, s32 = _pair_sum(gfull, r1, place, name="pair_sum_%d" % i)
        chip_f32.append(s32)
        chip_b16.append(s16)
    recv2 = _chip_exchange(chip_b16)
    halves = [_chip_sum(o, r2, name="chip_sum_%d" % i) for i, (o, r2) in enumerate(zip(chip_f32, recv2))]
    g_pool_w, g_win_s, g_wout_s, g_w1_0, g_w1_1, g_w2_0, g_w2_1 = _pair_gather_halves(halves)

    small = jnp.concatenate(rows, axis=0)
    small = jnp.pad(small, ((0, (-small.shape[0]) % 8), (0, 0)))
    small = _all_reduce_small(small, name="all_reduce_small", sum_row0=True)
    loss = small[0, 0]
    g_final = small[1]
    g_norm_mix = small[2:4]
    g_norm_mlp = small[4:6]
    g_pool_b, g_pool_scale = small[6:7], small[7:8]
    g_alog, g_dtb, g_dsk = small[8:9, :N_HEADS], small[9:10, :N_HEADS], small[10:11, :N_HEADS]
    g_gate_full = small[11:13].reshape(1, D_INNER)
    g_convw_full = small[13:25].reshape(CONV_K, CONV_DIM)
    g_convb_full = small[25:28].reshape(1, CONV_DIM)
    g_gate = lax.dynamic_slice_in_dim(g_gate_full, my_chip * (D_INNER // N_CHIPS), D_INNER // N_CHIPS, axis=1)
    g_convw = lax.dynamic_slice_in_dim(g_convw_full, my_chip * (CONV_DIM // N_CHIPS), CONV_DIM // N_CHIPS, axis=1)
    g_convb = lax.dynamic_slice_in_dim(g_convb_full, my_chip * (CONV_DIM // N_CHIPS), CONV_DIM // N_CHIPS, axis=1)

    grads = {
        "norm_mix_g": g_norm_mix, "norm_mlp_g": g_norm_mlp,
        "pool_w": g_pool_w.reshape(pool_w.shape), "pool_b": g_pool_b, "pool_scale": g_pool_scale,
        "ssm_w_in": g_win_s.reshape(ssm_w_in.shape), "ssm_conv_w": g_convw.reshape(ssm_conv_w.shape),
        "ssm_conv_b": g_convb, "ssm_dt_bias": g_dtb, "ssm_a_log": g_alog, "ssm_d": g_dsk,
        "ssm_norm_g": g_gate, "ssm_w_out": g_wout_s.reshape(ssm_w_out.shape),
        "mlp_w1": jnp.stack([g_w1_0, g_w1_1]), "mlp_w2": jnp.stack([g_w2_0, g_w2_1]),
        "final_g": g_final,
    }
    weights = dict(norm_mix_g=norm_mix_g, norm_mlp_g=norm_mlp_g, pool_w=pool_w, pool_b=pool_b,
                   pool_scale=pool_scale, ssm_w_in=ssm_w_in, ssm_conv_w=ssm_conv_w, ssm_conv_b=ssm_conv_b,
                   ssm_dt_bias=ssm_dt_bias, ssm_a_log=ssm_a_log, ssm_d=ssm_d, ssm_norm_g=ssm_norm_g,
                   ssm_w_out=ssm_w_out, mlp_w1=mlp_w1, mlp_w2=mlp_w2, final_g=final_g)
    moms = dict(norm_mix_g=(m_norm_mix_g, v_norm_mix_g), norm_mlp_g=(m_norm_mlp_g, v_norm_mlp_g),
                pool_w=(m_pool_w, v_pool_w), pool_b=(m_pool_b, v_pool_b),
                pool_scale=(m_pool_scale, v_pool_scale), ssm_w_in=(m_ssm_w_in, v_ssm_w_in),
                ssm_conv_w=(m_ssm_conv_w, v_ssm_conv_w), ssm_conv_b=(m_ssm_conv_b, v_ssm_conv_b),
                ssm_dt_bias=(m_ssm_dt_bias, v_ssm_dt_bias), ssm_a_log=(m_ssm_a_log, v_ssm_a_log),
                ssm_d=(m_ssm_d, v_ssm_d), ssm_norm_g=(m_ssm_norm_g, v_ssm_norm_g),
                ssm_w_out=(m_ssm_w_out, v_ssm_w_out), mlp_w1=(m_mlp_w1, v_mlp_w1),
                mlp_w2=(m_mlp_w2, v_mlp_w2), final_g=(m_final_g, v_final_g))
    names = list(weights)
    big_names = ("pool_w", "ssm_w_in", "ssm_w_out", "mlp_w1", "mlp_w2")
    deltas, new_m, new_v = {}, {}, {}
    for nm in big_names:
        w = weights[nm]
        two_d = (-1, w.shape[-1])
        d_, m_, v_ = _adamw(w.reshape(two_d), grads[nm].reshape(two_d), moms[nm][0].reshape(two_d),
                            moms[nm][1].reshape(two_d), name="adamw_" + nm)
        deltas[nm], new_m[nm], new_v[nm] = d_.reshape(w.shape), m_.reshape(w.shape), v_.reshape(w.shape)
    small_names = [nm for nm in names if nm not in big_names]
    sizes = [weights[nm].size for nm in small_names]

    def pack(parts):
        flat = jnp.concatenate([p.reshape(-1) for p in parts])
        pad = (-flat.shape[0]) % (8 * D_MODEL)
        return jnp.pad(flat, (0, pad)).reshape(-1, D_MODEL)

    d_, m_, v_ = _adamw(pack([weights[nm] for nm in small_names]), pack([grads[nm] for nm in small_names]),
                        pack([moms[nm][0] for nm in small_names]), pack([moms[nm][1] for nm in small_names]),
                        name="adamw_small")
    off = 0
    for nm, sz in zip(small_names, sizes):
        shp = weights[nm].shape
        deltas[nm] = d_.reshape(-1)[off:off + sz].reshape(shp)
        new_m[nm] = m_.reshape(-1)[off:off + sz].reshape(shp)
        new_v[nm] = v_.reshape(-1)[off:off + sz].reshape(shp)
        off += sz

    grad_x = dx.reshape(x.shape)
    out_grads = [grads[nm].reshape(weights[nm].shape) for nm in names]
    return (loss, grad_x, *out_grads, *[deltas[nm] for nm in names],
            *[new_m[nm] for nm in names], *[new_v[nm] for nm in names])
```

```python
import functools

import jax
import jax.numpy as jnp
from jax import lax
from jax.experimental import pallas as pl
from jax.experimental.pallas import tpu as pltpu

F32 = jnp.float32
BF16 = jnp.bfloat16
MESH = pl.DeviceIdType.MESH

D_MODEL = 1024
RMS_EPS = 1e-5
POOL_WINDOWS = (2, 4, 8, 16)
POOL_GROUP = 256
POOL_HALO = 16
D_INNER = 2048
HEAD_DIM = 64
N_HEADS = 32
N_GROUPS = 4
HEADS_PER_GROUP = 8
D_STATE = 128
CONV_K = 4
CONV_HALO = 8
CHUNK = 128
CONV_DIM = 3072
IN_PROJ_DIM = 5152
D_FF = 4096
N_CHIPS = 4
N_DEV = 8

ADAM_LR = 0.001
ADAM_B1 = 0.9
ADAM_B2 = 0.999
ADAM_EPS = 1e-08
ADAM_WD = 0.01
ADAM_STEP = 10

VMEM_LIMIT = 56 * 1024 * 1024
NEG_INF = float("-inf")


def _pcall(body, **kw):
    return pl.pallas_call(body, **kw)


def _params(*sem):
    return pltpu.CompilerParams(dimension_semantics=sem, vmem_limit_bytes=VMEM_LIMIT)


def _sigmoid(v):
    return 1.0 / (1.0 + jnp.exp(-v))


def _row_spec(tb, d, nb=None, reverse=False):
    if reverse:
        return pl.BlockSpec((tb, d), lambda i: (nb - 1 - i, 0))
    return pl.BlockSpec((tb, d), lambda i: (i, 0))


def _const_spec(shape):
    return pl.BlockSpec(shape, lambda *_: tuple(0 for _ in shape))


_DIMS = {"nn": (((1,), (0,)), ((), ())),
         "nt": (((1,), (1,)), ((), ())),
         "tn": (((0,), (0,)), ((), ()))}


_MATMUL_VMEM_BUDGET = 40 * 1024 * 1024


def _matmul_tiles(m_dim, n_dim, k_dim, a_bytes, b_bytes, mn_bytes):
    tm, tn = min(m_dim, 1024), min(n_dim, 1024)
    while 2 * (tm * k_dim * a_bytes + tn * k_dim * b_bytes + tm * tn * mn_bytes) > _MATMUL_VMEM_BUDGET:
        if tm >= tn:
            tm //= 2
        else:
            tn //= 2
    return tm, tn


def _matmul(a, b, mode, *, name, out_dtype=F32, a_relu2=False, add=None, relu2_grad_of=None,
            out_col_shards=1):
    if mode == "tn":
        k_dim, m_dim = a.shape
    else:
        m_dim, k_dim = a.shape
    n_dim = b.shape[0] if mode == "nt" else b.shape[1]
    mn_bytes = jnp.dtype(out_dtype).itemsize
    if relu2_grad_of is not None:
        mn_bytes += relu2_grad_of.dtype.itemsize
    if add is not None:
        mn_bytes += add.dtype.itemsize
    tm, tn = _matmul_tiles(m_dim, n_dim, k_dim, a.dtype.itemsize, b.dtype.itemsize, mn_bytes)
    assert m_dim % tm == 0 and n_dim % tn == 0
    a_spec = (pl.BlockSpec((k_dim, tm), lambda i, j: (0, i)) if mode == "tn"
              else pl.BlockSpec((tm, k_dim), lambda i, j: (i, 0)))
    b_spec = (pl.BlockSpec((tn, k_dim), lambda i, j: (j, 0)) if mode == "nt"
              else pl.BlockSpec((k_dim, tn), lambda i, j: (0, j)))
    mn_spec = pl.BlockSpec((tm, tn), lambda i, j: (i, j))
    operands, in_specs = [a, b], [a_spec, b_spec]
    if relu2_grad_of is not None:
        operands.append(relu2_grad_of)
        in_specs.append(mn_spec)
    if add is not None:
        operands.append(add)
        in_specs.append(mn_spec)
    if out_col_shards == 1:
        out_shape = jax.ShapeDtypeStruct((m_dim, n_dim), out_dtype)
        out_spec = mn_spec
    else:
        n_shard = n_dim // out_col_shards
        assert n_shard % tn == 0
        per = n_shard // tn
        out_shape = jax.ShapeDtypeStruct((out_col_shards, m_dim, n_shard), out_dtype)
        out_spec = pl.BlockSpec((None, tm, tn), lambda i, j: (j // per, i, j % per))

    def body(*refs):
        a_ref, b_ref, o_ref = refs[0], refs[1], refs[-1]
        av = a_ref[...]
        if a_relu2:
            av = jnp.maximum(av, 0)
            av = av * av
        r = lax.dot_general(av.astype(BF16), b_ref[...].astype(BF16), _DIMS[mode],
                            preferred_element_type=F32)
        nxt = 2
        if relu2_grad_of is not None:
            r = r * (2.0 * jnp.maximum(refs[nxt][...].astype(F32), 0.0))
            nxt += 1
        if add is not None:
            r = r + refs[nxt][...]
        o_ref[...] = r.astype(out_dtype)

    return _pcall(
        body, name=name, out_shape=out_shape,
        grid=(m_dim // tm, n_dim // tn),
        in_specs=in_specs, out_specs=out_spec,
        compiler_params=_params("parallel", "parallel"),
    )(*operands)


def _rms(x):
    return lax.rsqrt(jnp.mean(x * x, axis=-1, keepdims=True) + RMS_EPS)


def _rmsnorm_fwd(h, g, *, name, tb=512):
    t_dim, d = h.shape

    def body(h_ref, g_ref, o_ref):
        x = h_ref[...]
        o_ref[...] = (x * _rms(x) * g_ref[...]).astype(BF16)

    return _pcall(
        body, name=name, out_shape=jax.ShapeDtypeStruct((t_dim, d), BF16),
        grid=(t_dim // tb,), in_specs=[_row_spec(tb, d), _const_spec((1, d))],
        out_specs=_row_spec(tb, d), compiler_params=_params("parallel"),
    )(h, g)


def _rmsnorm_bwd(dy, h, g, dres, *, name, tb=512):
    t_dim, d = h.shape

    def body(dy_ref, h_ref, g_ref, dres_ref, dh_ref, dh16_ref, dg_ref):
        @pl.when(pl.program_id(0) == 0)
        def _():
            dg_ref[...] = jnp.zeros_like(dg_ref)

        x = h_ref[...]
        r = _rms(x)
        xhat = x * r
        dyv = dy_ref[...]
        dxhat = dyv * g_ref[...]
        dh = dres_ref[...] + r * (dxhat - xhat * jnp.mean(dxhat * xhat, axis=-1, keepdims=True))
        dh_ref[...] = dh
        dh16_ref[...] = dh.astype(BF16)
        dg_ref[0:1, :] += jnp.sum(dyv * xhat, axis=0, keepdims=True)

    return _pcall(
        body, name=name,
        out_shape=(jax.ShapeDtypeStruct((t_dim, d), F32), jax.ShapeDtypeStruct((t_dim, d), BF16),
                   jax.ShapeDtypeStruct((8, d), F32)),
        grid=(t_dim // tb,),
        in_specs=[_row_spec(tb, d), _row_spec(tb, d), _const_spec((1, d)), _row_spec(tb, d)],
        out_specs=(_row_spec(tb, d), _row_spec(tb, d), _const_spec((8, d))),
        compiler_params=_params("arbitrary"),
    )(dy, h, g, dres)


def _pool_mixed(ext, hn, t0, tb):
    t = t0 + lax.broadcasted_iota(jnp.int32, (tb, 1), 0)
    parts = []
    for gi, w in enumerate(POOL_WINDOWS):
        lanes = slice(gi * POOL_GROUP, (gi + 1) * POOL_GROUP)
        s = ext[:, lanes]
        k = 1
        while k < w:
            s = s + pltpu.roll(s, k, 0)
            k *= 2
        cnt = jnp.minimum(t + 1, w).astype(F32)
        parts.append(s[POOL_HALO:, :] / cnt - hn[:, lanes])
    return parts


def _pool_fwd(x, g, pw, pb, ps, *, tb=512):
    t_dim, d = x.shape

    def body(x_ref, g_ref, pw_ref, pb_ref, ps_ref, o_ref, ext_ref):
        i = pl.program_id(0)

        @pl.when(i == 0)
        def _():
            ext_ref[0:POOL_HALO, :] = jnp.zeros((POOL_HALO, d), F32)

        xv = x_ref[...]
        hn = xv * _rms(xv) * g_ref[...]
        ext_ref[POOL_HALO:, :] = hn
        mixed = _pool_mixed(ext_ref[...], hn, i * tb, tb)
        for gi in range(len(POOL_WINDOWS)):
            lanes = slice(gi * POOL_GROUP, (gi + 1) * POOL_GROUP)
            out = jnp.dot(mixed[gi].astype(BF16), pw_ref[gi], preferred_element_type=F32)
            o_ref[:, lanes] = xv[:, lanes] + (out + pb_ref[:, lanes]) * ps_ref[:, lanes]
        ext_ref[0:POOL_HALO, :] = hn[tb - POOL_HALO:, :]

    return _pcall(
        body, name="pool_fwd", out_shape=jax.ShapeDtypeStruct((t_dim, d), F32),
        grid=(t_dim // tb,),
        in_specs=[_row_spec(tb, d), _const_spec((1, d)), _const_spec((4, POOL_GROUP, POOL_GROUP)),
                  _const_spec((1, d)), _const_spec((1, d))],
        out_specs=_row_spec(tb, d),
        scratch_shapes=[pltpu.VMEM((POOL_HALO + tb, d), F32)],
        compiler_params=_params("arbitrary"),
    )(x, g, pw, pb, ps)


def _pool_bwd(x, g, pw, pb, ps, dh1, *, tb=512):
    t_dim, d = x.shape
    nb = t_dim // tb
    halo_per_block = tb // POOL_HALO

    def body(x_ref, xprev_ref, g_ref, pw_ref, pb_ref, ps_ref, dh1_ref,
             dx_ref, dpw_ref, small_ref, ext_ref, dext_ref):
        i = pl.program_id(0)
        blk = nb - 1 - i

        @pl.when(i == 0)
        def _():
            dpw_ref[...] = jnp.zeros_like(dpw_ref)
            small_ref[...] = jnp.zeros_like(small_ref)
            dext_ref[tb:, :] = jnp.zeros((POOL_HALO, d), F32)

        gv = g_ref[...]
        xv = x_ref[...]
        r = _rms(xv)
        xhat = xv * r
        hn = xhat * gv
        xp = xprev_ref[...]
        hprev = xp * _rms(xp) * gv * (blk > 0).astype(F32)
        ext_ref[0:POOL_HALO, :] = hprev
        ext_ref[POOL_HALO:, :] = hn
        mixed = _pool_mixed(ext_ref[...], hn, blk * tb, tb)

        dout = dh1_ref[...]
        t = blk * tb + lax.broadcasted_iota(jnp.int32, (tb, 1), 0)
        for gi, w in enumerate(POOL_WINDOWS):
            lanes = slice(gi * POOL_GROUP, (gi + 1) * POOL_GROUP)
            mb = mixed[gi].astype(BF16)
            pre = jnp.dot(mb, pw_ref[gi], preferred_element_type=F32) + pb_ref[:, lanes]
            dg_out = dout[:, lanes]
            small_ref[2:3, lanes] += jnp.sum(dg_out * pre, axis=0, keepdims=True)
            dpre = dg_out * ps_ref[:, lanes]
            small_ref[1:2, lanes] += jnp.sum(dpre, axis=0, keepdims=True)
            dpb16 = dpre.astype(BF16)
            dpw_ref[gi] += lax.dot_general(mb, dpb16, _DIMS["tn"], preferred_element_type=F32)
            dmixed = lax.dot_general(dpb16, pw_ref[gi], _DIMS["nt"], preferred_element_type=F32)
            cnt = jnp.minimum(t + 1, w).astype(F32)
            dq = dmixed / cnt
            dext_ref[0:tb, lanes] = dq
            s = dext_ref[:, lanes]
            k = 1
            while k < w:
                s = s + pltpu.roll(s, tb + POOL_HALO - k, 0)
                k *= 2
            dhn = s[0:tb, :] - dmixed
            dext_ref[tb:, lanes] = dq[0:POOL_HALO, :]
            small_ref[0:1, lanes] += jnp.sum(dhn * xhat[:, lanes], axis=0, keepdims=True)
            ext_ref[POOL_HALO:, lanes] = dhn * gv[:, lanes]
        dxhat = ext_ref[POOL_HALO:, :]
        dx_ref[...] = dout + r * (dxhat - xhat * jnp.mean(dxhat * xhat, axis=-1, keepdims=True))

    return _pcall(
        body, name="pool_bwd",
        out_shape=(jax.ShapeDtypeStruct((t_dim, d), F32),
                   jax.ShapeDtypeStruct((4, POOL_GROUP, POOL_GROUP), F32),
                   jax.ShapeDtypeStruct((8, d), F32)),
        grid=(nb,),
        in_specs=[_row_spec(tb, d, nb, True),
                  pl.BlockSpec((POOL_HALO, d),
                               lambda i: (jnp.maximum((nb - 1 - i) * halo_per_block - 1, 0), 0)),
                  _const_spec((1, d)), _const_spec((4, POOL_GROUP, POOL_GROUP)),
                  _const_spec((1, d)), _const_spec((1, d)), _row_spec(tb, d, nb, True)],
        out_specs=(_row_spec(tb, d, nb, True), _const_spec((4, POOL_GROUP, POOL_GROUP)),
                   _const_spec((8, d))),
        scratch_shapes=[pltpu.VMEM((POOL_HALO + tb, d), F32), pltpu.VMEM((tb + POOL_HALO, d), F32)],
        compiler_params=_params("arbitrary"),
    )(x, x, g, pw, pb, ps, dh1)


_CONV_CB = 1024


def _conv_pre(e, u, w, b, tb):
    acc = u * w[CONV_K - 1:CONV_K, :] + b
    for sh in range(1, CONV_K):
        acc = acc + pltpu.roll(e, sh, 0)[CONV_HALO:, :] * w[CONV_K - 1 - sh:CONV_K - sh, :]
    return acc


def _conv_fwd(u, w, b, *, tb=512):
    t_dim, c = u.shape
    cb = _CONV_CB

    def body(u_ref, w_ref, b_ref, o_ref, ext_ref):
        @pl.when(pl.program_id(1) == 0)
        def _():
            ext_ref[0:CONV_HALO, :] = jnp.zeros((CONV_HALO, cb), F32)

        uv = u_ref[...].astype(F32)
        ext_ref[CONV_HALO:, :] = uv
        v = _conv_pre(ext_ref[...], uv, w_ref[...], b_ref[...], tb)
        o_ref[...] = (v * _sigmoid(v)).astype(BF16)
        ext_ref[0:CONV_HALO, :] = uv[tb - CONV_HALO:, :]

    blk = pl.BlockSpec((tb, cb), lambda j, t: (t, j))
    return _pcall(
        body, name="conv_fwd", out_shape=jax.ShapeDtypeStruct((t_dim, c), BF16),
        grid=(c // cb, t_dim // tb),
        in_specs=[blk, pl.BlockSpec((CONV_K, cb), lambda j, t: (0, j)),
                  pl.BlockSpec((1, cb), lambda j, t: (0, j))],
        out_specs=blk,
        scratch_shapes=[pltpu.VMEM((CONV_HALO + tb, cb), F32)],
        compiler_params=_params("parallel", "arbitrary"),
    )(u, w, b)


def _conv_bwd_act(u, dxc, w, b, *, tb=512):
    t_dim, c = u.shape
    cb = _CONV_CB

    def body(u_ref, d_ref, w_ref, b_ref, dv_ref, dwb_ref, ext_ref):
        @pl.when(pl.program_id(1) == 0)
        def _():
            ext_ref[0:CONV_HALO, :] = jnp.zeros((CONV_HALO, cb), F32)
            dwb_ref[...] = jnp.zeros_like(dwb_ref)

        uv = u_ref[...].astype(F32)
        ext_ref[CONV_HALO:, :] = uv
        e = ext_ref[...]
        v = _conv_pre(e, uv, w_ref[...], b_ref[...], tb)
        sg = _sigmoid(v)
        dv = d_ref[...].astype(F32) * (sg * (1.0 + v * (1.0 - sg)))
        dv_ref[...] = dv.astype(BF16)
        dwb_ref[CONV_K:CONV_K + 1, :] += jnp.sum(dv, axis=0, keepdims=True)
        dwb_ref[CONV_K - 1:CONV_K, :] += jnp.sum(dv * uv, axis=0, keepdims=True)
        for sh in range(1, CONV_K):
            us = pltpu.roll(e, sh, 0)[CONV_HALO:, :]
            dwb_ref[CONV_K - 1 - sh:CONV_K - sh, :] += jnp.sum(dv * us, axis=0, keepdims=True)
        ext_ref[0:CONV_HALO, :] = uv[tb - CONV_HALO:, :]

    blk = pl.BlockSpec((tb, cb), lambda j, t: (t, j))
    return _pcall(
        body, name="conv_bwd_act",
        out_shape=(jax.ShapeDtypeStruct((t_dim, c), BF16), jax.ShapeDtypeStruct((8, c), F32)),
        grid=(c // cb, t_dim // tb),
        in_specs=[blk, blk, pl.BlockSpec((CONV_K, cb), lambda j, t: (0, j)),
                  pl.BlockSpec((1, cb), lambda j, t: (0, j))],
        out_specs=(blk, pl.BlockSpec((8, cb), lambda j, t: (0, j))),
        scratch_shapes=[pltpu.VMEM((CONV_HALO + tb, cb), F32)],
        compiler_params=_params("parallel", "arbitrary"),
    )(u, dxc, w, b)


def _conv_bwd_in(dv, w, *, tb=512):
    t_dim, c = dv.shape
    cb = _CONV_CB
    nb = t_dim // tb

    def body(dv_ref, w_ref, du_ref, ext_ref):
        @pl.when(pl.program_id(1) == 0)
        def _():
            ext_ref[tb:, :] = jnp.zeros((CONV_HALO, cb), F32)

        d = dv_ref[...].astype(F32)
        ext_ref[0:tb, :] = d
        e = ext_ref[...]
        wv = w_ref[...]
        acc = d * wv[CONV_K - 1:CONV_K, :]
        for sh in range(1, CONV_K):
            acc = acc + pltpu.roll(e, tb + CONV_HALO - sh, 0)[0:tb, :] * wv[CONV_K - 1 - sh:CONV_K - sh, :]
        du_ref[...] = acc.astype(BF16)
        ext_ref[tb:, :] = d[0:CONV_HALO, :]

    blk = pl.BlockSpec((tb, cb), lambda j, t: (nb - 1 - t, j))
    return _pcall(
        body, name="conv_bwd_in", out_shape=jax.ShapeDtypeStruct((t_dim, c), BF16),
        grid=(c // cb, nb),
        in_specs=[blk, pl.BlockSpec((CONV_K, cb), lambda j, t: (0, j))],
        out_specs=blk,
        scratch_shapes=[pltpu.VMEM((tb + CONV_HALO, cb), F32)],
        compiler_params=_params("parallel", "arbitrary"),
    )(dv, w)


def _softplus(v):
    e = jnp.exp(-jnp.abs(v))
    w = 1.0 + e
    log1p = jnp.where(w == 1.0, e, jnp.log(w) * e / jnp.where(w == 1.0, 1.0, w - 1.0))
    return jnp.maximum(v, 0.0) + log1p


def _cumsum_rows(v):
    row = lax.broadcasted_iota(jnp.int32, v.shape, 0)
    k = 1
    while k < CHUNK:
        v = v + jnp.where(row >= k, pltpu.roll(v, k, 0), 0.0)
        k *= 2
    return v


def _cumsum_lanes(v):
    col = lax.broadcasted_iota(jnp.int32, v.shape, 1)
    k = 1
    while k < CHUNK:
        v = v + jnp.where(col >= k, pltpu.roll(v, k, 1), 0.0)
        k *= 2
    return v


def _rev_cumsum_rows(v):
    row = lax.broadcasted_iota(jnp.int32, v.shape, 0)
    k = 1
    while k < CHUNK:
        v = v + jnp.where(row < CHUNK - k, pltpu.roll(v, CHUNK - k, 0), 0.0)
        k *= 2
    return v


PAIR = 2 * HEAD_DIM
GROUP_LANES = HEADS_PER_GROUP * HEAD_DIM


def _head_lane_matrix():
    h = lax.broadcasted_iota(jnp.int32, (128, D_INNER), 0)
    j = lax.broadcasted_iota(jnp.int32, (128, D_INNER), 1)
    return (j // HEAD_DIM == h).astype(BF16)


def _split_bf16(v, pieces):
    out = []
    for _ in range(pieces):
        p = v.astype(BF16)
        out.append(p)
        v = v - p.astype(F32)
    return out


def _expand_heads(v, e):
    return sum(jnp.dot(p, e, preferred_element_type=F32) for p in _split_bf16(v, 3))


def _reduce_heads(v, et, pieces):
    return sum(jnp.dot(p, et, preferred_element_type=F32) for p in _split_bf16(v, pieces))


def _ssd_common(dtr_ref, dtt_ref, e_ref, bx, bc, ax, ac):
    dtx = _expand_heads(dtr_ref[...], e_ref[...])
    dt = _softplus(dtx + bx)
    a_x = -jnp.exp(ax)
    acs = _cumsum_rows(dt * a_x)
    acs_c = _cumsum_lanes(_softplus(dtt_ref[...] + bc) * (-jnp.exp(ac)))
    return dtx, dt, a_x, acs, acs_c


def _pair_decay(acs_slab, acs_c, h0, causal, left):
    other = pltpu.roll(acs_slab, HEAD_DIM, 1)
    col0 = jnp.where(left, acs_slab, other)
    col1 = jnp.where(left, other, acs_slab)
    l0 = jnp.exp(jnp.where(causal, col0 - acs_c[h0:h0 + 1, :], NEG_INF))
    l1 = jnp.exp(jnp.where(causal, col1 - acs_c[h0 + 1:h0 + 2, :], NEG_INF))
    return l0, l1


def _ssd_fwd(xc, dt_raw, dt_raw_t, bias_x, bias_c, alog_x, alog_c, dskip_x, e_mat):
    t_dim = xc.shape[0]
    nc = t_dim // CHUNK

    def body(xc_ref, dtr_ref, dtt_ref, bx_ref, bc_ref, ax_ref, ac_ref, dk_ref, e_ref,
             y_ref, st_ref, state):
        @pl.when(pl.program_id(0) == 0)
        def _():
            state[...] = jnp.zeros_like(state)

        _, dt, _, acs, acs_c = _ssd_common(dtr_ref, dtt_ref, e_ref, bx_ref[...], bc_ref[...],
                                           ax_ref[...], ac_ref[...])
        st_ref[0] = state[...]
        last = acs[CHUNK - 1:CHUNK, :]
        xs32 = xc_ref[:, 0:D_INNER].astype(F32)
        xdt = xs32 * dt
        xdt16 = xdt.astype(BF16)
        xdte16 = (xdt * jnp.exp(last - acs)).astype(BF16)
        ea = jnp.exp(acs)
        cd = jnp.exp(last)
        skip = dk_ref[...] * xs32
        causal = (lax.broadcasted_iota(jnp.int32, (CHUNK, CHUNK), 0)
                  >= lax.broadcasted_iota(jnp.int32, (CHUNK, CHUNK), 1))
        left = lax.broadcasted_iota(jnp.int32, (CHUNK, PAIR), 1) < HEAD_DIM
        for g in range(N_GROUPS):
            gl = slice(g * GROUP_LANES, (g + 1) * GROUP_LANES)
            bg = xc_ref[:, D_INNER + g * D_STATE:D_INNER + (g + 1) * D_STATE]
            cg = xc_ref[:, D_INNER + (N_GROUPS + g) * D_STATE:D_INNER + (N_GROUPS + g + 1) * D_STATE]
            cb = lax.dot_general(cg, bg, _DIMS["nt"], preferred_element_type=F32)
            hprev = state[:, gl]
            ch = jnp.dot(cg, hprev.astype(BF16), preferred_element_type=F32)
            for j in range(HEADS_PER_GROUP // 2):
                pl_ = slice(g * GROUP_LANES + j * PAIR, g * GROUP_LANES + (j + 1) * PAIR)
                h0 = g * HEADS_PER_GROUP + 2 * j
                l0, l1 = _pair_decay(acs[:, pl_], acs_c, h0, causal, left)
                lhs = jnp.concatenate([(cb * l0).astype(BF16), (cb * l1).astype(BF16)], axis=1)
                xp = xdt16[:, pl_]
                zero = jnp.zeros_like(xp)
                rhs = jnp.concatenate([jnp.where(left, xp, zero), jnp.where(left, zero, xp)], axis=0)
                ydiag = jnp.dot(lhs, rhs, preferred_element_type=F32)
                y_ref[:, pl_] = ydiag + ch[:, j * PAIR:(j + 1) * PAIR] * ea[:, pl_] + skip[:, pl_]
            s_new = lax.dot_general(bg, xdte16[:, gl], _DIMS["tn"], preferred_element_type=F32)
            state[:, gl] = hprev * cd[:, gl] + s_new

    rows = lambda w: pl.BlockSpec((CHUNK, w), lambda c: (c, 0))
    return _pcall(
        body, name="ssd_fwd",
        out_shape=(jax.ShapeDtypeStruct((t_dim, D_INNER), F32),
                   jax.ShapeDtypeStruct((nc, D_STATE, D_INNER), F32)),
        grid=(nc,),
        in_specs=[rows(CONV_DIM), rows(128), pl.BlockSpec((N_HEADS, CHUNK), lambda c: (0, c)),
                  _const_spec((1, D_INNER)), _const_spec((N_HEADS, 1)),
                  _const_spec((1, D_INNER)), _const_spec((N_HEADS, 1)), _const_spec((1, D_INNER)),
                  _const_spec((128, D_INNER))],
        out_specs=(rows(D_INNER), pl.BlockSpec((1, D_STATE, D_INNER), lambda c: (c, 0, 0))),
        scratch_shapes=[pltpu.VMEM((D_STATE, D_INNER), F32)],
        compiler_params=_params("arbitrary"),
    )(xc, dt_raw, dt_raw_t, bias_x, bias_c, alog_x, alog_c, dskip_x, e_mat)


def _ssd_bwd(xc, dt_raw, dt_raw_t, bias_x, bias_c, alog_x, alog_c, dskip_x, bias_r, alog_r,
             e_mat, et_mat, states, dy):
    t_dim = xc.shape[0]
    nc = t_dim // CHUNK

    def body(xc_ref, dtr_ref, dtt_ref, bx_ref, bc_ref, ax_ref, ac_ref, dk_ref, br_ref, ar_ref,
             e_ref, et_ref, st_ref, dy_ref,
             dxc_ref, ddt_ref, small_ref, dstate, dacs_ref, dxdt_ref, acc_x, acc_r):
        step = pl.program_id(0)

        @pl.when(step == 0)
        def _():
            dstate[...] = jnp.zeros_like(dstate)
            acc_x[...] = jnp.zeros_like(acc_x)
            acc_r[...] = jnp.zeros_like(acc_r)

        dtx, dt, a_x, acs, acs_c = _ssd_common(dtr_ref, dtt_ref, e_ref, bx_ref[...], bc_ref[...],
                                               ax_ref[...], ac_ref[...])
        last = acs[CHUNK - 1:CHUNK, :]
        xs32 = xc_ref[:, 0:D_INNER].astype(F32)
        xdt = xs32 * dt
        xdt16 = xdt.astype(BF16)
        dte = jnp.exp(last - acs)
        xdte = xdt * dte
        xdte16 = xdte.astype(BF16)
        cd = jnp.exp(last)
        dyv = dy_ref[...]
        dy16 = dyv.astype(BF16)
        dye = dyv * jnp.exp(acs)
        dye16 = dye.astype(BF16)
        causal = (lax.broadcasted_iota(jnp.int32, (CHUNK, CHUNK), 0)
                  >= lax.broadcasted_iota(jnp.int32, (CHUNK, CHUNK), 1))
        left = lax.broadcasted_iota(jnp.int32, (CHUNK, PAIR), 1) < HEAD_DIM
        lane_id = lax.broadcasted_iota(jnp.int32, (CHUNK, 128), 1)
        row_id = lax.broadcasted_iota(jnp.int32, (CHUNK, 128), 0)
        is_last_row = lax.broadcasted_iota(jnp.int32, (CHUNK, 1), 0) == CHUNK - 1
        dacs_cols = jnp.zeros((CHUNK, 128), F32)
        dacs_rows = jnp.zeros((CHUNK, 128), F32)
        for g in range(N_GROUPS):
            gl = slice(g * GROUP_LANES, (g + 1) * GROUP_LANES)
            b_lanes = slice(D_INNER + g * D_STATE, D_INNER + (g + 1) * D_STATE)
            c_lanes = slice(D_INNER + (N_GROUPS + g) * D_STATE, D_INNER + (N_GROUPS + g + 1) * D_STATE)
            bg = xc_ref[:, b_lanes]
            cg = xc_ref[:, c_lanes]
            cb = lax.dot_general(cg, bg, _DIMS["nt"], preferred_element_type=F32)
            hprev = st_ref[0, :, gl]
            hp16 = hprev.astype(BF16)
            dhn = dstate[:, gl]
            dhn16 = dhn.astype(BF16)
            ch = jnp.dot(cg, hp16, preferred_element_type=F32)
            gmat = jnp.dot(bg, dhn16, preferred_element_type=F32)
            gx = gmat * xdte[:, gl]
            dlast = jnp.sum(gx, axis=0, keepdims=True) + cd[:, gl] * jnp.sum(dhn * hprev, axis=0, keepdims=True)
            dacs_ref[:, gl] = dye[:, gl] * ch - gx + jnp.where(is_last_row, dlast, 0.0)
            dc_acc = lax.dot_general(dye16[:, gl], hp16, _DIMS["nt"], preferred_element_type=F32)
            db_acc = lax.dot_general(xdte16[:, gl], dhn16, _DIMS["nt"], preferred_element_type=F32)
            dstate[:, gl] = dhn * cd[:, gl] + lax.dot_general(cg, dye16[:, gl], _DIMS["tn"],
                                                             preferred_element_type=F32)
            dcb = jnp.zeros((CHUNK, CHUNK), F32)
            for j in range(HEADS_PER_GROUP // 2):
                pl_ = slice(g * GROUP_LANES + j * PAIR, g * GROUP_LANES + (j + 1) * PAIR)
                h0 = g * HEADS_PER_GROUP + 2 * j
                l0, l1 = _pair_decay(acs[:, pl_], acs_c, h0, causal, left)
                m0, m1 = cb * l0, cb * l1
                lhs = jnp.concatenate([m0.astype(BF16), m1.astype(BF16)], axis=1)
                dyp = dy16[:, pl_]
                zero = jnp.zeros_like(dyp)
                both = lax.dot_general(lhs, dyp, _DIMS["tn"], preferred_element_type=F32)
                dxdt_ref[:, pl_] = (jnp.where(left, both[0:CHUNK, :], both[CHUNK:, :])
                                    + gmat[:, j * PAIR:(j + 1) * PAIR] * dte[:, pl_])
                lhs2 = jnp.concatenate([jnp.where(left, dyp, zero), jnp.where(left, zero, dyp)], axis=0)
                dm = lax.dot_general(lhs2, xdt16[:, pl_], _DIMS["nt"], preferred_element_type=F32)
                dm0, dm1 = dm[0:CHUNK, :], dm[CHUNK:, :]
                dcb = dcb + dm0 * l0 + dm1 * l1
                ds0, ds1 = dm0 * m0, dm1 * m1
                dacs_cols = jnp.where(lane_id == h0, jnp.sum(ds0, axis=1, keepdims=True), dacs_cols)
                dacs_cols = jnp.where(lane_id == h0 + 1, jnp.sum(ds1, axis=1, keepdims=True), dacs_cols)
                dacs_rows = jnp.where(row_id == h0, jnp.sum(ds0, axis=0, keepdims=True), dacs_rows)
                dacs_rows = jnp.where(row_id == h0 + 1, jnp.sum(ds1, axis=0, keepdims=True), dacs_rows)
            dcb16 = dcb.astype(BF16)
            dxc_ref[:, c_lanes] = (dc_acc + jnp.dot(dcb16, bg, preferred_element_type=F32)).astype(BF16)
            dxc_ref[:, b_lanes] = (db_acc + lax.dot_general(dcb16, cg, _DIMS["tn"],
                                                           preferred_element_type=F32)).astype(BF16)
        dxdt = dxdt_ref[...]
        dxc_ref[:, 0:D_INNER] = (dxdt * dt + dk_ref[...] * dyv).astype(BF16)
        dadt = _rev_cumsum_rows(dacs_ref[...])
        ddraw_x = (dxdt * xs32 + dadt * a_x) * _sigmoid(dtx + bx_ref[...])
        acc_x[0:1, :] += jnp.sum(dadt * dt, axis=0, keepdims=True) * a_x
        acc_x[1:2, :] += jnp.sum(ddraw_x, axis=0, keepdims=True)
        acc_x[2:3, :] += jnp.sum(dyv * xs32, axis=0, keepdims=True)
        a_r = -jnp.exp(ar_ref[...])
        pre_r = dtr_ref[...] + br_ref[...]
        dadt_r = _rev_cumsum_rows(dacs_cols - dacs_rows.T)
        ddraw_r = jnp.where(lane_id < N_HEADS, dadt_r * a_r * _sigmoid(pre_r), 0.0)
        acc_r[0:1, :] += jnp.where(lane_id[0:1, :] < N_HEADS,
                                   jnp.sum(dadt_r * _softplus(pre_r), axis=0, keepdims=True) * a_r, 0.0)
        acc_r[1:2, :] += jnp.sum(ddraw_r, axis=0, keepdims=True)
        ddt_ref[...] = ddraw_r + _reduce_heads(ddraw_x, et_ref[...], 2)

        @pl.when(step == nc - 1)
        def _():
            small_ref[...] = acc_r[...] + _reduce_heads(acc_x[...], et_ref[...], 3)

    rev = lambda w: pl.BlockSpec((CHUNK, w), lambda c: (nc - 1 - c, 0))
    return _pcall(
        body, name="ssd_bwd",
        out_shape=(jax.ShapeDtypeStruct((t_dim, CONV_DIM), BF16),
                   jax.ShapeDtypeStruct((t_dim, 128), F32),
                   jax.ShapeDtypeStruct((8, 128), F32)),
        grid=(nc,),
        in_specs=[rev(CONV_DIM), rev(128), pl.BlockSpec((N_HEADS, CHUNK), lambda c: (0, nc - 1 - c)),
                  _const_spec((1, D_INNER)), _const_spec((N_HEADS, 1)),
                  _const_spec((1, D_INNER)), _const_spec((N_HEADS, 1)), _const_spec((1, D_INNER)),
                  _const_spec((1, 128)), _const_spec((1, 128)),
                  _const_spec((128, D_INNER)), _const_spec((D_INNER, 128)),
                  pl.BlockSpec((1, D_STATE, D_INNER), lambda c: (nc - 1 - c, 0, 0)),
                  rev(D_INNER)],
        out_specs=(rev(CONV_DIM), rev(128), _const_spec((8, 128))),
        scratch_shapes=[pltpu.VMEM((D_STATE, D_INNER), F32), pltpu.VMEM((CHUNK, D_INNER), F32),
                        pltpu.VMEM((CHUNK, D_INNER), F32), pltpu.VMEM((8, D_INNER), F32),
                        pltpu.VMEM((8, 128), F32)],
        compiler_params=_params("arbitrary"),
    )(xc, dt_raw, dt_raw_t, bias_x, bias_c, alog_x, alog_c, dskip_x, bias_r, alog_r,
      e_mat, et_mat, states, dy)


_GATE_GROUP = D_INNER // N_GROUPS


def _gate_fwd(y, z, g, *, tb=256):
    t_dim = y.shape[0]

    def body(y_ref, z_ref, g_ref, o_ref):
        for gi in range(N_GROUPS):
            lanes = slice(gi * _GATE_GROUP, (gi + 1) * _GATE_GROUP)
            zv = z_ref[:, lanes].astype(F32)
            wv = y_ref[:, lanes] * (zv * _sigmoid(zv))
            o_ref[:, lanes] = (wv * _rms(wv) * g_ref[:, lanes]).astype(BF16)

    return _pcall(
        body, name="gate_fwd", out_shape=jax.ShapeDtypeStruct((t_dim, D_INNER), BF16),
        grid=(t_dim // tb,),
        in_specs=[_row_spec(tb, D_INNER), _row_spec(tb, D_INNER), _const_spec((1, D_INNER))],
        out_specs=_row_spec(tb, D_INNER), compiler_params=_params("parallel"),
    )(y, z, g)


def _gate_bwd(dyn, y, z, g, *, tb=256):
    t_dim = y.shape[0]

    def body(d_ref, y_ref, z_ref, g_ref, dy_ref, dz_ref, dg_ref):
        @pl.when(pl.program_id(0) == 0)
        def _():
            dg_ref[...] = jnp.zeros_like(dg_ref)

        for gi in range(N_GROUPS):
            lanes = slice(gi * _GATE_GROUP, (gi + 1) * _GATE_GROUP)
            zv = z_ref[:, lanes].astype(F32)
            sg = _sigmoid(zv)
            sz = zv * sg
            yv = y_ref[:, lanes]
            wv = yv * sz
            r = _rms(wv)
            what = wv * r
            dv = d_ref[:, lanes]
            dwhat = dv * g_ref[:, lanes]
            dw = r * (dwhat - what * jnp.mean(dwhat * what, axis=-1, keepdims=True))
            dg_ref[0:1, lanes] += jnp.sum(dv * what, axis=0, keepdims=True)
            dy_ref[:, lanes] = dw * sz
            dz_ref[:, lanes] = (dw * yv * (sg * (1.0 + zv * (1.0 - sg)))).astype(BF16)

    return _pcall(
        body, name="gate_bwd",
        out_shape=(jax.ShapeDtypeStruct((t_dim, D_INNER), F32),
                   jax.ShapeDtypeStruct((t_dim, D_INNER), BF16),
                   jax.ShapeDtypeStruct((8, D_INNER), F32)),
        grid=(t_dim // tb,),
        in_specs=[_row_spec(tb, D_INNER), _row_spec(tb, D_INNER), _row_spec(tb, D_INNER),
                  _const_spec((1, D_INNER))],
        out_specs=(_row_spec(tb, D_INNER), _row_spec(tb, D_INNER), _const_spec((8, D_INNER))),
        compiler_params=_params("arbitrary"),
    )(dyn, y, z, g)


def _loss_head(h, g, target, *, tb=512):
    t_dim, d = h.shape

    def body(h_ref, g_ref, t_ref, dh_ref, dh16_ref, small_ref):
        @pl.when(pl.program_id(0) == 0)
        def _():
            small_ref[...] = jnp.zeros_like(small_ref)

        x = h_ref[...]
        r = _rms(x)
        xhat = x * r
        gv = g_ref[...]
        err = xhat * gv - t_ref[...]
        small_ref[1:2, :] += (0.5 / d) * jnp.sum(err * err, axis=0, keepdims=True)
        dyv = err * (1.0 / d)
        dxhat = dyv * gv
        dh = r * (dxhat - xhat * jnp.mean(dxhat * xhat, axis=-1, keepdims=True))
        dh_ref[...] = dh
        dh16_ref[...] = dh.astype(BF16)
        small_ref[0:1, :] += jnp.sum(dyv * xhat, axis=0, keepdims=True)

    return _pcall(
        body, name="loss_head",
        out_shape=(jax.ShapeDtypeStruct((t_dim, d), F32), jax.ShapeDtypeStruct((t_dim, d), BF16),
                   jax.ShapeDtypeStruct((8, d), F32)),
        grid=(t_dim // tb,),
        in_specs=[_row_spec(tb, d), _const_spec((1, d)), _row_spec(tb, d)],
        out_specs=(_row_spec(tb, d), _row_spec(tb, d), _const_spec((8, d))),
        compiler_params=_params("arbitrary"),
    )(h, g, target)


def _adamw(w, g, m, v, *, name):
    r_dim, c = w.shape
    tb = r_dim
    for cand in (512, 256, 128, 64, 32, 16, 8):
        if r_dim % cand == 0:
            tb = cand
            break
    c1 = 1.0 / (1.0 - ADAM_B1 ** ADAM_STEP)
    c2 = 1.0 / (1.0 - ADAM_B2 ** ADAM_STEP)

    def body(w_ref, g_ref, m_ref, v_ref, d_ref, mo_ref, vo_ref):
        gv = g_ref[...]
        mn = ADAM_B1 * m_ref[...] + (1.0 - ADAM_B1) * gv
        vn = ADAM_B2 * v_ref[...] + (1.0 - ADAM_B2) * (gv * gv)
        mo_ref[...] = mn
        vo_ref[...] = vn
        d_ref[...] = -ADAM_LR * ((mn * c1) / (jnp.sqrt(vn * c2) + ADAM_EPS) + ADAM_WD * w_ref[...])

    spec = _row_spec(tb, c)
    sds = jax.ShapeDtypeStruct((r_dim, c), F32)
    return _pcall(
        body, name=name, out_shape=(sds, sds, sds), grid=(r_dim // tb,),
        in_specs=[spec] * 4, out_specs=(spec,) * 3, compiler_params=_params("parallel"),
    )(w, g, m, v)


def _pair_sum(grad, recv, place, *, name):
    s_dim, r_dim, c = grad.shape
    half = r_dim // 2
    tb = min(half, 256)
    per_half = half // tb

    def body(place_ref, a_ref, b_ref, o16_ref, o32_ref):
        s = a_ref[...] + b_ref[...]
        o16_ref[...] = s.astype(BF16)

        @pl.when(pl.program_id(1) == place_ref[1])
        def _():
            o32_ref[...] = s[0]

    grid_spec = pltpu.PrefetchScalarGridSpec(
        num_scalar_prefetch=1, grid=(per_half, s_dim),
        in_specs=[pl.BlockSpec((1, tb, c), lambda i, s, p: (s, p[0] * per_half + i, 0)),
                  pl.BlockSpec((1, tb, c), lambda i, s, p: (s, i, 0))],
        out_specs=(pl.BlockSpec((1, tb, c), lambda i, s, p: (s, i, 0)),
                   pl.BlockSpec((tb, c), lambda i, s, p: (i, 0))))
    return _pcall(
        body, name=name, grid_spec=grid_spec,
        out_shape=(jax.ShapeDtypeStruct((s_dim, half, c), BF16), jax.ShapeDtypeStruct((half, c), F32)),
        compiler_params=_params("parallel", "arbitrary"),
    )(place, grad, recv)


def _chip_sum(own, recv, *, name):
    r_dim, c = own.shape
    tb = min(r_dim, 256)

    def body(a_ref, b_ref, o_ref):
        s = a_ref[...]
        for k in range(1, N_CHIPS):
            s = s + b_ref[k].astype(F32)
        o_ref[...] = s

    return _pcall(
        body, name=name, out_shape=jax.ShapeDtypeStruct((r_dim, c), F32),
        grid=(r_dim // tb,),
        in_specs=[_row_spec(tb, c), pl.BlockSpec((N_CHIPS, tb, c), lambda i: (0, i, 0))],
        out_specs=_row_spec(tb, c), compiler_params=_params("parallel"),
    )(own, recv)


def _position():
    return lax.axis_index("x"), lax.axis_index("y"), lax.axis_index("c")


def _chip_peer(x, y, k):
    return x ^ (k >> 1), y ^ (k & 1)


_ANY = pl.BlockSpec(memory_space=pl.ANY)


def _all_gather_weights(shards):
    n = len(shards)
    hops = N_CHIPS - 1

    def body(*refs):
        srcs, outs = refs[:n], refs[n:2 * n]
        send_sems, recv_sems = refs[2 * n:]
        x, y, c = _position()
        me = 2 * x + y

        def over_ici(w, k, chip, to):
            return pltpu.make_async_remote_copy(
                src_ref=srcs[w].at[c], dst_ref=outs[w].at[chip, c],
                send_sem=send_sems.at[w, k - 1], recv_sem=recv_sems.at[w, k - 1],
                device_id=to, device_id_type=MESH)

        def over_d2d(w, k, chip, half):
            return pltpu.make_async_remote_copy(
                src_ref=outs[w].at[chip, half], dst_ref=outs[w].at[chip, half],
                send_sem=send_sems.at[w, hops + k - 1], recv_sem=recv_sems.at[w, hops + k - 1],
                device_id=(x, y, 1 - c), device_id_type=MESH)

        sends = []
        for w in range(n):
            for k in range(1, N_CHIPS):
                px, py = _chip_peer(x, y, k)
                cp = over_ici(w, k, me, (px, py, c))
                cp.start()
                sends.append(cp)
        for w in range(n):
            for k in range(1, N_CHIPS):
                px, py = _chip_peer(x, y, k)
                over_ici(w, k, 2 * px + py, (px, py, c)).wait_recv()
                cp = over_d2d(w, k, 2 * px + py, c)
                cp.start()
                sends.append(cp)
        for w in range(n):
            for k in range(1, N_CHIPS):
                px, py = _chip_peer(x, y, k)
                over_d2d(w, k, 2 * px + py, 1 - c).wait_recv()
        for cp in sends:
            cp.wait_send()

    return _pcall(
        body, name="gather_weights",
        out_shape=tuple(jax.ShapeDtypeStruct((N_CHIPS,) + s.shape, s.dtype) for s in shards),
        in_specs=[_ANY] * n, out_specs=(_ANY,) * n,
        scratch_shapes=[pltpu.SemaphoreType.DMA((n, 2 * hops)),
                        pltpu.SemaphoreType.DMA((n, 2 * hops))],
    )(*shards)


def _pair_copies(srcs, lands, send_sems, recv_sems):
    x, y, c = _position()
    copies = []
    for w in range(len(srcs)):
        half = srcs[w].shape[1] // 2
        copies.append(pltpu.make_async_remote_copy(
            src_ref=srcs[w].at[:, pl.ds((1 - c) * half, half), :], dst_ref=lands[w],
            send_sem=send_sems.at[w], recv_sem=recv_sems.at[w],
            device_id=(x, y, 1 - c), device_id_type=MESH))
    return copies


def _chip_copies(srcs, lands, send_sems, recv_sems):
    x, y, c = _position()
    copies = []
    for w in range(len(srcs)):
        for k in range(1, N_CHIPS):
            px, py = _chip_peer(x, y, k)
            i = w * (N_CHIPS - 1) + k - 1
            copies.append(pltpu.make_async_remote_copy(
                src_ref=srcs[w].at[2 * px + py], dst_ref=lands[w].at[k],
                send_sem=send_sems.at[i], recv_sem=recv_sems.at[i],
                device_id=(px, py, c), device_id_type=MESH))
    return copies


def _gather_copies(srcs, lands, send_sems, recv_sems):
    x, y, c = _position()
    me = 2 * x + y
    copies = []
    for w in range(len(srcs)):
        for k in range(1, N_CHIPS):
            px, py = _chip_peer(x, y, k)
            i = w * (N_CHIPS - 1) + k - 1
            copies.append(pltpu.make_async_remote_copy(
                src_ref=srcs[w].at[c], dst_ref=lands[w].at[me, c],
                send_sem=send_sems.at[i], recv_sem=recv_sems.at[i],
                device_id=(px, py, c), device_id_type=MESH))
    return copies


def _exchange(name, copies_of, n_copies, srcs, land_shapes):
    n = len(srcs)

    def body(*refs):
        copies = copies_of(refs[:n], refs[n:2 * n], refs[2 * n], refs[2 * n + 1])
        for cp in copies:
            cp.start()
        for cp in copies:
            cp.wait_recv()
        for cp in copies:
            cp.wait_send()

    return _pcall(
        body, name=name, out_shape=tuple(land_shapes),
        in_specs=[_ANY] * n, out_specs=(_ANY,) * n,
        scratch_shapes=[pltpu.SemaphoreType.DMA((n_copies,)), pltpu.SemaphoreType.DMA((n_copies,))],
    )(*srcs)


_HBM = pl.BlockSpec(memory_space=pltpu.HBM)
_SEM = pl.BlockSpec(memory_space=pltpu.SEMAPHORE)
_DATAFLOW = pltpu.SideEffectType.DATAFLOW_SIDE_EFFECTING


def _exchange_start(name, copies_of, n_copies, srcs, land_shapes):
    n = len(srcs)
    lands = [lax.empty(s.shape, s.dtype) for s in land_shapes]

    def body(*refs):
        for cp in copies_of(refs[:n], refs[n:2 * n], refs[2 * n], refs[2 * n + 1]):
            cp.start()
        refs[-1][...] = jnp.zeros_like(refs[-1])

    through = [pltpu.HBM(a.shape, a.dtype) for a in list(srcs) + lands]
    outs = _pcall(
        body, name=name,
        out_shape=(pltpu.SemaphoreType.DMA((n_copies,)), pltpu.SemaphoreType.DMA((n_copies,)),
                   *through, jax.ShapeDtypeStruct((8, 128), F32)),
        in_specs=[_HBM] * (2 * n),
        out_specs=(_SEM, _SEM, *([_HBM] * (2 * n)), pl.BlockSpec(memory_space=pltpu.VMEM)),
        input_output_aliases={i: 2 + i for i in range(2 * n)},
        compiler_params=pltpu.CompilerParams(has_side_effects=_DATAFLOW),
    )(*[pltpu.with_memory_space_constraint(a, pltpu.HBM) for a in list(srcs) + lands])
    return outs[:-1], outs[-1][0, 0]


def _exchange_wait(name, copies_of, state, after):
    send_sems, recv_sems, through = state[0], state[1], state[2:]
    n = len(through) // 2

    def body(*refs):
        for cp in copies_of(refs[:n], refs[n:2 * n], refs[2 * n], refs[2 * n + 1]):
            cp.wait_send()
            cp.wait_recv()

    outs = _pcall(
        body, name=name,
        out_shape=tuple(pltpu.HBM(a.shape, a.dtype) for a in through),
        in_specs=[_HBM] * (2 * n) + [_SEM, _SEM, _ANY], out_specs=tuple([_HBM] * (2 * n)),
        input_output_aliases={i: i for i in range(2 * n)},
        compiler_params=pltpu.CompilerParams(has_side_effects=_DATAFLOW),
    )(*through, send_sems, recv_sems, after)
    return outs[:n], outs[n:]


def _forward_halves(lands):
    n = len(lands)
    hops = N_CHIPS - 1

    def body(*refs):
        ins, outs = refs[:n], refs[n:2 * n]
        send_sems, recv_sems = refs[2 * n], refs[2 * n + 1]
        x, y, c = _position()
        copies = []
        for w in range(n):
            for k in range(1, N_CHIPS):
                px, py = _chip_peer(x, y, k)
                i = w * hops + k - 1
                copies.append(pltpu.make_async_remote_copy(
                    src_ref=ins[w].at[2 * px + py, c], dst_ref=outs[w].at[2 * px + py, c],
                    send_sem=send_sems.at[i], recv_sem=recv_sems.at[i],
                    device_id=(x, y, 1 - c), device_id_type=MESH))
        for cp in copies:
            cp.start()
        for cp in copies:
            cp.wait_recv()
        for cp in copies:
            cp.wait_send()

    return _pcall(
        body, name="forward_halves",
        out_shape=tuple(jax.ShapeDtypeStruct(a.shape, a.dtype) for a in lands),
        in_specs=[_ANY] * n, out_specs=(_ANY,) * n,
        input_output_aliases={i: i for i in range(n)},
        scratch_shapes=[pltpu.SemaphoreType.DMA((n * hops,)), pltpu.SemaphoreType.DMA((n * hops,))],
    )(*lands)


def _pair_lands(grads):
    return [jax.ShapeDtypeStruct((g.shape[0], g.shape[1] // 2, g.shape[2]), F32) for g in grads]


def _same_lands(parts):
    return [jax.ShapeDtypeStruct(p.shape, p.dtype) for p in parts]


def _pair_gather_halves(halves):
    n = len(halves)

    def body(*refs):
        srcs, outs = refs[:n], refs[n:2 * n]
        send_sems, recv_sems = refs[2 * n:]
        x, y, c = _position()
        sends = []
        for w in range(n):
            cp = pltpu.make_async_remote_copy(
                src_ref=srcs[w], dst_ref=outs[w],
                send_sem=send_sems.at[w], recv_sem=recv_sems.at[w],
                device_id=(x, y, 1 - c), device_id_type=MESH)
            cp.start()
            sends.append(cp)
        for cp in sends:
            cp.wait_recv()
        for cp in sends:
            cp.wait_send()

    theirs = _pcall(
        body, name="pair_gather_halves",
        out_shape=tuple(jax.ShapeDtypeStruct(h.shape, F32) for h in halves),
        in_specs=[_ANY] * n, out_specs=(_ANY,) * n,
        scratch_shapes=[pltpu.SemaphoreType.DMA((n,)), pltpu.SemaphoreType.DMA((n,))],
    )(*halves)
    my_c = lax.axis_index("c")
    whole = []
    for mine, other in zip(halves, theirs):
        both = jnp.stack([other, other])
        both = lax.dynamic_update_index_in_dim(both, mine, my_c, axis=0)
        whole.append(both.reshape(2 * mine.shape[0], mine.shape[1]))
    return whole


def _all_reduce_small(packed, *, name, sum_row0):
    r_dim, c = packed.shape

    def body(src_ref, out_ref, recv_ref, send_sems, recv_sems):
        x, y, c_ = _position()
        me = 4 * x + 2 * y + c_
        recv_ref[0] = src_ref[...]
        sends = []
        for k in range(1, N_DEV):
            peer = (x ^ (k >> 2), y ^ ((k >> 1) & 1), c_ ^ (k & 1))
            cp = pltpu.make_async_remote_copy(
                src_ref=src_ref, dst_ref=recv_ref.at[k],
                send_sem=send_sems.at[k - 1], recv_sem=recv_sems.at[k - 1],
                device_id=peer, device_id_type=MESH)
            cp.start()
            sends.append(cp)
        for cp in sends:
            cp.wait_recv()
        total = recv_ref[me]
        for d in range(1, N_DEV):
            total = total + recv_ref[d ^ me]
        if sum_row0:
            row0 = jnp.sum(total[0:1, :], axis=1, keepdims=True)
            rid = lax.broadcasted_iota(jnp.int32, total.shape, 0)
            total = jnp.where(rid == 0, row0, total)
        out_ref[...] = total
        for cp in sends:
            cp.wait_send()

    return _pcall(
        body, name=name, out_shape=jax.ShapeDtypeStruct((r_dim, c), F32),
        in_specs=[pl.BlockSpec(memory_space=pltpu.VMEM)],
        out_specs=pl.BlockSpec(memory_space=pltpu.VMEM),
        scratch_shapes=[pltpu.VMEM((N_DEV, r_dim, c), F32),
                        pltpu.SemaphoreType.DMA((N_DEV - 1,)), pltpu.SemaphoreType.DMA((N_DEV - 1,))],
    )(packed)


def _pad_lanes(v, width):
    return jnp.pad(v, ((0, 0), (0, width - v.shape[1])))


def _rows_1024(v):
    flat = v.reshape(-1)
    pad = (-flat.shape[0]) % D_MODEL
    return jnp.pad(flat, (0, pad)).reshape(-1, D_MODEL)


def _local_step(xs, target, pw, w1_0, w2_0, late_weights, layer1_grads_ready, layer1_grads_midway,
                conv_w, conv_b, gate_g,
                norm_mix_g, norm_mlp_g, pool_b, pool_scale, ssm_dt_bias, ssm_a_log, ssm_d, final_g):
    bias_r = _pad_lanes(ssm_dt_bias, 128)
    alog_r = _pad_lanes(ssm_a_log, 128)
    bias_x = jnp.repeat(ssm_dt_bias, HEAD_DIM, axis=1)
    alog_x = jnp.repeat(ssm_a_log, HEAD_DIM, axis=1)
    dskip_x = jnp.repeat(ssm_d, HEAD_DIM, axis=1)
    bias_c = ssm_dt_bias.reshape(N_HEADS, 1)
    alog_c = ssm_a_log.reshape(N_HEADS, 1)
    e_mat = _head_lane_matrix()

    g_mix0, g_mix1 = norm_mix_g[0:1], norm_mix_g[1:2]
    g_mlp0, g_mlp1 = norm_mlp_g[0:1], norm_mlp_g[1:2]
    fg = final_g.reshape(1, D_MODEL)

    h1 = _pool_fwd(xs, g_mix0, pw, pool_b, pool_scale)
    hm0 = _rmsnorm_fwd(h1, g_mlp0, name="norm_mlp0")
    u0 = _matmul(hm0, w1_0, "nn", name="mlp0_up", out_dtype=BF16)
    h2 = _matmul(u0, w2_0, "nn", name="mlp0_down", a_relu2=True, add=h1)

    w_z, w_xbc, w_dt, wout, w1_1, w2_1 = late_weights(h2)
    hn1 = _rmsnorm_fwd(h2, g_mix1, name="norm_mix1")
    z = _matmul(hn1, w_z, "nn", name="in_proj_z", out_dtype=BF16)
    xbc = _matmul(hn1, w_xbc, "nn", name="in_proj_xbc", out_dtype=BF16)
    dt_raw = _matmul(hn1, w_dt, "nn", name="in_proj_dt")
    dt_raw_t = dt_raw[:, :N_HEADS].T
    xc = _conv_fwd(xbc, conv_w, conv_b)
    y, states = _ssd_fwd(xc, dt_raw, dt_raw_t, bias_x, bias_c, alog_x, alog_c, dskip_x, e_mat)
    yn = _gate_fwd(y, z, gate_g)
    h3 = _matmul(yn, wout, "nn", name="out_proj", add=h2)
    hm1 = _rmsnorm_fwd(h3, g_mlp1, name="norm_mlp1")
    u1 = _matmul(hm1, w1_1, "nn", name="mlp1_up", out_dtype=BF16)
    h4 = _matmul(u1, w2_1, "nn", name="mlp1_down", a_relu2=True, add=h3)

    dh4, dh4_16, small_final = _loss_head(h4, fg, target)

    def mlp_bwd_weights(dh_out16, hm, u, w2_i, tag):
        du = _matmul(dh_out16, w2_i, "nt", name=tag + "_du", out_dtype=BF16, relu2_grad_of=u)
        dw2 = _matmul(u, dh_out16, "tn", name=tag + "_dw2", a_relu2=True)
        dw1 = _matmul(hm, du, "tn", name=tag + "_dw1", out_col_shards=N_CHIPS)
        return du, dw1, dw2.reshape(N_CHIPS, D_FF // N_CHIPS, D_MODEL)

    def mlp_bwd_input(du, dh_out, h_in, w1_i, g_i, tag):
        dhm = _matmul(du, w1_i, "nt", name=tag + "_dhm")
        return _rmsnorm_bwd(dhm, h_in, g_i, dh_out, name=tag + "_norm_bwd")

    du1, dw1_1, dw2_1 = mlp_bwd_weights(dh4_16, hm1, u1, w2_1, "mlp1")
    dh3, dh3_16, dg_mlp1 = mlp_bwd_input(du1, dh4, h3, w1_1, g_mlp1, "mlp1")

    dyn = _matmul(dh3_16, wout, "nt", name="out_proj_dyn")
    dwout = _matmul(yn, dh3_16, "tn", name="out_proj_dw").reshape(N_CHIPS, D_INNER // N_CHIPS, D_MODEL)
    dy, dz, dg_gate = _gate_bwd(dyn, y, z, gate_g)
    dxc, ddt_raw, small_ssd = _ssd_bwd(xc, dt_raw, dt_raw_t, bias_x, bias_c, alog_x, alog_c, dskip_x,
                                       bias_r, alog_r, e_mat, e_mat.T, states, dy)
    dv, dconv = _conv_bwd_act(xbc, dxc, conv_w, conv_b)
    dxbc = _conv_bwd_in(dv, conv_w)
    dhn1 = _matmul(ddt_raw, w_dt, "nt", name="in_proj_dt_dh")
    dhn1 = _matmul(dz, w_z, "nt", name="in_proj_z_dh", add=dhn1)
    dhn1 = _matmul(dxbc, w_xbc, "nt", name="in_proj_xbc_dh", add=dhn1)
    dw_z = _matmul(hn1, dz, "tn", name="in_proj_z_dw")
    dw_xbc = _matmul(hn1, dxbc, "tn", name="in_proj_xbc_dw")
    dw_dt = _matmul(hn1, ddt_raw, "tn", name="in_proj_dt_dw")
    dwin = jnp.concatenate([dw_z, dw_xbc, dw_dt[:, :N_HEADS]], axis=1)
    dwin = jnp.transpose(dwin.reshape(D_MODEL, N_CHIPS, IN_PROJ_DIM // N_CHIPS), (1, 0, 2))
    behind = layer1_grads_ready([dwin, dwout, dw1_1, dw2_1])
    dh2, dh2_16, dg_mix1 = _rmsnorm_bwd(dhn1, h2, g_mix1 + behind, dh3, name="norm_mix1_bwd")

    du0, dw1_0, dw2_0 = mlp_bwd_weights(dh2_16, hm0, u0, w2_0, "mlp0")
    behind = layer1_grads_midway(dw1_0)
    dh1, _, dg_mlp0 = mlp_bwd_input(du0, dh2, h1, w1_0, g_mlp0 + behind, "mlp0")
    dx, dpw, small_pool = _pool_bwd(xs, g_mix0, pw, pool_b, pool_scale, dh1)
    dpw = jnp.transpose(dpw.reshape(4, N_CHIPS, POOL_GROUP // N_CHIPS, POOL_GROUP), (1, 0, 2, 3))
    dpw = dpw.reshape(N_CHIPS, 4 * (POOL_GROUP // N_CHIPS), POOL_GROUP)

    big = [dpw, dw1_0, dw2_0]
    rows = [
        small_final[1:2],
        small_final[0:1],
        small_pool[0:1], dg_mix1[0:1],
        dg_mlp0[0:1], dg_mlp1[0:1],
        small_pool[1:2], small_pool[2:3],
        _pad_lanes(small_ssd[0:3], D_MODEL),
        _rows_1024(dg_gate[0:1]),
        _rows_1024(dconv[0:CONV_K]),
        _rows_1024(dconv[CONV_K:CONV_K + 1]),
    ]
    return dx, big, rows


def kernel(x, norm_mix_g, norm_mlp_g, pool_w, pool_b, pool_scale, ssm_w_in, ssm_conv_w, ssm_conv_b, ssm_dt_bias, ssm_a_log, ssm_d, ssm_norm_g, ssm_w_out, mlp_w1, mlp_w2, final_g, loss_target, m_norm_mix_g, m_norm_mlp_g, m_pool_w, m_pool_b, m_pool_scale, m_ssm_w_in, m_ssm_conv_w, m_ssm_conv_b, m_ssm_dt_bias, m_ssm_a_log, m_ssm_d, m_ssm_norm_g, m_ssm_w_out, m_mlp_w1, m_mlp_w2, m_final_g, v_norm_mix_g, v_norm_mlp_g, v_pool_w, v_pool_b, v_pool_scale, v_ssm_w_in, v_ssm_conv_w, v_ssm_conv_b, v_ssm_dt_bias, v_ssm_a_log, v_ssm_d, v_ssm_norm_g, v_ssm_w_out, v_mlp_w1, v_mlp_w2, v_final_g):
    t_dim = x.shape[1]
    xs = x[0]
    target = loss_target[0]
    my_x, my_y, my_c = _position()
    my_chip = 2 * my_x + my_y

    def halves(w):
        return w.astype(BF16).reshape((2, w.shape[0] // 2) + w.shape[1:])

    def whole(gathered, own_shard):
        g = lax.dynamic_update_index_in_dim(gathered, own_shard, my_chip, axis=0)
        return g.reshape((N_CHIPS, 2 * g.shape[2]) + g.shape[3:])

    def up_weight(g):
        return jnp.transpose(g, (1, 0, 2)).reshape(D_MODEL, D_FF)

    early_own = [halves(pool_w[0]), halves(mlp_w1[0]), halves(mlp_w2[0])]
    early = _all_gather_weights(early_own)
    late_own = [halves(ssm_w_in[0]), halves(ssm_w_out[0]), halves(mlp_w1[1]), halves(mlp_w2[1])]
    early, late_own = lax.optimization_barrier((early, late_own))
    late_lands = [jax.ShapeDtypeStruct((N_CHIPS,) + s.shape, BF16) for s in late_own]
    n_late = len(late_own) * (N_CHIPS - 1)
    late_state, behind_gather = _exchange_start("gather_late_start", _gather_copies, n_late, late_own, late_lands)

    g_pool, g_w1_0, g_w2_0 = [whole(g, o) for g, o in zip(early, early_own)]
    pw = jnp.transpose(g_pool, (1, 0, 2, 3)).reshape(4, POOL_GROUP, POOL_GROUP)
    w1_0 = up_weight(g_w1_0)
    w2_0 = g_w2_0.reshape(D_FF, D_MODEL)

    def late_weights(after):
        late_own_thru, landed = _exchange_wait("gather_late_wait", _gather_copies, late_state, after)
        g_win, g_wout, g_w1_1, g_w2_1 = [whole(g, o) for g, o in zip(_forward_halves(landed), late_own_thru)]
        win = jnp.transpose(g_win, (1, 0, 2)).reshape(D_MODEL, IN_PROJ_DIM)
        w_dt = _pad_lanes(win[:, D_INNER + CONV_DIM:], 128)
        return (win[:, :D_INNER], win[:, D_INNER:D_INNER + CONV_DIM], w_dt,
                g_wout.reshape(D_INNER, D_MODEL), up_weight(g_w1_1), g_w2_1.reshape(D_FF, D_MODEL))

    place = jnp.stack([my_c, my_chip]).astype(jnp.int32)
    reduce1 = {}

    def layer1_grads_ready(grads):
        reduce1["n"] = len(grads)
        reduce1["pair"], behind = _exchange_start("pair_exchange1_start", _pair_copies, len(grads), grads,
                                                  _pair_lands(grads))
        return behind

    def layer1_grads_midway(after):
        grads, recv = _exchange_wait("pair_exchange1_wait", _pair_copies, reduce1["pair"], after)
        sums = [_pair_sum(g, r, place, name="pair_sum1_%d" % i) for i, (g, r) in enumerate(zip(grads, recv))]
        reduce1["f32"] = [s32 for _, s32 in sums]
        b16 = [s16 for s16, _ in sums]
        reduce1["chip"], behind = _exchange_start("chip_exchange1_start", _chip_copies,
                                                  len(b16) * (N_CHIPS - 1), b16, _same_lands(b16))
        return behind

    conv_w = jnp.zeros((CONV_K, CONV_DIM), F32)
    conv_b = jnp.zeros((1, CONV_DIM), F32)
    gate_g = jnp.zeros((1, D_INNER), F32)
    conv_w = lax.dynamic_update_slice(conv_w, ssm_conv_w[0], (0, my_chip * (CONV_DIM // N_CHIPS)))
    conv_b = lax.dynamic_update_slice(conv_b, ssm_conv_b, (0, my_chip * (CONV_DIM // N_CHIPS)))
    gate_g = lax.dynamic_update_slice(gate_g, ssm_norm_g, (0, my_chip * (D_INNER // N_CHIPS)))
    vec_rows = jnp.concatenate([_rows_1024(conv_w), _rows_1024(conv_b), _rows_1024(gate_g)], axis=0)
    vec_rows = jnp.pad(vec_rows, ((0, (-vec_rows.shape[0]) % 8), (0, 0)))
    vec_rows = _all_reduce_small(vec_rows * 0.5, name="gather_vectors", sum_row0=False)
    conv_w = vec_rows[0:12].reshape(CONV_K, CONV_DIM)
    conv_b = vec_rows[12:15].reshape(1, CONV_DIM)
    gate_g = vec_rows[15:17].reshape(1, D_INNER)

    dx, big0, rows = _local_step(xs, target, pw, w1_0, w2_0, late_weights, layer1_grads_ready,
                                 layer1_grads_midway, conv_w, conv_b, gate_g,
                                 norm_mix_g + behind_gather, norm_mlp_g, pool_b, pool_scale,
                                 ssm_dt_bias, ssm_a_log, ssm_d, final_g)

    recv0 = _exchange("pair_exchange0", _pair_copies, len(big0), big0, _pair_lands(big0))
    sums0 = [_pair_sum(g, r, place, name="pair_sum0_%d" % i) for i, (g, r) in enumerate(zip(big0, recv0))]
    b16_0 = [s16 for s16, _ in sums0]
    got0 = _exchange("chip_exchange0", _chip_copies, len(b16_0) * (N_CHIPS - 1), b16_0, _same_lands(b16_0))
    finished = [_chip_sum(s32, r, name="chip_sum0_%d" % i) for i, ((_, s32), r) in enumerate(zip(sums0, got0))]
    _, got1 = _exchange_wait("chip_exchange1_wait", _chip_copies, reduce1["chip"], finished[-1])
    finished += [_chip_sum(s32, r, name="chip_sum1_%d" % i) for i, (s32, r) in enumerate(zip(reduce1["f32"], got1))]
    g_pool_w, g_w1_0, g_w2_0, g_win_s, g_wout_s, g_w1_1, g_w2_1 = _pair_gather_halves(finished)

    small = jnp.concatenate(rows, axis=0)
    small = jnp.pad(small, ((0, (-small.shape[0]) % 8), (0, 0)))
    small = _all_reduce_small(small, name="all_reduce_small", sum_row0=True)
    loss = small[0, 0]
    g_final = small[1]
    g_norm_mix = small[2:4]
    g_norm_mlp = small[4:6]
    g_pool_b, g_pool_scale = small[6:7], small[7:8]
    g_alog, g_dtb, g_dsk = small[8:9, :N_HEADS], small[9:10, :N_HEADS], small[10:11, :N_HEADS]
    g_gate_full = small[11:13].reshape(1, D_INNER)
    g_convw_full = small[13:25].reshape(CONV_K, CONV_DIM)
    g_convb_full = small[25:28].reshape(1, CONV_DIM)
    g_gate = lax.dynamic_slice_in_dim(g_gate_full, my_chip * (D_INNER // N_CHIPS), D_INNER // N_CHIPS, axis=1)
    g_convw = lax.dynamic_slice_in_dim(g_convw_full, my_chip * (CONV_DIM // N_CHIPS), CONV_DIM // N_CHIPS, axis=1)
    g_convb = lax.dynamic_slice_in_dim(g_convb_full, my_chip * (CONV_DIM // N_CHIPS), CONV_DIM // N_CHIPS, axis=1)

    grads = {
        "norm_mix_g": g_norm_mix, "norm_mlp_g": g_norm_mlp,
        "pool_w": g_pool_w.reshape(pool_w.shape), "pool_b": g_pool_b, "pool_scale": g_pool_scale,
        "ssm_w_in": g_win_s.reshape(ssm_w_in.shape), "ssm_conv_w": g_convw.reshape(ssm_conv_w.shape),
        "ssm_conv_b": g_convb, "ssm_dt_bias": g_dtb, "ssm_a_log": g_alog, "ssm_d": g_dsk,
        "ssm_norm_g": g_gate, "ssm_w_out": g_wout_s.reshape(ssm_w_out.shape),
        "mlp_w1": jnp.stack([g_w1_0, g_w1_1]), "mlp_w2": jnp.stack([g_w2_0, g_w2_1]),
        "final_g": g_final,
    }
    weights = dict(norm_mix_g=norm_mix_g, norm_mlp_g=norm_mlp_g, pool_w=pool_w, pool_b=pool_b,
                   pool_scale=pool_scale, ssm_w_in=ssm_w_in, ssm_conv_w=ssm_conv_w, ssm_conv_b=ssm_conv_b,
                   ssm_dt_bias=ssm_dt_bias, ssm_a_log=ssm_a_log, ssm_d=ssm_d, ssm_norm_g=ssm_norm_g,
                   ssm_w_out=ssm_w_out, mlp_w1=mlp_w1, mlp_w2=mlp_w2, final_g=final_g)
    moms = dict(norm_mix_g=(m_norm_mix_g, v_norm_mix_g), norm_mlp_g=(m_norm_mlp_g, v_norm_mlp_g),
                pool_w=(m_pool_w, v_pool_w), pool_b=(m_pool_b, v_pool_b),
                pool_scale=(m_pool_scale, v_pool_scale), ssm_w_in=(m_ssm_w_in, v_ssm_w_in),
                ssm_conv_w=(m_ssm_conv_w, v_ssm_conv_w), ssm_conv_b=(m_ssm_conv_b, v_ssm_conv_b),
                ssm_dt_bias=(m_ssm_dt_bias, v_ssm_dt_bias), ssm_a_log=(m_ssm_a_log, v_ssm_a_log),
                ssm_d=(m_ssm_d, v_ssm_d), ssm_norm_g=(m_ssm_norm_g, v_ssm_norm_g),
                ssm_w_out=(m_ssm_w_out, v_ssm_w_out), mlp_w1=(m_mlp_w1, v_mlp_w1),
                mlp_w2=(m_mlp_w2, v_mlp_w2), final_g=(m_final_g, v_final_g))
    names = list(weights)
    big_names = ("pool_w", "ssm_w_in", "ssm_w_out", "mlp_w1", "mlp_w2")
    deltas, new_m, new_v = {}, {}, {}
    for nm in big_names:
        w = weights[nm]
        two_d = (-1, w.shape[-1])
        d_, m_, v_ = _adamw(w.reshape(two_d), grads[nm].reshape(two_d), moms[nm][0].reshape(two_d),
                            moms[nm][1].reshape(two_d), name="adamw_" + nm)
        deltas[nm], new_m[nm], new_v[nm] = d_.reshape(w.shape), m_.reshape(w.shape), v_.reshape(w.shape)
    small_names = [nm for nm in names if nm not in big_names]
    sizes = [weights[nm].size for nm in small_names]

    def pack(parts):
        flat = jnp.concatenate([p.reshape(-1) for p in parts])
        pad = (-flat.shape[0]) % (8 * D_MODEL)
        return jnp.pad(flat, (0, pad)).reshape(-1, D_MODEL)

    d_, m_, v_ = _adamw(pack([weights[nm] for nm in small_names]), pack([grads[nm] for nm in small_names]),
                        pack([moms[nm][0] for nm in small_names]), pack([moms[nm][1] for nm in small_names]),
                        name="adamw_small")
    off = 0
    for nm, sz in zip(small_names, sizes):
        shp = weights[nm].shape
        deltas[nm] = d_.reshape(-1)[off:off + sz].reshape(shp)
        new_m[nm] = m_.reshape(-1)[off:off + sz].reshape(shp)
        new_v[nm] = v_.reshape(-1)[off:off + sz].reshape(shp)
        off += sz

    grad_x = dx.reshape(x.shape)
    out_grads = [grads[nm].reshape(weights[nm].shape) for nm in names]
    return (loss, grad_x, *out_grads, *[deltas[nm] for nm in names],
            *[new_m[nm] for nm in names], *[new_v[nm] for nm in names])
```

```python
import functools

import jax
import jax.numpy as jnp
from jax import lax
from jax.experimental import pallas as pl
from jax.experimental.pallas import tpu as pltpu

F32 = jnp.float32
BF16 = jnp.bfloat16
MESH = pl.DeviceIdType.MESH

D_MODEL = 1024
RMS_EPS = 1e-5
POOL_WINDOWS = (2, 4, 8, 16)
POOL_GROUP = 256
POOL_HALO = 16
D_INNER = 2048
HEAD_DIM = 64
N_HEADS = 32
N_GROUPS = 4
HEADS_PER_GROUP = 8
D_STATE = 128
CONV_K = 4
CONV_HALO = 8
CHUNK = 128
CONV_DIM = 3072
IN_PROJ_DIM = 5152
D_FF = 4096
N_CHIPS = 4
N_DEV = 8

ADAM_LR = 0.001
ADAM_B1 = 0.9
ADAM_B2 = 0.999
ADAM_EPS = 1e-08
ADAM_WD = 0.01
ADAM_STEP = 10

VMEM_LIMIT = 56 * 1024 * 1024
NEG_INF = float("-inf")


def _pcall(body, **kw):
    return pl.pallas_call(body, **kw)


def _params(*sem):
    return pltpu.CompilerParams(dimension_semantics=sem, vmem_limit_bytes=VMEM_LIMIT)


def _sigmoid(v):
    return 1.0 / (1.0 + jnp.exp(-v))


def _row_spec(tb, d, nb=None, reverse=False):
    if reverse:
        return pl.BlockSpec((tb, d), lambda i: (nb - 1 - i, 0))
    return pl.BlockSpec((tb, d), lambda i: (i, 0))


def _const_spec(shape):
    return pl.BlockSpec(shape, lambda *_: tuple(0 for _ in shape))


_DIMS = {"nn": (((1,), (0,)), ((), ())),
         "nt": (((1,), (1,)), ((), ())),
         "tn": (((0,), (0,)), ((), ()))}


_MATMUL_VMEM_BUDGET = 40 * 1024 * 1024


def _matmul_tiles(m_dim, n_dim, k_dim, a_bytes, b_bytes, mn_bytes):
    tm, tn = min(m_dim, 1024), min(n_dim, 1024)
    while 2 * (tm * k_dim * a_bytes + tn * k_dim * b_bytes + tm * tn * mn_bytes) > _MATMUL_VMEM_BUDGET:
        if tm >= tn:
            tm //= 2
        else:
            tn //= 2
    return tm, tn


def _matmul(a, b, mode, *, name, out_dtype=F32, a_relu2=False, add=None, relu2_grad_of=None,
            out_col_shards=1):
    if mode == "tn":
        k_dim, m_dim = a.shape
    else:
        m_dim, k_dim = a.shape
    n_dim = b.shape[0] if mode == "nt" else b.shape[1]
    mn_bytes = jnp.dtype(out_dtype).itemsize
    if relu2_grad_of is not None:
        mn_bytes += relu2_grad_of.dtype.itemsize
    if add is not None:
        mn_bytes += add.dtype.itemsize
    tm, tn = _matmul_tiles(m_dim, n_dim, k_dim, a.dtype.itemsize, b.dtype.itemsize, mn_bytes)
    assert m_dim % tm == 0 and n_dim % tn == 0
    a_spec = (pl.BlockSpec((k_dim, tm), lambda i, j: (0, i)) if mode == "tn"
              else pl.BlockSpec((tm, k_dim), lambda i, j: (i, 0)))
    b_spec = (pl.BlockSpec((tn, k_dim), lambda i, j: (j, 0)) if mode == "nt"
              else pl.BlockSpec((k_dim, tn), lambda i, j: (0, j)))
    mn_spec = pl.BlockSpec((tm, tn), lambda i, j: (i, j))
    operands, in_specs = [a, b], [a_spec, b_spec]
    if relu2_grad_of is not None:
        operands.append(relu2_grad_of)
        in_specs.append(mn_spec)
    if add is not None:
        operands.append(add)
        in_specs.append(mn_spec)
    if out_col_shards == 1:
        out_shape = jax.ShapeDtypeStruct((m_dim, n_dim), out_dtype)
        out_spec = mn_spec
    else:
        n_shard = n_dim // out_col_shards
        assert n_shard % tn == 0
        per = n_shard // tn
        out_shape = jax.ShapeDtypeStruct((out_col_shards, m_dim, n_shard), out_dtype)
        out_spec = pl.BlockSpec((None, tm, tn), lambda i, j: (j // per, i, j % per))

    def body(*refs):
        a_ref, b_ref, o_ref = refs[0], refs[1], refs[-1]
        av = a_ref[...]
        if a_relu2:
            av = jnp.maximum(av, 0)
            av = av * av
        r = lax.dot_general(av.astype(BF16), b_ref[...].astype(BF16), _DIMS[mode],
                            preferred_element_type=F32)
        nxt = 2
        if relu2_grad_of is not None:
            r = r * (2.0 * jnp.maximum(refs[nxt][...].astype(F32), 0.0))
            nxt += 1
        if add is not None:
            r = r + refs[nxt][...]
        o_ref[...] = r.astype(out_dtype)

    return _pcall(
        body, name=name, out_shape=out_shape,
        grid=(m_dim // tm, n_dim // tn),
        in_specs=in_specs, out_specs=out_spec,
        compiler_params=_params("parallel", "parallel"),
    )(*operands)


def _rms(x):
    return lax.rsqrt(jnp.mean(x * x, axis=-1, keepdims=True) + RMS_EPS)


def _rmsnorm_fwd(h, g, *, name, tb=512):
    t_dim, d = h.shape

    def body(h_ref, g_ref, o_ref):
        x = h_ref[...]
        o_ref[...] = (x * _rms(x) * g_ref[...]).astype(BF16)

    return _pcall(
        body, name=name, out_shape=jax.ShapeDtypeStruct((t_dim, d), BF16),
        grid=(t_dim // tb,), in_specs=[_row_spec(tb, d), _const_spec((1, d))],
        out_specs=_row_spec(tb, d), compiler_params=_params("parallel"),
    )(h, g)


def _rmsnorm_bwd(dy, h, g, dres, *, name, tb=512):
    t_dim, d = h.shape

    def body(dy_ref, h_ref, g_ref, dres_ref, dh_ref, dh16_ref, dg_ref):
        @pl.when(pl.program_id(0) == 0)
        def _():
            dg_ref[...] = jnp.zeros_like(dg_ref)

        x = h_ref[...]
        r = _rms(x)
        xhat = x * r
        dyv = dy_ref[...]
        dxhat = dyv * g_ref[...]
        dh = dres_ref[...] + r * (dxhat - xhat * jnp.mean(dxhat * xhat, axis=-1, keepdims=True))
        dh_ref[...] = dh
        dh16_ref[...] = dh.astype(BF16)
        dg_ref[0:1, :] += jnp.sum(dyv * xhat, axis=0, keepdims=True)

    return _pcall(
        body, name=name,
        out_shape=(jax.ShapeDtypeStruct((t_dim, d), F32), jax.ShapeDtypeStruct((t_dim, d), BF16),
                   jax.ShapeDtypeStruct((8, d), F32)),
        grid=(t_dim // tb,),
        in_specs=[_row_spec(tb, d), _row_spec(tb, d), _const_spec((1, d)), _row_spec(tb, d)],
        out_specs=(_row_spec(tb, d), _row_spec(tb, d), _const_spec((8, d))),
        compiler_params=_params("arbitrary"),
    )(dy, h, g, dres)


def _pool_mixed(ext, hn, t0, tb):
    t = t0 + lax.broadcasted_iota(jnp.int32, (tb, 1), 0)
    parts = []
    for gi, w in enumerate(POOL_WINDOWS):
        lanes = slice(gi * POOL_GROUP, (gi + 1) * POOL_GROUP)
        s = ext[:, lanes]
        k = 1
        while k < w:
            s = s + pltpu.roll(s, k, 0)
            k *= 2
        cnt = jnp.minimum(t + 1, w).astype(F32)
        parts.append(s[POOL_HALO:, :] / cnt - hn[:, lanes])
    return parts


def _pool_fwd(x, g, pw, pb, ps, *, tb=512):
    t_dim, d = x.shape

    def body(x_ref, g_ref, pw_ref, pb_ref, ps_ref, o_ref, ext_ref):
        i = pl.program_id(0)

        @pl.when(i == 0)
        def _():
            ext_ref[0:POOL_HALO, :] = jnp.zeros((POOL_HALO, d), F32)

        xv = x_ref[...]
        hn = xv * _rms(xv) * g_ref[...]
        ext_ref[POOL_HALO:, :] = hn
        mixed = _pool_mixed(ext_ref[...], hn, i * tb, tb)
        for gi in range(len(POOL_WINDOWS)):
            lanes = slice(gi * POOL_GROUP, (gi + 1) * POOL_GROUP)
            out = jnp.dot(mixed[gi].astype(BF16), pw_ref[gi], preferred_element_type=F32)
            o_ref[:, lanes] = xv[:, lanes] + (out + pb_ref[:, lanes]) * ps_ref[:, lanes]
        ext_ref[0:POOL_HALO, :] = hn[tb - POOL_HALO:, :]

    return _pcall(
        body, name="pool_fwd", out_shape=jax.ShapeDtypeStruct((t_dim, d), F32),
        grid=(t_dim // tb,),
        in_specs=[_row_spec(tb, d), _const_spec((1, d)), _const_spec((4, POOL_GROUP, POOL_GROUP)),
                  _const_spec((1, d)), _const_spec((1, d))],
        out_specs=_row_spec(tb, d),
        scratch_shapes=[pltpu.VMEM((POOL_HALO + tb, d), F32)],
        compiler_params=_params("arbitrary"),
    )(x, g, pw, pb, ps)


def _pool_bwd(x, g, pw, pb, ps, dh1, *, tb=512):
    t_dim, d = x.shape
    nb = t_dim // tb
    halo_per_block = tb // POOL_HALO

    def body(x_ref, xprev_ref, g_ref, pw_ref, pb_ref, ps_ref, dh1_ref,
             dx_ref, dpw_ref, small_ref, ext_ref, dext_ref):
        i = pl.program_id(0)
        blk = nb - 1 - i

        @pl.when(i == 0)
        def _():
            dpw_ref[...] = jnp.zeros_like(dpw_ref)
            small_ref[...] = jnp.zeros_like(small_ref)
            dext_ref[tb:, :] = jnp.zeros((POOL_HALO, d), F32)

        gv = g_ref[...]
        xv = x_ref[...]
        r = _rms(xv)
        xhat = xv * r
        hn = xhat * gv
        xp = xprev_ref[...]
        hprev = xp * _rms(xp) * gv * (blk > 0).astype(F32)
        ext_ref[0:POOL_HALO, :] = hprev
        ext_ref[POOL_HALO:, :] = hn
        mixed = _pool_mixed(ext_ref[...], hn, blk * tb, tb)

        dout = dh1_ref[...]
        t = blk * tb + lax.broadcasted_iota(jnp.int32, (tb, 1), 0)
        for gi, w in enumerate(POOL_WINDOWS):
            lanes = slice(gi * POOL_GROUP, (gi + 1) * POOL_GROUP)
            mb = mixed[gi].astype(BF16)
            pre = jnp.dot(mb, pw_ref[gi], preferred_element_type=F32) + pb_ref[:, lanes]
            dg_out = dout[:, lanes]
            small_ref[2:3, lanes] += jnp.sum(dg_out * pre, axis=0, keepdims=True)
            dpre = dg_out * ps_ref[:, lanes]
            small_ref[1:2, lanes] += jnp.sum(dpre, axis=0, keepdims=True)
            dpb16 = dpre.astype(BF16)
            dpw_ref[gi] += lax.dot_general(mb, dpb16, _DIMS["tn"], preferred_element_type=F32)
            dmixed = lax.dot_general(dpb16, pw_ref[gi], _DIMS["nt"], preferred_element_type=F32)
            cnt = jnp.minimum(t + 1, w).astype(F32)
            dq = dmixed / cnt
            dext_ref[0:tb, lanes] = dq
            s = dext_ref[:, lanes]
            k = 1
            while k < w:
                s = s + pltpu.roll(s, tb + POOL_HALO - k, 0)
                k *= 2
            dhn = s[0:tb, :] - dmixed
            dext_ref[tb:, lanes] = dq[0:POOL_HALO, :]
            small_ref[0:1, lanes] += jnp.sum(dhn * xhat[:, lanes], axis=0, keepdims=True)
            ext_ref[POOL_HALO:, lanes] = dhn * gv[:, lanes]
        dxhat = ext_ref[POOL_HALO:, :]
        dx_ref[...] = dout + r * (dxhat - xhat * jnp.mean(dxhat * xhat, axis=-1, keepdims=True))

    return _pcall(
        body, name="pool_bwd",
        out_shape=(jax.ShapeDtypeStruct((t_dim, d), F32),
                   jax.ShapeDtypeStruct((4, POOL_GROUP, POOL_GROUP), F32),
                   jax.ShapeDtypeStruct((8, d), F32)),
        grid=(nb,),
        in_specs=[_row_spec(tb, d, nb, True),
                  pl.BlockSpec((POOL_HALO, d),
                               lambda i: (jnp.maximum((nb - 1 - i) * halo_per_block - 1, 0), 0)),
                  _const_spec((1, d)), _const_spec((4, POOL_GROUP, POOL_GROUP)),
                  _const_spec((1, d)), _const_spec((1, d)), _row_spec(tb, d, nb, True)],
        out_specs=(_row_spec(tb, d, nb, True), _const_spec((4, POOL_GROUP, POOL_GROUP)),
                   _const_spec((8, d))),
        scratch_shapes=[pltpu.VMEM((POOL_HALO + tb, d), F32), pltpu.VMEM((tb + POOL_HALO, d), F32)],
        compiler_params=_params("arbitrary"),
    )(x, x, g, pw, pb, ps, dh1)


_CONV_CB = 1024


def _conv_pre(e, u, w, b, tb):
    acc = u * w[CONV_K - 1:CONV_K, :] + b
    for sh in range(1, CONV_K):
        acc = acc + pltpu.roll(e, sh, 0)[CONV_HALO:, :] * w[CONV_K - 1 - sh:CONV_K - sh, :]
    return acc


def _conv_fwd(u, w, b, *, tb=512):
    t_dim, c = u.shape
    cb = _CONV_CB

    def body(u_ref, w_ref, b_ref, o_ref, ext_ref):
        @pl.when(pl.program_id(1) == 0)
        def _():
            ext_ref[0:CONV_HALO, :] = jnp.zeros((CONV_HALO, cb), F32)

        uv = u_ref[...].astype(F32)
        ext_ref[CONV_HALO:, :] = uv
        v = _conv_pre(ext_ref[...], uv, w_ref[...], b_ref[...], tb)
        o_ref[...] = (v * _sigmoid(v)).astype(BF16)
        ext_ref[0:CONV_HALO, :] = uv[tb - CONV_HALO:, :]

    blk = pl.BlockSpec((tb, cb), lambda j, t: (t, j))
    return _pcall(
        body, name="conv_fwd", out_shape=jax.ShapeDtypeStruct((t_dim, c), BF16),
        grid=(c // cb, t_dim // tb),
        in_specs=[blk, pl.BlockSpec((CONV_K, cb), lambda j, t: (0, j)),
                  pl.BlockSpec((1, cb), lambda j, t: (0, j))],
        out_specs=blk,
        scratch_shapes=[pltpu.VMEM((CONV_HALO + tb, cb), F32)],
        compiler_params=_params("parallel", "arbitrary"),
    )(u, w, b)


def _conv_bwd_act(u, dxc, w, b, *, tb=512):
    t_dim, c = u.shape
    cb = _CONV_CB

    def body(u_ref, d_ref, w_ref, b_ref, dv_ref, dwb_ref, ext_ref):
        @pl.when(pl.program_id(1) == 0)
        def _():
            ext_ref[0:CONV_HALO, :] = jnp.zeros((CONV_HALO, cb), F32)
            dwb_ref[...] = jnp.zeros_like(dwb_ref)

        uv = u_ref[...].astype(F32)
        ext_ref[CONV_HALO:, :] = uv
        e = ext_ref[...]
        v = _conv_pre(e, uv, w_ref[...], b_ref[...], tb)
        sg = _sigmoid(v)
        dv = d_ref[...].astype(F32) * (sg * (1.0 + v * (1.0 - sg)))
        dv_ref[...] = dv.astype(BF16)
        dwb_ref[CONV_K:CONV_K + 1, :] += jnp.sum(dv, axis=0, keepdims=True)
        dwb_ref[CONV_K - 1:CONV_K, :] += jnp.sum(dv * uv, axis=0, keepdims=True)
        for sh in range(1, CONV_K):
            us = pltpu.roll(e, sh, 0)[CONV_HALO:, :]
            dwb_ref[CONV_K - 1 - sh:CONV_K - sh, :] += jnp.sum(dv * us, axis=0, keepdims=True)
        ext_ref[0:CONV_HALO, :] = uv[tb - CONV_HALO:, :]

    blk = pl.BlockSpec((tb, cb), lambda j, t: (t, j))
    return _pcall(
        body, name="conv_bwd_act",
        out_shape=(jax.ShapeDtypeStruct((t_dim, c), BF16), jax.ShapeDtypeStruct((8, c), F32)),
        grid=(c // cb, t_dim // tb),
        in_specs=[blk, blk, pl.BlockSpec((CONV_K, cb), lambda j, t: (0, j)),
                  pl.BlockSpec((1, cb), lambda j, t: (0, j))],
        out_specs=(blk, pl.BlockSpec((8, cb), lambda j, t: (0, j))),
        scratch_shapes=[pltpu.VMEM((CONV_HALO + tb, cb), F32)],
        compiler_params=_params("parallel", "arbitrary"),
    )(u, dxc, w, b)


def _conv_bwd_in(dv, w, *, tb=512):
    t_dim, c = dv.shape
    cb = _CONV_CB
    nb = t_dim // tb

    def body(dv_ref, w_ref, du_ref, ext_ref):
        @pl.when(pl.program_id(1) == 0)
        def _():
            ext_ref[tb:, :] = jnp.zeros((CONV_HALO, cb), F32)

        d = dv_ref[...].astype(F32)
        ext_ref[0:tb, :] = d
        e = ext_ref[...]
        wv = w_ref[...]
        acc = d * wv[CONV_K - 1:CONV_K, :]
        for sh in range(1, CONV_K):
            acc = acc + pltpu.roll(e, tb + CONV_HALO - sh, 0)[0:tb, :] * wv[CONV_K - 1 - sh:CONV_K - sh, :]
        du_ref[...] = acc.astype(BF16)
        ext_ref[tb:, :] = d[0:CONV_HALO, :]

    blk = pl.BlockSpec((tb, cb), lambda j, t: (nb - 1 - t, j))
    return _pcall(
        body, name="conv_bwd_in", out_shape=jax.ShapeDtypeStruct((t_dim, c), BF16),
        grid=(c // cb, nb),
        in_specs=[blk, pl.BlockSpec((CONV_K, cb), lambda j, t: (0, j))],
        out_specs=blk,
        scratch_shapes=[pltpu.VMEM((tb + CONV_HALO, cb), F32)],
        compiler_params=_params("parallel", "arbitrary"),
    )(dv, w)


def _softplus(v):
    e = jnp.exp(-jnp.abs(v))
    w = 1.0 + e
    log1p = jnp.where(w == 1.0, e, jnp.log(w) * e / jnp.where(w == 1.0, 1.0, w - 1.0))
    return jnp.maximum(v, 0.0) + log1p


def _cumsum_rows(v):
    row = lax.broadcasted_iota(jnp.int32, v.shape, 0)
    k = 1
    while k < CHUNK:
        v = v + jnp.where(row >= k, pltpu.roll(v, k, 0), 0.0)
        k *= 2
    return v


def _cumsum_lanes(v):
    col = lax.broadcasted_iota(jnp.int32, v.shape, 1)
    k = 1
    while k < CHUNK:
        v = v + jnp.where(col >= k, pltpu.roll(v, k, 1), 0.0)
        k *= 2
    return v


def _rev_cumsum_rows(v):
    row = lax.broadcasted_iota(jnp.int32, v.shape, 0)
    k = 1
    while k < CHUNK:
        v = v + jnp.where(row < CHUNK - k, pltpu.roll(v, CHUNK - k, 0), 0.0)
        k *= 2
    return v


PAIR = 2 * HEAD_DIM
GROUP_LANES = HEADS_PER_GROUP * HEAD_DIM


def _head_lane_matrix():
    h = lax.broadcasted_iota(jnp.int32, (128, D_INNER), 0)
    j = lax.broadcasted_iota(jnp.int32, (128, D_INNER), 1)
    return (j // HEAD_DIM == h).astype(BF16)


def _split_bf16(v, pieces):
    out = []
    for _ in range(pieces):
        p = v.astype(BF16)
        out.append(p)
        v = v - p.astype(F32)
    return out


def _expand_heads(v, e):
    return sum(jnp.dot(p, e, preferred_element_type=F32) for p in _split_bf16(v, 3))


def _reduce_heads(v, et, pieces):
    return sum(jnp.dot(p, et, preferred_element_type=F32) for p in _split_bf16(v, pieces))


def _ssd_common(dtr_ref, dtt_ref, e_ref, bx, bc, ax, ac):
    dtx = _expand_heads(dtr_ref[...], e_ref[...])
    dt = _softplus(dtx + bx)
    a_x = -jnp.exp(ax)
    acs = _cumsum_rows(dt * a_x)
    acs_c = _cumsum_lanes(_softplus(dtt_ref[...] + bc) * (-jnp.exp(ac)))
    return dtx, dt, a_x, acs, acs_c


def _pair_decay(acs_slab, acs_c, h0, causal, left):
    other = pltpu.roll(acs_slab, HEAD_DIM, 1)
    col0 = jnp.where(left, acs_slab, other)
    col1 = jnp.where(left, other, acs_slab)
    l0 = jnp.exp(jnp.where(causal, col0 - acs_c[h0:h0 + 1, :], NEG_INF))
    l1 = jnp.exp(jnp.where(causal, col1 - acs_c[h0 + 1:h0 + 2, :], NEG_INF))
    return l0, l1


def _ssd_fwd(xc, dt_raw, dt_raw_t, bias_x, bias_c, alog_x, alog_c, dskip_x, e_mat):
    t_dim = xc.shape[0]
    nc = t_dim // CHUNK

    def body(xc_ref, dtr_ref, dtt_ref, bx_ref, bc_ref, ax_ref, ac_ref, dk_ref, e_ref,
             y_ref, st_ref, state):
        @pl.when(pl.program_id(0) == 0)
        def _():
            state[...] = jnp.zeros_like(state)

        _, dt, _, acs, acs_c = _ssd_common(dtr_ref, dtt_ref, e_ref, bx_ref[...], bc_ref[...],
                                           ax_ref[...], ac_ref[...])
        st_ref[0] = state[...]
        last = acs[CHUNK - 1:CHUNK, :]
        xs32 = xc_ref[:, 0:D_INNER].astype(F32)
        xdt = xs32 * dt
        xdt16 = xdt.astype(BF16)
        xdte16 = (xdt * jnp.exp(last - acs)).astype(BF16)
        ea = jnp.exp(acs)
        cd = jnp.exp(last)
        skip = dk_ref[...] * xs32
        causal = (lax.broadcasted_iota(jnp.int32, (CHUNK, CHUNK), 0)
                  >= lax.broadcasted_iota(jnp.int32, (CHUNK, CHUNK), 1))
        left = lax.broadcasted_iota(jnp.int32, (CHUNK, PAIR), 1) < HEAD_DIM
        for g in range(N_GROUPS):
            gl = slice(g * GROUP_LANES, (g + 1) * GROUP_LANES)
            bg = xc_ref[:, D_INNER + g * D_STATE:D_INNER + (g + 1) * D_STATE]
            cg = xc_ref[:, D_INNER + (N_GROUPS + g) * D_STATE:D_INNER + (N_GROUPS + g + 1) * D_STATE]
            cb = lax.dot_general(cg, bg, _DIMS["nt"], preferred_element_type=F32)
            hprev = state[:, gl]
            ch = jnp.dot(cg, hprev.astype(BF16), preferred_element_type=F32)
            for j in range(HEADS_PER_GROUP // 2):
                pl_ = slice(g * GROUP_LANES + j * PAIR, g * GROUP_LANES + (j + 1) * PAIR)
                h0 = g * HEADS_PER_GROUP + 2 * j
                l0, l1 = _pair_decay(acs[:, pl_], acs_c, h0, causal, left)
                lhs = jnp.concatenate([(cb * l0).astype(BF16), (cb * l1).astype(BF16)], axis=1)
                xp = xdt16[:, pl_]
                zero = jnp.zeros_like(xp)
                rhs = jnp.concatenate([jnp.where(left, xp, zero), jnp.where(left, zero, xp)], axis=0)
                ydiag = jnp.dot(lhs, rhs, preferred_element_type=F32)
                y_ref[:, pl_] = ydiag + ch[:, j * PAIR:(j + 1) * PAIR] * ea[:, pl_] + skip[:, pl_]
            s_new = lax.dot_general(bg, xdte16[:, gl], _DIMS["tn"], preferred_element_type=F32)
            state[:, gl] = hprev * cd[:, gl] + s_new

    rows = lambda w: pl.BlockSpec((CHUNK, w), lambda c: (c, 0))
    return _pcall(
        body, name="ssd_fwd",
        out_shape=(jax.ShapeDtypeStruct((t_dim, D_INNER), F32),
                   jax.ShapeDtypeStruct((nc, D_STATE, D_INNER), F32)),
        grid=(nc,),
        in_specs=[rows(CONV_DIM), rows(128), pl.BlockSpec((N_HEADS, CHUNK), lambda c: (0, c)),
                  _const_spec((1, D_INNER)), _const_spec((N_HEADS, 1)),
                  _const_spec((1, D_INNER)), _const_spec((N_HEADS, 1)), _const_spec((1, D_INNER)),
                  _const_spec((128, D_INNER))],
        out_specs=(rows(D_INNER), pl.BlockSpec((1, D_STATE, D_INNER), lambda c: (c, 0, 0))),
        scratch_shapes=[pltpu.VMEM((D_STATE, D_INNER), F32)],
        compiler_params=_params("arbitrary"),
    )(xc, dt_raw, dt_raw_t, bias_x, bias_c, alog_x, alog_c, dskip_x, e_mat)


def _ssd_bwd(xc, dt_raw, dt_raw_t, bias_x, bias_c, alog_x, alog_c, dskip_x, bias_r, alog_r,
             e_mat, et_mat, states, dy):
    t_dim = xc.shape[0]
    nc = t_dim // CHUNK

    def body(xc_ref, dtr_ref, dtt_ref, bx_ref, bc_ref, ax_ref, ac_ref, dk_ref, br_ref, ar_ref,
             e_ref, et_ref, st_ref, dy_ref,
             dxc_ref, ddt_ref, small_ref, dstate, dacs_ref, dxdt_ref, acc_x, acc_r):
        step = pl.program_id(0)

        @pl.when(step == 0)
        def _():
            dstate[...] = jnp.zeros_like(dstate)
            acc_x[...] = jnp.zeros_like(acc_x)
            acc_r[...] = jnp.zeros_like(acc_r)

        dtx, dt, a_x, acs, acs_c = _ssd_common(dtr_ref, dtt_ref, e_ref, bx_ref[...], bc_ref[...],
                                               ax_ref[...], ac_ref[...])
        last = acs[CHUNK - 1:CHUNK, :]
        xs32 = xc_ref[:, 0:D_INNER].astype(F32)
        xdt = xs32 * dt
        xdt16 = xdt.astype(BF16)
        dte = jnp.exp(last - acs)
        xdte = xdt * dte
        xdte16 = xdte.astype(BF16)
        cd = jnp.exp(last)
        dyv = dy_ref[...]
        dy16 = dyv.astype(BF16)
        dye = dyv * jnp.exp(acs)
        dye16 = dye.astype(BF16)
        causal = (lax.broadcasted_iota(jnp.int32, (CHUNK, CHUNK), 0)
                  >= lax.broadcasted_iota(jnp.int32, (CHUNK, CHUNK), 1))
        left = lax.broadcasted_iota(jnp.int32, (CHUNK, PAIR), 1) < HEAD_DIM
        lane_id = lax.broadcasted_iota(jnp.int32, (CHUNK, 128), 1)
        row_id = lax.broadcasted_iota(jnp.int32, (CHUNK, 128), 0)
        is_last_row = lax.broadcasted_iota(jnp.int32, (CHUNK, 1), 0) == CHUNK - 1
        dacs_cols = jnp.zeros((CHUNK, 128), F32)
        dacs_rows = jnp.zeros((CHUNK, 128), F32)
        for g in range(N_GROUPS):
            gl = slice(g * GROUP_LANES, (g + 1) * GROUP_LANES)
            b_lanes = slice(D_INNER + g * D_STATE, D_INNER + (g + 1) * D_STATE)
            c_lanes = slice(D_INNER + (N_GROUPS + g) * D_STATE, D_INNER + (N_GROUPS + g + 1) * D_STATE)
            bg = xc_ref[:, b_lanes]
            cg = xc_ref[:, c_lanes]
            cb = lax.dot_general(cg, bg, _DIMS["nt"], preferred_element_type=F32)
            hprev = st_ref[0, :, gl]
            hp16 = hprev.astype(BF16)
            dhn = dstate[:, gl]
            dhn16 = dhn.astype(BF16)
            ch = jnp.dot(cg, hp16, preferred_element_type=F32)
            gmat = jnp.dot(bg, dhn16, preferred_element_type=F32)
            gx = gmat * xdte[:, gl]
            dlast = jnp.sum(gx, axis=0, keepdims=True) + cd[:, gl] * jnp.sum(dhn * hprev, axis=0, keepdims=True)
            dacs_ref[:, gl] = dye[:, gl] * ch - gx + jnp.where(is_last_row, dlast, 0.0)
            dc_acc = lax.dot_general(dye16[:, gl], hp16, _DIMS["nt"], preferred_element_type=F32)
            db_acc = lax.dot_general(xdte16[:, gl], dhn16, _DIMS["nt"], preferred_element_type=F32)
            dstate[:, gl] = dhn * cd[:, gl] + lax.dot_general(cg, dye16[:, gl], _DIMS["tn"],
                                                             preferred_element_type=F32)
            dcb = jnp.zeros((CHUNK, CHUNK), F32)
            for j in range(HEADS_PER_GROUP // 2):
                pl_ = slice(g * GROUP_LANES + j * PAIR, g * GROUP_LANES + (j + 1) * PAIR)
                h0 = g * HEADS_PER_GROUP + 2 * j
                l0, l1 = _pair_decay(acs[:, pl_], acs_c, h0, causal, left)
                m0, m1 = cb * l0, cb * l1
                lhs = jnp.concatenate([m0.astype(BF16), m1.astype(BF16)], axis=1)
                dyp = dy16[:, pl_]
                zero = jnp.zeros_like(dyp)
                both = lax.dot_general(lhs, dyp, _DIMS["tn"], preferred_element_type=F32)
                dxdt_ref[:, pl_] = (jnp.where(left, both[0:CHUNK, :], both[CHUNK:, :])
                                    + gmat[:, j * PAIR:(j + 1) * PAIR] * dte[:, pl_])
                lhs2 = jnp.concatenate([jnp.where(left, dyp, zero), jnp.where(left, zero, dyp)], axis=0)
                dm = lax.dot_general(lhs2, xdt16[:, pl_], _DIMS["nt"], preferred_element_type=F32)
                dm0, dm1 = dm[0:CHUNK, :], dm[CHUNK:, :]
                dcb = dcb + dm0 * l0 + dm1 * l1
                ds0, ds1 = dm0 * m0, dm1 * m1
                dacs_cols = jnp.where(lane_id == h0, jnp.sum(ds0, axis=1, keepdims=True), dacs_cols)
                dacs_cols = jnp.where(lane_id == h0 + 1, jnp.sum(ds1, axis=1, keepdims=True), dacs_cols)
                dacs_rows = jnp.where(row_id == h0, jnp.sum(ds0, axis=0, keepdims=True), dacs_rows)
                dacs_rows = jnp.where(row_id == h0 + 1, jnp.sum(ds1, axis=0, keepdims=True), dacs_rows)
            dcb16 = dcb.astype(BF16)
            dxc_ref[:, c_lanes] = (dc_acc + jnp.dot(dcb16, bg, preferred_element_type=F32)).astype(BF16)
            dxc_ref[:, b_lanes] = (db_acc + lax.dot_general(dcb16, cg, _DIMS["tn"],
                                                           preferred_element_type=F32)).astype(BF16)
        dxdt = dxdt_ref[...]
        dxc_ref[:, 0:D_INNER] = (dxdt * dt + dk_ref[...] * dyv).astype(BF16)
        dadt = _rev_cumsum_rows(dacs_ref[...])
        ddraw_x = (dxdt * xs32 + dadt * a_x) * _sigmoid(dtx + bx_ref[...])
        acc_x[0:1, :] += jnp.sum(dadt * dt, axis=0, keepdims=True) * a_x
        acc_x[1:2, :] += jnp.sum(ddraw_x, axis=0, keepdims=True)
        acc_x[2:3, :] += jnp.sum(dyv * xs32, axis=0, keepdims=True)
        a_r = -jnp.exp(ar_ref[...])
        pre_r = dtr_ref[...] + br_ref[...]
        dadt_r = _rev_cumsum_rows(dacs_cols - dacs_rows.T)
        ddraw_r = jnp.where(lane_id < N_HEADS, dadt_r * a_r * _sigmoid(pre_r), 0.0)
        acc_r[0:1, :] += jnp.where(lane_id[0:1, :] < N_HEADS,
                                   jnp.sum(dadt_r * _softplus(pre_r), axis=0, keepdims=True) * a_r, 0.0)
        acc_r[1:2, :] += jnp.sum(ddraw_r, axis=0, keepdims=True)
        ddt_ref[...] = ddraw_r + _reduce_heads(ddraw_x, et_ref[...], 2)

        @pl.when(step == nc - 1)
        def _():
            small_ref[...] = acc_r[...] + _reduce_heads(acc_x[...], et_ref[...], 3)

    rev = lambda w: pl.BlockSpec((CHUNK, w), lambda c: (nc - 1 - c, 0))
    return _pcall(
        body, name="ssd_bwd",
        out_shape=(jax.ShapeDtypeStruct((t_dim, CONV_DIM), BF16),
                   jax.ShapeDtypeStruct((t_dim, 128), F32),
                   jax.ShapeDtypeStruct((8, 128), F32)),
        grid=(nc,),
        in_specs=[rev(CONV_DIM), rev(128), pl.BlockSpec((N_HEADS, CHUNK), lambda c: (0, nc - 1 - c)),
                  _const_spec((1, D_INNER)), _const_spec((N_HEADS, 1)),
                  _const_spec((1, D_INNER)), _const_spec((N_HEADS, 1)), _const_spec((1, D_INNER)),
                  _const_spec((1, 128)), _const_spec((1, 128)),
                  _const_spec((128, D_INNER)), _const_spec((D_INNER, 128)),
                  pl.BlockSpec((1, D_STATE, D_INNER), lambda c: (nc - 1 - c, 0, 0)),
                  rev(D_INNER)],
        out_specs=(rev(CONV_DIM), rev(128), _const_spec((8, 128))),
        scratch_shapes=[pltpu.VMEM((D_STATE, D_INNER), F32), pltpu.VMEM((CHUNK, D_INNER), F32),
                        pltpu.VMEM((CHUNK, D_INNER), F32), pltpu.VMEM((8, D_INNER), F32),
                        pltpu.VMEM((8, 128), F32)],
        compiler_params=_params("arbitrary"),
    )(xc, dt_raw, dt_raw_t, bias_x, bias_c, alog_x, alog_c, dskip_x, bias_r, alog_r,
      e_mat, et_mat, states, dy)


_GATE_GROUP = D_INNER // N_GROUPS


def _gate_fwd(y, z, g, *, tb=256):
    t_dim = y.shape[0]

    def body(y_ref, z_ref, g_ref, o_ref):
        for gi in range(N_GROUPS):
            lanes = slice(gi * _GATE_GROUP, (gi + 1) * _GATE_GROUP)
            zv = z_ref[:, lanes].astype(F32)
            wv = y_ref[:, lanes] * (zv * _sigmoid(zv))
            o_ref[:, lanes] = (wv * _rms(wv) * g_ref[:, lanes]).astype(BF16)

    return _pcall(
        body, name="gate_fwd", out_shape=jax.ShapeDtypeStruct((t_dim, D_INNER), BF16),
        grid=(t_dim // tb,),
        in_specs=[_row_spec(tb, D_INNER), _row_spec(tb, D_INNER), _const_spec((1, D_INNER))],
        out_specs=_row_spec(tb, D_INNER), compiler_params=_params("parallel"),
    )(y, z, g)


def _gate_bwd(dyn, y, z, g, *, tb=256):
    t_dim = y.shape[0]

    def body(d_ref, y_ref, z_ref, g_ref, dy_ref, dz_ref, dg_ref):
        @pl.when(pl.program_id(0) == 0)
        def _():
            dg_ref[...] = jnp.zeros_like(dg_ref)

        for gi in range(N_GROUPS):
            lanes = slice(gi * _GATE_GROUP, (gi + 1) * _GATE_GROUP)
            zv = z_ref[:, lanes].astype(F32)
            sg = _sigmoid(zv)
            sz = zv * sg
            yv = y_ref[:, lanes]
            wv = yv * sz
            r = _rms(wv)
            what = wv * r
            dv = d_ref[:, lanes]
            dwhat = dv * g_ref[:, lanes]
            dw = r * (dwhat - what * jnp.mean(dwhat * what, axis=-1, keepdims=True))
            dg_ref[0:1, lanes] += jnp.sum(dv * what, axis=0, keepdims=True)
            dy_ref[:, lanes] = dw * sz
            dz_ref[:, lanes] = (dw * yv * (sg * (1.0 + zv * (1.0 - sg)))).astype(BF16)

    return _pcall(
        body, name="gate_bwd",
        out_shape=(jax.ShapeDtypeStruct((t_dim, D_INNER), F32),
                   jax.ShapeDtypeStruct((t_dim, D_INNER), BF16),
                   jax.ShapeDtypeStruct((8, D_INNER), F32)),
        grid=(t_dim // tb,),
        in_specs=[_row_spec(tb, D_INNER), _row_spec(tb, D_INNER), _row_spec(tb, D_INNER),
                  _const_spec((1, D_INNER))],
        out_specs=(_row_spec(tb, D_INNER), _row_spec(tb, D_INNER), _const_spec((8, D_INNER))),
        compiler_params=_params("arbitrary"),
    )(dyn, y, z, g)


def _loss_head(h, g, target, *, tb=512):
    t_dim, d = h.shape

    def body(h_ref, g_ref, t_ref, dh_ref, dh16_ref, small_ref):
        @pl.when(pl.program_id(0) == 0)
        def _():
            small_ref[...] = jnp.zeros_like(small_ref)

        x = h_ref[...]
        r = _rms(x)
        xhat = x * r
        gv = g_ref[...]
        err = xhat * gv - t_ref[...]
        small_ref[1:2, :] += (0.5 / d) * jnp.sum(err * err, axis=0, keepdims=True)
        dyv = err * (1.0 / d)
        dxhat = dyv * gv
        dh = r * (dxhat - xhat * jnp.mean(dxhat * xhat, axis=-1, keepdims=True))
        dh_ref[...] = dh
        dh16_ref[...] = dh.astype(BF16)
        small_ref[0:1, :] += jnp.sum(dyv * xhat, axis=0, keepdims=True)

    return _pcall(
        body, name="loss_head",
        out_shape=(jax.ShapeDtypeStruct((t_dim, d), F32), jax.ShapeDtypeStruct((t_dim, d), BF16),
                   jax.ShapeDtypeStruct((8, d), F32)),
        grid=(t_dim // tb,),
        in_specs=[_row_spec(tb, d), _const_spec((1, d)), _row_spec(tb, d)],
        out_specs=(_row_spec(tb, d), _row_spec(tb, d), _const_spec((8, d))),
        compiler_params=_params("arbitrary"),
    )(h, g, target)


def _adamw(w, g, m, v, *, name, part=None, into=None):
    r_dim, c = w.shape
    rows = r_dim if part is None else r_dim // 2
    assert g.shape == (rows, c)
    tb = rows
    for cand in (512, 256, 128, 64, 32, 16, 8):
        if rows % cand == 0:
            tb = cand
            break
    first = 0 if part is None else part * (rows // tb)
    c1 = 1.0 / (1.0 - ADAM_B1 ** ADAM_STEP)
    c2 = 1.0 / (1.0 - ADAM_B2 ** ADAM_STEP)

    def body(w_ref, g_ref, m_ref, v_ref, *rest):
        d_ref, mo_ref, vo_ref = rest[-3:]
        gv = g_ref[...]
        mn = ADAM_B1 * m_ref[...] + (1.0 - ADAM_B1) * gv
        vn = ADAM_B2 * v_ref[...] + (1.0 - ADAM_B2) * (gv * gv)
        mo_ref[...] = mn
        vo_ref[...] = vn
        d_ref[...] = -ADAM_LR * ((mn * c1) / (jnp.sqrt(vn * c2) + ADAM_EPS) + ADAM_WD * w_ref[...])

    spec = pl.BlockSpec((tb, c), lambda i: (first + i, 0))
    sds = jax.ShapeDtypeStruct((r_dim, c), F32)
    in_specs = [spec, _row_spec(tb, c), spec, spec]
    operands = [w, g, m, v]
    aliases = {}
    if into is not None:
        in_specs += [_ANY] * 3
        operands += list(into)
        aliases = {4: 0, 5: 1, 6: 2}
    return _pcall(
        body, name=name, out_shape=(sds, sds, sds), grid=(rows // tb,),
        in_specs=in_specs, out_specs=(spec,) * 3, input_output_aliases=aliases,
        compiler_params=_params("parallel"),
    )(*operands)


def _pair_sum(grad, recv, place, *, name):
    s_dim, r_dim, c = grad.shape
    half = r_dim // 2
    tb = min(half, 256)
    per_half = half // tb

    def body(place_ref, a_ref, b_ref, o16_ref, o32_ref):
        s = a_ref[...] + b_ref[...]
        o16_ref[...] = s.astype(BF16)

        @pl.when(pl.program_id(1) == place_ref[1])
        def _():
            o32_ref[...] = s[0]

    grid_spec = pltpu.PrefetchScalarGridSpec(
        num_scalar_prefetch=1, grid=(per_half, s_dim),
        in_specs=[pl.BlockSpec((1, tb, c), lambda i, s, p: (s, p[0] * per_half + i, 0)),
                  pl.BlockSpec((1, tb, c), lambda i, s, p: (s, i, 0))],
        out_specs=(pl.BlockSpec((1, tb, c), lambda i, s, p: (s, i, 0)),
                   pl.BlockSpec((tb, c), lambda i, s, p: (i, 0))))
    return _pcall(
        body, name=name, grid_spec=grid_spec,
        out_shape=(jax.ShapeDtypeStruct((s_dim, half, c), BF16), jax.ShapeDtypeStruct((half, c), F32)),
        compiler_params=_params("parallel", "arbitrary"),
    )(place, grad, recv)


def _chip_sum(own, recv, *, name):
    r_dim, c = own.shape
    tb = min(r_dim, 256)

    def body(a_ref, b_ref, o_ref):
        s = a_ref[...]
        for k in range(1, N_CHIPS):
            s = s + b_ref[k].astype(F32)
        o_ref[...] = s

    return _pcall(
        body, name=name, out_shape=jax.ShapeDtypeStruct((r_dim, c), F32),
        grid=(r_dim // tb,),
        in_specs=[_row_spec(tb, c), pl.BlockSpec((N_CHIPS, tb, c), lambda i: (0, i, 0))],
        out_specs=_row_spec(tb, c), compiler_params=_params("parallel"),
    )(own, recv)


def _position():
    return lax.axis_index("x"), lax.axis_index("y"), lax.axis_index("c")


def _chip_peer(x, y, k):
    return x ^ (k >> 1), y ^ (k & 1)


_ANY = pl.BlockSpec(memory_space=pl.ANY)


def _all_gather_weights(shards):
    n = len(shards)
    hops = N_CHIPS - 1

    def body(*refs):
        srcs, outs = refs[:n], refs[n:2 * n]
        send_sems, recv_sems = refs[2 * n:]
        x, y, c = _position()
        me = 2 * x + y

        def over_ici(w, k, chip, to):
            return pltpu.make_async_remote_copy(
                src_ref=srcs[w].at[c], dst_ref=outs[w].at[chip, c],
                send_sem=send_sems.at[w, k - 1], recv_sem=recv_sems.at[w, k - 1],
                device_id=to, device_id_type=MESH)

        def over_d2d(w, k, chip, half):
            return pltpu.make_async_remote_copy(
                src_ref=outs[w].at[chip, half], dst_ref=outs[w].at[chip, half],
                send_sem=send_sems.at[w, hops + k - 1], recv_sem=recv_sems.at[w, hops + k - 1],
                device_id=(x, y, 1 - c), device_id_type=MESH)

        sends = []
        for w in range(n):
            for k in range(1, N_CHIPS):
                px, py = _chip_peer(x, y, k)
                cp = over_ici(w, k, me, (px, py, c))
                cp.start()
                sends.append(cp)
        for w in range(n):
            for k in range(1, N_CHIPS):
                px, py = _chip_peer(x, y, k)
                over_ici(w, k, 2 * px + py, (px, py, c)).wait_recv()
                cp = over_d2d(w, k, 2 * px + py, c)
                cp.start()
                sends.append(cp)
        for w in range(n):
            for k in range(1, N_CHIPS):
                px, py = _chip_peer(x, y, k)
                over_d2d(w, k, 2 * px + py, 1 - c).wait_recv()
        for cp in sends:
            cp.wait_send()

    return _pcall(
        body, name="gather_weights",
        out_shape=tuple(jax.ShapeDtypeStruct((N_CHIPS,) + s.shape, s.dtype) for s in shards),
        in_specs=[_ANY] * n, out_specs=(_ANY,) * n,
        scratch_shapes=[pltpu.SemaphoreType.DMA((n, 2 * hops)),
                        pltpu.SemaphoreType.DMA((n, 2 * hops))],
    )(*shards)


def _pair_copies(srcs, lands, send_sems, recv_sems):
    x, y, c = _position()
    copies = []
    for w in range(len(srcs)):
        half = srcs[w].shape[1] // 2
        copies.append(pltpu.make_async_remote_copy(
            src_ref=srcs[w].at[:, pl.ds((1 - c) * half, half), :], dst_ref=lands[w],
            send_sem=send_sems.at[w], recv_sem=recv_sems.at[w],
            device_id=(x, y, 1 - c), device_id_type=MESH))
    return copies


def _chip_copies(srcs, lands, send_sems, recv_sems):
    x, y, c = _position()
    copies = []
    for w in range(len(srcs)):
        for k in range(1, N_CHIPS):
            px, py = _chip_peer(x, y, k)
            i = w * (N_CHIPS - 1) + k - 1
            copies.append(pltpu.make_async_remote_copy(
                src_ref=srcs[w].at[2 * px + py], dst_ref=lands[w].at[k],
                send_sem=send_sems.at[i], recv_sem=recv_sems.at[i],
                device_id=(px, py, c), device_id_type=MESH))
    return copies


def _gather_copies(srcs, lands, send_sems, recv_sems):
    x, y, c = _position()
    me = 2 * x + y
    copies = []
    for w in range(len(srcs)):
        for k in range(1, N_CHIPS):
            px, py = _chip_peer(x, y, k)
            i = w * (N_CHIPS - 1) + k - 1
            copies.append(pltpu.make_async_remote_copy(
                src_ref=srcs[w].at[c], dst_ref=lands[w].at[me, c],
                send_sem=send_sems.at[i], recv_sem=recv_sems.at[i],
                device_id=(px, py, c), device_id_type=MESH))
    return copies


def _exchange(name, copies_of, n_copies, srcs, land_shapes):
    n = len(srcs)

    def body(*refs):
        copies = copies_of(refs[:n], refs[n:2 * n], refs[2 * n], refs[2 * n + 1])
        for cp in copies:
            cp.start()
        for cp in copies:
            cp.wait_recv()
        for cp in copies:
            cp.wait_send()

    return _pcall(
        body, name=name, out_shape=tuple(land_shapes),
        in_specs=[_ANY] * n, out_specs=(_ANY,) * n,
        scratch_shapes=[pltpu.SemaphoreType.DMA((n_copies,)), pltpu.SemaphoreType.DMA((n_copies,))],
    )(*srcs)


_HBM = pl.BlockSpec(memory_space=pltpu.HBM)
_SEM = pl.BlockSpec(memory_space=pltpu.SEMAPHORE)
_DATAFLOW = pltpu.SideEffectType.DATAFLOW_SIDE_EFFECTING


def _exchange_start(name, copies_of, n_copies, srcs, land_shapes):
    n = len(srcs)
    lands = [lax.empty(s.shape, s.dtype) for s in land_shapes]

    def body(*refs):
        for cp in copies_of(refs[:n], refs[n:2 * n], refs[2 * n], refs[2 * n + 1]):
            cp.start()
        refs[-1][...] = jnp.zeros_like(refs[-1])

    through = [pltpu.HBM(a.shape, a.dtype) for a in list(srcs) + lands]
    outs = _pcall(
        body, name=name,
        out_shape=(pltpu.SemaphoreType.DMA((n_copies,)), pltpu.SemaphoreType.DMA((n_copies,)),
                   *through, jax.ShapeDtypeStruct((8, 128), F32)),
        in_specs=[_HBM] * (2 * n),
        out_specs=(_SEM, _SEM, *([_HBM] * (2 * n)), pl.BlockSpec(memory_space=pltpu.VMEM)),
        input_output_aliases={i: 2 + i for i in range(2 * n)},
        compiler_params=pltpu.CompilerParams(has_side_effects=_DATAFLOW),
    )(*[pltpu.with_memory_space_constraint(a, pltpu.HBM) for a in list(srcs) + lands])
    return outs[:-1], outs[-1][0, 0]


def _exchange_wait(name, copies_of, state, after):
    send_sems, recv_sems, through = state[0], state[1], state[2:]
    n = len(through) // 2
    if after.ndim == 0:
        after = jnp.broadcast_to(after, (8, 128))
    after = pltpu.with_memory_space_constraint(after, pltpu.HBM)

    def body(*refs):
        for cp in copies_of(refs[:n], refs[n:2 * n], refs[2 * n], refs[2 * n + 1]):
            cp.wait_send()
            cp.wait_recv()

    outs = _pcall(
        body, name=name,
        out_shape=tuple(pltpu.HBM(a.shape, a.dtype) for a in through),
        in_specs=[_HBM] * (2 * n) + [_SEM, _SEM, _HBM], out_specs=tuple([_HBM] * (2 * n)),
        input_output_aliases={i: i for i in range(2 * n)},
        compiler_params=pltpu.CompilerParams(has_side_effects=_DATAFLOW),
    )(*through, send_sems, recv_sems, after)
    return outs[:n], outs[n:]


def _forward_halves(lands, *, name):
    n = len(lands)
    hops = N_CHIPS - 1

    def body(*refs):
        ins, outs = refs[:n], refs[n:2 * n]
        send_sems, recv_sems = refs[2 * n], refs[2 * n + 1]
        x, y, c = _position()
        copies = []
        for w in range(n):
            for k in range(1, N_CHIPS):
                px, py = _chip_peer(x, y, k)
                i = w * hops + k - 1
                copies.append(pltpu.make_async_remote_copy(
                    src_ref=ins[w].at[2 * px + py, c], dst_ref=outs[w].at[2 * px + py, c],
                    send_sem=send_sems.at[i], recv_sem=recv_sems.at[i],
                    device_id=(x, y, 1 - c), device_id_type=MESH))
        for cp in copies:
            cp.start()
        for cp in copies:
            cp.wait_recv()
        for cp in copies:
            cp.wait_send()

    return _pcall(
        body, name=name,
        out_shape=tuple(jax.ShapeDtypeStruct(a.shape, a.dtype) for a in lands),
        in_specs=[_ANY] * n, out_specs=(_ANY,) * n,
        input_output_aliases={i: i for i in range(n)},
        scratch_shapes=[pltpu.SemaphoreType.DMA((n * hops,)), pltpu.SemaphoreType.DMA((n * hops,))],
    )(*lands)


def _pair_lands(grads):
    return [jax.ShapeDtypeStruct((g.shape[0], g.shape[1] // 2, g.shape[2]), F32) for g in grads]


def _same_lands(parts):
    return [jax.ShapeDtypeStruct(p.shape, p.dtype) for p in parts]


def _pair_gather_halves(halves, *, name):
    n = len(halves)

    def body(*refs):
        srcs, outs = refs[:n], refs[n:2 * n]
        send_sems, recv_sems = refs[2 * n:]
        x, y, c = _position()
        sends = []
        for w in range(n):
            cp = pltpu.make_async_remote_copy(
                src_ref=srcs[w], dst_ref=outs[w],
                send_sem=send_sems.at[w], recv_sem=recv_sems.at[w],
                device_id=(x, y, 1 - c), device_id_type=MESH)
            cp.start()
            sends.append(cp)
        for cp in sends:
            cp.wait_recv()
        for cp in sends:
            cp.wait_send()

    theirs = _pcall(
        body, name=name,
        out_shape=tuple(jax.ShapeDtypeStruct(h.shape, F32) for h in halves),
        in_specs=[_ANY] * n, out_specs=(_ANY,) * n,
        scratch_shapes=[pltpu.SemaphoreType.DMA((n,)), pltpu.SemaphoreType.DMA((n,))],
    )(*halves)
    my_c = lax.axis_index("c")
    whole = []
    for mine, other in zip(halves, theirs):
        both = jnp.stack([other, other])
        both = lax.dynamic_update_index_in_dim(both, mine, my_c, axis=0)
        whole.append(both.reshape(2 * mine.shape[0], mine.shape[1]))
    return whole


def _all_reduce_small(packed, *, name, sum_row0):
    r_dim, c = packed.shape

    def body(src_ref, out_ref, recv_ref, send_sems, recv_sems):
        x, y, c_ = _position()
        me = 4 * x + 2 * y + c_
        recv_ref[0] = src_ref[...]
        sends = []
        for k in range(1, N_DEV):
            peer = (x ^ (k >> 2), y ^ ((k >> 1) & 1), c_ ^ (k & 1))
            cp = pltpu.make_async_remote_copy(
                src_ref=src_ref, dst_ref=recv_ref.at[k],
                send_sem=send_sems.at[k - 1], recv_sem=recv_sems.at[k - 1],
                device_id=peer, device_id_type=MESH)
            cp.start()
            sends.append(cp)
        for cp in sends:
            cp.wait_recv()
        total = recv_ref[me]
        for d in range(1, N_DEV):
            total = total + recv_ref[d ^ me]
        if sum_row0:
            row0 = jnp.sum(total[0:1, :], axis=1, keepdims=True)
            rid = lax.broadcasted_iota(jnp.int32, total.shape, 0)
            total = jnp.where(rid == 0, row0, total)
        out_ref[...] = total
        for cp in sends:
            cp.wait_send()

    return _pcall(
        body, name=name, out_shape=jax.ShapeDtypeStruct((r_dim, c), F32),
        in_specs=[pl.BlockSpec(memory_space=pltpu.VMEM)],
        out_specs=pl.BlockSpec(memory_space=pltpu.VMEM),
        scratch_shapes=[pltpu.VMEM((N_DEV, r_dim, c), F32),
                        pltpu.SemaphoreType.DMA((N_DEV - 1,)), pltpu.SemaphoreType.DMA((N_DEV - 1,))],
    )(packed)


def _pad_lanes(v, width):
    return jnp.pad(v, ((0, 0), (0, width - v.shape[1])))


def _rows_1024(v):
    flat = v.reshape(-1)
    pad = (-flat.shape[0]) % D_MODEL
    return jnp.pad(flat, (0, pad)).reshape(-1, D_MODEL)


def _local_step(xs, target, pw, w1_0, w2_0, fetch, reduce_start, reduce_midway,
                conv_w, conv_b, gate_g,
                norm_mix_g, norm_mlp_g, pool_b, pool_scale, ssm_dt_bias, ssm_a_log, ssm_d, final_g):
    bias_r = _pad_lanes(ssm_dt_bias, 128)
    alog_r = _pad_lanes(ssm_a_log, 128)
    bias_x = jnp.repeat(ssm_dt_bias, HEAD_DIM, axis=1)
    alog_x = jnp.repeat(ssm_a_log, HEAD_DIM, axis=1)
    dskip_x = jnp.repeat(ssm_d, HEAD_DIM, axis=1)
    bias_c = ssm_dt_bias.reshape(N_HEADS, 1)
    alog_c = ssm_a_log.reshape(N_HEADS, 1)
    e_mat = _head_lane_matrix()

    g_mix0, g_mix1 = norm_mix_g[0:1], norm_mix_g[1:2]
    g_mlp0, g_mlp1 = norm_mlp_g[0:1], norm_mlp_g[1:2]
    fg = final_g.reshape(1, D_MODEL)

    h1 = _pool_fwd(xs, g_mix0, pw, pool_b, pool_scale)
    hm0 = _rmsnorm_fwd(h1, g_mlp0, name="norm_mlp0")
    u0 = _matmul(hm0, w1_0, "nn", name="mlp0_up", out_dtype=BF16)
    h2 = _matmul(u0, w2_0, "nn", name="mlp0_down", a_relu2=True, add=h1)

    w_z, w_xbc, w_dt = fetch("in_proj", h2)
    hn1 = _rmsnorm_fwd(h2, g_mix1, name="norm_mix1")
    z = _matmul(hn1, w_z, "nn", name="in_proj_z", out_dtype=BF16)
    xbc = _matmul(hn1, w_xbc, "nn", name="in_proj_xbc", out_dtype=BF16)
    dt_raw = _matmul(hn1, w_dt, "nn", name="in_proj_dt")
    dt_raw_t = dt_raw[:, :N_HEADS].T
    xc = _conv_fwd(xbc, conv_w, conv_b)
    wout, w1_1, w2_1 = fetch("rest", xc)
    y, states = _ssd_fwd(xc, dt_raw, dt_raw_t, bias_x, bias_c, alog_x, alog_c, dskip_x, e_mat)
    yn = _gate_fwd(y, z, gate_g)
    h3 = _matmul(yn, wout, "nn", name="out_proj", add=h2)
    hm1 = _rmsnorm_fwd(h3, g_mlp1, name="norm_mlp1")
    u1 = _matmul(hm1, w1_1, "nn", name="mlp1_up", out_dtype=BF16)
    h4 = _matmul(u1, w2_1, "nn", name="mlp1_down", a_relu2=True, add=h3)

    dh4, dh4_16, small_final = _loss_head(h4, fg, target)

    def mlp_bwd_weights(dh_out16, hm, u, w2_i, tag):
        du = _matmul(dh_out16, w2_i, "nt", name=tag + "_du", out_dtype=BF16, relu2_grad_of=u)
        dw2 = _matmul(u, dh_out16, "tn", name=tag + "_dw2", a_relu2=True)
        dw1 = _matmul(hm, du, "tn", name=tag + "_dw1", out_col_shards=N_CHIPS)
        return du, dw1, dw2.reshape(N_CHIPS, D_FF // N_CHIPS, D_MODEL)

    def mlp_bwd_input(du, dh_out, h_in, w1_i, g_i, tag):
        dhm = _matmul(du, w1_i, "nt", name=tag + "_dhm")
        return _rmsnorm_bwd(dhm, h_in, g_i, dh_out, name=tag + "_norm_bwd")

    du1, dw1_1, dw2_1 = mlp_bwd_weights(dh4_16, hm1, u1, w2_1, "mlp1")
    dh3, dh3_16, dg_mlp1 = mlp_bwd_input(du1, dh4, h3, w1_1, g_mlp1, "mlp1")

    dyn = _matmul(dh3_16, wout, "nt", name="out_proj_dyn")
    dwout = _matmul(yn, dh3_16, "tn", name="out_proj_dw").reshape(N_CHIPS, D_INNER // N_CHIPS, D_MODEL)
    behind = reduce_start("mlp1_out", [dw1_1, dw2_1, dwout])
    dy, dz, dg_gate = _gate_bwd(dyn, y, z, gate_g + behind)
    behind = reduce_midway("mlp1_out", dz)
    dxc, ddt_raw, small_ssd = _ssd_bwd(xc, dt_raw, dt_raw_t, bias_x, bias_c, alog_x, alog_c, dskip_x + behind,
                                       bias_r, alog_r, e_mat, e_mat.T, states, dy)
    dv, dconv = _conv_bwd_act(xbc, dxc, conv_w, conv_b)
    dxbc = _conv_bwd_in(dv, conv_w)
    dhn1 = _matmul(ddt_raw, w_dt, "nt", name="in_proj_dt_dh")
    dhn1 = _matmul(dz, w_z, "nt", name="in_proj_z_dh", add=dhn1)
    dhn1 = _matmul(dxbc, w_xbc, "nt", name="in_proj_xbc_dh", add=dhn1)
    dw_z = _matmul(hn1, dz, "tn", name="in_proj_z_dw")
    dw_xbc = _matmul(hn1, dxbc, "tn", name="in_proj_xbc_dw")
    dw_dt = _matmul(hn1, ddt_raw, "tn", name="in_proj_dt_dw")
    dwin = jnp.concatenate([dw_z, dw_xbc, dw_dt[:, :N_HEADS]], axis=1)
    dwin = jnp.transpose(dwin.reshape(D_MODEL, N_CHIPS, IN_PROJ_DIM // N_CHIPS), (1, 0, 2))
    behind = reduce_start("in_proj", [dwin])
    dh2, dh2_16, dg_mix1 = _rmsnorm_bwd(dhn1, h2, g_mix1 + behind, dh3, name="norm_mix1_bwd")
    behind = reduce_midway("in_proj", dh2_16)

    du0, dw1_0, dw2_0 = mlp_bwd_weights(dh2_16, hm0, u0, w2_0, "mlp0")
    dh1, _, dg_mlp0 = mlp_bwd_input(du0, dh2, h1, w1_0, g_mlp0 + behind, "mlp0")
    dx, dpw, small_pool = _pool_bwd(xs, g_mix0, pw, pool_b, pool_scale, dh1)
    dpw = jnp.transpose(dpw.reshape(4, N_CHIPS, POOL_GROUP // N_CHIPS, POOL_GROUP), (1, 0, 2, 3))
    dpw = dpw.reshape(N_CHIPS, 4 * (POOL_GROUP // N_CHIPS), POOL_GROUP)

    big = [dpw, dw1_0, dw2_0]
    rows = [
        small_final[1:2],
        small_final[0:1],
        small_pool[0:1], dg_mix1[0:1],
        dg_mlp0[0:1], dg_mlp1[0:1],
        small_pool[1:2], small_pool[2:3],
        _pad_lanes(small_ssd[0:3], D_MODEL),
        _rows_1024(dg_gate[0:1]),
        _rows_1024(dconv[0:CONV_K]),
        _rows_1024(dconv[CONV_K:CONV_K + 1]),
    ]
    return dx, big, rows


def kernel(x, norm_mix_g, norm_mlp_g, pool_w, pool_b, pool_scale, ssm_w_in, ssm_conv_w, ssm_conv_b, ssm_dt_bias, ssm_a_log, ssm_d, ssm_norm_g, ssm_w_out, mlp_w1, mlp_w2, final_g, loss_target, m_norm_mix_g, m_norm_mlp_g, m_pool_w, m_pool_b, m_pool_scale, m_ssm_w_in, m_ssm_conv_w, m_ssm_conv_b, m_ssm_dt_bias, m_ssm_a_log, m_ssm_d, m_ssm_norm_g, m_ssm_w_out, m_mlp_w1, m_mlp_w2, m_final_g, v_norm_mix_g, v_norm_mlp_g, v_pool_w, v_pool_b, v_pool_scale, v_ssm_w_in, v_ssm_conv_w, v_ssm_conv_b, v_ssm_dt_bias, v_ssm_a_log, v_ssm_d, v_ssm_norm_g, v_ssm_w_out, v_mlp_w1, v_mlp_w2, v_final_g):
    t_dim = x.shape[1]
    xs = x[0]
    target = loss_target[0]
    my_x, my_y, my_c = _position()
    my_chip = 2 * my_x + my_y

    def halves(w):
        return w.astype(BF16).reshape((2, w.shape[0] // 2) + w.shape[1:])

    def whole(gathered, own_shard):
        g = lax.dynamic_update_index_in_dim(gathered, own_shard, my_chip, axis=0)
        return g.reshape((N_CHIPS, 2 * g.shape[2]) + g.shape[3:])

    def up_weight(g):
        return jnp.transpose(g, (1, 0, 2)).reshape(D_MODEL, D_FF)

    def gather_lands(own):
        return [jax.ShapeDtypeStruct((N_CHIPS,) + s.shape, BF16) for s in own]

    early_own = [halves(pool_w[0]), halves(mlp_w1[0]), halves(mlp_w2[0])]
    early = _all_gather_weights(early_own)
    g_pool, g_w1_0, g_w2_0 = [whole(g, o) for g, o in zip(early, early_own)]
    pw = jnp.transpose(g_pool, (1, 0, 2, 3)).reshape(4, POOL_GROUP, POOL_GROUP)
    w1_0 = up_weight(g_w1_0)
    w2_0 = g_w2_0.reshape(D_FF, D_MODEL)

    conv_w = jnp.zeros((CONV_K, CONV_DIM), F32)
    conv_b = jnp.zeros((1, CONV_DIM), F32)
    gate_g = jnp.zeros((1, D_INNER), F32)
    conv_w = lax.dynamic_update_slice(conv_w, ssm_conv_w[0], (0, my_chip * (CONV_DIM // N_CHIPS)))
    conv_b = lax.dynamic_update_slice(conv_b, ssm_conv_b, (0, my_chip * (CONV_DIM // N_CHIPS)))
    gate_g = lax.dynamic_update_slice(gate_g, ssm_norm_g, (0, my_chip * (D_INNER // N_CHIPS)))
    vec_rows = jnp.concatenate([_rows_1024(conv_w), _rows_1024(conv_b), _rows_1024(gate_g)], axis=0)
    vec_rows = jnp.pad(vec_rows, ((0, (-vec_rows.shape[0]) % 8), (0, 0)))
    vec_rows = _all_reduce_small(vec_rows * 0.5, name="gather_vectors", sum_row0=False)
    conv_w = vec_rows[0:12].reshape(CONV_K, CONV_DIM)
    conv_b = vec_rows[12:15].reshape(1, CONV_DIM)
    gate_g = vec_rows[15:17].reshape(1, D_INNER)

    in_own = [halves(ssm_w_in[0])]
    rest_own = [halves(ssm_w_out[0]), halves(mlp_w1[1]), halves(mlp_w2[1])]
    early, vec_rows, in_own = lax.optimization_barrier((early, vec_rows, in_own))
    fetches = {}
    fetches["in_proj"], behind_gather = _exchange_start(
        "gather_in_proj_start", _gather_copies, len(in_own) * (N_CHIPS - 1), in_own, gather_lands(in_own))

    def fetch(what, after):
        if what == "in_proj":
            own_thru, landed = _exchange_wait("gather_in_proj_wait", _gather_copies, fetches["in_proj"], after)
            landed = _forward_halves(landed, name="forward_in_proj")
            landed, rest = lax.optimization_barrier((landed, rest_own))
            fetches["rest"], behind = _exchange_start(
                "gather_rest_start", _gather_copies, len(rest) * (N_CHIPS - 1), rest, gather_lands(rest))
            win = jnp.transpose(whole(landed[0], own_thru[0]), (1, 0, 2)).reshape(D_MODEL, IN_PROJ_DIM)
            w_dt = _pad_lanes(win[:, D_INNER + CONV_DIM:], 128) + behind.astype(BF16)
            return win[:, :D_INNER], win[:, D_INNER:D_INNER + CONV_DIM], w_dt
        own_thru, landed = _exchange_wait("gather_rest_wait", _gather_copies, fetches["rest"], after)
        landed = _forward_halves(landed, name="forward_rest")
        g_wout, g_w1_1, g_w2_1 = [whole(g, o) for g, o in zip(landed, own_thru)]
        return g_wout.reshape(D_INNER, D_MODEL), up_weight(g_w1_1), g_w2_1.reshape(D_FF, D_MODEL)

    place = jnp.stack([my_c, my_chip]).astype(jnp.int32)
    waves = {}

    def reduce_start(wave, grads):
        waves[wave] = {}
        waves[wave]["pair"], behind = _exchange_start(
            "pair_%s_start" % wave, _pair_copies, len(grads), grads, _pair_lands(grads))
        return behind

    def reduce_midway(wave, after):
        st = waves[wave]
        grads, recv = _exchange_wait("pair_%s_wait" % wave, _pair_copies, st["pair"], after)
        sums = [_pair_sum(g, r, place, name="pair_sum_%s_%d" % (wave, i))
                for i, (g, r) in enumerate(zip(grads, recv))]
        st["f32"] = [s32 for _, s32 in sums]
        b16 = [s16 for s16, _ in sums]
        st["chip"], behind = _exchange_start(
            "chip_%s_start" % wave, _chip_copies, len(b16) * (N_CHIPS - 1), b16, _same_lands(b16))
        return behind

    def reduce_finish(wave, after):
        st = waves[wave]
        _, got = _exchange_wait("chip_%s_wait" % wave, _chip_copies, st["chip"], after)
        return [_chip_sum(s32, r, name="chip_sum_%s_%d" % (wave, i))
                for i, (s32, r) in enumerate(zip(st["f32"], got))]

    dx, big0, rows = _local_step(xs, target, pw, w1_0, w2_0, fetch, reduce_start, reduce_midway,
                                 conv_w, conv_b, gate_g,
                                 norm_mix_g + behind_gather, norm_mlp_g, pool_b, pool_scale,
                                 ssm_dt_bias, ssm_a_log, ssm_d, final_g)

    behind = reduce_start("layer0", big0)
    small = jnp.concatenate(rows, axis=0)
    small = jnp.pad(small, ((0, (-small.shape[0]) % 8), (0, 0))) + behind
    small = _all_reduce_small(small, name="all_reduce_small", sum_row0=True)
    behind = reduce_midway("layer0", small)
    h_w1_1, h_w2_1, h_wout = reduce_finish("mlp1_out", behind)
    (h_win,) = reduce_finish("in_proj", behind)
    g_w1_1, g_w2_1, g_wout_s, g_win_s = _pair_gather_halves([h_w1_1, h_w2_1, h_wout, h_win],
                                                            name="pair_gather_layer1")
    loss = small[0, 0]
    g_final = small[1]
    g_norm_mix = small[2:4]
    g_norm_mlp = small[4:6]
    g_pool_b, g_pool_scale = small[6:7], small[7:8]
    g_alog, g_dtb, g_dsk = small[8:9, :N_HEADS], small[9:10, :N_HEADS], small[10:11, :N_HEADS]
    g_gate_full = small[11:13].reshape(1, D_INNER)
    g_convw_full = small[13:25].reshape(CONV_K, CONV_DIM)
    g_convb_full = small[25:28].reshape(1, CONV_DIM)
    g_gate = lax.dynamic_slice_in_dim(g_gate_full, my_chip * (D_INNER // N_CHIPS), D_INNER // N_CHIPS, axis=1)
    g_convw = lax.dynamic_slice_in_dim(g_convw_full, my_chip * (CONV_DIM // N_CHIPS), CONV_DIM // N_CHIPS, axis=1)
    g_convb = lax.dynamic_slice_in_dim(g_convb_full, my_chip * (CONV_DIM // N_CHIPS), CONV_DIM // N_CHIPS, axis=1)

    grads = {
        "norm_mix_g": g_norm_mix, "norm_mlp_g": g_norm_mlp,
        "pool_b": g_pool_b, "pool_scale": g_pool_scale,
        "ssm_w_in": g_win_s.reshape(ssm_w_in.shape), "ssm_conv_w": g_convw.reshape(ssm_conv_w.shape),
        "ssm_conv_b": g_convb, "ssm_dt_bias": g_dtb, "ssm_a_log": g_alog, "ssm_d": g_dsk,
        "ssm_norm_g": g_gate, "ssm_w_out": g_wout_s.reshape(ssm_w_out.shape),
        "final_g": g_final,
    }
    weights = dict(norm_mix_g=norm_mix_g, norm_mlp_g=norm_mlp_g, pool_w=pool_w, pool_b=pool_b,
                   pool_scale=pool_scale, ssm_w_in=ssm_w_in, ssm_conv_w=ssm_conv_w, ssm_conv_b=ssm_conv_b,
                   ssm_dt_bias=ssm_dt_bias, ssm_a_log=ssm_a_log, ssm_d=ssm_d, ssm_norm_g=ssm_norm_g,
                   ssm_w_out=ssm_w_out, mlp_w1=mlp_w1, mlp_w2=mlp_w2, final_g=final_g)
    moms = dict(norm_mix_g=(m_norm_mix_g, v_norm_mix_g), norm_mlp_g=(m_norm_mlp_g, v_norm_mlp_g),
                pool_w=(m_pool_w, v_pool_w), pool_b=(m_pool_b, v_pool_b),
                pool_scale=(m_pool_scale, v_pool_scale), ssm_w_in=(m_ssm_w_in, v_ssm_w_in),
                ssm_conv_w=(m_ssm_conv_w, v_ssm_conv_w), ssm_conv_b=(m_ssm_conv_b, v_ssm_conv_b),
                ssm_dt_bias=(m_ssm_dt_bias, v_ssm_dt_bias), ssm_a_log=(m_ssm_a_log, v_ssm_a_log),
                ssm_d=(m_ssm_d, v_ssm_d), ssm_norm_g=(m_ssm_norm_g, v_ssm_norm_g),
                ssm_w_out=(m_ssm_w_out, v_ssm_w_out), mlp_w1=(m_mlp_w1, v_mlp_w1),
                mlp_w2=(m_mlp_w2, v_mlp_w2), final_g=(m_final_g, v_final_g))
    names = list(weights)
    big_names = ("pool_w", "ssm_w_in", "ssm_w_out", "mlp_w1", "mlp_w2")
    deltas, new_m, new_v = {}, {}, {}

    def update(nm, grad, layer=None, into=None):
        w = weights[nm]
        two_d = (-1, w.shape[-1])
        return _adamw(w.reshape(two_d), grad.reshape(two_d), moms[nm][0].reshape(two_d),
                      moms[nm][1].reshape(two_d), name="adamw_%s_%s" % (nm, layer), part=layer, into=into)

    def keep(nm, results):
        shp = weights[nm].shape
        deltas[nm], new_m[nm], new_v[nm] = [r.reshape(shp) for r in results]

    keep("ssm_w_in", update("ssm_w_in", grads["ssm_w_in"]))
    keep("ssm_w_out", update("ssm_w_out", grads["ssm_w_out"]))
    w1_done = update("mlp_w1", g_w1_1, layer=1)
    w2_done = update("mlp_w2", g_w2_1, layer=1)
    g_pool_w, g_w1_0, g_w2_0 = _pair_gather_halves(reduce_finish("layer0", w2_done[0]),
                                                   name="pair_gather_layer0")
    keep("mlp_w1", update("mlp_w1", g_w1_0, layer=0, into=w1_done))
    keep("mlp_w2", update("mlp_w2", g_w2_0, layer=0, into=w2_done))
    grads["pool_w"] = g_pool_w.reshape(pool_w.shape)
    grads["mlp_w1"] = jnp.stack([g_w1_0, g_w1_1])
    grads["mlp_w2"] = jnp.stack([g_w2_0, g_w2_1])
    keep("pool_w", update("pool_w", grads["pool_w"]))
    small_names = [nm for nm in names if nm not in big_names]
    sizes = [weights[nm].size for nm in small_names]

    def pack(parts):
        flat = jnp.concatenate([p.reshape(-1) for p in parts])
        pad = (-flat.shape[0]) % (8 * D_MODEL)
        return jnp.pad(flat, (0, pad)).reshape(-1, D_MODEL)

    d_, m_, v_ = _adamw(pack([weights[nm] for nm in small_names]), pack([grads[nm] for nm in small_names]),
                        pack([moms[nm][0] for nm in small_names]), pack([moms[nm][1] for nm in small_names]),
                        name="adamw_small")
    off = 0
    for nm, sz in zip(small_names, sizes):
        shp = weights[nm].shape
        deltas[nm] = d_.reshape(-1)[off:off + sz].reshape(shp)
        new_m[nm] = m_.reshape(-1)[off:off + sz].reshape(shp)
        new_v[nm] = v_.reshape(-1)[off:off + sz].reshape(shp)
        off += sz

    grad_x = dx.reshape(x.shape)
    out_grads = [grads[nm].reshape(weights[nm].shape) for nm in names]
    return (loss, grad_x, *out_grads, *[deltas[nm] for nm in names],
            *[new_m[nm] for nm in names], *[new_v[nm] for nm in names])
```

```python
import functools

import jax
import jax.numpy as jnp
from jax import lax
from jax.experimental import pallas as pl
from jax.experimental.pallas import tpu as pltpu

F32 = jnp.float32
BF16 = jnp.bfloat16
MESH = pl.DeviceIdType.MESH

D_MODEL = 1024
RMS_EPS = 1e-5
POOL_WINDOWS = (2, 4, 8, 16)
POOL_GROUP = 256
POOL_HALO = 16
D_INNER = 2048
HEAD_DIM = 64
N_HEADS = 32
N_GROUPS = 4
HEADS_PER_GROUP = 8
D_STATE = 128
CONV_K = 4
CONV_HALO = 8
CHUNK = 128
CONV_DIM = 3072
IN_PROJ_DIM = 5152
D_FF = 4096
N_CHIPS = 4
N_DEV = 8

ADAM_LR = 0.001
ADAM_B1 = 0.9
ADAM_B2 = 0.999
ADAM_EPS = 1e-08
ADAM_WD = 0.01
ADAM_STEP = 10

VMEM_LIMIT = 56 * 1024 * 1024
NEG_INF = float("-inf")


def _pcall(body, **kw):
    return pl.pallas_call(body, **kw)


def _params(*sem):
    return pltpu.CompilerParams(dimension_semantics=sem, vmem_limit_bytes=VMEM_LIMIT)


def _sigmoid(v):
    return 1.0 / (1.0 + jnp.exp(-v))


def _row_spec(tb, d, nb=None, reverse=False):
    if reverse:
        return pl.BlockSpec((tb, d), lambda i: (nb - 1 - i, 0))
    return pl.BlockSpec((tb, d), lambda i: (i, 0))


def _const_spec(shape):
    return pl.BlockSpec(shape, lambda *_: tuple(0 for _ in shape))


_DIMS = {"nn": (((1,), (0,)), ((), ())),
         "nt": (((1,), (1,)), ((), ())),
         "tn": (((0,), (0,)), ((), ()))}


_MATMUL_VMEM_BUDGET = 40 * 1024 * 1024


def _matmul_tiles(m_dim, n_dim, k_dim, a_bytes, b_bytes, mn_bytes):
    tm, tn = min(m_dim, 1024), min(n_dim, 1024)
    while 2 * (tm * k_dim * a_bytes + tn * k_dim * b_bytes + tm * tn * mn_bytes) > _MATMUL_VMEM_BUDGET:
        if tm >= tn:
            tm //= 2
        else:
            tn //= 2
    return tm, tn


def _matmul(a, b, mode, *, name, out_dtype=F32, a_relu2=False, add=None, relu2_grad_of=None,
            out_col_shards=1):
    if mode == "tn":
        k_dim, m_dim = a.shape
    else:
        m_dim, k_dim = a.shape
    n_dim = b.shape[0] if mode == "nt" else b.shape[1]
    mn_bytes = jnp.dtype(out_dtype).itemsize
    if relu2_grad_of is not None:
        mn_bytes += relu2_grad_of.dtype.itemsize
    if add is not None:
        mn_bytes += add.dtype.itemsize
    tm, tn = _matmul_tiles(m_dim, n_dim, k_dim, a.dtype.itemsize, b.dtype.itemsize, mn_bytes)
    assert m_dim % tm == 0 and n_dim % tn == 0
    a_spec = (pl.BlockSpec((k_dim, tm), lambda i, j: (0, i)) if mode == "tn"
              else pl.BlockSpec((tm, k_dim), lambda i, j: (i, 0)))
    b_spec = (pl.BlockSpec((tn, k_dim), lambda i, j: (j, 0)) if mode == "nt"
              else pl.BlockSpec((k_dim, tn), lambda i, j: (0, j)))
    mn_spec = pl.BlockSpec((tm, tn), lambda i, j: (i, j))
    operands, in_specs = [a, b], [a_spec, b_spec]
    if relu2_grad_of is not None:
        operands.append(relu2_grad_of)
        in_specs.append(mn_spec)
    if add is not None:
        operands.append(add)
        in_specs.append(mn_spec)
    if out_col_shards == 1:
        out_shape = jax.ShapeDtypeStruct((m_dim, n_dim), out_dtype)
        out_spec = mn_spec
    else:
        n_shard = n_dim // out_col_shards
        assert n_shard % tn == 0
        per = n_shard // tn
        out_shape = jax.ShapeDtypeStruct((out_col_shards, m_dim, n_shard), out_dtype)
        out_spec = pl.BlockSpec((None, tm, tn), lambda i, j: (j // per, i, j % per))

    def body(*refs):
        a_ref, b_ref, o_ref = refs[0], refs[1], refs[-1]
        av = a_ref[...]
        if a_relu2:
            av = jnp.maximum(av, 0)
            av = av * av
        r = lax.dot_general(av.astype(BF16), b_ref[...].astype(BF16), _DIMS[mode],
                            preferred_element_type=F32)
        nxt = 2
        if relu2_grad_of is not None:
            r = r * (2.0 * jnp.maximum(refs[nxt][...].astype(F32), 0.0))
            nxt += 1
        if add is not None:
            r = r + refs[nxt][...]
        o_ref[...] = r.astype(out_dtype)

    return _pcall(
        body, name=name, out_shape=out_shape,
        grid=(m_dim // tm, n_dim // tn),
        in_specs=in_specs, out_specs=out_spec,
        compiler_params=_params("parallel", "parallel"),
    )(*operands)


def _rms(x):
    return lax.rsqrt(jnp.mean(x * x, axis=-1, keepdims=True) + RMS_EPS)


def _rmsnorm_fwd(h, g, *, name, tb=512):
    t_dim, d = h.shape

    def body(h_ref, g_ref, o_ref):
        x = h_ref[...]
        o_ref[...] = (x * _rms(x) * g_ref[...]).astype(BF16)

    return _pcall(
        body, name=name, out_shape=jax.ShapeDtypeStruct((t_dim, d), BF16),
        grid=(t_dim // tb,), in_specs=[_row_spec(tb, d), _const_spec((1, d))],
        out_specs=_row_spec(tb, d), compiler_params=_params("parallel"),
    )(h, g)


def _rmsnorm_bwd(dy, h, g, dres, *, name, tb=512):
    t_dim, d = h.shape

    def body(dy_ref, h_ref, g_ref, dres_ref, dh_ref, dh16_ref, dg_ref):
        @pl.when(pl.program_id(0) == 0)
        def _():
            dg_ref[...] = jnp.zeros_like(dg_ref)

        x = h_ref[...]
        r = _rms(x)
        xhat = x * r
        dyv = dy_ref[...]
        dxhat = dyv * g_ref[...]
        dh = dres_ref[...] + r * (dxhat - xhat * jnp.mean(dxhat * xhat, axis=-1, keepdims=True))
        dh_ref[...] = dh
        dh16_ref[...] = dh.astype(BF16)
        dg_ref[0:1, :] += jnp.sum(dyv * xhat, axis=0, keepdims=True)

    return _pcall(
        body, name=name,
        out_shape=(jax.ShapeDtypeStruct((t_dim, d), F32), jax.ShapeDtypeStruct((t_dim, d), BF16),
                   jax.ShapeDtypeStruct((8, d), F32)),
        grid=(t_dim // tb,),
        in_specs=[_row_spec(tb, d), _row_spec(tb, d), _const_spec((1, d)), _row_spec(tb, d)],
        out_specs=(_row_spec(tb, d), _row_spec(tb, d), _const_spec((8, d))),
        compiler_params=_params("arbitrary"),
    )(dy, h, g, dres)


def _pool_mixed(ext, hn, t0, tb):
    t = t0 + lax.broadcasted_iota(jnp.int32, (tb, 1), 0)
    parts = []
    for gi, w in enumerate(POOL_WINDOWS):
        lanes = slice(gi * POOL_GROUP, (gi + 1) * POOL_GROUP)
        s = ext[:, lanes]
        k = 1
        while k < w:
            s = s + pltpu.roll(s, k, 0)
            k *= 2
        cnt = jnp.minimum(t + 1, w).astype(F32)
        parts.append(s[POOL_HALO:, :] / cnt - hn[:, lanes])
    return parts


def _pool_fwd(x, g, pw, pb, ps, *, tb=512):
    t_dim, d = x.shape

    def body(x_ref, g_ref, pw_ref, pb_ref, ps_ref, o_ref, ext_ref):
        i = pl.program_id(0)

        @pl.when(i == 0)
        def _():
            ext_ref[0:POOL_HALO, :] = jnp.zeros((POOL_HALO, d), F32)

        xv = x_ref[...]
        hn = xv * _rms(xv) * g_ref[...]
        ext_ref[POOL_HALO:, :] = hn
        mixed = _pool_mixed(ext_ref[...], hn, i * tb, tb)
        for gi in range(len(POOL_WINDOWS)):
            lanes = slice(gi * POOL_GROUP, (gi + 1) * POOL_GROUP)
            out = jnp.dot(mixed[gi].astype(BF16), pw_ref[gi], preferred_element_type=F32)
            o_ref[:, lanes] = xv[:, lanes] + (out + pb_ref[:, lanes]) * ps_ref[:, lanes]
        ext_ref[0:POOL_HALO, :] = hn[tb - POOL_HALO:, :]

    return _pcall(
        body, name="pool_fwd", out_shape=jax.ShapeDtypeStruct((t_dim, d), F32),
        grid=(t_dim // tb,),
        in_specs=[_row_spec(tb, d), _const_spec((1, d)), _const_spec((4, POOL_GROUP, POOL_GROUP)),
                  _const_spec((1, d)), _const_spec((1, d))],
        out_specs=_row_spec(tb, d),
        scratch_shapes=[pltpu.VMEM((POOL_HALO + tb, d), F32)],
        compiler_params=_params("arbitrary"),
    )(x, g, pw, pb, ps)


def _pool_bwd(x, g, pw, pb, ps, dh1, *, tb=512):
    t_dim, d = x.shape
    nb = t_dim // tb
    halo_per_block = tb // POOL_HALO

    def body(x_ref, xprev_ref, g_ref, pw_ref, pb_ref, ps_ref, dh1_ref,
             dx_ref, dpw_ref, small_ref, ext_ref, dext_ref):
        i = pl.program_id(0)
        blk = nb - 1 - i

        @pl.when(i == 0)
        def _():
            dpw_ref[...] = jnp.zeros_like(dpw_ref)
            small_ref[...] = jnp.zeros_like(small_ref)
            dext_ref[tb:, :] = jnp.zeros((POOL_HALO, d), F32)

        gv = g_ref[...]
        xv = x_ref[...]
        r = _rms(xv)
        xhat = xv * r
        hn = xhat * gv
        xp = xprev_ref[...]
        hprev = xp * _rms(xp) * gv * (blk > 0).astype(F32)
        ext_ref[0:POOL_HALO, :] = hprev
        ext_ref[POOL_HALO:, :] = hn
        mixed = _pool_mixed(ext_ref[...], hn, blk * tb, tb)

        dout = dh1_ref[...]
        t = blk * tb + lax.broadcasted_iota(jnp.int32, (tb, 1), 0)
        for gi, w in enumerate(POOL_WINDOWS):
            lanes = slice(gi * POOL_GROUP, (gi + 1) * POOL_GROUP)
            mb = mixed[gi].astype(BF16)
            pre = jnp.dot(mb, pw_ref[gi], preferred_element_type=F32) + pb_ref[:, lanes]
            dg_out = dout[:, lanes]
            small_ref[2:3, lanes] += jnp.sum(dg_out * pre, axis=0, keepdims=True)
            dpre = dg_out * ps_ref[:, lanes]
            small_ref[1:2, lanes] += jnp.sum(dpre, axis=0, keepdims=True)
            dpb16 = dpre.astype(BF16)
            dpw_ref[gi] += lax.dot_general(mb, dpb16, _DIMS["tn"], preferred_element_type=F32)
            dmixed = lax.dot_general(dpb16, pw_ref[gi], _DIMS["nt"], preferred_element_type=F32)
            cnt = jnp.minimum(t + 1, w).astype(F32)
            dq = dmixed / cnt
            dext_ref[0:tb, lanes] = dq
            s = dext_ref[:, lanes]
            k = 1
            while k < w:
                s = s + pltpu.roll(s, tb + POOL_HALO - k, 0)
                k *= 2
            dhn = s[0:tb, :] - dmixed
            dext_ref[tb:, lanes] = dq[0:POOL_HALO, :]
            small_ref[0:1, lanes] += jnp.sum(dhn * xhat[:, lanes], axis=0, keepdims=True)
            ext_ref[POOL_HALO:, lanes] = dhn * gv[:, lanes]
        dxhat = ext_ref[POOL_HALO:, :]
        dx_ref[...] = dout + r * (dxhat - xhat * jnp.mean(dxhat * xhat, axis=-1, keepdims=True))

    return _pcall(
        body, name="pool_bwd",
        out_shape=(jax.ShapeDtypeStruct((t_dim, d), F32),
                   jax.ShapeDtypeStruct((4, POOL_GROUP, POOL_GROUP), F32),
                   jax.ShapeDtypeStruct((8, d), F32)),
        grid=(nb,),
        in_specs=[_row_spec(tb, d, nb, True),
                  pl.BlockSpec((POOL_HALO, d),
                               lambda i: (jnp.maximum((nb - 1 - i) * halo_per_block - 1, 0), 0)),
                  _const_spec((1, d)), _const_spec((4, POOL_GROUP, POOL_GROUP)),
                  _const_spec((1, d)), _const_spec((1, d)), _row_spec(tb, d, nb, True)],
        out_specs=(_row_spec(tb, d, nb, True), _const_spec((4, POOL_GROUP, POOL_GROUP)),
                   _const_spec((8, d))),
        scratch_shapes=[pltpu.VMEM((POOL_HALO + tb, d), F32), pltpu.VMEM((tb + POOL_HALO, d), F32)],
        compiler_params=_params("arbitrary"),
    )(x, x, g, pw, pb, ps, dh1)


_CONV_CB = 1024
_STRIP = 16


def _strips(tb, fn, unroll=4):
    def step(i, carry):
        fn(pl.multiple_of(i * _STRIP, _STRIP))
        return carry
    lax.fori_loop(0, tb // _STRIP, step, 0, unroll=unroll)


def _conv_taps(ext_ref, r0, w):
    shifted = [ext_ref[CONV_HALO + r0 - sh:CONV_HALO + r0 - sh + _STRIP, :] for sh in range(CONV_K)]
    acc = shifted[0] * w[CONV_K - 1:CONV_K, :]
    for sh in range(1, CONV_K):
        acc = acc + shifted[sh] * w[CONV_K - 1 - sh:CONV_K - sh, :]
    return shifted, acc


def _conv_fwd(u, w, b, *, tb=512):
    t_dim, c = u.shape
    cb = _CONV_CB

    def body(u_ref, w_ref, b_ref, o_ref, ext_ref):
        @pl.when(pl.program_id(1) == 0)
        def _():
            ext_ref[0:CONV_HALO, :] = jnp.zeros((CONV_HALO, cb), F32)

        wv = w_ref[...]
        bv = b_ref[...]

        def fill(r0):
            ext_ref[pl.ds(CONV_HALO + r0, _STRIP), :] = u_ref[pl.ds(r0, _STRIP), :].astype(F32)

        _strips(tb, fill)
        for r0 in range(0, tb, _STRIP):
            v = _conv_taps(ext_ref, r0, wv)[1] + bv
            o_ref[r0:r0 + _STRIP, :] = (v * _sigmoid(v)).astype(BF16)
        ext_ref[0:CONV_HALO, :] = ext_ref[tb:tb + CONV_HALO, :]

    blk = pl.BlockSpec((tb, cb), lambda j, t: (t, j))
    return _pcall(
        body, name="conv_fwd", out_shape=jax.ShapeDtypeStruct((t_dim, c), BF16),
        grid=(c // cb, t_dim // tb),
        in_specs=[blk, pl.BlockSpec((CONV_K, cb), lambda j, t: (0, j)),
                  pl.BlockSpec((1, cb), lambda j, t: (0, j))],
        out_specs=blk,
        scratch_shapes=[pltpu.VMEM((CONV_HALO + tb, cb), F32)],
        compiler_params=_params("parallel", "arbitrary"),
    )(u, w, b)


def _conv_bwd_act(u, dxc, w, b, *, tb=512):
    t_dim, c = u.shape
    cb = _CONV_CB
    half = _STRIP // 2

    def body(u_ref, d_ref, w_ref, b_ref, dv_ref, dwb_ref, ext_ref, acc_ref):
        @pl.when(pl.program_id(1) == 0)
        def _():
            ext_ref[0:CONV_HALO, :] = jnp.zeros((CONV_HALO, cb), F32)
            dwb_ref[...] = jnp.zeros_like(dwb_ref)

        acc_ref[...] = jnp.zeros_like(acc_ref)
        wv = w_ref[...]
        bv = b_ref[...]

        def fill(r0):
            ext_ref[pl.ds(CONV_HALO + r0, _STRIP), :] = u_ref[pl.ds(r0, _STRIP), :].astype(F32)

        _strips(tb, fill)
        for r0 in range(0, tb, _STRIP):
            shifted, v = _conv_taps(ext_ref, r0, wv)
            v = v + bv
            sg = _sigmoid(v)
            dv = d_ref[r0:r0 + _STRIP, :].astype(F32) * (sg * (1.0 + v * (1.0 - sg)))
            dv_ref[r0:r0 + _STRIP, :] = dv.astype(BF16)
            acc_ref[CONV_K] += dv[0:half, :] + dv[half:, :]
            for sh in range(CONV_K):
                p = dv * shifted[sh]
                acc_ref[CONV_K - 1 - sh] += p[0:half, :] + p[half:, :]
        for k in range(CONV_K + 1):
            dwb_ref[k:k + 1, :] += jnp.sum(acc_ref[k], axis=0, keepdims=True)
        ext_ref[0:CONV_HALO, :] = ext_ref[tb:tb + CONV_HALO, :]

    blk = pl.BlockSpec((tb, cb), lambda j, t: (t, j))
    return _pcall(
        body, name="conv_bwd_act",
        out_shape=(jax.ShapeDtypeStruct((t_dim, c), BF16), jax.ShapeDtypeStruct((8, c), F32)),
        grid=(c // cb, t_dim // tb),
        in_specs=[blk, blk, pl.BlockSpec((CONV_K, cb), lambda j, t: (0, j)),
                  pl.BlockSpec((1, cb), lambda j, t: (0, j))],
        out_specs=(blk, pl.BlockSpec((8, cb), lambda j, t: (0, j))),
        scratch_shapes=[pltpu.VMEM((CONV_HALO + tb, cb), F32), pltpu.VMEM((CONV_K + 1, half, cb), F32)],
        compiler_params=_params("parallel", "arbitrary"),
    )(u, dxc, w, b)


def _conv_bwd_in(dv, w, *, tb=512):
    t_dim, c = dv.shape
    cb = _CONV_CB
    nb = t_dim // tb

    def body(dv_ref, w_ref, du_ref, ext_ref):
        @pl.when(pl.program_id(1) == 0)
        def _():
            ext_ref[tb:, :] = jnp.zeros((CONV_HALO, cb), F32)

        wv = w_ref[...]

        def fill(r0):
            ext_ref[pl.ds(r0, _STRIP), :] = dv_ref[pl.ds(r0, _STRIP), :].astype(F32)

        _strips(tb, fill)
        for r0 in range(0, tb, _STRIP):
            acc = ext_ref[r0:r0 + _STRIP, :] * wv[CONV_K - 1:CONV_K, :]
            for sh in range(1, CONV_K):
                acc = acc + ext_ref[r0 + sh:r0 + sh + _STRIP, :] * wv[CONV_K - 1 - sh:CONV_K - sh, :]
            du_ref[r0:r0 + _STRIP, :] = acc.astype(BF16)
        ext_ref[tb:, :] = ext_ref[0:CONV_HALO, :]

    blk = pl.BlockSpec((tb, cb), lambda j, t: (nb - 1 - t, j))
    return _pcall(
        body, name="conv_bwd_in", out_shape=jax.ShapeDtypeStruct((t_dim, c), BF16),
        grid=(c // cb, nb),
        in_specs=[blk, pl.BlockSpec((CONV_K, cb), lambda j, t: (0, j))],
        out_specs=blk,
        scratch_shapes=[pltpu.VMEM((tb + CONV_HALO, cb), F32)],
        compiler_params=_params("parallel", "arbitrary"),
    )(dv, w)


def _softplus(v):
    e = jnp.exp(-jnp.abs(v))
    w = 1.0 + e
    log1p = jnp.where(w == 1.0, e, jnp.log(w) * e / jnp.where(w == 1.0, 1.0, w - 1.0))
    return jnp.maximum(v, 0.0) + log1p


def _cumsum_rows(v):
    row = lax.broadcasted_iota(jnp.int32, v.shape, 0)
    k = 1
    while k < CHUNK:
        v = v + jnp.where(row >= k, pltpu.roll(v, k, 0), 0.0)
        k *= 2
    return v


def _cumsum_lanes(v):
    col = lax.broadcasted_iota(jnp.int32, v.shape, 1)
    k = 1
    while k < CHUNK:
        v = v + jnp.where(col >= k, pltpu.roll(v, k, 1), 0.0)
        k *= 2
    return v


def _rev_cumsum_rows(v):
    row = lax.broadcasted_iota(jnp.int32, v.shape, 0)
    k = 1
    while k < CHUNK:
        v = v + jnp.where(row < CHUNK - k, pltpu.roll(v, CHUNK - k, 0), 0.0)
        k *= 2
    return v


PAIR = 2 * HEAD_DIM
GROUP_LANES = HEADS_PER_GROUP * HEAD_DIM


def _head_lane_matrix():
    h = lax.broadcasted_iota(jnp.int32, (128, D_INNER), 0)
    j = lax.broadcasted_iota(jnp.int32, (128, D_INNER), 1)
    return (j // HEAD_DIM == h).astype(BF16)


def _split_bf16(v, pieces):
    out = []
    for _ in range(pieces):
        p = v.astype(BF16)
        out.append(p)
        v = v - p.astype(F32)
    return out


def _expand_heads(v, e):
    return sum(jnp.dot(p, e, preferred_element_type=F32) for p in _split_bf16(v, 3))


def _reduce_heads(v, et, pieces):
    return sum(jnp.dot(p, et, preferred_element_type=F32) for p in _split_bf16(v, pieces))


def _ssd_common(dtr_ref, dtt_ref, e_ref, bx, bc, ax, ac):
    dtx = _expand_heads(dtr_ref[...], e_ref[...])
    dt = _softplus(dtx + bx)
    a_x = -jnp.exp(ax)
    acs = _cumsum_rows(dt * a_x)
    acs_c = _cumsum_lanes(_softplus(dtt_ref[...] + bc) * (-jnp.exp(ac)))
    return dtx, dt, a_x, acs, acs_c


def _pair_decay(acs_slab, acs_c, h0, causal, left):
    other = pltpu.roll(acs_slab, HEAD_DIM, 1)
    col0 = jnp.where(left, acs_slab, other)
    col1 = jnp.where(left, other, acs_slab)
    l0 = jnp.exp(jnp.where(causal, col0 - acs_c[h0:h0 + 1, :], NEG_INF))
    l1 = jnp.exp(jnp.where(causal, col1 - acs_c[h0 + 1:h0 + 2, :], NEG_INF))
    return l0, l1


def _ssd_fwd(xc, dt_raw, dt_raw_t, bias_x, bias_c, alog_x, alog_c, dskip_x, e_mat):
    t_dim = xc.shape[0]
    nc = t_dim // CHUNK

    def body(xc_ref, dtr_ref, dtt_ref, bx_ref, bc_ref, ax_ref, ac_ref, dk_ref, e_ref,
             y_ref, st_ref, state):
        @pl.when(pl.program_id(0) == 0)
        def _():
            state[...] = jnp.zeros_like(state)

        _, dt, _, acs, acs_c = _ssd_common(dtr_ref, dtt_ref, e_ref, bx_ref[...], bc_ref[...],
                                           ax_ref[...], ac_ref[...])
        st_ref[0] = state[...]
        last = acs[CHUNK - 1:CHUNK, :]
        xs32 = xc_ref[:, 0:D_INNER].astype(F32)
        xdt = xs32 * dt
        xdt16 = xdt.astype(BF16)
        xdte16 = (xdt * jnp.exp(last - acs)).astype(BF16)
        ea = jnp.exp(acs)
        cd = jnp.exp(last)
        skip = dk_ref[...] * xs32
        causal = (lax.broadcasted_iota(jnp.int32, (CHUNK, CHUNK), 0)
                  >= lax.broadcasted_iota(jnp.int32, (CHUNK, CHUNK), 1))
        left = lax.broadcasted_iota(jnp.int32, (CHUNK, PAIR), 1) < HEAD_DIM
        for g in range(N_GROUPS):
            gl = slice(g * GROUP_LANES, (g + 1) * GROUP_LANES)
            bg = xc_ref[:, D_INNER + g * D_STATE:D_INNER + (g + 1) * D_STATE]
            cg = xc_ref[:, D_INNER + (N_GROUPS + g) * D_STATE:D_INNER + (N_GROUPS + g + 1) * D_STATE]
            cb = lax.dot_general(cg, bg, _DIMS["nt"], preferred_element_type=F32)
            hprev = state[:, gl]
            ch = jnp.dot(cg, hprev.astype(BF16), preferred_element_type=F32)
            for j in range(HEADS_PER_GROUP // 2):
                pl_ = slice(g * GROUP_LANES + j * PAIR, g * GROUP_LANES + (j + 1) * PAIR)
                h0 = g * HEADS_PER_GROUP + 2 * j
                l0, l1 = _pair_decay(acs[:, pl_], acs_c, h0, causal, left)
                lhs = jnp.concatenate([(cb * l0).astype(BF16), (cb * l1).astype(BF16)], axis=1)
                xp = xdt16[:, pl_]
                zero = jnp.zeros_like(xp)
                rhs = jnp.concatenate([jnp.where(left, xp, zero), jnp.where(left, zero, xp)], axis=0)
                ydiag = jnp.dot(lhs, rhs, preferred_element_type=F32)
                y_ref[:, pl_] = ydiag + ch[:, j * PAIR:(j + 1) * PAIR] * ea[:, pl_] + skip[:, pl_]
            s_new = lax.dot_general(bg, xdte16[:, gl], _DIMS["tn"], preferred_element_type=F32)
            state[:, gl] = hprev * cd[:, gl] + s_new

    rows = lambda w: pl.BlockSpec((CHUNK, w), lambda c: (c, 0))
    return _pcall(
        body, name="ssd_fwd",
        out_shape=(jax.ShapeDtypeStruct((t_dim, D_INNER), F32),
                   jax.ShapeDtypeStruct((nc, D_STATE, D_INNER), F32)),
        grid=(nc,),
        in_specs=[rows(CONV_DIM), rows(128), pl.BlockSpec((N_HEADS, CHUNK), lambda c: (0, c)),
                  _const_spec((1, D_INNER)), _const_spec((N_HEADS, 1)),
                  _const_spec((1, D_INNER)), _const_spec((N_HEADS, 1)), _const_spec((1, D_INNER)),
                  _const_spec((128, D_INNER))],
        out_specs=(rows(D_INNER), pl.BlockSpec((1, D_STATE, D_INNER), lambda c: (c, 0, 0))),
        scratch_shapes=[pltpu.VMEM((D_STATE, D_INNER), F32)],
        compiler_params=_params("arbitrary"),
    )(xc, dt_raw, dt_raw_t, bias_x, bias_c, alog_x, alog_c, dskip_x, e_mat)


def _ssd_bwd(xc, dt_raw, dt_raw_t, bias_x, bias_c, alog_x, alog_c, dskip_x, bias_r, alog_r,
             e_mat, et_mat, states, dy):
    t_dim = xc.shape[0]
    nc = t_dim // CHUNK

    def body(xc_ref, dtr_ref, dtt_ref, bx_ref, bc_ref, ax_ref, ac_ref, dk_ref, br_ref, ar_ref,
             e_ref, et_ref, st_ref, dy_ref,
             dxc_ref, ddt_ref, small_ref, dstate, dacs_ref, dxdt_ref, acc_x, acc_r):
        step = pl.program_id(0)

        @pl.when(step == 0)
        def _():
            dstate[...] = jnp.zeros_like(dstate)
            acc_x[...] = jnp.zeros_like(acc_x)
            acc_r[...] = jnp.zeros_like(acc_r)

        dtx, dt, a_x, acs, acs_c = _ssd_common(dtr_ref, dtt_ref, e_ref, bx_ref[...], bc_ref[...],
                                               ax_ref[...], ac_ref[...])
        last = acs[CHUNK - 1:CHUNK, :]
        xs32 = xc_ref[:, 0:D_INNER].astype(F32)
        xdt = xs32 * dt
        xdt16 = xdt.astype(BF16)
        dte = jnp.exp(last - acs)
        xdte = xdt * dte
        xdte16 = xdte.astype(BF16)
        cd = jnp.exp(last)
        dyv = dy_ref[...]
        dy16 = dyv.astype(BF16)
        dye = dyv * jnp.exp(acs)
        dye16 = dye.astype(BF16)
        causal = (lax.broadcasted_iota(jnp.int32, (CHUNK, CHUNK), 0)
                  >= lax.broadcasted_iota(jnp.int32, (CHUNK, CHUNK), 1))
        left = lax.broadcasted_iota(jnp.int32, (CHUNK, PAIR), 1) < HEAD_DIM
        lane_id = lax.broadcasted_iota(jnp.int32, (CHUNK, 128), 1)
        row_id = lax.broadcasted_iota(jnp.int32, (CHUNK, 128), 0)
        is_last_row = lax.broadcasted_iota(jnp.int32, (CHUNK, 1), 0) == CHUNK - 1
        dacs_cols = jnp.zeros((CHUNK, 128), F32)
        dacs_rows = jnp.zeros((CHUNK, 128), F32)
        for g in range(N_GROUPS):
            gl = slice(g * GROUP_LANES, (g + 1) * GROUP_LANES)
            b_lanes = slice(D_INNER + g * D_STATE, D_INNER + (g + 1) * D_STATE)
            c_lanes = slice(D_INNER + (N_GROUPS + g) * D_STATE, D_INNER + (N_GROUPS + g + 1) * D_STATE)
            bg = xc_ref[:, b_lanes]
            cg = xc_ref[:, c_lanes]
            cb = lax.dot_general(cg, bg, _DIMS["nt"], preferred_element_type=F32)
            hprev = st_ref[0, :, gl]
            hp16 = hprev.astype(BF16)
            dhn = dstate[:, gl]
            dhn16 = dhn.astype(BF16)
            ch = jnp.dot(cg, hp16, preferred_element_type=F32)
            gmat = jnp.dot(bg, dhn16, preferred_element_type=F32)
            gx = gmat * xdte[:, gl]
            dlast = jnp.sum(gx, axis=0, keepdims=True) + cd[:, gl] * jnp.sum(dhn * hprev, axis=0, keepdims=True)
            dacs_ref[:, gl] = dye[:, gl] * ch - gx + jnp.where(is_last_row, dlast, 0.0)
            dc_acc = lax.dot_general(dye16[:, gl], hp16, _DIMS["nt"], preferred_element_type=F32)
            db_acc = lax.dot_general(xdte16[:, gl], dhn16, _DIMS["nt"], preferred_element_type=F32)
            dstate[:, gl] = dhn * cd[:, gl] + lax.dot_general(cg, dye16[:, gl], _DIMS["tn"],
                                                             preferred_element_type=F32)
            dcb = jnp.zeros((CHUNK, CHUNK), F32)
            for j in range(HEADS_PER_GROUP // 2):
                pl_ = slice(g * GROUP_LANES + j * PAIR, g * GROUP_LANES + (j + 1) * PAIR)
                h0 = g * HEADS_PER_GROUP + 2 * j
                l0, l1 = _pair_decay(acs[:, pl_], acs_c, h0, causal, left)
                m0, m1 = cb * l0, cb * l1
                lhs = jnp.concatenate([m0.astype(BF16), m1.astype(BF16)], axis=1)
                dyp = dy16[:, pl_]
                zero = jnp.zeros_like(dyp)
                both = lax.dot_general(lhs, dyp, _DIMS["tn"], preferred_element_type=F32)
                dxdt_ref[:, pl_] = (jnp.where(left, both[0:CHUNK, :], both[CHUNK:, :])
                                    + gmat[:, j * PAIR:(j + 1) * PAIR] * dte[:, pl_])
                lhs2 = jnp.concatenate([jnp.where(left, dyp, zero), jnp.where(left, zero, dyp)], axis=0)
                dm = lax.dot_general(lhs2, xdt16[:, pl_], _DIMS["nt"], preferred_element_type=F32)
                dm0, dm1 = dm[0:CHUNK, :], dm[CHUNK:, :]
                dcb = dcb + dm0 * l0 + dm1 * l1
                ds0, ds1 = dm0 * m0, dm1 * m1
                dacs_cols = jnp.where(lane_id == h0, jnp.sum(ds0, axis=1, keepdims=True), dacs_cols)
                dacs_cols = jnp.where(lane_id == h0 + 1, jnp.sum(ds1, axis=1, keepdims=True), dacs_cols)
                dacs_rows = jnp.where(row_id == h0, jnp.sum(ds0, axis=0, keepdims=True), dacs_rows)
                dacs_rows = jnp.where(row_id == h0 + 1, jnp.sum(ds1, axis=0, keepdims=True), dacs_rows)
            dcb16 = dcb.astype(BF16)
            dxc_ref[:, c_lanes] = (dc_acc + jnp.dot(dcb16, bg, preferred_element_type=F32)).astype(BF16)
            dxc_ref[:, b_lanes] = (db_acc + lax.dot_general(dcb16, cg, _DIMS["tn"],
                                                           preferred_element_type=F32)).astype(BF16)
        dxdt = dxdt_ref[...]
        dxc_ref[:, 0:D_INNER] = (dxdt * dt + dk_ref[...] * dyv).astype(BF16)
        dadt = _rev_cumsum_rows(dacs_ref[...])
        ddraw_x = (dxdt * xs32 + dadt * a_x) * _sigmoid(dtx + bx_ref[...])
        acc_x[0:1, :] += jnp.sum(dadt * dt, axis=0, keepdims=True) * a_x
        acc_x[1:2, :] += jnp.sum(ddraw_x, axis=0, keepdims=True)
        acc_x[2:3, :] += jnp.sum(dyv * xs32, axis=0, keepdims=True)
        a_r = -jnp.exp(ar_ref[...])
        pre_r = dtr_ref[...] + br_ref[...]
        dadt_r = _rev_cumsum_rows(dacs_cols - dacs_rows.T)
        ddraw_r = jnp.where(lane_id < N_HEADS, dadt_r * a_r * _sigmoid(pre_r), 0.0)
        acc_r[0:1, :] += jnp.where(lane_id[0:1, :] < N_HEADS,
                                   jnp.sum(dadt_r * _softplus(pre_r), axis=0, keepdims=True) * a_r, 0.0)
        acc_r[1:2, :] += jnp.sum(ddraw_r, axis=0, keepdims=True)
        ddt_ref[...] = ddraw_r + _reduce_heads(ddraw_x, et_ref[...], 2)

        @pl.when(step == nc - 1)
        def _():
            small_ref[...] = acc_r[...] + _reduce_heads(acc_x[...], et_ref[...], 3)

    rev = lambda w: pl.BlockSpec((CHUNK, w), lambda c: (nc - 1 - c, 0))
    return _pcall(
        body, name="ssd_bwd",
        out_shape=(jax.ShapeDtypeStruct((t_dim, CONV_DIM), BF16),
                   jax.ShapeDtypeStruct((t_dim, 128), F32),
                   jax.ShapeDtypeStruct((8, 128), F32)),
        grid=(nc,),
        in_specs=[rev(CONV_DIM), rev(128), pl.BlockSpec((N_HEADS, CHUNK), lambda c: (0, nc - 1 - c)),
                  _const_spec((1, D_INNER)), _const_spec((N_HEADS, 1)),
                  _const_spec((1, D_INNER)), _const_spec((N_HEADS, 1)), _const_spec((1, D_INNER)),
                  _const_spec((1, 128)), _const_spec((1, 128)),
                  _const_spec((128, D_INNER)), _const_spec((D_INNER, 128)),
                  pl.BlockSpec((1, D_STATE, D_INNER), lambda c: (nc - 1 - c, 0, 0)),
                  rev(D_INNER)],
        out_specs=(rev(CONV_DIM), rev(128), _const_spec((8, 128))),
        scratch_shapes=[pltpu.VMEM((D_STATE, D_INNER), F32), pltpu.VMEM((CHUNK, D_INNER), F32),
                        pltpu.VMEM((CHUNK, D_INNER), F32), pltpu.VMEM((8, D_INNER), F32),
                        pltpu.VMEM((8, 128), F32)],
        compiler_params=_params("arbitrary"),
    )(xc, dt_raw, dt_raw_t, bias_x, bias_c, alog_x, alog_c, dskip_x, bias_r, alog_r,
      e_mat, et_mat, states, dy)


_GATE_GROUP = D_INNER // N_GROUPS


def _gate_fwd(y, z, g, *, tb=256):
    t_dim = y.shape[0]

    def body(y_ref, z_ref, g_ref, o_ref):
        for gi in range(N_GROUPS):
            lanes = slice(gi * _GATE_GROUP, (gi + 1) * _GATE_GROUP)
            zv = z_ref[:, lanes].astype(F32)
            wv = y_ref[:, lanes] * (zv * _sigmoid(zv))
            o_ref[:, lanes] = (wv * _rms(wv) * g_ref[:, lanes]).astype(BF16)

    return _pcall(
        body, name="gate_fwd", out_shape=jax.ShapeDtypeStruct((t_dim, D_INNER), BF16),
        grid=(t_dim // tb,),
        in_specs=[_row_spec(tb, D_INNER), _row_spec(tb, D_INNER), _const_spec((1, D_INNER))],
        out_specs=_row_spec(tb, D_INNER), compiler_params=_params("parallel"),
    )(y, z, g)


def _gate_bwd(dyn, y, z, g, *, tb=256):
    t_dim = y.shape[0]

    def body(d_ref, y_ref, z_ref, g_ref, dy_ref, dz_ref, dg_ref):
        @pl.when(pl.program_id(0) == 0)
        def _():
            dg_ref[...] = jnp.zeros_like(dg_ref)

        for gi in range(N_GROUPS):
            lanes = slice(gi * _GATE_GROUP, (gi + 1) * _GATE_GROUP)
            zv = z_ref[:, lanes].astype(F32)
            sg = _sigmoid(zv)
            sz = zv * sg
            yv = y_ref[:, lanes]
            wv = yv * sz
            r = _rms(wv)
            what = wv * r
            dv = d_ref[:, lanes]
            dwhat = dv * g_ref[:, lanes]
            dw = r * (dwhat - what * jnp.mean(dwhat * what, axis=-1, keepdims=True))
            dg_ref[0:1, lanes] += jnp.sum(dv * what, axis=0, keepdims=True)
            dy_ref[:, lanes] = dw * sz
            dz_ref[:, lanes] = (dw * yv * (sg * (1.0 + zv * (1.0 - sg)))).astype(BF16)

    return _pcall(
        body, name="gate_bwd",
        out_shape=(jax.ShapeDtypeStruct((t_dim, D_INNER), F32),
                   jax.ShapeDtypeStruct((t_dim, D_INNER), BF16),
                   jax.ShapeDtypeStruct((8, D_INNER), F32)),
        grid=(t_dim // tb,),
        in_specs=[_row_spec(tb, D_INNER), _row_spec(tb, D_INNER), _row_spec(tb, D_INNER),
                  _const_spec((1, D_INNER))],
        out_specs=(_row_spec(tb, D_INNER), _row_spec(tb, D_INNER), _const_spec((8, D_INNER))),
        compiler_params=_params("arbitrary"),
    )(dyn, y, z, g)


def _loss_head(h, g, target, *, tb=512):
    t_dim, d = h.shape

    def body(h_ref, g_ref, t_ref, dh_ref, dh16_ref, small_ref):
        @pl.when(pl.program_id(0) == 0)
        def _():
            small_ref[...] = jnp.zeros_like(small_ref)

        x = h_ref[...]
        r = _rms(x)
        xhat = x * r
        gv = g_ref[...]
        err = xhat * gv - t_ref[...]
        small_ref[1:2, :] += (0.5 / d) * jnp.sum(err * err, axis=0, keepdims=True)
        dyv = err * (1.0 / d)
        dxhat = dyv * gv
        dh = r * (dxhat - xhat * jnp.mean(dxhat * xhat, axis=-1, keepdims=True))
        dh_ref[...] = dh
        dh16_ref[...] = dh.astype(BF16)
        small_ref[0:1, :] += jnp.sum(dyv * xhat, axis=0, keepdims=True)

    return _pcall(
        body, name="loss_head",
        out_shape=(jax.ShapeDtypeStruct((t_dim, d), F32), jax.ShapeDtypeStruct((t_dim, d), BF16),
                   jax.ShapeDtypeStruct((8, d), F32)),
        grid=(t_dim // tb,),
        in_specs=[_row_spec(tb, d), _const_spec((1, d)), _row_spec(tb, d)],
        out_specs=(_row_spec(tb, d), _row_spec(tb, d), _const_spec((8, d))),
        compiler_params=_params("arbitrary"),
    )(h, g, target)


def _adamw(w, g, m, v, *, name, part=None, into=None):
    r_dim, c = w.shape
    rows = r_dim if part is None else r_dim // 2
    assert g.shape == (rows, c)
    tb = rows
    for cand in (512, 256, 128, 64, 32, 16, 8):
        if rows % cand == 0:
            tb = cand
            break
    first = 0 if part is None else part * (rows // tb)
    c1 = 1.0 / (1.0 - ADAM_B1 ** ADAM_STEP)
    c2 = 1.0 / (1.0 - ADAM_B2 ** ADAM_STEP)

    def body(w_ref, g_ref, m_ref, v_ref, *rest):
        d_ref, mo_ref, vo_ref = rest[-3:]
        gv = g_ref[...]
        mn = ADAM_B1 * m_ref[...] + (1.0 - ADAM_B1) * gv
        vn = ADAM_B2 * v_ref[...] + (1.0 - ADAM_B2) * (gv * gv)
        mo_ref[...] = mn
        vo_ref[...] = vn
        d_ref[...] = -ADAM_LR * ((mn * c1) / (jnp.sqrt(vn * c2) + ADAM_EPS) + ADAM_WD * w_ref[...])

    spec = pl.BlockSpec((tb, c), lambda i: (first + i, 0))
    sds = jax.ShapeDtypeStruct((r_dim, c), F32)
    in_specs = [spec, _row_spec(tb, c), spec, spec]
    operands = [w, g, m, v]
    aliases = {}
    if into is not None:
        in_specs += [_ANY] * 3
        operands += list(into)
        aliases = {4: 0, 5: 1, 6: 2}
    return _pcall(
        body, name=name, out_shape=(sds, sds, sds), grid=(rows // tb,),
        in_specs=in_specs, out_specs=(spec,) * 3, input_output_aliases=aliases,
        compiler_params=_params("parallel"),
    )(*operands)


def _pair_sum(grad, recv, place, *, name):
    s_dim, r_dim, c = grad.shape
    half = r_dim // 2
    tb = min(half, 256)
    per_half = half // tb

    def body(place_ref, a_ref, b_ref, o16_ref, o32_ref):
        s = a_ref[...] + b_ref[...]
        o16_ref[...] = s.astype(BF16)

        @pl.when(pl.program_id(1) == place_ref[1])
        def _():
            o32_ref[...] = s[0]

    grid_spec = pltpu.PrefetchScalarGridSpec(
        num_scalar_prefetch=1, grid=(per_half, s_dim),
        in_specs=[pl.BlockSpec((1, tb, c), lambda i, s, p: (s, p[0] * per_half + i, 0)),
                  pl.BlockSpec((1, tb, c), lambda i, s, p: (s, i, 0))],
        out_specs=(pl.BlockSpec((1, tb, c), lambda i, s, p: (s, i, 0)),
                   pl.BlockSpec((tb, c), lambda i, s, p: (i, 0))))
    return _pcall(
        body, name=name, grid_spec=grid_spec,
        out_shape=(jax.ShapeDtypeStruct((s_dim, half, c), BF16), jax.ShapeDtypeStruct((half, c), F32)),
        compiler_params=_params("parallel", "arbitrary"),
    )(place, grad, recv)


def _chip_sum(own, recv, *, name):
    r_dim, c = own.shape
    tb = min(r_dim, 256)

    def body(a_ref, b_ref, o_ref):
        s = a_ref[...]
        for k in range(1, N_CHIPS):
            s = s + b_ref[k].astype(F32)
        o_ref[...] = s

    return _pcall(
        body, name=name, out_shape=jax.ShapeDtypeStruct((r_dim, c), F32),
        grid=(r_dim // tb,),
        in_specs=[_row_spec(tb, c), pl.BlockSpec((N_CHIPS, tb, c), lambda i: (0, i, 0))],
        out_specs=_row_spec(tb, c), compiler_params=_params("parallel"),
    )(own, recv)


def _position():
    return lax.axis_index("x"), lax.axis_index("y"), lax.axis_index("c")


def _chip_peer(x, y, k):
    return x ^ (k >> 1), y ^ (k & 1)


_ANY = pl.BlockSpec(memory_space=pl.ANY)


def _all_gather_weights(shards):
    n = len(shards)
    hops = N_CHIPS - 1

    def body(*refs):
        srcs, outs = refs[:n], refs[n:2 * n]
        send_sems, recv_sems = refs[2 * n:]
        x, y, c = _position()
        me = 2 * x + y

        def over_ici(w, k, chip, to):
            return pltpu.make_async_remote_copy(
                src_ref=srcs[w].at[c], dst_ref=outs[w].at[chip, c],
                send_sem=send_sems.at[w, k - 1], recv_sem=recv_sems.at[w, k - 1],
                device_id=to, device_id_type=MESH)

        def over_d2d(w, k, chip, half):
            return pltpu.make_async_remote_copy(
                src_ref=outs[w].at[chip, half], dst_ref=outs[w].at[chip, half],
                send_sem=send_sems.at[w, hops + k - 1], recv_sem=recv_sems.at[w, hops + k - 1],
                device_id=(x, y, 1 - c), device_id_type=MESH)

        sends = []
        for w in range(n):
            for k in range(1, N_CHIPS):
                px, py = _chip_peer(x, y, k)
                cp = over_ici(w, k, me, (px, py, c))
                cp.start()
                sends.append(cp)
        for w in range(n):
            for k in range(1, N_CHIPS):
                px, py = _chip_peer(x, y, k)
                over_ici(w, k, 2 * px + py, (px, py, c)).wait_recv()
                cp = over_d2d(w, k, 2 * px + py, c)
                cp.start()
                sends.append(cp)
        for w in range(n):
            for k in range(1, N_CHIPS):
                px, py = _chip_peer(x, y, k)
                over_d2d(w, k, 2 * px + py, 1 - c).wait_recv()
        for cp in sends:
            cp.wait_send()

    return _pcall(
        body, name="gather_weights",
        out_shape=tuple(jax.ShapeDtypeStruct((N_CHIPS,) + s.shape, s.dtype) for s in shards),
        in_specs=[_ANY] * n, out_specs=(_ANY,) * n,
        scratch_shapes=[pltpu.SemaphoreType.DMA((n, 2 * hops)),
                        pltpu.SemaphoreType.DMA((n, 2 * hops))],
    )(*shards)


def _pair_copies(srcs, lands, send_sems, recv_sems):
    x, y, c = _position()
    copies = []
    for w in range(len(srcs)):
        half = srcs[w].shape[1] // 2
        copies.append(pltpu.make_async_remote_copy(
            src_ref=srcs[w].at[:, pl.ds((1 - c) * half, half), :], dst_ref=lands[w],
            send_sem=send_sems.at[w], recv_sem=recv_sems.at[w],
            device_id=(x, y, 1 - c), device_id_type=MESH))
    return copies


def _chip_copies(srcs, lands, send_sems, recv_sems):
    x, y, c = _position()
    copies = []
    for w in range(len(srcs)):
        for k in range(1, N_CHIPS):
            px, py = _chip_peer(x, y, k)
            i = w * (N_CHIPS - 1) + k - 1
            copies.append(pltpu.make_async_remote_copy(
                src_ref=srcs[w].at[2 * px + py], dst_ref=lands[w].at[k],
                send_sem=send_sems.at[i], recv_sem=recv_sems.at[i],
                device_id=(px, py, c), device_id_type=MESH))
    return copies


def _gather_copies(srcs, lands, send_sems, recv_sems):
    x, y, c = _position()
    me = 2 * x + y
    copies = []
    for w in range(len(srcs)):
        for k in range(1, N_CHIPS):
            px, py = _chip_peer(x, y, k)
            i = w * (N_CHIPS - 1) + k - 1
            copies.append(pltpu.make_async_remote_copy(
                src_ref=srcs[w].at[c], dst_ref=lands[w].at[me, c],
                send_sem=send_sems.at[i], recv_sem=recv_sems.at[i],
                device_id=(px, py, c), device_id_type=MESH))
    return copies


def _exchange(name, copies_of, n_copies, srcs, land_shapes):
    n = len(srcs)

    def body(*refs):
        copies = copies_of(refs[:n], refs[n:2 * n], refs[2 * n], refs[2 * n + 1])
        for cp in copies:
            cp.start()
        for cp in copies:
            cp.wait_recv()
        for cp in copies:
            cp.wait_send()

    return _pcall(
        body, name=name, out_shape=tuple(land_shapes),
        in_specs=[_ANY] * n, out_specs=(_ANY,) * n,
        scratch_shapes=[pltpu.SemaphoreType.DMA((n_copies,)), pltpu.SemaphoreType.DMA((n_copies,))],
    )(*srcs)


_HBM = pl.BlockSpec(memory_space=pltpu.HBM)
_SEM = pl.BlockSpec(memory_space=pltpu.SEMAPHORE)
_DATAFLOW = pltpu.SideEffectType.DATAFLOW_SIDE_EFFECTING


def _exchange_start(name, copies_of, n_copies, srcs, land_shapes):
    n = len(srcs)
    lands = [lax.empty(s.shape, s.dtype) for s in land_shapes]

    def body(*refs):
        for cp in copies_of(refs[:n], refs[n:2 * n], refs[2 * n], refs[2 * n + 1]):
            cp.start()
        refs[-1][...] = jnp.zeros_like(refs[-1])

    through = [pltpu.HBM(a.shape, a.dtype) for a in list(srcs) + lands]
    outs = _pcall(
        body, name=name,
        out_shape=(pltpu.SemaphoreType.DMA((n_copies,)), pltpu.SemaphoreType.DMA((n_copies,)),
                   *through, jax.ShapeDtypeStruct((8, 128), F32)),
        in_specs=[_HBM] * (2 * n),
        out_specs=(_SEM, _SEM, *([_HBM] * (2 * n)), pl.BlockSpec(memory_space=pltpu.VMEM)),
        input_output_aliases={i: 2 + i for i in range(2 * n)},
        compiler_params=pltpu.CompilerParams(has_side_effects=_DATAFLOW),
    )(*[pltpu.with_memory_space_constraint(a, pltpu.HBM) for a in list(srcs) + lands])
    return outs[:-1], outs[-1][0, 0]


def _exchange_wait(name, copies_of, state, after):
    send_sems, recv_sems, through = state[0], state[1], state[2:]
    n = len(through) // 2
    if after.ndim == 0:
        after = jnp.broadcast_to(after, (8, 128))
    after = pltpu.with_memory_space_constraint(after, pltpu.HBM)

    def body(*refs):
        for cp in copies_of(refs[:n], refs[n:2 * n], refs[2 * n], refs[2 * n + 1]):
            cp.wait_send()
            cp.wait_recv()

    outs = _pcall(
        body, name=name,
        out_shape=tuple(pltpu.HBM(a.shape, a.dtype) for a in through),
        in_specs=[_HBM] * (2 * n) + [_SEM, _SEM, _HBM], out_specs=tuple([_HBM] * (2 * n)),
        input_output_aliases={i: i for i in range(2 * n)},
        compiler_params=pltpu.CompilerParams(has_side_effects=_DATAFLOW),
    )(*through, send_sems, recv_sems, after)
    return outs[:n], outs[n:]


def _forward_halves(lands, *, name):
    n = len(lands)
    hops = N_CHIPS - 1

    def body(*refs):
        ins, outs = refs[:n], refs[n:2 * n]
        send_sems, recv_sems = refs[2 * n], refs[2 * n + 1]
        x, y, c = _position()
        copies = []
        for w in range(n):
            for k in range(1, N_CHIPS):
                px, py = _chip_peer(x, y, k)
                i = w * hops + k - 1
                copies.append(pltpu.make_async_remote_copy(
                    src_ref=ins[w].at[2 * px + py, c], dst_ref=outs[w].at[2 * px + py, c],
                    send_sem=send_sems.at[i], recv_sem=recv_sems.at[i],
                    device_id=(x, y, 1 - c), device_id_type=MESH))
        for cp in copies:
            cp.start()
        for cp in copies:
            cp.wait_recv()
        for cp in copies:
            cp.wait_send()

    return _pcall(
        body, name=name,
        out_shape=tuple(jax.ShapeDtypeStruct(a.shape, a.dtype) for a in lands),
        in_specs=[_ANY] * n, out_specs=(_ANY,) * n,
        input_output_aliases={i: i for i in range(n)},
        scratch_shapes=[pltpu.SemaphoreType.DMA((n * hops,)), pltpu.SemaphoreType.DMA((n * hops,))],
    )(*lands)


def _pair_lands(grads):
    return [jax.ShapeDtypeStruct((g.shape[0], g.shape[1] // 2, g.shape[2]), F32) for g in grads]


def _same_lands(parts):
    return [jax.ShapeDtypeStruct(p.shape, p.dtype) for p in parts]


def _pair_gather_halves(halves, *, name):
    n = len(halves)

    def body(*refs):
        srcs, outs = refs[:n], refs[n:2 * n]
        send_sems, recv_sems = refs[2 * n:]
        x, y, c = _position()
        sends = []
        for w in range(n):
            cp = pltpu.make_async_remote_copy(
                src_ref=srcs[w], dst_ref=outs[w],
                send_sem=send_sems.at[w], recv_sem=recv_sems.at[w],
                device_id=(x, y, 1 - c), device_id_type=MESH)
            cp.start()
            sends.append(cp)
        for cp in sends:
            cp.wait_recv()
        for cp in sends:
            cp.wait_send()

    theirs = _pcall(
        body, name=name,
        out_shape=tuple(jax.ShapeDtypeStruct(h.shape, F32) for h in halves),
        in_specs=[_ANY] * n, out_specs=(_ANY,) * n,
        scratch_shapes=[pltpu.SemaphoreType.DMA((n,)), pltpu.SemaphoreType.DMA((n,))],
    )(*halves)
    my_c = lax.axis_index("c")
    whole = []
    for mine, other in zip(halves, theirs):
        both = jnp.stack([other, other])
        both = lax.dynamic_update_index_in_dim(both, mine, my_c, axis=0)
        whole.append(both.reshape(2 * mine.shape[0], mine.shape[1]))
    return whole


def _all_reduce_small(packed, *, name, sum_row0):
    r_dim, c = packed.shape

    def body(src_ref, out_ref, recv_ref, send_sems, recv_sems):
        x, y, c_ = _position()
        me = 4 * x + 2 * y + c_
        recv_ref[0] = src_ref[...]
        sends = []
        for k in range(1, N_DEV):
            peer = (x ^ (k >> 2), y ^ ((k >> 1) & 1), c_ ^ (k & 1))
            cp = pltpu.make_async_remote_copy(
                src_ref=src_ref, dst_ref=recv_ref.at[k],
                send_sem=send_sems.at[k - 1], recv_sem=recv_sems.at[k - 1],
                device_id=peer, device_id_type=MESH)
            cp.start()
            sends.append(cp)
        for cp in sends:
            cp.wait_recv()
        total = recv_ref[me]
        for d in range(1, N_DEV):
            total = total + recv_ref[d ^ me]
        if sum_row0:
            row0 = jnp.sum(total[0:1, :], axis=1, keepdims=True)
            rid = lax.broadcasted_iota(jnp.int32, total.shape, 0)
            total = jnp.where(rid == 0, row0, total)
        out_ref[...] = total
        for cp in sends:
            cp.wait_send()

    return _pcall(
        body, name=name, out_shape=jax.ShapeDtypeStruct((r_dim, c), F32),
        in_specs=[pl.BlockSpec(memory_space=pltpu.VMEM)],
        out_specs=pl.BlockSpec(memory_space=pltpu.VMEM),
        scratch_shapes=[pltpu.VMEM((N_DEV, r_dim, c), F32),
                        pltpu.SemaphoreType.DMA((N_DEV - 1,)), pltpu.SemaphoreType.DMA((N_DEV - 1,))],
    )(packed)


def _pad_lanes(v, width):
    return jnp.pad(v, ((0, 0), (0, width - v.shape[1])))


def _rows_1024(v):
    flat = v.reshape(-1)
    pad = (-flat.shape[0]) % D_MODEL
    return jnp.pad(flat, (0, pad)).reshape(-1, D_MODEL)


def _local_step(xs, target, pw, w1_0, fetch, reduce_start, reduce_midway,
                conv_w, conv_b, gate_g,
                norm_mix_g, norm_mlp_g, pool_b, pool_scale, ssm_dt_bias, ssm_a_log, ssm_d, final_g):
    bias_r = _pad_lanes(ssm_dt_bias, 128)
    alog_r = _pad_lanes(ssm_a_log, 128)
    bias_x = jnp.repeat(ssm_dt_bias, HEAD_DIM, axis=1)
    alog_x = jnp.repeat(ssm_a_log, HEAD_DIM, axis=1)
    dskip_x = jnp.repeat(ssm_d, HEAD_DIM, axis=1)
    bias_c = ssm_dt_bias.reshape(N_HEADS, 1)
    alog_c = ssm_a_log.reshape(N_HEADS, 1)
    e_mat = _head_lane_matrix()

    g_mix0, g_mix1 = norm_mix_g[0:1], norm_mix_g[1:2]
    g_mlp0, g_mlp1 = norm_mlp_g[0:1], norm_mlp_g[1:2]
    fg = final_g.reshape(1, D_MODEL)

    h1 = _pool_fwd(xs, g_mix0, pw, pool_b, pool_scale)
    hm0 = _rmsnorm_fwd(h1, g_mlp0, name="norm_mlp0")
    u0 = _matmul(hm0, w1_0, "nn", name="mlp0_up", out_dtype=BF16)
    w2_0 = fetch("mlp0_down", u0)
    h2 = _matmul(u0, w2_0, "nn", name="mlp0_down", a_relu2=True, add=h1)

    w_z, w_xbc, w_dt = fetch("in_proj", h2)
    hn1 = _rmsnorm_fwd(h2, g_mix1, name="norm_mix1")
    z = _matmul(hn1, w_z, "nn", name="in_proj_z", out_dtype=BF16)
    xbc = _matmul(hn1, w_xbc, "nn", name="in_proj_xbc", out_dtype=BF16)
    dt_raw = _matmul(hn1, w_dt, "nn", name="in_proj_dt")
    dt_raw_t = dt_raw[:, :N_HEADS].T
    xc = _conv_fwd(xbc, conv_w, conv_b)
    wout, w1_1, w2_1 = fetch("rest", xc)
    y, states = _ssd_fwd(xc, dt_raw, dt_raw_t, bias_x, bias_c, alog_x, alog_c, dskip_x, e_mat)
    yn = _gate_fwd(y, z, gate_g)
    h3 = _matmul(yn, wout, "nn", name="out_proj", add=h2)
    hm1 = _rmsnorm_fwd(h3, g_mlp1, name="norm_mlp1")
    u1 = _matmul(hm1, w1_1, "nn", name="mlp1_up", out_dtype=BF16)
    h4 = _matmul(u1, w2_1, "nn", name="mlp1_down", a_relu2=True, add=h3)

    dh4, dh4_16, small_final = _loss_head(h4, fg, target)

    def mlp_bwd_weights(dh_out16, hm, u, w2_i, tag):
        du = _matmul(dh_out16, w2_i, "nt", name=tag + "_du", out_dtype=BF16, relu2_grad_of=u)
        dw2 = _matmul(u, dh_out16, "tn", name=tag + "_dw2", a_relu2=True)
        dw1 = _matmul(hm, du, "tn", name=tag + "_dw1", out_col_shards=N_CHIPS)
        return du, dw1, dw2.reshape(N_CHIPS, D_FF // N_CHIPS, D_MODEL)

    def mlp_bwd_input(du, dh_out, h_in, w1_i, g_i, tag):
        dhm = _matmul(du, w1_i, "nt", name=tag + "_dhm")
        return _rmsnorm_bwd(dhm, h_in, g_i, dh_out, name=tag + "_norm_bwd")

    du1, dw1_1, dw2_1 = mlp_bwd_weights(dh4_16, hm1, u1, w2_1, "mlp1")
    dh3, dh3_16, dg_mlp1 = mlp_bwd_input(du1, dh4, h3, w1_1, g_mlp1, "mlp1")

    dyn = _matmul(dh3_16, wout, "nt", name="out_proj_dyn")
    dwout = _matmul(yn, dh3_16, "tn", name="out_proj_dw").reshape(N_CHIPS, D_INNER // N_CHIPS, D_MODEL)
    behind = reduce_start("mlp1_out", [dw1_1, dw2_1, dwout])
    dy, dz, dg_gate = _gate_bwd(dyn, y, z, gate_g + behind)
    behind = reduce_midway("mlp1_out", dz)
    dxc, ddt_raw, small_ssd = _ssd_bwd(xc, dt_raw, dt_raw_t, bias_x, bias_c, alog_x, alog_c, dskip_x + behind,
                                       bias_r, alog_r, e_mat, e_mat.T, states, dy)
    dv, dconv = _conv_bwd_act(xbc, dxc, conv_w, conv_b)
    dxbc = _conv_bwd_in(dv, conv_w)
    dhn1 = _matmul(ddt_raw, w_dt, "nt", name="in_proj_dt_dh")
    dhn1 = _matmul(dz, w_z, "nt", name="in_proj_z_dh", add=dhn1)
    dhn1 = _matmul(dxbc, w_xbc, "nt", name="in_proj_xbc_dh", add=dhn1)
    dw_z = _matmul(hn1, dz, "tn", name="in_proj_z_dw")
    dw_xbc = _matmul(hn1, dxbc, "tn", name="in_proj_xbc_dw")
    dw_dt = _matmul(hn1, ddt_raw, "tn", name="in_proj_dt_dw")
    dwin = jnp.concatenate([dw_z, dw_xbc, dw_dt[:, :N_HEADS]], axis=1)
    dwin = jnp.transpose(dwin.reshape(D_MODEL, N_CHIPS, IN_PROJ_DIM // N_CHIPS), (1, 0, 2))
    behind = reduce_start("in_proj", [dwin])
    dh2, dh2_16, dg_mix1 = _rmsnorm_bwd(dhn1, h2, g_mix1 + behind, dh3, name="norm_mix1_bwd")
    behind = reduce_midway("in_proj", dh2_16)

    du0, dw1_0, dw2_0 = mlp_bwd_weights(dh2_16, hm0, u0, w2_0, "mlp0")
    dh1, _, dg_mlp0 = mlp_bwd_input(du0, dh2, h1, w1_0, g_mlp0 + behind, "mlp0")
    dx, dpw, small_pool = _pool_bwd(xs, g_mix0, pw, pool_b, pool_scale, dh1)
    dpw = jnp.transpose(dpw.reshape(4, N_CHIPS, POOL_GROUP // N_CHIPS, POOL_GROUP), (1, 0, 2, 3))
    dpw = dpw.reshape(N_CHIPS, 4 * (POOL_GROUP // N_CHIPS), POOL_GROUP)

    big = [dpw, dw1_0, dw2_0]
    rows = [
        small_final[1:2],
        small_final[0:1],
        small_pool[0:1], dg_mix1[0:1],
        dg_mlp0[0:1], dg_mlp1[0:1],
        small_pool[1:2], small_pool[2:3],
        _pad_lanes(small_ssd[0:3], D_MODEL),
        _rows_1024(dg_gate[0:1]),
        _rows_1024(dconv[0:CONV_K]),
        _rows_1024(dconv[CONV_K:CONV_K + 1]),
    ]
    return dx, big, rows


def kernel(x, norm_mix_g, norm_mlp_g, pool_w, pool_b, pool_scale, ssm_w_in, ssm_conv_w, ssm_conv_b, ssm_dt_bias, ssm_a_log, ssm_d, ssm_norm_g, ssm_w_out, mlp_w1, mlp_w2, final_g, loss_target, m_norm_mix_g, m_norm_mlp_g, m_pool_w, m_pool_b, m_pool_scale, m_ssm_w_in, m_ssm_conv_w, m_ssm_conv_b, m_ssm_dt_bias, m_ssm_a_log, m_ssm_d, m_ssm_norm_g, m_ssm_w_out, m_mlp_w1, m_mlp_w2, m_final_g, v_norm_mix_g, v_norm_mlp_g, v_pool_w, v_pool_b, v_pool_scale, v_ssm_w_in, v_ssm_conv_w, v_ssm_conv_b, v_ssm_dt_bias, v_ssm_a_log, v_ssm_d, v_ssm_norm_g, v_ssm_w_out, v_mlp_w1, v_mlp_w2, v_final_g):
    t_dim = x.shape[1]
    xs = x[0]
    target = loss_target[0]
    my_x, my_y, my_c = _position()
    my_chip = 2 * my_x + my_y

    def halves(w):
        return w.astype(BF16).reshape((2, w.shape[0] // 2) + w.shape[1:])

    def whole(gathered, own_shard):
        g = lax.dynamic_update_index_in_dim(gathered, own_shard, my_chip, axis=0)
        return g.reshape((N_CHIPS, 2 * g.shape[2]) + g.shape[3:])

    def up_weight(g):
        return jnp.transpose(g, (1, 0, 2)).reshape(D_MODEL, D_FF)

    def gather_lands(own):
        return [jax.ShapeDtypeStruct((N_CHIPS,) + s.shape, s.dtype) for s in own]

    vec_cols = CONV_DIM // N_CHIPS
    vec_own = jnp.concatenate([ssm_conv_w[0], ssm_conv_b, _pad_lanes(ssm_norm_g, vec_cols)], axis=0)
    early_own = [halves(pool_w[0]), halves(mlp_w1[0]), vec_own.reshape(2, (CONV_K + 2) // 2, vec_cols)]
    early = _all_gather_weights(early_own)
    g_pool, g_w1_0, g_vec = [whole(g, o) for g, o in zip(early, early_own)]
    pw = jnp.transpose(g_pool, (1, 0, 2, 3)).reshape(4, POOL_GROUP, POOL_GROUP)
    w1_0 = up_weight(g_w1_0)
    conv_w = jnp.transpose(g_vec[:, 0:CONV_K, :], (1, 0, 2)).reshape(CONV_K, CONV_DIM)
    conv_b = g_vec[:, CONV_K, :].reshape(1, CONV_DIM)
    gate_g = g_vec[:, CONV_K + 1, :D_INNER // N_CHIPS].reshape(1, D_INNER)

    down_own = [halves(mlp_w2[0])]
    in_own = [halves(ssm_w_in[0])]
    rest_own = [halves(ssm_w_out[0]), halves(mlp_w1[1]), halves(mlp_w2[1])]
    early, down_own = lax.optimization_barrier((early, down_own))
    fetches = {}
    fetches["mlp0_down"], behind_gather = _exchange_start(
        "gather_mlp0_down_start", _gather_copies, len(down_own) * (N_CHIPS - 1), down_own, gather_lands(down_own))
    behind_gather, in_own = lax.optimization_barrier((behind_gather, in_own))
    fetches["in_proj"], _ = _exchange_start(
        "gather_in_proj_start", _gather_copies, len(in_own) * (N_CHIPS - 1), in_own, gather_lands(in_own))

    def fetch(what, after):
        if what == "mlp0_down":
            own_thru, landed = _exchange_wait("gather_mlp0_down_wait", _gather_copies, fetches[what], after)
            landed = _forward_halves(landed, name="forward_mlp0_down")
            return whole(landed[0], own_thru[0]).reshape(D_FF, D_MODEL)
        if what == "in_proj":
            own_thru, landed = _exchange_wait("gather_in_proj_wait", _gather_copies, fetches["in_proj"], after)
            landed = _forward_halves(landed, name="forward_in_proj")
            landed, rest = lax.optimization_barrier((landed, rest_own))
            fetches["rest"], behind = _exchange_start(
                "gather_rest_start", _gather_copies, len(rest) * (N_CHIPS - 1), rest, gather_lands(rest))
            win = jnp.transpose(whole(landed[0], own_thru[0]), (1, 0, 2)).reshape(D_MODEL, IN_PROJ_DIM)
            w_dt = _pad_lanes(win[:, D_INNER + CONV_DIM:], 128) + behind.astype(BF16)
            return win[:, :D_INNER], win[:, D_INNER:D_INNER + CONV_DIM], w_dt
        own_thru, landed = _exchange_wait("gather_rest_wait", _gather_copies, fetches["rest"], after)
        landed = _forward_halves(landed, name="forward_rest")
        g_wout, g_w1_1, g_w2_1 = [whole(g, o) for g, o in zip(landed, own_thru)]
        return g_wout.reshape(D_INNER, D_MODEL), up_weight(g_w1_1), g_w2_1.reshape(D_FF, D_MODEL)

    place = jnp.stack([my_c, my_chip]).astype(jnp.int32)
    waves = {}

    def reduce_start(wave, grads):
        waves[wave] = {}
        waves[wave]["pair"], behind = _exchange_start(
            "pair_%s_start" % wave, _pair_copies, len(grads), grads, _pair_lands(grads))
        return behind

    def reduce_midway(wave, after):
        st = waves[wave]
        grads, recv = _exchange_wait("pair_%s_wait" % wave, _pair_copies, st["pair"], after)
        sums = [_pair_sum(g, r, place, name="pair_sum_%s_%d" % (wave, i))
                for i, (g, r) in enumerate(zip(grads, recv))]
        st["f32"] = [s32 for _, s32 in sums]
        b16 = [s16 for s16, _ in sums]
        st["chip"], behind = _exchange_start(
            "chip_%s_start" % wave, _chip_copies, len(b16) * (N_CHIPS - 1), b16, _same_lands(b16))
        return behind

    def reduce_finish(wave, after):
        st = waves[wave]
        _, got = _exchange_wait("chip_%s_wait" % wave, _chip_copies, st["chip"], after)
        return [_chip_sum(s32, r, name="chip_sum_%s_%d" % (wave, i))
                for i, (s32, r) in enumerate(zip(st["f32"], got))]

    dx, big0, rows = _local_step(xs, target, pw, w1_0, fetch, reduce_start, reduce_midway,
                                 conv_w, conv_b, gate_g,
                                 norm_mix_g + behind_gather, norm_mlp_g, pool_b, pool_scale,
                                 ssm_dt_bias, ssm_a_log, ssm_d, final_g)

    behind = reduce_start("layer0", big0)
    small = jnp.concatenate(rows, axis=0)
    small = jnp.pad(small, ((0, (-small.shape[0]) % 8), (0, 0))) + behind
    small = _all_reduce_small(small, name="all_reduce_small", sum_row0=True)
    behind = reduce_midway("layer0", small)
    h_w1_1, h_w2_1, h_wout = reduce_finish("mlp1_out", behind)
    (h_win,) = reduce_finish("in_proj", behind)
    g_w1_1, g_w2_1, g_wout_s, g_win_s = _pair_gather_halves([h_w1_1, h_w2_1, h_wout, h_win],
                                                            name="pair_gather_layer1")
    loss = small[0, 0]
    g_final = small[1]
    g_norm_mix = small[2:4]
    g_norm_mlp = small[4:6]
    g_pool_b, g_pool_scale = small[6:7], small[7:8]
    g_alog, g_dtb, g_dsk = small[8:9, :N_HEADS], small[9:10, :N_HEADS], small[10:11, :N_HEADS]
    g_gate_full = small[11:13].reshape(1, D_INNER)
    g_convw_full = small[13:25].reshape(CONV_K, CONV_DIM)
    g_convb_full = small[25:28].reshape(1, CONV_DIM)
    g_gate = lax.dynamic_slice_in_dim(g_gate_full, my_chip * (D_INNER // N_CHIPS), D_INNER // N_CHIPS, axis=1)
    g_convw = lax.dynamic_slice_in_dim(g_convw_full, my_chip * (CONV_DIM // N_CHIPS), CONV_DIM // N_CHIPS, axis=1)
    g_convb = lax.dynamic_slice_in_dim(g_convb_full, my_chip * (CONV_DIM // N_CHIPS), CONV_DIM // N_CHIPS, axis=1)

    grads = {
        "norm_mix_g": g_norm_mix, "norm_mlp_g": g_norm_mlp,
        "pool_b": g_pool_b, "pool_scale": g_pool_scale,
        "ssm_w_in": g_win_s.reshape(ssm_w_in.shape), "ssm_conv_w": g_convw.reshape(ssm_conv_w.shape),
        "ssm_conv_b": g_convb, "ssm_dt_bias": g_dtb, "ssm_a_log": g_alog, "ssm_d": g_dsk,
        "ssm_norm_g": g_gate, "ssm_w_out": g_wout_s.reshape(ssm_w_out.shape),
        "final_g": g_final,
    }
    weights = dict(norm_mix_g=norm_mix_g, norm_mlp_g=norm_mlp_g, pool_w=pool_w, pool_b=pool_b,
                   pool_scale=pool_scale, ssm_w_in=ssm_w_in, ssm_conv_w=ssm_conv_w, ssm_conv_b=ssm_conv_b,
                   ssm_dt_bias=ssm_dt_bias, ssm_a_log=ssm_a_log, ssm_d=ssm_d, ssm_norm_g=ssm_norm_g,
                   ssm_w_out=ssm_w_out, mlp_w1=mlp_w1, mlp_w2=mlp_w2, final_g=final_g)
    moms = dict(norm_mix_g=(m_norm_mix_g, v_norm_mix_g), norm_mlp_g=(m_norm_mlp_g, v_norm_mlp_g),
                pool_w=(m_pool_w, v_pool_w), pool_b=(m_pool_b, v_pool_b),
                pool_scale=(m_pool_scale, v_pool_scale), ssm_w_in=(m_ssm_w_in, v_ssm_w_in),
                ssm_conv_w=(m_ssm_conv_w, v_ssm_conv_w), ssm_conv_b=(m_ssm_conv_b, v_ssm_conv_b),
                ssm_dt_bias=(m_ssm_dt_bias, v_ssm_dt_bias), ssm_a_log=(m_ssm_a_log, v_ssm_a_log),
                ssm_d=(m_ssm_d, v_ssm_d), ssm_norm_g=(m_ssm_norm_g, v_ssm_norm_g),
                ssm_w_out=(m_ssm_w_out, v_ssm_w_out), mlp_w1=(m_mlp_w1, v_mlp_w1),
                mlp_w2=(m_mlp_w2, v_mlp_w2), final_g=(m_final_g, v_final_g))
    names = list(weights)
    big_names = ("pool_w", "ssm_w_in", "ssm_w_out", "mlp_w1", "mlp_w2")
    deltas, new_m, new_v = {}, {}, {}

    def update(nm, grad, layer=None, into=None):
        w = weights[nm]
        two_d = (-1, w.shape[-1])
        return _adamw(w.reshape(two_d), grad.reshape(two_d), moms[nm][0].reshape(two_d),
                      moms[nm][1].reshape(two_d), name="adamw_%s_%s" % (nm, layer), part=layer, into=into)

    def keep(nm, results):
        shp = weights[nm].shape
        deltas[nm], new_m[nm], new_v[nm] = [r.reshape(shp) for r in results]

    keep("ssm_w_in", update("ssm_w_in", grads["ssm_w_in"]))
    keep("ssm_w_out", update("ssm_w_out", grads["ssm_w_out"]))
    w1_done = update("mlp_w1", g_w1_1, layer=1)
    w2_done = update("mlp_w2", g_w2_1, layer=1)
    small_names = [nm for nm in names if nm not in big_names]
    sizes = [weights[nm].size for nm in small_names]

    def pack(parts):
        flat = jnp.concatenate([p.reshape(-1) for p in parts])
        pad = (-flat.shape[0]) % (8 * D_MODEL)
        return jnp.pad(flat, (0, pad)).reshape(-1, D_MODEL)

    d_, m_, v_ = _adamw(pack([weights[nm] for nm in small_names]), pack([grads[nm] for nm in small_names]),
                        pack([moms[nm][0] for nm in small_names]), pack([moms[nm][1] for nm in small_names]),
                        name="adamw_small")
    off = 0
    for nm, sz in zip(small_names, sizes):
        shp = weights[nm].shape
        deltas[nm] = d_.reshape(-1)[off:off + sz].reshape(shp)
        new_m[nm] = m_.reshape(-1)[off:off + sz].reshape(shp)
        new_v[nm] = v_.reshape(-1)[off:off + sz].reshape(shp)
        off += sz

    above = (deltas["ssm_w_in"][0, 0, 0] + deltas["ssm_w_out"][0, 0, 0] + w1_done[0][-1, -1]
             + w2_done[0][-1, -1] + d_[0, 0])
    g_pool_w, g_w1_0, g_w2_0 = _pair_gather_halves(reduce_finish("layer0", above), name="pair_gather_layer0")
    keep("mlp_w1", update("mlp_w1", g_w1_0, layer=0, into=w1_done))
    keep("mlp_w2", update("mlp_w2", g_w2_0, layer=0, into=w2_done))
    grads["pool_w"] = g_pool_w.reshape(pool_w.shape)
    grads["mlp_w1"] = jnp.stack([g_w1_0, g_w1_1])
    grads["mlp_w2"] = jnp.stack([g_w2_0, g_w2_1])
    keep("pool_w", update("pool_w", grads["pool_w"]))

    grad_x = dx.reshape(x.shape)
    out_grads = [grads[nm].reshape(weights[nm].shape) for nm in names]
    return (loss, grad_x, *out_grads, *[deltas[nm] for nm in names],
            *[new_m[nm] for nm in names], *[new_v[nm] for nm in names])
```

```python
import functools

import jax
import jax.numpy as jnp
from jax import lax
from jax.experimental import pallas as pl
from jax.experimental.pallas import tpu as pltpu

F32 = jnp.float32
BF16 = jnp.bfloat16
MESH = pl.DeviceIdType.MESH

D_MODEL = 1024
RMS_EPS = 1e-5
POOL_WINDOWS = (2, 4, 8, 16)
POOL_GROUP = 256
POOL_HALO = 16
D_INNER = 2048
HEAD_DIM = 64
N_HEADS = 32
N_GROUPS = 4
HEADS_PER_GROUP = 8
D_STATE = 128
CONV_K = 4
CONV_HALO = 8
CHUNK = 128
CONV_DIM = 3072
IN_PROJ_DIM = 5152
D_FF = 4096
N_CHIPS = 4
N_DEV = 8

ADAM_LR = 0.001
ADAM_B1 = 0.9
ADAM_B2 = 0.999
ADAM_EPS = 1e-08
ADAM_WD = 0.01
ADAM_STEP = 10

VMEM_LIMIT = 56 * 1024 * 1024
NEG_INF = float("-inf")


def _pcall(body, **kw):
    return pl.pallas_call(body, **kw)


def _params(*sem):
    return pltpu.CompilerParams(dimension_semantics=sem, vmem_limit_bytes=VMEM_LIMIT)


def _sigmoid(v):
    return 1.0 / (1.0 + jnp.exp(-v))


def _row_spec(tb, d, nb=None, reverse=False):
    if reverse:
        return pl.BlockSpec((tb, d), lambda i: (nb - 1 - i, 0))
    return pl.BlockSpec((tb, d), lambda i: (i, 0))


def _const_spec(shape):
    return pl.BlockSpec(shape, lambda *_: tuple(0 for _ in shape))


_DIMS = {"nn": (((1,), (0,)), ((), ())),
         "nt": (((1,), (1,)), ((), ())),
         "tn": (((0,), (0,)), ((), ()))}


_MATMUL_VMEM_BUDGET = 40 * 1024 * 1024


def _matmul_tiles(m_dim, n_dim, k_dim, a_bytes, b_bytes, mn_bytes):
    tm, tn = min(m_dim, 1024), min(n_dim, 1024)
    while 2 * (tm * k_dim * a_bytes + tn * k_dim * b_bytes + tm * tn * mn_bytes) > _MATMUL_VMEM_BUDGET:
        if tm >= tn:
            tm //= 2
        else:
            tn //= 2
    return tm, tn


def _matmul(a, b, mode, *, name, out_dtype=F32, a_relu2=False, add=None, relu2_grad_of=None,
            out_col_shards=1, b_col_shards=False):
    if mode == "tn":
        k_dim, m_dim = a.shape
    else:
        m_dim, k_dim = a.shape
    if b_col_shards:
        n_shards, shard_cols = b.shape[0], b.shape[2]
        n_dim = n_shards * shard_cols if mode == "nn" else b.shape[1]
    else:
        n_dim = b.shape[0] if mode == "nt" else b.shape[1]
    mn_bytes = jnp.dtype(out_dtype).itemsize
    if relu2_grad_of is not None:
        mn_bytes += relu2_grad_of.dtype.itemsize
    if add is not None:
        mn_bytes += add.dtype.itemsize
    tm, tn = _matmul_tiles(m_dim, n_dim, k_dim, a.dtype.itemsize, b.dtype.itemsize, mn_bytes)
    assert m_dim % tm == 0 and n_dim % tn == 0
    a_spec = (pl.BlockSpec((k_dim, tm), lambda i, j: (0, i)) if mode == "tn"
              else pl.BlockSpec((tm, k_dim), lambda i, j: (i, 0)))
    if b_col_shards and mode == "nn":
        assert shard_cols % tn == 0
        per_shard = shard_cols // tn
        b_spec = pl.BlockSpec((None, k_dim, tn), lambda i, j: (j // per_shard, 0, j % per_shard))
    elif b_col_shards:
        assert mode == "nt" and k_dim == n_shards * shard_cols
        b_spec = pl.BlockSpec((n_shards, tn, shard_cols), lambda i, j: (0, j, 0))
    else:
        b_spec = (pl.BlockSpec((tn, k_dim), lambda i, j: (j, 0)) if mode == "nt"
                  else pl.BlockSpec((k_dim, tn), lambda i, j: (0, j)))
    mn_spec = pl.BlockSpec((tm, tn), lambda i, j: (i, j))
    operands, in_specs = [a, b], [a_spec, b_spec]
    if relu2_grad_of is not None:
        operands.append(relu2_grad_of)
        in_specs.append(mn_spec)
    if add is not None:
        operands.append(add)
        in_specs.append(mn_spec)
    if out_col_shards == 1:
        out_shape = jax.ShapeDtypeStruct((m_dim, n_dim), out_dtype)
        out_spec = mn_spec
    else:
        n_shard = n_dim // out_col_shards
        assert n_shard % tn == 0
        per = n_shard // tn
        out_shape = jax.ShapeDtypeStruct((out_col_shards, m_dim, n_shard), out_dtype)
        out_spec = pl.BlockSpec((None, tm, tn), lambda i, j: (j // per, i, j % per))

    def body(*refs):
        a_ref, b_ref, o_ref = refs[0], refs[1], refs[-1]
        av = a_ref[...]
        if a_relu2:
            av = jnp.maximum(av, 0)
            av = av * av
        if b_col_shards and mode == "nt":
            r = None
            for s in range(n_shards):
                part = lax.dot_general(av[:, s * shard_cols:(s + 1) * shard_cols].astype(BF16),
                                       b_ref[s].astype(BF16), _DIMS[mode], preferred_element_type=F32)
                r = part if r is None else r + part
        else:
            r = lax.dot_general(av.astype(BF16), b_ref[...].astype(BF16), _DIMS[mode],
                                preferred_element_type=F32)
        nxt = 2
        if relu2_grad_of is not None:
            r = r * (2.0 * jnp.maximum(refs[nxt][...].astype(F32), 0.0))
            nxt += 1
        if add is not None:
            r = r + refs[nxt][...]
        o_ref[...] = r.astype(out_dtype)

    return _pcall(
        body, name=name, out_shape=out_shape,
        grid=(m_dim // tm, n_dim // tn),
        in_specs=in_specs, out_specs=out_spec,
        compiler_params=_params("parallel", "parallel"),
    )(*operands)


def _rms(x):
    return lax.rsqrt(jnp.mean(x * x, axis=-1, keepdims=True) + RMS_EPS)


def _rmsnorm_fwd(h, g, *, name, tb=512):
    t_dim, d = h.shape

    def body(h_ref, g_ref, o_ref):
        x = h_ref[...]
        o_ref[...] = (x * _rms(x) * g_ref[...]).astype(BF16)

    return _pcall(
        body, name=name, out_shape=jax.ShapeDtypeStruct((t_dim, d), BF16),
        grid=(t_dim // tb,), in_specs=[_row_spec(tb, d), _const_spec((1, d))],
        out_specs=_row_spec(tb, d), compiler_params=_params("parallel"),
    )(h, g)


def _rmsnorm_bwd(dy, h, g, dres, *, name, tb=512):
    t_dim, d = h.shape

    def body(dy_ref, h_ref, g_ref, dres_ref, dh_ref, dh16_ref, dg_ref):
        @pl.when(pl.program_id(0) == 0)
        def _():
            dg_ref[...] = jnp.zeros_like(dg_ref)

        x = h_ref[...]
        r = _rms(x)
        xhat = x * r
        dyv = dy_ref[...]
        dxhat = dyv * g_ref[...]
        dh = dres_ref[...] + r * (dxhat - xhat * jnp.mean(dxhat * xhat, axis=-1, keepdims=True))
        dh_ref[...] = dh
        dh16_ref[...] = dh.astype(BF16)
        dg_ref[0:1, :] += jnp.sum(dyv * xhat, axis=0, keepdims=True)

    return _pcall(
        body, name=name,
        out_shape=(jax.ShapeDtypeStruct((t_dim, d), F32), jax.ShapeDtypeStruct((t_dim, d), BF16),
                   jax.ShapeDtypeStruct((8, d), F32)),
        grid=(t_dim // tb,),
        in_specs=[_row_spec(tb, d), _row_spec(tb, d), _const_spec((1, d)), _row_spec(tb, d)],
        out_specs=(_row_spec(tb, d), _row_spec(tb, d), _const_spec((8, d))),
        compiler_params=_params("arbitrary"),
    )(dy, h, g, dres)


def _pool_mixed(ext, hn, t0, tb):
    t = t0 + lax.broadcasted_iota(jnp.int32, (tb, 1), 0)
    parts = []
    for gi, w in enumerate(POOL_WINDOWS):
        lanes = slice(gi * POOL_GROUP, (gi + 1) * POOL_GROUP)
        s = ext[:, lanes]
        k = 1
        while k < w:
            s = s + pltpu.roll(s, k, 0)
            k *= 2
        cnt = jnp.minimum(t + 1, w).astype(F32)
        parts.append(s[POOL_HALO:, :] / cnt - hn[:, lanes])
    return parts


def _pool_fwd(x, g, pw, pb, ps, *, tb=512):
    t_dim, d = x.shape

    def body(x_ref, g_ref, pw_ref, pb_ref, ps_ref, o_ref, ext_ref):
        i = pl.program_id(0)

        @pl.when(i == 0)
        def _():
            ext_ref[0:POOL_HALO, :] = jnp.zeros((POOL_HALO, d), F32)

        xv = x_ref[...]
        hn = xv * _rms(xv) * g_ref[...]
        ext_ref[POOL_HALO:, :] = hn
        mixed = _pool_mixed(ext_ref[...], hn, i * tb, tb)
        for gi in range(len(POOL_WINDOWS)):
            lanes = slice(gi * POOL_GROUP, (gi + 1) * POOL_GROUP)
            out = jnp.dot(mixed[gi].astype(BF16), pw_ref[gi], preferred_element_type=F32)
            o_ref[:, lanes] = xv[:, lanes] + (out + pb_ref[:, lanes]) * ps_ref[:, lanes]
        ext_ref[0:POOL_HALO, :] = hn[tb - POOL_HALO:, :]

    return _pcall(
        body, name="pool_fwd", out_shape=jax.ShapeDtypeStruct((t_dim, d), F32),
        grid=(t_dim // tb,),
        in_specs=[_row_spec(tb, d), _const_spec((1, d)), _const_spec((4, POOL_GROUP, POOL_GROUP)),
                  _const_spec((1, d)), _const_spec((1, d))],
        out_specs=_row_spec(tb, d),
        scratch_shapes=[pltpu.VMEM((POOL_HALO + tb, d), F32)],
        compiler_params=_params("arbitrary"),
    )(x, g, pw, pb, ps)


def _pool_bwd(x, g, pw, pb, ps, dh1, *, tb=512):
    t_dim, d = x.shape
    nb = t_dim // tb
    halo_per_block = tb // POOL_HALO

    def body(x_ref, xprev_ref, g_ref, pw_ref, pb_ref, ps_ref, dh1_ref,
             dx_ref, dpw_ref, small_ref, ext_ref, dext_ref):
        i = pl.program_id(0)
        blk = nb - 1 - i

        @pl.when(i == 0)
        def _():
            dpw_ref[...] = jnp.zeros_like(dpw_ref)
            small_ref[...] = jnp.zeros_like(small_ref)
            dext_ref[tb:, :] = jnp.zeros((POOL_HALO, d), F32)

        gv = g_ref[...]
        xv = x_ref[...]
        r = _rms(xv)
        xhat = xv * r
        hn = xhat * gv
        xp = xprev_ref[...]
        hprev = xp * _rms(xp) * gv * (blk > 0).astype(F32)
        ext_ref[0:POOL_HALO, :] = hprev
        ext_ref[POOL_HALO:, :] = hn
        mixed = _pool_mixed(ext_ref[...], hn, blk * tb, tb)

        dout = dh1_ref[...]
        t = blk * tb + lax.broadcasted_iota(jnp.int32, (tb, 1), 0)
        for gi, w in enumerate(POOL_WINDOWS):
            lanes = slice(gi * POOL_GROUP, (gi + 1) * POOL_GROUP)
            mb = mixed[gi].astype(BF16)
            pre = jnp.dot(mb, pw_ref[gi], preferred_element_type=F32) + pb_ref[:, lanes]
            dg_out = dout[:, lanes]
            small_ref[2:3, lanes] += jnp.sum(dg_out * pre, axis=0, keepdims=True)
            dpre = dg_out * ps_ref[:, lanes]
            small_ref[1:2, lanes] += jnp.sum(dpre, axis=0, keepdims=True)
            dpb16 = dpre.astype(BF16)
            dpw_ref[gi] += lax.dot_general(mb, dpb16, _DIMS["tn"], preferred_element_type=F32)
            dmixed = lax.dot_general(dpb16, pw_ref[gi], _DIMS["nt"], preferred_element_type=F32)
            cnt = jnp.minimum(t + 1, w).astype(F32)
            dq = dmixed / cnt
            dext_ref[0:tb, lanes] = dq
            s = dext_ref[:, lanes]
            k = 1
            while k < w:
                s = s + pltpu.roll(s, tb + POOL_HALO - k, 0)
                k *= 2
            dhn = s[0:tb, :] - dmixed
            dext_ref[tb:, lanes] = dq[0:POOL_HALO, :]
            small_ref[0:1, lanes] += jnp.sum(dhn * xhat[:, lanes], axis=0, keepdims=True)
            ext_ref[POOL_HALO:, lanes] = dhn * gv[:, lanes]
        dxhat = ext_ref[POOL_HALO:, :]
        dx_ref[...] = dout + r * (dxhat - xhat * jnp.mean(dxhat * xhat, axis=-1, keepdims=True))

    return _pcall(
        body, name="pool_bwd",
        out_shape=(jax.ShapeDtypeStruct((t_dim, d), F32),
                   jax.ShapeDtypeStruct((4, POOL_GROUP, POOL_GROUP), F32),
                   jax.ShapeDtypeStruct((8, d), F32)),
        grid=(nb,),
        in_specs=[_row_spec(tb, d, nb, True),
                  pl.BlockSpec((POOL_HALO, d),
                               lambda i: (jnp.maximum((nb - 1 - i) * halo_per_block - 1, 0), 0)),
                  _const_spec((1, d)), _const_spec((4, POOL_GROUP, POOL_GROUP)),
                  _const_spec((1, d)), _const_spec((1, d)), _row_spec(tb, d, nb, True)],
        out_specs=(_row_spec(tb, d, nb, True), _const_spec((4, POOL_GROUP, POOL_GROUP)),
                   _const_spec((8, d))),
        scratch_shapes=[pltpu.VMEM((POOL_HALO + tb, d), F32), pltpu.VMEM((tb + POOL_HALO, d), F32)],
        compiler_params=_params("arbitrary"),
    )(x, x, g, pw, pb, ps, dh1)


_CONV_CB = 1024
_STRIP = 16


def _strips(tb, fn, unroll=4):
    def step(i, carry):
        fn(pl.multiple_of(i * _STRIP, _STRIP))
        return carry
    lax.fori_loop(0, tb // _STRIP, step, 0, unroll=unroll)


def _conv_taps(ext_ref, r0, w):
    shifted = [ext_ref[CONV_HALO + r0 - sh:CONV_HALO + r0 - sh + _STRIP, :] for sh in range(CONV_K)]
    acc = shifted[0] * w[CONV_K - 1:CONV_K, :]
    for sh in range(1, CONV_K):
        acc = acc + shifted[sh] * w[CONV_K - 1 - sh:CONV_K - sh, :]
    return shifted, acc


def _conv_fwd(u, w, b, *, tb=512):
    t_dim, c = u.shape
    cb = _CONV_CB

    def body(u_ref, w_ref, b_ref, o_ref, ext_ref):
        @pl.when(pl.program_id(1) == 0)
        def _():
            ext_ref[0:CONV_HALO, :] = jnp.zeros((CONV_HALO, cb), F32)

        wv = w_ref[...]
        bv = b_ref[...]

        def fill(r0):
            ext_ref[pl.ds(CONV_HALO + r0, _STRIP), :] = u_ref[pl.ds(r0, _STRIP), :].astype(F32)

        _strips(tb, fill)
        for r0 in range(0, tb, _STRIP):
            v = _conv_taps(ext_ref, r0, wv)[1] + bv
            o_ref[r0:r0 + _STRIP, :] = (v * _sigmoid(v)).astype(BF16)
        ext_ref[0:CONV_HALO, :] = ext_ref[tb:tb + CONV_HALO, :]

    blk = pl.BlockSpec((tb, cb), lambda j, t: (t, j))
    return _pcall(
        body, name="conv_fwd", out_shape=jax.ShapeDtypeStruct((t_dim, c), BF16),
        grid=(c // cb, t_dim // tb),
        in_specs=[blk, pl.BlockSpec((CONV_K, cb), lambda j, t: (0, j)),
                  pl.BlockSpec((1, cb), lambda j, t: (0, j))],
        out_specs=blk,
        scratch_shapes=[pltpu.VMEM((CONV_HALO + tb, cb), F32)],
        compiler_params=_params("parallel", "arbitrary"),
    )(u, w, b)


def _conv_bwd_act(u, dxc, w, b, *, tb=512):
    t_dim, c = u.shape
    cb = _CONV_CB
    half = _STRIP // 2

    def body(u_ref, d_ref, w_ref, b_ref, dv_ref, dwb_ref, ext_ref, acc_ref):
        @pl.when(pl.program_id(1) == 0)
        def _():
            ext_ref[0:CONV_HALO, :] = jnp.zeros((CONV_HALO, cb), F32)
            dwb_ref[...] = jnp.zeros_like(dwb_ref)

        acc_ref[...] = jnp.zeros_like(acc_ref)
        wv = w_ref[...]
        bv = b_ref[...]

        def fill(r0):
            ext_ref[pl.ds(CONV_HALO + r0, _STRIP), :] = u_ref[pl.ds(r0, _STRIP), :].astype(F32)

        _strips(tb, fill)
        for r0 in range(0, tb, _STRIP):
            shifted, v = _conv_taps(ext_ref, r0, wv)
            v = v + bv
            sg = _sigmoid(v)
            dv = d_ref[r0:r0 + _STRIP, :].astype(F32) * (sg * (1.0 + v * (1.0 - sg)))
            dv_ref[r0:r0 + _STRIP, :] = dv.astype(BF16)
            acc_ref[CONV_K] += dv[0:half, :] + dv[half:, :]
            for sh in range(CONV_K):
                p = dv * shifted[sh]
                acc_ref[CONV_K - 1 - sh] += p[0:half, :] + p[half:, :]
        for k in range(CONV_K + 1):
            dwb_ref[k:k + 1, :] += jnp.sum(acc_ref[k], axis=0, keepdims=True)
        ext_ref[0:CONV_HALO, :] = ext_ref[tb:tb + CONV_HALO, :]

    blk = pl.BlockSpec((tb, cb), lambda j, t: (t, j))
    return _pcall(
        body, name="conv_bwd_act",
        out_shape=(jax.ShapeDtypeStruct((t_dim, c), BF16), jax.ShapeDtypeStruct((8, c), F32)),
        grid=(c // cb, t_dim // tb),
        in_specs=[blk, blk, pl.BlockSpec((CONV_K, cb), lambda j, t: (0, j)),
                  pl.BlockSpec((1, cb), lambda j, t: (0, j))],
        out_specs=(blk, pl.BlockSpec((8, cb), lambda j, t: (0, j))),
        scratch_shapes=[pltpu.VMEM((CONV_HALO + tb, cb), F32), pltpu.VMEM((CONV_K + 1, half, cb), F32)],
        compiler_params=_params("parallel", "arbitrary"),
    )(u, dxc, w, b)


def _conv_bwd_in(dv, w, *, tb=512):
    t_dim, c = dv.shape
    cb = _CONV_CB
    nb = t_dim // tb

    def body(dv_ref, w_ref, du_ref, ext_ref):
        @pl.when(pl.program_id(1) == 0)
        def _():
            ext_ref[tb:, :] = jnp.zeros((CONV_HALO, cb), F32)

        wv = w_ref[...]

        def fill(r0):
            ext_ref[pl.ds(r0, _STRIP), :] = dv_ref[pl.ds(r0, _STRIP), :].astype(F32)

        _strips(tb, fill)
        for r0 in range(0, tb, _STRIP):
            acc = ext_ref[r0:r0 + _STRIP, :] * wv[CONV_K - 1:CONV_K, :]
            for sh in range(1, CONV_K):
                acc = acc + ext_ref[r0 + sh:r0 + sh + _STRIP, :] * wv[CONV_K - 1 - sh:CONV_K - sh, :]
            du_ref[r0:r0 + _STRIP, :] = acc.astype(BF16)
        ext_ref[tb:, :] = ext_ref[0:CONV_HALO, :]

    blk = pl.BlockSpec((tb, cb), lambda j, t: (nb - 1 - t, j))
    return _pcall(
        body, name="conv_bwd_in", out_shape=jax.ShapeDtypeStruct((t_dim, c), BF16),
        grid=(c // cb, nb),
        in_specs=[blk, pl.BlockSpec((CONV_K, cb), lambda j, t: (0, j))],
        out_specs=blk,
        scratch_shapes=[pltpu.VMEM((tb + CONV_HALO, cb), F32)],
        compiler_params=_params("parallel", "arbitrary"),
    )(dv, w)


def _softplus(v):
    e = jnp.exp(-jnp.abs(v))
    w = 1.0 + e
    log1p = jnp.where(w == 1.0, e, jnp.log(w) * e / jnp.where(w == 1.0, 1.0, w - 1.0))
    return jnp.maximum(v, 0.0) + log1p


def _cumsum_rows(v):
    row = lax.broadcasted_iota(jnp.int32, v.shape, 0)
    k = 1
    while k < CHUNK:
        v = v + jnp.where(row >= k, pltpu.roll(v, k, 0), 0.0)
        k *= 2
    return v


def _cumsum_lanes(v):
    col = lax.broadcasted_iota(jnp.int32, v.shape, 1)
    k = 1
    while k < CHUNK:
        v = v + jnp.where(col >= k, pltpu.roll(v, k, 1), 0.0)
        k *= 2
    return v


def _rev_cumsum_rows(v):
    row = lax.broadcasted_iota(jnp.int32, v.shape, 0)
    k = 1
    while k < CHUNK:
        v = v + jnp.where(row < CHUNK - k, pltpu.roll(v, CHUNK - k, 0), 0.0)
        k *= 2
    return v


PAIR = 2 * HEAD_DIM
GROUP_LANES = HEADS_PER_GROUP * HEAD_DIM


def _head_lane_matrix():
    h = lax.broadcasted_iota(jnp.int32, (128, D_INNER), 0)
    j = lax.broadcasted_iota(jnp.int32, (128, D_INNER), 1)
    return (j // HEAD_DIM == h).astype(BF16)


def _split_bf16(v, pieces):
    out = []
    for _ in range(pieces):
        p = v.astype(BF16)
        out.append(p)
        v = v - p.astype(F32)
    return out


def _expand_heads(v, e):
    return sum(jnp.dot(p, e, preferred_element_type=F32) for p in _split_bf16(v, 3))


def _reduce_heads(v, et, pieces):
    return sum(jnp.dot(p, et, preferred_element_type=F32) for p in _split_bf16(v, pieces))


def _ssd_common(dtr_ref, dtt_ref, e_ref, bx, bc, ax, ac):
    dtx = _expand_heads(dtr_ref[...], e_ref[...])
    dt = _softplus(dtx + bx)
    a_x = -jnp.exp(ax)
    acs = _cumsum_rows(dt * a_x)
    acs_c = _cumsum_lanes(_softplus(dtt_ref[...] + bc) * (-jnp.exp(ac)))
    return dtx, dt, a_x, acs, acs_c


def _pair_decay(acs_slab, acs_c, h0, causal, left):
    other = pltpu.roll(acs_slab, HEAD_DIM, 1)
    col0 = jnp.where(left, acs_slab, other)
    col1 = jnp.where(left, other, acs_slab)
    l0 = jnp.exp(jnp.where(causal, col0 - acs_c[h0:h0 + 1, :], NEG_INF))
    l1 = jnp.exp(jnp.where(causal, col1 - acs_c[h0 + 1:h0 + 2, :], NEG_INF))
    return l0, l1


def _ssd_fwd(xc, dt_raw, dt_raw_t, bias_x, bias_c, alog_x, alog_c, dskip_x, e_mat):
    t_dim = xc.shape[0]
    nc = t_dim // CHUNK

    def body(xc_ref, dtr_ref, dtt_ref, bx_ref, bc_ref, ax_ref, ac_ref, dk_ref, e_ref,
             y_ref, st_ref, state):
        @pl.when(pl.program_id(0) == 0)
        def _():
            state[...] = jnp.zeros_like(state)

        _, dt, _, acs, acs_c = _ssd_common(dtr_ref, dtt_ref, e_ref, bx_ref[...], bc_ref[...],
                                           ax_ref[...], ac_ref[...])
        st_ref[0] = state[...]
        last = acs[CHUNK - 1:CHUNK, :]
        xs32 = xc_ref[:, 0:D_INNER].astype(F32)
        xdt = xs32 * dt
        xdt16 = xdt.astype(BF16)
        xdte16 = (xdt * jnp.exp(last - acs)).astype(BF16)
        ea = jnp.exp(acs)
        cd = jnp.exp(last)
        skip = dk_ref[...] * xs32
        causal = (lax.broadcasted_iota(jnp.int32, (CHUNK, CHUNK), 0)
                  >= lax.broadcasted_iota(jnp.int32, (CHUNK, CHUNK), 1))
        left = lax.broadcasted_iota(jnp.int32, (CHUNK, PAIR), 1) < HEAD_DIM
        for g in range(N_GROUPS):
            gl = slice(g * GROUP_LANES, (g + 1) * GROUP_LANES)
            bg = xc_ref[:, D_INNER + g * D_STATE:D_INNER + (g + 1) * D_STATE]
            cg = xc_ref[:, D_INNER + (N_GROUPS + g) * D_STATE:D_INNER + (N_GROUPS + g + 1) * D_STATE]
            cb = lax.dot_general(cg, bg, _DIMS["nt"], preferred_element_type=F32)
            hprev = state[:, gl]
            ch = jnp.dot(cg, hprev.astype(BF16), preferred_element_type=F32)
            for j in range(HEADS_PER_GROUP // 2):
                pl_ = slice(g * GROUP_LANES + j * PAIR, g * GROUP_LANES + (j + 1) * PAIR)
                h0 = g * HEADS_PER_GROUP + 2 * j
                l0, l1 = _pair_decay(acs[:, pl_], acs_c, h0, causal, left)
                lhs = jnp.concatenate([(cb * l0).astype(BF16), (cb * l1).astype(BF16)], axis=1)
                xp = xdt16[:, pl_]
                zero = jnp.zeros_like(xp)
                rhs = jnp.concatenate([jnp.where(left, xp, zero), jnp.where(left, zero, xp)], axis=0)
                ydiag = jnp.dot(lhs, rhs, preferred_element_type=F32)
                y_ref[:, pl_] = ydiag + ch[:, j * PAIR:(j + 1) * PAIR] * ea[:, pl_] + skip[:, pl_]
            s_new = lax.dot_general(bg, xdte16[:, gl], _DIMS["tn"], preferred_element_type=F32)
            state[:, gl] = hprev * cd[:, gl] + s_new

    rows = lambda w: pl.BlockSpec((CHUNK, w), lambda c: (c, 0))
    return _pcall(
        body, name="ssd_fwd",
        out_shape=(jax.ShapeDtypeStruct((t_dim, D_INNER), F32),
                   jax.ShapeDtypeStruct((nc, D_STATE, D_INNER), F32)),
        grid=(nc,),
        in_specs=[rows(CONV_DIM), rows(128), pl.BlockSpec((N_HEADS, CHUNK), lambda c: (0, c)),
                  _const_spec((1, D_INNER)), _const_spec((N_HEADS, 1)),
                  _const_spec((1, D_INNER)), _const_spec((N_HEADS, 1)), _const_spec((1, D_INNER)),
                  _const_spec((128, D_INNER))],
        out_specs=(rows(D_INNER), pl.BlockSpec((1, D_STATE, D_INNER), lambda c: (c, 0, 0))),
        scratch_shapes=[pltpu.VMEM((D_STATE, D_INNER), F32)],
        compiler_params=_params("arbitrary"),
    )(xc, dt_raw, dt_raw_t, bias_x, bias_c, alog_x, alog_c, dskip_x, e_mat)


def _ssd_bwd(xc, dt_raw, dt_raw_t, bias_x, bias_c, alog_x, alog_c, dskip_x, bias_r, alog_r,
             e_mat, et_mat, states, dy):
    t_dim = xc.shape[0]
    nc = t_dim // CHUNK

    def body(xc_ref, dtr_ref, dtt_ref, bx_ref, bc_ref, ax_ref, ac_ref, dk_ref, br_ref, ar_ref,
             e_ref, et_ref, st_ref, dy_ref,
             dxc_ref, ddt_ref, small_ref, dstate, dacs_ref, dxdt_ref, acc_x, acc_r):
        step = pl.program_id(0)

        @pl.when(step == 0)
        def _():
            dstate[...] = jnp.zeros_like(dstate)
            acc_x[...] = jnp.zeros_like(acc_x)
            acc_r[...] = jnp.zeros_like(acc_r)

        dtx, dt, a_x, acs, acs_c = _ssd_common(dtr_ref, dtt_ref, e_ref, bx_ref[...], bc_ref[...],
                                               ax_ref[...], ac_ref[...])
        last = acs[CHUNK - 1:CHUNK, :]
        xs32 = xc_ref[:, 0:D_INNER].astype(F32)
        xdt = xs32 * dt
        xdt16 = xdt.astype(BF16)
        dte = jnp.exp(last - acs)
        xdte = xdt * dte
        xdte16 = xdte.astype(BF16)
        cd = jnp.exp(last)
        dyv = dy_ref[...]
        dy16 = dyv.astype(BF16)
        dye = dyv * jnp.exp(acs)
        dye16 = dye.astype(BF16)
        causal = (lax.broadcasted_iota(jnp.int32, (CHUNK, CHUNK), 0)
                  >= lax.broadcasted_iota(jnp.int32, (CHUNK, CHUNK), 1))
        left = lax.broadcasted_iota(jnp.int32, (CHUNK, PAIR), 1) < HEAD_DIM
        lane_id = lax.broadcasted_iota(jnp.int32, (CHUNK, 128), 1)
        row_id = lax.broadcasted_iota(jnp.int32, (CHUNK, 128), 0)
        is_last_row = lax.broadcasted_iota(jnp.int32, (CHUNK, 1), 0) == CHUNK - 1
        dacs_cols = jnp.zeros((CHUNK, 128), F32)
        dacs_rows = jnp.zeros((CHUNK, 128), F32)
        for g in range(N_GROUPS):
            gl = slice(g * GROUP_LANES, (g + 1) * GROUP_LANES)
            b_lanes = slice(D_INNER + g * D_STATE, D_INNER + (g + 1) * D_STATE)
            c_lanes = slice(D_INNER + (N_GROUPS + g) * D_STATE, D_INNER + (N_GROUPS + g + 1) * D_STATE)
            bg = xc_ref[:, b_lanes]
            cg = xc_ref[:, c_lanes]
            cb = lax.dot_general(cg, bg, _DIMS["nt"], preferred_element_type=F32)
            hprev = st_ref[0, :, gl]
            hp16 = hprev.astype(BF16)
            dhn = dstate[:, gl]
            dhn16 = dhn.astype(BF16)
            ch = jnp.dot(cg, hp16, preferred_element_type=F32)
            gmat = jnp.dot(bg, dhn16, preferred_element_type=F32)
            gx = gmat * xdte[:, gl]
            dlast = jnp.sum(gx, axis=0, keepdims=True) + cd[:, gl] * jnp.sum(dhn * hprev, axis=0, keepdims=True)
            dacs_ref[:, gl] = dye[:, gl] * ch - gx + jnp.where(is_last_row, dlast, 0.0)
            dc_acc = lax.dot_general(dye16[:, gl], hp16, _DIMS["nt"], preferred_element_type=F32)
            db_acc = lax.dot_general(xdte16[:, gl], dhn16, _DIMS["nt"], preferred_element_type=F32)
            dstate[:, gl] = dhn * cd[:, gl] + lax.dot_general(cg, dye16[:, gl], _DIMS["tn"],
                                                             preferred_element_type=F32)
            dcb = jnp.zeros((CHUNK, CHUNK), F32)
            for j in range(HEADS_PER_GROUP // 2):
                pl_ = slice(g * GROUP_LANES + j * PAIR, g * GROUP_LANES + (j + 1) * PAIR)
                h0 = g * HEADS_PER_GROUP + 2 * j
                l0, l1 = _pair_decay(acs[:, pl_], acs_c, h0, causal, left)
                m0, m1 = cb * l0, cb * l1
                lhs = jnp.concatenate([m0.astype(BF16), m1.astype(BF16)], axis=1)
                dyp = dy16[:, pl_]
                zero = jnp.zeros_like(dyp)
                both = lax.dot_general(lhs, dyp, _DIMS["tn"], preferred_element_type=F32)
                dxdt_ref[:, pl_] = (jnp.where(left, both[0:CHUNK, :], both[CHUNK:, :])
                                    + gmat[:, j * PAIR:(j + 1) * PAIR] * dte[:, pl_])
                lhs2 = jnp.concatenate([jnp.where(left, dyp, zero), jnp.where(left, zero, dyp)], axis=0)
                dm = lax.dot_general(lhs2, xdt16[:, pl_], _DIMS["nt"], preferred_element_type=F32)
                dm0, dm1 = dm[0:CHUNK, :], dm[CHUNK:, :]
                dcb = dcb + dm0 * l0 + dm1 * l1
                ds0, ds1 = dm0 * m0, dm1 * m1
                dacs_cols = jnp.where(lane_id == h0, jnp.sum(ds0, axis=1, keepdims=True), dacs_cols)
                dacs_cols = jnp.where(lane_id == h0 + 1, jnp.sum(ds1, axis=1, keepdims=True), dacs_cols)
                dacs_rows = jnp.where(row_id == h0, jnp.sum(ds0, axis=0, keepdims=True), dacs_rows)
                dacs_rows = jnp.where(row_id == h0 + 1, jnp.sum(ds1, axis=0, keepdims=True), dacs_rows)
            dcb16 = dcb.astype(BF16)
            dxc_ref[:, c_lanes] = (dc_acc + jnp.dot(dcb16, bg, preferred_element_type=F32)).astype(BF16)
            dxc_ref[:, b_lanes] = (db_acc + lax.dot_general(dcb16, cg, _DIMS["tn"],
                                                           preferred_element_type=F32)).astype(BF16)
        dxdt = dxdt_ref[...]
        dxc_ref[:, 0:D_INNER] = (dxdt * dt + dk_ref[...] * dyv).astype(BF16)
        dadt = _rev_cumsum_rows(dacs_ref[...])
        ddraw_x = (dxdt * xs32 + dadt * a_x) * _sigmoid(dtx + bx_ref[...])
        acc_x[0:1, :] += jnp.sum(dadt * dt, axis=0, keepdims=True) * a_x
        acc_x[1:2, :] += jnp.sum(ddraw_x, axis=0, keepdims=True)
        acc_x[2:3, :] += jnp.sum(dyv * xs32, axis=0, keepdims=True)
        a_r = -jnp.exp(ar_ref[...])
        pre_r = dtr_ref[...] + br_ref[...]
        dadt_r = _rev_cumsum_rows(dacs_cols - dacs_rows.T)
        ddraw_r = jnp.where(lane_id < N_HEADS, dadt_r * a_r * _sigmoid(pre_r), 0.0)
        acc_r[0:1, :] += jnp.where(lane_id[0:1, :] < N_HEADS,
                                   jnp.sum(dadt_r * _softplus(pre_r), axis=0, keepdims=True) * a_r, 0.0)
        acc_r[1:2, :] += jnp.sum(ddraw_r, axis=0, keepdims=True)
        ddt_ref[...] = ddraw_r + _reduce_heads(ddraw_x, et_ref[...], 2)

        @pl.when(step == nc - 1)
        def _():
            small_ref[...] = acc_r[...] + _reduce_heads(acc_x[...], et_ref[...], 3)

    rev = lambda w: pl.BlockSpec((CHUNK, w), lambda c: (nc - 1 - c, 0))
    return _pcall(
        body, name="ssd_bwd",
        out_shape=(jax.ShapeDtypeStruct((t_dim, CONV_DIM), BF16),
                   jax.ShapeDtypeStruct((t_dim, 128), F32),
                   jax.ShapeDtypeStruct((8, 128), F32)),
        grid=(nc,),
        in_specs=[rev(CONV_DIM), rev(128), pl.BlockSpec((N_HEADS, CHUNK), lambda c: (0, nc - 1 - c)),
                  _const_spec((1, D_INNER)), _const_spec((N_HEADS, 1)),
                  _const_spec((1, D_INNER)), _const_spec((N_HEADS, 1)), _const_spec((1, D_INNER)),
                  _const_spec((1, 128)), _const_spec((1, 128)),
                  _const_spec((128, D_INNER)), _const_spec((D_INNER, 128)),
                  pl.BlockSpec((1, D_STATE, D_INNER), lambda c: (nc - 1 - c, 0, 0)),
                  rev(D_INNER)],
        out_specs=(rev(CONV_DIM), rev(128), _const_spec((8, 128))),
        scratch_shapes=[pltpu.VMEM((D_STATE, D_INNER), F32), pltpu.VMEM((CHUNK, D_INNER), F32),
                        pltpu.VMEM((CHUNK, D_INNER), F32), pltpu.VMEM((8, D_INNER), F32),
                        pltpu.VMEM((8, 128), F32)],
        compiler_params=_params("arbitrary"),
    )(xc, dt_raw, dt_raw_t, bias_x, bias_c, alog_x, alog_c, dskip_x, bias_r, alog_r,
      e_mat, et_mat, states, dy)


_GATE_GROUP = D_INNER // N_GROUPS


def _gate_fwd(y, z, g, *, tb=256):
    t_dim = y.shape[0]

    def body(y_ref, z_ref, g_ref, o_ref):
        for gi in range(N_GROUPS):
            lanes = slice(gi * _GATE_GROUP, (gi + 1) * _GATE_GROUP)
            zv = z_ref[:, lanes].astype(F32)
            wv = y_ref[:, lanes] * (zv * _sigmoid(zv))
            o_ref[:, lanes] = (wv * _rms(wv) * g_ref[:, lanes]).astype(BF16)

    return _pcall(
        body, name="gate_fwd", out_shape=jax.ShapeDtypeStruct((t_dim, D_INNER), BF16),
        grid=(t_dim // tb,),
        in_specs=[_row_spec(tb, D_INNER), _row_spec(tb, D_INNER), _const_spec((1, D_INNER))],
        out_specs=_row_spec(tb, D_INNER), compiler_params=_params("parallel"),
    )(y, z, g)


def _gate_bwd(dyn, y, z, g, *, tb=256):
    t_dim = y.shape[0]

    def body(d_ref, y_ref, z_ref, g_ref, dy_ref, dz_ref, dg_ref):
        @pl.when(pl.program_id(0) == 0)
        def _():
            dg_ref[...] = jnp.zeros_like(dg_ref)

        for gi in range(N_GROUPS):
            lanes = slice(gi * _GATE_GROUP, (gi + 1) * _GATE_GROUP)
            zv = z_ref[:, lanes].astype(F32)
            sg = _sigmoid(zv)
            sz = zv * sg
            yv = y_ref[:, lanes]
            wv = yv * sz
            r = _rms(wv)
            what = wv * r
            dv = d_ref[:, lanes]
            dwhat = dv * g_ref[:, lanes]
            dw = r * (dwhat - what * jnp.mean(dwhat * what, axis=-1, keepdims=True))
            dg_ref[0:1, lanes] += jnp.sum(dv * what, axis=0, keepdims=True)
            dy_ref[:, lanes] = dw * sz
            dz_ref[:, lanes] = (dw * yv * (sg * (1.0 + zv * (1.0 - sg)))).astype(BF16)

    return _pcall(
        body, name="gate_bwd",
        out_shape=(jax.ShapeDtypeStruct((t_dim, D_INNER), F32),
                   jax.ShapeDtypeStruct((t_dim, D_INNER), BF16),
                   jax.ShapeDtypeStruct((8, D_INNER), F32)),
        grid=(t_dim // tb,),
        in_specs=[_row_spec(tb, D_INNER), _row_spec(tb, D_INNER), _row_spec(tb, D_INNER),
                  _const_spec((1, D_INNER))],
        out_specs=(_row_spec(tb, D_INNER), _row_spec(tb, D_INNER), _const_spec((8, D_INNER))),
        compiler_params=_params("arbitrary"),
    )(dyn, y, z, g)


def _loss_head(h, g, target, *, tb=512):
    t_dim, d = h.shape

    def body(h_ref, g_ref, t_ref, dh_ref, dh16_ref, small_ref):
        @pl.when(pl.program_id(0) == 0)
        def _():
            small_ref[...] = jnp.zeros_like(small_ref)

        x = h_ref[...]
        r = _rms(x)
        xhat = x * r
        gv = g_ref[...]
        err = xhat * gv - t_ref[...]
        small_ref[1:2, :] += (0.5 / d) * jnp.sum(err * err, axis=0, keepdims=True)
        dyv = err * (1.0 / d)
        dxhat = dyv * gv
        dh = r * (dxhat - xhat * jnp.mean(dxhat * xhat, axis=-1, keepdims=True))
        dh_ref[...] = dh
        dh16_ref[...] = dh.astype(BF16)
        small_ref[0:1, :] += jnp.sum(dyv * xhat, axis=0, keepdims=True)

    return _pcall(
        body, name="loss_head",
        out_shape=(jax.ShapeDtypeStruct((t_dim, d), F32), jax.ShapeDtypeStruct((t_dim, d), BF16),
                   jax.ShapeDtypeStruct((8, d), F32)),
        grid=(t_dim // tb,),
        in_specs=[_row_spec(tb, d), _const_spec((1, d)), _row_spec(tb, d)],
        out_specs=(_row_spec(tb, d), _row_spec(tb, d), _const_spec((8, d))),
        compiler_params=_params("arbitrary"),
    )(h, g, target)


def _adamw(w, g, m, v, *, name, part=None, into=None):
    r_dim, c = w.shape
    rows = r_dim if part is None else r_dim // 2
    assert g.shape == (rows, c)
    tb = rows
    for cand in (512, 256, 128, 64, 32, 16, 8):
        if rows % cand == 0:
            tb = cand
            break
    first = 0 if part is None else part * (rows // tb)
    c1 = 1.0 / (1.0 - ADAM_B1 ** ADAM_STEP)
    c2 = 1.0 / (1.0 - ADAM_B2 ** ADAM_STEP)

    def body(w_ref, g_ref, m_ref, v_ref, *rest):
        d_ref, mo_ref, vo_ref = rest[-3:]
        gv = g_ref[...]
        mn = ADAM_B1 * m_ref[...] + (1.0 - ADAM_B1) * gv
        vn = ADAM_B2 * v_ref[...] + (1.0 - ADAM_B2) * (gv * gv)
        mo_ref[...] = mn
        vo_ref[...] = vn
        d_ref[...] = -ADAM_LR * ((mn * c1) / (jnp.sqrt(vn * c2) + ADAM_EPS) + ADAM_WD * w_ref[...])

    spec = pl.BlockSpec((tb, c), lambda i: (first + i, 0))
    sds = jax.ShapeDtypeStruct((r_dim, c), F32)
    in_specs = [spec, _row_spec(tb, c), spec, spec]
    operands = [w, g, m, v]
    aliases = {}
    if into is not None:
        in_specs += [_ANY] * 3
        operands += list(into)
        aliases = {4: 0, 5: 1, 6: 2}
    return _pcall(
        body, name=name, out_shape=(sds, sds, sds), grid=(rows // tb,),
        in_specs=in_specs, out_specs=(spec,) * 3, input_output_aliases=aliases,
        compiler_params=_params("parallel"),
    )(*operands)


def _pair_sum(grad, recv, place, *, name):
    s_dim, r_dim, c = grad.shape
    half = r_dim // 2
    tb = 256 if half % 256 == 0 else half
    per_half = half // tb

    def body(place_ref, a_ref, b_ref, o16_ref, o32_ref):
        s = a_ref[...] + b_ref[...]
        o16_ref[...] = s.astype(BF16)

        @pl.when(pl.program_id(1) == place_ref[1])
        def _():
            o32_ref[...] = s[0]

    grid_spec = pltpu.PrefetchScalarGridSpec(
        num_scalar_prefetch=1, grid=(per_half, s_dim),
        in_specs=[pl.BlockSpec((1, tb, c), lambda i, s, p: (s, p[0] * per_half + i, 0)),
                  pl.BlockSpec((1, tb, c), lambda i, s, p: (s, i, 0))],
        out_specs=(pl.BlockSpec((1, tb, c), lambda i, s, p: (s, i, 0)),
                   pl.BlockSpec((tb, c), lambda i, s, p: (i, 0))))
    return _pcall(
        body, name=name, grid_spec=grid_spec,
        out_shape=(jax.ShapeDtypeStruct((s_dim, half, c), BF16), jax.ShapeDtypeStruct((half, c), F32)),
        compiler_params=_params("parallel", "arbitrary"),
    )(place, grad, recv)


def _chip_sum(own, recv, *, name):
    r_dim, c = own.shape
    tb = 256 if r_dim % 256 == 0 else r_dim

    def body(a_ref, b_ref, o_ref):
        s = a_ref[...]
        for k in range(1, N_CHIPS):
            s = s + b_ref[k].astype(F32)
        o_ref[...] = s

    return _pcall(
        body, name=name, out_shape=jax.ShapeDtypeStruct((r_dim, c), F32),
        grid=(r_dim // tb,),
        in_specs=[_row_spec(tb, c), pl.BlockSpec((N_CHIPS, tb, c), lambda i: (0, i, 0))],
        out_specs=_row_spec(tb, c), compiler_params=_params("parallel"),
    )(own, recv)


def _position():
    return lax.axis_index("x"), lax.axis_index("y"), lax.axis_index("c")


def _chip_peer(x, y, k):
    return x ^ (k >> 1), y ^ (k & 1)


_ANY = pl.BlockSpec(memory_space=pl.ANY)


def _all_gather_weights(shards):
    n = len(shards)
    hops = N_CHIPS - 1

    def body(*refs):
        srcs, outs = refs[:n], refs[n:2 * n]
        send_sems, recv_sems = refs[2 * n:]
        x, y, c = _position()
        me = 2 * x + y

        def over_ici(w, k, chip, to):
            return pltpu.make_async_remote_copy(
                src_ref=srcs[w].at[c], dst_ref=outs[w].at[chip, c],
                send_sem=send_sems.at[w, k - 1], recv_sem=recv_sems.at[w, k - 1],
                device_id=to, device_id_type=MESH)

        def over_d2d(w, k, chip, half):
            return pltpu.make_async_remote_copy(
                src_ref=outs[w].at[chip, half], dst_ref=outs[w].at[chip, half],
                send_sem=send_sems.at[w, hops + k - 1], recv_sem=recv_sems.at[w, hops + k - 1],
                device_id=(x, y, 1 - c), device_id_type=MESH)

        sends = []
        for w in range(n):
            for k in range(1, N_CHIPS):
                px, py = _chip_peer(x, y, k)
                cp = over_ici(w, k, me, (px, py, c))
                cp.start()
                sends.append(cp)
        for w in range(n):
            for k in range(1, N_CHIPS):
                px, py = _chip_peer(x, y, k)
                over_ici(w, k, 2 * px + py, (px, py, c)).wait_recv()
                cp = over_d2d(w, k, 2 * px + py, c)
                cp.start()
                sends.append(cp)
        for w in range(n):
            for k in range(1, N_CHIPS):
                px, py = _chip_peer(x, y, k)
                over_d2d(w, k, 2 * px + py, 1 - c).wait_recv()
        for cp in sends:
            cp.wait_send()

    return _pcall(
        body, name="gather_weights",
        out_shape=tuple(jax.ShapeDtypeStruct((N_CHIPS,) + s.shape, s.dtype) for s in shards),
        in_specs=[_ANY] * n, out_specs=(_ANY,) * n,
        scratch_shapes=[pltpu.SemaphoreType.DMA((n, 2 * hops)),
                        pltpu.SemaphoreType.DMA((n, 2 * hops))],
    )(*shards)


def _pair_copies(srcs, lands, send_sems, recv_sems):
    x, y, c = _position()
    copies = []
    for w in range(len(srcs)):
        half = srcs[w].shape[1] // 2
        copies.append(pltpu.make_async_remote_copy(
            src_ref=srcs[w].at[:, pl.ds((1 - c) * half, half), :], dst_ref=lands[w],
            send_sem=send_sems.at[w], recv_sem=recv_sems.at[w],
            device_id=(x, y, 1 - c), device_id_type=MESH))
    return copies


def _chip_copies(srcs, lands, send_sems, recv_sems):
    x, y, c = _position()
    copies = []
    for w in range(len(srcs)):
        for k in range(1, N_CHIPS):
            px, py = _chip_peer(x, y, k)
            i = w * (N_CHIPS - 1) + k - 1
            copies.append(pltpu.make_async_remote_copy(
                src_ref=srcs[w].at[2 * px + py], dst_ref=lands[w].at[k],
                send_sem=send_sems.at[i], recv_sem=recv_sems.at[i],
                device_id=(px, py, c), device_id_type=MESH))
    return copies


def _gather_copies(srcs, lands, send_sems, recv_sems):
    x, y, c = _position()
    me = 2 * x + y
    copies = []
    for w in range(len(srcs)):
        for k in range(1, N_CHIPS):
            px, py = _chip_peer(x, y, k)
            i = w * (N_CHIPS - 1) + k - 1
            copies.append(pltpu.make_async_remote_copy(
                src_ref=srcs[w].at[c], dst_ref=lands[w].at[me, c],
                send_sem=send_sems.at[i], recv_sem=recv_sems.at[i],
                device_id=(px, py, c), device_id_type=MESH))
    return copies


def _exchange(name, copies_of, n_copies, srcs, land_shapes):
    n = len(srcs)

    def body(*refs):
        copies = copies_of(refs[:n], refs[n:2 * n], refs[2 * n], refs[2 * n + 1])
        for cp in copies:
            cp.start()
        for cp in copies:
            cp.wait_recv()
        for cp in copies:
            cp.wait_send()

    return _pcall(
        body, name=name, out_shape=tuple(land_shapes),
        in_specs=[_ANY] * n, out_specs=(_ANY,) * n,
        scratch_shapes=[pltpu.SemaphoreType.DMA((n_copies,)), pltpu.SemaphoreType.DMA((n_copies,))],
    )(*srcs)


_HBM = pl.BlockSpec(memory_space=pltpu.HBM)
_SEM = pl.BlockSpec(memory_space=pltpu.SEMAPHORE)
_DATAFLOW = pltpu.SideEffectType.DATAFLOW_SIDE_EFFECTING


def _exchange_start(name, copies_of, n_copies, srcs, land_shapes):
    n = len(srcs)
    lands = [lax.empty(s.shape, s.dtype) for s in land_shapes]

    def body(*refs):
        for cp in copies_of(refs[:n], refs[n:2 * n], refs[2 * n], refs[2 * n + 1]):
            cp.start()
        refs[-1][...] = jnp.zeros_like(refs[-1])

    through = [pltpu.HBM(a.shape, a.dtype) for a in list(srcs) + lands]
    outs = _pcall(
        body, name=name,
        out_shape=(pltpu.SemaphoreType.DMA((n_copies,)), pltpu.SemaphoreType.DMA((n_copies,)),
                   *through, jax.ShapeDtypeStruct((8, 128), F32)),
        in_specs=[_HBM] * (2 * n),
        out_specs=(_SEM, _SEM, *([_HBM] * (2 * n)), pl.BlockSpec(memory_space=pltpu.VMEM)),
        input_output_aliases={i: 2 + i for i in range(2 * n)},
        compiler_params=pltpu.CompilerParams(has_side_effects=_DATAFLOW),
    )(*[pltpu.with_memory_space_constraint(a, pltpu.HBM) for a in list(srcs) + lands])
    return outs[:-1], outs[-1][0, 0]


def _exchange_wait(name, copies_of, state, after):
    send_sems, recv_sems, through = state[0], state[1], state[2:]
    n = len(through) // 2
    if after.ndim == 0:
        after = jnp.broadcast_to(after, (8, 128))
    after = pltpu.with_memory_space_constraint(after, pltpu.HBM)

    def body(*refs):
        for cp in copies_of(refs[:n], refs[n:2 * n], refs[2 * n], refs[2 * n + 1]):
            cp.wait_send()
            cp.wait_recv()

    outs = _pcall(
        body, name=name,
        out_shape=tuple(pltpu.HBM(a.shape, a.dtype) for a in through),
        in_specs=[_HBM] * (2 * n) + [_SEM, _SEM, _HBM], out_specs=tuple([_HBM] * (2 * n)),
        input_output_aliases={i: i for i in range(2 * n)},
        compiler_params=pltpu.CompilerParams(has_side_effects=_DATAFLOW),
    )(*through, send_sems, recv_sems, after)
    return outs[:n], outs[n:]


def _forward_halves(lands, *, name):
    n = len(lands)
    hops = N_CHIPS - 1

    def body(*refs):
        ins, outs = refs[:n], refs[n:2 * n]
        send_sems, recv_sems = refs[2 * n], refs[2 * n + 1]
        x, y, c = _position()
        copies = []
        for w in range(n):
            for k in range(1, N_CHIPS):
                px, py = _chip_peer(x, y, k)
                i = w * hops + k - 1
                copies.append(pltpu.make_async_remote_copy(
                    src_ref=ins[w].at[2 * px + py, c], dst_ref=outs[w].at[2 * px + py, c],
                    send_sem=send_sems.at[i], recv_sem=recv_sems.at[i],
                    device_id=(x, y, 1 - c), device_id_type=MESH))
        for cp in copies:
            cp.start()
        for cp in copies:
            cp.wait_recv()
        for cp in copies:
            cp.wait_send()

    return _pcall(
        body, name=name,
        out_shape=tuple(jax.ShapeDtypeStruct(a.shape, a.dtype) for a in lands),
        in_specs=[_ANY] * n, out_specs=(_ANY,) * n,
        input_output_aliases={i: i for i in range(n)},
        scratch_shapes=[pltpu.SemaphoreType.DMA((n * hops,)), pltpu.SemaphoreType.DMA((n * hops,))],
    )(*lands)


def _pair_lands(grads):
    return [jax.ShapeDtypeStruct((g.shape[0], g.shape[1] // 2, g.shape[2]), F32) for g in grads]


def _same_lands(parts):
    return [jax.ShapeDtypeStruct(p.shape, p.dtype) for p in parts]


def _pair_gather_halves(halves, *, name):
    n = len(halves)

    def body(*refs):
        srcs, outs = refs[:n], refs[n:2 * n]
        send_sems, recv_sems = refs[2 * n:]
        x, y, c = _position()
        sends = []
        for w in range(n):
            cp = pltpu.make_async_remote_copy(
                src_ref=srcs[w], dst_ref=outs[w],
                send_sem=send_sems.at[w], recv_sem=recv_sems.at[w],
                device_id=(x, y, 1 - c), device_id_type=MESH)
            cp.start()
            sends.append(cp)
        for cp in sends:
            cp.wait_recv()
        for cp in sends:
            cp.wait_send()

    theirs = _pcall(
        body, name=name,
        out_shape=tuple(jax.ShapeDtypeStruct(h.shape, F32) for h in halves),
        in_specs=[_ANY] * n, out_specs=(_ANY,) * n,
        scratch_shapes=[pltpu.SemaphoreType.DMA((n,)), pltpu.SemaphoreType.DMA((n,))],
    )(*halves)
    my_c = lax.axis_index("c")
    whole = []
    for mine, other in zip(halves, theirs):
        both = jnp.stack([other, other])
        both = lax.dynamic_update_index_in_dim(both, mine, my_c, axis=0)
        whole.append(both.reshape(2 * mine.shape[0], mine.shape[1]))
    return whole


def _all_reduce_small(packed, *, name, sum_row0):
    r_dim, c = packed.shape

    def body(src_ref, out_ref, recv_ref, send_sems, recv_sems):
        x, y, c_ = _position()
        me = 4 * x + 2 * y + c_
        recv_ref[0] = src_ref[...]
        sends = []
        for k in range(1, N_DEV):
            peer = (x ^ (k >> 2), y ^ ((k >> 1) & 1), c_ ^ (k & 1))
            cp = pltpu.make_async_remote_copy(
                src_ref=src_ref, dst_ref=recv_ref.at[k],
                send_sem=send_sems.at[k - 1], recv_sem=recv_sems.at[k - 1],
                device_id=peer, device_id_type=MESH)
            cp.start()
            sends.append(cp)
        for cp in sends:
            cp.wait_recv()
        total = recv_ref[me]
        for d in range(1, N_DEV):
            total = total + recv_ref[d ^ me]
        if sum_row0:
            row0 = jnp.sum(total[0:1, :], axis=1, keepdims=True)
            rid = lax.broadcasted_iota(jnp.int32, total.shape, 0)
            total = jnp.where(rid == 0, row0, total)
        out_ref[...] = total
        for cp in sends:
            cp.wait_send()

    return _pcall(
        body, name=name, out_shape=jax.ShapeDtypeStruct((r_dim, c), F32),
        in_specs=[pl.BlockSpec(memory_space=pltpu.VMEM)],
        out_specs=pl.BlockSpec(memory_space=pltpu.VMEM),
        scratch_shapes=[pltpu.VMEM((N_DEV, r_dim, c), F32),
                        pltpu.SemaphoreType.DMA((N_DEV - 1,)), pltpu.SemaphoreType.DMA((N_DEV - 1,))],
    )(packed)


def _pad_lanes(v, width):
    return jnp.pad(v, ((0, 0), (0, width - v.shape[1])))


def _pad_rows(v, rows):
    pad = [(0, 0)] * v.ndim
    pad[-2] = (0, rows - v.shape[-2])
    return jnp.pad(v, pad)


_IN_PROJ_SHARD_ROWS = 1312


def _rows_1024(v):
    flat = v.reshape(-1)
    pad = (-flat.shape[0]) % D_MODEL
    return jnp.pad(flat, (0, pad)).reshape(-1, D_MODEL)


def _local_step(xs, target, pw, w1_0, fetch, reduce_start, reduce_midway,
                conv_w, conv_b, gate_g,
                norm_mix_g, norm_mlp_g, pool_b, pool_scale, ssm_dt_bias, ssm_a_log, ssm_d, final_g):
    bias_r = _pad_lanes(ssm_dt_bias, 128)
    alog_r = _pad_lanes(ssm_a_log, 128)
    bias_x = jnp.repeat(ssm_dt_bias, HEAD_DIM, axis=1)
    alog_x = jnp.repeat(ssm_a_log, HEAD_DIM, axis=1)
    dskip_x = jnp.repeat(ssm_d, HEAD_DIM, axis=1)
    bias_c = ssm_dt_bias.reshape(N_HEADS, 1)
    alog_c = ssm_a_log.reshape(N_HEADS, 1)
    e_mat = _head_lane_matrix()

    g_mix0, g_mix1 = norm_mix_g[0:1], norm_mix_g[1:2]
    g_mlp0, g_mlp1 = norm_mlp_g[0:1], norm_mlp_g[1:2]
    fg = final_g.reshape(1, D_MODEL)

    h1 = _pool_fwd(xs, g_mix0, pw, pool_b, pool_scale)
    hm0 = _rmsnorm_fwd(h1, g_mlp0, name="norm_mlp0")
    u0 = _matmul(hm0, w1_0, "nn", name="mlp0_up", out_dtype=BF16, b_col_shards=True)
    w2_0 = fetch("mlp0_down", u0)
    h2 = _matmul(u0, w2_0, "nn", name="mlp0_down", a_relu2=True, add=h1)

    w_z, w_xbc, w_dt = fetch("in_proj", h2)
    hn1 = _rmsnorm_fwd(h2, g_mix1, name="norm_mix1")
    z = _matmul(hn1, w_z, "nt", name="in_proj_z", out_dtype=BF16)
    xbc = _matmul(hn1, w_xbc, "nt", name="in_proj_xbc", out_dtype=BF16)
    dt_raw = _matmul(hn1, w_dt, "nt", name="in_proj_dt")
    dt_raw_t = dt_raw[:, :N_HEADS].T
    xc = _conv_fwd(xbc, conv_w, conv_b)
    wout, w1_1, w2_1 = fetch("rest", xc)
    y, states = _ssd_fwd(xc, dt_raw, dt_raw_t, bias_x, bias_c, alog_x, alog_c, dskip_x, e_mat)
    yn = _gate_fwd(y, z, gate_g)
    h3 = _matmul(yn, wout, "nn", name="out_proj", add=h2)
    hm1 = _rmsnorm_fwd(h3, g_mlp1, name="norm_mlp1")
    u1 = _matmul(hm1, w1_1, "nn", name="mlp1_up", out_dtype=BF16, b_col_shards=True)
    h4 = _matmul(u1, w2_1, "nn", name="mlp1_down", a_relu2=True, add=h3)

    dh4, dh4_16, small_final = _loss_head(h4, fg, target)

    def mlp_bwd_weights(dh_out16, hm, u, w2_i, tag):
        du = _matmul(dh_out16, w2_i, "nt", name=tag + "_du", out_dtype=BF16, relu2_grad_of=u)
        dw2 = _matmul(u, dh_out16, "tn", name=tag + "_dw2", a_relu2=True)
        dw1 = _matmul(hm, du, "tn", name=tag + "_dw1", out_col_shards=N_CHIPS)
        return du, dw1, dw2.reshape(N_CHIPS, D_FF // N_CHIPS, D_MODEL)

    def mlp_bwd_input(du, dh_out, h_in, w1_i, g_i, tag):
        dhm = _matmul(du, w1_i, "nt", name=tag + "_dhm", b_col_shards=True)
        return _rmsnorm_bwd(dhm, h_in, g_i, dh_out, name=tag + "_norm_bwd")

    du1, dw1_1, dw2_1 = mlp_bwd_weights(dh4_16, hm1, u1, w2_1, "mlp1")
    dh3, dh3_16, dg_mlp1 = mlp_bwd_input(du1, dh4, h3, w1_1, g_mlp1, "mlp1")

    dyn = _matmul(dh3_16, wout, "nt", name="out_proj_dyn")
    dwout = _matmul(yn, dh3_16, "tn", name="out_proj_dw").reshape(N_CHIPS, D_INNER // N_CHIPS, D_MODEL)
    behind = reduce_start("mlp1_out", [dw1_1, dw2_1, dwout])
    dy, dz, dg_gate = _gate_bwd(dyn, y, z, gate_g + behind)
    behind = reduce_midway("mlp1_out", dz)
    dxc, ddt_raw, small_ssd = _ssd_bwd(xc, dt_raw, dt_raw_t, bias_x, bias_c, alog_x, alog_c, dskip_x + behind,
                                       bias_r, alog_r, e_mat, e_mat.T, states, dy)
    dv, dconv = _conv_bwd_act(xbc, dxc, conv_w, conv_b)
    dxbc = _conv_bwd_in(dv, conv_w)
    dhn1 = _matmul(ddt_raw, w_dt, "nn", name="in_proj_dt_dh")
    dhn1 = _matmul(dz, w_z, "nn", name="in_proj_z_dh", add=dhn1)
    dhn1 = _matmul(dxbc, w_xbc, "nn", name="in_proj_xbc_dh", add=dhn1)
    dw_z = _matmul(dz, hn1, "tn", name="in_proj_z_dw")
    dw_xbc = _matmul(dxbc, hn1, "tn", name="in_proj_xbc_dw")
    dw_dt = _matmul(ddt_raw, hn1, "tn", name="in_proj_dt_dw")
    dwin = jnp.concatenate([dw_z, dw_xbc, dw_dt[:N_HEADS]], axis=0)
    dwin = _pad_rows(dwin.reshape(N_CHIPS, IN_PROJ_DIM // N_CHIPS, D_MODEL), _IN_PROJ_SHARD_ROWS)
    behind = reduce_start("in_proj", [dwin])
    dh2, dh2_16, dg_mix1 = _rmsnorm_bwd(dhn1, h2, g_mix1 + behind, dh3, name="norm_mix1_bwd")
    behind = reduce_midway("in_proj", dh2_16)

    du0, dw1_0, dw2_0 = mlp_bwd_weights(dh2_16, hm0, u0, w2_0, "mlp0")
    dh1, _, dg_mlp0 = mlp_bwd_input(du0, dh2, h1, w1_0, g_mlp0 + behind, "mlp0")
    dx, dpw, small_pool = _pool_bwd(xs, g_mix0, pw, pool_b, pool_scale, dh1)
    dpw = jnp.transpose(dpw.reshape(4, N_CHIPS, POOL_GROUP // N_CHIPS, POOL_GROUP), (1, 0, 2, 3))
    dpw = dpw.reshape(N_CHIPS, 4 * (POOL_GROUP // N_CHIPS), POOL_GROUP)

    big = [dpw, dw1_0, dw2_0]
    rows = [
        small_final[1:2],
        small_final[0:1],
        small_pool[0:1], dg_mix1[0:1],
        dg_mlp0[0:1], dg_mlp1[0:1],
        small_pool[1:2], small_pool[2:3],
        _pad_lanes(small_ssd[0:3], D_MODEL),
        _rows_1024(dg_gate[0:1]),
        _rows_1024(dconv[0:CONV_K]),
        _rows_1024(dconv[CONV_K:CONV_K + 1]),
    ]
    return dx, big, rows


def kernel(x, norm_mix_g, norm_mlp_g, pool_w, pool_b, pool_scale, ssm_w_in, ssm_conv_w, ssm_conv_b, ssm_dt_bias, ssm_a_log, ssm_d, ssm_norm_g, ssm_w_out, mlp_w1, mlp_w2, final_g, loss_target, m_norm_mix_g, m_norm_mlp_g, m_pool_w, m_pool_b, m_pool_scale, m_ssm_w_in, m_ssm_conv_w, m_ssm_conv_b, m_ssm_dt_bias, m_ssm_a_log, m_ssm_d, m_ssm_norm_g, m_ssm_w_out, m_mlp_w1, m_mlp_w2, m_final_g, v_norm_mix_g, v_norm_mlp_g, v_pool_w, v_pool_b, v_pool_scale, v_ssm_w_in, v_ssm_conv_w, v_ssm_conv_b, v_ssm_dt_bias, v_ssm_a_log, v_ssm_d, v_ssm_norm_g, v_ssm_w_out, v_mlp_w1, v_mlp_w2, v_final_g):
    t_dim = x.shape[1]
    xs = x[0]
    target = loss_target[0]
    my_x, my_y, my_c = _position()
    my_chip = 2 * my_x + my_y

    def halves(w):
        return w.astype(BF16).reshape((2, w.shape[0] // 2) + w.shape[1:])

    def whole(gathered, own_shard):
        g = lax.dynamic_update_index_in_dim(gathered, own_shard, my_chip, axis=0)
        return g.reshape((N_CHIPS, 2 * g.shape[2]) + g.shape[3:])

    def gather_lands(own):
        return [jax.ShapeDtypeStruct((N_CHIPS,) + s.shape, s.dtype) for s in own]

    vec_cols = CONV_DIM // N_CHIPS
    vec_own = jnp.concatenate([ssm_conv_w[0], ssm_conv_b, _pad_lanes(ssm_norm_g, vec_cols)], axis=0)
    early_own = [halves(pool_w[0]), halves(mlp_w1[0]), vec_own.reshape(2, (CONV_K + 2) // 2, vec_cols)]
    early = _all_gather_weights(early_own)
    g_pool, w1_0, g_vec = [whole(g, o) for g, o in zip(early, early_own)]
    pw = jnp.transpose(g_pool, (1, 0, 2, 3)).reshape(4, POOL_GROUP, POOL_GROUP)
    conv_w = jnp.transpose(g_vec[:, 0:CONV_K, :], (1, 0, 2)).reshape(CONV_K, CONV_DIM)
    conv_b = g_vec[:, CONV_K, :].reshape(1, CONV_DIM)
    gate_g = g_vec[:, CONV_K + 1, :D_INNER // N_CHIPS].reshape(1, D_INNER)

    down_own = [halves(mlp_w2[0])]
    in_own = [halves(_pad_rows(ssm_w_in[0].T, _IN_PROJ_SHARD_ROWS))]
    rest_own = [halves(ssm_w_out[0]), halves(mlp_w1[1]), halves(mlp_w2[1])]
    early, down_own = lax.optimization_barrier((early, down_own))
    fetches = {}
    fetches["mlp0_down"], behind_gather = _exchange_start(
        "gather_mlp0_down_start", _gather_copies, len(down_own) * (N_CHIPS - 1), down_own, gather_lands(down_own))
    behind_gather, in_own = lax.optimization_barrier((behind_gather, in_own))
    fetches["in_proj"], _ = _exchange_start(
        "gather_in_proj_start", _gather_copies, len(in_own) * (N_CHIPS - 1), in_own, gather_lands(in_own))

    def fetch(what, after):
        if what == "mlp0_down":
            own_thru, landed = _exchange_wait("gather_mlp0_down_wait", _gather_copies, fetches[what], after)
            landed = _forward_halves(landed, name="forward_mlp0_down")
            return whole(landed[0], own_thru[0]).reshape(D_FF, D_MODEL)
        if what == "in_proj":
            own_thru, landed = _exchange_wait("gather_in_proj_wait", _gather_copies, fetches["in_proj"], after)
            landed = _forward_halves(landed, name="forward_in_proj")
            landed, rest = lax.optimization_barrier((landed, rest_own))
            fetches["rest"], behind = _exchange_start(
                "gather_rest_start", _gather_copies, len(rest) * (N_CHIPS - 1), rest, gather_lands(rest))
            win = whole(landed[0], own_thru[0])[:, :IN_PROJ_DIM // N_CHIPS].reshape(IN_PROJ_DIM, D_MODEL)
            w_dt = _pad_rows(win[D_INNER + CONV_DIM:], 128) + behind.astype(BF16)
            return win[:D_INNER], win[D_INNER:D_INNER + CONV_DIM], w_dt
        own_thru, landed = _exchange_wait("gather_rest_wait", _gather_copies, fetches["rest"], after)
        landed = _forward_halves(landed, name="forward_rest")
        g_wout, w1_1, g_w2_1 = [whole(g, o) for g, o in zip(landed, own_thru)]
        return g_wout.reshape(D_INNER, D_MODEL), w1_1, g_w2_1.reshape(D_FF, D_MODEL)

    place = jnp.stack([my_c, my_chip]).astype(jnp.int32)
    waves = {}

    def reduce_start(wave, grads):
        waves[wave] = {}
        waves[wave]["pair"], behind = _exchange_start(
            "pair_%s_start" % wave, _pair_copies, len(grads), grads, _pair_lands(grads))
        return behind

    def reduce_midway(wave, after):
        st = waves[wave]
        grads, recv = _exchange_wait("pair_%s_wait" % wave, _pair_copies, st["pair"], after)
        sums = [_pair_sum(g, r, place, name="pair_sum_%s_%d" % (wave, i))
                for i, (g, r) in enumerate(zip(grads, recv))]
        st["f32"] = [s32 for _, s32 in sums]
        b16 = [s16 for s16, _ in sums]
        st["chip"], behind = _exchange_start(
            "chip_%s_start" % wave, _chip_copies, len(b16) * (N_CHIPS - 1), b16, _same_lands(b16))
        return behind

    def reduce_finish(wave, after):
        st = waves[wave]
        _, got = _exchange_wait("chip_%s_wait" % wave, _chip_copies, st["chip"], after)
        return [_chip_sum(s32, r, name="chip_sum_%s_%d" % (wave, i))
                for i, (s32, r) in enumerate(zip(st["f32"], got))]

    dx, big0, rows = _local_step(xs, target, pw, w1_0, fetch, reduce_start, reduce_midway,
                                 conv_w, conv_b, gate_g,
                                 norm_mix_g + behind_gather, norm_mlp_g, pool_b, pool_scale,
                                 ssm_dt_bias, ssm_a_log, ssm_d, final_g)

    behind = reduce_start("layer0", big0)
    small = jnp.concatenate(rows, axis=0)
    small = jnp.pad(small, ((0, (-small.shape[0]) % 8), (0, 0))) + behind
    small = _all_reduce_small(small, name="all_reduce_small", sum_row0=True)
    behind = reduce_midway("layer0", small)
    h_w1_1, h_w2_1, h_wout = reduce_finish("mlp1_out", behind)
    (h_win,) = reduce_finish("in_proj", behind)
    g_w1_1, g_w2_1, g_wout_s, g_win_s = _pair_gather_halves([h_w1_1, h_w2_1, h_wout, h_win],
                                                            name="pair_gather_layer1")
    loss = small[0, 0]
    g_final = small[1]
    g_norm_mix = small[2:4]
    g_norm_mlp = small[4:6]
    g_pool_b, g_pool_scale = small[6:7], small[7:8]
    g_alog, g_dtb, g_dsk = small[8:9, :N_HEADS], small[9:10, :N_HEADS], small[10:11, :N_HEADS]
    g_gate_full = small[11:13].reshape(1, D_INNER)
    g_convw_full = small[13:25].reshape(CONV_K, CONV_DIM)
    g_convb_full = small[25:28].reshape(1, CONV_DIM)
    g_gate = lax.dynamic_slice_in_dim(g_gate_full, my_chip * (D_INNER // N_CHIPS), D_INNER // N_CHIPS, axis=1)
    g_convw = lax.dynamic_slice_in_dim(g_convw_full, my_chip * (CONV_DIM // N_CHIPS), CONV_DIM // N_CHIPS, axis=1)
    g_convb = lax.dynamic_slice_in_dim(g_convb_full, my_chip * (CONV_DIM // N_CHIPS), CONV_DIM // N_CHIPS, axis=1)

    grads = {
        "norm_mix_g": g_norm_mix, "norm_mlp_g": g_norm_mlp,
        "pool_b": g_pool_b, "pool_scale": g_pool_scale,
        "ssm_conv_w": g_convw.reshape(ssm_conv_w.shape),
        "ssm_conv_b": g_convb, "ssm_dt_bias": g_dtb, "ssm_a_log": g_alog, "ssm_d": g_dsk,
        "ssm_norm_g": g_gate, "ssm_w_out": g_wout_s.reshape(ssm_w_out.shape),
        "final_g": g_final,
    }
    weights = dict(norm_mix_g=norm_mix_g, norm_mlp_g=norm_mlp_g, pool_w=pool_w, pool_b=pool_b,
                   pool_scale=pool_scale, ssm_w_in=ssm_w_in, ssm_conv_w=ssm_conv_w, ssm_conv_b=ssm_conv_b,
                   ssm_dt_bias=ssm_dt_bias, ssm_a_log=ssm_a_log, ssm_d=ssm_d, ssm_norm_g=ssm_norm_g,
                   ssm_w_out=ssm_w_out, mlp_w1=mlp_w1, mlp_w2=mlp_w2, final_g=final_g)
    moms = dict(norm_mix_g=(m_norm_mix_g, v_norm_mix_g), norm_mlp_g=(m_norm_mlp_g, v_norm_mlp_g),
                pool_w=(m_pool_w, v_pool_w), pool_b=(m_pool_b, v_pool_b),
                pool_scale=(m_pool_scale, v_pool_scale), ssm_w_in=(m_ssm_w_in, v_ssm_w_in),
                ssm_conv_w=(m_ssm_conv_w, v_ssm_conv_w), ssm_conv_b=(m_ssm_conv_b, v_ssm_conv_b),
                ssm_dt_bias=(m_ssm_dt_bias, v_ssm_dt_bias), ssm_a_log=(m_ssm_a_log, v_ssm_a_log),
                ssm_d=(m_ssm_d, v_ssm_d), ssm_norm_g=(m_ssm_norm_g, v_ssm_norm_g),
                ssm_w_out=(m_ssm_w_out, v_ssm_w_out), mlp_w1=(m_mlp_w1, v_mlp_w1),
                mlp_w2=(m_mlp_w2, v_mlp_w2), final_g=(m_final_g, v_final_g))
    names = list(weights)
    big_names = ("pool_w", "ssm_w_in", "ssm_w_out", "mlp_w1", "mlp_w2")
    deltas, new_m, new_v = {}, {}, {}

    def as_rows(nm, a):
        return a[0].T if nm == "ssm_w_in" else a.reshape(-1, a.shape[-1])

    def from_rows(nm, r):
        return r.T[None] if nm == "ssm_w_in" else r.reshape(weights[nm].shape)

    def update(nm, grad_rows, layer=None, into=None):
        return _adamw(as_rows(nm, weights[nm]), grad_rows, as_rows(nm, moms[nm][0]), as_rows(nm, moms[nm][1]),
                      name="adamw_%s_%s" % (nm, layer), part=layer, into=into)

    def keep(nm, results):
        deltas[nm], new_m[nm], new_v[nm] = [from_rows(nm, r) for r in results]

    g_win_rows = g_win_s[:IN_PROJ_DIM // N_CHIPS]
    grads["ssm_w_in"] = from_rows("ssm_w_in", g_win_rows)
    keep("ssm_w_in", update("ssm_w_in", g_win_rows))
    keep("ssm_w_out", update("ssm_w_out", g_wout_s))
    w1_done = update("mlp_w1", g_w1_1, layer=1)
    w2_done = update("mlp_w2", g_w2_1, layer=1)
    small_names = [nm for nm in names if nm not in big_names]
    sizes = [weights[nm].size for nm in small_names]

    def pack(parts):
        flat = jnp.concatenate([p.reshape(-1) for p in parts])
        pad = (-flat.shape[0]) % (8 * D_MODEL)
        return jnp.pad(flat, (0, pad)).reshape(-1, D_MODEL)

    d_, m_, v_ = _adamw(pack([weights[nm] for nm in small_names]), pack([grads[nm] for nm in small_names]),
                        pack([moms[nm][0] for nm in small_names]), pack([moms[nm][1] for nm in small_names]),
                        name="adamw_small")
    off = 0
    for nm, sz in zip(small_names, sizes):
        shp = weights[nm].shape
        deltas[nm] = d_.reshape(-1)[off:off + sz].reshape(shp)
        new_m[nm] = m_.reshape(-1)[off:off + sz].reshape(shp)
        new_v[nm] = v_.reshape(-1)[off:off + sz].reshape(shp)
        off += sz

    above = (deltas["ssm_w_in"][0, 0, 0] + deltas["ssm_w_out"][0, 0, 0] + w1_done[0][-1, -1]
             + w2_done[0][-1, -1] + d_[0, 0])
    g_pool_w, g_w1_0, g_w2_0 = _pair_gather_halves(reduce_finish("layer0", above), name="pair_gather_layer0")
    keep("mlp_w1", update("mlp_w1", g_w1_0, layer=0, into=w1_done))
    keep("mlp_w2", update("mlp_w2", g_w2_0, layer=0, into=w2_done))
    grads["pool_w"] = g_pool_w.reshape(pool_w.shape)
    grads["mlp_w1"] = jnp.stack([g_w1_0, g_w1_1])
    grads["mlp_w2"] = jnp.stack([g_w2_0, g_w2_1])
    keep("pool_w", update("pool_w", g_pool_w))

    grad_x = dx.reshape(x.shape)
    out_grads = [grads[nm].reshape(weights[nm].shape) for nm in names]
    return (loss, grad_x, *out_grads, *[deltas[nm] for nm in names],
            *[new_m[nm] for nm in names], *[new_v[nm] for nm in names])
```

```python
import functools

import jax
import jax.numpy as jnp
from jax import lax
from jax.experimental import pallas as pl
from jax.experimental.pallas import tpu as pltpu

F32 = jnp.float32
BF16 = jnp.bfloat16
MESH = pl.DeviceIdType.MESH

D_MODEL = 1024
RMS_EPS = 1e-5
POOL_WINDOWS = (2, 4, 8, 16)
POOL_GROUP = 256
POOL_HALO = 16
D_INNER = 2048
HEAD_DIM = 64
N_HEADS = 32
N_GROUPS = 4
HEADS_PER_GROUP = 8
D_STATE = 128
CONV_K = 4
CONV_HALO = 8
CHUNK = 128
CONV_DIM = 3072
IN_PROJ_DIM = 5152
D_FF = 4096
N_CHIPS = 4
N_DEV = 8

ADAM_LR = 0.001
ADAM_B1 = 0.9
ADAM_B2 = 0.999
ADAM_EPS = 1e-08
ADAM_WD = 0.01
ADAM_STEP = 10

VMEM_LIMIT = 56 * 1024 * 1024
NEG_INF = float("-inf")


def _pcall(body, **kw):
    return pl.pallas_call(body, **kw)


def _params(*sem):
    return pltpu.CompilerParams(dimension_semantics=sem, vmem_limit_bytes=VMEM_LIMIT)


def _sigmoid(v):
    return 1.0 / (1.0 + jnp.exp(-v))


def _row_spec(tb, d, nb=None, reverse=False):
    if reverse:
        return pl.BlockSpec((tb, d), lambda i: (nb - 1 - i, 0))
    return pl.BlockSpec((tb, d), lambda i: (i, 0))


def _const_spec(shape):
    return pl.BlockSpec(shape, lambda *_: tuple(0 for _ in shape))


_DIMS = {"nn": (((1,), (0,)), ((), ())),
         "nt": (((1,), (1,)), ((), ())),
         "tn": (((0,), (0,)), ((), ()))}


_MATMUL_VMEM_BUDGET = 40 * 1024 * 1024


def _matmul_tiles(m_dim, n_dim, k_dim, a_bytes, b_bytes, mn_bytes):
    tm, tn = min(m_dim, 1024), min(n_dim, 1024)
    while 2 * (tm * k_dim * a_bytes + tn * k_dim * b_bytes + tm * tn * mn_bytes) > _MATMUL_VMEM_BUDGET:
        if tm >= tn:
            tm //= 2
        else:
            tn //= 2
    return tm, tn


def _matmul(a, b, mode, *, name, out_dtype=F32, a_relu2=False, add=None, relu2_grad_of=None,
            out_col_shards=1, b_col_shards=False):
    if mode == "tn":
        k_dim, m_dim = a.shape
    else:
        m_dim, k_dim = a.shape
    if b_col_shards:
        n_shards, shard_cols = b.shape[0], b.shape[2]
        n_dim = n_shards * shard_cols if mode == "nn" else b.shape[1]
    else:
        n_dim = b.shape[0] if mode == "nt" else b.shape[1]
    mn_bytes = jnp.dtype(out_dtype).itemsize
    if relu2_grad_of is not None:
        mn_bytes += relu2_grad_of.dtype.itemsize
    if add is not None:
        mn_bytes += add.dtype.itemsize
    tm, tn = _matmul_tiles(m_dim, n_dim, k_dim, a.dtype.itemsize, b.dtype.itemsize, mn_bytes)
    assert m_dim % tm == 0 and n_dim % tn == 0
    a_spec = (pl.BlockSpec((k_dim, tm), lambda i, j: (0, i)) if mode == "tn"
              else pl.BlockSpec((tm, k_dim), lambda i, j: (i, 0)))
    if b_col_shards and mode == "nn":
        assert shard_cols % tn == 0
        per_shard = shard_cols // tn
        b_spec = pl.BlockSpec((None, k_dim, tn), lambda i, j: (j // per_shard, 0, j % per_shard))
    elif b_col_shards:
        assert mode == "nt" and k_dim == n_shards * shard_cols
        b_spec = pl.BlockSpec((n_shards, tn, shard_cols), lambda i, j: (0, j, 0))
    else:
        b_spec = (pl.BlockSpec((tn, k_dim), lambda i, j: (j, 0)) if mode == "nt"
                  else pl.BlockSpec((k_dim, tn), lambda i, j: (0, j)))
    mn_spec = pl.BlockSpec((tm, tn), lambda i, j: (i, j))
    operands, in_specs = [a, b], [a_spec, b_spec]
    if relu2_grad_of is not None:
        operands.append(relu2_grad_of)
        in_specs.append(mn_spec)
    if add is not None:
        operands.append(add)
        in_specs.append(mn_spec)
    if out_col_shards == 1:
        out_shape = jax.ShapeDtypeStruct((m_dim, n_dim), out_dtype)
        out_spec = mn_spec
    else:
        n_shard = n_dim // out_col_shards
        assert n_shard % tn == 0
        per = n_shard // tn
        out_shape = jax.ShapeDtypeStruct((out_col_shards, m_dim, n_shard), out_dtype)
        out_spec = pl.BlockSpec((None, tm, tn), lambda i, j: (j // per, i, j % per))

    def body(*refs):
        a_ref, b_ref, o_ref = refs[0], refs[1], refs[-1]
        av = a_ref[...]
        if a_relu2:
            av = jnp.maximum(av, 0)
            av = av * av
        if b_col_shards and mode == "nt":
            r = None
            for s in range(n_shards):
                part = lax.dot_general(av[:, s * shard_cols:(s + 1) * shard_cols].astype(BF16),
                                       b_ref[s].astype(BF16), _DIMS[mode], preferred_element_type=F32)
                r = part if r is None else r + part
        else:
            r = lax.dot_general(av.astype(BF16), b_ref[...].astype(BF16), _DIMS[mode],
                                preferred_element_type=F32)
        nxt = 2
        if relu2_grad_of is not None:
            r = r * (2.0 * jnp.maximum(refs[nxt][...].astype(F32), 0.0))
            nxt += 1
        if add is not None:
            r = r + refs[nxt][...]
        o_ref[...] = r.astype(out_dtype)

    return _pcall(
        body, name=name, out_shape=out_shape,
        grid=(m_dim // tm, n_dim // tn),
        in_specs=in_specs, out_specs=out_spec,
        compiler_params=_params("parallel", "parallel"),
    )(*operands)


def _rms(x):
    return lax.rsqrt(jnp.mean(x * x, axis=-1, keepdims=True) + RMS_EPS)


def _rmsnorm_fwd(h, g, *, name, tb=512):
    t_dim, d = h.shape

    def body(h_ref, g_ref, o_ref):
        x = h_ref[...]
        o_ref[...] = (x * _rms(x) * g_ref[...]).astype(BF16)

    return _pcall(
        body, name=name, out_shape=jax.ShapeDtypeStruct((t_dim, d), BF16),
        grid=(t_dim // tb,), in_specs=[_row_spec(tb, d), _const_spec((1, d))],
        out_specs=_row_spec(tb, d), compiler_params=_params("parallel"),
    )(h, g)


def _rmsnorm_bwd(dy, h, g, dres, *, name, tb=512):
    t_dim, d = h.shape

    def body(dy_ref, h_ref, g_ref, dres_ref, dh_ref, dh16_ref, dg_ref):
        @pl.when(pl.program_id(0) == 0)
        def _():
            dg_ref[...] = jnp.zeros_like(dg_ref)

        x = h_ref[...]
        r = _rms(x)
        xhat = x * r
        dyv = dy_ref[...]
        dxhat = dyv * g_ref[...]
        dh = dres_ref[...] + r * (dxhat - xhat * jnp.mean(dxhat * xhat, axis=-1, keepdims=True))
        dh_ref[...] = dh
        dh16_ref[...] = dh.astype(BF16)
        dg_ref[0:1, :] += jnp.sum(dyv * xhat, axis=0, keepdims=True)

    return _pcall(
        body, name=name,
        out_shape=(jax.ShapeDtypeStruct((t_dim, d), F32), jax.ShapeDtypeStruct((t_dim, d), BF16),
                   jax.ShapeDtypeStruct((8, d), F32)),
        grid=(t_dim // tb,),
        in_specs=[_row_spec(tb, d), _row_spec(tb, d), _const_spec((1, d)), _row_spec(tb, d)],
        out_specs=(_row_spec(tb, d), _row_spec(tb, d), _const_spec((8, d))),
        compiler_params=_params("arbitrary"),
    )(dy, h, g, dres)


def _pool_mixed(ext, hn, t0, tb):
    t = t0 + lax.broadcasted_iota(jnp.int32, (tb, 1), 0)
    parts = []
    for gi, w in enumerate(POOL_WINDOWS):
        lanes = slice(gi * POOL_GROUP, (gi + 1) * POOL_GROUP)
        s = ext[:, lanes]
        k = 1
        while k < w:
            s = s + pltpu.roll(s, k, 0)
            k *= 2
        cnt = jnp.minimum(t + 1, w).astype(F32)
        parts.append(s[POOL_HALO:, :] / cnt - hn[:, lanes])
    return parts


def _pool_fwd(x, g, pw, pb, ps, *, tb=512):
    t_dim, d = x.shape

    def body(x_ref, g_ref, pw_ref, pb_ref, ps_ref, o_ref, ext_ref):
        i = pl.program_id(0)

        @pl.when(i == 0)
        def _():
            ext_ref[0:POOL_HALO, :] = jnp.zeros((POOL_HALO, d), F32)

        xv = x_ref[...]
        hn = xv * _rms(xv) * g_ref[...]
        ext_ref[POOL_HALO:, :] = hn
        mixed = _pool_mixed(ext_ref[...], hn, i * tb, tb)
        for gi in range(len(POOL_WINDOWS)):
            lanes = slice(gi * POOL_GROUP, (gi + 1) * POOL_GROUP)
            out = jnp.dot(mixed[gi].astype(BF16), pw_ref[gi], preferred_element_type=F32)
            o_ref[:, lanes] = xv[:, lanes] + (out + pb_ref[:, lanes]) * ps_ref[:, lanes]
        ext_ref[0:POOL_HALO, :] = hn[tb - POOL_HALO:, :]

    return _pcall(
        body, name="pool_fwd", out_shape=jax.ShapeDtypeStruct((t_dim, d), F32),
        grid=(t_dim // tb,),
        in_specs=[_row_spec(tb, d), _const_spec((1, d)), _const_spec((4, POOL_GROUP, POOL_GROUP)),
                  _const_spec((1, d)), _const_spec((1, d))],
        out_specs=_row_spec(tb, d),
        scratch_shapes=[pltpu.VMEM((POOL_HALO + tb, d), F32)],
        compiler_params=_params("arbitrary"),
    )(x, g, pw, pb, ps)


def _pool_bwd(x, g, pw, pb, ps, dh1, *, tb=512):
    t_dim, d = x.shape
    nb = t_dim // tb
    halo_per_block = tb // POOL_HALO

    def body(x_ref, xprev_ref, g_ref, pw_ref, pb_ref, ps_ref, dh1_ref,
             dx_ref, dpw_ref, small_ref, ext_ref, dext_ref):
        i = pl.program_id(0)
        blk = nb - 1 - i

        @pl.when(i == 0)
        def _():
            dpw_ref[...] = jnp.zeros_like(dpw_ref)
            small_ref[...] = jnp.zeros_like(small_ref)
            dext_ref[tb:, :] = jnp.zeros((POOL_HALO, d), F32)

        gv = g_ref[...]
        xv = x_ref[...]
        r = _rms(xv)
        xhat = xv * r
        hn = xhat * gv
        xp = xprev_ref[...]
        hprev = xp * _rms(xp) * gv * (blk > 0).astype(F32)
        ext_ref[0:POOL_HALO, :] = hprev
        ext_ref[POOL_HALO:, :] = hn
        mixed = _pool_mixed(ext_ref[...], hn, blk * tb, tb)

        dout = dh1_ref[...]
        t = blk * tb + lax.broadcasted_iota(jnp.int32, (tb, 1), 0)
        for gi, w in enumerate(POOL_WINDOWS):
            lanes = slice(gi * POOL_GROUP, (gi + 1) * POOL_GROUP)
            mb = mixed[gi].astype(BF16)
            pre = jnp.dot(mb, pw_ref[gi], preferred_element_type=F32) + pb_ref[:, lanes]
            dg_out = dout[:, lanes]
            small_ref[2:3, lanes] += jnp.sum(dg_out * pre, axis=0, keepdims=True)
            dpre = dg_out * ps_ref[:, lanes]
            small_ref[1:2, lanes] += jnp.sum(dpre, axis=0, keepdims=True)
            dpb16 = dpre.astype(BF16)
            dpw_ref[gi] += lax.dot_general(mb, dpb16, _DIMS["tn"], preferred_element_type=F32)
            dmixed = lax.dot_general(dpb16, pw_ref[gi], _DIMS["nt"], preferred_element_type=F32)
            cnt = jnp.minimum(t + 1, w).astype(F32)
            dq = dmixed / cnt
            dext_ref[0:tb, lanes] = dq
            s = dext_ref[:, lanes]
            k = 1
            while k < w:
                s = s + pltpu.roll(s, tb + POOL_HALO - k, 0)
                k *= 2
            dhn = s[0:tb, :] - dmixed
            dext_ref[tb:, lanes] = dq[0:POOL_HALO, :]
            small_ref[0:1, lanes] += jnp.sum(dhn * xhat[:, lanes], axis=0, keepdims=True)
            ext_ref[POOL_HALO:, lanes] = dhn * gv[:, lanes]
        dxhat = ext_ref[POOL_HALO:, :]
        dx_ref[...] = dout + r * (dxhat - xhat * jnp.mean(dxhat * xhat, axis=-1, keepdims=True))

    return _pcall(
        body, name="pool_bwd",
        out_shape=(jax.ShapeDtypeStruct((t_dim, d), F32),
                   jax.ShapeDtypeStruct((4, POOL_GROUP, POOL_GROUP), F32),
                   jax.ShapeDtypeStruct((8, d), F32)),
        grid=(nb,),
        in_specs=[_row_spec(tb, d, nb, True),
                  pl.BlockSpec((POOL_HALO, d),
                               lambda i: (jnp.maximum((nb - 1 - i) * halo_per_block - 1, 0), 0)),
                  _const_spec((1, d)), _const_spec((4, POOL_GROUP, POOL_GROUP)),
                  _const_spec((1, d)), _const_spec((1, d)), _row_spec(tb, d, nb, True)],
        out_specs=(_row_spec(tb, d, nb, True), _const_spec((4, POOL_GROUP, POOL_GROUP)),
                   _const_spec((8, d))),
        scratch_shapes=[pltpu.VMEM((POOL_HALO + tb, d), F32), pltpu.VMEM((tb + POOL_HALO, d), F32)],
        compiler_params=_params("arbitrary"),
    )(x, x, g, pw, pb, ps, dh1)


_CONV_CB = 1024
_STRIP = 16


def _strips(tb, fn, unroll=4):
    def step(i, carry):
        fn(pl.multiple_of(i * _STRIP, _STRIP))
        return carry
    lax.fori_loop(0, tb // _STRIP, step, 0, unroll=unroll)


def _conv_taps(ext_ref, r0, w):
    shifted = [ext_ref[CONV_HALO + r0 - sh:CONV_HALO + r0 - sh + _STRIP, :] for sh in range(CONV_K)]
    acc = shifted[0] * w[CONV_K - 1:CONV_K, :]
    for sh in range(1, CONV_K):
        acc = acc + shifted[sh] * w[CONV_K - 1 - sh:CONV_K - sh, :]
    return shifted, acc


def _conv_fwd(u, w, b, *, tb=512):
    t_dim, c = u.shape
    cb = _CONV_CB

    def body(u_ref, w_ref, b_ref, o_ref, ext_ref):
        @pl.when(pl.program_id(1) == 0)
        def _():
            ext_ref[0:CONV_HALO, :] = jnp.zeros((CONV_HALO, cb), F32)

        wv = w_ref[...]
        bv = b_ref[...]

        def fill(r0):
            ext_ref[pl.ds(CONV_HALO + r0, _STRIP), :] = u_ref[pl.ds(r0, _STRIP), :].astype(F32)

        _strips(tb, fill)
        for r0 in range(0, tb, _STRIP):
            v = _conv_taps(ext_ref, r0, wv)[1] + bv
            o_ref[r0:r0 + _STRIP, :] = (v * _sigmoid(v)).astype(BF16)
        ext_ref[0:CONV_HALO, :] = ext_ref[tb:tb + CONV_HALO, :]

    blk = pl.BlockSpec((tb, cb), lambda j, t: (t, j))
    return _pcall(
        body, name="conv_fwd", out_shape=jax.ShapeDtypeStruct((t_dim, c), BF16),
        grid=(c // cb, t_dim // tb),
        in_specs=[blk, pl.BlockSpec((CONV_K, cb), lambda j, t: (0, j)),
                  pl.BlockSpec((1, cb), lambda j, t: (0, j))],
        out_specs=blk,
        scratch_shapes=[pltpu.VMEM((CONV_HALO + tb, cb), F32)],
        compiler_params=_params("parallel", "arbitrary"),
    )(u, w, b)


def _conv_bwd_act(u, dxc, w, b, *, tb=512):
    t_dim, c = u.shape
    cb = _CONV_CB
    half = _STRIP // 2

    def body(u_ref, d_ref, w_ref, b_ref, dv_ref, dwb_ref, ext_ref, acc_ref):
        @pl.when(pl.program_id(1) == 0)
        def _():
            ext_ref[0:CONV_HALO, :] = jnp.zeros((CONV_HALO, cb), F32)
            dwb_ref[...] = jnp.zeros_like(dwb_ref)

        acc_ref[...] = jnp.zeros_like(acc_ref)
        wv = w_ref[...]
        bv = b_ref[...]

        def fill(r0):
            ext_ref[pl.ds(CONV_HALO + r0, _STRIP), :] = u_ref[pl.ds(r0, _STRIP), :].astype(F32)

        _strips(tb, fill)
        for r0 in range(0, tb, _STRIP):
            shifted, v = _conv_taps(ext_ref, r0, wv)
            v = v + bv
            sg = _sigmoid(v)
            dv = d_ref[r0:r0 + _STRIP, :].astype(F32) * (sg * (1.0 + v * (1.0 - sg)))
            dv_ref[r0:r0 + _STRIP, :] = dv.astype(BF16)
            acc_ref[CONV_K] += dv[0:half, :] + dv[half:, :]
            for sh in range(CONV_K):
                p = dv * shifted[sh]
                acc_ref[CONV_K - 1 - sh] += p[0:half, :] + p[half:, :]
        for k in range(CONV_K + 1):
            dwb_ref[k:k + 1, :] += jnp.sum(acc_ref[k], axis=0, keepdims=True)
        ext_ref[0:CONV_HALO, :] = ext_ref[tb:tb + CONV_HALO, :]

    blk = pl.BlockSpec((tb, cb), lambda j, t: (t, j))
    return _pcall(
        body, name="conv_bwd_act",
        out_shape=(jax.ShapeDtypeStruct((t_dim, c), BF16), jax.ShapeDtypeStruct((8, c), F32)),
        grid=(c // cb, t_dim // tb),
        in_specs=[blk, blk, pl.BlockSpec((CONV_K, cb), lambda j, t: (0, j)),
                  pl.BlockSpec((1, cb), lambda j, t: (0, j))],
        out_specs=(blk, pl.BlockSpec((8, cb), lambda j, t: (0, j))),
        scratch_shapes=[pltpu.VMEM((CONV_HALO + tb, cb), F32), pltpu.VMEM((CONV_K + 1, half, cb), F32)],
        compiler_params=_params("parallel", "arbitrary"),
    )(u, dxc, w, b)


def _conv_bwd_in(dv, w, *, tb=512):
    t_dim, c = dv.shape
    cb = _CONV_CB
    nb = t_dim // tb

    def body(dv_ref, w_ref, du_ref, ext_ref):
        @pl.when(pl.program_id(1) == 0)
        def _():
            ext_ref[tb:, :] = jnp.zeros((CONV_HALO, cb), F32)

        wv = w_ref[...]

        def fill(r0):
            ext_ref[pl.ds(r0, _STRIP), :] = dv_ref[pl.ds(r0, _STRIP), :].astype(F32)

        _strips(tb, fill)
        for r0 in range(0, tb, _STRIP):
            acc = ext_ref[r0:r0 + _STRIP, :] * wv[CONV_K - 1:CONV_K, :]
            for sh in range(1, CONV_K):
                acc = acc + ext_ref[r0 + sh:r0 + sh + _STRIP, :] * wv[CONV_K - 1 - sh:CONV_K - sh, :]
            du_ref[r0:r0 + _STRIP, :] = acc.astype(BF16)
        ext_ref[tb:, :] = ext_ref[0:CONV_HALO, :]

    blk = pl.BlockSpec((tb, cb), lambda j, t: (nb - 1 - t, j))
    return _pcall(
        body, name="conv_bwd_in", out_shape=jax.ShapeDtypeStruct((t_dim, c), BF16),
        grid=(c // cb, nb),
        in_specs=[blk, pl.BlockSpec((CONV_K, cb), lambda j, t: (0, j))],
        out_specs=blk,
        scratch_shapes=[pltpu.VMEM((tb + CONV_HALO, cb), F32)],
        compiler_params=_params("parallel", "arbitrary"),
    )(dv, w)


def _softplus(v):
    e = jnp.exp(-jnp.abs(v))
    w = 1.0 + e
    log1p = jnp.where(w == 1.0, e, jnp.log(w) * e / jnp.where(w == 1.0, 1.0, w - 1.0))
    return jnp.maximum(v, 0.0) + log1p


def _cumsum_rows(v):
    row = lax.broadcasted_iota(jnp.int32, v.shape, 0)
    k = 1
    while k < CHUNK:
        v = v + jnp.where(row >= k, pltpu.roll(v, k, 0), 0.0)
        k *= 2
    return v


def _cumsum_lanes(v):
    col = lax.broadcasted_iota(jnp.int32, v.shape, 1)
    k = 1
    while k < CHUNK:
        v = v + jnp.where(col >= k, pltpu.roll(v, k, 1), 0.0)
        k *= 2
    return v


def _rev_cumsum_rows(v):
    row = lax.broadcasted_iota(jnp.int32, v.shape, 0)
    k = 1
    while k < CHUNK:
        v = v + jnp.where(row < CHUNK - k, pltpu.roll(v, CHUNK - k, 0), 0.0)
        k *= 2
    return v


PAIR = 2 * HEAD_DIM
GROUP_LANES = HEADS_PER_GROUP * HEAD_DIM


def _head_lane_matrix():
    h = lax.broadcasted_iota(jnp.int32, (128, D_INNER), 0)
    j = lax.broadcasted_iota(jnp.int32, (128, D_INNER), 1)
    return (j // HEAD_DIM == h).astype(BF16)


def _split_bf16(v, pieces):
    out = []
    for _ in range(pieces):
        p = v.astype(BF16)
        out.append(p)
        v = v - p.astype(F32)
    return out


def _expand_heads(v, e):
    return sum(jnp.dot(p, e, preferred_element_type=F32) for p in _split_bf16(v, 3))


def _reduce_heads(v, et, pieces):
    return sum(jnp.dot(p, et, preferred_element_type=F32) for p in _split_bf16(v, pieces))


def _ssd_common(dtr_ref, dtt_ref, e_ref, bx, bc, ax, ac):
    dtx = _expand_heads(dtr_ref[...], e_ref[...])
    dt = _softplus(dtx + bx)
    a_x = -jnp.exp(ax)
    acs = _cumsum_rows(dt * a_x)
    acs_c = _cumsum_lanes(_softplus(dtt_ref[...] + bc) * (-jnp.exp(ac)))
    return dtx, dt, a_x, acs, acs_c


def _pair_decay(acs_slab, acs_c, h0, causal, left):
    other = pltpu.roll(acs_slab, HEAD_DIM, 1)
    col0 = jnp.where(left, acs_slab, other)
    col1 = jnp.where(left, other, acs_slab)
    l0 = jnp.exp(jnp.where(causal, col0 - acs_c[h0:h0 + 1, :], NEG_INF))
    l1 = jnp.exp(jnp.where(causal, col1 - acs_c[h0 + 1:h0 + 2, :], NEG_INF))
    return l0, l1


def _ssd_fwd(xc, dt_raw, dt_raw_t, bias_x, bias_c, alog_x, alog_c, dskip_x, e_mat):
    t_dim = xc.shape[0]
    nc = t_dim // CHUNK

    def body(xc_ref, dtr_ref, dtt_ref, bx_ref, bc_ref, ax_ref, ac_ref, dk_ref, e_ref,
             y_ref, st_ref, state):
        @pl.when(pl.program_id(0) == 0)
        def _():
            state[...] = jnp.zeros_like(state)

        _, dt, _, acs, acs_c = _ssd_common(dtr_ref, dtt_ref, e_ref, bx_ref[...], bc_ref[...],
                                           ax_ref[...], ac_ref[...])
        st_ref[0] = state[...]
        last = acs[CHUNK - 1:CHUNK, :]
        xs32 = xc_ref[:, 0:D_INNER].astype(F32)
        xdt = xs32 * dt
        xdt16 = xdt.astype(BF16)
        xdte16 = (xdt * jnp.exp(last - acs)).astype(BF16)
        ea = jnp.exp(acs)
        cd = jnp.exp(last)
        skip = dk_ref[...] * xs32
        causal = (lax.broadcasted_iota(jnp.int32, (CHUNK, CHUNK), 0)
                  >= lax.broadcasted_iota(jnp.int32, (CHUNK, CHUNK), 1))
        left = lax.broadcasted_iota(jnp.int32, (CHUNK, PAIR), 1) < HEAD_DIM
        for g in range(N_GROUPS):
            gl = slice(g * GROUP_LANES, (g + 1) * GROUP_LANES)
            bg = xc_ref[:, D_INNER + g * D_STATE:D_INNER + (g + 1) * D_STATE]
            cg = xc_ref[:, D_INNER + (N_GROUPS + g) * D_STATE:D_INNER + (N_GROUPS + g + 1) * D_STATE]
            cb = lax.dot_general(cg, bg, _DIMS["nt"], preferred_element_type=F32)
            hprev = state[:, gl]
            ch = jnp.dot(cg, hprev.astype(BF16), preferred_element_type=F32)
            for j in range(HEADS_PER_GROUP // 2):
                pl_ = slice(g * GROUP_LANES + j * PAIR, g * GROUP_LANES + (j + 1) * PAIR)
                h0 = g * HEADS_PER_GROUP + 2 * j
                l0, l1 = _pair_decay(acs[:, pl_], acs_c, h0, causal, left)
                lhs = jnp.concatenate([(cb * l0).astype(BF16), (cb * l1).astype(BF16)], axis=1)
                xp = xdt16[:, pl_]
                zero = jnp.zeros_like(xp)
                rhs = jnp.concatenate([jnp.where(left, xp, zero), jnp.where(left, zero, xp)], axis=0)
                ydiag = jnp.dot(lhs, rhs, preferred_element_type=F32)
                y_ref[:, pl_] = ydiag + ch[:, j * PAIR:(j + 1) * PAIR] * ea[:, pl_] + skip[:, pl_]
            s_new = lax.dot_general(bg, xdte16[:, gl], _DIMS["tn"], preferred_element_type=F32)
            state[:, gl] = hprev * cd[:, gl] + s_new

    rows = lambda w: pl.BlockSpec((CHUNK, w), lambda c: (c, 0))
    return _pcall(
        body, name="ssd_fwd",
        out_shape=(jax.ShapeDtypeStruct((t_dim, D_INNER), F32),
                   jax.ShapeDtypeStruct((nc, D_STATE, D_INNER), F32)),
        grid=(nc,),
        in_specs=[rows(CONV_DIM), rows(128), pl.BlockSpec((N_HEADS, CHUNK), lambda c: (0, c)),
                  _const_spec((1, D_INNER)), _const_spec((N_HEADS, 1)),
                  _const_spec((1, D_INNER)), _const_spec((N_HEADS, 1)), _const_spec((1, D_INNER)),
                  _const_spec((128, D_INNER))],
        out_specs=(rows(D_INNER), pl.BlockSpec((1, D_STATE, D_INNER), lambda c: (c, 0, 0))),
        scratch_shapes=[pltpu.VMEM((D_STATE, D_INNER), F32)],
        compiler_params=_params("arbitrary"),
    )(xc, dt_raw, dt_raw_t, bias_x, bias_c, alog_x, alog_c, dskip_x, e_mat)


def _ssd_bwd(xc, dt_raw, dt_raw_t, bias_x, bias_c, alog_x, alog_c, dskip_x, bias_r, alog_r,
             e_mat, et_mat, states, dy):
    t_dim = xc.shape[0]
    nc = t_dim // CHUNK

    def body(xc_ref, dtr_ref, dtt_ref, bx_ref, bc_ref, ax_ref, ac_ref, dk_ref, br_ref, ar_ref,
             e_ref, et_ref, st_ref, dy_ref,
             dxc_ref, ddt_ref, small_ref, dstate, dacs_ref, dxdt_ref, acc_x, acc_r):
        step = pl.program_id(0)

        @pl.when(step == 0)
        def _():
            dstate[...] = jnp.zeros_like(dstate)
            acc_x[...] = jnp.zeros_like(acc_x)
            acc_r[...] = jnp.zeros_like(acc_r)

        dtx, dt, a_x, acs, acs_c = _ssd_common(dtr_ref, dtt_ref, e_ref, bx_ref[...], bc_ref[...],
                                               ax_ref[...], ac_ref[...])
        last = acs[CHUNK - 1:CHUNK, :]
        xs32 = xc_ref[:, 0:D_INNER].astype(F32)
        xdt = xs32 * dt
        xdt16 = xdt.astype(BF16)
        dte = jnp.exp(last - acs)
        xdte = xdt * dte
        xdte16 = xdte.astype(BF16)
        cd = jnp.exp(last)
        dyv = dy_ref[...]
        dy16 = dyv.astype(BF16)
        dye = dyv * jnp.exp(acs)
        dye16 = dye.astype(BF16)
        causal = (lax.broadcasted_iota(jnp.int32, (CHUNK, CHUNK), 0)
                  >= lax.broadcasted_iota(jnp.int32, (CHUNK, CHUNK), 1))
        left = lax.broadcasted_iota(jnp.int32, (CHUNK, PAIR), 1) < HEAD_DIM
        lane_id = lax.broadcasted_iota(jnp.int32, (CHUNK, 128), 1)
        row_id = lax.broadcasted_iota(jnp.int32, (CHUNK, 128), 0)
        is_last_row = lax.broadcasted_iota(jnp.int32, (CHUNK, 1), 0) == CHUNK - 1
        dacs_cols = jnp.zeros((CHUNK, 128), F32)
        dacs_rows = jnp.zeros((CHUNK, 128), F32)
        for g in range(N_GROUPS):
            gl = slice(g * GROUP_LANES, (g + 1) * GROUP_LANES)
            b_lanes = slice(D_INNER + g * D_STATE, D_INNER + (g + 1) * D_STATE)
            c_lanes = slice(D_INNER + (N_GROUPS + g) * D_STATE, D_INNER + (N_GROUPS + g + 1) * D_STATE)
            bg = xc_ref[:, b_lanes]
            cg = xc_ref[:, c_lanes]
            cb = lax.dot_general(cg, bg, _DIMS["nt"], preferred_element_type=F32)
            hprev = st_ref[0, :, gl]
            hp16 = hprev.astype(BF16)
            dhn = dstate[:, gl]
            dhn16 = dhn.astype(BF16)
            ch = jnp.dot(cg, hp16, preferred_element_type=F32)
            gmat = jnp.dot(bg, dhn16, preferred_element_type=F32)
            gx = gmat * xdte[:, gl]
            dlast = jnp.sum(gx, axis=0, keepdims=True) + cd[:, gl] * jnp.sum(dhn * hprev, axis=0, keepdims=True)
            dacs_ref[:, gl] = dye[:, gl] * ch - gx + jnp.where(is_last_row, dlast, 0.0)
            dc_acc = lax.dot_general(dye16[:, gl], hp16, _DIMS["nt"], preferred_element_type=F32)
            db_acc = lax.dot_general(xdte16[:, gl], dhn16, _DIMS["nt"], preferred_element_type=F32)
            dstate[:, gl] = dhn * cd[:, gl] + lax.dot_general(cg, dye16[:, gl], _DIMS["tn"],
                                                             preferred_element_type=F32)
            dcb = jnp.zeros((CHUNK, CHUNK), F32)
            for j in range(HEADS_PER_GROUP // 2):
                pl_ = slice(g * GROUP_LANES + j * PAIR, g * GROUP_LANES + (j + 1) * PAIR)
                h0 = g * HEADS_PER_GROUP + 2 * j
                l0, l1 = _pair_decay(acs[:, pl_], acs_c, h0, causal, left)
                m0, m1 = cb * l0, cb * l1
                lhs = jnp.concatenate([m0.astype(BF16), m1.astype(BF16)], axis=1)
                dyp = dy16[:, pl_]
                zero = jnp.zeros_like(dyp)
                both = lax.dot_general(lhs, dyp, _DIMS["tn"], preferred_element_type=F32)
                dxdt_ref[:, pl_] = (jnp.where(left, both[0:CHUNK, :], both[CHUNK:, :])
                                    + gmat[:, j * PAIR:(j + 1) * PAIR] * dte[:, pl_])
                lhs2 = jnp.concatenate([jnp.where(left, dyp, zero), jnp.where(left, zero, dyp)], axis=0)
                dm = lax.dot_general(lhs2, xdt16[:, pl_], _DIMS["nt"], preferred_element_type=F32)
                dm0, dm1 = dm[0:CHUNK, :], dm[CHUNK:, :]
                dcb = dcb + dm0 * l0 + dm1 * l1
                ds0, ds1 = dm0 * m0, dm1 * m1
                dacs_cols = jnp.where(lane_id == h0, jnp.sum(ds0, axis=1, keepdims=True), dacs_cols)
                dacs_cols = jnp.where(lane_id == h0 + 1, jnp.sum(ds1, axis=1, keepdims=True), dacs_cols)
                dacs_rows = jnp.where(row_id == h0, jnp.sum(ds0, axis=0, keepdims=True), dacs_rows)
                dacs_rows = jnp.where(row_id == h0 + 1, jnp.sum(ds1, axis=0, keepdims=True), dacs_rows)
            dcb16 = dcb.astype(BF16)
            dxc_ref[:, c_lanes] = (dc_acc + jnp.dot(dcb16, bg, preferred_element_type=F32)).astype(BF16)
            dxc_ref[:, b_lanes] = (db_acc + lax.dot_general(dcb16, cg, _DIMS["tn"],
                                                           preferred_element_type=F32)).astype(BF16)
        dxdt = dxdt_ref[...]
        dxc_ref[:, 0:D_INNER] = (dxdt * dt + dk_ref[...] * dyv).astype(BF16)
        dadt = _rev_cumsum_rows(dacs_ref[...])
        ddraw_x = (dxdt * xs32 + dadt * a_x) * _sigmoid(dtx + bx_ref[...])
        acc_x[0:1, :] += jnp.sum(dadt * dt, axis=0, keepdims=True) * a_x
        acc_x[1:2, :] += jnp.sum(ddraw_x, axis=0, keepdims=True)
        acc_x[2:3, :] += jnp.sum(dyv * xs32, axis=0, keepdims=True)
        a_r = -jnp.exp(ar_ref[...])
        pre_r = dtr_ref[...] + br_ref[...]
        dadt_r = _rev_cumsum_rows(dacs_cols - dacs_rows.T)
        ddraw_r = jnp.where(lane_id < N_HEADS, dadt_r * a_r * _sigmoid(pre_r), 0.0)
        acc_r[0:1, :] += jnp.where(lane_id[0:1, :] < N_HEADS,
                                   jnp.sum(dadt_r * _softplus(pre_r), axis=0, keepdims=True) * a_r, 0.0)
        acc_r[1:2, :] += jnp.sum(ddraw_r, axis=0, keepdims=True)
        ddt_ref[...] = ddraw_r + _reduce_heads(ddraw_x, et_ref[...], 2)

        @pl.when(step == nc - 1)
        def _():
            small_ref[...] = acc_r[...] + _reduce_heads(acc_x[...], et_ref[...], 3)

    rev = lambda w: pl.BlockSpec((CHUNK, w), lambda c: (nc - 1 - c, 0))
    return _pcall(
        body, name="ssd_bwd",
        out_shape=(jax.ShapeDtypeStruct((t_dim, CONV_DIM), BF16),
                   jax.ShapeDtypeStruct((t_dim, 128), F32),
                   jax.ShapeDtypeStruct((8, 128), F32)),
        grid=(nc,),
        in_specs=[rev(CONV_DIM), rev(128), pl.BlockSpec((N_HEADS, CHUNK), lambda c: (0, nc - 1 - c)),
                  _const_spec((1, D_INNER)), _const_spec((N_HEADS, 1)),
                  _const_spec((1, D_INNER)), _const_spec((N_HEADS, 1)), _const_spec((1, D_INNER)),
                  _const_spec((1, 128)), _const_spec((1, 128)),
                  _const_spec((128, D_INNER)), _const_spec((D_INNER, 128)),
                  pl.BlockSpec((1, D_STATE, D_INNER), lambda c: (nc - 1 - c, 0, 0)),
                  rev(D_INNER)],
        out_specs=(rev(CONV_DIM), rev(128), _const_spec((8, 128))),
        scratch_shapes=[pltpu.VMEM((D_STATE, D_INNER), F32), pltpu.VMEM((CHUNK, D_INNER), F32),
                        pltpu.VMEM((CHUNK, D_INNER), F32), pltpu.VMEM((8, D_INNER), F32),
                        pltpu.VMEM((8, 128), F32)],
        compiler_params=_params("arbitrary"),
    )(xc, dt_raw, dt_raw_t, bias_x, bias_c, alog_x, alog_c, dskip_x, bias_r, alog_r,
      e_mat, et_mat, states, dy)


_GATE_GROUP = D_INNER // N_GROUPS


def _gate_fwd(y, z, g, *, tb=256):
    t_dim = y.shape[0]

    def body(y_ref, z_ref, g_ref, o_ref):
        for gi in range(N_GROUPS):
            lanes = slice(gi * _GATE_GROUP, (gi + 1) * _GATE_GROUP)
            zv = z_ref[:, lanes].astype(F32)
            wv = y_ref[:, lanes] * (zv * _sigmoid(zv))
            o_ref[:, lanes] = (wv * _rms(wv) * g_ref[:, lanes]).astype(BF16)

    return _pcall(
        body, name="gate_fwd", out_shape=jax.ShapeDtypeStruct((t_dim, D_INNER), BF16),
        grid=(t_dim // tb,),
        in_specs=[_row_spec(tb, D_INNER), _row_spec(tb, D_INNER), _const_spec((1, D_INNER))],
        out_specs=_row_spec(tb, D_INNER), compiler_params=_params("parallel"),
    )(y, z, g)


def _gate_bwd(dyn, y, z, g, *, tb=256):
    t_dim = y.shape[0]

    def body(d_ref, y_ref, z_ref, g_ref, dy_ref, dz_ref, dg_ref):
        @pl.when(pl.program_id(0) == 0)
        def _():
            dg_ref[...] = jnp.zeros_like(dg_ref)

        for gi in range(N_GROUPS):
            lanes = slice(gi * _GATE_GROUP, (gi + 1) * _GATE_GROUP)
            zv = z_ref[:, lanes].astype(F32)
            sg = _sigmoid(zv)
            sz = zv * sg
            yv = y_ref[:, lanes]
            wv = yv * sz
            r = _rms(wv)
            what = wv * r
            dv = d_ref[:, lanes]
            dwhat = dv * g_ref[:, lanes]
            dw = r * (dwhat - what * jnp.mean(dwhat * what, axis=-1, keepdims=True))
            dg_ref[0:1, lanes] += jnp.sum(dv * what, axis=0, keepdims=True)
            dy_ref[:, lanes] = dw * sz
            dz_ref[:, lanes] = (dw * yv * (sg * (1.0 + zv * (1.0 - sg)))).astype(BF16)

    return _pcall(
        body, name="gate_bwd",
        out_shape=(jax.ShapeDtypeStruct((t_dim, D_INNER), F32),
                   jax.ShapeDtypeStruct((t_dim, D_INNER), BF16),
                   jax.ShapeDtypeStruct((8, D_INNER), F32)),
        grid=(t_dim // tb,),
        in_specs=[_row_spec(tb, D_INNER), _row_spec(tb, D_INNER), _row_spec(tb, D_INNER),
                  _const_spec((1, D_INNER))],
        out_specs=(_row_spec(tb, D_INNER), _row_spec(tb, D_INNER), _const_spec((8, D_INNER))),
        compiler_params=_params("arbitrary"),
    )(dyn, y, z, g)


def _loss_head(h, g, target, *, tb=512):
    t_dim, d = h.shape

    def body(h_ref, g_ref, t_ref, dh_ref, dh16_ref, small_ref):
        @pl.when(pl.program_id(0) == 0)
        def _():
            small_ref[...] = jnp.zeros_like(small_ref)

        x = h_ref[...]
        r = _rms(x)
        xhat = x * r
        gv = g_ref[...]
        err = xhat * gv - t_ref[...]
        small_ref[1:2, :] += (0.5 / d) * jnp.sum(err * err, axis=0, keepdims=True)
        dyv = err * (1.0 / d)
        dxhat = dyv * gv
        dh = r * (dxhat - xhat * jnp.mean(dxhat * xhat, axis=-1, keepdims=True))
        dh_ref[...] = dh
        dh16_ref[...] = dh.astype(BF16)
        small_ref[0:1, :] += jnp.sum(dyv * xhat, axis=0, keepdims=True)

    return _pcall(
        body, name="loss_head",
        out_shape=(jax.ShapeDtypeStruct((t_dim, d), F32), jax.ShapeDtypeStruct((t_dim, d), BF16),
                   jax.ShapeDtypeStruct((8, d), F32)),
        grid=(t_dim // tb,),
        in_specs=[_row_spec(tb, d), _const_spec((1, d)), _row_spec(tb, d)],
        out_specs=(_row_spec(tb, d), _row_spec(tb, d), _const_spec((8, d))),
        compiler_params=_params("arbitrary"),
    )(h, g, target)


def _adamw(w, g, m, v, *, name, part=None, into=None):
    r_dim, c = w.shape
    rows = r_dim if part is None else r_dim // 2
    assert g.shape == (rows, c)
    tb = max(t for t in range(8, 513, 8) if rows % t == 0)
    first = 0 if part is None else part * (rows // tb)
    c1 = 1.0 / (1.0 - ADAM_B1 ** ADAM_STEP)
    c2 = 1.0 / (1.0 - ADAM_B2 ** ADAM_STEP)

    def body(w_ref, g_ref, m_ref, v_ref, *rest):
        d_ref, mo_ref, vo_ref = rest[-3:]
        gv = g_ref[...]
        mn = ADAM_B1 * m_ref[...] + (1.0 - ADAM_B1) * gv
        vn = ADAM_B2 * v_ref[...] + (1.0 - ADAM_B2) * (gv * gv)
        mo_ref[...] = mn
        vo_ref[...] = vn
        d_ref[...] = -ADAM_LR * ((mn * c1) / (jnp.sqrt(vn * c2) + ADAM_EPS) + ADAM_WD * w_ref[...])

    spec = pl.BlockSpec((tb, c), lambda i: (first + i, 0))
    sds = jax.ShapeDtypeStruct((r_dim, c), F32)
    in_specs = [spec, _row_spec(tb, c), spec, spec]
    operands = [w, g, m, v]
    aliases = {}
    if into is not None:
        in_specs += [_ANY] * 3
        operands += list(into)
        aliases = {4: 0, 5: 1, 6: 2}
    return _pcall(
        body, name=name, out_shape=(sds, sds, sds), grid=(rows // tb,),
        in_specs=in_specs, out_specs=(spec,) * 3, input_output_aliases=aliases,
        compiler_params=_params("parallel"),
    )(*operands)


def _pair_sum(grad, recv, place, *, name):
    s_dim, r_dim, c = grad.shape
    half = r_dim // 2
    tb = 256 if half % 256 == 0 else half
    per_half = half // tb

    def body(place_ref, a_ref, b_ref, o16_ref, o32_ref):
        s = a_ref[...] + b_ref[...]
        o16_ref[...] = s.astype(BF16)

        @pl.when(pl.program_id(1) == place_ref[1])
        def _():
            o32_ref[...] = s[0]

    grid_spec = pltpu.PrefetchScalarGridSpec(
        num_scalar_prefetch=1, grid=(per_half, s_dim),
        in_specs=[pl.BlockSpec((1, tb, c), lambda i, s, p: (s, p[0] * per_half + i, 0)),
                  pl.BlockSpec((1, tb, c), lambda i, s, p: (s, i, 0))],
        out_specs=(pl.BlockSpec((1, tb, c), lambda i, s, p: (s, i, 0)),
                   pl.BlockSpec((tb, c), lambda i, s, p: (i, 0))))
    return _pcall(
        body, name=name, grid_spec=grid_spec,
        out_shape=(jax.ShapeDtypeStruct((s_dim, half, c), BF16), jax.ShapeDtypeStruct((half, c), F32)),
        compiler_params=_params("parallel", "arbitrary"),
    )(place, grad, recv)


def _chip_sum(own, recv, *, name):
    r_dim, c = own.shape
    tb = 256 if r_dim % 256 == 0 else r_dim

    def body(a_ref, b_ref, o_ref):
        s = a_ref[...]
        for k in range(1, N_CHIPS):
            s = s + b_ref[k].astype(F32)
        o_ref[...] = s

    return _pcall(
        body, name=name, out_shape=jax.ShapeDtypeStruct((r_dim, c), F32),
        grid=(r_dim // tb,),
        in_specs=[_row_spec(tb, c), pl.BlockSpec((N_CHIPS, tb, c), lambda i: (0, i, 0))],
        out_specs=_row_spec(tb, c), compiler_params=_params("parallel"),
    )(own, recv)


def _position():
    return lax.axis_index("x"), lax.axis_index("y"), lax.axis_index("c")


def _chip_peer(x, y, k):
    return x ^ (k >> 1), y ^ (k & 1)


_ANY = pl.BlockSpec(memory_space=pl.ANY)


def _all_gather_weights(shards):
    n = len(shards)
    hops = N_CHIPS - 1

    def body(*refs):
        srcs, outs = refs[:n], refs[n:2 * n]
        send_sems, recv_sems = refs[2 * n:]
        x, y, c = _position()
        me = 2 * x + y

        def over_ici(w, k, chip, to):
            return pltpu.make_async_remote_copy(
                src_ref=srcs[w].at[c], dst_ref=outs[w].at[chip, c],
                send_sem=send_sems.at[w, k - 1], recv_sem=recv_sems.at[w, k - 1],
                device_id=to, device_id_type=MESH)

        def over_d2d(w, k, chip, half):
            return pltpu.make_async_remote_copy(
                src_ref=outs[w].at[chip, half], dst_ref=outs[w].at[chip, half],
                send_sem=send_sems.at[w, hops + k - 1], recv_sem=recv_sems.at[w, hops + k - 1],
                device_id=(x, y, 1 - c), device_id_type=MESH)

        sends = []
        for w in range(n):
            for k in range(1, N_CHIPS):
                px, py = _chip_peer(x, y, k)
                cp = over_ici(w, k, me, (px, py, c))
                cp.start()
                sends.append(cp)
        for w in range(n):
            for k in range(1, N_CHIPS):
                px, py = _chip_peer(x, y, k)
                over_ici(w, k, 2 * px + py, (px, py, c)).wait_recv()
                cp = over_d2d(w, k, 2 * px + py, c)
                cp.start()
                sends.append(cp)
        for w in range(n):
            for k in range(1, N_CHIPS):
                px, py = _chip_peer(x, y, k)
                over_d2d(w, k, 2 * px + py, 1 - c).wait_recv()
        for cp in sends:
            cp.wait_send()

    return _pcall(
        body, name="gather_weights",
        out_shape=tuple(jax.ShapeDtypeStruct((N_CHIPS,) + s.shape, s.dtype) for s in shards),
        in_specs=[_ANY] * n, out_specs=(_ANY,) * n,
        scratch_shapes=[pltpu.SemaphoreType.DMA((n, 2 * hops)),
                        pltpu.SemaphoreType.DMA((n, 2 * hops))],
    )(*shards)


def _pair_copies(srcs, lands, send_sems, recv_sems):
    x, y, c = _position()
    copies = []
    for w in range(len(srcs)):
        half = srcs[w].shape[1] // 2
        copies.append(pltpu.make_async_remote_copy(
            src_ref=srcs[w].at[:, pl.ds((1 - c) * half, half), :], dst_ref=lands[w],
            send_sem=send_sems.at[w], recv_sem=recv_sems.at[w],
            device_id=(x, y, 1 - c), device_id_type=MESH))
    return copies


def _chip_copies(srcs, lands, send_sems, recv_sems):
    x, y, c = _position()
    copies = []
    for w in range(len(srcs)):
        for k in range(1, N_CHIPS):
            px, py = _chip_peer(x, y, k)
            i = w * (N_CHIPS - 1) + k - 1
            copies.append(pltpu.make_async_remote_copy(
                src_ref=srcs[w].at[2 * px + py], dst_ref=lands[w].at[k],
                send_sem=send_sems.at[i], recv_sem=recv_sems.at[i],
                device_id=(px, py, c), device_id_type=MESH))
    return copies


def _gather_copies(srcs, lands, send_sems, recv_sems):
    x, y, c = _position()
    me = 2 * x + y
    copies = []
    for w in range(len(srcs)):
        for k in range(1, N_CHIPS):
            px, py = _chip_peer(x, y, k)
            i = w * (N_CHIPS - 1) + k - 1
            copies.append(pltpu.make_async_remote_copy(
                src_ref=srcs[w].at[c], dst_ref=lands[w].at[me, c],
                send_sem=send_sems.at[i], recv_sem=recv_sems.at[i],
                device_id=(px, py, c), device_id_type=MESH))
    return copies


def _exchange(name, copies_of, n_copies, srcs, land_shapes):
    n = len(srcs)

    def body(*refs):
        copies = copies_of(refs[:n], refs[n:2 * n], refs[2 * n], refs[2 * n + 1])
        for cp in copies:
            cp.start()
        for cp in copies:
            cp.wait_recv()
        for cp in copies:
            cp.wait_send()

    return _pcall(
        body, name=name, out_shape=tuple(land_shapes),
        in_specs=[_ANY] * n, out_specs=(_ANY,) * n,
        scratch_shapes=[pltpu.SemaphoreType.DMA((n_copies,)), pltpu.SemaphoreType.DMA((n_copies,))],
    )(*srcs)


_HBM = pl.BlockSpec(memory_space=pltpu.HBM)
_SEM = pl.BlockSpec(memory_space=pltpu.SEMAPHORE)
_DATAFLOW = pltpu.SideEffectType.DATAFLOW_SIDE_EFFECTING


def _exchange_start(name, copies_of, n_copies, srcs, land_shapes):
    n = len(srcs)
    lands = [lax.empty(s.shape, s.dtype) for s in land_shapes]

    def body(*refs):
        for cp in copies_of(refs[:n], refs[n:2 * n], refs[2 * n], refs[2 * n + 1]):
            cp.start()
        refs[-1][...] = jnp.zeros_like(refs[-1])

    through = [pltpu.HBM(a.shape, a.dtype) for a in list(srcs) + lands]
    outs = _pcall(
        body, name=name,
        out_shape=(pltpu.SemaphoreType.DMA((n_copies,)), pltpu.SemaphoreType.DMA((n_copies,)),
                   *through, jax.ShapeDtypeStruct((8, 128), F32)),
        in_specs=[_HBM] * (2 * n),
        out_specs=(_SEM, _SEM, *([_HBM] * (2 * n)), pl.BlockSpec(memory_space=pltpu.VMEM)),
        input_output_aliases={i: 2 + i for i in range(2 * n)},
        compiler_params=pltpu.CompilerParams(has_side_effects=_DATAFLOW),
    )(*[pltpu.with_memory_space_constraint(a, pltpu.HBM) for a in list(srcs) + lands])
    return outs[:-1], outs[-1][0, 0]


def _exchange_wait(name, copies_of, state, after):
    send_sems, recv_sems, through = state[0], state[1], state[2:]
    n = len(through) // 2
    if after.ndim == 0:
        after = jnp.broadcast_to(after, (8, 128))
    after = pltpu.with_memory_space_constraint(after, pltpu.HBM)

    def body(*refs):
        for cp in copies_of(refs[:n], refs[n:2 * n], refs[2 * n], refs[2 * n + 1]):
            cp.wait_send()
            cp.wait_recv()

    outs = _pcall(
        body, name=name,
        out_shape=tuple(pltpu.HBM(a.shape, a.dtype) for a in through),
        in_specs=[_HBM] * (2 * n) + [_SEM, _SEM, _HBM], out_specs=tuple([_HBM] * (2 * n)),
        input_output_aliases={i: i for i in range(2 * n)},
        compiler_params=pltpu.CompilerParams(has_side_effects=_DATAFLOW),
    )(*through, send_sems, recv_sems, after)
    return outs[:n], outs[n:]


def _forward_halves(lands, *, name):
    n = len(lands)
    hops = N_CHIPS - 1

    def body(*refs):
        ins, outs = refs[:n], refs[n:2 * n]
        send_sems, recv_sems = refs[2 * n], refs[2 * n + 1]
        x, y, c = _position()
        copies = []
        for w in range(n):
            for k in range(1, N_CHIPS):
                px, py = _chip_peer(x, y, k)
                i = w * hops + k - 1
                copies.append(pltpu.make_async_remote_copy(
                    src_ref=ins[w].at[2 * px + py, c], dst_ref=outs[w].at[2 * px + py, c],
                    send_sem=send_sems.at[i], recv_sem=recv_sems.at[i],
                    device_id=(x, y, 1 - c), device_id_type=MESH))
        for cp in copies:
            cp.start()
        for cp in copies:
            cp.wait_recv()
        for cp in copies:
            cp.wait_send()

    return _pcall(
        body, name=name,
        out_shape=tuple(jax.ShapeDtypeStruct(a.shape, a.dtype) for a in lands),
        in_specs=[_ANY] * n, out_specs=(_ANY,) * n,
        input_output_aliases={i: i for i in range(n)},
        scratch_shapes=[pltpu.SemaphoreType.DMA((n * hops,)), pltpu.SemaphoreType.DMA((n * hops,))],
    )(*lands)


def _pair_lands(grads):
    return [jax.ShapeDtypeStruct((g.shape[0], g.shape[1] // 2, g.shape[2]), F32) for g in grads]


def _same_lands(parts):
    return [jax.ShapeDtypeStruct(p.shape, p.dtype) for p in parts]


def _pair_gather_halves(halves, *, name):
    n = len(halves)

    def body(*refs):
        srcs, outs = refs[:n], refs[n:2 * n]
        send_sems, recv_sems = refs[2 * n:]
        x, y, c = _position()
        sends = []
        for w in range(n):
            cp = pltpu.make_async_remote_copy(
                src_ref=srcs[w], dst_ref=outs[w],
                send_sem=send_sems.at[w], recv_sem=recv_sems.at[w],
                device_id=(x, y, 1 - c), device_id_type=MESH)
            cp.start()
            sends.append(cp)
        for cp in sends:
            cp.wait_recv()
        for cp in sends:
            cp.wait_send()

    theirs = _pcall(
        body, name=name,
        out_shape=tuple(jax.ShapeDtypeStruct(h.shape, F32) for h in halves),
        in_specs=[_ANY] * n, out_specs=(_ANY,) * n,
        scratch_shapes=[pltpu.SemaphoreType.DMA((n,)), pltpu.SemaphoreType.DMA((n,))],
    )(*halves)
    my_c = lax.axis_index("c")
    whole = []
    for mine, other in zip(halves, theirs):
        both = jnp.stack([other, other])
        both = lax.dynamic_update_index_in_dim(both, mine, my_c, axis=0)
        whole.append(both.reshape(2 * mine.shape[0], mine.shape[1]))
    return whole


def _all_reduce_small(packed, *, name, sum_row0):
    r_dim, c = packed.shape

    def body(src_ref, out_ref, recv_ref, send_sems, recv_sems):
        x, y, c_ = _position()
        me = 4 * x + 2 * y + c_
        recv_ref[0] = src_ref[...]
        sends = []
        for k in range(1, N_DEV):
            peer = (x ^ (k >> 2), y ^ ((k >> 1) & 1), c_ ^ (k & 1))
            cp = pltpu.make_async_remote_copy(
                src_ref=src_ref, dst_ref=recv_ref.at[k],
                send_sem=send_sems.at[k - 1], recv_sem=recv_sems.at[k - 1],
                device_id=peer, device_id_type=MESH)
            cp.start()
            sends.append(cp)
        for cp in sends:
            cp.wait_recv()
        total = recv_ref[me]
        for d in range(1, N_DEV):
            total = total + recv_ref[d ^ me]
        if sum_row0:
            row0 = jnp.sum(total[0:1, :], axis=1, keepdims=True)
            rid = lax.broadcasted_iota(jnp.int32, total.shape, 0)
            total = jnp.where(rid == 0, row0, total)
        out_ref[...] = total
        for cp in sends:
            cp.wait_send()

    return _pcall(
        body, name=name, out_shape=jax.ShapeDtypeStruct((r_dim, c), F32),
        in_specs=[pl.BlockSpec(memory_space=pltpu.VMEM)],
        out_specs=pl.BlockSpec(memory_space=pltpu.VMEM),
        scratch_shapes=[pltpu.VMEM((N_DEV, r_dim, c), F32),
                        pltpu.SemaphoreType.DMA((N_DEV - 1,)), pltpu.SemaphoreType.DMA((N_DEV - 1,))],
    )(packed)


def _pad_lanes(v, width):
    return jnp.pad(v, ((0, 0), (0, width - v.shape[1])))


def _pad_rows(v, rows):
    pad = [(0, 0)] * v.ndim
    pad[-2] = (0, rows - v.shape[-2])
    return jnp.pad(v, pad)


_IN_PROJ_SHARD_ROWS = 1312


def _rows_1024(v):
    flat = v.reshape(-1)
    pad = (-flat.shape[0]) % D_MODEL
    return jnp.pad(flat, (0, pad)).reshape(-1, D_MODEL)


def _local_step(xs, target, pw, w1_0, fetch, reduce_start, reduce_midway,
                conv_w, conv_b, gate_g,
                norm_mix_g, norm_mlp_g, pool_b, pool_scale, ssm_dt_bias, ssm_a_log, ssm_d, final_g):
    bias_r = _pad_lanes(ssm_dt_bias, 128)
    alog_r = _pad_lanes(ssm_a_log, 128)
    bias_x = jnp.repeat(ssm_dt_bias, HEAD_DIM, axis=1)
    alog_x = jnp.repeat(ssm_a_log, HEAD_DIM, axis=1)
    dskip_x = jnp.repeat(ssm_d, HEAD_DIM, axis=1)
    bias_c = ssm_dt_bias.reshape(N_HEADS, 1)
    alog_c = ssm_a_log.reshape(N_HEADS, 1)
    e_mat = _head_lane_matrix()

    g_mix0, g_mix1 = norm_mix_g[0:1], norm_mix_g[1:2]
    g_mlp0, g_mlp1 = norm_mlp_g[0:1], norm_mlp_g[1:2]
    fg = final_g.reshape(1, D_MODEL)

    h1 = _pool_fwd(xs, g_mix0, pw, pool_b, pool_scale)
    hm0 = _rmsnorm_fwd(h1, g_mlp0, name="norm_mlp0")
    u0 = _matmul(hm0, w1_0, "nn", name="mlp0_up", out_dtype=BF16, b_col_shards=True)
    w2_0 = fetch("mlp0_down", u0)
    h2 = _matmul(u0, w2_0, "nn", name="mlp0_down", a_relu2=True, add=h1)

    w_z, w_xbc, w_dt = fetch("in_proj", h2)
    hn1 = _rmsnorm_fwd(h2, g_mix1, name="norm_mix1")
    z = _matmul(hn1, w_z, "nt", name="in_proj_z", out_dtype=BF16)
    xbc = _matmul(hn1, w_xbc, "nt", name="in_proj_xbc", out_dtype=BF16)
    dt_raw = _matmul(hn1, w_dt, "nt", name="in_proj_dt")
    dt_raw_t = dt_raw[:, :N_HEADS].T
    xc = _conv_fwd(xbc, conv_w, conv_b)
    wout, w1_1, w2_1 = fetch("rest", xc)
    y, states = _ssd_fwd(xc, dt_raw, dt_raw_t, bias_x, bias_c, alog_x, alog_c, dskip_x, e_mat)
    yn = _gate_fwd(y, z, gate_g)
    h3 = _matmul(yn, wout, "nn", name="out_proj", add=h2)
    hm1 = _rmsnorm_fwd(h3, g_mlp1, name="norm_mlp1")
    u1 = _matmul(hm1, w1_1, "nn", name="mlp1_up", out_dtype=BF16, b_col_shards=True)
    h4 = _matmul(u1, w2_1, "nn", name="mlp1_down", a_relu2=True, add=h3)

    dh4, dh4_16, small_final = _loss_head(h4, fg, target)

    def mlp_bwd_weights(dh_out16, hm, u, w2_i, tag):
        du = _matmul(dh_out16, w2_i, "nt", name=tag + "_du", out_dtype=BF16, relu2_grad_of=u)
        dw2 = _matmul(u, dh_out16, "tn", name=tag + "_dw2", a_relu2=True)
        dw1 = _matmul(hm, du, "tn", name=tag + "_dw1", out_col_shards=N_CHIPS)
        return du, dw1, dw2.reshape(N_CHIPS, D_FF // N_CHIPS, D_MODEL)

    def mlp_bwd_input(du, dh_out, h_in, w1_i, g_i, tag):
        dhm = _matmul(du, w1_i, "nt", name=tag + "_dhm", b_col_shards=True)
        return _rmsnorm_bwd(dhm, h_in, g_i, dh_out, name=tag + "_norm_bwd")

    du1, dw1_1, dw2_1 = mlp_bwd_weights(dh4_16, hm1, u1, w2_1, "mlp1")
    dh3, dh3_16, dg_mlp1 = mlp_bwd_input(du1, dh4, h3, w1_1, g_mlp1, "mlp1")

    dyn = _matmul(dh3_16, wout, "nt", name="out_proj_dyn")
    dwout = _matmul(yn, dh3_16, "tn", name="out_proj_dw").reshape(N_CHIPS, D_INNER // N_CHIPS, D_MODEL)
    behind = reduce_start("mlp1_out", [dw1_1, dw2_1, dwout])
    dy, dz, dg_gate = _gate_bwd(dyn, y, z, gate_g + behind)
    behind = reduce_midway("mlp1_out", dz)
    dxc, ddt_raw, small_ssd = _ssd_bwd(xc, dt_raw, dt_raw_t, bias_x, bias_c, alog_x, alog_c, dskip_x + behind,
                                       bias_r, alog_r, e_mat, e_mat.T, states, dy)
    dv, dconv = _conv_bwd_act(xbc, dxc, conv_w, conv_b)
    dxbc = _conv_bwd_in(dv, conv_w)
    dhn1 = _matmul(ddt_raw, w_dt, "nn", name="in_proj_dt_dh")
    dhn1 = _matmul(dz, w_z, "nn", name="in_proj_z_dh", add=dhn1)
    dhn1 = _matmul(dxbc, w_xbc, "nn", name="in_proj_xbc_dh", add=dhn1)
    dw_z = _matmul(dz, hn1, "tn", name="in_proj_z_dw")
    dw_xbc = _matmul(dxbc, hn1, "tn", name="in_proj_xbc_dw")
    dw_dt = _matmul(ddt_raw, hn1, "tn", name="in_proj_dt_dw")
    dwin = jnp.concatenate([dw_z, dw_xbc, dw_dt[:N_HEADS]], axis=0)
    dwin = _pad_rows(dwin.reshape(N_CHIPS, IN_PROJ_DIM // N_CHIPS, D_MODEL), _IN_PROJ_SHARD_ROWS)
    behind = reduce_start("in_proj", [dwin])
    dh2, dh2_16, dg_mix1 = _rmsnorm_bwd(dhn1, h2, g_mix1 + behind, dh3, name="norm_mix1_bwd")
    behind = reduce_midway("in_proj", dh2_16)

    du0, dw1_0, dw2_0 = mlp_bwd_weights(dh2_16, hm0, u0, w2_0, "mlp0")
    dh1, _, dg_mlp0 = mlp_bwd_input(du0, dh2, h1, w1_0, g_mlp0 + behind, "mlp0")
    dx, dpw, small_pool = _pool_bwd(xs, g_mix0, pw, pool_b, pool_scale, dh1)
    dpw = jnp.transpose(dpw.reshape(4, N_CHIPS, POOL_GROUP // N_CHIPS, POOL_GROUP), (1, 0, 2, 3))
    dpw = dpw.reshape(N_CHIPS, 4 * (POOL_GROUP // N_CHIPS), POOL_GROUP)

    big = [dpw, dw1_0, dw2_0]
    rows = [
        small_final[1:2],
        small_final[0:1],
        small_pool[0:1], dg_mix1[0:1],
        dg_mlp0[0:1], dg_mlp1[0:1],
        small_pool[1:2], small_pool[2:3],
        _pad_lanes(small_ssd[0:3], D_MODEL),
        _rows_1024(dg_gate[0:1]),
        _rows_1024(dconv[0:CONV_K]),
        _rows_1024(dconv[CONV_K:CONV_K + 1]),
    ]
    return dx, big, rows


def kernel(x, norm_mix_g, norm_mlp_g, pool_w, pool_b, pool_scale, ssm_w_in, ssm_conv_w, ssm_conv_b, ssm_dt_bias, ssm_a_log, ssm_d, ssm_norm_g, ssm_w_out, mlp_w1, mlp_w2, final_g, loss_target, m_norm_mix_g, m_norm_mlp_g, m_pool_w, m_pool_b, m_pool_scale, m_ssm_w_in, m_ssm_conv_w, m_ssm_conv_b, m_ssm_dt_bias, m_ssm_a_log, m_ssm_d, m_ssm_norm_g, m_ssm_w_out, m_mlp_w1, m_mlp_w2, m_final_g, v_norm_mix_g, v_norm_mlp_g, v_pool_w, v_pool_b, v_pool_scale, v_ssm_w_in, v_ssm_conv_w, v_ssm_conv_b, v_ssm_dt_bias, v_ssm_a_log, v_ssm_d, v_ssm_norm_g, v_ssm_w_out, v_mlp_w1, v_mlp_w2, v_final_g):
    t_dim = x.shape[1]
    xs = x[0]
    target = loss_target[0]
    my_x, my_y, my_c = _position()
    my_chip = 2 * my_x + my_y

    def halves(w):
        return w.astype(BF16).reshape((2, w.shape[0] // 2) + w.shape[1:])

    def whole(gathered, own_shard):
        g = lax.dynamic_update_index_in_dim(gathered, own_shard, my_chip, axis=0)
        return g.reshape((N_CHIPS, 2 * g.shape[2]) + g.shape[3:])

    def gather_lands(own):
        return [jax.ShapeDtypeStruct((N_CHIPS,) + s.shape, s.dtype) for s in own]

    vec_cols = CONV_DIM // N_CHIPS
    vec_own = jnp.concatenate([ssm_conv_w[0], ssm_conv_b, _pad_lanes(ssm_norm_g, vec_cols)], axis=0)
    early_own = [halves(pool_w[0]), halves(mlp_w1[0]), vec_own.reshape(2, (CONV_K + 2) // 2, vec_cols)]
    early = _all_gather_weights(early_own)
    g_pool, w1_0, g_vec = [whole(g, o) for g, o in zip(early, early_own)]
    pw = jnp.transpose(g_pool, (1, 0, 2, 3)).reshape(4, POOL_GROUP, POOL_GROUP)
    conv_w = jnp.transpose(g_vec[:, 0:CONV_K, :], (1, 0, 2)).reshape(CONV_K, CONV_DIM)
    conv_b = g_vec[:, CONV_K, :].reshape(1, CONV_DIM)
    gate_g = g_vec[:, CONV_K + 1, :D_INNER // N_CHIPS].reshape(1, D_INNER)

    down_own = [halves(mlp_w2[0])]
    in_own = [halves(_pad_rows(ssm_w_in[0].T, _IN_PROJ_SHARD_ROWS))]
    rest_own = [halves(ssm_w_out[0]), halves(mlp_w1[1]), halves(mlp_w2[1])]
    early, down_own = lax.optimization_barrier((early, down_own))
    fetches = {}
    fetches["mlp0_down"], behind_gather = _exchange_start(
        "gather_mlp0_down_start", _gather_copies, len(down_own) * (N_CHIPS - 1), down_own, gather_lands(down_own))
    behind_gather, in_own = lax.optimization_barrier((behind_gather, in_own))
    fetches["in_proj"], behind_gather = _exchange_start(
        "gather_in_proj_start", _gather_copies, len(in_own) * (N_CHIPS - 1), in_own, gather_lands(in_own))

    def fetch(what, after):
        if what == "mlp0_down":
            own_thru, landed = _exchange_wait("gather_mlp0_down_wait", _gather_copies, fetches[what], after)
            landed = _forward_halves(landed, name="forward_mlp0_down")
            return whole(landed[0], own_thru[0]).reshape(D_FF, D_MODEL)
        if what == "in_proj":
            own_thru, landed = _exchange_wait("gather_in_proj_wait", _gather_copies, fetches["in_proj"], after)
            landed = _forward_halves(landed, name="forward_in_proj")
            landed, rest = lax.optimization_barrier((landed, rest_own))
            fetches["rest"], behind = _exchange_start(
                "gather_rest_start", _gather_copies, len(rest) * (N_CHIPS - 1), rest, gather_lands(rest))
            win = whole(landed[0], own_thru[0])[:, :IN_PROJ_DIM // N_CHIPS].reshape(IN_PROJ_DIM, D_MODEL)
            w_dt = _pad_rows(win[D_INNER + CONV_DIM:], 128) + behind.astype(BF16)
            return win[:D_INNER], win[D_INNER:D_INNER + CONV_DIM], w_dt
        own_thru, landed = _exchange_wait("gather_rest_wait", _gather_copies, fetches["rest"], after)
        landed = _forward_halves(landed, name="forward_rest")
        g_wout, w1_1, g_w2_1 = [whole(g, o) for g, o in zip(landed, own_thru)]
        return g_wout.reshape(D_INNER, D_MODEL), w1_1, g_w2_1.reshape(D_FF, D_MODEL)

    place = jnp.stack([my_c, my_chip]).astype(jnp.int32)
    waves = {}

    def reduce_start(wave, grads):
        waves[wave] = {}
        waves[wave]["pair"], behind = _exchange_start(
            "pair_%s_start" % wave, _pair_copies, len(grads), grads, _pair_lands(grads))
        return behind

    def reduce_midway(wave, after):
        st = waves[wave]
        grads, recv = _exchange_wait("pair_%s_wait" % wave, _pair_copies, st["pair"], after)
        sums = [_pair_sum(g, r, place, name="pair_sum_%s_%d" % (wave, i))
                for i, (g, r) in enumerate(zip(grads, recv))]
        st["f32"] = [s32 for _, s32 in sums]
        b16 = [s16 for s16, _ in sums]
        st["chip"], behind = _exchange_start(
            "chip_%s_start" % wave, _chip_copies, len(b16) * (N_CHIPS - 1), b16, _same_lands(b16))
        return behind

    def reduce_finish(wave, after):
        st = waves[wave]
        _, got = _exchange_wait("chip_%s_wait" % wave, _chip_copies, st["chip"], after)
        return [_chip_sum(s32, r, name="chip_sum_%s_%d" % (wave, i))
                for i, (s32, r) in enumerate(zip(st["f32"], got))]

    dx, big0, rows = _local_step(xs, target, pw, w1_0, fetch, reduce_start, reduce_midway,
                                 conv_w, conv_b, gate_g,
                                 norm_mix_g + behind_gather, norm_mlp_g, pool_b, pool_scale,
                                 ssm_dt_bias, ssm_a_log, ssm_d, final_g)

    behind = reduce_start("layer0", big0)
    small = jnp.concatenate(rows, axis=0)
    small = jnp.pad(small, ((0, (-small.shape[0]) % 8), (0, 0))) + behind
    small = _all_reduce_small(small, name="all_reduce_small", sum_row0=True)
    behind = reduce_midway("layer0", small)
    h_w1_1, h_w2_1, h_wout = reduce_finish("mlp1_out", behind)
    (h_win,) = reduce_finish("in_proj", behind)
    g_w1_1, g_w2_1, g_wout_s, g_win_s = _pair_gather_halves([h_w1_1, h_w2_1, h_wout, h_win],
                                                            name="pair_gather_layer1")
    loss = small[0, 0]
    g_final = small[1]
    g_norm_mix = small[2:4]
    g_norm_mlp = small[4:6]
    g_pool_b, g_pool_scale = small[6:7], small[7:8]
    g_alog, g_dtb, g_dsk = small[8:9, :N_HEADS], small[9:10, :N_HEADS], small[10:11, :N_HEADS]
    g_gate_full = small[11:13].reshape(1, D_INNER)
    g_convw_full = small[13:25].reshape(CONV_K, CONV_DIM)
    g_convb_full = small[25:28].reshape(1, CONV_DIM)
    g_gate = lax.dynamic_slice_in_dim(g_gate_full, my_chip * (D_INNER // N_CHIPS), D_INNER // N_CHIPS, axis=1)
    g_convw = lax.dynamic_slice_in_dim(g_convw_full, my_chip * (CONV_DIM // N_CHIPS), CONV_DIM // N_CHIPS, axis=1)
    g_convb = lax.dynamic_slice_in_dim(g_convb_full, my_chip * (CONV_DIM // N_CHIPS), CONV_DIM // N_CHIPS, axis=1)

    grads = {
        "norm_mix_g": g_norm_mix, "norm_mlp_g": g_norm_mlp,
        "pool_b": g_pool_b, "pool_scale": g_pool_scale,
        "ssm_conv_w": g_convw.reshape(ssm_conv_w.shape),
        "ssm_conv_b": g_convb, "ssm_dt_bias": g_dtb, "ssm_a_log": g_alog, "ssm_d": g_dsk,
        "ssm_norm_g": g_gate, "ssm_w_out": g_wout_s.reshape(ssm_w_out.shape),
        "final_g": g_final,
    }
    weights = dict(norm_mix_g=norm_mix_g, norm_mlp_g=norm_mlp_g, pool_w=pool_w, pool_b=pool_b,
                   pool_scale=pool_scale, ssm_w_in=ssm_w_in, ssm_conv_w=ssm_conv_w, ssm_conv_b=ssm_conv_b,
                   ssm_dt_bias=ssm_dt_bias, ssm_a_log=ssm_a_log, ssm_d=ssm_d, ssm_norm_g=ssm_norm_g,
                   ssm_w_out=ssm_w_out, mlp_w1=mlp_w1, mlp_w2=mlp_w2, final_g=final_g)
    moms = dict(norm_mix_g=(m_norm_mix_g, v_norm_mix_g), norm_mlp_g=(m_norm_mlp_g, v_norm_mlp_g),
                pool_w=(m_pool_w, v_pool_w), pool_b=(m_pool_b, v_pool_b),
                pool_scale=(m_pool_scale, v_pool_scale), ssm_w_in=(m_ssm_w_in, v_ssm_w_in),
                ssm_conv_w=(m_ssm_conv_w, v_ssm_conv_w), ssm_conv_b=(m_ssm_conv_b, v_ssm_conv_b),
                ssm_dt_bias=(m_ssm_dt_bias, v_ssm_dt_bias), ssm_a_log=(m_ssm_a_log, v_ssm_a_log),
                ssm_d=(m_ssm_d, v_ssm_d), ssm_norm_g=(m_ssm_norm_g, v_ssm_norm_g),
                ssm_w_out=(m_ssm_w_out, v_ssm_w_out), mlp_w1=(m_mlp_w1, v_mlp_w1),
                mlp_w2=(m_mlp_w2, v_mlp_w2), final_g=(m_final_g, v_final_g))
    names = list(weights)
    big_names = ("pool_w", "ssm_w_in", "ssm_w_out", "mlp_w1", "mlp_w2")
    deltas, new_m, new_v = {}, {}, {}

    def as_rows(nm, a):
        return a[0].T if nm == "ssm_w_in" else a.reshape(-1, a.shape[-1])

    def from_rows(nm, r):
        return r.T[None] if nm == "ssm_w_in" else r.reshape(weights[nm].shape)

    def update(nm, grad_rows, layer=None, into=None):
        return _adamw(as_rows(nm, weights[nm]), grad_rows, as_rows(nm, moms[nm][0]), as_rows(nm, moms[nm][1]),
                      name="adamw_%s_%s" % (nm, layer), part=layer, into=into)

    def keep(nm, results):
        deltas[nm], new_m[nm], new_v[nm] = [from_rows(nm, r) for r in results]

    g_win_rows = g_win_s[:IN_PROJ_DIM // N_CHIPS]
    grads["ssm_w_in"] = from_rows("ssm_w_in", g_win_rows)
    keep("ssm_w_in", update("ssm_w_in", g_win_rows))
    keep("ssm_w_out", update("ssm_w_out", g_wout_s))
    w1_done = update("mlp_w1", g_w1_1, layer=1)
    w2_done = update("mlp_w2", g_w2_1, layer=1)
    small_names = [nm for nm in names if nm not in big_names]
    sizes = [weights[nm].size for nm in small_names]

    def pack(parts):
        flat = jnp.concatenate([p.reshape(-1) for p in parts])
        pad = (-flat.shape[0]) % (8 * D_MODEL)
        return jnp.pad(flat, (0, pad)).reshape(-1, D_MODEL)

    d_, m_, v_ = _adamw(pack([weights[nm] for nm in small_names]), pack([grads[nm] for nm in small_names]),
                        pack([moms[nm][0] for nm in small_names]), pack([moms[nm][1] for nm in small_names]),
                        name="adamw_small")
    off = 0
    for nm, sz in zip(small_names, sizes):
        shp = weights[nm].shape
        deltas[nm] = d_.reshape(-1)[off:off + sz].reshape(shp)
        new_m[nm] = m_.reshape(-1)[off:off + sz].reshape(shp)
        new_v[nm] = v_.reshape(-1)[off:off + sz].reshape(shp)
        off += sz

    above = (deltas["ssm_w_in"][0, 0, 0] + deltas["ssm_w_out"][0, 0, 0] + w1_done[0][-1, -1]
             + w2_done[0][-1, -1] + d_[0, 0])
    g_pool_w, g_w1_0, g_w2_0 = _pair_gather_halves(reduce_finish("layer0", above), name="pair_gather_layer0")
    keep("mlp_w1", update("mlp_w1", g_w1_0, layer=0, into=w1_done))
    keep("mlp_w2", update("mlp_w2", g_w2_0, layer=0, into=w2_done))
    grads["pool_w"] = g_pool_w.reshape(pool_w.shape)
    grads["mlp_w1"] = jnp.stack([g_w1_0, g_w1_1])
    grads["mlp_w2"] = jnp.stack([g_w2_0, g_w2_1])
    keep("pool_w", update("pool_w", g_pool_w))

    grad_x = dx.reshape(x.shape)
    out_grads = [grads[nm].reshape(weights[nm].shape) for nm in names]
    return (loss, grad_x, *out_grads, *[deltas[nm] for nm in names],
            *[new_m[nm] for nm in names], *[new_v[nm] for nm in names])
```

```python
import functools

import jax
import jax.numpy as jnp
from jax import lax
from jax.experimental import pallas as pl
from jax.experimental.pallas import tpu as pltpu

F32 = jnp.float32
BF16 = jnp.bfloat16
MESH = pl.DeviceIdType.MESH

D_MODEL = 1024
RMS_EPS = 1e-5
POOL_WINDOWS = (2, 4, 8, 16)
POOL_GROUP = 256
POOL_HALO = 16
D_INNER = 2048
HEAD_DIM = 64
N_HEADS = 32
N_GROUPS = 4
HEADS_PER_GROUP = 8
D_STATE = 128
CONV_K = 4
CONV_HALO = 8
CHUNK = 128
CONV_DIM = 3072
IN_PROJ_DIM = 5152
D_FF = 4096
N_CHIPS = 4
N_DEV = 8

ADAM_LR = 0.001
ADAM_B1 = 0.9
ADAM_B2 = 0.999
ADAM_EPS = 1e-08
ADAM_WD = 0.01
ADAM_STEP = 10

VMEM_LIMIT = 56 * 1024 * 1024
NEG_INF = float("-inf")


def _pcall(body, **kw):
    return pl.pallas_call(body, **kw)


def _params(*sem):
    return pltpu.CompilerParams(dimension_semantics=sem, vmem_limit_bytes=VMEM_LIMIT)


def _sigmoid(v):
    return 1.0 / (1.0 + jnp.exp(-v))


def _row_spec(tb, d, nb=None, reverse=False):
    if reverse:
        return pl.BlockSpec((tb, d), lambda i: (nb - 1 - i, 0))
    return pl.BlockSpec((tb, d), lambda i: (i, 0))


def _const_spec(shape):
    return pl.BlockSpec(shape, lambda *_: tuple(0 for _ in shape))


_DIMS = {"nn": (((1,), (0,)), ((), ())),
         "nt": (((1,), (1,)), ((), ())),
         "tn": (((0,), (0,)), ((), ()))}


_MATMUL_VMEM_BUDGET = 40 * 1024 * 1024


def _matmul_tiles(m_dim, n_dim, k_dim, a_bytes, b_bytes, mn_bytes):
    tm, tn = min(m_dim, 1024), min(n_dim, 1024)
    while 2 * (tm * k_dim * a_bytes + tn * k_dim * b_bytes + tm * tn * mn_bytes) > _MATMUL_VMEM_BUDGET:
        if tm >= tn:
            tm //= 2
        else:
            tn //= 2
    return tm, tn


def _matmul(a, b, mode, *, name, out_dtype=F32, a_relu2=False, add=None, relu2_grad_of=None,
            out_col_shards=1, b_col_shards=False):
    if mode == "tn":
        k_dim, m_dim = a.shape
    else:
        m_dim, k_dim = a.shape
    if b_col_shards:
        n_shards, shard_cols = b.shape[0], b.shape[2]
        n_dim = n_shards * shard_cols if mode == "nn" else b.shape[1]
    else:
        n_dim = b.shape[0] if mode == "nt" else b.shape[1]
    mn_bytes = jnp.dtype(out_dtype).itemsize
    if relu2_grad_of is not None:
        mn_bytes += relu2_grad_of.dtype.itemsize
    if add is not None:
        mn_bytes += add.dtype.itemsize
    tm, tn = _matmul_tiles(m_dim, n_dim, k_dim, a.dtype.itemsize, b.dtype.itemsize, mn_bytes)
    assert m_dim % tm == 0 and n_dim % tn == 0
    a_spec = (pl.BlockSpec((k_dim, tm), lambda i, j: (0, i)) if mode == "tn"
              else pl.BlockSpec((tm, k_dim), lambda i, j: (i, 0)))
    if b_col_shards and mode == "nn":
        assert shard_cols % tn == 0
        per_shard = shard_cols // tn
        b_spec = pl.BlockSpec((None, k_dim, tn), lambda i, j: (j // per_shard, 0, j % per_shard))
    elif b_col_shards:
        assert mode == "nt" and k_dim == n_shards * shard_cols
        b_spec = pl.BlockSpec((n_shards, tn, shard_cols), lambda i, j: (0, j, 0))
    else:
        b_spec = (pl.BlockSpec((tn, k_dim), lambda i, j: (j, 0)) if mode == "nt"
                  else pl.BlockSpec((k_dim, tn), lambda i, j: (0, j)))
    mn_spec = pl.BlockSpec((tm, tn), lambda i, j: (i, j))
    operands, in_specs = [a, b], [a_spec, b_spec]
    if relu2_grad_of is not None:
        operands.append(relu2_grad_of)
        in_specs.append(mn_spec)
    if add is not None:
        operands.append(add)
        in_specs.append(mn_spec)
    if out_col_shards == 1:
        out_shape = jax.ShapeDtypeStruct((m_dim, n_dim), out_dtype)
        out_spec = mn_spec
    else:
        n_shard = n_dim // out_col_shards
        assert n_shard % tn == 0
        per = n_shard // tn
        out_shape = jax.ShapeDtypeStruct((out_col_shards, m_dim, n_shard), out_dtype)
        out_spec = pl.BlockSpec((None, tm, tn), lambda i, j: (j // per, i, j % per))

    def body(*refs):
        a_ref, b_ref, o_ref = refs[0], refs[1], refs[-1]
        av = a_ref[...]
        if a_relu2:
            av = jnp.maximum(av, 0)
            av = av * av
        if b_col_shards and mode == "nt":
            r = None
            for s in range(n_shards):
                part = lax.dot_general(av[:, s * shard_cols:(s + 1) * shard_cols].astype(BF16),
                                       b_ref[s].astype(BF16), _DIMS[mode], preferred_element_type=F32)
                r = part if r is None else r + part
        else:
            r = lax.dot_general(av.astype(BF16), b_ref[...].astype(BF16), _DIMS[mode],
                                preferred_element_type=F32)
        nxt = 2
        if relu2_grad_of is not None:
            r = r * (2.0 * jnp.maximum(refs[nxt][...].astype(F32), 0.0))
            nxt += 1
        if add is not None:
            r = r + refs[nxt][...]
        o_ref[...] = r.astype(out_dtype)

    return _pcall(
        body, name=name, out_shape=out_shape,
        grid=(m_dim // tm, n_dim // tn),
        in_specs=in_specs, out_specs=out_spec,
        compiler_params=_params("parallel", "parallel"),
    )(*operands)


def _rms(x):
    return lax.rsqrt(jnp.mean(x * x, axis=-1, keepdims=True) + RMS_EPS)


def _rmsnorm_fwd(h, g, *, name, tb=512):
    t_dim, d = h.shape

    def body(h_ref, g_ref, o_ref):
        x = h_ref[...]
        o_ref[...] = (x * _rms(x) * g_ref[...]).astype(BF16)

    return _pcall(
        body, name=name, out_shape=jax.ShapeDtypeStruct((t_dim, d), BF16),
        grid=(t_dim // tb,), in_specs=[_row_spec(tb, d), _const_spec((1, d))],
        out_specs=_row_spec(tb, d), compiler_params=_params("parallel"),
    )(h, g)


def _rmsnorm_bwd(dy, h, g, dres, *, name, tb=512):
    t_dim, d = h.shape

    def body(dy_ref, h_ref, g_ref, dres_ref, dh_ref, dh16_ref, dg_ref):
        @pl.when(pl.program_id(0) == 0)
        def _():
            dg_ref[...] = jnp.zeros_like(dg_ref)

        x = h_ref[...]
        r = _rms(x)
        xhat = x * r
        dyv = dy_ref[...]
        dxhat = dyv * g_ref[...]
        dh = dres_ref[...] + r * (dxhat - xhat * jnp.mean(dxhat * xhat, axis=-1, keepdims=True))
        dh_ref[...] = dh
        dh16_ref[...] = dh.astype(BF16)
        dg_ref[0:1, :] += jnp.sum(dyv * xhat, axis=0, keepdims=True)

    return _pcall(
        body, name=name,
        out_shape=(jax.ShapeDtypeStruct((t_dim, d), F32), jax.ShapeDtypeStruct((t_dim, d), BF16),
                   jax.ShapeDtypeStruct((8, d), F32)),
        grid=(t_dim // tb,),
        in_specs=[_row_spec(tb, d), _row_spec(tb, d), _const_spec((1, d)), _row_spec(tb, d)],
        out_specs=(_row_spec(tb, d), _row_spec(tb, d), _const_spec((8, d))),
        compiler_params=_params("arbitrary"),
    )(dy, h, g, dres)


def _pool_mixed(ext, hn, t0, tb):
    t = t0 + lax.broadcasted_iota(jnp.int32, (tb, 1), 0)
    parts = []
    for gi, w in enumerate(POOL_WINDOWS):
        lanes = slice(gi * POOL_GROUP, (gi + 1) * POOL_GROUP)
        s = ext[:, lanes]
        k = 1
        while k < w:
            s = s + pltpu.roll(s, k, 0)
            k *= 2
        cnt = jnp.minimum(t + 1, w).astype(F32)
        parts.append(s[POOL_HALO:, :] / cnt - hn[:, lanes])
    return parts


def _pool_fwd(x, g, pw, pb, ps, *, tb=512):
    t_dim, d = x.shape

    def body(x_ref, g_ref, pw_ref, pb_ref, ps_ref, o_ref, ext_ref):
        i = pl.program_id(0)

        @pl.when(i == 0)
        def _():
            ext_ref[0:POOL_HALO, :] = jnp.zeros((POOL_HALO, d), F32)

        xv = x_ref[...]
        hn = xv * _rms(xv) * g_ref[...]
        ext_ref[POOL_HALO:, :] = hn
        mixed = _pool_mixed(ext_ref[...], hn, i * tb, tb)
        for gi in range(len(POOL_WINDOWS)):
            lanes = slice(gi * POOL_GROUP, (gi + 1) * POOL_GROUP)
            out = jnp.dot(mixed[gi].astype(BF16), pw_ref[gi], preferred_element_type=F32)
            o_ref[:, lanes] = xv[:, lanes] + (out + pb_ref[:, lanes]) * ps_ref[:, lanes]
        ext_ref[0:POOL_HALO, :] = hn[tb - POOL_HALO:, :]

    return _pcall(
        body, name="pool_fwd", out_shape=jax.ShapeDtypeStruct((t_dim, d), F32),
        grid=(t_dim // tb,),
        in_specs=[_row_spec(tb, d), _const_spec((1, d)), _const_spec((4, POOL_GROUP, POOL_GROUP)),
                  _const_spec((1, d)), _const_spec((1, d))],
        out_specs=_row_spec(tb, d),
        scratch_shapes=[pltpu.VMEM((POOL_HALO + tb, d), F32)],
        compiler_params=_params("arbitrary"),
    )(x, g, pw, pb, ps)


def _pool_bwd(x, g, pw, pb, ps, dh1, *, tb=512):
    t_dim, d = x.shape
    nb = t_dim // tb
    halo_per_block = tb // POOL_HALO

    def body(x_ref, xprev_ref, g_ref, pw_ref, pb_ref, ps_ref, dh1_ref,
             dx_ref, dpw_ref, small_ref, ext_ref, dext_ref):
        i = pl.program_id(0)
        blk = nb - 1 - i

        @pl.when(i == 0)
        def _():
            dpw_ref[...] = jnp.zeros_like(dpw_ref)
            small_ref[...] = jnp.zeros_like(small_ref)
            dext_ref[tb:, :] = jnp.zeros((POOL_HALO, d), F32)

        gv = g_ref[...]
        xv = x_ref[...]
        r = _rms(xv)
        xhat = xv * r
        hn = xhat * gv
        xp = xprev_ref[...]
        hprev = xp * _rms(xp) * gv * (blk > 0).astype(F32)
        ext_ref[0:POOL_HALO, :] = hprev
        ext_ref[POOL_HALO:, :] = hn
        mixed = _pool_mixed(ext_ref[...], hn, blk * tb, tb)

        dout = dh1_ref[...]
        t = blk * tb + lax.broadcasted_iota(jnp.int32, (tb, 1), 0)
        for gi, w in enumerate(POOL_WINDOWS):
            lanes = slice(gi * POOL_GROUP, (gi + 1) * POOL_GROUP)
            mb = mixed[gi].astype(BF16)
            pre = jnp.dot(mb, pw_ref[gi], preferred_element_type=F32) + pb_ref[:, lanes]
            dg_out = dout[:, lanes]
            small_ref[2:3, lanes] += jnp.sum(dg_out * pre, axis=0, keepdims=True)
            dpre = dg_out * ps_ref[:, lanes]
            small_ref[1:2, lanes] += jnp.sum(dpre, axis=0, keepdims=True)
            dpb16 = dpre.astype(BF16)
            dpw_ref[gi] += lax.dot_general(mb, dpb16, _DIMS["tn"], preferred_element_type=F32)
            dmixed = lax.dot_general(dpb16, pw_ref[gi], _DIMS["nt"], preferred_element_type=F32)
            cnt = jnp.minimum(t + 1, w).astype(F32)
            dq = dmixed / cnt
            dext_ref[0:tb, lanes] = dq
            s = dext_ref[:, lanes]
            k = 1
            while k < w:
                s = s + pltpu.roll(s, tb + POOL_HALO - k, 0)
                k *= 2
            dhn = s[0:tb, :] - dmixed
            dext_ref[tb:, lanes] = dq[0:POOL_HALO, :]
            small_ref[0:1, lanes] += jnp.sum(dhn * xhat[:, lanes], axis=0, keepdims=True)
            ext_ref[POOL_HALO:, lanes] = dhn * gv[:, lanes]
        dxhat = ext_ref[POOL_HALO:, :]
        dx_ref[...] = dout + r * (dxhat - xhat * jnp.mean(dxhat * xhat, axis=-1, keepdims=True))

    return _pcall(
        body, name="pool_bwd",
        out_shape=(jax.ShapeDtypeStruct((t_dim, d), F32),
                   jax.ShapeDtypeStruct((4, POOL_GROUP, POOL_GROUP), F32),
                   jax.ShapeDtypeStruct((8, d), F32)),
        grid=(nb,),
        in_specs=[_row_spec(tb, d, nb, True),
                  pl.BlockSpec((POOL_HALO, d),
                               lambda i: (jnp.maximum((nb - 1 - i) * halo_per_block - 1, 0), 0)),
                  _const_spec((1, d)), _const_spec((4, POOL_GROUP, POOL_GROUP)),
                  _const_spec((1, d)), _const_spec((1, d)), _row_spec(tb, d, nb, True)],
        out_specs=(_row_spec(tb, d, nb, True), _const_spec((4, POOL_GROUP, POOL_GROUP)),
                   _const_spec((8, d))),
        scratch_shapes=[pltpu.VMEM((POOL_HALO + tb, d), F32), pltpu.VMEM((tb + POOL_HALO, d), F32)],
        compiler_params=_params("arbitrary"),
    )(x, x, g, pw, pb, ps, dh1)


_CONV_CB = 1024
_STRIP = 16


def _strips(tb, fn, unroll=4):
    def step(i, carry):
        fn(pl.multiple_of(i * _STRIP, _STRIP))
        return carry
    lax.fori_loop(0, tb // _STRIP, step, 0, unroll=unroll)


def _conv_taps(ext_ref, r0, w):
    shifted = [ext_ref[CONV_HALO + r0 - sh:CONV_HALO + r0 - sh + _STRIP, :] for sh in range(CONV_K)]
    acc = shifted[0] * w[CONV_K - 1:CONV_K, :]
    for sh in range(1, CONV_K):
        acc = acc + shifted[sh] * w[CONV_K - 1 - sh:CONV_K - sh, :]
    return shifted, acc


def _conv_fwd(u, w, b, *, tb=512):
    t_dim, c = u.shape
    cb = _CONV_CB

    def body(u_ref, w_ref, b_ref, o_ref, ext_ref):
        @pl.when(pl.program_id(1) == 0)
        def _():
            ext_ref[0:CONV_HALO, :] = jnp.zeros((CONV_HALO, cb), F32)

        wv = w_ref[...]
        bv = b_ref[...]

        def fill(r0):
            ext_ref[pl.ds(CONV_HALO + r0, _STRIP), :] = u_ref[pl.ds(r0, _STRIP), :].astype(F32)

        _strips(tb, fill)
        for r0 in range(0, tb, _STRIP):
            v = _conv_taps(ext_ref, r0, wv)[1] + bv
            o_ref[r0:r0 + _STRIP, :] = (v * _sigmoid(v)).astype(BF16)
        ext_ref[0:CONV_HALO, :] = ext_ref[tb:tb + CONV_HALO, :]

    blk = pl.BlockSpec((tb, cb), lambda j, t: (t, j))
    return _pcall(
        body, name="conv_fwd", out_shape=jax.ShapeDtypeStruct((t_dim, c), BF16),
        grid=(c // cb, t_dim // tb),
        in_specs=[blk, pl.BlockSpec((CONV_K, cb), lambda j, t: (0, j)),
                  pl.BlockSpec((1, cb), lambda j, t: (0, j))],
        out_specs=blk,
        scratch_shapes=[pltpu.VMEM((CONV_HALO + tb, cb), F32)],
        compiler_params=_params("parallel", "arbitrary"),
    )(u, w, b)


def _conv_bwd_act(u, dxc, w, b, *, tb=512):
    t_dim, c = u.shape
    cb = _CONV_CB
    half = _STRIP // 2

    def body(u_ref, d_ref, w_ref, b_ref, dv_ref, dwb_ref, ext_ref, acc_ref):
        @pl.when(pl.program_id(1) == 0)
        def _():
            ext_ref[0:CONV_HALO, :] = jnp.zeros((CONV_HALO, cb), F32)
            dwb_ref[...] = jnp.zeros_like(dwb_ref)

        acc_ref[...] = jnp.zeros_like(acc_ref)
        wv = w_ref[...]
        bv = b_ref[...]

        def fill(r0):
            ext_ref[pl.ds(CONV_HALO + r0, _STRIP), :] = u_ref[pl.ds(r0, _STRIP), :].astype(F32)

        _strips(tb, fill)
        for r0 in range(0, tb, _STRIP):
            shifted, v = _conv_taps(ext_ref, r0, wv)
            v = v + bv
            sg = _sigmoid(v)
            dv = d_ref[r0:r0 + _STRIP, :].astype(F32) * (sg * (1.0 + v * (1.0 - sg)))
            dv_ref[r0:r0 + _STRIP, :] = dv.astype(BF16)
            acc_ref[CONV_K] += dv[0:half, :] + dv[half:, :]
            for sh in range(CONV_K):
                p = dv * shifted[sh]
                acc_ref[CONV_K - 1 - sh] += p[0:half, :] + p[half:, :]
        for k in range(CONV_K + 1):
            dwb_ref[k:k + 1, :] += jnp.sum(acc_ref[k], axis=0, keepdims=True)
        ext_ref[0:CONV_HALO, :] = ext_ref[tb:tb + CONV_HALO, :]

    blk = pl.BlockSpec((tb, cb), lambda j, t: (t, j))
    return _pcall(
        body, name="conv_bwd_act",
        out_shape=(jax.ShapeDtypeStruct((t_dim, c), BF16), jax.ShapeDtypeStruct((8, c), F32)),
        grid=(c // cb, t_dim // tb),
        in_specs=[blk, blk, pl.BlockSpec((CONV_K, cb), lambda j, t: (0, j)),
                  pl.BlockSpec((1, cb), lambda j, t: (0, j))],
        out_specs=(blk, pl.BlockSpec((8, cb), lambda j, t: (0, j))),
        scratch_shapes=[pltpu.VMEM((CONV_HALO + tb, cb), F32), pltpu.VMEM((CONV_K + 1, half, cb), F32)],
        compiler_params=_params("parallel", "arbitrary"),
    )(u, dxc, w, b)


def _conv_bwd_in(dv, w, *, tb=512):
    t_dim, c = dv.shape
    cb = _CONV_CB
    nb = t_dim // tb

    def body(dv_ref, w_ref, du_ref, ext_ref):
        @pl.when(pl.program_id(1) == 0)
        def _():
            ext_ref[tb:, :] = jnp.zeros((CONV_HALO, cb), F32)

        wv = w_ref[...]

        def fill(r0):
            ext_ref[pl.ds(r0, _STRIP), :] = dv_ref[pl.ds(r0, _STRIP), :].astype(F32)

        _strips(tb, fill)
        for r0 in range(0, tb, _STRIP):
            acc = ext_ref[r0:r0 + _STRIP, :] * wv[CONV_K - 1:CONV_K, :]
            for sh in range(1, CONV_K):
                acc = acc + ext_ref[r0 + sh:r0 + sh + _STRIP, :] * wv[CONV_K - 1 - sh:CONV_K - sh, :]
            du_ref[r0:r0 + _STRIP, :] = acc.astype(BF16)
        ext_ref[tb:, :] = ext_ref[0:CONV_HALO, :]

    blk = pl.BlockSpec((tb, cb), lambda j, t: (nb - 1 - t, j))
    return _pcall(
        body, name="conv_bwd_in", out_shape=jax.ShapeDtypeStruct((t_dim, c), BF16),
        grid=(c // cb, nb),
        in_specs=[blk, pl.BlockSpec((CONV_K, cb), lambda j, t: (0, j))],
        out_specs=blk,
        scratch_shapes=[pltpu.VMEM((tb + CONV_HALO, cb), F32)],
        compiler_params=_params("parallel", "arbitrary"),
    )(dv, w)


def _softplus(v):
    e = jnp.exp(-jnp.abs(v))
    w = 1.0 + e
    log1p = jnp.where(w == 1.0, e, jnp.log(w) * e / jnp.where(w == 1.0, 1.0, w - 1.0))
    return jnp.maximum(v, 0.0) + log1p


def _cumsum_rows(v):
    row = lax.broadcasted_iota(jnp.int32, v.shape, 0)
    k = 1
    while k < CHUNK:
        v = v + jnp.where(row >= k, pltpu.roll(v, k, 0), 0.0)
        k *= 2
    return v


def _cumsum_lanes(v):
    col = lax.broadcasted_iota(jnp.int32, v.shape, 1)
    k = 1
    while k < CHUNK:
        v = v + jnp.where(col >= k, pltpu.roll(v, k, 1), 0.0)
        k *= 2
    return v


def _rev_cumsum_rows(v):
    row = lax.broadcasted_iota(jnp.int32, v.shape, 0)
    k = 1
    while k < CHUNK:
        v = v + jnp.where(row < CHUNK - k, pltpu.roll(v, CHUNK - k, 0), 0.0)
        k *= 2
    return v


PAIR = 2 * HEAD_DIM
GROUP_LANES = HEADS_PER_GROUP * HEAD_DIM


def _head_lane_matrix():
    h = lax.broadcasted_iota(jnp.int32, (128, D_INNER), 0)
    j = lax.broadcasted_iota(jnp.int32, (128, D_INNER), 1)
    return (j // HEAD_DIM == h).astype(BF16)


def _split_bf16(v, pieces):
    out = []
    for _ in range(pieces):
        p = v.astype(BF16)
        out.append(p)
        v = v - p.astype(F32)
    return out


def _expand_heads(v, e):
    return sum(jnp.dot(p, e, preferred_element_type=F32) for p in _split_bf16(v, 3))


def _reduce_heads(v, et, pieces):
    return sum(jnp.dot(p, et, preferred_element_type=F32) for p in _split_bf16(v, pieces))


def _ssd_common(dtr_ref, dtt_ref, e_ref, bx, bc, ax, ac):
    dtx = _expand_heads(dtr_ref[...], e_ref[...])
    dt = _softplus(dtx + bx)
    a_x = -jnp.exp(ax)
    acs = _cumsum_rows(dt * a_x)
    acs_c = _cumsum_lanes(_softplus(dtt_ref[...] + bc) * (-jnp.exp(ac)))
    return dtx, dt, a_x, acs, acs_c


def _pair_decay(acs_slab, acs_c, h0, causal, left):
    other = pltpu.roll(acs_slab, HEAD_DIM, 1)
    col0 = jnp.where(left, acs_slab, other)
    col1 = jnp.where(left, other, acs_slab)
    l0 = jnp.exp(jnp.where(causal, col0 - acs_c[h0:h0 + 1, :], NEG_INF))
    l1 = jnp.exp(jnp.where(causal, col1 - acs_c[h0 + 1:h0 + 2, :], NEG_INF))
    return l0, l1


def _ssd_fwd(xc, dt_raw, dt_raw_t, bias_x, bias_c, alog_x, alog_c, dskip_x, e_mat):
    t_dim = xc.shape[0]
    nc = t_dim // CHUNK

    def body(xc_ref, dtr_ref, dtt_ref, bx_ref, bc_ref, ax_ref, ac_ref, dk_ref, e_ref,
             y_ref, st_ref, state):
        @pl.when(pl.program_id(0) == 0)
        def _():
            state[...] = jnp.zeros_like(state)

        _, dt, _, acs, acs_c = _ssd_common(dtr_ref, dtt_ref, e_ref, bx_ref[...], bc_ref[...],
                                           ax_ref[...], ac_ref[...])
        st_ref[0] = state[...]
        last = acs[CHUNK - 1:CHUNK, :]
        xs32 = xc_ref[:, 0:D_INNER].astype(F32)
        xdt = xs32 * dt
        xdt16 = xdt.astype(BF16)
        xdte16 = (xdt * jnp.exp(last - acs)).astype(BF16)
        ea = jnp.exp(acs)
        cd = jnp.exp(last)
        skip = dk_ref[...] * xs32
        causal = (lax.broadcasted_iota(jnp.int32, (CHUNK, CHUNK), 0)
                  >= lax.broadcasted_iota(jnp.int32, (CHUNK, CHUNK), 1))
        left = lax.broadcasted_iota(jnp.int32, (CHUNK, PAIR), 1) < HEAD_DIM
        for g in range(N_GROUPS):
            gl = slice(g * GROUP_LANES, (g + 1) * GROUP_LANES)
            bg = xc_ref[:, D_INNER + g * D_STATE:D_INNER + (g + 1) * D_STATE]
            cg = xc_ref[:, D_INNER + (N_GROUPS + g) * D_STATE:D_INNER + (N_GROUPS + g + 1) * D_STATE]
            cb = lax.dot_general(cg, bg, _DIMS["nt"], preferred_element_type=F32)
            hprev = state[:, gl]
            ch = jnp.dot(cg, hprev.astype(BF16), preferred_element_type=F32)
            for j in range(HEADS_PER_GROUP // 2):
                pl_ = slice(g * GROUP_LANES + j * PAIR, g * GROUP_LANES + (j + 1) * PAIR)
                h0 = g * HEADS_PER_GROUP + 2 * j
                l0, l1 = _pair_decay(acs[:, pl_], acs_c, h0, causal, left)
                lhs = jnp.concatenate([(cb * l0).astype(BF16), (cb * l1).astype(BF16)], axis=1)
                xp = xdt16[:, pl_]
                zero = jnp.zeros_like(xp)
                rhs = jnp.concatenate([jnp.where(left, xp, zero), jnp.where(left, zero, xp)], axis=0)
                ydiag = jnp.dot(lhs, rhs, preferred_element_type=F32)
                y_ref[:, pl_] = ydiag + ch[:, j * PAIR:(j + 1) * PAIR] * ea[:, pl_] + skip[:, pl_]
            s_new = lax.dot_general(bg, xdte16[:, gl], _DIMS["tn"], preferred_element_type=F32)
            state[:, gl] = hprev * cd[:, gl] + s_new

    rows = lambda w: pl.BlockSpec((CHUNK, w), lambda c: (c, 0))
    return _pcall(
        body, name="ssd_fwd",
        out_shape=(jax.ShapeDtypeStruct((t_dim, D_INNER), F32),
                   jax.ShapeDtypeStruct((nc, D_STATE, D_INNER), F32)),
        grid=(nc,),
        in_specs=[rows(CONV_DIM), rows(128), pl.BlockSpec((N_HEADS, CHUNK), lambda c: (0, c)),
                  _const_spec((1, D_INNER)), _const_spec((N_HEADS, 1)),
                  _const_spec((1, D_INNER)), _const_spec((N_HEADS, 1)), _const_spec((1, D_INNER)),
                  _const_spec((128, D_INNER))],
        out_specs=(rows(D_INNER), pl.BlockSpec((1, D_STATE, D_INNER), lambda c: (c, 0, 0))),
        scratch_shapes=[pltpu.VMEM((D_STATE, D_INNER), F32)],
        compiler_params=_params("arbitrary"),
    )(xc, dt_raw, dt_raw_t, bias_x, bias_c, alog_x, alog_c, dskip_x, e_mat)


def _ssd_bwd(xc, dt_raw, dt_raw_t, bias_x, bias_c, alog_x, alog_c, dskip_x, bias_r, alog_r,
             e_mat, et_mat, states, dy):
    t_dim = xc.shape[0]
    nc = t_dim // CHUNK

    def body(xc_ref, dtr_ref, dtt_ref, bx_ref, bc_ref, ax_ref, ac_ref, dk_ref, br_ref, ar_ref,
             e_ref, et_ref, st_ref, dy_ref,
             dxc_ref, ddt_ref, small_ref, dstate, dacs_ref, dxdt_ref, acc_x, acc_r):
        step = pl.program_id(0)

        @pl.when(step == 0)
        def _():
            dstate[...] = jnp.zeros_like(dstate)
            acc_x[...] = jnp.zeros_like(acc_x)
            acc_r[...] = jnp.zeros_like(acc_r)

        dtx, dt, a_x, acs, acs_c = _ssd_common(dtr_ref, dtt_ref, e_ref, bx_ref[...], bc_ref[...],
                                               ax_ref[...], ac_ref[...])
        last = acs[CHUNK - 1:CHUNK, :]
        xs32 = xc_ref[:, 0:D_INNER].astype(F32)
        xdt = xs32 * dt
        xdt16 = xdt.astype(BF16)
        dte = jnp.exp(last - acs)
        xdte = xdt * dte
        xdte16 = xdte.astype(BF16)
        cd = jnp.exp(last)
        dyv = dy_ref[...]
        dy16 = dyv.astype(BF16)
        dye = dyv * jnp.exp(acs)
        dye16 = dye.astype(BF16)
        causal = (lax.broadcasted_iota(jnp.int32, (CHUNK, CHUNK), 0)
                  >= lax.broadcasted_iota(jnp.int32, (CHUNK, CHUNK), 1))
        left = lax.broadcasted_iota(jnp.int32, (CHUNK, PAIR), 1) < HEAD_DIM
        lane_id = lax.broadcasted_iota(jnp.int32, (CHUNK, 128), 1)
        row_id = lax.broadcasted_iota(jnp.int32, (CHUNK, 128), 0)
        is_last_row = lax.broadcasted_iota(jnp.int32, (CHUNK, 1), 0) == CHUNK - 1
        dacs_cols = jnp.zeros((CHUNK, 128), F32)
        dacs_rows = jnp.zeros((CHUNK, 128), F32)
        for g in range(N_GROUPS):
            gl = slice(g * GROUP_LANES, (g + 1) * GROUP_LANES)
            b_lanes = slice(D_INNER + g * D_STATE, D_INNER + (g + 1) * D_STATE)
            c_lanes = slice(D_INNER + (N_GROUPS + g) * D_STATE, D_INNER + (N_GROUPS + g + 1) * D_STATE)
            bg = xc_ref[:, b_lanes]
            cg = xc_ref[:, c_lanes]
            cb = lax.dot_general(cg, bg, _DIMS["nt"], preferred_element_type=F32)
            hprev = st_ref[0, :, gl]
            hp16 = hprev.astype(BF16)
            dhn = dstate[:, gl]
            dhn16 = dhn.astype(BF16)
            ch = jnp.dot(cg, hp16, preferred_element_type=F32)
            gmat = jnp.dot(bg, dhn16, preferred_element_type=F32)
            gx = gmat * xdte[:, gl]
            dlast = jnp.sum(gx, axis=0, keepdims=True) + cd[:, gl] * jnp.sum(dhn * hprev, axis=0, keepdims=True)
            dacs_ref[:, gl] = dye[:, gl] * ch - gx + jnp.where(is_last_row, dlast, 0.0)
            dc_acc = lax.dot_general(dye16[:, gl], hp16, _DIMS["nt"], preferred_element_type=F32)
            db_acc = lax.dot_general(xdte16[:, gl], dhn16, _DIMS["nt"], preferred_element_type=F32)
            dstate[:, gl] = dhn * cd[:, gl] + lax.dot_general(cg, dye16[:, gl], _DIMS["tn"],
                                                             preferred_element_type=F32)
            dcb = jnp.zeros((CHUNK, CHUNK), F32)
            for j in range(HEADS_PER_GROUP // 2):
                pl_ = slice(g * GROUP_LANES + j * PAIR, g * GROUP_LANES + (j + 1) * PAIR)
                h0 = g * HEADS_PER_GROUP + 2 * j
                l0, l1 = _pair_decay(acs[:, pl_], acs_c, h0, causal, left)
                m0, m1 = cb * l0, cb * l1
                lhs = jnp.concatenate([m0.astype(BF16), m1.astype(BF16)], axis=1)
                dyp = dy16[:, pl_]
                zero = jnp.zeros_like(dyp)
                both = lax.dot_general(lhs, dyp, _DIMS["tn"], preferred_element_type=F32)
                dxdt_ref[:, pl_] = (jnp.where(left, both[0:CHUNK, :], both[CHUNK:, :])
                                    + gmat[:, j * PAIR:(j + 1) * PAIR] * dte[:, pl_])
                lhs2 = jnp.concatenate([jnp.where(left, dyp, zero), jnp.where(left, zero, dyp)], axis=0)
                dm = lax.dot_general(lhs2, xdt16[:, pl_], _DIMS["nt"], preferred_element_type=F32)
                dm0, dm1 = dm[0:CHUNK, :], dm[CHUNK:, :]
                dcb = dcb + dm0 * l0 + dm1 * l1
                ds0, ds1 = dm0 * m0, dm1 * m1
                dacs_cols = jnp.where(lane_id == h0, jnp.sum(ds0, axis=1, keepdims=True), dacs_cols)
                dacs_cols = jnp.where(lane_id == h0 + 1, jnp.sum(ds1, axis=1, keepdims=True), dacs_cols)
                dacs_rows = jnp.where(row_id == h0, jnp.sum(ds0, axis=0, keepdims=True), dacs_rows)
                dacs_rows = jnp.where(row_id == h0 + 1, jnp.sum(ds1, axis=0, keepdims=True), dacs_rows)
            dcb16 = dcb.astype(BF16)
            dxc_ref[:, c_lanes] = (dc_acc + jnp.dot(dcb16, bg, preferred_element_type=F32)).astype(BF16)
            dxc_ref[:, b_lanes] = (db_acc + lax.dot_general(dcb16, cg, _DIMS["tn"],
                                                           preferred_element_type=F32)).astype(BF16)
        dxdt = dxdt_ref[...]
        dxc_ref[:, 0:D_INNER] = (dxdt * dt + dk_ref[...] * dyv).astype(BF16)
        dadt = _rev_cumsum_rows(dacs_ref[...])
        ddraw_x = (dxdt * xs32 + dadt * a_x) * _sigmoid(dtx + bx_ref[...])
        acc_x[0:1, :] += jnp.sum(dadt * dt, axis=0, keepdims=True) * a_x
        acc_x[1:2, :] += jnp.sum(ddraw_x, axis=0, keepdims=True)
        acc_x[2:3, :] += jnp.sum(dyv * xs32, axis=0, keepdims=True)
        a_r = -jnp.exp(ar_ref[...])
        pre_r = dtr_ref[...] + br_ref[...]
        dadt_r = _rev_cumsum_rows(dacs_cols - dacs_rows.T)
        ddraw_r = jnp.where(lane_id < N_HEADS, dadt_r * a_r * _sigmoid(pre_r), 0.0)
        acc_r[0:1, :] += jnp.where(lane_id[0:1, :] < N_HEADS,
                                   jnp.sum(dadt_r * _softplus(pre_r), axis=0, keepdims=True) * a_r, 0.0)
        acc_r[1:2, :] += jnp.sum(ddraw_r, axis=0, keepdims=True)
        ddt_ref[...] = ddraw_r + _reduce_heads(ddraw_x, et_ref[...], 2)

        @pl.when(step == nc - 1)
        def _():
            small_ref[...] = acc_r[...] + _reduce_heads(acc_x[...], et_ref[...], 3)

    rev = lambda w: pl.BlockSpec((CHUNK, w), lambda c: (nc - 1 - c, 0))
    return _pcall(
        body, name="ssd_bwd",
        out_shape=(jax.ShapeDtypeStruct((t_dim, CONV_DIM), BF16),
                   jax.ShapeDtypeStruct((t_dim, 128), F32),
                   jax.ShapeDtypeStruct((8, 128), F32)),
        grid=(nc,),
        in_specs=[rev(CONV_DIM), rev(128), pl.BlockSpec((N_HEADS, CHUNK), lambda c: (0, nc - 1 - c)),
                  _const_spec((1, D_INNER)), _const_spec((N_HEADS, 1)),
                  _const_spec((1, D_INNER)), _const_spec((N_HEADS, 1)), _const_spec((1, D_INNER)),
                  _const_spec((1, 128)), _const_spec((1, 128)),
                  _const_spec((128, D_INNER)), _const_spec((D_INNER, 128)),
                  pl.BlockSpec((1, D_STATE, D_INNER), lambda c: (nc - 1 - c, 0, 0)),
                  rev(D_INNER)],
        out_specs=(rev(CONV_DIM), rev(128), _const_spec((8, 128))),
        scratch_shapes=[pltpu.VMEM((D_STATE, D_INNER), F32), pltpu.VMEM((CHUNK, D_INNER), F32),
                        pltpu.VMEM((CHUNK, D_INNER), F32), pltpu.VMEM((8, D_INNER), F32),
                        pltpu.VMEM((8, 128), F32)],
        compiler_params=_params("arbitrary"),
    )(xc, dt_raw, dt_raw_t, bias_x, bias_c, alog_x, alog_c, dskip_x, bias_r, alog_r,
      e_mat, et_mat, states, dy)


_GATE_GROUP = D_INNER // N_GROUPS


def _gate_fwd(y, z, g, *, tb=256):
    t_dim = y.shape[0]

    def body(y_ref, z_ref, g_ref, o_ref):
        for gi in range(N_GROUPS):
            lanes = slice(gi * _GATE_GROUP, (gi + 1) * _GATE_GROUP)
            zv = z_ref[:, lanes].astype(F32)
            wv = y_ref[:, lanes] * (zv * _sigmoid(zv))
            o_ref[:, lanes] = (wv * _rms(wv) * g_ref[:, lanes]).astype(BF16)

    return _pcall(
        body, name="gate_fwd", out_shape=jax.ShapeDtypeStruct((t_dim, D_INNER), BF16),
        grid=(t_dim // tb,),
        in_specs=[_row_spec(tb, D_INNER), _row_spec(tb, D_INNER), _const_spec((1, D_INNER))],
        out_specs=_row_spec(tb, D_INNER), compiler_params=_params("parallel"),
    )(y, z, g)


def _gate_bwd(dyn, y, z, g, *, tb=256):
    t_dim = y.shape[0]

    def body(d_ref, y_ref, z_ref, g_ref, dy_ref, dz_ref, dg_ref):
        @pl.when(pl.program_id(0) == 0)
        def _():
            dg_ref[...] = jnp.zeros_like(dg_ref)

        for gi in range(N_GROUPS):
            lanes = slice(gi * _GATE_GROUP, (gi + 1) * _GATE_GROUP)
            zv = z_ref[:, lanes].astype(F32)
            sg = _sigmoid(zv)
            sz = zv * sg
            yv = y_ref[:, lanes]
            wv = yv * sz
            r = _rms(wv)
            what = wv * r
            dv = d_ref[:, lanes]
            dwhat = dv * g_ref[:, lanes]
            dw = r * (dwhat - what * jnp.mean(dwhat * what, axis=-1, keepdims=True))
            dg_ref[0:1, lanes] += jnp.sum(dv * what, axis=0, keepdims=True)
            dy_ref[:, lanes] = dw * sz
            dz_ref[:, lanes] = (dw * yv * (sg * (1.0 + zv * (1.0 - sg)))).astype(BF16)

    return _pcall(
        body, name="gate_bwd",
        out_shape=(jax.ShapeDtypeStruct((t_dim, D_INNER), F32),
                   jax.ShapeDtypeStruct((t_dim, D_INNER), BF16),
                   jax.ShapeDtypeStruct((8, D_INNER), F32)),
        grid=(t_dim // tb,),
        in_specs=[_row_spec(tb, D_INNER), _row_spec(tb, D_INNER), _row_spec(tb, D_INNER),
                  _const_spec((1, D_INNER))],
        out_specs=(_row_spec(tb, D_INNER), _row_spec(tb, D_INNER), _const_spec((8, D_INNER))),
        compiler_params=_params("arbitrary"),
    )(dyn, y, z, g)


def _loss_head(h, g, target, *, tb=512):
    t_dim, d = h.shape

    def body(h_ref, g_ref, t_ref, dh_ref, dh16_ref, small_ref):
        @pl.when(pl.program_id(0) == 0)
        def _():
            small_ref[...] = jnp.zeros_like(small_ref)

        x = h_ref[...]
        r = _rms(x)
        xhat = x * r
        gv = g_ref[...]
        err = xhat * gv - t_ref[...]
        small_ref[1:2, :] += (0.5 / d) * jnp.sum(err * err, axis=0, keepdims=True)
        dyv = err * (1.0 / d)
        dxhat = dyv * gv
        dh = r * (dxhat - xhat * jnp.mean(dxhat * xhat, axis=-1, keepdims=True))
        dh_ref[...] = dh
        dh16_ref[...] = dh.astype(BF16)
        small_ref[0:1, :] += jnp.sum(dyv * xhat, axis=0, keepdims=True)

    return _pcall(
        body, name="loss_head",
        out_shape=(jax.ShapeDtypeStruct((t_dim, d), F32), jax.ShapeDtypeStruct((t_dim, d), BF16),
                   jax.ShapeDtypeStruct((8, d), F32)),
        grid=(t_dim // tb,),
        in_specs=[_row_spec(tb, d), _const_spec((1, d)), _row_spec(tb, d)],
        out_specs=(_row_spec(tb, d), _row_spec(tb, d), _const_spec((8, d))),
        compiler_params=_params("arbitrary"),
    )(h, g, target)


def _adamw(w, g, m, v, *, name, part=None, into=None):
    r_dim, c = w.shape
    rows = r_dim if part is None else r_dim // 2
    assert g.shape == (rows, c)
    tb = max(t for t in range(8, 513, 8) if rows % t == 0)
    first = 0 if part is None else part * (rows // tb)
    c1 = 1.0 / (1.0 - ADAM_B1 ** ADAM_STEP)
    c2 = 1.0 / (1.0 - ADAM_B2 ** ADAM_STEP)

    def body(w_ref, g_ref, m_ref, v_ref, *rest):
        d_ref, mo_ref, vo_ref = rest[-3:]
        gv = g_ref[...]
        mn = ADAM_B1 * m_ref[...] + (1.0 - ADAM_B1) * gv
        vn = ADAM_B2 * v_ref[...] + (1.0 - ADAM_B2) * (gv * gv)
        mo_ref[...] = mn
        vo_ref[...] = vn
        d_ref[...] = -ADAM_LR * ((mn * c1) / (jnp.sqrt(vn * c2) + ADAM_EPS) + ADAM_WD * w_ref[...])

    spec = pl.BlockSpec((tb, c), lambda i: (first + i, 0))
    sds = jax.ShapeDtypeStruct((r_dim, c), F32)
    in_specs = [spec, _row_spec(tb, c), spec, spec]
    operands = [w, g, m, v]
    aliases = {}
    if into is not None:
        in_specs += [_ANY] * 3
        operands += list(into)
        aliases = {4: 0, 5: 1, 6: 2}
    return _pcall(
        body, name=name, out_shape=(sds, sds, sds), grid=(rows // tb,),
        in_specs=in_specs, out_specs=(spec,) * 3, input_output_aliases=aliases,
        compiler_params=_params("parallel"),
    )(*operands)


def _pair_sum(grad, recv, place, *, name):
    s_dim, r_dim, c = grad.shape
    half = r_dim // 2
    tb = 256 if half % 256 == 0 else half
    per_half = half // tb

    def body(place_ref, a_ref, b_ref, o16_ref, o32_ref):
        s = a_ref[...] + b_ref[...]
        o16_ref[...] = s.astype(BF16)

        @pl.when(pl.program_id(1) == place_ref[1])
        def _():
            o32_ref[...] = s[0]

    grid_spec = pltpu.PrefetchScalarGridSpec(
        num_scalar_prefetch=1, grid=(per_half, s_dim),
        in_specs=[pl.BlockSpec((1, tb, c), lambda i, s, p: (s, p[0] * per_half + i, 0)),
                  pl.BlockSpec((1, tb, c), lambda i, s, p: (s, i, 0))],
        out_specs=(pl.BlockSpec((1, tb, c), lambda i, s, p: (s, i, 0)),
                   pl.BlockSpec((tb, c), lambda i, s, p: (i, 0))))
    return _pcall(
        body, name=name, grid_spec=grid_spec,
        out_shape=(jax.ShapeDtypeStruct((s_dim, half, c), BF16), jax.ShapeDtypeStruct((half, c), F32)),
        compiler_params=_params("parallel", "arbitrary"),
    )(place, grad, recv)


def _chip_sum(own, recv, *, name):
    r_dim, c = own.shape
    tb = 256 if r_dim % 256 == 0 else r_dim

    def body(a_ref, b_ref, o_ref):
        s = a_ref[...]
        for k in range(1, N_CHIPS):
            s = s + b_ref[k].astype(F32)
        o_ref[...] = s

    return _pcall(
        body, name=name, out_shape=jax.ShapeDtypeStruct((r_dim, c), F32),
        grid=(r_dim // tb,),
        in_specs=[_row_spec(tb, c), pl.BlockSpec((N_CHIPS, tb, c), lambda i: (0, i, 0))],
        out_specs=_row_spec(tb, c), compiler_params=_params("parallel"),
    )(own, recv)


def _position():
    return lax.axis_index("x"), lax.axis_index("y"), lax.axis_index("c")


def _chip_peer(x, y, k):
    return x ^ (k >> 1), y ^ (k & 1)


_ANY = pl.BlockSpec(memory_space=pl.ANY)
_TOKEN = jax.ShapeDtypeStruct((8, 128), F32)


def _all_gather_weights(shards):
    n = len(shards)
    hops = N_CHIPS - 1

    def body(*refs):
        srcs, outs, done = refs[:n], refs[n:2 * n], refs[2 * n]
        send_sems, recv_sems = refs[2 * n + 1:]
        x, y, c = _position()
        me = 2 * x + y
        done[...] = jnp.zeros_like(done)

        def over_ici(w, k, chip, to):
            return pltpu.make_async_remote_copy(
                src_ref=srcs[w].at[c], dst_ref=outs[w].at[chip, c],
                send_sem=send_sems.at[w, k - 1], recv_sem=recv_sems.at[w, k - 1],
                device_id=to, device_id_type=MESH)

        def over_d2d(w, k, chip, half):
            return pltpu.make_async_remote_copy(
                src_ref=outs[w].at[chip, half], dst_ref=outs[w].at[chip, half],
                send_sem=send_sems.at[w, hops + k - 1], recv_sem=recv_sems.at[w, hops + k - 1],
                device_id=(x, y, 1 - c), device_id_type=MESH)

        sends = []
        for w in range(n):
            for k in range(1, N_CHIPS):
                px, py = _chip_peer(x, y, k)
                cp = over_ici(w, k, me, (px, py, c))
                cp.start()
                sends.append(cp)
        for w in range(n):
            for k in range(1, N_CHIPS):
                px, py = _chip_peer(x, y, k)
                over_ici(w, k, 2 * px + py, (px, py, c)).wait_recv()
                cp = over_d2d(w, k, 2 * px + py, c)
                cp.start()
                sends.append(cp)
        for w in range(n):
            for k in range(1, N_CHIPS):
                px, py = _chip_peer(x, y, k)
                over_d2d(w, k, 2 * px + py, 1 - c).wait_recv()
        for cp in sends:
            cp.wait_send()

    outs = _pcall(
        body, name="gather_weights",
        out_shape=tuple(jax.ShapeDtypeStruct((N_CHIPS,) + s.shape, s.dtype) for s in shards) + (_TOKEN,),
        in_specs=[_ANY] * n, out_specs=(_ANY,) * n + (pl.BlockSpec(memory_space=pltpu.VMEM),),
        scratch_shapes=[pltpu.SemaphoreType.DMA((n, 2 * hops)),
                        pltpu.SemaphoreType.DMA((n, 2 * hops))],
    )(*shards)
    return outs[:n], outs[n][0, 0]


def _pair_copies(srcs, lands, send_sems, recv_sems):
    x, y, c = _position()
    copies = []
    for w in range(len(srcs)):
        half = srcs[w].shape[1] // 2
        copies.append(pltpu.make_async_remote_copy(
            src_ref=srcs[w].at[:, pl.ds((1 - c) * half, half), :], dst_ref=lands[w],
            send_sem=send_sems.at[w], recv_sem=recv_sems.at[w],
            device_id=(x, y, 1 - c), device_id_type=MESH))
    return copies


def _chip_copies(srcs, lands, send_sems, recv_sems):
    x, y, c = _position()
    copies = []
    for w in range(len(srcs)):
        for k in range(1, N_CHIPS):
            px, py = _chip_peer(x, y, k)
            i = w * (N_CHIPS - 1) + k - 1
            copies.append(pltpu.make_async_remote_copy(
                src_ref=srcs[w].at[2 * px + py], dst_ref=lands[w].at[k],
                send_sem=send_sems.at[i], recv_sem=recv_sems.at[i],
                device_id=(px, py, c), device_id_type=MESH))
    return copies


def _gather_copies(srcs, lands, send_sems, recv_sems):
    x, y, c = _position()
    me = 2 * x + y
    copies = []
    for w in range(len(srcs)):
        for k in range(1, N_CHIPS):
            px, py = _chip_peer(x, y, k)
            i = w * (N_CHIPS - 1) + k - 1
            copies.append(pltpu.make_async_remote_copy(
                src_ref=srcs[w].at[c], dst_ref=lands[w].at[me, c],
                send_sem=send_sems.at[i], recv_sem=recv_sems.at[i],
                device_id=(px, py, c), device_id_type=MESH))
    return copies


def _exchange(name, copies_of, n_copies, srcs, land_shapes):
    n = len(srcs)

    def body(*refs):
        copies = copies_of(refs[:n], refs[n:2 * n], refs[2 * n], refs[2 * n + 1])
        for cp in copies:
            cp.start()
        for cp in copies:
            cp.wait_recv()
        for cp in copies:
            cp.wait_send()

    return _pcall(
        body, name=name, out_shape=tuple(land_shapes),
        in_specs=[_ANY] * n, out_specs=(_ANY,) * n,
        scratch_shapes=[pltpu.SemaphoreType.DMA((n_copies,)), pltpu.SemaphoreType.DMA((n_copies,))],
    )(*srcs)


_HBM = pl.BlockSpec(memory_space=pltpu.HBM)
_SEM = pl.BlockSpec(memory_space=pltpu.SEMAPHORE)
_DATAFLOW = pltpu.SideEffectType.DATAFLOW_SIDE_EFFECTING


def _exchange_start(name, copies_of, n_copies, srcs, land_shapes):
    n = len(srcs)
    lands = [lax.empty(s.shape, s.dtype) for s in land_shapes]

    def body(*refs):
        for cp in copies_of(refs[:n], refs[n:2 * n], refs[2 * n], refs[2 * n + 1]):
            cp.start()
        refs[-1][...] = jnp.zeros_like(refs[-1])

    through = [pltpu.HBM(a.shape, a.dtype) for a in list(srcs) + lands]
    outs = _pcall(
        body, name=name,
        out_shape=(pltpu.SemaphoreType.DMA((n_copies,)), pltpu.SemaphoreType.DMA((n_copies,)),
                   *through, jax.ShapeDtypeStruct((8, 128), F32)),
        in_specs=[_HBM] * (2 * n),
        out_specs=(_SEM, _SEM, *([_HBM] * (2 * n)), pl.BlockSpec(memory_space=pltpu.VMEM)),
        input_output_aliases={i: 2 + i for i in range(2 * n)},
        compiler_params=pltpu.CompilerParams(has_side_effects=_DATAFLOW),
    )(*[pltpu.with_memory_space_constraint(a, pltpu.HBM) for a in list(srcs) + lands])
    return outs[:-1], outs[-1][0, 0]


def _exchange_wait(name, copies_of, state, after):
    send_sems, recv_sems, through = state[0], state[1], state[2:]
    n = len(through) // 2
    if after.ndim == 0:
        after = jnp.broadcast_to(after, (8, 128))
    after = pltpu.with_memory_space_constraint(after, pltpu.HBM)

    def body(*refs):
        for cp in copies_of(refs[:n], refs[n:2 * n], refs[2 * n], refs[2 * n + 1]):
            cp.wait_send()
            cp.wait_recv()

    outs = _pcall(
        body, name=name,
        out_shape=tuple(pltpu.HBM(a.shape, a.dtype) for a in through),
        in_specs=[_HBM] * (2 * n) + [_SEM, _SEM, _HBM], out_specs=tuple([_HBM] * (2 * n)),
        input_output_aliases={i: i for i in range(2 * n)},
        compiler_params=pltpu.CompilerParams(has_side_effects=_DATAFLOW),
    )(*through, send_sems, recv_sems, after)
    return outs[:n], outs[n:]


def _forward_halves(lands, *, name):
    n = len(lands)
    hops = N_CHIPS - 1

    def body(*refs):
        ins, outs, done = refs[:n], refs[n:2 * n], refs[2 * n]
        send_sems, recv_sems = refs[2 * n + 1], refs[2 * n + 2]
        x, y, c = _position()
        done[...] = jnp.zeros_like(done)
        copies = []
        for w in range(n):
            for k in range(1, N_CHIPS):
                px, py = _chip_peer(x, y, k)
                i = w * hops + k - 1
                copies.append(pltpu.make_async_remote_copy(
                    src_ref=ins[w].at[2 * px + py, c], dst_ref=outs[w].at[2 * px + py, c],
                    send_sem=send_sems.at[i], recv_sem=recv_sems.at[i],
                    device_id=(x, y, 1 - c), device_id_type=MESH))
        for cp in copies:
            cp.start()
        for cp in copies:
            cp.wait_recv()
        for cp in copies:
            cp.wait_send()

    outs = _pcall(
        body, name=name,
        out_shape=tuple(jax.ShapeDtypeStruct(a.shape, a.dtype) for a in lands) + (_TOKEN,),
        in_specs=[_ANY] * n, out_specs=(_ANY,) * n + (pl.BlockSpec(memory_space=pltpu.VMEM),),
        input_output_aliases={i: i for i in range(n)},
        scratch_shapes=[pltpu.SemaphoreType.DMA((n * hops,)), pltpu.SemaphoreType.DMA((n * hops,))],
    )(*lands)
    return outs[:n], outs[n][0, 0]


def _pair_lands(grads):
    return [jax.ShapeDtypeStruct((g.shape[0], g.shape[1] // 2, g.shape[2]), F32) for g in grads]


def _same_lands(parts):
    return [jax.ShapeDtypeStruct(p.shape, p.dtype) for p in parts]


def _pair_gather_halves(halves, *, name):
    n = len(halves)

    def body(*refs):
        srcs, outs = refs[:n], refs[n:2 * n]
        send_sems, recv_sems = refs[2 * n:]
        x, y, c = _position()
        sends = []
        for w in range(n):
            cp = pltpu.make_async_remote_copy(
                src_ref=srcs[w], dst_ref=outs[w],
                send_sem=send_sems.at[w], recv_sem=recv_sems.at[w],
                device_id=(x, y, 1 - c), device_id_type=MESH)
            cp.start()
            sends.append(cp)
        for cp in sends:
            cp.wait_recv()
        for cp in sends:
            cp.wait_send()

    theirs = _pcall(
        body, name=name,
        out_shape=tuple(jax.ShapeDtypeStruct(h.shape, F32) for h in halves),
        in_specs=[_ANY] * n, out_specs=(_ANY,) * n,
        scratch_shapes=[pltpu.SemaphoreType.DMA((n,)), pltpu.SemaphoreType.DMA((n,))],
    )(*halves)
    my_c = lax.axis_index("c")
    whole = []
    for mine, other in zip(halves, theirs):
        both = jnp.stack([other, other])
        both = lax.dynamic_update_index_in_dim(both, mine, my_c, axis=0)
        whole.append(both.reshape(2 * mine.shape[0], mine.shape[1]))
    return whole


def _all_reduce_small(packed, *, name, sum_row0):
    r_dim, c = packed.shape

    def body(src_ref, out_ref, recv_ref, send_sems, recv_sems):
        x, y, c_ = _position()
        me = 4 * x + 2 * y + c_
        recv_ref[0] = src_ref[...]
        sends = []
        for k in range(1, N_DEV):
            peer = (x ^ (k >> 2), y ^ ((k >> 1) & 1), c_ ^ (k & 1))
            cp = pltpu.make_async_remote_copy(
                src_ref=src_ref, dst_ref=recv_ref.at[k],
                send_sem=send_sems.at[k - 1], recv_sem=recv_sems.at[k - 1],
                device_id=peer, device_id_type=MESH)
            cp.start()
            sends.append(cp)
        for cp in sends:
            cp.wait_recv()
        total = recv_ref[me]
        for d in range(1, N_DEV):
            total = total + recv_ref[d ^ me]
        if sum_row0:
            row0 = jnp.sum(total[0:1, :], axis=1, keepdims=True)
            rid = lax.broadcasted_iota(jnp.int32, total.shape, 0)
            total = jnp.where(rid == 0, row0, total)
        out_ref[...] = total
        for cp in sends:
            cp.wait_send()

    return _pcall(
        body, name=name, out_shape=jax.ShapeDtypeStruct((r_dim, c), F32),
        in_specs=[pl.BlockSpec(memory_space=pltpu.VMEM)],
        out_specs=pl.BlockSpec(memory_space=pltpu.VMEM),
        scratch_shapes=[pltpu.VMEM((N_DEV, r_dim, c), F32),
                        pltpu.SemaphoreType.DMA((N_DEV - 1,)), pltpu.SemaphoreType.DMA((N_DEV - 1,))],
    )(packed)


def _pad_lanes(v, width):
    return jnp.pad(v, ((0, 0), (0, width - v.shape[1])))


def _pad_rows(v, rows):
    pad = [(0, 0)] * v.ndim
    pad[-2] = (0, rows - v.shape[-2])
    return jnp.pad(v, pad)


_IN_PROJ_SHARD_ROWS = 1312


def _rows_1024(v):
    flat = v.reshape(-1)
    pad = (-flat.shape[0]) % D_MODEL
    return jnp.pad(flat, (0, pad)).reshape(-1, D_MODEL)


def _local_step(xs, target, pw, w1_0, fetch, reduce_start, reduce_midway,
                conv_w, conv_b, gate_g,
                norm_mix_g, norm_mlp_g, pool_b, pool_scale, ssm_dt_bias, ssm_a_log, ssm_d, final_g):
    bias_r = _pad_lanes(ssm_dt_bias, 128)
    alog_r = _pad_lanes(ssm_a_log, 128)
    bias_x = jnp.repeat(ssm_dt_bias, HEAD_DIM, axis=1)
    alog_x = jnp.repeat(ssm_a_log, HEAD_DIM, axis=1)
    dskip_x = jnp.repeat(ssm_d, HEAD_DIM, axis=1)
    bias_c = ssm_dt_bias.reshape(N_HEADS, 1)
    alog_c = ssm_a_log.reshape(N_HEADS, 1)
    e_mat = _head_lane_matrix()

    g_mix0, g_mix1 = norm_mix_g[0:1], norm_mix_g[1:2]
    g_mlp0, g_mlp1 = norm_mlp_g[0:1], norm_mlp_g[1:2]
    fg = final_g.reshape(1, D_MODEL)

    h1 = _pool_fwd(xs, g_mix0, pw, pool_b, pool_scale)
    hm0 = _rmsnorm_fwd(h1, g_mlp0, name="norm_mlp0")
    u0 = _matmul(hm0, w1_0, "nn", name="mlp0_up", out_dtype=BF16, b_col_shards=True)
    w2_0 = fetch("mlp0_down", u0)
    h2 = _matmul(u0, w2_0, "nn", name="mlp0_down", a_relu2=True, add=h1)

    w_z, w_xbc, w_dt = fetch("in_proj", h2)
    hn1 = _rmsnorm_fwd(h2, g_mix1, name="norm_mix1")
    z = _matmul(hn1, w_z, "nt", name="in_proj_z", out_dtype=BF16)
    xbc = _matmul(hn1, w_xbc, "nt", name="in_proj_xbc", out_dtype=BF16)
    dt_raw = _matmul(hn1, w_dt, "nt", name="in_proj_dt")
    dt_raw_t = dt_raw[:, :N_HEADS].T
    xc = _conv_fwd(xbc, conv_w, conv_b)
    wout, w1_1, w2_1 = fetch("rest", xc)
    y, states = _ssd_fwd(xc, dt_raw, dt_raw_t, bias_x, bias_c, alog_x, alog_c, dskip_x, e_mat)
    yn = _gate_fwd(y, z, gate_g)
    h3 = _matmul(yn, wout, "nn", name="out_proj", add=h2)
    hm1 = _rmsnorm_fwd(h3, g_mlp1, name="norm_mlp1")
    u1 = _matmul(hm1, w1_1, "nn", name="mlp1_up", out_dtype=BF16, b_col_shards=True)
    h4 = _matmul(u1, w2_1, "nn", name="mlp1_down", a_relu2=True, add=h3)

    dh4, dh4_16, small_final = _loss_head(h4, fg, target)

    def mlp_bwd_weights(dh_out16, hm, u, w2_i, tag):
        du = _matmul(dh_out16, w2_i, "nt", name=tag + "_du", out_dtype=BF16, relu2_grad_of=u)
        dw2 = _matmul(u, dh_out16, "tn", name=tag + "_dw2", a_relu2=True)
        dw1 = _matmul(hm, du, "tn", name=tag + "_dw1", out_col_shards=N_CHIPS)
        return du, dw1, dw2.reshape(N_CHIPS, D_FF // N_CHIPS, D_MODEL)

    def mlp_bwd_input(du, dh_out, h_in, w1_i, g_i, tag):
        dhm = _matmul(du, w1_i, "nt", name=tag + "_dhm", b_col_shards=True)
        return _rmsnorm_bwd(dhm, h_in, g_i, dh_out, name=tag + "_norm_bwd")

    du1, dw1_1, dw2_1 = mlp_bwd_weights(dh4_16, hm1, u1, w2_1, "mlp1")
    dh3, dh3_16, dg_mlp1 = mlp_bwd_input(du1, dh4, h3, w1_1, g_mlp1, "mlp1")

    dyn = _matmul(dh3_16, wout, "nt", name="out_proj_dyn")
    dwout = _matmul(yn, dh3_16, "tn", name="out_proj_dw").reshape(N_CHIPS, D_INNER // N_CHIPS, D_MODEL)
    behind = reduce_start("mlp1_out", [dw1_1, dw2_1, dwout])
    dy, dz, dg_gate = _gate_bwd(dyn, y, z, gate_g + behind)
    behind = reduce_midway("mlp1_out", dz)
    dxc, ddt_raw, small_ssd = _ssd_bwd(xc, dt_raw, dt_raw_t, bias_x, bias_c, alog_x, alog_c, dskip_x + behind,
                                       bias_r, alog_r, e_mat, e_mat.T, states, dy)
    dv, dconv = _conv_bwd_act(xbc, dxc, conv_w, conv_b)
    dxbc = _conv_bwd_in(dv, conv_w)
    dhn1 = _matmul(ddt_raw, w_dt, "nn", name="in_proj_dt_dh")
    dhn1 = _matmul(dz, w_z, "nn", name="in_proj_z_dh", add=dhn1)
    dhn1 = _matmul(dxbc, w_xbc, "nn", name="in_proj_xbc_dh", add=dhn1)
    dw_z = _matmul(dz, hn1, "tn", name="in_proj_z_dw")
    dw_xbc = _matmul(dxbc, hn1, "tn", name="in_proj_xbc_dw")
    dw_dt = _matmul(ddt_raw, hn1, "tn", name="in_proj_dt_dw")
    dwin = jnp.concatenate([dw_z, dw_xbc, dw_dt[:N_HEADS]], axis=0)
    dwin = _pad_rows(dwin.reshape(N_CHIPS, IN_PROJ_DIM // N_CHIPS, D_MODEL), _IN_PROJ_SHARD_ROWS)
    behind = reduce_start("in_proj", [dwin])
    dh2, dh2_16, dg_mix1 = _rmsnorm_bwd(dhn1, h2, g_mix1 + behind, dh3, name="norm_mix1_bwd")
    behind = reduce_midway("in_proj", dh2_16)

    du0, dw1_0, dw2_0 = mlp_bwd_weights(dh2_16, hm0, u0, w2_0, "mlp0")
    dh1, _, dg_mlp0 = mlp_bwd_input(du0, dh2, h1, w1_0, g_mlp0 + behind, "mlp0")
    dx, dpw, small_pool = _pool_bwd(xs, g_mix0, pw, pool_b, pool_scale, dh1)
    dpw = jnp.transpose(dpw.reshape(4, N_CHIPS, POOL_GROUP // N_CHIPS, POOL_GROUP), (1, 0, 2, 3))
    dpw = dpw.reshape(N_CHIPS, 4 * (POOL_GROUP // N_CHIPS), POOL_GROUP)

    big = [dpw, dw1_0, dw2_0]
    rows = [
        small_final[1:2],
        small_final[0:1],
        small_pool[0:1], dg_mix1[0:1],
        dg_mlp0[0:1], dg_mlp1[0:1],
        small_pool[1:2], small_pool[2:3],
        _pad_lanes(small_ssd[0:3], D_MODEL),
        _rows_1024(dg_gate[0:1]),
        _rows_1024(dconv[0:CONV_K]),
        _rows_1024(dconv[CONV_K:CONV_K + 1]),
    ]
    return dx, big, rows


def kernel(x, norm_mix_g, norm_mlp_g, pool_w, pool_b, pool_scale, ssm_w_in, ssm_conv_w, ssm_conv_b, ssm_dt_bias, ssm_a_log, ssm_d, ssm_norm_g, ssm_w_out, mlp_w1, mlp_w2, final_g, loss_target, m_norm_mix_g, m_norm_mlp_g, m_pool_w, m_pool_b, m_pool_scale, m_ssm_w_in, m_ssm_conv_w, m_ssm_conv_b, m_ssm_dt_bias, m_ssm_a_log, m_ssm_d, m_ssm_norm_g, m_ssm_w_out, m_mlp_w1, m_mlp_w2, m_final_g, v_norm_mix_g, v_norm_mlp_g, v_pool_w, v_pool_b, v_pool_scale, v_ssm_w_in, v_ssm_conv_w, v_ssm_conv_b, v_ssm_dt_bias, v_ssm_a_log, v_ssm_d, v_ssm_norm_g, v_ssm_w_out, v_mlp_w1, v_mlp_w2, v_final_g):
    t_dim = x.shape[1]
    xs = x[0]
    target = loss_target[0]
    my_x, my_y, my_c = _position()
    my_chip = 2 * my_x + my_y

    def halves(w):
        return w.astype(BF16).reshape((2, w.shape[0] // 2) + w.shape[1:])

    def whole(gathered, own_shard):
        g = lax.dynamic_update_index_in_dim(gathered, own_shard, my_chip, axis=0)
        return g.reshape((N_CHIPS, 2 * g.shape[2]) + g.shape[3:])

    def gather_lands(own):
        return [jax.ShapeDtypeStruct((N_CHIPS,) + s.shape, s.dtype) for s in own]

    vec_cols = CONV_DIM // N_CHIPS
    vec_own = jnp.concatenate([ssm_conv_w[0], ssm_conv_b, _pad_lanes(ssm_norm_g, vec_cols)], axis=0)
    early_own = [halves(pool_w[0]), halves(mlp_w1[0]), vec_own.reshape(2, (CONV_K + 2) // 2, vec_cols)]
    early, behind_early = _all_gather_weights(early_own)
    g_pool, w1_0, g_vec = [whole(g, o) for g, o in zip(early, early_own)]
    pw = jnp.transpose(g_pool, (1, 0, 2, 3)).reshape(4, POOL_GROUP, POOL_GROUP)
    conv_w = jnp.transpose(g_vec[:, 0:CONV_K, :], (1, 0, 2)).reshape(CONV_K, CONV_DIM)
    conv_b = g_vec[:, CONV_K, :].reshape(1, CONV_DIM)
    gate_g = g_vec[:, CONV_K + 1, :D_INNER // N_CHIPS].reshape(1, D_INNER)

    def behind_it(zero, ws):
        return [halves(w + zero) for w in ws]

    down_own = behind_it(behind_early, [mlp_w2[0]])
    fetches = {}
    fetches["mlp0_down"], behind_gather = _exchange_start(
        "gather_mlp0_down_start", _gather_copies, len(down_own) * (N_CHIPS - 1), down_own, gather_lands(down_own))
    in_own = behind_it(behind_gather, [_pad_rows(ssm_w_in[0].T, _IN_PROJ_SHARD_ROWS)])
    fetches["in_proj"], behind_gather = _exchange_start(
        "gather_in_proj_start", _gather_copies, len(in_own) * (N_CHIPS - 1), in_own, gather_lands(in_own))

    def fetch(what, after):
        if what == "mlp0_down":
            own_thru, landed = _exchange_wait("gather_mlp0_down_wait", _gather_copies, fetches[what], after)
            landed, _ = _forward_halves(landed, name="forward_mlp0_down")
            return whole(landed[0], own_thru[0]).reshape(D_FF, D_MODEL)
        if what == "in_proj":
            own_thru, landed = _exchange_wait("gather_in_proj_wait", _gather_copies, fetches["in_proj"], after)
            landed, behind = _forward_halves(landed, name="forward_in_proj")
            rest = behind_it(behind, [ssm_w_out[0], mlp_w1[1], mlp_w2[1]])
            fetches["rest"], behind = _exchange_start(
                "gather_rest_start", _gather_copies, len(rest) * (N_CHIPS - 1), rest, gather_lands(rest))
            win = whole(landed[0], own_thru[0])[:, :IN_PROJ_DIM // N_CHIPS].reshape(IN_PROJ_DIM, D_MODEL)
            w_dt = _pad_rows(win[D_INNER + CONV_DIM:], 128) + behind.astype(BF16)
            return win[:D_INNER], win[D_INNER:D_INNER + CONV_DIM], w_dt
        own_thru, landed = _exchange_wait("gather_rest_wait", _gather_copies, fetches["rest"], after)
        landed, _ = _forward_halves(landed, name="forward_rest")
        g_wout, w1_1, g_w2_1 = [whole(g, o) for g, o in zip(landed, own_thru)]
        return g_wout.reshape(D_INNER, D_MODEL), w1_1, g_w2_1.reshape(D_FF, D_MODEL)

    place = jnp.stack([my_c, my_chip]).astype(jnp.int32)
    waves = {}

    def reduce_start(wave, grads):
        waves[wave] = {}
        waves[wave]["pair"], behind = _exchange_start(
            "pair_%s_start" % wave, _pair_copies, len(grads), grads, _pair_lands(grads))
        return behind

    def reduce_midway(wave, after):
        st = waves[wave]
        grads, recv = _exchange_wait("pair_%s_wait" % wave, _pair_copies, st["pair"], after)
        sums = [_pair_sum(g, r, place, name="pair_sum_%s_%d" % (wave, i))
                for i, (g, r) in enumerate(zip(grads, recv))]
        st["f32"] = [s32 for _, s32 in sums]
        b16 = [s16 for s16, _ in sums]
        st["chip"], behind = _exchange_start(
            "chip_%s_start" % wave, _chip_copies, len(b16) * (N_CHIPS - 1), b16, _same_lands(b16))
        return behind

    def reduce_finish(wave, after):
        st = waves[wave]
        _, got = _exchange_wait("chip_%s_wait" % wave, _chip_copies, st["chip"], after)
        return [_chip_sum(s32, r, name="chip_sum_%s_%d" % (wave, i))
                for i, (s32, r) in enumerate(zip(st["f32"], got))]

    dx, big0, rows = _local_step(xs, target, pw, w1_0, fetch, reduce_start, reduce_midway,
                                 conv_w, conv_b, gate_g,
                                 norm_mix_g + behind_gather, norm_mlp_g, pool_b, pool_scale,
                                 ssm_dt_bias, ssm_a_log, ssm_d, final_g)

    behind = reduce_start("layer0", big0)
    small = jnp.concatenate(rows, axis=0)
    small = jnp.pad(small, ((0, (-small.shape[0]) % 8), (0, 0))) + behind
    small = _all_reduce_small(small, name="all_reduce_small", sum_row0=True)
    behind = reduce_midway("layer0", small)
    h_w1_1, h_w2_1, h_wout = reduce_finish("mlp1_out", behind)
    (h_win,) = reduce_finish("in_proj", behind)
    g_w1_1, g_w2_1, g_wout_s, g_win_s = _pair_gather_halves([h_w1_1, h_w2_1, h_wout, h_win],
                                                            name="pair_gather_layer1")
    loss = small[0, 0]
    g_final = small[1]
    g_norm_mix = small[2:4]
    g_norm_mlp = small[4:6]
    g_pool_b, g_pool_scale = small[6:7], small[7:8]
    g_alog, g_dtb, g_dsk = small[8:9, :N_HEADS], small[9:10, :N_HEADS], small[10:11, :N_HEADS]
    g_gate_full = small[11:13].reshape(1, D_INNER)
    g_convw_full = small[13:25].reshape(CONV_K, CONV_DIM)
    g_convb_full = small[25:28].reshape(1, CONV_DIM)
    g_gate = lax.dynamic_slice_in_dim(g_gate_full, my_chip * (D_INNER // N_CHIPS), D_INNER // N_CHIPS, axis=1)
    g_convw = lax.dynamic_slice_in_dim(g_convw_full, my_chip * (CONV_DIM // N_CHIPS), CONV_DIM // N_CHIPS, axis=1)
    g_convb = lax.dynamic_slice_in_dim(g_convb_full, my_chip * (CONV_DIM // N_CHIPS), CONV_DIM // N_CHIPS, axis=1)

    grads = {
        "norm_mix_g": g_norm_mix, "norm_mlp_g": g_norm_mlp,
        "pool_b": g_pool_b, "pool_scale": g_pool_scale,
        "ssm_conv_w": g_convw.reshape(ssm_conv_w.shape),
        "ssm_conv_b": g_convb, "ssm_dt_bias": g_dtb, "ssm_a_log": g_alog, "ssm_d": g_dsk,
        "ssm_norm_g": g_gate, "ssm_w_out": g_wout_s.reshape(ssm_w_out.shape),
        "final_g": g_final,
    }
    weights = dict(norm_mix_g=norm_mix_g, norm_mlp_g=norm_mlp_g, pool_w=pool_w, pool_b=pool_b,
                   pool_scale=pool_scale, ssm_w_in=ssm_w_in, ssm_conv_w=ssm_conv_w, ssm_conv_b=ssm_conv_b,
                   ssm_dt_bias=ssm_dt_bias, ssm_a_log=ssm_a_log, ssm_d=ssm_d, ssm_norm_g=ssm_norm_g,
                   ssm_w_out=ssm_w_out, mlp_w1=mlp_w1, mlp_w2=mlp_w2, final_g=final_g)
    moms = dict(norm_mix_g=(m_norm_mix_g, v_norm_mix_g), norm_mlp_g=(m_norm_mlp_g, v_norm_mlp_g),
                pool_w=(m_pool_w, v_pool_w), pool_b=(m_pool_b, v_pool_b),
                pool_scale=(m_pool_scale, v_pool_scale), ssm_w_in=(m_ssm_w_in, v_ssm_w_in),
                ssm_conv_w=(m_ssm_conv_w, v_ssm_conv_w), ssm_conv_b=(m_ssm_conv_b, v_ssm_conv_b),
                ssm_dt_bias=(m_ssm_dt_bias, v_ssm_dt_bias), ssm_a_log=(m_ssm_a_log, v_ssm_a_log),
                ssm_d=(m_ssm_d, v_ssm_d), ssm_norm_g=(m_ssm_norm_g, v_ssm_norm_g),
                ssm_w_out=(m_ssm_w_out, v_ssm_w_out), mlp_w1=(m_mlp_w1, v_mlp_w1),
                mlp_w2=(m_mlp_w2, v_mlp_w2), final_g=(m_final_g, v_final_g))
    names = list(weights)
    big_names = ("pool_w", "ssm_w_in", "ssm_w_out", "mlp_w1", "mlp_w2")
    deltas, new_m, new_v = {}, {}, {}

    def as_rows(nm, a):
        return a[0].T if nm == "ssm_w_in" else a.reshape(-1, a.shape[-1])

    def from_rows(nm, r):
        return r.T[None] if nm == "ssm_w_in" else r.reshape(weights[nm].shape)

    def update(nm, grad_rows, layer=None, into=None):
        return _adamw(as_rows(nm, weights[nm]), grad_rows, as_rows(nm, moms[nm][0]), as_rows(nm, moms[nm][1]),
                      name="adamw_%s_%s" % (nm, layer), part=layer, into=into)

    def keep(nm, results):
        deltas[nm], new_m[nm], new_v[nm] = [from_rows(nm, r) for r in results]

    g_win_rows = g_win_s[:IN_PROJ_DIM // N_CHIPS]
    grads["ssm_w_in"] = from_rows("ssm_w_in", g_win_rows)
    keep("ssm_w_in", update("ssm_w_in", g_win_rows))
    keep("ssm_w_out", update("ssm_w_out", g_wout_s))
    w1_done = update("mlp_w1", g_w1_1, layer=1)
    w2_done = update("mlp_w2", g_w2_1, layer=1)
    small_names = [nm for nm in names if nm not in big_names]
    sizes = [weights[nm].size for nm in small_names]

    def pack(parts):
        flat = jnp.concatenate([p.reshape(-1) for p in parts])
        pad = (-flat.shape[0]) % (8 * D_MODEL)
        return jnp.pad(flat, (0, pad)).reshape(-1, D_MODEL)

    d_, m_, v_ = _adamw(pack([weights[nm] for nm in small_names]), pack([grads[nm] for nm in small_names]),
                        pack([moms[nm][0] for nm in small_names]), pack([moms[nm][1] for nm in small_names]),
                        name="adamw_small")
    off = 0
    for nm, sz in zip(small_names, sizes):
        shp = weights[nm].shape
        deltas[nm] = d_.reshape(-1)[off:off + sz].reshape(shp)
        new_m[nm] = m_.reshape(-1)[off:off + sz].reshape(shp)
        new_v[nm] = v_.reshape(-1)[off:off + sz].reshape(shp)
        off += sz

    above = (deltas["ssm_w_in"][0, 0, 0] + deltas["ssm_w_out"][0, 0, 0] + w1_done[0][-1, -1]
             + w2_done[0][-1, -1] + d_[0, 0])
    g_pool_w, g_w1_0, g_w2_0 = _pair_gather_halves(reduce_finish("layer0", above), name="pair_gather_layer0")
    keep("mlp_w1", update("mlp_w1", g_w1_0, layer=0, into=w1_done))
    keep("mlp_w2", update("mlp_w2", g_w2_0, layer=0, into=w2_done))
    grads["pool_w"] = g_pool_w.reshape(pool_w.shape)
    grads["mlp_w1"] = jnp.stack([g_w1_0, g_w1_1])
    grads["mlp_w2"] = jnp.stack([g_w2_0, g_w2_1])
    keep("pool_w", update("pool_w", g_pool_w))

    grad_x = dx.reshape(x.shape)
    out_grads = [grads[nm].reshape(weights[nm].shape) for nm in names]
    return (loss, grad_x, *out_grads, *[deltas[nm] for nm in names],
            *[new_m[nm] for nm in names], *[new_v[nm] for nm in names])
```

```python
import functools

import jax
import jax.numpy as jnp
from jax import lax
from jax.experimental import pallas as pl
from jax.experimental.pallas import tpu as pltpu

F32 = jnp.float32
BF16 = jnp.bfloat16
MESH = pl.DeviceIdType.MESH

D_MODEL = 1024
RMS_EPS = 1e-5
POOL_WINDOWS = (2, 4, 8, 16)
POOL_GROUP = 256
POOL_HALO = 16
D_INNER = 2048
HEAD_DIM = 64
N_HEADS = 32
N_GROUPS = 4
HEADS_PER_GROUP = 8
D_STATE = 128
CONV_K = 4
CONV_HALO = 8
CHUNK = 128
CONV_DIM = 3072
IN_PROJ_DIM = 5152
D_FF = 4096
N_CHIPS = 4
N_DEV = 8

ADAM_LR = 0.001
ADAM_B1 = 0.9
ADAM_B2 = 0.999
ADAM_EPS = 1e-08
ADAM_WD = 0.01
ADAM_STEP = 10

VMEM_LIMIT = 56 * 1024 * 1024
NEG_INF = float("-inf")


def _pcall(body, **kw):
    return pl.pallas_call(body, **kw)


def _params(*sem):
    return pltpu.CompilerParams(dimension_semantics=sem, vmem_limit_bytes=VMEM_LIMIT)


def _sigmoid(v):
    return 1.0 / (1.0 + jnp.exp(-v))


def _row_spec(tb, d, nb=None, reverse=False):
    if reverse:
        return pl.BlockSpec((tb, d), lambda i: (nb - 1 - i, 0))
    return pl.BlockSpec((tb, d), lambda i: (i, 0))


def _const_spec(shape):
    return pl.BlockSpec(shape, lambda *_: tuple(0 for _ in shape))


_DIMS = {"nn": (((1,), (0,)), ((), ())),
         "nt": (((1,), (1,)), ((), ())),
         "tn": (((0,), (0,)), ((), ()))}


_MATMUL_VMEM_BUDGET = 40 * 1024 * 1024


def _matmul_tiles(m_dim, n_dim, k_dim, a_bytes, b_bytes, mn_bytes):
    tm, tn = min(m_dim, 1024), min(n_dim, 1024)
    while 2 * (tm * k_dim * a_bytes + tn * k_dim * b_bytes + tm * tn * mn_bytes) > _MATMUL_VMEM_BUDGET:
        if tm >= tn:
            tm //= 2
        else:
            tn //= 2
    return tm, tn


def _matmul(a, b, mode, *, name, out_dtype=F32, a_relu2=False, add=None, relu2_grad_of=None,
            out_col_shards=1, b_col_shards=False):
    if mode == "tn":
        k_dim, m_dim = a.shape
    else:
        m_dim, k_dim = a.shape
    if b_col_shards:
        n_shards, shard_cols = b.shape[0], b.shape[2]
        n_dim = n_shards * shard_cols if mode == "nn" else b.shape[1]
    else:
        n_dim = b.shape[0] if mode == "nt" else b.shape[1]
    mn_bytes = jnp.dtype(out_dtype).itemsize
    if relu2_grad_of is not None:
        mn_bytes += relu2_grad_of.dtype.itemsize
    if add is not None:
        mn_bytes += add.dtype.itemsize
    tm, tn = _matmul_tiles(m_dim, n_dim, k_dim, a.dtype.itemsize, b.dtype.itemsize, mn_bytes)
    assert m_dim % tm == 0 and n_dim % tn == 0
    a_spec = (pl.BlockSpec((k_dim, tm), lambda i, j: (0, i)) if mode == "tn"
              else pl.BlockSpec((tm, k_dim), lambda i, j: (i, 0)))
    if b_col_shards and mode == "nn":
        assert shard_cols % tn == 0
        per_shard = shard_cols // tn
        b_spec = pl.BlockSpec((None, k_dim, tn), lambda i, j: (j // per_shard, 0, j % per_shard))
    elif b_col_shards:
        assert mode == "nt" and k_dim == n_shards * shard_cols
        b_spec = pl.BlockSpec((n_shards, tn, shard_cols), lambda i, j: (0, j, 0))
    else:
        b_spec = (pl.BlockSpec((tn, k_dim), lambda i, j: (j, 0)) if mode == "nt"
                  else pl.BlockSpec((k_dim, tn), lambda i, j: (0, j)))
    mn_spec = pl.BlockSpec((tm, tn), lambda i, j: (i, j))
    operands, in_specs = [a, b], [a_spec, b_spec]
    if relu2_grad_of is not None:
        operands.append(relu2_grad_of)
        in_specs.append(mn_spec)
    if add is not None:
        operands.append(add)
        in_specs.append(mn_spec)
    if out_col_shards == 1:
        out_shape = jax.ShapeDtypeStruct((m_dim, n_dim), out_dtype)
        out_spec = mn_spec
    else:
        n_shard = n_dim // out_col_shards
        assert n_shard % tn == 0
        per = n_shard // tn
        out_shape = jax.ShapeDtypeStruct((out_col_shards, m_dim, n_shard), out_dtype)
        out_spec = pl.BlockSpec((None, tm, tn), lambda i, j: (j // per, i, j % per))

    def body(*refs):
        a_ref, b_ref, o_ref = refs[0], refs[1], refs[-1]
        av = a_ref[...]
        if a_relu2:
            av = jnp.maximum(av, 0)
            av = av * av
        if b_col_shards and mode == "nt":
            r = None
            for s in range(n_shards):
                part = lax.dot_general(av[:, s * shard_cols:(s + 1) * shard_cols].astype(BF16),
                                       b_ref[s].astype(BF16), _DIMS[mode], preferred_element_type=F32)
                r = part if r is None else r + part
        else:
            r = lax.dot_general(av.astype(BF16), b_ref[...].astype(BF16), _DIMS[mode],
                                preferred_element_type=F32)
        nxt = 2
        if relu2_grad_of is not None:
            r = r * (2.0 * jnp.maximum(refs[nxt][...].astype(F32), 0.0))
            nxt += 1
        if add is not None:
            r = r + refs[nxt][...]
        o_ref[...] = r.astype(out_dtype)

    return _pcall(
        body, name=name, out_shape=out_shape,
        grid=(m_dim // tm, n_dim // tn),
        in_specs=in_specs, out_specs=out_spec,
        compiler_params=_params("parallel", "parallel"),
    )(*operands)


def _rms(x):
    return lax.rsqrt(jnp.mean(x * x, axis=-1, keepdims=True) + RMS_EPS)


def _rmsnorm_fwd(h, g, *, name, tb=512):
    t_dim, d = h.shape

    def body(h_ref, g_ref, o_ref):
        x = h_ref[...]
        o_ref[...] = (x * _rms(x) * g_ref[...]).astype(BF16)

    return _pcall(
        body, name=name, out_shape=jax.ShapeDtypeStruct((t_dim, d), BF16),
        grid=(t_dim // tb,), in_specs=[_row_spec(tb, d), _const_spec((1, d))],
        out_specs=_row_spec(tb, d), compiler_params=_params("parallel"),
    )(h, g)


def _rmsnorm_bwd(dy, h, g, dres, *, name, tb=512):
    t_dim, d = h.shape

    def body(dy_ref, h_ref, g_ref, dres_ref, dh_ref, dh16_ref, dg_ref):
        @pl.when(pl.program_id(0) == 0)
        def _():
            dg_ref[...] = jnp.zeros_like(dg_ref)

        x = h_ref[...]
        r = _rms(x)
        xhat = x * r
        dyv = dy_ref[...]
        dxhat = dyv * g_ref[...]
        dh = dres_ref[...] + r * (dxhat - xhat * jnp.mean(dxhat * xhat, axis=-1, keepdims=True))
        dh_ref[...] = dh
        dh16_ref[...] = dh.astype(BF16)
        dg_ref[0:1, :] += jnp.sum(dyv * xhat, axis=0, keepdims=True)

    return _pcall(
        body, name=name,
        out_shape=(jax.ShapeDtypeStruct((t_dim, d), F32), jax.ShapeDtypeStruct((t_dim, d), BF16),
                   jax.ShapeDtypeStruct((8, d), F32)),
        grid=(t_dim // tb,),
        in_specs=[_row_spec(tb, d), _row_spec(tb, d), _const_spec((1, d)), _row_spec(tb, d)],
        out_specs=(_row_spec(tb, d), _row_spec(tb, d), _const_spec((8, d))),
        compiler_params=_params("arbitrary"),
    )(dy, h, g, dres)


def _pool_mixed(ext, hn, t0, tb):
    t = t0 + lax.broadcasted_iota(jnp.int32, (tb, 1), 0)
    parts = []
    for gi, w in enumerate(POOL_WINDOWS):
        lanes = slice(gi * POOL_GROUP, (gi + 1) * POOL_GROUP)
        s = ext[:, lanes]
        k = 1
        while k < w:
            s = s + pltpu.roll(s, k, 0)
            k *= 2
        cnt = jnp.minimum(t + 1, w).astype(F32)
        parts.append(s[POOL_HALO:, :] / cnt - hn[:, lanes])
    return parts


def _pool_fwd(x, g, pw, pb, ps, *, tb=512):
    t_dim, d = x.shape

    def body(x_ref, g_ref, pw_ref, pb_ref, ps_ref, o_ref, ext_ref):
        i = pl.program_id(0)

        @pl.when(i == 0)
        def _():
            ext_ref[0:POOL_HALO, :] = jnp.zeros((POOL_HALO, d), F32)

        xv = x_ref[...]
        hn = xv * _rms(xv) * g_ref[...]
        ext_ref[POOL_HALO:, :] = hn
        mixed = _pool_mixed(ext_ref[...], hn, i * tb, tb)
        for gi in range(len(POOL_WINDOWS)):
            lanes = slice(gi * POOL_GROUP, (gi + 1) * POOL_GROUP)
            out = jnp.dot(mixed[gi].astype(BF16), pw_ref[gi], preferred_element_type=F32)
            o_ref[:, lanes] = xv[:, lanes] + (out + pb_ref[:, lanes]) * ps_ref[:, lanes]
        ext_ref[0:POOL_HALO, :] = hn[tb - POOL_HALO:, :]

    return _pcall(
        body, name="pool_fwd", out_shape=jax.ShapeDtypeStruct((t_dim, d), F32),
        grid=(t_dim // tb,),
        in_specs=[_row_spec(tb, d), _const_spec((1, d)), _const_spec((4, POOL_GROUP, POOL_GROUP)),
                  _const_spec((1, d)), _const_spec((1, d))],
        out_specs=_row_spec(tb, d),
        scratch_shapes=[pltpu.VMEM((POOL_HALO + tb, d), F32)],
        compiler_params=_params("arbitrary"),
    )(x, g, pw, pb, ps)


def _pool_bwd(x, g, pw, pb, ps, dh1, *, tb=512):
    t_dim, d = x.shape
    nb = t_dim // tb
    halo_per_block = tb // POOL_HALO

    def body(x_ref, xprev_ref, g_ref, pw_ref, pb_ref, ps_ref, dh1_ref,
             dx_ref, dpw_ref, small_ref, ext_ref, dext_ref):
        i = pl.program_id(0)
        blk = nb - 1 - i

        @pl.when(i == 0)
        def _():
            dpw_ref[...] = jnp.zeros_like(dpw_ref)
            small_ref[...] = jnp.zeros_like(small_ref)
            dext_ref[tb:, :] = jnp.zeros((POOL_HALO, d), F32)

        gv = g_ref[...]
        xv = x_ref[...]
        r = _rms(xv)
        xhat = xv * r
        hn = xhat * gv
        xp = xprev_ref[...]
        hprev = xp * _rms(xp) * gv * (blk > 0).astype(F32)
        ext_ref[0:POOL_HALO, :] = hprev
        ext_ref[POOL_HALO:, :] = hn
        mixed = _pool_mixed(ext_ref[...], hn, blk * tb, tb)

        dout = dh1_ref[...]
        t = blk * tb + lax.broadcasted_iota(jnp.int32, (tb, 1), 0)
        for gi, w in enumerate(POOL_WINDOWS):
            lanes = slice(gi * POOL_GROUP, (gi + 1) * POOL_GROUP)
            mb = mixed[gi].astype(BF16)
            pre = jnp.dot(mb, pw_ref[gi], preferred_element_type=F32) + pb_ref[:, lanes]
            dg_out = dout[:, lanes]
            small_ref[2:3, lanes] += jnp.sum(dg_out * pre, axis=0, keepdims=True)
            dpre = dg_out * ps_ref[:, lanes]
            small_ref[1:2, lanes] += jnp.sum(dpre, axis=0, keepdims=True)
            dpb16 = dpre.astype(BF16)
            dpw_ref[gi] += lax.dot_general(mb, dpb16, _DIMS["tn"], preferred_element_type=F32)
            dmixed = lax.dot_general(dpb16, pw_ref[gi], _DIMS["nt"], preferred_element_type=F32)
            cnt = jnp.minimum(t + 1, w).astype(F32)
            dq = dmixed / cnt
            dext_ref[0:tb, lanes] = dq
            s = dext_ref[:, lanes]
            k = 1
            while k < w:
                s = s + pltpu.roll(s, tb + POOL_HALO - k, 0)
                k *= 2
            dhn = s[0:tb, :] - dmixed
            dext_ref[tb:, lanes] = dq[0:POOL_HALO, :]
            small_ref[0:1, lanes] += jnp.sum(dhn * xhat[:, lanes], axis=0, keepdims=True)
            ext_ref[POOL_HALO:, lanes] = dhn * gv[:, lanes]
        dxhat = ext_ref[POOL_HALO:, :]
        dx_ref[...] = dout + r * (dxhat - xhat * jnp.mean(dxhat * xhat, axis=-1, keepdims=True))

    return _pcall(
        body, name="pool_bwd",
        out_shape=(jax.ShapeDtypeStruct((t_dim, d), F32),
                   jax.ShapeDtypeStruct((4, POOL_GROUP, POOL_GROUP), F32),
                   jax.ShapeDtypeStruct((8, d), F32)),
        grid=(nb,),
        in_specs=[_row_spec(tb, d, nb, True),
                  pl.BlockSpec((POOL_HALO, d),
                               lambda i: (jnp.maximum((nb - 1 - i) * halo_per_block - 1, 0), 0)),
                  _const_spec((1, d)), _const_spec((4, POOL_GROUP, POOL_GROUP)),
                  _const_spec((1, d)), _const_spec((1, d)), _row_spec(tb, d, nb, True)],
        out_specs=(_row_spec(tb, d, nb, True), _const_spec((4, POOL_GROUP, POOL_GROUP)),
                   _const_spec((8, d))),
        scratch_shapes=[pltpu.VMEM((POOL_HALO + tb, d), F32), pltpu.VMEM((tb + POOL_HALO, d), F32)],
        compiler_params=_params("arbitrary"),
    )(x, x, g, pw, pb, ps, dh1)


_CONV_CB = 1024
_STRIP = 16


def _strips(tb, fn, unroll=4):
    def step(i, carry):
        fn(pl.multiple_of(i * _STRIP, _STRIP))
        return carry
    lax.fori_loop(0, tb // _STRIP, step, 0, unroll=unroll)


def _conv_taps(ext_ref, r0, w):
    shifted = [ext_ref[CONV_HALO + r0 - sh:CONV_HALO + r0 - sh + _STRIP, :] for sh in range(CONV_K)]
    acc = shifted[0] * w[CONV_K - 1:CONV_K, :]
    for sh in range(1, CONV_K):
        acc = acc + shifted[sh] * w[CONV_K - 1 - sh:CONV_K - sh, :]
    return shifted, acc


def _conv_fwd(u, w, b, *, tb=512):
    t_dim, c = u.shape
    cb = _CONV_CB

    def body(u_ref, w_ref, b_ref, o_ref, ext_ref):
        @pl.when(pl.program_id(1) == 0)
        def _():
            ext_ref[0:CONV_HALO, :] = jnp.zeros((CONV_HALO, cb), F32)

        wv = w_ref[...]
        bv = b_ref[...]

        def fill(r0):
            ext_ref[pl.ds(CONV_HALO + r0, _STRIP), :] = u_ref[pl.ds(r0, _STRIP), :].astype(F32)

        _strips(tb, fill)
        for r0 in range(0, tb, _STRIP):
            v = _conv_taps(ext_ref, r0, wv)[1] + bv
            o_ref[r0:r0 + _STRIP, :] = (v * _sigmoid(v)).astype(BF16)
        ext_ref[0:CONV_HALO, :] = ext_ref[tb:tb + CONV_HALO, :]

    blk = pl.BlockSpec((tb, cb), lambda j, t: (t, j))
    return _pcall(
        body, name="conv_fwd", out_shape=jax.ShapeDtypeStruct((t_dim, c), BF16),
        grid=(c // cb, t_dim // tb),
        in_specs=[blk, pl.BlockSpec((CONV_K, cb), lambda j, t: (0, j)),
                  pl.BlockSpec((1, cb), lambda j, t: (0, j))],
        out_specs=blk,
        scratch_shapes=[pltpu.VMEM((CONV_HALO + tb, cb), F32)],
        compiler_params=_params("parallel", "arbitrary"),
    )(u, w, b)


def _conv_bwd_act(u, dxc, w, b, *, tb=512):
    t_dim, c = u.shape
    cb = _CONV_CB
    half = _STRIP // 2

    def body(u_ref, d_ref, w_ref, b_ref, dv_ref, dwb_ref, ext_ref, acc_ref):
        @pl.when(pl.program_id(1) == 0)
        def _():
            ext_ref[0:CONV_HALO, :] = jnp.zeros((CONV_HALO, cb), F32)
            dwb_ref[...] = jnp.zeros_like(dwb_ref)

        acc_ref[...] = jnp.zeros_like(acc_ref)
        wv = w_ref[...]
        bv = b_ref[...]

        def fill(r0):
            ext_ref[pl.ds(CONV_HALO + r0, _STRIP), :] = u_ref[pl.ds(r0, _STRIP), :].astype(F32)

        _strips(tb, fill)
        for r0 in range(0, tb, _STRIP):
            shifted, v = _conv_taps(ext_ref, r0, wv)
            v = v + bv
            sg = _sigmoid(v)
            dv = d_ref[r0:r0 + _STRIP, :].astype(F32) * (sg * (1.0 + v * (1.0 - sg)))
            dv_ref[r0:r0 + _STRIP, :] = dv.astype(BF16)
            acc_ref[CONV_K] += dv[0:half, :] + dv[half:, :]
            for sh in range(CONV_K):
                p = dv * shifted[sh]
                acc_ref[CONV_K - 1 - sh] += p[0:half, :] + p[half:, :]
        for k in range(CONV_K + 1):
            dwb_ref[k:k + 1, :] += jnp.sum(acc_ref[k], axis=0, keepdims=True)
        ext_ref[0:CONV_HALO, :] = ext_ref[tb:tb + CONV_HALO, :]

    blk = pl.BlockSpec((tb, cb), lambda j, t: (t, j))
    return _pcall(
        body, name="conv_bwd_act",
        out_shape=(jax.ShapeDtypeStruct((t_dim, c), BF16), jax.ShapeDtypeStruct((8, c), F32)),
        grid=(c // cb, t_dim // tb),
        in_specs=[blk, blk, pl.BlockSpec((CONV_K, cb), lambda j, t: (0, j)),
                  pl.BlockSpec((1, cb), lambda j, t: (0, j))],
        out_specs=(blk, pl.BlockSpec((8, cb), lambda j, t: (0, j))),
        scratch_shapes=[pltpu.VMEM((CONV_HALO + tb, cb), F32), pltpu.VMEM((CONV_K + 1, half, cb), F32)],
        compiler_params=_params("parallel", "arbitrary"),
    )(u, dxc, w, b)


def _conv_bwd_in(dv, w, *, tb=512):
    t_dim, c = dv.shape
    cb = _CONV_CB
    nb = t_dim // tb

    def body(dv_ref, w_ref, du_ref, ext_ref):
        @pl.when(pl.program_id(1) == 0)
        def _():
            ext_ref[tb:, :] = jnp.zeros((CONV_HALO, cb), F32)

        wv = w_ref[...]

        def fill(r0):
            ext_ref[pl.ds(r0, _STRIP), :] = dv_ref[pl.ds(r0, _STRIP), :].astype(F32)

        _strips(tb, fill)
        for r0 in range(0, tb, _STRIP):
            acc = ext_ref[r0:r0 + _STRIP, :] * wv[CONV_K - 1:CONV_K, :]
            for sh in range(1, CONV_K):
                acc = acc + ext_ref[r0 + sh:r0 + sh + _STRIP, :] * wv[CONV_K - 1 - sh:CONV_K - sh, :]
            du_ref[r0:r0 + _STRIP, :] = acc.astype(BF16)
        ext_ref[tb:, :] = ext_ref[0:CONV_HALO, :]

    blk = pl.BlockSpec((tb, cb), lambda j, t: (nb - 1 - t, j))
    return _pcall(
        body, name="conv_bwd_in", out_shape=jax.ShapeDtypeStruct((t_dim, c), BF16),
        grid=(c // cb, nb),
        in_specs=[blk, pl.BlockSpec((CONV_K, cb), lambda j, t: (0, j))],
        out_specs=blk,
        scratch_shapes=[pltpu.VMEM((tb + CONV_HALO, cb), F32)],
        compiler_params=_params("parallel", "arbitrary"),
    )(dv, w)


def _softplus(v):
    e = jnp.exp(-jnp.abs(v))
    w = 1.0 + e
    log1p = jnp.where(w == 1.0, e, jnp.log(w) * e / jnp.where(w == 1.0, 1.0, w - 1.0))
    return jnp.maximum(v, 0.0) + log1p


def _cumsum_rows(v):
    row = lax.broadcasted_iota(jnp.int32, v.shape, 0) & (CHUNK - 1)
    k = 1
    while k < CHUNK:
        v = v + jnp.where(row >= k, pltpu.roll(v, k, 0), 0.0)
        k *= 2
    return v


def _cumsum_lanes(v):
    col = lax.broadcasted_iota(jnp.int32, v.shape, 1) & (CHUNK - 1)
    k = 1
    while k < CHUNK:
        v = v + jnp.where(col >= k, pltpu.roll(v, k, 1), 0.0)
        k *= 2
    return v


def _rev_cumsum_rows(v):
    row = lax.broadcasted_iota(jnp.int32, v.shape, 0)
    k = 1
    while k < CHUNK:
        v = v + jnp.where(row < CHUNK - k, pltpu.roll(v, CHUNK - k, 0), 0.0)
        k *= 2
    return v


PAIR = 2 * HEAD_DIM
GROUP_LANES = HEADS_PER_GROUP * HEAD_DIM


def _head_lane_matrix():
    h = lax.broadcasted_iota(jnp.int32, (128, D_INNER), 0)
    j = lax.broadcasted_iota(jnp.int32, (128, D_INNER), 1)
    return (j // HEAD_DIM == h).astype(BF16)


def _split_bf16(v, pieces):
    out = []
    for _ in range(pieces):
        p = v.astype(BF16)
        out.append(p)
        v = v - p.astype(F32)
    return out


_EXACT_PIECES = 3


def _expand_heads(values, e3):
    lhs = jnp.concatenate([jnp.concatenate(_split_bf16(v, _EXACT_PIECES), axis=1) for v in values], axis=0)
    out = jnp.dot(lhs, e3, preferred_element_type=F32)
    rows = values[0].shape[0]
    return [out[i * rows:(i + 1) * rows, :] for i in range(len(values))]


def _reduce_heads(v, et, pieces):
    return sum(jnp.dot(p, et, preferred_element_type=F32) for p in _split_bf16(v, pieces))


def _ssd_decay(dt_raw, dt_raw_t, bias_r, bias_c, alog_r, alog_c, *, tb=1024):
    t_dim = dt_raw.shape[0]

    def body(dtr_ref, dtt_ref, br_ref, bc_ref, ar_ref, ac_ref, dt_ref, acs_ref, sg_ref, acst_ref):
        pre = dtr_ref[...] + br_ref[...]
        dt = _softplus(pre)
        dt_ref[...] = dt
        sg_ref[...] = _sigmoid(pre)
        acs_ref[...] = _cumsum_rows(dt * (-jnp.exp(ar_ref[...])))
        acst_ref[...] = _cumsum_lanes(_softplus(dtt_ref[...] + bc_ref[...]) * (-jnp.exp(ac_ref[...])))

    rows = pl.BlockSpec((tb, 128), lambda i: (i, 0))
    cols = pl.BlockSpec((N_HEADS, tb), lambda i: (0, i))
    sds = jax.ShapeDtypeStruct((t_dim, 128), F32)
    return _pcall(
        body, name="ssd_decay",
        out_shape=(sds, sds, sds, jax.ShapeDtypeStruct((N_HEADS, t_dim), F32)),
        grid=(t_dim // tb,),
        in_specs=[rows, cols, _const_spec((1, 128)), _const_spec((N_HEADS, 1)),
                  _const_spec((1, 128)), _const_spec((N_HEADS, 1))],
        out_specs=(rows, rows, rows, cols), compiler_params=_params("parallel"),
    )(dt_raw, dt_raw_t, bias_r, bias_c, alog_r, alog_c)


def _pair_decay(acs_slab, acs_c, h0, causal, left):
    other = pltpu.roll(acs_slab, HEAD_DIM, 1)
    col0 = jnp.where(left, acs_slab, other)
    col1 = jnp.where(left, other, acs_slab)
    l0 = jnp.exp(jnp.where(causal, col0 - acs_c[h0:h0 + 1, :], NEG_INF))
    l1 = jnp.exp(jnp.where(causal, col1 - acs_c[h0 + 1:h0 + 2, :], NEG_INF))
    return l0, l1


def _ssd_fwd(xc, dt_r, acs_r, acs_t, dskip_x, e3_mat):
    t_dim = xc.shape[0]
    nc = t_dim // CHUNK

    def body(xc_ref, dt_ref, acs_ref, acst_ref, dk_ref, e3_ref, y_ref, st_ref, state):
        @pl.when(pl.program_id(0) == 0)
        def _():
            state[...] = jnp.zeros_like(state)

        dt, acs = _expand_heads([dt_ref[...], acs_ref[...]], e3_ref[...])
        acs_c = acst_ref[...]
        st_ref[0] = state[...]
        last = acs[CHUNK - 1:CHUNK, :]
        xs32 = xc_ref[:, 0:D_INNER].astype(F32)
        xdt = xs32 * dt
        xdt16 = xdt.astype(BF16)
        xdte16 = (xdt * jnp.exp(last - acs)).astype(BF16)
        ea = jnp.exp(acs)
        cd = jnp.exp(last)
        skip = dk_ref[...] * xs32
        causal = (lax.broadcasted_iota(jnp.int32, (CHUNK, CHUNK), 0)
                  >= lax.broadcasted_iota(jnp.int32, (CHUNK, CHUNK), 1))
        left = lax.broadcasted_iota(jnp.int32, (CHUNK, PAIR), 1) < HEAD_DIM
        for g in range(N_GROUPS):
            gl = slice(g * GROUP_LANES, (g + 1) * GROUP_LANES)
            bg = xc_ref[:, D_INNER + g * D_STATE:D_INNER + (g + 1) * D_STATE]
            cg = xc_ref[:, D_INNER + (N_GROUPS + g) * D_STATE:D_INNER + (N_GROUPS + g + 1) * D_STATE]
            cb = lax.dot_general(cg, bg, _DIMS["nt"], preferred_element_type=F32)
            hprev = state[:, gl]
            ch = jnp.dot(cg, hprev.astype(BF16), preferred_element_type=F32)
            for j in range(HEADS_PER_GROUP // 2):
                pl_ = slice(g * GROUP_LANES + j * PAIR, g * GROUP_LANES + (j + 1) * PAIR)
                h0 = g * HEADS_PER_GROUP + 2 * j
                l0, l1 = _pair_decay(acs[:, pl_], acs_c, h0, causal, left)
                lhs = jnp.concatenate([(cb * l0).astype(BF16), (cb * l1).astype(BF16)], axis=1)
                xp = xdt16[:, pl_]
                zero = jnp.zeros_like(xp)
                rhs = jnp.concatenate([jnp.where(left, xp, zero), jnp.where(left, zero, xp)], axis=0)
                ydiag = jnp.dot(lhs, rhs, preferred_element_type=F32)
                y_ref[:, pl_] = ydiag + ch[:, j * PAIR:(j + 1) * PAIR] * ea[:, pl_] + skip[:, pl_]
            s_new = lax.dot_general(bg, xdte16[:, gl], _DIMS["tn"], preferred_element_type=F32)
            state[:, gl] = hprev * cd[:, gl] + s_new

    rows = lambda w: pl.BlockSpec((CHUNK, w), lambda c: (c, 0))
    return _pcall(
        body, name="ssd_fwd",
        out_shape=(jax.ShapeDtypeStruct((t_dim, D_INNER), F32),
                   jax.ShapeDtypeStruct((nc, D_STATE, D_INNER), F32)),
        grid=(nc,),
        in_specs=[rows(CONV_DIM), rows(128), rows(128), pl.BlockSpec((N_HEADS, CHUNK), lambda c: (0, c)),
                  _const_spec((1, D_INNER)), _const_spec((_EXACT_PIECES * 128, D_INNER))],
        out_specs=(rows(D_INNER), pl.BlockSpec((1, D_STATE, D_INNER), lambda c: (c, 0, 0))),
        scratch_shapes=[pltpu.VMEM((D_STATE, D_INNER), F32)],
        compiler_params=_params("arbitrary"),
    )(xc, dt_r, acs_r, acs_t, dskip_x, e3_mat)


def _ssd_bwd(xc, dt_r, acs_r, sg_r, acs_t, alog_r, dskip_x, e3_mat, et_mat, states, dy):
    t_dim = xc.shape[0]
    nc = t_dim // CHUNK

    def body(xc_ref, dt_ref, acs_ref, sg_ref, acst_ref, ar_ref, dk_ref, e3_ref, et_ref, st_ref, dy_ref,
             dxc_ref, ddt_ref, small_ref, dstate, dacs_ref, dxdt_ref, acc_x, acc_r):
        step = pl.program_id(0)

        @pl.when(step == 0)
        def _():
            dstate[...] = jnp.zeros_like(dstate)
            acc_x[...] = jnp.zeros_like(acc_x)
            acc_r[...] = jnp.zeros_like(acc_r)

        dt_r = dt_ref[...]
        a_r = -jnp.exp(ar_ref[...])
        dt, acs = _expand_heads([dt_r, acs_ref[...]], e3_ref[...])
        acs_c = acst_ref[...]
        last = acs[CHUNK - 1:CHUNK, :]
        xs32 = xc_ref[:, 0:D_INNER].astype(F32)
        xdt = xs32 * dt
        xdt16 = xdt.astype(BF16)
        dte = jnp.exp(last - acs)
        xdte = xdt * dte
        xdte16 = xdte.astype(BF16)
        cd = jnp.exp(last)
        dyv = dy_ref[...]
        dy16 = dyv.astype(BF16)
        dye = dyv * jnp.exp(acs)
        dye16 = dye.astype(BF16)
        causal = (lax.broadcasted_iota(jnp.int32, (CHUNK, CHUNK), 0)
                  >= lax.broadcasted_iota(jnp.int32, (CHUNK, CHUNK), 1))
        left = lax.broadcasted_iota(jnp.int32, (CHUNK, PAIR), 1) < HEAD_DIM
        lane_id = lax.broadcasted_iota(jnp.int32, (CHUNK, 128), 1)
        row_id = lax.broadcasted_iota(jnp.int32, (CHUNK, 128), 0)
        is_last_row = lax.broadcasted_iota(jnp.int32, (CHUNK, 1), 0) == CHUNK - 1
        dacs_cols = jnp.zeros((CHUNK, 128), F32)
        dacs_rows = jnp.zeros((CHUNK, 128), F32)
        for g in range(N_GROUPS):
            gl = slice(g * GROUP_LANES, (g + 1) * GROUP_LANES)
            b_lanes = slice(D_INNER + g * D_STATE, D_INNER + (g + 1) * D_STATE)
            c_lanes = slice(D_INNER + (N_GROUPS + g) * D_STATE, D_INNER + (N_GROUPS + g + 1) * D_STATE)
            bg = xc_ref[:, b_lanes]
            cg = xc_ref[:, c_lanes]
            cb = lax.dot_general(cg, bg, _DIMS["nt"], preferred_element_type=F32)
            hprev = st_ref[0, :, gl]
            hp16 = hprev.astype(BF16)
            dhn = dstate[:, gl]
            dhn16 = dhn.astype(BF16)
            ch = jnp.dot(cg, hp16, preferred_element_type=F32)
            gmat = jnp.dot(bg, dhn16, preferred_element_type=F32)
            gx = gmat * xdte[:, gl]
            dlast = jnp.sum(gx, axis=0, keepdims=True) + cd[:, gl] * jnp.sum(dhn * hprev, axis=0, keepdims=True)
            dacs_ref[:, gl] = dye[:, gl] * ch - gx + jnp.where(is_last_row, dlast, 0.0)
            dc_acc = lax.dot_general(dye16[:, gl], hp16, _DIMS["nt"], preferred_element_type=F32)
            db_acc = lax.dot_general(xdte16[:, gl], dhn16, _DIMS["nt"], preferred_element_type=F32)
            dstate[:, gl] = dhn * cd[:, gl] + lax.dot_general(cg, dye16[:, gl], _DIMS["tn"],
                                                             preferred_element_type=F32)
            dcb = jnp.zeros((CHUNK, CHUNK), F32)
            for j in range(HEADS_PER_GROUP // 2):
                pl_ = slice(g * GROUP_LANES + j * PAIR, g * GROUP_LANES + (j + 1) * PAIR)
                h0 = g * HEADS_PER_GROUP + 2 * j
                l0, l1 = _pair_decay(acs[:, pl_], acs_c, h0, causal, left)
                m0, m1 = cb * l0, cb * l1
                lhs = jnp.concatenate([m0.astype(BF16), m1.astype(BF16)], axis=1)
                dyp = dy16[:, pl_]
                zero = jnp.zeros_like(dyp)
                both = lax.dot_general(lhs, dyp, _DIMS["tn"], preferred_element_type=F32)
                dxdt_ref[:, pl_] = (jnp.where(left, both[0:CHUNK, :], both[CHUNK:, :])
                                    + gmat[:, j * PAIR:(j + 1) * PAIR] * dte[:, pl_])
                lhs2 = jnp.concatenate([jnp.where(left, dyp, zero), jnp.where(left, zero, dyp)], axis=0)
                dm = lax.dot_general(lhs2, xdt16[:, pl_], _DIMS["nt"], preferred_element_type=F32)
                dm0, dm1 = dm[0:CHUNK, :], dm[CHUNK:, :]
                dcb = dcb + dm0 * l0 + dm1 * l1
                ds0, ds1 = dm0 * m0, dm1 * m1
                dacs_cols = jnp.where(lane_id == h0, jnp.sum(ds0, axis=1, keepdims=True), dacs_cols)
                dacs_cols = jnp.where(lane_id == h0 + 1, jnp.sum(ds1, axis=1, keepdims=True), dacs_cols)
                dacs_rows = jnp.where(row_id == h0, jnp.sum(ds0, axis=0, keepdims=True), dacs_rows)
                dacs_rows = jnp.where(row_id == h0 + 1, jnp.sum(ds1, axis=0, keepdims=True), dacs_rows)
            dcb16 = dcb.astype(BF16)
            dxc_ref[:, c_lanes] = (dc_acc + jnp.dot(dcb16, bg, preferred_element_type=F32)).astype(BF16)
            dxc_ref[:, b_lanes] = (db_acc + lax.dot_general(dcb16, cg, _DIMS["tn"],
                                                           preferred_element_type=F32)).astype(BF16)
        dxdt = dxdt_ref[...]
        dxc_ref[:, 0:D_INNER] = (dxdt * dt + dk_ref[...] * dyv).astype(BF16)
        acc_x[0:1, :] += jnp.sum(dyv * xs32, axis=0, keepdims=True)
        et = et_ref[...]
        dacs = _reduce_heads(dacs_ref[...], et, 2) + dacs_cols - dacs_rows.T
        dadt = _rev_cumsum_rows(dacs)
        ddraw = (_reduce_heads(dxdt * xs32, et, 1) + dadt * a_r) * sg_ref[...]
        ddraw = jnp.where(lane_id < N_HEADS, ddraw, 0.0)
        ddt_ref[...] = ddraw
        acc_r[0:1, :] += jnp.where(lane_id[0:1, :] < N_HEADS,
                                   jnp.sum(dadt * dt_r, axis=0, keepdims=True) * a_r, 0.0)
        acc_r[1:2, :] += jnp.sum(ddraw, axis=0, keepdims=True)

        @pl.when(step == nc - 1)
        def _():
            dd = _reduce_heads(acc_x[...], et_ref[...], 3)
            rid = lax.broadcasted_iota(jnp.int32, (8, 128), 0)
            small_ref[...] = acc_r[...] + jnp.where(rid == 2, pltpu.roll(dd, 2, 0), 0.0)

    rev = lambda w: pl.BlockSpec((CHUNK, w), lambda c: (nc - 1 - c, 0))
    return _pcall(
        body, name="ssd_bwd",
        out_shape=(jax.ShapeDtypeStruct((t_dim, CONV_DIM), BF16),
                   jax.ShapeDtypeStruct((t_dim, 128), F32),
                   jax.ShapeDtypeStruct((8, 128), F32)),
        grid=(nc,),
        in_specs=[rev(CONV_DIM), rev(128), rev(128), rev(128),
                  pl.BlockSpec((N_HEADS, CHUNK), lambda c: (0, nc - 1 - c)),
                  _const_spec((1, 128)), _const_spec((1, D_INNER)),
                  _const_spec((_EXACT_PIECES * 128, D_INNER)), _const_spec((D_INNER, 128)),
                  pl.BlockSpec((1, D_STATE, D_INNER), lambda c: (nc - 1 - c, 0, 0)),
                  rev(D_INNER)],
        out_specs=(rev(CONV_DIM), rev(128), _const_spec((8, 128))),
        scratch_shapes=[pltpu.VMEM((D_STATE, D_INNER), F32), pltpu.VMEM((CHUNK, D_INNER), F32),
                        pltpu.VMEM((CHUNK, D_INNER), F32), pltpu.VMEM((8, D_INNER), F32),
                        pltpu.VMEM((8, 128), F32)],
        compiler_params=_params("arbitrary"),
    )(xc, dt_r, acs_r, sg_r, acs_t, alog_r, dskip_x, e3_mat, et_mat, states, dy)


_GATE_GROUP = D_INNER // N_GROUPS


def _gate_fwd(y, z, g, *, tb=256):
    t_dim = y.shape[0]

    def body(y_ref, z_ref, g_ref, o_ref):
        for gi in range(N_GROUPS):
            lanes = slice(gi * _GATE_GROUP, (gi + 1) * _GATE_GROUP)
            zv = z_ref[:, lanes].astype(F32)
            wv = y_ref[:, lanes] * (zv * _sigmoid(zv))
            o_ref[:, lanes] = (wv * _rms(wv) * g_ref[:, lanes]).astype(BF16)

    return _pcall(
        body, name="gate_fwd", out_shape=jax.ShapeDtypeStruct((t_dim, D_INNER), BF16),
        grid=(t_dim // tb,),
        in_specs=[_row_spec(tb, D_INNER), _row_spec(tb, D_INNER), _const_spec((1, D_INNER))],
        out_specs=_row_spec(tb, D_INNER), compiler_params=_params("parallel"),
    )(y, z, g)


def _gate_bwd(dyn, y, z, g, *, tb=256):
    t_dim = y.shape[0]

    def body(d_ref, y_ref, z_ref, g_ref, dy_ref, dz_ref, dg_ref):
        @pl.when(pl.program_id(0) == 0)
        def _():
            dg_ref[...] = jnp.zeros_like(dg_ref)

        for gi in range(N_GROUPS):
            lanes = slice(gi * _GATE_GROUP, (gi + 1) * _GATE_GROUP)
            zv = z_ref[:, lanes].astype(F32)
            sg = _sigmoid(zv)
            sz = zv * sg
            yv = y_ref[:, lanes]
            wv = yv * sz
            r = _rms(wv)
            what = wv * r
            dv = d_ref[:, lanes]
            dwhat = dv * g_ref[:, lanes]
            dw = r * (dwhat - what * jnp.mean(dwhat * what, axis=-1, keepdims=True))
            dg_ref[0:1, lanes] += jnp.sum(dv * what, axis=0, keepdims=True)
            dy_ref[:, lanes] = dw * sz
            dz_ref[:, lanes] = (dw * yv * (sg * (1.0 + zv * (1.0 - sg)))).astype(BF16)

    return _pcall(
        body, name="gate_bwd",
        out_shape=(jax.ShapeDtypeStruct((t_dim, D_INNER), F32),
                   jax.ShapeDtypeStruct((t_dim, D_INNER), BF16),
                   jax.ShapeDtypeStruct((8, D_INNER), F32)),
        grid=(t_dim // tb,),
        in_specs=[_row_spec(tb, D_INNER), _row_spec(tb, D_INNER), _row_spec(tb, D_INNER),
                  _const_spec((1, D_INNER))],
        out_specs=(_row_spec(tb, D_INNER), _row_spec(tb, D_INNER), _const_spec((8, D_INNER))),
        compiler_params=_params("arbitrary"),
    )(dyn, y, z, g)


def _loss_head(h, g, target, *, tb=512):
    t_dim, d = h.shape

    def body(h_ref, g_ref, t_ref, dh_ref, dh16_ref, small_ref):
        @pl.when(pl.program_id(0) == 0)
        def _():
            small_ref[...] = jnp.zeros_like(small_ref)

        x = h_ref[...]
        r = _rms(x)
        xhat = x * r
        gv = g_ref[...]
        err = xhat * gv - t_ref[...]
        small_ref[1:2, :] += (0.5 / d) * jnp.sum(err * err, axis=0, keepdims=True)
        dyv = err * (1.0 / d)
        dxhat = dyv * gv
        dh = r * (dxhat - xhat * jnp.mean(dxhat * xhat, axis=-1, keepdims=True))
        dh_ref[...] = dh
        dh16_ref[...] = dh.astype(BF16)
        small_ref[0:1, :] += jnp.sum(dyv * xhat, axis=0, keepdims=True)

    return _pcall(
        body, name="loss_head",
        out_shape=(jax.ShapeDtypeStruct((t_dim, d), F32), jax.ShapeDtypeStruct((t_dim, d), BF16),
                   jax.ShapeDtypeStruct((8, d), F32)),
        grid=(t_dim // tb,),
        in_specs=[_row_spec(tb, d), _const_spec((1, d)), _row_spec(tb, d)],
        out_specs=(_row_spec(tb, d), _row_spec(tb, d), _const_spec((8, d))),
        compiler_params=_params("arbitrary"),
    )(h, g, target)


def _adamw(w, g, m, v, *, name, part=None, into=None):
    r_dim, c = w.shape
    rows = r_dim if part is None else r_dim // 2
    assert g.shape == (rows, c)
    tb = max(t for t in range(8, 513, 8) if rows % t == 0)
    first = 0 if part is None else part * (rows // tb)
    c1 = 1.0 / (1.0 - ADAM_B1 ** ADAM_STEP)
    c2 = 1.0 / (1.0 - ADAM_B2 ** ADAM_STEP)

    def body(w_ref, g_ref, m_ref, v_ref, *rest):
        d_ref, mo_ref, vo_ref = rest[-3:]
        gv = g_ref[...]
        mn = ADAM_B1 * m_ref[...] + (1.0 - ADAM_B1) * gv
        vn = ADAM_B2 * v_ref[...] + (1.0 - ADAM_B2) * (gv * gv)
        mo_ref[...] = mn
        vo_ref[...] = vn
        d_ref[...] = -ADAM_LR * ((mn * c1) / (jnp.sqrt(vn * c2) + ADAM_EPS) + ADAM_WD * w_ref[...])

    spec = pl.BlockSpec((tb, c), lambda i: (first + i, 0))
    sds = jax.ShapeDtypeStruct((r_dim, c), F32)
    in_specs = [spec, _row_spec(tb, c), spec, spec]
    operands = [w, g, m, v]
    aliases = {}
    if into is not None:
        in_specs += [_ANY] * 3
        operands += list(into)
        aliases = {4: 0, 5: 1, 6: 2}
    return _pcall(
        body, name=name, out_shape=(sds, sds, sds), grid=(rows // tb,),
        in_specs=in_specs, out_specs=(spec,) * 3, input_output_aliases=aliases,
        compiler_params=_params("parallel"),
    )(*operands)


def _pair_sum(grad, recv, place, *, name):
    s_dim, r_dim, c = grad.shape
    half = r_dim // 2
    tb = 256 if half % 256 == 0 else half
    per_half = half // tb

    def body(place_ref, a_ref, b_ref, o16_ref, o32_ref):
        s = a_ref[...] + b_ref[...]
        o16_ref[...] = s.astype(BF16)

        @pl.when(pl.program_id(1) == place_ref[1])
        def _():
            o32_ref[...] = s[0]

    grid_spec = pltpu.PrefetchScalarGridSpec(
        num_scalar_prefetch=1, grid=(per_half, s_dim),
        in_specs=[pl.BlockSpec((1, tb, c), lambda i, s, p: (s, p[0] * per_half + i, 0)),
                  pl.BlockSpec((1, tb, c), lambda i, s, p: (s, i, 0))],
        out_specs=(pl.BlockSpec((1, tb, c), lambda i, s, p: (s, i, 0)),
                   pl.BlockSpec((tb, c), lambda i, s, p: (i, 0))))
    return _pcall(
        body, name=name, grid_spec=grid_spec,
        out_shape=(jax.ShapeDtypeStruct((s_dim, half, c), BF16), jax.ShapeDtypeStruct((half, c), F32)),
        compiler_params=_params("parallel", "arbitrary"),
    )(place, grad, recv)


def _chip_sum(own, recv, *, name):
    r_dim, c = own.shape
    tb = 256 if r_dim % 256 == 0 else r_dim

    def body(a_ref, b_ref, o_ref):
        s = a_ref[...]
        for k in range(1, N_CHIPS):
            s = s + b_ref[k].astype(F32)
        o_ref[...] = s

    return _pcall(
        body, name=name, out_shape=jax.ShapeDtypeStruct((r_dim, c), F32),
        grid=(r_dim // tb,),
        in_specs=[_row_spec(tb, c), pl.BlockSpec((N_CHIPS, tb, c), lambda i: (0, i, 0))],
        out_specs=_row_spec(tb, c), compiler_params=_params("parallel"),
    )(own, recv)


def _position():
    return lax.axis_index("x"), lax.axis_index("y"), lax.axis_index("c")


def _chip_peer(x, y, k):
    return x ^ (k >> 1), y ^ (k & 1)


_ANY = pl.BlockSpec(memory_space=pl.ANY)
_TOKEN = jax.ShapeDtypeStruct((8, 128), F32)


def _all_gather_weights(shards):
    n = len(shards)
    hops = N_CHIPS - 1

    def body(*refs):
        srcs, outs, done = refs[:n], refs[n:2 * n], refs[2 * n]
        send_sems, recv_sems = refs[2 * n + 1:]
        x, y, c = _position()
        me = 2 * x + y
        done[...] = jnp.zeros_like(done)

        def over_ici(w, k, chip, to):
            return pltpu.make_async_remote_copy(
                src_ref=srcs[w].at[c], dst_ref=outs[w].at[chip, c],
                send_sem=send_sems.at[w, k - 1], recv_sem=recv_sems.at[w, k - 1],
                device_id=to, device_id_type=MESH)

        def over_d2d(w, k, chip, half):
            return pltpu.make_async_remote_copy(
                src_ref=outs[w].at[chip, half], dst_ref=outs[w].at[chip, half],
                send_sem=send_sems.at[w, hops + k - 1], recv_sem=recv_sems.at[w, hops + k - 1],
                device_id=(x, y, 1 - c), device_id_type=MESH)

        sends = []
        for w in range(n):
            for k in range(1, N_CHIPS):
                px, py = _chip_peer(x, y, k)
                cp = over_ici(w, k, me, (px, py, c))
                cp.start()
                sends.append(cp)
        for w in range(n):
            for k in range(1, N_CHIPS):
                px, py = _chip_peer(x, y, k)
                over_ici(w, k, 2 * px + py, (px, py, c)).wait_recv()
                cp = over_d2d(w, k, 2 * px + py, c)
                cp.start()
                sends.append(cp)
        for w in range(n):
            for k in range(1, N_CHIPS):
                px, py = _chip_peer(x, y, k)
                over_d2d(w, k, 2 * px + py, 1 - c).wait_recv()
        for cp in sends:
            cp.wait_send()

    outs = _pcall(
        body, name="gather_weights",
        out_shape=tuple(jax.ShapeDtypeStruct((N_CHIPS,) + s.shape, s.dtype) for s in shards) + (_TOKEN,),
        in_specs=[_ANY] * n, out_specs=(_ANY,) * n + (pl.BlockSpec(memory_space=pltpu.VMEM),),
        scratch_shapes=[pltpu.SemaphoreType.DMA((n, 2 * hops)),
                        pltpu.SemaphoreType.DMA((n, 2 * hops))],
    )(*shards)
    return outs[:n], outs[n][0, 0]


def _pair_copies(srcs, lands, send_sems, recv_sems):
    x, y, c = _position()
    copies = []
    for w in range(len(srcs)):
        half = srcs[w].shape[1] // 2
        copies.append(pltpu.make_async_remote_copy(
            src_ref=srcs[w].at[:, pl.ds((1 - c) * half, half), :], dst_ref=lands[w],
            send_sem=send_sems.at[w], recv_sem=recv_sems.at[w],
            device_id=(x, y, 1 - c), device_id_type=MESH))
    return copies


def _chip_copies(srcs, lands, send_sems, recv_sems):
    x, y, c = _position()
    copies = []
    for w in range(len(srcs)):
        for k in range(1, N_CHIPS):
            px, py = _chip_peer(x, y, k)
            i = w * (N_CHIPS - 1) + k - 1
            copies.append(pltpu.make_async_remote_copy(
                src_ref=srcs[w].at[2 * px + py], dst_ref=lands[w].at[k],
                send_sem=send_sems.at[i], recv_sem=recv_sems.at[i],
                device_id=(px, py, c), device_id_type=MESH))
    return copies


def _gather_copies(srcs, lands, send_sems, recv_sems):
    x, y, c = _position()
    me = 2 * x + y
    copies = []
    for w in range(len(srcs)):
        for k in range(1, N_CHIPS):
            px, py = _chip_peer(x, y, k)
            i = w * (N_CHIPS - 1) + k - 1
            copies.append(pltpu.make_async_remote_copy(
                src_ref=srcs[w].at[c], dst_ref=lands[w].at[me, c],
                send_sem=send_sems.at[i], recv_sem=recv_sems.at[i],
                device_id=(px, py, c), device_id_type=MESH))
    return copies


def _exchange(name, copies_of, n_copies, srcs, land_shapes):
    n = len(srcs)

    def body(*refs):
        copies = copies_of(refs[:n], refs[n:2 * n], refs[2 * n], refs[2 * n + 1])
        for cp in copies:
            cp.start()
        for cp in copies:
            cp.wait_recv()
        for cp in copies:
            cp.wait_send()

    return _pcall(
        body, name=name, out_shape=tuple(land_shapes),
        in_specs=[_ANY] * n, out_specs=(_ANY,) * n,
        scratch_shapes=[pltpu.SemaphoreType.DMA((n_copies,)), pltpu.SemaphoreType.DMA((n_copies,))],
    )(*srcs)


_HBM = pl.BlockSpec(memory_space=pltpu.HBM)
_SEM = pl.BlockSpec(memory_space=pltpu.SEMAPHORE)
_DATAFLOW = pltpu.SideEffectType.DATAFLOW_SIDE_EFFECTING


def _exchange_start(name, copies_of, n_copies, srcs, land_shapes):
    n = len(srcs)
    lands = [lax.empty(s.shape, s.dtype) for s in land_shapes]

    def body(*refs):
        for cp in copies_of(refs[:n], refs[n:2 * n], refs[2 * n], refs[2 * n + 1]):
            cp.start()
        refs[-1][...] = jnp.zeros_like(refs[-1])

    through = [pltpu.HBM(a.shape, a.dtype) for a in list(srcs) + lands]
    outs = _pcall(
        body, name=name,
        out_shape=(pltpu.SemaphoreType.DMA((n_copies,)), pltpu.SemaphoreType.DMA((n_copies,)),
                   *through, jax.ShapeDtypeStruct((8, 128), F32)),
        in_specs=[_HBM] * (2 * n),
        out_specs=(_SEM, _SEM, *([_HBM] * (2 * n)), pl.BlockSpec(memory_space=pltpu.VMEM)),
        input_output_aliases={i: 2 + i for i in range(2 * n)},
        compiler_params=pltpu.CompilerParams(has_side_effects=_DATAFLOW),
    )(*[pltpu.with_memory_space_constraint(a, pltpu.HBM) for a in list(srcs) + lands])
    return outs[:-1], outs[-1][0, 0]


def _exchange_wait(name, copies_of, state, after):
    send_sems, recv_sems, through = state[0], state[1], state[2:]
    n = len(through) // 2
    if after.ndim == 0:
        after = jnp.broadcast_to(after, (8, 128))
    after = pltpu.with_memory_space_constraint(after, pltpu.HBM)

    def body(*refs):
        for cp in copies_of(refs[:n], refs[n:2 * n], refs[2 * n], refs[2 * n + 1]):
            cp.wait_send()
            cp.wait_recv()

    outs = _pcall(
        body, name=name,
        out_shape=tuple(pltpu.HBM(a.shape, a.dtype) for a in through),
        in_specs=[_HBM] * (2 * n) + [_SEM, _SEM, _HBM], out_specs=tuple([_HBM] * (2 * n)),
        input_output_aliases={i: i for i in range(2 * n)},
        compiler_params=pltpu.CompilerParams(has_side_effects=_DATAFLOW),
    )(*through, send_sems, recv_sems, after)
    return outs[:n], outs[n:]


def _forward_halves(lands, *, name):
    n = len(lands)
    hops = N_CHIPS - 1

    def body(*refs):
        ins, outs, done = refs[:n], refs[n:2 * n], refs[2 * n]
        send_sems, recv_sems = refs[2 * n + 1], refs[2 * n + 2]
        x, y, c = _position()
        done[...] = jnp.zeros_like(done)
        copies = []
        for w in range(n):
            for k in range(1, N_CHIPS):
                px, py = _chip_peer(x, y, k)
                i = w * hops + k - 1
                copies.append(pltpu.make_async_remote_copy(
                    src_ref=ins[w].at[2 * px + py, c], dst_ref=outs[w].at[2 * px + py, c],
                    send_sem=send_sems.at[i], recv_sem=recv_sems.at[i],
                    device_id=(x, y, 1 - c), device_id_type=MESH))
        for cp in copies:
            cp.start()
        for cp in copies:
            cp.wait_recv()
        for cp in copies:
            cp.wait_send()

    outs = _pcall(
        body, name=name,
        out_shape=tuple(jax.ShapeDtypeStruct(a.shape, a.dtype) for a in lands) + (_TOKEN,),
        in_specs=[_ANY] * n, out_specs=(_ANY,) * n + (pl.BlockSpec(memory_space=pltpu.VMEM),),
        input_output_aliases={i: i for i in range(n)},
        scratch_shapes=[pltpu.SemaphoreType.DMA((n * hops,)), pltpu.SemaphoreType.DMA((n * hops,))],
    )(*lands)
    return outs[:n], outs[n][0, 0]


def _pair_lands(grads):
    return [jax.ShapeDtypeStruct((g.shape[0], g.shape[1] // 2, g.shape[2]), F32) for g in grads]


def _same_lands(parts):
    return [jax.ShapeDtypeStruct(p.shape, p.dtype) for p in parts]


def _pair_gather_halves(halves, *, name):
    n = len(halves)

    def body(*refs):
        srcs, outs = refs[:n], refs[n:2 * n]
        send_sems, recv_sems = refs[2 * n:]
        x, y, c = _position()
        sends = []
        for w in range(n):
            cp = pltpu.make_async_remote_copy(
                src_ref=srcs[w], dst_ref=outs[w],
                send_sem=send_sems.at[w], recv_sem=recv_sems.at[w],
                device_id=(x, y, 1 - c), device_id_type=MESH)
            cp.start()
            sends.append(cp)
        for cp in sends:
            cp.wait_recv()
        for cp in sends:
            cp.wait_send()

    theirs = _pcall(
        body, name=name,
        out_shape=tuple(jax.ShapeDtypeStruct(h.shape, F32) for h in halves),
        in_specs=[_ANY] * n, out_specs=(_ANY,) * n,
        scratch_shapes=[pltpu.SemaphoreType.DMA((n,)), pltpu.SemaphoreType.DMA((n,))],
    )(*halves)
    my_c = lax.axis_index("c")
    whole = []
    for mine, other in zip(halves, theirs):
        both = jnp.stack([other, other])
        both = lax.dynamic_update_index_in_dim(both, mine, my_c, axis=0)
        whole.append(both.reshape(2 * mine.shape[0], mine.shape[1]))
    return whole


def _all_reduce_small(packed, *, name, sum_row0):
    r_dim, c = packed.shape

    def body(src_ref, out_ref, recv_ref, send_sems, recv_sems):
        x, y, c_ = _position()
        me = 4 * x + 2 * y + c_
        recv_ref[0] = src_ref[...]
        sends = []
        for k in range(1, N_DEV):
            peer = (x ^ (k >> 2), y ^ ((k >> 1) & 1), c_ ^ (k & 1))
            cp = pltpu.make_async_remote_copy(
                src_ref=src_ref, dst_ref=recv_ref.at[k],
                send_sem=send_sems.at[k - 1], recv_sem=recv_sems.at[k - 1],
                device_id=peer, device_id_type=MESH)
            cp.start()
            sends.append(cp)
        for cp in sends:
            cp.wait_recv()
        total = recv_ref[me]
        for d in range(1, N_DEV):
            total = total + recv_ref[d ^ me]
        if sum_row0:
            row0 = jnp.sum(total[0:1, :], axis=1, keepdims=True)
            rid = lax.broadcasted_iota(jnp.int32, total.shape, 0)
            total = jnp.where(rid == 0, row0, total)
        out_ref[...] = total
        for cp in sends:
            cp.wait_send()

    return _pcall(
        body, name=name, out_shape=jax.ShapeDtypeStruct((r_dim, c), F32),
        in_specs=[pl.BlockSpec(memory_space=pltpu.VMEM)],
        out_specs=pl.BlockSpec(memory_space=pltpu.VMEM),
        scratch_shapes=[pltpu.VMEM((N_DEV, r_dim, c), F32),
                        pltpu.SemaphoreType.DMA((N_DEV - 1,)), pltpu.SemaphoreType.DMA((N_DEV - 1,))],
    )(packed)


def _pad_lanes(v, width):
    return jnp.pad(v, ((0, 0), (0, width - v.shape[1])))


def _pad_rows(v, rows):
    pad = [(0, 0)] * v.ndim
    pad[-2] = (0, rows - v.shape[-2])
    return jnp.pad(v, pad)


_IN_PROJ_SHARD_ROWS = 1312


def _rows_1024(v):
    flat = v.reshape(-1)
    pad = (-flat.shape[0]) % D_MODEL
    return jnp.pad(flat, (0, pad)).reshape(-1, D_MODEL)


def _local_step(xs, target, pw, w1_0, fetch, reduce_start, reduce_midway,
                conv_w, conv_b, gate_g,
                norm_mix_g, norm_mlp_g, pool_b, pool_scale, ssm_dt_bias, ssm_a_log, ssm_d, final_g):
    bias_r = _pad_lanes(ssm_dt_bias, 128)
    alog_r = _pad_lanes(ssm_a_log, 128)
    dskip_x = jnp.repeat(ssm_d, HEAD_DIM, axis=1)
    bias_c = ssm_dt_bias.reshape(N_HEADS, 1)
    alog_c = ssm_a_log.reshape(N_HEADS, 1)
    e_mat = _head_lane_matrix()
    e3_mat = jnp.tile(e_mat, (_EXACT_PIECES, 1))

    g_mix0, g_mix1 = norm_mix_g[0:1], norm_mix_g[1:2]
    g_mlp0, g_mlp1 = norm_mlp_g[0:1], norm_mlp_g[1:2]
    fg = final_g.reshape(1, D_MODEL)

    h1 = _pool_fwd(xs, g_mix0, pw, pool_b, pool_scale)
    hm0 = _rmsnorm_fwd(h1, g_mlp0, name="norm_mlp0")
    u0 = _matmul(hm0, w1_0, "nn", name="mlp0_up", out_dtype=BF16, b_col_shards=True)
    w2_0 = fetch("mlp0_down", u0)
    h2 = _matmul(u0, w2_0, "nn", name="mlp0_down", a_relu2=True, add=h1)

    w_z, w_xbc, w_dt = fetch("in_proj", h2)
    hn1 = _rmsnorm_fwd(h2, g_mix1, name="norm_mix1")
    z = _matmul(hn1, w_z, "nt", name="in_proj_z", out_dtype=BF16)
    xbc = _matmul(hn1, w_xbc, "nt", name="in_proj_xbc", out_dtype=BF16)
    dt_raw = _matmul(hn1, w_dt, "nt", name="in_proj_dt")
    dt_raw_t = dt_raw[:, :N_HEADS].T
    xc = _conv_fwd(xbc, conv_w, conv_b)
    wout, w1_1, w2_1 = fetch("rest", xc)
    dt_r, acs_r, sg_r, acs_t = _ssd_decay(dt_raw, dt_raw_t, bias_r, bias_c, alog_r, alog_c)
    y, states = _ssd_fwd(xc, dt_r, acs_r, acs_t, dskip_x, e3_mat)
    yn = _gate_fwd(y, z, gate_g)
    h3 = _matmul(yn, wout, "nn", name="out_proj", add=h2)
    hm1 = _rmsnorm_fwd(h3, g_mlp1, name="norm_mlp1")
    u1 = _matmul(hm1, w1_1, "nn", name="mlp1_up", out_dtype=BF16, b_col_shards=True)
    h4 = _matmul(u1, w2_1, "nn", name="mlp1_down", a_relu2=True, add=h3)

    dh4, dh4_16, small_final = _loss_head(h4, fg, target)

    def mlp_bwd_weights(dh_out16, hm, u, w2_i, tag):
        du = _matmul(dh_out16, w2_i, "nt", name=tag + "_du", out_dtype=BF16, relu2_grad_of=u)
        dw2 = _matmul(u, dh_out16, "tn", name=tag + "_dw2", a_relu2=True)
        dw1 = _matmul(hm, du, "tn", name=tag + "_dw1", out_col_shards=N_CHIPS)
        return du, dw1, dw2.reshape(N_CHIPS, D_FF // N_CHIPS, D_MODEL)

    def mlp_bwd_input(du, dh_out, h_in, w1_i, g_i, tag):
        dhm = _matmul(du, w1_i, "nt", name=tag + "_dhm", b_col_shards=True)
        return _rmsnorm_bwd(dhm, h_in, g_i, dh_out, name=tag + "_norm_bwd")

    du1, dw1_1, dw2_1 = mlp_bwd_weights(dh4_16, hm1, u1, w2_1, "mlp1")
    dh3, dh3_16, dg_mlp1 = mlp_bwd_input(du1, dh4, h3, w1_1, g_mlp1, "mlp1")

    dyn = _matmul(dh3_16, wout, "nt", name="out_proj_dyn")
    dwout = _matmul(yn, dh3_16, "tn", name="out_proj_dw").reshape(N_CHIPS, D_INNER // N_CHIPS, D_MODEL)
    behind = reduce_start("mlp1_out", [dw1_1, dw2_1, dwout])
    dy, dz, dg_gate = _gate_bwd(dyn, y, z, gate_g + behind)
    behind = reduce_midway("mlp1_out", dz)
    dxc, ddt_raw, small_ssd = _ssd_bwd(xc, dt_r, acs_r, sg_r, acs_t, alog_r, dskip_x + behind,
                                       e3_mat, e_mat.T, states, dy)
    dv, dconv = _conv_bwd_act(xbc, dxc, conv_w, conv_b)
    dxbc = _conv_bwd_in(dv, conv_w)
    dhn1 = _matmul(ddt_raw, w_dt, "nn", name="in_proj_dt_dh")
    dhn1 = _matmul(dz, w_z, "nn", name="in_proj_z_dh", add=dhn1)
    dhn1 = _matmul(dxbc, w_xbc, "nn", name="in_proj_xbc_dh", add=dhn1)
    dw_z = _matmul(dz, hn1, "tn", name="in_proj_z_dw")
    dw_xbc = _matmul(dxbc, hn1, "tn", name="in_proj_xbc_dw")
    dw_dt = _matmul(ddt_raw, hn1, "tn", name="in_proj_dt_dw")
    dwin = jnp.concatenate([dw_z, dw_xbc, dw_dt[:N_HEADS]], axis=0)
    dwin = _pad_rows(dwin.reshape(N_CHIPS, IN_PROJ_DIM // N_CHIPS, D_MODEL), _IN_PROJ_SHARD_ROWS)
    behind = reduce_start("in_proj", [dwin])
    dh2, dh2_16, dg_mix1 = _rmsnorm_bwd(dhn1, h2, g_mix1 + behind, dh3, name="norm_mix1_bwd")
    behind = reduce_midway("in_proj", dh2_16)

    du0, dw1_0, dw2_0 = mlp_bwd_weights(dh2_16, hm0, u0, w2_0, "mlp0")
    dh1, _, dg_mlp0 = mlp_bwd_input(du0, dh2, h1, w1_0, g_mlp0 + behind, "mlp0")
    dx, dpw, small_pool = _pool_bwd(xs, g_mix0, pw, pool_b, pool_scale, dh1)
    dpw = jnp.transpose(dpw.reshape(4, N_CHIPS, POOL_GROUP // N_CHIPS, POOL_GROUP), (1, 0, 2, 3))
    dpw = dpw.reshape(N_CHIPS, 4 * (POOL_GROUP // N_CHIPS), POOL_GROUP)

    big = [dpw, dw1_0, dw2_0]
    rows = [
        small_final[1:2],
        small_final[0:1],
        small_pool[0:1], dg_mix1[0:1],
        dg_mlp0[0:1], dg_mlp1[0:1],
        small_pool[1:2], small_pool[2:3],
        _pad_lanes(small_ssd[0:3], D_MODEL),
        _rows_1024(dg_gate[0:1]),
        _rows_1024(dconv[0:CONV_K]),
        _rows_1024(dconv[CONV_K:CONV_K + 1]),
    ]
    return dx, big, rows


def kernel(x, norm_mix_g, norm_mlp_g, pool_w, pool_b, pool_scale, ssm_w_in, ssm_conv_w, ssm_conv_b, ssm_dt_bias, ssm_a_log, ssm_d, ssm_norm_g, ssm_w_out, mlp_w1, mlp_w2, final_g, loss_target, m_norm_mix_g, m_norm_mlp_g, m_pool_w, m_pool_b, m_pool_scale, m_ssm_w_in, m_ssm_conv_w, m_ssm_conv_b, m_ssm_dt_bias, m_ssm_a_log, m_ssm_d, m_ssm_norm_g, m_ssm_w_out, m_mlp_w1, m_mlp_w2, m_final_g, v_norm_mix_g, v_norm_mlp_g, v_pool_w, v_pool_b, v_pool_scale, v_ssm_w_in, v_ssm_conv_w, v_ssm_conv_b, v_ssm_dt_bias, v_ssm_a_log, v_ssm_d, v_ssm_norm_g, v_ssm_w_out, v_mlp_w1, v_mlp_w2, v_final_g):
    t_dim = x.shape[1]
    xs = x[0]
    target = loss_target[0]
    my_x, my_y, my_c = _position()
    my_chip = 2 * my_x + my_y

    def halves(w):
        return w.astype(BF16).reshape((2, w.shape[0] // 2) + w.shape[1:])

    def whole(gathered, own_shard):
        g = lax.dynamic_update_index_in_dim(gathered, own_shard, my_chip, axis=0)
        return g.reshape((N_CHIPS, 2 * g.shape[2]) + g.shape[3:])

    def gather_lands(own):
        return [jax.ShapeDtypeStruct((N_CHIPS,) + s.shape, s.dtype) for s in own]

    vec_cols = CONV_DIM // N_CHIPS
    vec_own = jnp.concatenate([ssm_conv_w[0], ssm_conv_b, _pad_lanes(ssm_norm_g, vec_cols)], axis=0)
    early_own = [halves(pool_w[0]), halves(mlp_w1[0]), vec_own.reshape(2, (CONV_K + 2) // 2, vec_cols)]
    early, behind_early = _all_gather_weights(early_own)
    g_pool, w1_0, g_vec = [whole(g, o) for g, o in zip(early, early_own)]
    pw = jnp.transpose(g_pool, (1, 0, 2, 3)).reshape(4, POOL_GROUP, POOL_GROUP)
    conv_w = jnp.transpose(g_vec[:, 0:CONV_K, :], (1, 0, 2)).reshape(CONV_K, CONV_DIM)
    conv_b = g_vec[:, CONV_K, :].reshape(1, CONV_DIM)
    gate_g = g_vec[:, CONV_K + 1, :D_INNER // N_CHIPS].reshape(1, D_INNER)

    def behind_it(zero, ws):
        return [halves(w + zero) for w in ws]

    down_own = behind_it(behind_early, [mlp_w2[0]])
    fetches = {}
    fetches["mlp0_down"], behind_gather = _exchange_start(
        "gather_mlp0_down_start", _gather_copies, len(down_own) * (N_CHIPS - 1), down_own, gather_lands(down_own))
    in_own = behind_it(behind_gather, [_pad_rows(ssm_w_in[0].T, _IN_PROJ_SHARD_ROWS)])
    fetches["in_proj"], behind_gather = _exchange_start(
        "gather_in_proj_start", _gather_copies, len(in_own) * (N_CHIPS - 1), in_own, gather_lands(in_own))

    def fetch(what, after):
        if what == "mlp0_down":
            own_thru, landed = _exchange_wait("gather_mlp0_down_wait", _gather_copies, fetches[what], after)
            landed, _ = _forward_halves(landed, name="forward_mlp0_down")
            return whole(landed[0], own_thru[0]).reshape(D_FF, D_MODEL)
        if what == "in_proj":
            own_thru, landed = _exchange_wait("gather_in_proj_wait", _gather_copies, fetches["in_proj"], after)
            landed, behind = _forward_halves(landed, name="forward_in_proj")
            rest = behind_it(behind, [ssm_w_out[0], mlp_w1[1], mlp_w2[1]])
            fetches["rest"], behind = _exchange_start(
                "gather_rest_start", _gather_copies, len(rest) * (N_CHIPS - 1), rest, gather_lands(rest))
            win = whole(landed[0], own_thru[0])[:, :IN_PROJ_DIM // N_CHIPS].reshape(IN_PROJ_DIM, D_MODEL)
            w_dt = _pad_rows(win[D_INNER + CONV_DIM:], 128) + behind.astype(BF16)
            return win[:D_INNER], win[D_INNER:D_INNER + CONV_DIM], w_dt
        own_thru, landed = _exchange_wait("gather_rest_wait", _gather_copies, fetches["rest"], after)
        landed, _ = _forward_halves(landed, name="forward_rest")
        g_wout, w1_1, g_w2_1 = [whole(g, o) for g, o in zip(landed, own_thru)]
        return g_wout.reshape(D_INNER, D_MODEL), w1_1, g_w2_1.reshape(D_FF, D_MODEL)

    place = jnp.stack([my_c, my_chip]).astype(jnp.int32)
    waves = {}

    def reduce_start(wave, grads):
        waves[wave] = {}
        waves[wave]["pair"], behind = _exchange_start(
            "pair_%s_start" % wave, _pair_copies, len(grads), grads, _pair_lands(grads))
        return behind

    def reduce_midway(wave, after):
        st = waves[wave]
        grads, recv = _exchange_wait("pair_%s_wait" % wave, _pair_copies, st["pair"], after)
        sums = [_pair_sum(g, r, place, name="pair_sum_%s_%d" % (wave, i))
                for i, (g, r) in enumerate(zip(grads, recv))]
        st["f32"] = [s32 for _, s32 in sums]
        b16 = [s16 for s16, _ in sums]
        st["chip"], behind = _exchange_start(
            "chip_%s_start" % wave, _chip_copies, len(b16) * (N_CHIPS - 1), b16, _same_lands(b16))
        return behind

    def reduce_finish(wave, after):
        st = waves[wave]
        _, got = _exchange_wait("chip_%s_wait" % wave, _chip_copies, st["chip"], after)
        return [_chip_sum(s32, r, name="chip_sum_%s_%d" % (wave, i))
                for i, (s32, r) in enumerate(zip(st["f32"], got))]

    dx, big0, rows = _local_step(xs, target, pw, w1_0, fetch, reduce_start, reduce_midway,
                                 conv_w, conv_b, gate_g,
                                 norm_mix_g + behind_gather, norm_mlp_g, pool_b, pool_scale,
                                 ssm_dt_bias, ssm_a_log, ssm_d, final_g)

    behind = reduce_start("layer0", big0)
    small = jnp.concatenate(rows, axis=0)
    small = jnp.pad(small, ((0, (-small.shape[0]) % 8), (0, 0))) + behind
    small = _all_reduce_small(small, name="all_reduce_small", sum_row0=True)
    behind = reduce_midway("layer0", small)
    h_w1_1, h_w2_1, h_wout = reduce_finish("mlp1_out", behind)
    (h_win,) = reduce_finish("in_proj", behind)
    g_w1_1, g_w2_1, g_wout_s, g_win_s = _pair_gather_halves([h_w1_1, h_w2_1, h_wout, h_win],
                                                            name="pair_gather_layer1")
    loss = small[0, 0]
    g_final = small[1]
    g_norm_mix = small[2:4]
    g_norm_mlp = small[4:6]
    g_pool_b, g_pool_scale = small[6:7], small[7:8]
    g_alog, g_dtb, g_dsk = small[8:9, :N_HEADS], small[9:10, :N_HEADS], small[10:11, :N_HEADS]
    g_gate_full = small[11:13].reshape(1, D_INNER)
    g_convw_full = small[13:25].reshape(CONV_K, CONV_DIM)
    g_convb_full = small[25:28].reshape(1, CONV_DIM)
    g_gate = lax.dynamic_slice_in_dim(g_gate_full, my_chip * (D_INNER // N_CHIPS), D_INNER // N_CHIPS, axis=1)
    g_convw = lax.dynamic_slice_in_dim(g_convw_full, my_chip * (CONV_DIM // N_CHIPS), CONV_DIM // N_CHIPS, axis=1)
    g_convb = lax.dynamic_slice_in_dim(g_convb_full, my_chip * (CONV_DIM // N_CHIPS), CONV_DIM // N_CHIPS, axis=1)

    grads = {
        "norm_mix_g": g_norm_mix, "norm_mlp_g": g_norm_mlp,
        "pool_b": g_pool_b, "pool_scale": g_pool_scale,
        "ssm_conv_w": g_convw.reshape(ssm_conv_w.shape),
        "ssm_conv_b": g_convb, "ssm_dt_bias": g_dtb, "ssm_a_log": g_alog, "ssm_d": g_dsk,
        "ssm_norm_g": g_gate, "ssm_w_out": g_wout_s.reshape(ssm_w_out.shape),
        "final_g": g_final,
    }
    weights = dict(norm_mix_g=norm_mix_g, norm_mlp_g=norm_mlp_g, pool_w=pool_w, pool_b=pool_b,
                   pool_scale=pool_scale, ssm_w_in=ssm_w_in, ssm_conv_w=ssm_conv_w, ssm_conv_b=ssm_conv_b,
                   ssm_dt_bias=ssm_dt_bias, ssm_a_log=ssm_a_log, ssm_d=ssm_d, ssm_norm_g=ssm_norm_g,
                   ssm_w_out=ssm_w_out, mlp_w1=mlp_w1, mlp_w2=mlp_w2, final_g=final_g)
    moms = dict(norm_mix_g=(m_norm_mix_g, v_norm_mix_g), norm_mlp_g=(m_norm_mlp_g, v_norm_mlp_g),
                pool_w=(m_pool_w, v_pool_w), pool_b=(m_pool_b, v_pool_b),
                pool_scale=(m_pool_scale, v_pool_scale), ssm_w_in=(m_ssm_w_in, v_ssm_w_in),
                ssm_conv_w=(m_ssm_conv_w, v_ssm_conv_w), ssm_conv_b=(m_ssm_conv_b, v_ssm_conv_b),
                ssm_dt_bias=(m_ssm_dt_bias, v_ssm_dt_bias), ssm_a_log=(m_ssm_a_log, v_ssm_a_log),
                ssm_d=(m_ssm_d, v_ssm_d), ssm_norm_g=(m_ssm_norm_g, v_ssm_norm_g),
                ssm_w_out=(m_ssm_w_out, v_ssm_w_out), mlp_w1=(m_mlp_w1, v_mlp_w1),
                mlp_w2=(m_mlp_w2, v_mlp_w2), final_g=(m_final_g, v_final_g))
    names = list(weights)
    big_names = ("pool_w", "ssm_w_in", "ssm_w_out", "mlp_w1", "mlp_w2")
    deltas, new_m, new_v = {}, {}, {}

    def as_rows(nm, a):
        return a[0].T if nm == "ssm_w_in" else a.reshape(-1, a.shape[-1])

    def from_rows(nm, r):
        return r.T[None] if nm == "ssm_w_in" else r.reshape(weights[nm].shape)

    def update(nm, grad_rows, layer=None, into=None):
        return _adamw(as_rows(nm, weights[nm]), grad_rows, as_rows(nm, moms[nm][0]), as_rows(nm, moms[nm][1]),
                      name="adamw_%s_%s" % (nm, layer), part=layer, into=into)

    def keep(nm, results):
        deltas[nm], new_m[nm], new_v[nm] = [from_rows(nm, r) for r in results]

    g_win_rows = g_win_s[:IN_PROJ_DIM // N_CHIPS]
    grads["ssm_w_in"] = from_rows("ssm_w_in", g_win_rows)
    keep("ssm_w_in", update("ssm_w_in", g_win_rows))
    keep("ssm_w_out", update("ssm_w_out", g_wout_s))
    w1_done = update("mlp_w1", g_w1_1, layer=1)
    w2_done = update("mlp_w2", g_w2_1, layer=1)
    small_names = [nm for nm in names if nm not in big_names]
    sizes = [weights[nm].size for nm in small_names]

    def pack(parts):
        flat = jnp.concatenate([p.reshape(-1) for p in parts])
        pad = (-flat.shape[0]) % (8 * D_MODEL)
        return jnp.pad(flat, (0, pad)).reshape(-1, D_MODEL)

    d_, m_, v_ = _adamw(pack([weights[nm] for nm in small_names]), pack([grads[nm] for nm in small_names]),
                        pack([moms[nm][0] for nm in small_names]), pack([moms[nm][1] for nm in small_names]),
                        name="adamw_small")
    off = 0
    for nm, sz in zip(small_names, sizes):
        shp = weights[nm].shape
        deltas[nm] = d_.reshape(-1)[off:off + sz].reshape(shp)
        new_m[nm] = m_.reshape(-1)[off:off + sz].reshape(shp)
        new_v[nm] = v_.reshape(-1)[off:off + sz].reshape(shp)
        off += sz

    above = (deltas["ssm_w_in"][0, 0, 0] + deltas["ssm_w_out"][0, 0, 0] + w1_done[0][-1, -1]
             + w2_done[0][-1, -1] + d_[0, 0])
    g_pool_w, g_w1_0, g_w2_0 = _pair_gather_halves(reduce_finish("layer0", above), name="pair_gather_layer0")
    keep("mlp_w1", update("mlp_w1", g_w1_0, layer=0, into=w1_done))
    keep("mlp_w2", update("mlp_w2", g_w2_0, layer=0, into=w2_done))
    grads["pool_w"] = g_pool_w.reshape(pool_w.shape)
    grads["mlp_w1"] = jnp.stack([g_w1_0, g_w1_1])
    grads["mlp_w2"] = jnp.stack([g_w2_0, g_w2_1])
    keep("pool_w", update("pool_w", g_pool_w))

    grad_x = dx.reshape(x.shape)
    out_grads = [grads[nm].reshape(weights[nm].shape) for nm in names]
    return (loss, grad_x, *out_grads, *[deltas[nm] for nm in names],
            *[new_m[nm] for nm in names], *[new_v[nm] for nm in names])
```

```python
import functools

import jax
import jax.numpy as jnp
from jax import lax
from jax.experimental import pallas as pl
from jax.experimental.pallas import tpu as pltpu

F32 = jnp.float32
BF16 = jnp.bfloat16
MESH = pl.DeviceIdType.MESH

D_MODEL = 1024
RMS_EPS = 1e-5
POOL_WINDOWS = (2, 4, 8, 16)
POOL_GROUP = 256
POOL_HALO = 16
D_INNER = 2048
HEAD_DIM = 64
N_HEADS = 32
N_GROUPS = 4
HEADS_PER_GROUP = 8
D_STATE = 128
CONV_K = 4
CONV_HALO = 8
CHUNK = 128
CONV_DIM = 3072
IN_PROJ_DIM = 5152
D_FF = 4096
N_CHIPS = 4
N_DEV = 8

ADAM_LR = 0.001
ADAM_B1 = 0.9
ADAM_B2 = 0.999
ADAM_EPS = 1e-08
ADAM_WD = 0.01
ADAM_STEP = 10

VMEM_LIMIT = 56 * 1024 * 1024
NEG_INF = float("-inf")


def _pcall(body, **kw):
    return pl.pallas_call(body, **kw)


def _params(*sem):
    return pltpu.CompilerParams(dimension_semantics=sem, vmem_limit_bytes=VMEM_LIMIT)


def _sigmoid(v):
    return 1.0 / (1.0 + jnp.exp(-v))


def _row_spec(tb, d, nb=None, reverse=False):
    if reverse:
        return pl.BlockSpec((tb, d), lambda i: (nb - 1 - i, 0))
    return pl.BlockSpec((tb, d), lambda i: (i, 0))


def _const_spec(shape):
    return pl.BlockSpec(shape, lambda *_: tuple(0 for _ in shape))


_DIMS = {"nn": (((1,), (0,)), ((), ())),
         "nt": (((1,), (1,)), ((), ())),
         "tn": (((0,), (0,)), ((), ()))}


_MATMUL_VMEM_BUDGET = 40 * 1024 * 1024


def _matmul_tiles(m_dim, n_dim, k_dim, a_bytes, b_bytes, mn_bytes):
    tm, tn = min(m_dim, 1024), min(n_dim, 1024)
    while 2 * (tm * k_dim * a_bytes + tn * k_dim * b_bytes + tm * tn * mn_bytes) > _MATMUL_VMEM_BUDGET:
        if tm >= tn:
            tm //= 2
        else:
            tn //= 2
    return tm, tn


def _matmul(a, b, mode, *, name, out_dtype=F32, a_relu2=False, add=None, relu2_grad_of=None,
            out_col_shards=1, b_col_shards=False):
    if mode == "tn":
        k_dim, m_dim = a.shape
    else:
        m_dim, k_dim = a.shape
    if b_col_shards:
        n_shards, shard_cols = b.shape[0], b.shape[2]
        n_dim = n_shards * shard_cols if mode == "nn" else b.shape[1]
    else:
        n_dim = b.shape[0] if mode == "nt" else b.shape[1]
    mn_bytes = jnp.dtype(out_dtype).itemsize
    if relu2_grad_of is not None:
        mn_bytes += relu2_grad_of.dtype.itemsize
    if add is not None:
        mn_bytes += add.dtype.itemsize
    tm, tn = _matmul_tiles(m_dim, n_dim, k_dim, a.dtype.itemsize, b.dtype.itemsize, mn_bytes)
    assert m_dim % tm == 0 and n_dim % tn == 0
    a_spec = (pl.BlockSpec((k_dim, tm), lambda i, j: (0, i)) if mode == "tn"
              else pl.BlockSpec((tm, k_dim), lambda i, j: (i, 0)))
    if b_col_shards and mode == "nn":
        assert shard_cols % tn == 0
        per_shard = shard_cols // tn
        b_spec = pl.BlockSpec((None, k_dim, tn), lambda i, j: (j // per_shard, 0, j % per_shard))
    elif b_col_shards:
        assert mode == "nt" and k_dim == n_shards * shard_cols
        b_spec = pl.BlockSpec((n_shards, tn, shard_cols), lambda i, j: (0, j, 0))
    else:
        b_spec = (pl.BlockSpec((tn, k_dim), lambda i, j: (j, 0)) if mode == "nt"
                  else pl.BlockSpec((k_dim, tn), lambda i, j: (0, j)))
    mn_spec = pl.BlockSpec((tm, tn), lambda i, j: (i, j))
    operands, in_specs = [a, b], [a_spec, b_spec]
    if relu2_grad_of is not None:
        operands.append(relu2_grad_of)
        in_specs.append(mn_spec)
    if add is not None:
        operands.append(add)
        in_specs.append(mn_spec)
    if out_col_shards == 1:
        out_shape = jax.ShapeDtypeStruct((m_dim, n_dim), out_dtype)
        out_spec = mn_spec
    else:
        n_shard = n_dim // out_col_shards
        assert n_shard % tn == 0
        per = n_shard // tn
        out_shape = jax.ShapeDtypeStruct((out_col_shards, m_dim, n_shard), out_dtype)
        out_spec = pl.BlockSpec((None, tm, tn), lambda i, j: (j // per, i, j % per))

    def body(*refs):
        a_ref, b_ref, o_ref = refs[0], refs[1], refs[-1]
        av = a_ref[...]
        if a_relu2:
            av = jnp.maximum(av, 0)
            av = av * av
        if b_col_shards and mode == "nt":
            r = None
            for s in range(n_shards):
                part = lax.dot_general(av[:, s * shard_cols:(s + 1) * shard_cols].astype(BF16),
                                       b_ref[s].astype(BF16), _DIMS[mode], preferred_element_type=F32)
                r = part if r is None else r + part
        else:
            r = lax.dot_general(av.astype(BF16), b_ref[...].astype(BF16), _DIMS[mode],
                                preferred_element_type=F32)
        nxt = 2
        if relu2_grad_of is not None:
            r = r * (2.0 * jnp.maximum(refs[nxt][...].astype(F32), 0.0))
            nxt += 1
        if add is not None:
            r = r + refs[nxt][...]
        o_ref[...] = r.astype(out_dtype)

    return _pcall(
        body, name=name, out_shape=out_shape,
        grid=(m_dim // tm, n_dim // tn),
        in_specs=in_specs, out_specs=out_spec,
        compiler_params=_params("parallel", "parallel"),
    )(*operands)


def _rms(x):
    return lax.rsqrt(jnp.mean(x * x, axis=-1, keepdims=True) + RMS_EPS)


def _rmsnorm_fwd(h, g, *, name, tb=512):
    t_dim, d = h.shape

    def body(h_ref, g_ref, o_ref):
        x = h_ref[...]
        o_ref[...] = (x * _rms(x) * g_ref[...]).astype(BF16)

    return _pcall(
        body, name=name, out_shape=jax.ShapeDtypeStruct((t_dim, d), BF16),
        grid=(t_dim // tb,), in_specs=[_row_spec(tb, d), _const_spec((1, d))],
        out_specs=_row_spec(tb, d), compiler_params=_params("parallel"),
    )(h, g)


def _rmsnorm_bwd(dy, h, g, dres, *, name, tb=512):
    t_dim, d = h.shape

    def body(dy_ref, h_ref, g_ref, dres_ref, dh_ref, dh16_ref, dg_ref):
        @pl.when(pl.program_id(0) == 0)
        def _():
            dg_ref[...] = jnp.zeros_like(dg_ref)

        x = h_ref[...]
        r = _rms(x)
        xhat = x * r
        dyv = dy_ref[...]
        dxhat = dyv * g_ref[...]
        dh = dres_ref[...] + r * (dxhat - xhat * jnp.mean(dxhat * xhat, axis=-1, keepdims=True))
        dh_ref[...] = dh
        dh16_ref[...] = dh.astype(BF16)
        dg_ref[0:1, :] += jnp.sum(dyv * xhat, axis=0, keepdims=True)

    return _pcall(
        body, name=name,
        out_shape=(jax.ShapeDtypeStruct((t_dim, d), F32), jax.ShapeDtypeStruct((t_dim, d), BF16),
                   jax.ShapeDtypeStruct((8, d), F32)),
        grid=(t_dim // tb,),
        in_specs=[_row_spec(tb, d), _row_spec(tb, d), _const_spec((1, d)), _row_spec(tb, d)],
        out_specs=(_row_spec(tb, d), _row_spec(tb, d), _const_spec((8, d))),
        compiler_params=_params("arbitrary"),
    )(dy, h, g, dres)


def _pool_mixed(ext, hn, t0, tb):
    t = t0 + lax.broadcasted_iota(jnp.int32, (tb, 1), 0)
    parts = []
    for gi, w in enumerate(POOL_WINDOWS):
        lanes = slice(gi * POOL_GROUP, (gi + 1) * POOL_GROUP)
        s = ext[:, lanes]
        k = 1
        while k < w:
            s = s + pltpu.roll(s, k, 0)
            k *= 2
        cnt = jnp.minimum(t + 1, w).astype(F32)
        parts.append(s[POOL_HALO:, :] / cnt - hn[:, lanes])
    return parts


def _pool_fwd(x, g, pw, pb, ps, *, tb=512):
    t_dim, d = x.shape

    def body(x_ref, g_ref, pw_ref, pb_ref, ps_ref, o_ref, ext_ref):
        i = pl.program_id(0)

        @pl.when(i == 0)
        def _():
            ext_ref[0:POOL_HALO, :] = jnp.zeros((POOL_HALO, d), F32)

        xv = x_ref[...]
        hn = xv * _rms(xv) * g_ref[...]
        ext_ref[POOL_HALO:, :] = hn
        mixed = _pool_mixed(ext_ref[...], hn, i * tb, tb)
        for gi in range(len(POOL_WINDOWS)):
            lanes = slice(gi * POOL_GROUP, (gi + 1) * POOL_GROUP)
            out = jnp.dot(mixed[gi].astype(BF16), pw_ref[gi], preferred_element_type=F32)
            o_ref[:, lanes] = xv[:, lanes] + (out + pb_ref[:, lanes]) * ps_ref[:, lanes]
        ext_ref[0:POOL_HALO, :] = hn[tb - POOL_HALO:, :]

    return _pcall(
        body, name="pool_fwd", out_shape=jax.ShapeDtypeStruct((t_dim, d), F32),
        grid=(t_dim // tb,),
        in_specs=[_row_spec(tb, d), _const_spec((1, d)), _const_spec((4, POOL_GROUP, POOL_GROUP)),
                  _const_spec((1, d)), _const_spec((1, d))],
        out_specs=_row_spec(tb, d),
        scratch_shapes=[pltpu.VMEM((POOL_HALO + tb, d), F32)],
        compiler_params=_params("arbitrary"),
    )(x, g, pw, pb, ps)


def _pool_bwd(x, g, pw, pb, ps, dh1, *, tb=512):
    t_dim, d = x.shape
    nb = t_dim // tb
    halo_per_block = tb // POOL_HALO

    def body(x_ref, xprev_ref, g_ref, pw_ref, pb_ref, ps_ref, dh1_ref,
             dx_ref, dpw_ref, small_ref, ext_ref, dext_ref):
        i = pl.program_id(0)
        blk = nb - 1 - i

        @pl.when(i == 0)
        def _():
            dpw_ref[...] = jnp.zeros_like(dpw_ref)
            small_ref[...] = jnp.zeros_like(small_ref)
            dext_ref[tb:, :] = jnp.zeros((POOL_HALO, d), F32)

        gv = g_ref[...]
        xv = x_ref[...]
        r = _rms(xv)
        xhat = xv * r
        hn = xhat * gv
        xp = xprev_ref[...]
        hprev = xp * _rms(xp) * gv * (blk > 0).astype(F32)
        ext_ref[0:POOL_HALO, :] = hprev
        ext_ref[POOL_HALO:, :] = hn
        mixed = _pool_mixed(ext_ref[...], hn, blk * tb, tb)

        dout = dh1_ref[...]
        t = blk * tb + lax.broadcasted_iota(jnp.int32, (tb, 1), 0)
        for gi, w in enumerate(POOL_WINDOWS):
            lanes = slice(gi * POOL_GROUP, (gi + 1) * POOL_GROUP)
            mb = mixed[gi].astype(BF16)
            pre = jnp.dot(mb, pw_ref[gi], preferred_element_type=F32) + pb_ref[:, lanes]
            dg_out = dout[:, lanes]
            small_ref[2:3, lanes] += jnp.sum(dg_out * pre, axis=0, keepdims=True)
            dpre = dg_out * ps_ref[:, lanes]
            small_ref[1:2, lanes] += jnp.sum(dpre, axis=0, keepdims=True)
            dpb16 = dpre.astype(BF16)
            dpw_ref[gi] += lax.dot_general(mb, dpb16, _DIMS["tn"], preferred_element_type=F32)
            dmixed = lax.dot_general(dpb16, pw_ref[gi], _DIMS["nt"], preferred_element_type=F32)
            cnt = jnp.minimum(t + 1, w).astype(F32)
            dq = dmixed / cnt
            dext_ref[0:tb, lanes] = dq
            s = dext_ref[:, lanes]
            k = 1
            while k < w:
                s = s + pltpu.roll(s, tb + POOL_HALO - k, 0)
                k *= 2
            dhn = s[0:tb, :] - dmixed
            dext_ref[tb:, lanes] = dq[0:POOL_HALO, :]
            small_ref[0:1, lanes] += jnp.sum(dhn * xhat[:, lanes], axis=0, keepdims=True)
            ext_ref[POOL_HALO:, lanes] = dhn * gv[:, lanes]
        dxhat = ext_ref[POOL_HALO:, :]
        dx_ref[...] = dout + r * (dxhat - xhat * jnp.mean(dxhat * xhat, axis=-1, keepdims=True))

    return _pcall(
        body, name="pool_bwd",
        out_shape=(jax.ShapeDtypeStruct((t_dim, d), F32),
                   jax.ShapeDtypeStruct((4, POOL_GROUP, POOL_GROUP), F32),
                   jax.ShapeDtypeStruct((8, d), F32)),
        grid=(nb,),
        in_specs=[_row_spec(tb, d, nb, True),
                  pl.BlockSpec((POOL_HALO, d),
                               lambda i: (jnp.maximum((nb - 1 - i) * halo_per_block - 1, 0), 0)),
                  _const_spec((1, d)), _const_spec((4, POOL_GROUP, POOL_GROUP)),
                  _const_spec((1, d)), _const_spec((1, d)), _row_spec(tb, d, nb, True)],
        out_specs=(_row_spec(tb, d, nb, True), _const_spec((4, POOL_GROUP, POOL_GROUP)),
                   _const_spec((8, d))),
        scratch_shapes=[pltpu.VMEM((POOL_HALO + tb, d), F32), pltpu.VMEM((tb + POOL_HALO, d), F32)],
        compiler_params=_params("arbitrary"),
    )(x, x, g, pw, pb, ps, dh1)


_CONV_CB = 1024
_STRIP = 16


def _strips(tb, fn, unroll=4):
    def step(i, carry):
        fn(pl.multiple_of(i * _STRIP, _STRIP))
        return carry
    lax.fori_loop(0, tb // _STRIP, step, 0, unroll=unroll)


def _conv_taps(ext_ref, r0, w):
    shifted = [ext_ref[CONV_HALO + r0 - sh:CONV_HALO + r0 - sh + _STRIP, :] for sh in range(CONV_K)]
    acc = shifted[0] * w[CONV_K - 1:CONV_K, :]
    for sh in range(1, CONV_K):
        acc = acc + shifted[sh] * w[CONV_K - 1 - sh:CONV_K - sh, :]
    return shifted, acc


def _conv_fwd(u, w, b, *, tb=512):
    t_dim, c = u.shape
    cb = _CONV_CB

    def body(u_ref, w_ref, b_ref, o_ref, ext_ref):
        @pl.when(pl.program_id(1) == 0)
        def _():
            ext_ref[0:CONV_HALO, :] = jnp.zeros((CONV_HALO, cb), F32)

        wv = w_ref[...]
        bv = b_ref[...]

        def fill(r0):
            ext_ref[pl.ds(CONV_HALO + r0, _STRIP), :] = u_ref[pl.ds(r0, _STRIP), :].astype(F32)

        _strips(tb, fill)
        for r0 in range(0, tb, _STRIP):
            v = _conv_taps(ext_ref, r0, wv)[1] + bv
            o_ref[r0:r0 + _STRIP, :] = (v * _sigmoid(v)).astype(BF16)
        ext_ref[0:CONV_HALO, :] = ext_ref[tb:tb + CONV_HALO, :]

    blk = pl.BlockSpec((tb, cb), lambda j, t: (t, j))
    return _pcall(
        body, name="conv_fwd", out_shape=jax.ShapeDtypeStruct((t_dim, c), BF16),
        grid=(c // cb, t_dim // tb),
        in_specs=[blk, pl.BlockSpec((CONV_K, cb), lambda j, t: (0, j)),
                  pl.BlockSpec((1, cb), lambda j, t: (0, j))],
        out_specs=blk,
        scratch_shapes=[pltpu.VMEM((CONV_HALO + tb, cb), F32)],
        compiler_params=_params("parallel", "arbitrary"),
    )(u, w, b)


def _conv_bwd_act(u, dxc, w, b, *, tb=512):
    t_dim, c = u.shape
    cb = _CONV_CB
    half = _STRIP // 2

    def body(u_ref, d_ref, w_ref, b_ref, dv_ref, dwb_ref, ext_ref, acc_ref):
        @pl.when(pl.program_id(1) == 0)
        def _():
            ext_ref[0:CONV_HALO, :] = jnp.zeros((CONV_HALO, cb), F32)
            dwb_ref[...] = jnp.zeros_like(dwb_ref)

        acc_ref[...] = jnp.zeros_like(acc_ref)
        wv = w_ref[...]
        bv = b_ref[...]

        def fill(r0):
            ext_ref[pl.ds(CONV_HALO + r0, _STRIP), :] = u_ref[pl.ds(r0, _STRIP), :].astype(F32)

        _strips(tb, fill)
        for r0 in range(0, tb, _STRIP):
            shifted, v = _conv_taps(ext_ref, r0, wv)
            v = v + bv
            sg = _sigmoid(v)
            dv = d_ref[r0:r0 + _STRIP, :].astype(F32) * (sg * (1.0 + v * (1.0 - sg)))
            dv_ref[r0:r0 + _STRIP, :] = dv.astype(BF16)
            acc_ref[CONV_K] += dv[0:half, :] + dv[half:, :]
            for sh in range(CONV_K):
                p = dv * shifted[sh]
                acc_ref[CONV_K - 1 - sh] += p[0:half, :] + p[half:, :]
        for k in range(CONV_K + 1):
            dwb_ref[k:k + 1, :] += jnp.sum(acc_ref[k], axis=0, keepdims=True)
        ext_ref[0:CONV_HALO, :] = ext_ref[tb:tb + CONV_HALO, :]

    blk = pl.BlockSpec((tb, cb), lambda j, t: (t, j))
    return _pcall(
        body, name="conv_bwd_act",
        out_shape=(jax.ShapeDtypeStruct((t_dim, c), BF16), jax.ShapeDtypeStruct((8, c), F32)),
        grid=(c // cb, t_dim // tb),
        in_specs=[blk, blk, pl.BlockSpec((CONV_K, cb), lambda j, t: (0, j)),
                  pl.BlockSpec((1, cb), lambda j, t: (0, j))],
        out_specs=(blk, pl.BlockSpec((8, cb), lambda j, t: (0, j))),
        scratch_shapes=[pltpu.VMEM((CONV_HALO + tb, cb), F32), pltpu.VMEM((CONV_K + 1, half, cb), F32)],
        compiler_params=_params("parallel", "arbitrary"),
    )(u, dxc, w, b)


def _conv_bwd_in(dv, w, *, tb=512):
    t_dim, c = dv.shape
    cb = _CONV_CB
    nb = t_dim // tb

    def body(dv_ref, w_ref, du_ref, ext_ref):
        @pl.when(pl.program_id(1) == 0)
        def _():
            ext_ref[tb:, :] = jnp.zeros((CONV_HALO, cb), F32)

        wv = w_ref[...]

        def fill(r0):
            ext_ref[pl.ds(r0, _STRIP), :] = dv_ref[pl.ds(r0, _STRIP), :].astype(F32)

        _strips(tb, fill)
        for r0 in range(0, tb, _STRIP):
            acc = ext_ref[r0:r0 + _STRIP, :] * wv[CONV_K - 1:CONV_K, :]
            for sh in range(1, CONV_K):
                acc = acc + ext_ref[r0 + sh:r0 + sh + _STRIP, :] * wv[CONV_K - 1 - sh:CONV_K - sh, :]
            du_ref[r0:r0 + _STRIP, :] = acc.astype(BF16)
        ext_ref[tb:, :] = ext_ref[0:CONV_HALO, :]

    blk = pl.BlockSpec((tb, cb), lambda j, t: (nb - 1 - t, j))
    return _pcall(
        body, name="conv_bwd_in", out_shape=jax.ShapeDtypeStruct((t_dim, c), BF16),
        grid=(c // cb, nb),
        in_specs=[blk, pl.BlockSpec((CONV_K, cb), lambda j, t: (0, j))],
        out_specs=blk,
        scratch_shapes=[pltpu.VMEM((tb + CONV_HALO, cb), F32)],
        compiler_params=_params("parallel", "arbitrary"),
    )(dv, w)


def _softplus(v):
    e = jnp.exp(-jnp.abs(v))
    w = 1.0 + e
    log1p = jnp.where(w == 1.0, e, jnp.log(w) * e / jnp.where(w == 1.0, 1.0, w - 1.0))
    return jnp.maximum(v, 0.0) + log1p


def _cumsum_rows(v):
    row = lax.broadcasted_iota(jnp.int32, v.shape, 0) & (CHUNK - 1)
    k = 1
    while k < CHUNK:
        v = v + jnp.where(row >= k, pltpu.roll(v, k, 0), 0.0)
        k *= 2
    return v


def _cumsum_lanes(v):
    col = lax.broadcasted_iota(jnp.int32, v.shape, 1) & (CHUNK - 1)
    k = 1
    while k < CHUNK:
        v = v + jnp.where(col >= k, pltpu.roll(v, k, 1), 0.0)
        k *= 2
    return v


def _rev_cumsum_rows(v):
    row = lax.broadcasted_iota(jnp.int32, v.shape, 0)
    k = 1
    while k < CHUNK:
        v = v + jnp.where(row < CHUNK - k, pltpu.roll(v, CHUNK - k, 0), 0.0)
        k *= 2
    return v


PAIR = 2 * HEAD_DIM
GROUP_LANES = HEADS_PER_GROUP * HEAD_DIM


def _head_lane_matrix():
    h = lax.broadcasted_iota(jnp.int32, (128, D_INNER), 0)
    j = lax.broadcasted_iota(jnp.int32, (128, D_INNER), 1)
    return (j // HEAD_DIM == h).astype(BF16)


def _split_bf16(v, pieces):
    out = []
    for _ in range(pieces):
        p = v.astype(BF16)
        out.append(p)
        v = v - p.astype(F32)
    return out


_EXACT_PIECES = 3


def _expand_heads(values, e3):
    lhs = jnp.concatenate([jnp.concatenate(_split_bf16(v, _EXACT_PIECES), axis=1) for v in values], axis=0)
    out = jnp.dot(lhs, e3, preferred_element_type=F32)
    rows = values[0].shape[0]
    return [out[i * rows:(i + 1) * rows, :] for i in range(len(values))]


def _reduce_heads(v, et, pieces):
    return sum(jnp.dot(p, et, preferred_element_type=F32) for p in _split_bf16(v, pieces))


def _ssd_decay(dt_raw, dt_raw_t, bias_r, bias_c, alog_r, alog_c, *, tb=1024):
    t_dim = dt_raw.shape[0]
    tb = min(tb, t_dim)
    assert t_dim % tb == 0 and tb % CHUNK == 0

    def body(dtr_ref, dtt_ref, br_ref, bc_ref, ar_ref, ac_ref, dt_ref, acs_ref, sg_ref, acst_ref):
        pre = dtr_ref[...] + br_ref[...]
        dt = _softplus(pre)
        dt_ref[...] = dt
        sg_ref[...] = _sigmoid(pre)
        acs_ref[...] = _cumsum_rows(dt * (-jnp.exp(ar_ref[...])))
        acst_ref[...] = _cumsum_lanes(_softplus(dtt_ref[...] + bc_ref[...]) * (-jnp.exp(ac_ref[...])))

    rows = pl.BlockSpec((tb, 128), lambda i: (i, 0))
    cols = pl.BlockSpec((N_HEADS, tb), lambda i: (0, i))
    sds = jax.ShapeDtypeStruct((t_dim, 128), F32)
    return _pcall(
        body, name="ssd_decay",
        out_shape=(sds, sds, sds, jax.ShapeDtypeStruct((N_HEADS, t_dim), F32)),
        grid=(t_dim // tb,),
        in_specs=[rows, cols, _const_spec((1, 128)), _const_spec((N_HEADS, 1)),
                  _const_spec((1, 128)), _const_spec((N_HEADS, 1))],
        out_specs=(rows, rows, rows, cols), compiler_params=_params("parallel"),
    )(dt_raw, dt_raw_t, bias_r, bias_c, alog_r, alog_c)


def _pair_decay(acs_slab, acs_c, h0, causal, left):
    other = pltpu.roll(acs_slab, HEAD_DIM, 1)
    col0 = jnp.where(left, acs_slab, other)
    col1 = jnp.where(left, other, acs_slab)
    l0 = jnp.exp(jnp.where(causal, col0 - acs_c[h0:h0 + 1, :], NEG_INF))
    l1 = jnp.exp(jnp.where(causal, col1 - acs_c[h0 + 1:h0 + 2, :], NEG_INF))
    return l0, l1


def _ssd_fwd(xc, dt_r, acs_r, acs_t, dskip_x, e3_mat):
    t_dim = xc.shape[0]
    nc = t_dim // CHUNK

    def body(xc_ref, dt_ref, acs_ref, acst_ref, dk_ref, e3_ref, y_ref, st_ref, state):
        @pl.when(pl.program_id(0) == 0)
        def _():
            state[...] = jnp.zeros_like(state)

        dt, acs = _expand_heads([dt_ref[...], acs_ref[...]], e3_ref[...])
        acs_c = acst_ref[...]
        st_ref[0] = state[...]
        last = acs[CHUNK - 1:CHUNK, :]
        xs32 = xc_ref[:, 0:D_INNER].astype(F32)
        xdt = xs32 * dt
        xdt16 = xdt.astype(BF16)
        xdte16 = (xdt * jnp.exp(last - acs)).astype(BF16)
        ea = jnp.exp(acs)
        cd = jnp.exp(last)
        skip = dk_ref[...] * xs32
        causal = (lax.broadcasted_iota(jnp.int32, (CHUNK, CHUNK), 0)
                  >= lax.broadcasted_iota(jnp.int32, (CHUNK, CHUNK), 1))
        left = lax.broadcasted_iota(jnp.int32, (CHUNK, PAIR), 1) < HEAD_DIM
        for g in range(N_GROUPS):
            gl = slice(g * GROUP_LANES, (g + 1) * GROUP_LANES)
            bg = xc_ref[:, D_INNER + g * D_STATE:D_INNER + (g + 1) * D_STATE]
            cg = xc_ref[:, D_INNER + (N_GROUPS + g) * D_STATE:D_INNER + (N_GROUPS + g + 1) * D_STATE]
            cb = lax.dot_general(cg, bg, _DIMS["nt"], preferred_element_type=F32)
            hprev = state[:, gl]
            ch = jnp.dot(cg, hprev.astype(BF16), preferred_element_type=F32)
            for j in range(HEADS_PER_GROUP // 2):
                pl_ = slice(g * GROUP_LANES + j * PAIR, g * GROUP_LANES + (j + 1) * PAIR)
                h0 = g * HEADS_PER_GROUP + 2 * j
                l0, l1 = _pair_decay(acs[:, pl_], acs_c, h0, causal, left)
                lhs = jnp.concatenate([(cb * l0).astype(BF16), (cb * l1).astype(BF16)], axis=1)
                xp = xdt16[:, pl_]
                zero = jnp.zeros_like(xp)
                rhs = jnp.concatenate([jnp.where(left, xp, zero), jnp.where(left, zero, xp)], axis=0)
                ydiag = jnp.dot(lhs, rhs, preferred_element_type=F32)
                y_ref[:, pl_] = ydiag + ch[:, j * PAIR:(j + 1) * PAIR] * ea[:, pl_] + skip[:, pl_]
            s_new = lax.dot_general(bg, xdte16[:, gl], _DIMS["tn"], preferred_element_type=F32)
            state[:, gl] = hprev * cd[:, gl] + s_new

    rows = lambda w: pl.BlockSpec((CHUNK, w), lambda c: (c, 0))
    return _pcall(
        body, name="ssd_fwd",
        out_shape=(jax.ShapeDtypeStruct((t_dim, D_INNER), F32),
                   jax.ShapeDtypeStruct((nc, D_STATE, D_INNER), F32)),
        grid=(nc,),
        in_specs=[rows(CONV_DIM), rows(128), rows(128), pl.BlockSpec((N_HEADS, CHUNK), lambda c: (0, c)),
                  _const_spec((1, D_INNER)), _const_spec((_EXACT_PIECES * 128, D_INNER))],
        out_specs=(rows(D_INNER), pl.BlockSpec((1, D_STATE, D_INNER), lambda c: (c, 0, 0))),
        scratch_shapes=[pltpu.VMEM((D_STATE, D_INNER), F32)],
        compiler_params=_params("arbitrary"),
    )(xc, dt_r, acs_r, acs_t, dskip_x, e3_mat)


def _ssd_bwd(xc, dt_r, acs_r, sg_r, acs_t, alog_r, dskip_x, e3_mat, et_mat, states, dy):
    t_dim = xc.shape[0]
    nc = t_dim // CHUNK

    def body(xc_ref, dt_ref, acs_ref, sg_ref, acst_ref, ar_ref, dk_ref, e3_ref, et_ref, st_ref, dy_ref,
             dxc_ref, ddt_ref, small_ref, dstate, dacs_ref, dxdt_ref, acc_x, acc_r):
        step = pl.program_id(0)

        @pl.when(step == 0)
        def _():
            dstate[...] = jnp.zeros_like(dstate)
            acc_x[...] = jnp.zeros_like(acc_x)
            acc_r[...] = jnp.zeros_like(acc_r)

        dt_r = dt_ref[...]
        a_r = -jnp.exp(ar_ref[...])
        dt, acs = _expand_heads([dt_r, acs_ref[...]], e3_ref[...])
        acs_c = acst_ref[...]
        last = acs[CHUNK - 1:CHUNK, :]
        xs32 = xc_ref[:, 0:D_INNER].astype(F32)
        xdt = xs32 * dt
        xdt16 = xdt.astype(BF16)
        dte = jnp.exp(last - acs)
        xdte = xdt * dte
        xdte16 = xdte.astype(BF16)
        cd = jnp.exp(last)
        dyv = dy_ref[...]
        dy16 = dyv.astype(BF16)
        dye = dyv * jnp.exp(acs)
        dye16 = dye.astype(BF16)
        causal = (lax.broadcasted_iota(jnp.int32, (CHUNK, CHUNK), 0)
                  >= lax.broadcasted_iota(jnp.int32, (CHUNK, CHUNK), 1))
        left = lax.broadcasted_iota(jnp.int32, (CHUNK, PAIR), 1) < HEAD_DIM
        lane_id = lax.broadcasted_iota(jnp.int32, (CHUNK, 128), 1)
        row_id = lax.broadcasted_iota(jnp.int32, (CHUNK, 128), 0)
        is_last_row = lax.broadcasted_iota(jnp.int32, (CHUNK, 1), 0) == CHUNK - 1
        dacs_cols = jnp.zeros((CHUNK, 128), F32)
        dacs_rows = jnp.zeros((CHUNK, 128), F32)
        for g in range(N_GROUPS):
            gl = slice(g * GROUP_LANES, (g + 1) * GROUP_LANES)
            b_lanes = slice(D_INNER + g * D_STATE, D_INNER + (g + 1) * D_STATE)
            c_lanes = slice(D_INNER + (N_GROUPS + g) * D_STATE, D_INNER + (N_GROUPS + g + 1) * D_STATE)
            bg = xc_ref[:, b_lanes]
            cg = xc_ref[:, c_lanes]
            cb = lax.dot_general(cg, bg, _DIMS["nt"], preferred_element_type=F32)
            hprev = st_ref[0, :, gl]
            hp16 = hprev.astype(BF16)
            dhn = dstate[:, gl]
            dhn16 = dhn.astype(BF16)
            ch = jnp.dot(cg, hp16, preferred_element_type=F32)
            gmat = jnp.dot(bg, dhn16, preferred_element_type=F32)
            gx = gmat * xdte[:, gl]
            dlast = jnp.sum(gx, axis=0, keepdims=True) + cd[:, gl] * jnp.sum(dhn * hprev, axis=0, keepdims=True)
            dacs_ref[:, gl] = dye[:, gl] * ch - gx + jnp.where(is_last_row, dlast, 0.0)
            dc_acc = lax.dot_general(dye16[:, gl], hp16, _DIMS["nt"], preferred_element_type=F32)
            db_acc = lax.dot_general(xdte16[:, gl], dhn16, _DIMS["nt"], preferred_element_type=F32)
            dstate[:, gl] = dhn * cd[:, gl] + lax.dot_general(cg, dye16[:, gl], _DIMS["tn"],
                                                             preferred_element_type=F32)
            dcb = jnp.zeros((CHUNK, CHUNK), F32)
            for j in range(HEADS_PER_GROUP // 2):
                pl_ = slice(g * GROUP_LANES + j * PAIR, g * GROUP_LANES + (j + 1) * PAIR)
                h0 = g * HEADS_PER_GROUP + 2 * j
                l0, l1 = _pair_decay(acs[:, pl_], acs_c, h0, causal, left)
                m0, m1 = cb * l0, cb * l1
                lhs = jnp.concatenate([m0.astype(BF16), m1.astype(BF16)], axis=1)
                dyp = dy16[:, pl_]
                zero = jnp.zeros_like(dyp)
                both = lax.dot_general(lhs, dyp, _DIMS["tn"], preferred_element_type=F32)
                dxdt_ref[:, pl_] = (jnp.where(left, both[0:CHUNK, :], both[CHUNK:, :])
                                    + gmat[:, j * PAIR:(j + 1) * PAIR] * dte[:, pl_])
                lhs2 = jnp.concatenate([jnp.where(left, dyp, zero), jnp.where(left, zero, dyp)], axis=0)
                dm = lax.dot_general(lhs2, xdt16[:, pl_], _DIMS["nt"], preferred_element_type=F32)
                dm0, dm1 = dm[0:CHUNK, :], dm[CHUNK:, :]
                dcb = dcb + dm0 * l0 + dm1 * l1
                ds0, ds1 = dm0 * m0, dm1 * m1
                dacs_cols = jnp.where(lane_id == h0, jnp.sum(ds0, axis=1, keepdims=True), dacs_cols)
                dacs_cols = jnp.where(lane_id == h0 + 1, jnp.sum(ds1, axis=1, keepdims=True), dacs_cols)
                dacs_rows = jnp.where(row_id == h0, jnp.sum(ds0, axis=0, keepdims=True), dacs_rows)
                dacs_rows = jnp.where(row_id == h0 + 1, jnp.sum(ds1, axis=0, keepdims=True), dacs_rows)
            dcb16 = dcb.astype(BF16)
            dxc_ref[:, c_lanes] = (dc_acc + jnp.dot(dcb16, bg, preferred_element_type=F32)).astype(BF16)
            dxc_ref[:, b_lanes] = (db_acc + lax.dot_general(dcb16, cg, _DIMS["tn"],
                                                           preferred_element_type=F32)).astype(BF16)
        dxdt = dxdt_ref[...]
        dxc_ref[:, 0:D_INNER] = (dxdt * dt + dk_ref[...] * dyv).astype(BF16)
        acc_x[0:1, :] += jnp.sum(dyv * xs32, axis=0, keepdims=True)
        et = et_ref[...]
        dacs = _reduce_heads(dacs_ref[...], et, 2) + dacs_cols - dacs_rows.T
        dadt = _rev_cumsum_rows(dacs)
        ddraw = (_reduce_heads(dxdt * xs32, et, 1) + dadt * a_r) * sg_ref[...]
        ddraw = jnp.where(lane_id < N_HEADS, ddraw, 0.0)
        ddt_ref[...] = ddraw
        acc_r[0:1, :] += jnp.where(lane_id[0:1, :] < N_HEADS,
                                   jnp.sum(dadt * dt_r, axis=0, keepdims=True) * a_r, 0.0)
        acc_r[1:2, :] += jnp.sum(ddraw, axis=0, keepdims=True)

        @pl.when(step == nc - 1)
        def _():
            dd = _reduce_heads(acc_x[...], et_ref[...], 3)
            rid = lax.broadcasted_iota(jnp.int32, (8, 128), 0)
            small_ref[...] = acc_r[...] + jnp.where(rid == 2, pltpu.roll(dd, 2, 0), 0.0)

    rev = lambda w: pl.BlockSpec((CHUNK, w), lambda c: (nc - 1 - c, 0))
    return _pcall(
        body, name="ssd_bwd",
        out_shape=(jax.ShapeDtypeStruct((t_dim, CONV_DIM), BF16),
                   jax.ShapeDtypeStruct((t_dim, 128), F32),
                   jax.ShapeDtypeStruct((8, 128), F32)),
        grid=(nc,),
        in_specs=[rev(CONV_DIM), rev(128), rev(128), rev(128),
                  pl.BlockSpec((N_HEADS, CHUNK), lambda c: (0, nc - 1 - c)),
                  _const_spec((1, 128)), _const_spec((1, D_INNER)),
                  _const_spec((_EXACT_PIECES * 128, D_INNER)), _const_spec((D_INNER, 128)),
                  pl.BlockSpec((1, D_STATE, D_INNER), lambda c: (nc - 1 - c, 0, 0)),
                  rev(D_INNER)],
        out_specs=(rev(CONV_DIM), rev(128), _const_spec((8, 128))),
        scratch_shapes=[pltpu.VMEM((D_STATE, D_INNER), F32), pltpu.VMEM((CHUNK, D_INNER), F32),
                        pltpu.VMEM((CHUNK, D_INNER), F32), pltpu.VMEM((8, D_INNER), F32),
                        pltpu.VMEM((8, 128), F32)],
        compiler_params=_params("arbitrary"),
    )(xc, dt_r, acs_r, sg_r, acs_t, alog_r, dskip_x, e3_mat, et_mat, states, dy)


_GATE_GROUP = D_INNER // N_GROUPS


def _gate_fwd(y, z, g, *, tb=256):
    t_dim = y.shape[0]

    def body(y_ref, z_ref, g_ref, o_ref):
        for gi in range(N_GROUPS):
            lanes = slice(gi * _GATE_GROUP, (gi + 1) * _GATE_GROUP)
            zv = z_ref[:, lanes].astype(F32)
            wv = y_ref[:, lanes] * (zv * _sigmoid(zv))
            o_ref[:, lanes] = (wv * _rms(wv) * g_ref[:, lanes]).astype(BF16)

    return _pcall(
        body, name="gate_fwd", out_shape=jax.ShapeDtypeStruct((t_dim, D_INNER), BF16),
        grid=(t_dim // tb,),
        in_specs=[_row_spec(tb, D_INNER), _row_spec(tb, D_INNER), _const_spec((1, D_INNER))],
        out_specs=_row_spec(tb, D_INNER), compiler_params=_params("parallel"),
    )(y, z, g)


def _gate_bwd(dyn, y, z, g, *, tb=256):
    t_dim = y.shape[0]

    def body(d_ref, y_ref, z_ref, g_ref, dy_ref, dz_ref, dg_ref):
        @pl.when(pl.program_id(0) == 0)
        def _():
            dg_ref[...] = jnp.zeros_like(dg_ref)

        for gi in range(N_GROUPS):
            lanes = slice(gi * _GATE_GROUP, (gi + 1) * _GATE_GROUP)
            zv = z_ref[:, lanes].astype(F32)
            sg = _sigmoid(zv)
            sz = zv * sg
            yv = y_ref[:, lanes]
            wv = yv * sz
            r = _rms(wv)
            what = wv * r
            dv = d_ref[:, lanes]
            dwhat = dv * g_ref[:, lanes]
            dw = r * (dwhat - what * jnp.mean(dwhat * what, axis=-1, keepdims=True))
            dg_ref[0:1, lanes] += jnp.sum(dv * what, axis=0, keepdims=True)
            dy_ref[:, lanes] = dw * sz
            dz_ref[:, lanes] = (dw * yv * (sg * (1.0 + zv * (1.0 - sg)))).astype(BF16)

    return _pcall(
        body, name="gate_bwd",
        out_shape=(jax.ShapeDtypeStruct((t_dim, D_INNER), F32),
                   jax.ShapeDtypeStruct((t_dim, D_INNER), BF16),
                   jax.ShapeDtypeStruct((8, D_INNER), F32)),
        grid=(t_dim // tb,),
        in_specs=[_row_spec(tb, D_INNER), _row_spec(tb, D_INNER), _row_spec(tb, D_INNER),
                  _const_spec((1, D_INNER))],
        out_specs=(_row_spec(tb, D_INNER), _row_spec(tb, D_INNER), _const_spec((8, D_INNER))),
        compiler_params=_params("arbitrary"),
    )(dyn, y, z, g)


def _loss_head(h, g, target, *, tb=512):
    t_dim, d = h.shape

    def body(h_ref, g_ref, t_ref, dh_ref, dh16_ref, small_ref):
        @pl.when(pl.program_id(0) == 0)
        def _():
            small_ref[...] = jnp.zeros_like(small_ref)

        x = h_ref[...]
        r = _rms(x)
        xhat = x * r
        gv = g_ref[...]
        err = xhat * gv - t_ref[...]
        small_ref[1:2, :] += (0.5 / d) * jnp.sum(err * err, axis=0, keepdims=True)
        dyv = err * (1.0 / d)
        dxhat = dyv * gv
        dh = r * (dxhat - xhat * jnp.mean(dxhat * xhat, axis=-1, keepdims=True))
        dh_ref[...] = dh
        dh16_ref[...] = dh.astype(BF16)
        small_ref[0:1, :] += jnp.sum(dyv * xhat, axis=0, keepdims=True)

    return _pcall(
        body, name="loss_head",
        out_shape=(jax.ShapeDtypeStruct((t_dim, d), F32), jax.ShapeDtypeStruct((t_dim, d), BF16),
                   jax.ShapeDtypeStruct((8, d), F32)),
        grid=(t_dim // tb,),
        in_specs=[_row_spec(tb, d), _const_spec((1, d)), _row_spec(tb, d)],
        out_specs=(_row_spec(tb, d), _row_spec(tb, d), _const_spec((8, d))),
        compiler_params=_params("arbitrary"),
    )(h, g, target)


_ADAM_C1 = 1.0 / (1.0 - ADAM_B1 ** ADAM_STEP)
_ADAM_C2 = 1.0 / (1.0 - ADAM_B2 ** ADAM_STEP)


def _adamw_math(w, g, m, v):
    mn = ADAM_B1 * m + (1.0 - ADAM_B1) * g
    vn = ADAM_B2 * v + (1.0 - ADAM_B2) * (g * g)
    delta = -ADAM_LR * ((mn * _ADAM_C1) / (jnp.sqrt(vn * _ADAM_C2) + ADAM_EPS) + ADAM_WD * w)
    return delta, mn, vn


def _adamw(w, g, m, v, *, name, part=None, into=None):
    r_dim, c = w.shape
    rows = r_dim if part is None else r_dim // 2
    assert g.shape == (rows, c)
    tb = max(t for t in range(8, 513, 8) if rows % t == 0)
    first = 0 if part is None else part * (rows // tb)
    n_out = 3 if part is None else 4

    def body(w_ref, g_ref, m_ref, v_ref, *rest):
        outs = rest[-n_out:]
        gv = g_ref[...]
        outs[0][...], outs[1][...], outs[2][...] = _adamw_math(w_ref[...], gv, m_ref[...], v_ref[...])
        if part is not None:
            outs[3][...] = gv

    spec = pl.BlockSpec((tb, c), lambda i: (first + i, 0))
    sds = jax.ShapeDtypeStruct((r_dim, c), F32)
    in_specs = [spec, _row_spec(tb, c), spec, spec]
    operands = [w, g, m, v]
    aliases = {}
    if into is not None:
        in_specs += [_ANY] * n_out
        operands += list(into)
        aliases = {4 + i: i for i in range(n_out)}
    outs = _pcall(
        body, name=name, out_shape=(sds,) * n_out, grid=(rows // tb,),
        in_specs=in_specs, out_specs=(spec,) * n_out, input_output_aliases=aliases,
        compiler_params=_params("parallel"),
    )(*operands)
    return tuple(outs) if part is not None else tuple(outs) + (g,)


def _adamw_small(params, *, name):
    n = len(params)

    def body(*refs):
        ins, outs = refs[:4 * n], refs[4 * n:]
        for i in range(n):
            w_ref, g_ref, m_ref, v_ref = ins[4 * i:4 * i + 4]
            res = _adamw_math(w_ref[...], g_ref[...], m_ref[...], v_ref[...])
            for o_ref, r in zip(outs[3 * i:3 * i + 3], res):
                o_ref[...] = r

    vmem = pl.BlockSpec(memory_space=pltpu.VMEM)
    flat = [a for p in params for a in p]
    outs = _pcall(
        body, name=name,
        out_shape=tuple(jax.ShapeDtypeStruct(p[0].shape, F32) for p in params for _ in range(3)),
        in_specs=[vmem] * (4 * n), out_specs=(vmem,) * (3 * n),
    )(*flat)
    return [tuple(outs[3 * i:3 * i + 3]) for i in range(n)]


def _pair_sum(grad, recv, place, *, name):
    s_dim, r_dim, c = grad.shape
    half = r_dim // 2
    tb = 256 if half % 256 == 0 else half
    per_half = half // tb

    def body(place_ref, a_ref, b_ref, o16_ref, o32_ref):
        s = a_ref[...] + b_ref[...]
        o16_ref[...] = s.astype(BF16)

        @pl.when(pl.program_id(1) == place_ref[1])
        def _():
            o32_ref[...] = s[0]

    grid_spec = pltpu.PrefetchScalarGridSpec(
        num_scalar_prefetch=1, grid=(per_half, s_dim),
        in_specs=[pl.BlockSpec((1, tb, c), lambda i, s, p: (s, p[0] * per_half + i, 0)),
                  pl.BlockSpec((1, tb, c), lambda i, s, p: (s, i, 0))],
        out_specs=(pl.BlockSpec((1, tb, c), lambda i, s, p: (s, i, 0)),
                   pl.BlockSpec((tb, c), lambda i, s, p: (i, 0))))
    return _pcall(
        body, name=name, grid_spec=grid_spec,
        out_shape=(jax.ShapeDtypeStruct((s_dim, half, c), BF16), jax.ShapeDtypeStruct((half, c), F32)),
        compiler_params=_params("parallel", "arbitrary"),
    )(place, grad, recv)


def _chip_sum(own, recv, place, *, name):
    r_dim, c = own.shape
    tb = 256 if r_dim % 256 == 0 else r_dim

    def body(place_ref, a_ref, b_ref, o_ref):
        s = a_ref[...]
        for k in range(1, N_CHIPS):
            s = s + b_ref[k].astype(F32)
        o_ref[...] = s

    grid_spec = pltpu.PrefetchScalarGridSpec(
        num_scalar_prefetch=1, grid=(r_dim // tb,),
        in_specs=[pl.BlockSpec((tb, c), lambda i, p: (i, 0)),
                  pl.BlockSpec((N_CHIPS, tb, c), lambda i, p: (0, i, 0))],
        out_specs=pl.BlockSpec((None, tb, c), lambda i, p: (p[0], i, 0)))
    return _pcall(
        body, name=name, grid_spec=grid_spec, out_shape=jax.ShapeDtypeStruct((2, r_dim, c), F32),
        compiler_params=_params("parallel"),
    )(place, own, recv)


def _position():
    return lax.axis_index("x"), lax.axis_index("y"), lax.axis_index("c")


def _chip_peer(x, y, k):
    return x ^ (k >> 1), y ^ (k & 1)


_ANY = pl.BlockSpec(memory_space=pl.ANY)
_TOKEN = jax.ShapeDtypeStruct((8, 128), F32)


def _all_gather_weights(shards):
    n = len(shards)
    hops = N_CHIPS - 1

    def body(*refs):
        srcs, outs, done = refs[:n], refs[n:2 * n], refs[2 * n]
        send_sems, recv_sems = refs[2 * n + 1:]
        x, y, c = _position()
        me = 2 * x + y
        done[...] = jnp.zeros_like(done)

        def over_ici(w, k, chip, to):
            return pltpu.make_async_remote_copy(
                src_ref=srcs[w].at[c], dst_ref=outs[w].at[chip, c],
                send_sem=send_sems.at[w, k - 1], recv_sem=recv_sems.at[w, k - 1],
                device_id=to, device_id_type=MESH)

        def over_d2d(w, k, chip, half):
            return pltpu.make_async_remote_copy(
                src_ref=outs[w].at[chip, half], dst_ref=outs[w].at[chip, half],
                send_sem=send_sems.at[w, hops + k - 1], recv_sem=recv_sems.at[w, hops + k - 1],
                device_id=(x, y, 1 - c), device_id_type=MESH)

        sends = []
        for w in range(n):
            for k in range(1, N_CHIPS):
                px, py = _chip_peer(x, y, k)
                cp = over_ici(w, k, me, (px, py, c))
                cp.start()
                sends.append(cp)
        for w in range(n):
            for k in range(1, N_CHIPS):
                px, py = _chip_peer(x, y, k)
                over_ici(w, k, 2 * px + py, (px, py, c)).wait_recv()
                cp = over_d2d(w, k, 2 * px + py, c)
                cp.start()
                sends.append(cp)
        for w in range(n):
            for k in range(1, N_CHIPS):
                px, py = _chip_peer(x, y, k)
                over_d2d(w, k, 2 * px + py, 1 - c).wait_recv()
        for cp in sends:
            cp.wait_send()

    outs = _pcall(
        body, name="gather_weights",
        out_shape=tuple(jax.ShapeDtypeStruct((N_CHIPS,) + s.shape, s.dtype) for s in shards) + (_TOKEN,),
        in_specs=[_ANY] * n, out_specs=(_ANY,) * n + (pl.BlockSpec(memory_space=pltpu.VMEM),),
        scratch_shapes=[pltpu.SemaphoreType.DMA((n, 2 * hops)),
                        pltpu.SemaphoreType.DMA((n, 2 * hops))],
    )(*shards)
    return outs[:n], outs[n][0, 0]


def _pair_copies(srcs, lands, send_sems, recv_sems):
    x, y, c = _position()
    copies = []
    for w in range(len(srcs)):
        half = srcs[w].shape[1] // 2
        copies.append(pltpu.make_async_remote_copy(
            src_ref=srcs[w].at[:, pl.ds((1 - c) * half, half), :], dst_ref=lands[w],
            send_sem=send_sems.at[w], recv_sem=recv_sems.at[w],
            device_id=(x, y, 1 - c), device_id_type=MESH))
    return copies


def _chip_copies(srcs, lands, send_sems, recv_sems):
    x, y, c = _position()
    copies = []
    for w in range(len(srcs)):
        for k in range(1, N_CHIPS):
            px, py = _chip_peer(x, y, k)
            i = w * (N_CHIPS - 1) + k - 1
            copies.append(pltpu.make_async_remote_copy(
                src_ref=srcs[w].at[2 * px + py], dst_ref=lands[w].at[k],
                send_sem=send_sems.at[i], recv_sem=recv_sems.at[i],
                device_id=(px, py, c), device_id_type=MESH))
    return copies


def _gather_copies(srcs, lands, send_sems, recv_sems):
    x, y, c = _position()
    me = 2 * x + y
    copies = []
    for w in range(len(srcs)):
        for k in range(1, N_CHIPS):
            px, py = _chip_peer(x, y, k)
            i = w * (N_CHIPS - 1) + k - 1
            copies.append(pltpu.make_async_remote_copy(
                src_ref=srcs[w].at[c], dst_ref=lands[w].at[me, c],
                send_sem=send_sems.at[i], recv_sem=recv_sems.at[i],
                device_id=(px, py, c), device_id_type=MESH))
    return copies


def _exchange(name, copies_of, n_copies, srcs, land_shapes):
    n = len(srcs)

    def body(*refs):
        copies = copies_of(refs[:n], refs[n:2 * n], refs[2 * n], refs[2 * n + 1])
        for cp in copies:
            cp.start()
        for cp in copies:
            cp.wait_recv()
        for cp in copies:
            cp.wait_send()

    return _pcall(
        body, name=name, out_shape=tuple(land_shapes),
        in_specs=[_ANY] * n, out_specs=(_ANY,) * n,
        scratch_shapes=[pltpu.SemaphoreType.DMA((n_copies,)), pltpu.SemaphoreType.DMA((n_copies,))],
    )(*srcs)


_HBM = pl.BlockSpec(memory_space=pltpu.HBM)
_SEM = pl.BlockSpec(memory_space=pltpu.SEMAPHORE)
_DATAFLOW = pltpu.SideEffectType.DATAFLOW_SIDE_EFFECTING


def _exchange_start(name, copies_of, n_copies, srcs, land_shapes):
    n = len(srcs)
    lands = [lax.empty(s.shape, s.dtype) for s in land_shapes]

    def body(*refs):
        for cp in copies_of(refs[:n], refs[n:2 * n], refs[2 * n], refs[2 * n + 1]):
            cp.start()
        refs[-1][...] = jnp.zeros_like(refs[-1])

    through = [pltpu.HBM(a.shape, a.dtype) for a in list(srcs) + lands]
    outs = _pcall(
        body, name=name,
        out_shape=(pltpu.SemaphoreType.DMA((n_copies,)), pltpu.SemaphoreType.DMA((n_copies,)),
                   *through, jax.ShapeDtypeStruct((8, 128), F32)),
        in_specs=[_HBM] * (2 * n),
        out_specs=(_SEM, _SEM, *([_HBM] * (2 * n)), pl.BlockSpec(memory_space=pltpu.VMEM)),
        input_output_aliases={i: 2 + i for i in range(2 * n)},
        compiler_params=pltpu.CompilerParams(has_side_effects=_DATAFLOW),
    )(*[pltpu.with_memory_space_constraint(a, pltpu.HBM) for a in list(srcs) + lands])
    return outs[:-1], outs[-1][0, 0]


def _exchange_wait(name, copies_of, state, after):
    send_sems, recv_sems, through = state[0], state[1], state[2:]
    n = len(through) // 2
    if after.ndim == 0:
        after = jnp.broadcast_to(after, (8, 128))
    after = pltpu.with_memory_space_constraint(after, pltpu.HBM)

    def body(*refs):
        for cp in copies_of(refs[:n], refs[n:2 * n], refs[2 * n], refs[2 * n + 1]):
            cp.wait_send()
            cp.wait_recv()

    outs = _pcall(
        body, name=name,
        out_shape=tuple(pltpu.HBM(a.shape, a.dtype) for a in through),
        in_specs=[_HBM] * (2 * n) + [_SEM, _SEM, _HBM], out_specs=tuple([_HBM] * (2 * n)),
        input_output_aliases={i: i for i in range(2 * n)},
        compiler_params=pltpu.CompilerParams(has_side_effects=_DATAFLOW),
    )(*through, send_sems, recv_sems, after)
    return outs[:n], outs[n:]


def _forward_halves(lands, *, name):
    n = len(lands)
    hops = N_CHIPS - 1

    def body(*refs):
        ins, outs, done = refs[:n], refs[n:2 * n], refs[2 * n]
        send_sems, recv_sems = refs[2 * n + 1], refs[2 * n + 2]
        x, y, c = _position()
        done[...] = jnp.zeros_like(done)
        copies = []
        for w in range(n):
            for k in range(1, N_CHIPS):
                px, py = _chip_peer(x, y, k)
                i = w * hops + k - 1
                copies.append(pltpu.make_async_remote_copy(
                    src_ref=ins[w].at[2 * px + py, c], dst_ref=outs[w].at[2 * px + py, c],
                    send_sem=send_sems.at[i], recv_sem=recv_sems.at[i],
                    device_id=(x, y, 1 - c), device_id_type=MESH))
        for cp in copies:
            cp.start()
        for cp in copies:
            cp.wait_recv()
        for cp in copies:
            cp.wait_send()

    outs = _pcall(
        body, name=name,
        out_shape=tuple(jax.ShapeDtypeStruct(a.shape, a.dtype) for a in lands) + (_TOKEN,),
        in_specs=[_ANY] * n, out_specs=(_ANY,) * n + (pl.BlockSpec(memory_space=pltpu.VMEM),),
        input_output_aliases={i: i for i in range(n)},
        scratch_shapes=[pltpu.SemaphoreType.DMA((n * hops,)), pltpu.SemaphoreType.DMA((n * hops,))],
    )(*lands)
    return outs[:n], outs[n][0, 0]


def _pair_lands(grads):
    return [jax.ShapeDtypeStruct((g.shape[0], g.shape[1] // 2, g.shape[2]), F32) for g in grads]


def _same_lands(parts):
    return [jax.ShapeDtypeStruct(p.shape, p.dtype) for p in parts]


def _pair_gather_halves(halves, *, name):
    n = len(halves)

    def body(*refs):
        ins, outs = refs[:n], refs[n:2 * n]
        send_sems, recv_sems = refs[2 * n:]
        x, y, c = _position()
        sends = []
        for w in range(n):
            cp = pltpu.make_async_remote_copy(
                src_ref=ins[w].at[c], dst_ref=outs[w].at[c],
                send_sem=send_sems.at[w], recv_sem=recv_sems.at[w],
                device_id=(x, y, 1 - c), device_id_type=MESH)
            cp.start()
            sends.append(cp)
        for cp in sends:
            cp.wait_recv()
        for cp in sends:
            cp.wait_send()

    whole = _pcall(
        body, name=name,
        out_shape=tuple(jax.ShapeDtypeStruct(h.shape, F32) for h in halves),
        in_specs=[_ANY] * n, out_specs=(_ANY,) * n,
        input_output_aliases={i: i for i in range(n)},
        scratch_shapes=[pltpu.SemaphoreType.DMA((n,)), pltpu.SemaphoreType.DMA((n,))],
    )(*halves)
    return [w.reshape(2 * w.shape[1], w.shape[2]) for w in whole]


def _all_reduce_small(packed, *, name, sum_row0):
    r_dim, c = packed.shape

    def body(src_ref, out_ref, recv_ref, send_sems, recv_sems):
        x, y, c_ = _position()
        me = 4 * x + 2 * y + c_
        recv_ref[0] = src_ref[...]
        sends = []
        for k in range(1, N_DEV):
            peer = (x ^ (k >> 2), y ^ ((k >> 1) & 1), c_ ^ (k & 1))
            cp = pltpu.make_async_remote_copy(
                src_ref=src_ref, dst_ref=recv_ref.at[k],
                send_sem=send_sems.at[k - 1], recv_sem=recv_sems.at[k - 1],
                device_id=peer, device_id_type=MESH)
            cp.start()
            sends.append(cp)
        for cp in sends:
            cp.wait_recv()
        total = recv_ref[me]
        for d in range(1, N_DEV):
            total = total + recv_ref[d ^ me]
        if sum_row0:
            row0 = jnp.sum(total[0:1, :], axis=1, keepdims=True)
            rid = lax.broadcasted_iota(jnp.int32, total.shape, 0)
            total = jnp.where(rid == 0, row0, total)
        out_ref[...] = total
        for cp in sends:
            cp.wait_send()

    return _pcall(
        body, name=name, out_shape=jax.ShapeDtypeStruct((r_dim, c), F32),
        in_specs=[pl.BlockSpec(memory_space=pltpu.VMEM)],
        out_specs=pl.BlockSpec(memory_space=pltpu.VMEM),
        scratch_shapes=[pltpu.VMEM((N_DEV, r_dim, c), F32),
                        pltpu.SemaphoreType.DMA((N_DEV - 1,)), pltpu.SemaphoreType.DMA((N_DEV - 1,))],
    )(packed)


def _pad_lanes(v, width):
    return jnp.pad(v, ((0, 0), (0, width - v.shape[1])))


def _pad_rows(v, rows):
    pad = [(0, 0)] * v.ndim
    pad[-2] = (0, rows - v.shape[-2])
    return jnp.pad(v, pad)


_IN_PROJ_SHARD_ROWS = 1312


def _rows_1024(v):
    flat = v.reshape(-1)
    pad = (-flat.shape[0]) % D_MODEL
    return jnp.pad(flat, (0, pad)).reshape(-1, D_MODEL)


def _local_step(xs, target, pw, fetch, reduce_start, reduce_midway,
                conv_w, conv_b, gate_g,
                norm_mix_g, norm_mlp_g, pool_b, pool_scale, ssm_dt_bias, ssm_a_log, ssm_d, final_g):
    bias_r = _pad_lanes(ssm_dt_bias, 128)
    alog_r = _pad_lanes(ssm_a_log, 128)
    dskip_x = jnp.repeat(ssm_d, HEAD_DIM, axis=1)
    bias_c = ssm_dt_bias.reshape(N_HEADS, 1)
    alog_c = ssm_a_log.reshape(N_HEADS, 1)
    e_mat = _head_lane_matrix()
    e3_mat = jnp.tile(e_mat, (_EXACT_PIECES, 1))

    g_mix0, g_mix1 = norm_mix_g[0:1], norm_mix_g[1:2]
    g_mlp0, g_mlp1 = norm_mlp_g[0:1], norm_mlp_g[1:2]
    fg = final_g.reshape(1, D_MODEL)

    h1 = _pool_fwd(xs, g_mix0, pw, pool_b, pool_scale)
    hm0 = _rmsnorm_fwd(h1, g_mlp0, name="norm_mlp0")
    w1_0 = fetch("mlp0_up", hm0)
    u0 = _matmul(hm0, w1_0, "nn", name="mlp0_up", out_dtype=BF16, b_col_shards=True)
    w2_0 = fetch("mlp0_down", u0)
    h2 = _matmul(u0, w2_0, "nn", name="mlp0_down", a_relu2=True, add=h1)

    w_z, w_xbc, w_dt = fetch("in_proj", h2)
    hn1 = _rmsnorm_fwd(h2, g_mix1, name="norm_mix1")
    z = _matmul(hn1, w_z, "nt", name="in_proj_z", out_dtype=BF16)
    xbc = _matmul(hn1, w_xbc, "nt", name="in_proj_xbc", out_dtype=BF16)
    dt_raw = _matmul(hn1, w_dt, "nt", name="in_proj_dt")
    dt_raw_t = dt_raw[:, :N_HEADS].T
    xc = _conv_fwd(xbc, conv_w, conv_b)
    wout, w1_1, w2_1 = fetch("rest", xc)
    dt_r, acs_r, sg_r, acs_t = _ssd_decay(dt_raw, dt_raw_t, bias_r, bias_c, alog_r, alog_c)
    y, states = _ssd_fwd(xc, dt_r, acs_r, acs_t, dskip_x, e3_mat)
    yn = _gate_fwd(y, z, gate_g)
    h3 = _matmul(yn, wout, "nn", name="out_proj", add=h2)
    hm1 = _rmsnorm_fwd(h3, g_mlp1, name="norm_mlp1")
    u1 = _matmul(hm1, w1_1, "nn", name="mlp1_up", out_dtype=BF16, b_col_shards=True)
    h4 = _matmul(u1, w2_1, "nn", name="mlp1_down", a_relu2=True, add=h3)

    dh4, dh4_16, small_final = _loss_head(h4, fg, target)

    def mlp_bwd_weights(dh_out16, hm, u, w2_i, tag):
        du = _matmul(dh_out16, w2_i, "nt", name=tag + "_du", out_dtype=BF16, relu2_grad_of=u)
        dw2 = _matmul(u, dh_out16, "tn", name=tag + "_dw2", a_relu2=True)
        dw1 = _matmul(hm, du, "tn", name=tag + "_dw1", out_col_shards=N_CHIPS)
        return du, dw1, dw2.reshape(N_CHIPS, D_FF // N_CHIPS, D_MODEL)

    def mlp_bwd_input(du, dh_out, h_in, w1_i, g_i, tag):
        dhm = _matmul(du, w1_i, "nt", name=tag + "_dhm", b_col_shards=True)
        return _rmsnorm_bwd(dhm, h_in, g_i, dh_out, name=tag + "_norm_bwd")

    du1, dw1_1, dw2_1 = mlp_bwd_weights(dh4_16, hm1, u1, w2_1, "mlp1")
    dh3, dh3_16, dg_mlp1 = mlp_bwd_input(du1, dh4, h3, w1_1, g_mlp1, "mlp1")

    dyn = _matmul(dh3_16, wout, "nt", name="out_proj_dyn")
    dwout = _matmul(yn, dh3_16, "tn", name="out_proj_dw").reshape(N_CHIPS, D_INNER // N_CHIPS, D_MODEL)
    behind = reduce_start("mlp1_out", [dw1_1, dw2_1, dwout])
    dy, dz, dg_gate = _gate_bwd(dyn, y, z, gate_g + behind)
    behind = reduce_midway("mlp1_out", dz)
    dxc, ddt_raw, small_ssd = _ssd_bwd(xc, dt_r, acs_r, sg_r, acs_t, alog_r, dskip_x + behind,
                                       e3_mat, e_mat.T, states, dy)
    dv, dconv = _conv_bwd_act(xbc, dxc, conv_w, conv_b)
    dxbc = _conv_bwd_in(dv, conv_w)
    dhn1 = _matmul(ddt_raw, w_dt, "nn", name="in_proj_dt_dh")
    dhn1 = _matmul(dz, w_z, "nn", name="in_proj_z_dh", add=dhn1)
    dhn1 = _matmul(dxbc, w_xbc, "nn", name="in_proj_xbc_dh", add=dhn1)
    dw_z = _matmul(dz, hn1, "tn", name="in_proj_z_dw")
    dw_xbc = _matmul(dxbc, hn1, "tn", name="in_proj_xbc_dw")
    dw_dt = _matmul(ddt_raw, hn1, "tn", name="in_proj_dt_dw")
    dwin = jnp.concatenate([dw_z, dw_xbc, dw_dt[:N_HEADS]], axis=0)
    dwin = _pad_rows(dwin.reshape(N_CHIPS, IN_PROJ_DIM // N_CHIPS, D_MODEL), _IN_PROJ_SHARD_ROWS)
    behind = reduce_start("in_proj", [dwin])
    dh2, dh2_16, dg_mix1 = _rmsnorm_bwd(dhn1, h2, g_mix1 + behind, dh3, name="norm_mix1_bwd")
    behind = reduce_midway("in_proj", dh2_16)

    du0, dw1_0, dw2_0 = mlp_bwd_weights(dh2_16, hm0, u0, w2_0, "mlp0")
    dh1, _, dg_mlp0 = mlp_bwd_input(du0, dh2, h1, w1_0, g_mlp0 + behind, "mlp0")
    dx, dpw, small_pool = _pool_bwd(xs, g_mix0, pw, pool_b, pool_scale, dh1)
    dpw = jnp.transpose(dpw.reshape(4, N_CHIPS, POOL_GROUP // N_CHIPS, POOL_GROUP), (1, 0, 2, 3))
    dpw = dpw.reshape(N_CHIPS, 4 * (POOL_GROUP // N_CHIPS), POOL_GROUP)

    big = [dpw, dw1_0, dw2_0]
    rows = [
        small_final[1:2],
        small_final[0:1],
        small_pool[0:1], dg_mix1[0:1],
        dg_mlp0[0:1], dg_mlp1[0:1],
        small_pool[1:2], small_pool[2:3],
        _pad_lanes(small_ssd[0:3], D_MODEL),
        _rows_1024(dg_gate[0:1]),
        _rows_1024(dconv[0:CONV_K]),
        _rows_1024(dconv[CONV_K:CONV_K + 1]),
    ]
    return dx, big, rows


def kernel(x, norm_mix_g, norm_mlp_g, pool_w, pool_b, pool_scale, ssm_w_in, ssm_conv_w, ssm_conv_b, ssm_dt_bias, ssm_a_log, ssm_d, ssm_norm_g, ssm_w_out, mlp_w1, mlp_w2, final_g, loss_target, m_norm_mix_g, m_norm_mlp_g, m_pool_w, m_pool_b, m_pool_scale, m_ssm_w_in, m_ssm_conv_w, m_ssm_conv_b, m_ssm_dt_bias, m_ssm_a_log, m_ssm_d, m_ssm_norm_g, m_ssm_w_out, m_mlp_w1, m_mlp_w2, m_final_g, v_norm_mix_g, v_norm_mlp_g, v_pool_w, v_pool_b, v_pool_scale, v_ssm_w_in, v_ssm_conv_w, v_ssm_conv_b, v_ssm_dt_bias, v_ssm_a_log, v_ssm_d, v_ssm_norm_g, v_ssm_w_out, v_mlp_w1, v_mlp_w2, v_final_g):
    t_dim = x.shape[1]
    xs = x[0]
    target = loss_target[0]
    my_x, my_y, my_c = _position()
    my_chip = 2 * my_x + my_y

    def halves(w):
        return w.astype(BF16).reshape((2, w.shape[0] // 2) + w.shape[1:])

    def whole(gathered, own_shard):
        g = lax.dynamic_update_index_in_dim(gathered, own_shard, my_chip, axis=0)
        return g.reshape((N_CHIPS, 2 * g.shape[2]) + g.shape[3:])

    def gather_lands(own):
        return [jax.ShapeDtypeStruct((N_CHIPS,) + s.shape, s.dtype) for s in own]

    vec_cols = CONV_DIM // N_CHIPS
    vec_own = jnp.concatenate([ssm_conv_w[0], ssm_conv_b, _pad_lanes(ssm_norm_g, vec_cols)], axis=0)
    early_own = [halves(pool_w[0]), vec_own.reshape(2, (CONV_K + 2) // 2, vec_cols)]
    early, behind_early = _all_gather_weights(early_own)
    g_pool, g_vec = [whole(g, o) for g, o in zip(early, early_own)]
    pw = jnp.transpose(g_pool, (1, 0, 2, 3)).reshape(4, POOL_GROUP, POOL_GROUP)
    conv_w = jnp.transpose(g_vec[:, 0:CONV_K, :], (1, 0, 2)).reshape(CONV_K, CONV_DIM)
    conv_b = g_vec[:, CONV_K, :].reshape(1, CONV_DIM)
    gate_g = g_vec[:, CONV_K + 1, :D_INNER // N_CHIPS].reshape(1, D_INNER)

    def behind_it(zero, ws):
        return [halves(w + zero) for w in ws]

    fetches = {}
    up_own = behind_it(behind_early, [mlp_w1[0]])
    fetches["mlp0_up"], behind_gather = _exchange_start(
        "gather_mlp0_up_start", _gather_copies, len(up_own) * (N_CHIPS - 1), up_own, gather_lands(up_own))
    down_own = behind_it(behind_gather, [mlp_w2[0]])
    fetches["mlp0_down"], behind_gather = _exchange_start(
        "gather_mlp0_down_start", _gather_copies, len(down_own) * (N_CHIPS - 1), down_own, gather_lands(down_own))
    in_own = behind_it(behind_gather, [_pad_rows(ssm_w_in[0].T, _IN_PROJ_SHARD_ROWS)])
    fetches["in_proj"], behind_gather = _exchange_start(
        "gather_in_proj_start", _gather_copies, len(in_own) * (N_CHIPS - 1), in_own, gather_lands(in_own))

    def fetch(what, after):
        if what == "mlp0_up":
            own_thru, landed = _exchange_wait("gather_mlp0_up_wait", _gather_copies, fetches[what], after)
            landed, _ = _forward_halves(landed, name="forward_mlp0_up")
            return whole(landed[0], own_thru[0])
        if what == "mlp0_down":
            own_thru, landed = _exchange_wait("gather_mlp0_down_wait", _gather_copies, fetches[what], after)
            landed, _ = _forward_halves(landed, name="forward_mlp0_down")
            return whole(landed[0], own_thru[0]).reshape(D_FF, D_MODEL)
        if what == "in_proj":
            own_thru, landed = _exchange_wait("gather_in_proj_wait", _gather_copies, fetches["in_proj"], after)
            landed, behind = _forward_halves(landed, name="forward_in_proj")
            rest = behind_it(behind, [ssm_w_out[0], mlp_w1[1], mlp_w2[1]])
            fetches["rest"], behind = _exchange_start(
                "gather_rest_start", _gather_copies, len(rest) * (N_CHIPS - 1), rest, gather_lands(rest))
            win = whole(landed[0], own_thru[0])[:, :IN_PROJ_DIM // N_CHIPS].reshape(IN_PROJ_DIM, D_MODEL)
            w_dt = _pad_rows(win[D_INNER + CONV_DIM:], 128) + behind.astype(BF16)
            return win[:D_INNER], win[D_INNER:D_INNER + CONV_DIM], w_dt
        own_thru, landed = _exchange_wait("gather_rest_wait", _gather_copies, fetches["rest"], after)
        landed, _ = _forward_halves(landed, name="forward_rest")
        g_wout, w1_1, g_w2_1 = [whole(g, o) for g, o in zip(landed, own_thru)]
        return g_wout.reshape(D_INNER, D_MODEL), w1_1, g_w2_1.reshape(D_FF, D_MODEL)

    place = jnp.stack([my_c, my_chip]).astype(jnp.int32)
    waves = {}

    def reduce_start(wave, grads):
        waves[wave] = {}
        waves[wave]["pair"], behind = _exchange_start(
            "pair_%s_start" % wave, _pair_copies, len(grads), grads, _pair_lands(grads))
        return behind

    def reduce_midway(wave, after):
        st = waves[wave]
        grads, recv = _exchange_wait("pair_%s_wait" % wave, _pair_copies, st["pair"], after)
        sums = [_pair_sum(g, r, place, name="pair_sum_%s_%d" % (wave, i))
                for i, (g, r) in enumerate(zip(grads, recv))]
        st["f32"] = [s32 for _, s32 in sums]
        b16 = [s16 for s16, _ in sums]
        st["chip"], behind = _exchange_start(
            "chip_%s_start" % wave, _chip_copies, len(b16) * (N_CHIPS - 1), b16, _same_lands(b16))
        return behind

    def reduce_finish(wave, after):
        st = waves[wave]
        _, got = _exchange_wait("chip_%s_wait" % wave, _chip_copies, st["chip"], after)
        return [_chip_sum(s32, r, place, name="chip_sum_%s_%d" % (wave, i))
                for i, (s32, r) in enumerate(zip(st["f32"], got))]

    dx, big0, rows = _local_step(xs, target, pw, fetch, reduce_start, reduce_midway,
                                 conv_w, conv_b, gate_g,
                                 norm_mix_g + behind_gather, norm_mlp_g, pool_b, pool_scale,
                                 ssm_dt_bias, ssm_a_log, ssm_d, final_g)

    behind = reduce_start("layer0", big0)
    small = jnp.concatenate(rows, axis=0)
    small = jnp.pad(small, ((0, (-small.shape[0]) % 8), (0, 0))) + behind
    small = _all_reduce_small(small, name="all_reduce_small", sum_row0=True)
    behind = reduce_midway("layer0", small)
    h_w1_1, h_w2_1, h_wout = reduce_finish("mlp1_out", behind)
    (h_win,) = reduce_finish("in_proj", behind)
    g_w1_1, g_w2_1, g_wout_s, g_win_s = _pair_gather_halves([h_w1_1, h_w2_1, h_wout, h_win],
                                                            name="pair_gather_layer1")
    loss = small[0, 0]
    g_final = small[1]
    g_norm_mix = small[2:4]
    g_norm_mlp = small[4:6]
    g_pool_b, g_pool_scale = small[6:7], small[7:8]
    g_alog, g_dtb, g_dsk = small[8:9, :N_HEADS], small[9:10, :N_HEADS], small[10:11, :N_HEADS]
    g_gate_full = small[11:13].reshape(1, D_INNER)
    g_convw_full = small[13:25].reshape(CONV_K, CONV_DIM)
    g_convb_full = small[25:28].reshape(1, CONV_DIM)
    g_gate = lax.dynamic_slice_in_dim(g_gate_full, my_chip * (D_INNER // N_CHIPS), D_INNER // N_CHIPS, axis=1)
    g_convw = lax.dynamic_slice_in_dim(g_convw_full, my_chip * (CONV_DIM // N_CHIPS), CONV_DIM // N_CHIPS, axis=1)
    g_convb = lax.dynamic_slice_in_dim(g_convb_full, my_chip * (CONV_DIM // N_CHIPS), CONV_DIM // N_CHIPS, axis=1)

    grads = {
        "norm_mix_g": g_norm_mix, "norm_mlp_g": g_norm_mlp,
        "pool_b": g_pool_b, "pool_scale": g_pool_scale,
        "ssm_conv_w": g_convw.reshape(ssm_conv_w.shape),
        "ssm_conv_b": g_convb, "ssm_dt_bias": g_dtb, "ssm_a_log": g_alog, "ssm_d": g_dsk,
        "ssm_norm_g": g_gate, "ssm_w_out": g_wout_s.reshape(ssm_w_out.shape),
        "final_g": g_final,
    }
    weights = dict(norm_mix_g=norm_mix_g, norm_mlp_g=norm_mlp_g, pool_w=pool_w, pool_b=pool_b,
                   pool_scale=pool_scale, ssm_w_in=ssm_w_in, ssm_conv_w=ssm_conv_w, ssm_conv_b=ssm_conv_b,
                   ssm_dt_bias=ssm_dt_bias, ssm_a_log=ssm_a_log, ssm_d=ssm_d, ssm_norm_g=ssm_norm_g,
                   ssm_w_out=ssm_w_out, mlp_w1=mlp_w1, mlp_w2=mlp_w2, final_g=final_g)
    moms = dict(norm_mix_g=(m_norm_mix_g, v_norm_mix_g), norm_mlp_g=(m_norm_mlp_g, v_norm_mlp_g),
                pool_w=(m_pool_w, v_pool_w), pool_b=(m_pool_b, v_pool_b),
                pool_scale=(m_pool_scale, v_pool_scale), ssm_w_in=(m_ssm_w_in, v_ssm_w_in),
                ssm_conv_w=(m_ssm_conv_w, v_ssm_conv_w), ssm_conv_b=(m_ssm_conv_b, v_ssm_conv_b),
                ssm_dt_bias=(m_ssm_dt_bias, v_ssm_dt_bias), ssm_a_log=(m_ssm_a_log, v_ssm_a_log),
                ssm_d=(m_ssm_d, v_ssm_d), ssm_norm_g=(m_ssm_norm_g, v_ssm_norm_g),
                ssm_w_out=(m_ssm_w_out, v_ssm_w_out), mlp_w1=(m_mlp_w1, v_mlp_w1),
                mlp_w2=(m_mlp_w2, v_mlp_w2), final_g=(m_final_g, v_final_g))
    names = list(weights)
    big_names = ("pool_w", "ssm_w_in", "ssm_w_out", "mlp_w1", "mlp_w2")
    deltas, new_m, new_v = {}, {}, {}

    def as_rows(nm, a):
        return a[0].T if nm == "ssm_w_in" else a.reshape(-1, a.shape[-1])

    def from_rows(nm, r):
        return r.T[None] if nm == "ssm_w_in" else r.reshape(weights[nm].shape)

    def update(nm, grad_rows, layer=None, into=None):
        return _adamw(as_rows(nm, weights[nm]), grad_rows, as_rows(nm, moms[nm][0]), as_rows(nm, moms[nm][1]),
                      name="adamw_%s_%s" % (nm, layer), part=layer, into=into)

    def keep(nm, results):
        deltas[nm], new_m[nm], new_v[nm], grads[nm] = [from_rows(nm, r) for r in results]

    keep("ssm_w_in", update("ssm_w_in", g_win_s[:IN_PROJ_DIM // N_CHIPS]))
    keep("ssm_w_out", update("ssm_w_out", g_wout_s))
    w1_done = update("mlp_w1", g_w1_1, layer=1)
    w2_done = update("mlp_w2", g_w2_1, layer=1)
    small_names = [nm for nm in names if nm not in big_names]
    small_done = _adamw_small(
        [tuple(as_rows(nm, a) for a in (weights[nm], grads[nm], moms[nm][0], moms[nm][1])) for nm in small_names],
        name="adamw_small")
    for nm, (d_, m_, v_) in zip(small_names, small_done):
        deltas[nm], new_m[nm], new_v[nm] = [from_rows(nm, r) for r in (d_, m_, v_)]

    above = (deltas["ssm_w_in"][0, 0, 0] + deltas["ssm_w_out"][0, 0, 0] + w1_done[0][-1, -1]
             + w2_done[0][-1, -1] + small_done[0][0][0, 0])
    g_pool_w, g_w1_0, g_w2_0 = _pair_gather_halves(reduce_finish("layer0", above), name="pair_gather_layer0")
    keep("mlp_w1", update("mlp_w1", g_w1_0, layer=0, into=w1_done))
    keep("mlp_w2", update("mlp_w2", g_w2_0, layer=0, into=w2_done))
    keep("pool_w", update("pool_w", g_pool_w))

    grad_x = dx.reshape(x.shape)
    out_grads = [grads[nm].reshape(weights[nm].shape) for nm in names]
    return (loss, grad_x, *out_grads, *[deltas[nm] for nm in names],
            *[new_m[nm] for nm in names], *[new_v[nm] for nm in names])
```

```python
import functools

import jax
import jax.numpy as jnp
from jax import lax
from jax.experimental import pallas as pl
from jax.experimental.pallas import tpu as pltpu

F32 = jnp.float32
BF16 = jnp.bfloat16
MESH = pl.DeviceIdType.MESH

D_MODEL = 1024
RMS_EPS = 1e-5
POOL_WINDOWS = (2, 4, 8, 16)
POOL_GROUP = 256
POOL_HALO = 16
D_INNER = 2048
HEAD_DIM = 64
N_HEADS = 32
N_GROUPS = 4
HEADS_PER_GROUP = 8
D_STATE = 128
CONV_K = 4
CONV_HALO = 8
CHUNK = 128
CONV_DIM = 3072
IN_PROJ_DIM = 5152
D_FF = 4096
N_CHIPS = 4
N_DEV = 8

ADAM_LR = 0.001
ADAM_B1 = 0.9
ADAM_B2 = 0.999
ADAM_EPS = 1e-08
ADAM_WD = 0.01
ADAM_STEP = 10

VMEM_LIMIT = 56 * 1024 * 1024
NEG_INF = float("-inf")


def _pcall(body, **kw):
    return pl.pallas_call(body, **kw)


def _params(*sem):
    return pltpu.CompilerParams(dimension_semantics=sem, vmem_limit_bytes=VMEM_LIMIT)


def _sigmoid(v):
    return 1.0 / (1.0 + jnp.exp(-v))


def _row_spec(tb, d, nb=None, reverse=False):
    if reverse:
        return pl.BlockSpec((tb, d), lambda i: (nb - 1 - i, 0))
    return pl.BlockSpec((tb, d), lambda i: (i, 0))


def _const_spec(shape):
    return pl.BlockSpec(shape, lambda *_: tuple(0 for _ in shape))


_DIMS = {"nn": (((1,), (0,)), ((), ())),
         "nt": (((1,), (1,)), ((), ())),
         "tn": (((0,), (0,)), ((), ()))}


_MATMUL_VMEM_BUDGET = 40 * 1024 * 1024


def _matmul_tiles(m_dim, n_dim, k_dim, a_bytes, b_bytes, mn_bytes):
    tm, tn = min(m_dim, 1024), min(n_dim, 1024)
    while 2 * (tm * k_dim * a_bytes + tn * k_dim * b_bytes + tm * tn * mn_bytes) > _MATMUL_VMEM_BUDGET:
        if tm >= tn:
            tm //= 2
        else:
            tn //= 2
    return tm, tn


def _matmul(a, b, mode, *, name, out_dtype=F32, a_relu2=False, add=None, relu2_grad_of=None,
            out_col_shards=1, b_col_shards=False, norm_bwd=None):
    if mode == "tn":
        k_dim, m_dim = a.shape
    else:
        m_dim, k_dim = a.shape
    if b_col_shards:
        n_shards, shard_cols = b.shape[0], b.shape[2]
        n_dim = n_shards * shard_cols if mode == "nn" else b.shape[1]
    else:
        n_dim = b.shape[0] if mode == "nt" else b.shape[1]
    mn_bytes = jnp.dtype(out_dtype).itemsize
    if relu2_grad_of is not None:
        mn_bytes += relu2_grad_of.dtype.itemsize
    if add is not None:
        mn_bytes += add.dtype.itemsize
    if norm_bwd is not None:
        assert out_dtype == F32 and out_col_shards == 1
        mn_bytes += 4 + 4 + 2 + 8
    tm, tn = _matmul_tiles(m_dim, n_dim, k_dim, a.dtype.itemsize, b.dtype.itemsize, mn_bytes)
    if norm_bwd is not None:
        while tn < n_dim:
            tm, tn = tm // 2, tn * 2
    assert m_dim % tm == 0 and n_dim % tn == 0
    a_spec = (pl.BlockSpec((k_dim, tm), lambda i, j: (0, i)) if mode == "tn"
              else pl.BlockSpec((tm, k_dim), lambda i, j: (i, 0)))
    if b_col_shards and mode == "nn":
        assert shard_cols % tn == 0
        per_shard = shard_cols // tn
        b_spec = pl.BlockSpec((None, k_dim, tn), lambda i, j: (j // per_shard, 0, j % per_shard))
    elif b_col_shards:
        assert mode == "nt" and k_dim == n_shards * shard_cols
        b_spec = pl.BlockSpec((n_shards, tn, shard_cols), lambda i, j: (0, j, 0))
    else:
        b_spec = (pl.BlockSpec((tn, k_dim), lambda i, j: (j, 0)) if mode == "nt"
                  else pl.BlockSpec((k_dim, tn), lambda i, j: (0, j)))
    mn_spec = pl.BlockSpec((tm, tn), lambda i, j: (i, j))
    operands, in_specs = [a, b], [a_spec, b_spec]
    if relu2_grad_of is not None:
        operands.append(relu2_grad_of)
        in_specs.append(mn_spec)
    if add is not None:
        operands.append(add)
        in_specs.append(mn_spec)
    if norm_bwd is not None:
        h_in, g_in, dres_in = norm_bwd
        operands += [h_in, g_in, dres_in]
        in_specs += [mn_spec, pl.BlockSpec((1, n_dim), lambda i, j: (0, 0)), mn_spec]
        out_shape = (jax.ShapeDtypeStruct((m_dim, n_dim), F32), jax.ShapeDtypeStruct((m_dim, n_dim), BF16),
                     jax.ShapeDtypeStruct((8, n_dim), F32))
        out_spec = (mn_spec, mn_spec, pl.BlockSpec((8, n_dim), lambda i, j: (0, 0)))
    elif out_col_shards == 1:
        out_shape = jax.ShapeDtypeStruct((m_dim, n_dim), out_dtype)
        out_spec = mn_spec
    else:
        n_shard = n_dim // out_col_shards
        assert n_shard % tn == 0
        per = n_shard // tn
        out_shape = jax.ShapeDtypeStruct((out_col_shards, m_dim, n_shard), out_dtype)
        out_spec = pl.BlockSpec((None, tm, tn), lambda i, j: (j // per, i, j % per))

    n_in = len(operands)

    def body(*refs):
        a_ref, b_ref, o_ref = refs[0], refs[1], refs[n_in]
        av = a_ref[...]
        if a_relu2:
            av = jnp.maximum(av, 0)
            av = av * av
        if b_col_shards and mode == "nt":
            r = None
            for s in range(n_shards):
                part = lax.dot_general(av[:, s * shard_cols:(s + 1) * shard_cols].astype(BF16),
                                       b_ref[s].astype(BF16), _DIMS[mode], preferred_element_type=F32)
                r = part if r is None else r + part
        else:
            r = lax.dot_general(av.astype(BF16), b_ref[...].astype(BF16), _DIMS[mode],
                                preferred_element_type=F32)
        nxt = 2
        if relu2_grad_of is not None:
            r = r * (2.0 * jnp.maximum(refs[nxt][...].astype(F32), 0.0))
            nxt += 1
        if add is not None:
            r = r + refs[nxt][...]
            nxt += 1
        if norm_bwd is None:
            o_ref[...] = r.astype(out_dtype)
            return
        h_ref, g_ref, dres_ref = refs[nxt:nxt + 3]
        dh16_ref, dg_ref = refs[n_in + 1], refs[n_in + 2]

        @pl.when(pl.program_id(0) == 0)
        def _():
            dg_ref[...] = jnp.zeros_like(dg_ref)

        x = h_ref[...]
        rr = _rms(x)
        xhat = x * rr
        dxhat = r * g_ref[...]
        dh = dres_ref[...] + rr * (dxhat - xhat * jnp.mean(dxhat * xhat, axis=-1, keepdims=True))
        o_ref[...] = dh
        dh16_ref[...] = dh.astype(BF16)
        dg_ref[0:1, :] += jnp.sum(r * xhat, axis=0, keepdims=True)

    return _pcall(
        body, name=name, out_shape=out_shape,
        grid=(m_dim // tm, n_dim // tn),
        in_specs=in_specs, out_specs=out_spec,
        compiler_params=(_params("parallel", "parallel") if norm_bwd is None
                         else _params("arbitrary", "arbitrary")),
    )(*operands)


def _rms(x):
    return lax.rsqrt(jnp.mean(x * x, axis=-1, keepdims=True) + RMS_EPS)


def _rmsnorm_fwd(h, g, *, name, tb=512):
    t_dim, d = h.shape

    def body(h_ref, g_ref, o_ref):
        x = h_ref[...]
        o_ref[...] = (x * _rms(x) * g_ref[...]).astype(BF16)

    return _pcall(
        body, name=name, out_shape=jax.ShapeDtypeStruct((t_dim, d), BF16),
        grid=(t_dim // tb,), in_specs=[_row_spec(tb, d), _const_spec((1, d))],
        out_specs=_row_spec(tb, d), compiler_params=_params("parallel"),
    )(h, g)


def _pool_mixed(ext, hn, t0, tb):
    t = t0 + lax.broadcasted_iota(jnp.int32, (tb, 1), 0)
    parts = []
    for gi, w in enumerate(POOL_WINDOWS):
        lanes = slice(gi * POOL_GROUP, (gi + 1) * POOL_GROUP)
        s = ext[:, lanes]
        k = 1
        while k < w:
            s = s + pltpu.roll(s, k, 0)
            k *= 2
        cnt = jnp.minimum(t + 1, w).astype(F32)
        parts.append(s[POOL_HALO:, :] / cnt - hn[:, lanes])
    return parts


def _pool_fwd(x, g, pw, pb, ps, *, tb=512):
    t_dim, d = x.shape

    def body(x_ref, g_ref, pw_ref, pb_ref, ps_ref, o_ref, ext_ref):
        i = pl.program_id(0)

        @pl.when(i == 0)
        def _():
            ext_ref[0:POOL_HALO, :] = jnp.zeros((POOL_HALO, d), F32)

        xv = x_ref[...]
        hn = xv * _rms(xv) * g_ref[...]
        ext_ref[POOL_HALO:, :] = hn
        mixed = _pool_mixed(ext_ref[...], hn, i * tb, tb)
        for gi in range(len(POOL_WINDOWS)):
            lanes = slice(gi * POOL_GROUP, (gi + 1) * POOL_GROUP)
            out = jnp.dot(mixed[gi].astype(BF16), pw_ref[gi], preferred_element_type=F32)
            o_ref[:, lanes] = xv[:, lanes] + (out + pb_ref[:, lanes]) * ps_ref[:, lanes]
        ext_ref[0:POOL_HALO, :] = hn[tb - POOL_HALO:, :]

    return _pcall(
        body, name="pool_fwd", out_shape=jax.ShapeDtypeStruct((t_dim, d), F32),
        grid=(t_dim // tb,),
        in_specs=[_row_spec(tb, d), _const_spec((1, d)), _const_spec((4, POOL_GROUP, POOL_GROUP)),
                  _const_spec((1, d)), _const_spec((1, d))],
        out_specs=_row_spec(tb, d),
        scratch_shapes=[pltpu.VMEM((POOL_HALO + tb, d), F32)],
        compiler_params=_params("arbitrary"),
    )(x, g, pw, pb, ps)


def _pool_bwd(x, g, pw, pb, ps, dh1, *, tb=512):
    t_dim, d = x.shape
    nb = t_dim // tb
    halo_per_block = tb // POOL_HALO

    def body(x_ref, xprev_ref, g_ref, pw_ref, pb_ref, ps_ref, dh1_ref,
             dx_ref, dpw_ref, small_ref, ext_ref, dext_ref):
        i = pl.program_id(0)
        blk = nb - 1 - i

        @pl.when(i == 0)
        def _():
            dpw_ref[...] = jnp.zeros_like(dpw_ref)
            small_ref[...] = jnp.zeros_like(small_ref)
            dext_ref[tb:, :] = jnp.zeros((POOL_HALO, d), F32)

        gv = g_ref[...]
        xv = x_ref[...]
        r = _rms(xv)
        xhat = xv * r
        hn = xhat * gv
        xp = xprev_ref[...]
        hprev = xp * _rms(xp) * gv * (blk > 0).astype(F32)
        ext_ref[0:POOL_HALO, :] = hprev
        ext_ref[POOL_HALO:, :] = hn
        mixed = _pool_mixed(ext_ref[...], hn, blk * tb, tb)

        dout = dh1_ref[...]
        t = blk * tb + lax.broadcasted_iota(jnp.int32, (tb, 1), 0)
        for gi, w in enumerate(POOL_WINDOWS):
            lanes = slice(gi * POOL_GROUP, (gi + 1) * POOL_GROUP)
            mb = mixed[gi].astype(BF16)
            pre = jnp.dot(mb, pw_ref[gi], preferred_element_type=F32) + pb_ref[:, lanes]
            dg_out = dout[:, lanes]
            small_ref[2:3, lanes] += jnp.sum(dg_out * pre, axis=0, keepdims=True)
            dpre = dg_out * ps_ref[:, lanes]
            small_ref[1:2, lanes] += jnp.sum(dpre, axis=0, keepdims=True)
            dpb16 = dpre.astype(BF16)
            dpw_ref[gi] += lax.dot_general(mb, dpb16, _DIMS["tn"], preferred_element_type=F32)
            dmixed = lax.dot_general(dpb16, pw_ref[gi], _DIMS["nt"], preferred_element_type=F32)
            cnt = jnp.minimum(t + 1, w).astype(F32)
            dq = dmixed / cnt
            dext_ref[0:tb, lanes] = dq
            s = dext_ref[:, lanes]
            k = 1
            while k < w:
                s = s + pltpu.roll(s, tb + POOL_HALO - k, 0)
                k *= 2
            dhn = s[0:tb, :] - dmixed
            dext_ref[tb:, lanes] = dq[0:POOL_HALO, :]
            small_ref[0:1, lanes] += jnp.sum(dhn * xhat[:, lanes], axis=0, keepdims=True)
            ext_ref[POOL_HALO:, lanes] = dhn * gv[:, lanes]
        dxhat = ext_ref[POOL_HALO:, :]
        dx_ref[...] = dout + r * (dxhat - xhat * jnp.mean(dxhat * xhat, axis=-1, keepdims=True))

    return _pcall(
        body, name="pool_bwd",
        out_shape=(jax.ShapeDtypeStruct((t_dim, d), F32),
                   jax.ShapeDtypeStruct((4, POOL_GROUP, POOL_GROUP), F32),
                   jax.ShapeDtypeStruct((8, d), F32)),
        grid=(nb,),
        in_specs=[_row_spec(tb, d, nb, True),
                  pl.BlockSpec((POOL_HALO, d),
                               lambda i: (jnp.maximum((nb - 1 - i) * halo_per_block - 1, 0), 0)),
                  _const_spec((1, d)), _const_spec((4, POOL_GROUP, POOL_GROUP)),
                  _const_spec((1, d)), _const_spec((1, d)), _row_spec(tb, d, nb, True)],
        out_specs=(_row_spec(tb, d, nb, True), _const_spec((4, POOL_GROUP, POOL_GROUP)),
                   _const_spec((8, d))),
        scratch_shapes=[pltpu.VMEM((POOL_HALO + tb, d), F32), pltpu.VMEM((tb + POOL_HALO, d), F32)],
        compiler_params=_params("arbitrary"),
    )(x, x, g, pw, pb, ps, dh1)


_CONV_CB = 1024
_STRIP = 16


def _strips(tb, fn, unroll=4):
    def step(i, carry):
        fn(pl.multiple_of(i * _STRIP, _STRIP))
        return carry
    lax.fori_loop(0, tb // _STRIP, step, 0, unroll=unroll)


def _conv_taps(ext_ref, r0, w):
    shifted = [ext_ref[CONV_HALO + r0 - sh:CONV_HALO + r0 - sh + _STRIP, :] for sh in range(CONV_K)]
    acc = shifted[0] * w[CONV_K - 1:CONV_K, :]
    for sh in range(1, CONV_K):
        acc = acc + shifted[sh] * w[CONV_K - 1 - sh:CONV_K - sh, :]
    return shifted, acc


def _conv_fwd(u, w, b, *, tb=512):
    t_dim, c = u.shape
    cb = _CONV_CB

    def body(u_ref, w_ref, b_ref, o_ref, ext_ref):
        @pl.when(pl.program_id(1) == 0)
        def _():
            ext_ref[0:CONV_HALO, :] = jnp.zeros((CONV_HALO, cb), F32)

        wv = w_ref[...]
        bv = b_ref[...]

        def fill(r0):
            ext_ref[pl.ds(CONV_HALO + r0, _STRIP), :] = u_ref[pl.ds(r0, _STRIP), :].astype(F32)

        _strips(tb, fill)
        for r0 in range(0, tb, _STRIP):
            v = _conv_taps(ext_ref, r0, wv)[1] + bv
            o_ref[r0:r0 + _STRIP, :] = (v * _sigmoid(v)).astype(BF16)
        ext_ref[0:CONV_HALO, :] = ext_ref[tb:tb + CONV_HALO, :]

    blk = pl.BlockSpec((tb, cb), lambda j, t: (t, j))
    return _pcall(
        body, name="conv_fwd", out_shape=jax.ShapeDtypeStruct((t_dim, c), BF16),
        grid=(c // cb, t_dim // tb),
        in_specs=[blk, pl.BlockSpec((CONV_K, cb), lambda j, t: (0, j)),
                  pl.BlockSpec((1, cb), lambda j, t: (0, j))],
        out_specs=blk,
        scratch_shapes=[pltpu.VMEM((CONV_HALO + tb, cb), F32)],
        compiler_params=_params("parallel", "arbitrary"),
    )(u, w, b)


def _conv_bwd_act(u, dxc, w, b, *, tb=512):
    t_dim, c = u.shape
    cb = _CONV_CB
    half = _STRIP // 2

    def body(u_ref, d_ref, w_ref, b_ref, dv_ref, dwb_ref, ext_ref, acc_ref):
        @pl.when(pl.program_id(1) == 0)
        def _():
            ext_ref[0:CONV_HALO, :] = jnp.zeros((CONV_HALO, cb), F32)
            dwb_ref[...] = jnp.zeros_like(dwb_ref)

        acc_ref[...] = jnp.zeros_like(acc_ref)
        wv = w_ref[...]
        bv = b_ref[...]

        def fill(r0):
            ext_ref[pl.ds(CONV_HALO + r0, _STRIP), :] = u_ref[pl.ds(r0, _STRIP), :].astype(F32)

        _strips(tb, fill)
        for r0 in range(0, tb, _STRIP):
            shifted, v = _conv_taps(ext_ref, r0, wv)
            v = v + bv
            sg = _sigmoid(v)
            dv = d_ref[r0:r0 + _STRIP, :].astype(F32) * (sg * (1.0 + v * (1.0 - sg)))
            dv_ref[r0:r0 + _STRIP, :] = dv.astype(BF16)
            acc_ref[CONV_K] += dv[0:half, :] + dv[half:, :]
            for sh in range(CONV_K):
                p = dv * shifted[sh]
                acc_ref[CONV_K - 1 - sh] += p[0:half, :] + p[half:, :]
        for k in range(CONV_K + 1):
            dwb_ref[k:k + 1, :] += jnp.sum(acc_ref[k], axis=0, keepdims=True)
        ext_ref[0:CONV_HALO, :] = ext_ref[tb:tb + CONV_HALO, :]

    blk = pl.BlockSpec((tb, cb), lambda j, t: (t, j))
    return _pcall(
        body, name="conv_bwd_act",
        out_shape=(jax.ShapeDtypeStruct((t_dim, c), BF16), jax.ShapeDtypeStruct((8, c), F32)),
        grid=(c // cb, t_dim // tb),
        in_specs=[blk, blk, pl.BlockSpec((CONV_K, cb), lambda j, t: (0, j)),
                  pl.BlockSpec((1, cb), lambda j, t: (0, j))],
        out_specs=(blk, pl.BlockSpec((8, cb), lambda j, t: (0, j))),
        scratch_shapes=[pltpu.VMEM((CONV_HALO + tb, cb), F32), pltpu.VMEM((CONV_K + 1, half, cb), F32)],
        compiler_params=_params("parallel", "arbitrary"),
    )(u, dxc, w, b)


def _conv_bwd_in(dv, w, *, tb=512):
    t_dim, c = dv.shape
    cb = _CONV_CB
    nb = t_dim // tb

    def body(dv_ref, w_ref, du_ref, ext_ref):
        @pl.when(pl.program_id(1) == 0)
        def _():
            ext_ref[tb:, :] = jnp.zeros((CONV_HALO, cb), F32)

        wv = w_ref[...]

        def fill(r0):
            ext_ref[pl.ds(r0, _STRIP), :] = dv_ref[pl.ds(r0, _STRIP), :].astype(F32)

        _strips(tb, fill)
        for r0 in range(0, tb, _STRIP):
            acc = ext_ref[r0:r0 + _STRIP, :] * wv[CONV_K - 1:CONV_K, :]
            for sh in range(1, CONV_K):
                acc = acc + ext_ref[r0 + sh:r0 + sh + _STRIP, :] * wv[CONV_K - 1 - sh:CONV_K - sh, :]
            du_ref[r0:r0 + _STRIP, :] = acc.astype(BF16)
        ext_ref[tb:, :] = ext_ref[0:CONV_HALO, :]

    blk = pl.BlockSpec((tb, cb), lambda j, t: (nb - 1 - t, j))
    return _pcall(
        body, name="conv_bwd_in", out_shape=jax.ShapeDtypeStruct((t_dim, c), BF16),
        grid=(c // cb, nb),
        in_specs=[blk, pl.BlockSpec((CONV_K, cb), lambda j, t: (0, j))],
        out_specs=blk,
        scratch_shapes=[pltpu.VMEM((tb + CONV_HALO, cb), F32)],
        compiler_params=_params("parallel", "arbitrary"),
    )(dv, w)


def _softplus(v):
    e = jnp.exp(-jnp.abs(v))
    w = 1.0 + e
    log1p = jnp.where(w == 1.0, e, jnp.log(w) * e / jnp.where(w == 1.0, 1.0, w - 1.0))
    return jnp.maximum(v, 0.0) + log1p


def _cumsum_rows(v):
    row = lax.broadcasted_iota(jnp.int32, v.shape, 0) & (CHUNK - 1)
    k = 1
    while k < CHUNK:
        v = v + jnp.where(row >= k, pltpu.roll(v, k, 0), 0.0)
        k *= 2
    return v


def _cumsum_lanes(v):
    col = lax.broadcasted_iota(jnp.int32, v.shape, 1) & (CHUNK - 1)
    k = 1
    while k < CHUNK:
        v = v + jnp.where(col >= k, pltpu.roll(v, k, 1), 0.0)
        k *= 2
    return v


def _rev_cumsum_rows(v):
    row = lax.broadcasted_iota(jnp.int32, v.shape, 0)
    k = 1
    while k < CHUNK:
        v = v + jnp.where(row < CHUNK - k, pltpu.roll(v, CHUNK - k, 0), 0.0)
        k *= 2
    return v


PAIR = 2 * HEAD_DIM
GROUP_LANES = HEADS_PER_GROUP * HEAD_DIM


def _head_lane_matrix():
    h = lax.broadcasted_iota(jnp.int32, (128, D_INNER), 0)
    j = lax.broadcasted_iota(jnp.int32, (128, D_INNER), 1)
    return (j // HEAD_DIM == h).astype(BF16)


def _split_bf16(v, pieces):
    out = []
    for _ in range(pieces):
        p = v.astype(BF16)
        out.append(p)
        v = v - p.astype(F32)
    return out


_EXACT_PIECES = 3


def _expand_heads(values, e3):
    lhs = jnp.concatenate([jnp.concatenate(_split_bf16(v, _EXACT_PIECES), axis=1) for v in values], axis=0)
    out = jnp.dot(lhs, e3, preferred_element_type=F32)
    rows = values[0].shape[0]
    return [out[i * rows:(i + 1) * rows, :] for i in range(len(values))]


def _reduce_heads(v, et, pieces):
    return sum(jnp.dot(p, et, preferred_element_type=F32) for p in _split_bf16(v, pieces))


def _ssd_decay(dt_raw, dt_raw_t, bias_r, bias_c, alog_r, alog_c, *, tb=1024):
    t_dim = dt_raw.shape[0]
    tb = min(tb, t_dim)
    assert t_dim % tb == 0 and tb % CHUNK == 0

    def body(dtr_ref, dtt_ref, br_ref, bc_ref, ar_ref, ac_ref, dt_ref, acs_ref, sg_ref, acst_ref):
        pre = dtr_ref[...] + br_ref[...]
        dt = _softplus(pre)
        dt_ref[...] = dt
        sg_ref[...] = _sigmoid(pre)
        acs_ref[...] = _cumsum_rows(dt * (-jnp.exp(ar_ref[...])))
        acst_ref[...] = _cumsum_lanes(_softplus(dtt_ref[...] + bc_ref[...]) * (-jnp.exp(ac_ref[...])))

    rows = pl.BlockSpec((tb, 128), lambda i: (i, 0))
    cols = pl.BlockSpec((N_HEADS, tb), lambda i: (0, i))
    sds = jax.ShapeDtypeStruct((t_dim, 128), F32)
    return _pcall(
        body, name="ssd_decay",
        out_shape=(sds, sds, sds, jax.ShapeDtypeStruct((N_HEADS, t_dim), F32)),
        grid=(t_dim // tb,),
        in_specs=[rows, cols, _const_spec((1, 128)), _const_spec((N_HEADS, 1)),
                  _const_spec((1, 128)), _const_spec((N_HEADS, 1))],
        out_specs=(rows, rows, rows, cols), compiler_params=_params("parallel"),
    )(dt_raw, dt_raw_t, bias_r, bias_c, alog_r, alog_c)


def _pair_decay(acs_slab, acs_c, h0, causal, left):
    other = pltpu.roll(acs_slab, HEAD_DIM, 1)
    col0 = jnp.where(left, acs_slab, other)
    col1 = jnp.where(left, other, acs_slab)
    l0 = jnp.exp(jnp.where(causal, col0 - acs_c[h0:h0 + 1, :], NEG_INF))
    l1 = jnp.exp(jnp.where(causal, col1 - acs_c[h0 + 1:h0 + 2, :], NEG_INF))
    return l0, l1


def _ssd_fwd(xc, dt_r, acs_r, acs_t, dskip_x, e3_mat):
    t_dim = xc.shape[0]
    nc = t_dim // CHUNK

    def body(xc_ref, dt_ref, acs_ref, acst_ref, dk_ref, e3_ref, y_ref, st_ref, state):
        @pl.when(pl.program_id(0) == 0)
        def _():
            state[...] = jnp.zeros_like(state)

        dt, acs = _expand_heads([dt_ref[...], acs_ref[...]], e3_ref[...])
        acs_c = acst_ref[...]
        st_ref[0] = state[...]
        last = acs[CHUNK - 1:CHUNK, :]
        xs32 = xc_ref[:, 0:D_INNER].astype(F32)
        xdt = xs32 * dt
        xdt16 = xdt.astype(BF16)
        xdte16 = (xdt * jnp.exp(last - acs)).astype(BF16)
        ea = jnp.exp(acs)
        cd = jnp.exp(last)
        skip = dk_ref[...] * xs32
        causal = (lax.broadcasted_iota(jnp.int32, (CHUNK, CHUNK), 0)
                  >= lax.broadcasted_iota(jnp.int32, (CHUNK, CHUNK), 1))
        left = lax.broadcasted_iota(jnp.int32, (CHUNK, PAIR), 1) < HEAD_DIM
        for g in range(N_GROUPS):
            gl = slice(g * GROUP_LANES, (g + 1) * GROUP_LANES)
            bg = xc_ref[:, D_INNER + g * D_STATE:D_INNER + (g + 1) * D_STATE]
            cg = xc_ref[:, D_INNER + (N_GROUPS + g) * D_STATE:D_INNER + (N_GROUPS + g + 1) * D_STATE]
            cb = lax.dot_general(cg, bg, _DIMS["nt"], preferred_element_type=F32)
            hprev = state[:, gl]
            ch = jnp.dot(cg, hprev.astype(BF16), preferred_element_type=F32)
            for j in range(HEADS_PER_GROUP // 2):
                pl_ = slice(g * GROUP_LANES + j * PAIR, g * GROUP_LANES + (j + 1) * PAIR)
                h0 = g * HEADS_PER_GROUP + 2 * j
                l0, l1 = _pair_decay(acs[:, pl_], acs_c, h0, causal, left)
                lhs = jnp.concatenate([(cb * l0).astype(BF16), (cb * l1).astype(BF16)], axis=1)
                xp = xdt16[:, pl_]
                zero = jnp.zeros_like(xp)
                rhs = jnp.concatenate([jnp.where(left, xp, zero), jnp.where(left, zero, xp)], axis=0)
                ydiag = jnp.dot(lhs, rhs, preferred_element_type=F32)
                y_ref[:, pl_] = ydiag + ch[:, j * PAIR:(j + 1) * PAIR] * ea[:, pl_] + skip[:, pl_]
            s_new = lax.dot_general(bg, xdte16[:, gl], _DIMS["tn"], preferred_element_type=F32)
            state[:, gl] = hprev * cd[:, gl] + s_new

    rows = lambda w: pl.BlockSpec((CHUNK, w), lambda c: (c, 0))
    return _pcall(
        body, name="ssd_fwd",
        out_shape=(jax.ShapeDtypeStruct((t_dim, D_INNER), F32),
                   jax.ShapeDtypeStruct((nc, D_STATE, D_INNER), F32)),
        grid=(nc,),
        in_specs=[rows(CONV_DIM), rows(128), rows(128), pl.BlockSpec((N_HEADS, CHUNK), lambda c: (0, c)),
                  _const_spec((1, D_INNER)), _const_spec((_EXACT_PIECES * 128, D_INNER))],
        out_specs=(rows(D_INNER), pl.BlockSpec((1, D_STATE, D_INNER), lambda c: (c, 0, 0))),
        scratch_shapes=[pltpu.VMEM((D_STATE, D_INNER), F32)],
        compiler_params=_params("arbitrary"),
    )(xc, dt_r, acs_r, acs_t, dskip_x, e3_mat)


def _ssd_bwd(xc, dt_r, acs_r, sg_r, acs_t, alog_r, dskip_x, e3_mat, et_mat, states, dy):
    t_dim = xc.shape[0]
    nc = t_dim // CHUNK

    def body(xc_ref, dt_ref, acs_ref, sg_ref, acst_ref, ar_ref, dk_ref, e3_ref, et_ref, st_ref, dy_ref,
             dxc_ref, ddt_ref, small_ref, dstate, dacs_ref, dxdt_ref, acc_x, acc_r):
        step = pl.program_id(0)

        @pl.when(step == 0)
        def _():
            dstate[...] = jnp.zeros_like(dstate)
            acc_x[...] = jnp.zeros_like(acc_x)
            acc_r[...] = jnp.zeros_like(acc_r)

        dt_r = dt_ref[...]
        a_r = -jnp.exp(ar_ref[...])
        dt, acs = _expand_heads([dt_r, acs_ref[...]], e3_ref[...])
        acs_c = acst_ref[...]
        last = acs[CHUNK - 1:CHUNK, :]
        xs32 = xc_ref[:, 0:D_INNER].astype(F32)
        xdt = xs32 * dt
        xdt16 = xdt.astype(BF16)
        dte = jnp.exp(last - acs)
        xdte = xdt * dte
        xdte16 = xdte.astype(BF16)
        cd = jnp.exp(last)
        dyv = dy_ref[...]
        dy16 = dyv.astype(BF16)
        dye = dyv * jnp.exp(acs)
        dye16 = dye.astype(BF16)
        causal = (lax.broadcasted_iota(jnp.int32, (CHUNK, CHUNK), 0)
                  >= lax.broadcasted_iota(jnp.int32, (CHUNK, CHUNK), 1))
        left = lax.broadcasted_iota(jnp.int32, (CHUNK, PAIR), 1) < HEAD_DIM
        lane_id = lax.broadcasted_iota(jnp.int32, (CHUNK, 128), 1)
        row_id = lax.broadcasted_iota(jnp.int32, (CHUNK, 128), 0)
        is_last_row = lax.broadcasted_iota(jnp.int32, (CHUNK, 1), 0) == CHUNK - 1
        dacs_cols = jnp.zeros((CHUNK, 128), F32)
        dacs_rows = jnp.zeros((CHUNK, 128), F32)
        for g in range(N_GROUPS):
            gl = slice(g * GROUP_LANES, (g + 1) * GROUP_LANES)
            b_lanes = slice(D_INNER + g * D_STATE, D_INNER + (g + 1) * D_STATE)
            c_lanes = slice(D_INNER + (N_GROUPS + g) * D_STATE, D_INNER + (N_GROUPS + g + 1) * D_STATE)
            bg = xc_ref[:, b_lanes]
            cg = xc_ref[:, c_lanes]
            cb = lax.dot_general(cg, bg, _DIMS["nt"], preferred_element_type=F32)
            hprev = st_ref[0, :, gl]
            hp16 = hprev.astype(BF16)
            dhn = dstate[:, gl]
            dhn16 = dhn.astype(BF16)
            ch = jnp.dot(cg, hp16, preferred_element_type=F32)
            gmat = jnp.dot(bg, dhn16, preferred_element_type=F32)
            gx = gmat * xdte[:, gl]
            dlast = jnp.sum(gx, axis=0, keepdims=True) + cd[:, gl] * jnp.sum(dhn * hprev, axis=0, keepdims=True)
            dacs_ref[:, gl] = dye[:, gl] * ch - gx + jnp.where(is_last_row, dlast, 0.0)
            dc_acc = lax.dot_general(dye16[:, gl], hp16, _DIMS["nt"], preferred_element_type=F32)
            db_acc = lax.dot_general(xdte16[:, gl], dhn16, _DIMS["nt"], preferred_element_type=F32)
            dstate[:, gl] = dhn * cd[:, gl] + lax.dot_general(cg, dye16[:, gl], _DIMS["tn"],
                                                             preferred_element_type=F32)
            dcb = jnp.zeros((CHUNK, CHUNK), F32)
            for j in range(HEADS_PER_GROUP // 2):
                pl_ = slice(g * GROUP_LANES + j * PAIR, g * GROUP_LANES + (j + 1) * PAIR)
                h0 = g * HEADS_PER_GROUP + 2 * j
                l0, l1 = _pair_decay(acs[:, pl_], acs_c, h0, causal, left)
                m0, m1 = cb * l0, cb * l1
                lhs = jnp.concatenate([m0.astype(BF16), m1.astype(BF16)], axis=1)
                dyp = dy16[:, pl_]
                zero = jnp.zeros_like(dyp)
                both = lax.dot_general(lhs, dyp, _DIMS["tn"], preferred_element_type=F32)
                dxdt_ref[:, pl_] = (jnp.where(left, both[0:CHUNK, :], both[CHUNK:, :])
                                    + gmat[:, j * PAIR:(j + 1) * PAIR] * dte[:, pl_])
                lhs2 = jnp.concatenate([jnp.where(left, dyp, zero), jnp.where(left, zero, dyp)], axis=0)
                dm = lax.dot_general(lhs2, xdt16[:, pl_], _DIMS["nt"], preferred_element_type=F32)
                dm0, dm1 = dm[0:CHUNK, :], dm[CHUNK:, :]
                dcb = dcb + dm0 * l0 + dm1 * l1
                ds0, ds1 = dm0 * m0, dm1 * m1
                dacs_cols = jnp.where(lane_id == h0, jnp.sum(ds0, axis=1, keepdims=True), dacs_cols)
                dacs_cols = jnp.where(lane_id == h0 + 1, jnp.sum(ds1, axis=1, keepdims=True), dacs_cols)
                dacs_rows = jnp.where(row_id == h0, jnp.sum(ds0, axis=0, keepdims=True), dacs_rows)
                dacs_rows = jnp.where(row_id == h0 + 1, jnp.sum(ds1, axis=0, keepdims=True), dacs_rows)
            dcb16 = dcb.astype(BF16)
            dxc_ref[:, c_lanes] = (dc_acc + jnp.dot(dcb16, bg, preferred_element_type=F32)).astype(BF16)
            dxc_ref[:, b_lanes] = (db_acc + lax.dot_general(dcb16, cg, _DIMS["tn"],
                                                           preferred_element_type=F32)).astype(BF16)
        dxdt = dxdt_ref[...]
        dxc_ref[:, 0:D_INNER] = (dxdt * dt + dk_ref[...] * dyv).astype(BF16)
        acc_x[0:1, :] += jnp.sum(dyv * xs32, axis=0, keepdims=True)
        et = et_ref[...]
        dacs = _reduce_heads(dacs_ref[...], et, 2) + dacs_cols - dacs_rows.T
        dadt = _rev_cumsum_rows(dacs)
        ddraw = (_reduce_heads(dxdt * xs32, et, 1) + dadt * a_r) * sg_ref[...]
        ddraw = jnp.where(lane_id < N_HEADS, ddraw, 0.0)
        ddt_ref[...] = ddraw
        acc_r[0:1, :] += jnp.where(lane_id[0:1, :] < N_HEADS,
                                   jnp.sum(dadt * dt_r, axis=0, keepdims=True) * a_r, 0.0)
        acc_r[1:2, :] += jnp.sum(ddraw, axis=0, keepdims=True)

        @pl.when(step == nc - 1)
        def _():
            dd = _reduce_heads(acc_x[...], et_ref[...], 3)
            rid = lax.broadcasted_iota(jnp.int32, (8, 128), 0)
            small_ref[...] = acc_r[...] + jnp.where(rid == 2, pltpu.roll(dd, 2, 0), 0.0)

    rev = lambda w: pl.BlockSpec((CHUNK, w), lambda c: (nc - 1 - c, 0))
    return _pcall(
        body, name="ssd_bwd",
        out_shape=(jax.ShapeDtypeStruct((t_dim, CONV_DIM), BF16),
                   jax.ShapeDtypeStruct((t_dim, 128), F32),
                   jax.ShapeDtypeStruct((8, 128), F32)),
        grid=(nc,),
        in_specs=[rev(CONV_DIM), rev(128), rev(128), rev(128),
                  pl.BlockSpec((N_HEADS, CHUNK), lambda c: (0, nc - 1 - c)),
                  _const_spec((1, 128)), _const_spec((1, D_INNER)),
                  _const_spec((_EXACT_PIECES * 128, D_INNER)), _const_spec((D_INNER, 128)),
                  pl.BlockSpec((1, D_STATE, D_INNER), lambda c: (nc - 1 - c, 0, 0)),
                  rev(D_INNER)],
        out_specs=(rev(CONV_DIM), rev(128), _const_spec((8, 128))),
        scratch_shapes=[pltpu.VMEM((D_STATE, D_INNER), F32), pltpu.VMEM((CHUNK, D_INNER), F32),
                        pltpu.VMEM((CHUNK, D_INNER), F32), pltpu.VMEM((8, D_INNER), F32),
                        pltpu.VMEM((8, 128), F32)],
        compiler_params=_params("arbitrary"),
    )(xc, dt_r, acs_r, sg_r, acs_t, alog_r, dskip_x, e3_mat, et_mat, states, dy)


_GATE_GROUP = D_INNER // N_GROUPS


def _gate_fwd(y, z, g, *, tb=256):
    t_dim = y.shape[0]

    def body(y_ref, z_ref, g_ref, o_ref):
        for gi in range(N_GROUPS):
            lanes = slice(gi * _GATE_GROUP, (gi + 1) * _GATE_GROUP)
            zv = z_ref[:, lanes].astype(F32)
            wv = y_ref[:, lanes] * (zv * _sigmoid(zv))
            o_ref[:, lanes] = (wv * _rms(wv) * g_ref[:, lanes]).astype(BF16)

    return _pcall(
        body, name="gate_fwd", out_shape=jax.ShapeDtypeStruct((t_dim, D_INNER), BF16),
        grid=(t_dim // tb,),
        in_specs=[_row_spec(tb, D_INNER), _row_spec(tb, D_INNER), _const_spec((1, D_INNER))],
        out_specs=_row_spec(tb, D_INNER), compiler_params=_params("parallel"),
    )(y, z, g)


def _gate_bwd(dyn, y, z, g, *, tb=256):
    t_dim = y.shape[0]

    def body(d_ref, y_ref, z_ref, g_ref, dy_ref, dz_ref, dg_ref):
        @pl.when(pl.program_id(0) == 0)
        def _():
            dg_ref[...] = jnp.zeros_like(dg_ref)

        for gi in range(N_GROUPS):
            lanes = slice(gi * _GATE_GROUP, (gi + 1) * _GATE_GROUP)
            zv = z_ref[:, lanes].astype(F32)
            sg = _sigmoid(zv)
            sz = zv * sg
            yv = y_ref[:, lanes]
            wv = yv * sz
            r = _rms(wv)
            what = wv * r
            dv = d_ref[:, lanes]
            dwhat = dv * g_ref[:, lanes]
            dw = r * (dwhat - what * jnp.mean(dwhat * what, axis=-1, keepdims=True))
            dg_ref[0:1, lanes] += jnp.sum(dv * what, axis=0, keepdims=True)
            dy_ref[:, lanes] = dw * sz
            dz_ref[:, lanes] = (dw * yv * (sg * (1.0 + zv * (1.0 - sg)))).astype(BF16)

    return _pcall(
        body, name="gate_bwd",
        out_shape=(jax.ShapeDtypeStruct((t_dim, D_INNER), F32),
                   jax.ShapeDtypeStruct((t_dim, D_INNER), BF16),
                   jax.ShapeDtypeStruct((8, D_INNER), F32)),
        grid=(t_dim // tb,),
        in_specs=[_row_spec(tb, D_INNER), _row_spec(tb, D_INNER), _row_spec(tb, D_INNER),
                  _const_spec((1, D_INNER))],
        out_specs=(_row_spec(tb, D_INNER), _row_spec(tb, D_INNER), _const_spec((8, D_INNER))),
        compiler_params=_params("arbitrary"),
    )(dyn, y, z, g)


def _loss_head(h, g, target, *, tb=512):
    t_dim, d = h.shape

    def body(h_ref, g_ref, t_ref, dh_ref, dh16_ref, small_ref):
        @pl.when(pl.program_id(0) == 0)
        def _():
            small_ref[...] = jnp.zeros_like(small_ref)

        x = h_ref[...]
        r = _rms(x)
        xhat = x * r
        gv = g_ref[...]
        err = xhat * gv - t_ref[...]
        small_ref[1:2, :] += (0.5 / d) * jnp.sum(err * err, axis=0, keepdims=True)
        dyv = err * (1.0 / d)
        dxhat = dyv * gv
        dh = r * (dxhat - xhat * jnp.mean(dxhat * xhat, axis=-1, keepdims=True))
        dh_ref[...] = dh
        dh16_ref[...] = dh.astype(BF16)
        small_ref[0:1, :] += jnp.sum(dyv * xhat, axis=0, keepdims=True)

    return _pcall(
        body, name="loss_head",
        out_shape=(jax.ShapeDtypeStruct((t_dim, d), F32), jax.ShapeDtypeStruct((t_dim, d), BF16),
                   jax.ShapeDtypeStruct((8, d), F32)),
        grid=(t_dim // tb,),
        in_specs=[_row_spec(tb, d), _const_spec((1, d)), _row_spec(tb, d)],
        out_specs=(_row_spec(tb, d), _row_spec(tb, d), _const_spec((8, d))),
        compiler_params=_params("arbitrary"),
    )(h, g, target)


_ADAM_C1 = 1.0 / (1.0 - ADAM_B1 ** ADAM_STEP)
_ADAM_C2 = 1.0 / (1.0 - ADAM_B2 ** ADAM_STEP)


def _adamw_math(w, g, m, v):
    mn = ADAM_B1 * m + (1.0 - ADAM_B1) * g
    vn = ADAM_B2 * v + (1.0 - ADAM_B2) * (g * g)
    delta = -ADAM_LR * ((mn * _ADAM_C1) / (jnp.sqrt(vn * _ADAM_C2) + ADAM_EPS) + ADAM_WD * w)
    return delta, mn, vn


def _adamw(w, g, m, v, *, name, part=None, into=None):
    r_dim, c = w.shape
    rows = r_dim if part is None else r_dim // 2
    assert g.shape == (rows, c)
    tb = max(t for t in range(8, 513, 8) if rows % t == 0)
    first = 0 if part is None else part * (rows // tb)
    n_out = 3 if part is None else 4

    def body(w_ref, g_ref, m_ref, v_ref, *rest):
        outs = rest[-n_out:]
        gv = g_ref[...]
        outs[0][...], outs[1][...], outs[2][...] = _adamw_math(w_ref[...], gv, m_ref[...], v_ref[...])
        if part is not None:
            outs[3][...] = gv

    spec = pl.BlockSpec((tb, c), lambda i: (first + i, 0))
    sds = jax.ShapeDtypeStruct((r_dim, c), F32)
    in_specs = [spec, _row_spec(tb, c), spec, spec]
    operands = [w, g, m, v]
    aliases = {}
    if into is not None:
        in_specs += [_ANY] * n_out
        operands += list(into)
        aliases = {4 + i: i for i in range(n_out)}
    outs = _pcall(
        body, name=name, out_shape=(sds,) * n_out, grid=(rows // tb,),
        in_specs=in_specs, out_specs=(spec,) * n_out, input_output_aliases=aliases,
        compiler_params=_params("parallel"),
    )(*operands)
    return tuple(outs) if part is not None else tuple(outs) + (g,)


def _adamw_small(params, *, name):
    n = len(params)

    def body(*refs):
        ins, outs = refs[:4 * n], refs[4 * n:]
        for i in range(n):
            w_ref, g_ref, m_ref, v_ref = ins[4 * i:4 * i + 4]
            res = _adamw_math(w_ref[...], g_ref[...], m_ref[...], v_ref[...])
            for o_ref, r in zip(outs[3 * i:3 * i + 3], res):
                o_ref[...] = r

    vmem = pl.BlockSpec(memory_space=pltpu.VMEM)
    flat = [a for p in params for a in p]
    outs = _pcall(
        body, name=name,
        out_shape=tuple(jax.ShapeDtypeStruct(p[0].shape, F32) for p in params for _ in range(3)),
        in_specs=[vmem] * (4 * n), out_specs=(vmem,) * (3 * n),
    )(*flat)
    return [tuple(outs[3 * i:3 * i + 3]) for i in range(n)]


def _pair_sum(grad, recv, place, *, name):
    s_dim, r_dim, c = grad.shape
    half = r_dim // 2
    tb = 256 if half % 256 == 0 else half
    per_half = half // tb

    def body(place_ref, a_ref, b_ref, o16_ref, o32_ref):
        s = a_ref[...] + b_ref[...]
        o16_ref[...] = s.astype(BF16)

        @pl.when(pl.program_id(1) == place_ref[1])
        def _():
            o32_ref[...] = s[0]

    grid_spec = pltpu.PrefetchScalarGridSpec(
        num_scalar_prefetch=1, grid=(per_half, s_dim),
        in_specs=[pl.BlockSpec((1, tb, c), lambda i, s, p: (s, p[0] * per_half + i, 0)),
                  pl.BlockSpec((1, tb, c), lambda i, s, p: (s, i, 0))],
        out_specs=(pl.BlockSpec((1, tb, c), lambda i, s, p: (s, i, 0)),
                   pl.BlockSpec((tb, c), lambda i, s, p: (i, 0))))
    return _pcall(
        body, name=name, grid_spec=grid_spec,
        out_shape=(jax.ShapeDtypeStruct((s_dim, half, c), BF16), jax.ShapeDtypeStruct((half, c), F32)),
        compiler_params=_params("parallel", "arbitrary"),
    )(place, grad, recv)


def _chip_sum(own, recv, place, *, name):
    r_dim, c = own.shape
    tb = 256 if r_dim % 256 == 0 else r_dim

    def body(place_ref, a_ref, b_ref, o_ref):
        s = a_ref[...]
        for k in range(1, N_CHIPS):
            s = s + b_ref[k].astype(F32)
        o_ref[...] = s

    grid_spec = pltpu.PrefetchScalarGridSpec(
        num_scalar_prefetch=1, grid=(r_dim // tb,),
        in_specs=[pl.BlockSpec((tb, c), lambda i, p: (i, 0)),
                  pl.BlockSpec((N_CHIPS, tb, c), lambda i, p: (0, i, 0))],
        out_specs=pl.BlockSpec((None, tb, c), lambda i, p: (p[0], i, 0)))
    return _pcall(
        body, name=name, grid_spec=grid_spec, out_shape=jax.ShapeDtypeStruct((2, r_dim, c), F32),
        compiler_params=_params("parallel"),
    )(place, own, recv)


def _position():
    return lax.axis_index("x"), lax.axis_index("y"), lax.axis_index("c")


def _chip_peer(x, y, k):
    return x ^ (k >> 1), y ^ (k & 1)


_ANY = pl.BlockSpec(memory_space=pl.ANY)
_TOKEN = jax.ShapeDtypeStruct((8, 128), F32)


def _all_gather_weights(shards):
    n = len(shards)
    hops = N_CHIPS - 1

    def body(*refs):
        srcs, outs, done = refs[:n], refs[n:2 * n], refs[2 * n]
        send_sems, recv_sems = refs[2 * n + 1:]
        x, y, c = _position()
        me = 2 * x + y
        done[...] = jnp.zeros_like(done)

        def over_ici(w, k, chip, to):
            return pltpu.make_async_remote_copy(
                src_ref=srcs[w].at[c], dst_ref=outs[w].at[chip, c],
                send_sem=send_sems.at[w, k - 1], recv_sem=recv_sems.at[w, k - 1],
                device_id=to, device_id_type=MESH)

        def over_d2d(w, k, chip, half):
            return pltpu.make_async_remote_copy(
                src_ref=outs[w].at[chip, half], dst_ref=outs[w].at[chip, half],
                send_sem=send_sems.at[w, hops + k - 1], recv_sem=recv_sems.at[w, hops + k - 1],
                device_id=(x, y, 1 - c), device_id_type=MESH)

        sends = []
        for w in range(n):
            for k in range(1, N_CHIPS):
                px, py = _chip_peer(x, y, k)
                cp = over_ici(w, k, me, (px, py, c))
                cp.start()
                sends.append(cp)
        for w in range(n):
            for k in range(1, N_CHIPS):
                px, py = _chip_peer(x, y, k)
                over_ici(w, k, 2 * px + py, (px, py, c)).wait_recv()
                cp = over_d2d(w, k, 2 * px + py, c)
                cp.start()
                sends.append(cp)
        for w in range(n):
            for k in range(1, N_CHIPS):
                px, py = _chip_peer(x, y, k)
                over_d2d(w, k, 2 * px + py, 1 - c).wait_recv()
        for cp in sends:
            cp.wait_send()

    outs = _pcall(
        body, name="gather_weights",
        out_shape=tuple(jax.ShapeDtypeStruct((N_CHIPS,) + s.shape, s.dtype) for s in shards) + (_TOKEN,),
        in_specs=[_ANY] * n, out_specs=(_ANY,) * n + (pl.BlockSpec(memory_space=pltpu.VMEM),),
        scratch_shapes=[pltpu.SemaphoreType.DMA((n, 2 * hops)),
                        pltpu.SemaphoreType.DMA((n, 2 * hops))],
    )(*shards)
    return outs[:n], outs[n][0, 0]


def _pair_copies(srcs, lands, send_sems, recv_sems):
    x, y, c = _position()
    copies = []
    for w in range(len(srcs)):
        half = srcs[w].shape[1] // 2
        copies.append(pltpu.make_async_remote_copy(
            src_ref=srcs[w].at[:, pl.ds((1 - c) * half, half), :], dst_ref=lands[w],
            send_sem=send_sems.at[w], recv_sem=recv_sems.at[w],
            device_id=(x, y, 1 - c), device_id_type=MESH))
    return copies


def _chip_copies(srcs, lands, send_sems, recv_sems):
    x, y, c = _position()
    copies = []
    for w in range(len(srcs)):
        for k in range(1, N_CHIPS):
            px, py = _chip_peer(x, y, k)
            i = w * (N_CHIPS - 1) + k - 1
            copies.append(pltpu.make_async_remote_copy(
                src_ref=srcs[w].at[2 * px + py], dst_ref=lands[w].at[k],
                send_sem=send_sems.at[i], recv_sem=recv_sems.at[i],
                device_id=(px, py, c), device_id_type=MESH))
    return copies


def _gather_copies(srcs, lands, send_sems, recv_sems):
    x, y, c = _position()
    me = 2 * x + y
    copies = []
    for w in range(len(srcs)):
        for k in range(1, N_CHIPS):
            px, py = _chip_peer(x, y, k)
            i = w * (N_CHIPS - 1) + k - 1
            copies.append(pltpu.make_async_remote_copy(
                src_ref=srcs[w].at[c], dst_ref=lands[w].at[me, c],
                send_sem=send_sems.at[i], recv_sem=recv_sems.at[i],
                device_id=(px, py, c), device_id_type=MESH))
    return copies


def _exchange(name, copies_of, n_copies, srcs, land_shapes):
    n = len(srcs)

    def body(*refs):
        copies = copies_of(refs[:n], refs[n:2 * n], refs[2 * n], refs[2 * n + 1])
        for cp in copies:
            cp.start()
        for cp in copies:
            cp.wait_recv()
        for cp in copies:
            cp.wait_send()

    return _pcall(
        body, name=name, out_shape=tuple(land_shapes),
        in_specs=[_ANY] * n, out_specs=(_ANY,) * n,
        scratch_shapes=[pltpu.SemaphoreType.DMA((n_copies,)), pltpu.SemaphoreType.DMA((n_copies,))],
    )(*srcs)


_HBM = pl.BlockSpec(memory_space=pltpu.HBM)
_SEM = pl.BlockSpec(memory_space=pltpu.SEMAPHORE)
_DATAFLOW = pltpu.SideEffectType.DATAFLOW_SIDE_EFFECTING


def _exchange_start(name, copies_of, n_copies, srcs, land_shapes):
    n = len(srcs)
    lands = [lax.empty(s.shape, s.dtype) for s in land_shapes]

    def body(*refs):
        for cp in copies_of(refs[:n], refs[n:2 * n], refs[2 * n], refs[2 * n + 1]):
            cp.start()
        refs[-1][...] = jnp.zeros_like(refs[-1])

    through = [pltpu.HBM(a.shape, a.dtype) for a in list(srcs) + lands]
    outs = _pcall(
        body, name=name,
        out_shape=(pltpu.SemaphoreType.DMA((n_copies,)), pltpu.SemaphoreType.DMA((n_copies,)),
                   *through, jax.ShapeDtypeStruct((8, 128), F32)),
        in_specs=[_HBM] * (2 * n),
        out_specs=(_SEM, _SEM, *([_HBM] * (2 * n)), pl.BlockSpec(memory_space=pltpu.VMEM)),
        input_output_aliases={i: 2 + i for i in range(2 * n)},
        compiler_params=pltpu.CompilerParams(has_side_effects=_DATAFLOW),
    )(*[pltpu.with_memory_space_constraint(a, pltpu.HBM) for a in list(srcs) + lands])
    return outs[:-1], outs[-1][0, 0]


def _exchange_wait(name, copies_of, state, after):
    send_sems, recv_sems, through = state[0], state[1], state[2:]
    n = len(through) // 2
    if after.ndim == 0:
        after = jnp.broadcast_to(after, (8, 128))
    after = pltpu.with_memory_space_constraint(after, pltpu.HBM)

    def body(*refs):
        for cp in copies_of(refs[:n], refs[n:2 * n], refs[2 * n], refs[2 * n + 1]):
            cp.wait_send()
            cp.wait_recv()

    outs = _pcall(
        body, name=name,
        out_shape=tuple(pltpu.HBM(a.shape, a.dtype) for a in through),
        in_specs=[_HBM] * (2 * n) + [_SEM, _SEM, _HBM], out_specs=tuple([_HBM] * (2 * n)),
        input_output_aliases={i: i for i in range(2 * n)},
        compiler_params=pltpu.CompilerParams(has_side_effects=_DATAFLOW),
    )(*through, send_sems, recv_sems, after)
    return outs[:n], outs[n:]


def _forward_halves(lands, *, name):
    n = len(lands)
    hops = N_CHIPS - 1

    def body(*refs):
        ins, outs, done = refs[:n], refs[n:2 * n], refs[2 * n]
        send_sems, recv_sems = refs[2 * n + 1], refs[2 * n + 2]
        x, y, c = _position()
        done[...] = jnp.zeros_like(done)
        copies = []
        for w in range(n):
            for k in range(1, N_CHIPS):
                px, py = _chip_peer(x, y, k)
                i = w * hops + k - 1
                copies.append(pltpu.make_async_remote_copy(
                    src_ref=ins[w].at[2 * px + py, c], dst_ref=outs[w].at[2 * px + py, c],
                    send_sem=send_sems.at[i], recv_sem=recv_sems.at[i],
                    device_id=(x, y, 1 - c), device_id_type=MESH))
        for cp in copies:
            cp.start()
        for cp in copies:
            cp.wait_recv()
        for cp in copies:
            cp.wait_send()

    outs = _pcall(
        body, name=name,
        out_shape=tuple(jax.ShapeDtypeStruct(a.shape, a.dtype) for a in lands) + (_TOKEN,),
        in_specs=[_ANY] * n, out_specs=(_ANY,) * n + (pl.BlockSpec(memory_space=pltpu.VMEM),),
        input_output_aliases={i: i for i in range(n)},
        scratch_shapes=[pltpu.SemaphoreType.DMA((n * hops,)), pltpu.SemaphoreType.DMA((n * hops,))],
    )(*lands)
    return outs[:n], outs[n][0, 0]


def _pair_lands(grads):
    return [jax.ShapeDtypeStruct((g.shape[0], g.shape[1] // 2, g.shape[2]), F32) for g in grads]


def _same_lands(parts):
    return [jax.ShapeDtypeStruct(p.shape, p.dtype) for p in parts]


def _pair_gather_halves(halves, *, name):
    n = len(halves)

    def body(*refs):
        ins, outs = refs[:n], refs[n:2 * n]
        send_sems, recv_sems = refs[2 * n:]
        x, y, c = _position()
        sends = []
        for w in range(n):
            cp = pltpu.make_async_remote_copy(
                src_ref=ins[w].at[c], dst_ref=outs[w].at[c],
                send_sem=send_sems.at[w], recv_sem=recv_sems.at[w],
                device_id=(x, y, 1 - c), device_id_type=MESH)
            cp.start()
            sends.append(cp)
        for cp in sends:
            cp.wait_recv()
        for cp in sends:
            cp.wait_send()

    whole = _pcall(
        body, name=name,
        out_shape=tuple(jax.ShapeDtypeStruct(h.shape, F32) for h in halves),
        in_specs=[_ANY] * n, out_specs=(_ANY,) * n,
        input_output_aliases={i: i for i in range(n)},
        scratch_shapes=[pltpu.SemaphoreType.DMA((n,)), pltpu.SemaphoreType.DMA((n,))],
    )(*halves)
    return [w.reshape(2 * w.shape[1], w.shape[2]) for w in whole]


def _all_reduce_small(packed, *, name, sum_row0):
    r_dim, c = packed.shape

    def body(src_ref, out_ref, recv_ref, send_sems, recv_sems):
        x, y, c_ = _position()
        me = 4 * x + 2 * y + c_
        recv_ref[0] = src_ref[...]
        sends = []
        for k in range(1, N_DEV):
            peer = (x ^ (k >> 2), y ^ ((k >> 1) & 1), c_ ^ (k & 1))
            cp = pltpu.make_async_remote_copy(
                src_ref=src_ref, dst_ref=recv_ref.at[k],
                send_sem=send_sems.at[k - 1], recv_sem=recv_sems.at[k - 1],
                device_id=peer, device_id_type=MESH)
            cp.start()
            sends.append(cp)
        for cp in sends:
            cp.wait_recv()
        total = recv_ref[me]
        for d in range(1, N_DEV):
            total = total + recv_ref[d ^ me]
        if sum_row0:
            row0 = jnp.sum(total[0:1, :], axis=1, keepdims=True)
            rid = lax.broadcasted_iota(jnp.int32, total.shape, 0)
            total = jnp.where(rid == 0, row0, total)
        out_ref[...] = total
        for cp in sends:
            cp.wait_send()

    return _pcall(
        body, name=name, out_shape=jax.ShapeDtypeStruct((r_dim, c), F32),
        in_specs=[pl.BlockSpec(memory_space=pltpu.VMEM)],
        out_specs=pl.BlockSpec(memory_space=pltpu.VMEM),
        scratch_shapes=[pltpu.VMEM((N_DEV, r_dim, c), F32),
                        pltpu.SemaphoreType.DMA((N_DEV - 1,)), pltpu.SemaphoreType.DMA((N_DEV - 1,))],
    )(packed)


def _pad_lanes(v, width):
    return jnp.pad(v, ((0, 0), (0, width - v.shape[1])))


def _pad_rows(v, rows):
    pad = [(0, 0)] * v.ndim
    pad[-2] = (0, rows - v.shape[-2])
    return jnp.pad(v, pad)


_IN_PROJ_SHARD_ROWS = 1312


def _rows_1024(v):
    flat = v.reshape(-1)
    pad = (-flat.shape[0]) % D_MODEL
    return jnp.pad(flat, (0, pad)).reshape(-1, D_MODEL)


def _local_step(xs, target, pw, fetch, reduce_start, reduce_midway,
                conv_w, conv_b, gate_g,
                norm_mix_g, norm_mlp_g, pool_b, pool_scale, ssm_dt_bias, ssm_a_log, ssm_d, final_g):
    bias_r = _pad_lanes(ssm_dt_bias, 128)
    alog_r = _pad_lanes(ssm_a_log, 128)
    dskip_x = jnp.repeat(ssm_d, HEAD_DIM, axis=1)
    bias_c = ssm_dt_bias.reshape(N_HEADS, 1)
    alog_c = ssm_a_log.reshape(N_HEADS, 1)
    e_mat = _head_lane_matrix()
    e3_mat = jnp.tile(e_mat, (_EXACT_PIECES, 1))

    g_mix0, g_mix1 = norm_mix_g[0:1], norm_mix_g[1:2]
    g_mlp0, g_mlp1 = norm_mlp_g[0:1], norm_mlp_g[1:2]
    fg = final_g.reshape(1, D_MODEL)

    h1 = _pool_fwd(xs, g_mix0, pw, pool_b, pool_scale)
    hm0 = _rmsnorm_fwd(h1, g_mlp0, name="norm_mlp0")
    w1_0 = fetch("mlp0_up", hm0)
    u0 = _matmul(hm0, w1_0, "nn", name="mlp0_up", out_dtype=BF16, b_col_shards=True)
    w2_0 = fetch("mlp0_down", u0)
    h2 = _matmul(u0, w2_0, "nn", name="mlp0_down", a_relu2=True, add=h1)

    w_z, w_xbc, w_dt = fetch("in_proj", h2)
    hn1 = _rmsnorm_fwd(h2, g_mix1, name="norm_mix1")
    z = _matmul(hn1, w_z, "nt", name="in_proj_z", out_dtype=BF16)
    xbc = _matmul(hn1, w_xbc, "nt", name="in_proj_xbc", out_dtype=BF16)
    dt_raw = _matmul(hn1, w_dt, "nt", name="in_proj_dt")
    dt_raw_t = dt_raw[:, :N_HEADS].T
    xc = _conv_fwd(xbc, conv_w, conv_b)
    wout, w1_1, w2_1 = fetch("rest", xc)
    dt_r, acs_r, sg_r, acs_t = _ssd_decay(dt_raw, dt_raw_t, bias_r, bias_c, alog_r, alog_c)
    y, states = _ssd_fwd(xc, dt_r, acs_r, acs_t, dskip_x, e3_mat)
    yn = _gate_fwd(y, z, gate_g)
    h3 = _matmul(yn, wout, "nn", name="out_proj", add=h2)
    hm1 = _rmsnorm_fwd(h3, g_mlp1, name="norm_mlp1")
    u1 = _matmul(hm1, w1_1, "nn", name="mlp1_up", out_dtype=BF16, b_col_shards=True)
    h4 = _matmul(u1, w2_1, "nn", name="mlp1_down", a_relu2=True, add=h3)

    dh4, dh4_16, small_final = _loss_head(h4, fg, target)

    def mlp_bwd_weights(dh_out16, hm, u, w2_i, tag):
        du = _matmul(dh_out16, w2_i, "nt", name=tag + "_du", out_dtype=BF16, relu2_grad_of=u)
        dw2 = _matmul(u, dh_out16, "tn", name=tag + "_dw2", a_relu2=True)
        dw1 = _matmul(hm, du, "tn", name=tag + "_dw1", out_col_shards=N_CHIPS)
        return du, dw1, dw2.reshape(N_CHIPS, D_FF // N_CHIPS, D_MODEL)

    def mlp_bwd_input(du, dh_out, h_in, w1_i, g_i, tag):
        return _matmul(du, w1_i, "nt", name=tag + "_dhm", b_col_shards=True, norm_bwd=(h_in, g_i, dh_out))

    du1, dw1_1, dw2_1 = mlp_bwd_weights(dh4_16, hm1, u1, w2_1, "mlp1")
    dh3, dh3_16, dg_mlp1 = mlp_bwd_input(du1, dh4, h3, w1_1, g_mlp1, "mlp1")

    dyn = _matmul(dh3_16, wout, "nt", name="out_proj_dyn")
    dwout = _matmul(yn, dh3_16, "tn", name="out_proj_dw").reshape(N_CHIPS, D_INNER // N_CHIPS, D_MODEL)
    behind = reduce_start("mlp1_out", [dw1_1, dw2_1, dwout])
    dy, dz, dg_gate = _gate_bwd(dyn, y, z, gate_g + behind)
    behind = reduce_midway("mlp1_out", dz)
    dxc, ddt_raw, small_ssd = _ssd_bwd(xc, dt_r, acs_r, sg_r, acs_t, alog_r, dskip_x + behind,
                                       e3_mat, e_mat.T, states, dy)
    dv, dconv = _conv_bwd_act(xbc, dxc, conv_w, conv_b)
    dxbc = _conv_bwd_in(dv, conv_w)
    dw_z = _matmul(dz, hn1, "tn", name="in_proj_z_dw")
    dw_xbc = _matmul(dxbc, hn1, "tn", name="in_proj_xbc_dw")
    dw_dt = _matmul(ddt_raw, hn1, "tn", name="in_proj_dt_dw")
    dwin = jnp.concatenate([dw_z, dw_xbc, dw_dt[:N_HEADS]], axis=0)
    dwin = _pad_rows(dwin.reshape(N_CHIPS, IN_PROJ_DIM // N_CHIPS, D_MODEL), _IN_PROJ_SHARD_ROWS)
    behind = reduce_start("in_proj", [dwin])
    dhn1 = _matmul(ddt_raw, w_dt, "nn", name="in_proj_dt_dh")
    dhn1 = _matmul(dz, w_z, "nn", name="in_proj_z_dh", add=dhn1)
    dh2, dh2_16, dg_mix1 = _matmul(dxbc, w_xbc, "nn", name="in_proj_xbc_dh", add=dhn1,
                                   norm_bwd=(h2, g_mix1 + behind, dh3))
    behind = reduce_midway("in_proj", dh2_16)

    du0, dw1_0, dw2_0 = mlp_bwd_weights(dh2_16, hm0, u0, w2_0, "mlp0")
    dh1, _, dg_mlp0 = mlp_bwd_input(du0, dh2, h1, w1_0, g_mlp0 + behind, "mlp0")
    dx, dpw, small_pool = _pool_bwd(xs, g_mix0, pw, pool_b, pool_scale, dh1)
    dpw = jnp.transpose(dpw.reshape(4, N_CHIPS, POOL_GROUP // N_CHIPS, POOL_GROUP), (1, 0, 2, 3))
    dpw = dpw.reshape(N_CHIPS, 4 * (POOL_GROUP // N_CHIPS), POOL_GROUP)

    big = [dpw, dw1_0, dw2_0]
    rows = [
        small_final[1:2],
        small_final[0:1],
        small_pool[0:1], dg_mix1[0:1],
        dg_mlp0[0:1], dg_mlp1[0:1],
        small_pool[1:2], small_pool[2:3],
        _pad_lanes(small_ssd[0:3], D_MODEL),
        _rows_1024(dg_gate[0:1]),
        _rows_1024(dconv[0:CONV_K]),
        _rows_1024(dconv[CONV_K:CONV_K + 1]),
    ]
    return dx, big, rows


def kernel(x, norm_mix_g, norm_mlp_g, pool_w, pool_b, pool_scale, ssm_w_in, ssm_conv_w, ssm_conv_b, ssm_dt_bias, ssm_a_log, ssm_d, ssm_norm_g, ssm_w_out, mlp_w1, mlp_w2, final_g, loss_target, m_norm_mix_g, m_norm_mlp_g, m_pool_w, m_pool_b, m_pool_scale, m_ssm_w_in, m_ssm_conv_w, m_ssm_conv_b, m_ssm_dt_bias, m_ssm_a_log, m_ssm_d, m_ssm_norm_g, m_ssm_w_out, m_mlp_w1, m_mlp_w2, m_final_g, v_norm_mix_g, v_norm_mlp_g, v_pool_w, v_pool_b, v_pool_scale, v_ssm_w_in, v_ssm_conv_w, v_ssm_conv_b, v_ssm_dt_bias, v_ssm_a_log, v_ssm_d, v_ssm_norm_g, v_ssm_w_out, v_mlp_w1, v_mlp_w2, v_final_g):
    t_dim = x.shape[1]
    xs = x[0]
    target = loss_target[0]
    my_x, my_y, my_c = _position()
    my_chip = 2 * my_x + my_y

    def halves(w):
        return w.astype(BF16).reshape((2, w.shape[0] // 2) + w.shape[1:])

    def whole(gathered, own_shard):
        g = lax.dynamic_update_index_in_dim(gathered, own_shard, my_chip, axis=0)
        return g.reshape((N_CHIPS, 2 * g.shape[2]) + g.shape[3:])

    def gather_lands(own):
        return [jax.ShapeDtypeStruct((N_CHIPS,) + s.shape, s.dtype) for s in own]

    vec_cols = CONV_DIM // N_CHIPS
    vec_own = jnp.concatenate([ssm_conv_w[0], ssm_conv_b, _pad_lanes(ssm_norm_g, vec_cols)], axis=0)
    early_own = [halves(pool_w[0]), vec_own.reshape(2, (CONV_K + 2) // 2, vec_cols)]
    early, behind_early = _all_gather_weights(early_own)
    g_pool, g_vec = [whole(g, o) for g, o in zip(early, early_own)]
    pw = jnp.transpose(g_pool, (1, 0, 2, 3)).reshape(4, POOL_GROUP, POOL_GROUP)
    conv_w = jnp.transpose(g_vec[:, 0:CONV_K, :], (1, 0, 2)).reshape(CONV_K, CONV_DIM)
    conv_b = g_vec[:, CONV_K, :].reshape(1, CONV_DIM)
    gate_g = g_vec[:, CONV_K + 1, :D_INNER // N_CHIPS].reshape(1, D_INNER)

    def behind_it(zero, ws):
        return [halves(w + zero) for w in ws]

    fetches = {}
    up_own = behind_it(behind_early, [mlp_w1[0]])
    fetches["mlp0_up"], behind_gather = _exchange_start(
        "gather_mlp0_up_start", _gather_copies, len(up_own) * (N_CHIPS - 1), up_own, gather_lands(up_own))
    down_own = behind_it(behind_gather, [mlp_w2[0]])
    fetches["mlp0_down"], behind_gather = _exchange_start(
        "gather_mlp0_down_start", _gather_copies, len(down_own) * (N_CHIPS - 1), down_own, gather_lands(down_own))
    in_own = behind_it(behind_gather, [_pad_rows(ssm_w_in[0].T, _IN_PROJ_SHARD_ROWS)])
    fetches["in_proj"], behind_gather = _exchange_start(
        "gather_in_proj_start", _gather_copies, len(in_own) * (N_CHIPS - 1), in_own, gather_lands(in_own))

    def fetch(what, after):
        if what == "mlp0_up":
            own_thru, landed = _exchange_wait("gather_mlp0_up_wait", _gather_copies, fetches[what], after)
            landed, _ = _forward_halves(landed, name="forward_mlp0_up")
            return whole(landed[0], own_thru[0])
        if what == "mlp0_down":
            own_thru, landed = _exchange_wait("gather_mlp0_down_wait", _gather_copies, fetches[what], after)
            landed, _ = _forward_halves(landed, name="forward_mlp0_down")
            return whole(landed[0], own_thru[0]).reshape(D_FF, D_MODEL)
        if what == "in_proj":
            own_thru, landed = _exchange_wait("gather_in_proj_wait", _gather_copies, fetches["in_proj"], after)
            landed, behind = _forward_halves(landed, name="forward_in_proj")
            rest = behind_it(behind, [ssm_w_out[0], mlp_w1[1], mlp_w2[1]])
            fetches["rest"], behind = _exchange_start(
                "gather_rest_start", _gather_copies, len(rest) * (N_CHIPS - 1), rest, gather_lands(rest))
            win = whole(landed[0], own_thru[0])[:, :IN_PROJ_DIM // N_CHIPS].reshape(IN_PROJ_DIM, D_MODEL)
            w_dt = _pad_rows(win[D_INNER + CONV_DIM:], 128) + behind.astype(BF16)
            return win[:D_INNER], win[D_INNER:D_INNER + CONV_DIM], w_dt
        own_thru, landed = _exchange_wait("gather_rest_wait", _gather_copies, fetches["rest"], after)
        landed, _ = _forward_halves(landed, name="forward_rest")
        g_wout, w1_1, g_w2_1 = [whole(g, o) for g, o in zip(landed, own_thru)]
        return g_wout.reshape(D_INNER, D_MODEL), w1_1, g_w2_1.reshape(D_FF, D_MODEL)

    place = jnp.stack([my_c, my_chip]).astype(jnp.int32)
    waves = {}

    def reduce_start(wave, grads):
        waves[wave] = {}
        waves[wave]["pair"], behind = _exchange_start(
            "pair_%s_start" % wave, _pair_copies, len(grads), grads, _pair_lands(grads))
        return behind

    def reduce_midway(wave, after):
        st = waves[wave]
        grads, recv = _exchange_wait("pair_%s_wait" % wave, _pair_copies, st["pair"], after)
        sums = [_pair_sum(g, r, place, name="pair_sum_%s_%d" % (wave, i))
                for i, (g, r) in enumerate(zip(grads, recv))]
        st["f32"] = [s32 for _, s32 in sums]
        b16 = [s16 for s16, _ in sums]
        st["chip"], behind = _exchange_start(
            "chip_%s_start" % wave, _chip_copies, len(b16) * (N_CHIPS - 1), b16, _same_lands(b16))
        return behind

    def reduce_finish(wave, after):
        st = waves[wave]
        _, got = _exchange_wait("chip_%s_wait" % wave, _chip_copies, st["chip"], after)
        return [_chip_sum(s32, r, place, name="chip_sum_%s_%d" % (wave, i))
                for i, (s32, r) in enumerate(zip(st["f32"], got))]

    dx, big0, rows = _local_step(xs, target, pw, fetch, reduce_start, reduce_midway,
                                 conv_w, conv_b, gate_g,
                                 norm_mix_g + behind_gather, norm_mlp_g, pool_b, pool_scale,
                                 ssm_dt_bias, ssm_a_log, ssm_d, final_g)

    behind = reduce_start("layer0", big0)
    small = jnp.concatenate(rows, axis=0)
    small = jnp.pad(small, ((0, (-small.shape[0]) % 8), (0, 0))) + behind
    small = _all_reduce_small(small, name="all_reduce_small", sum_row0=True)
    behind = reduce_midway("layer0", small)
    h_w1_1, h_w2_1, h_wout = reduce_finish("mlp1_out", behind)
    (h_win,) = reduce_finish("in_proj", behind)
    g_w1_1, g_w2_1, g_wout_s, g_win_s = _pair_gather_halves([h_w1_1, h_w2_1, h_wout, h_win],
                                                            name="pair_gather_layer1")
    loss = small[0, 0]
    g_final = small[1]
    g_norm_mix = small[2:4]
    g_norm_mlp = small[4:6]
    g_pool_b, g_pool_scale = small[6:7], small[7:8]
    g_alog, g_dtb, g_dsk = small[8:9, :N_HEADS], small[9:10, :N_HEADS], small[10:11, :N_HEADS]
    g_gate_full = small[11:13].reshape(1, D_INNER)
    g_convw_full = small[13:25].reshape(CONV_K, CONV_DIM)
    g_convb_full = small[25:28].reshape(1, CONV_DIM)
    g_gate = lax.dynamic_slice_in_dim(g_gate_full, my_chip * (D_INNER // N_CHIPS), D_INNER // N_CHIPS, axis=1)
    g_convw = lax.dynamic_slice_in_dim(g_convw_full, my_chip * (CONV_DIM // N_CHIPS), CONV_DIM // N_CHIPS, axis=1)
    g_convb = lax.dynamic_slice_in_dim(g_convb_full, my_chip * (CONV_DIM // N_CHIPS), CONV_DIM // N_CHIPS, axis=1)

    grads = {
        "norm_mix_g": g_norm_mix, "norm_mlp_g": g_norm_mlp,
        "pool_b": g_pool_b, "pool_scale": g_pool_scale,
        "ssm_conv_w": g_convw.reshape(ssm_conv_w.shape),
        "ssm_conv_b": g_convb, "ssm_dt_bias": g_dtb, "ssm_a_log": g_alog, "ssm_d": g_dsk,
        "ssm_norm_g": g_gate, "ssm_w_out": g_wout_s.reshape(ssm_w_out.shape),
        "final_g": g_final,
    }
    weights = dict(norm_mix_g=norm_mix_g, norm_mlp_g=norm_mlp_g, pool_w=pool_w, pool_b=pool_b,
                   pool_scale=pool_scale, ssm_w_in=ssm_w_in, ssm_conv_w=ssm_conv_w, ssm_conv_b=ssm_conv_b,
                   ssm_dt_bias=ssm_dt_bias, ssm_a_log=ssm_a_log, ssm_d=ssm_d, ssm_norm_g=ssm_norm_g,
                   ssm_w_out=ssm_w_out, mlp_w1=mlp_w1, mlp_w2=mlp_w2, final_g=final_g)
    moms = dict(norm_mix_g=(m_norm_mix_g, v_norm_mix_g), norm_mlp_g=(m_norm_mlp_g, v_norm_mlp_g),
                pool_w=(m_pool_w, v_pool_w), pool_b=(m_pool_b, v_pool_b),
                pool_scale=(m_pool_scale, v_pool_scale), ssm_w_in=(m_ssm_w_in, v_ssm_w_in),
                ssm_conv_w=(m_ssm_conv_w, v_ssm_conv_w), ssm_conv_b=(m_ssm_conv_b, v_ssm_conv_b),
                ssm_dt_bias=(m_ssm_dt_bias, v_ssm_dt_bias), ssm_a_log=(m_ssm_a_log, v_ssm_a_log),
                ssm_d=(m_ssm_d, v_ssm_d), ssm_norm_g=(m_ssm_norm_g, v_ssm_norm_g),
                ssm_w_out=(m_ssm_w_out, v_ssm_w_out), mlp_w1=(m_mlp_w1, v_mlp_w1),
                mlp_w2=(m_mlp_w2, v_mlp_w2), final_g=(m_final_g, v_final_g))
    names = list(weights)
    big_names = ("pool_w", "ssm_w_in", "ssm_w_out", "mlp_w1", "mlp_w2")
    deltas, new_m, new_v = {}, {}, {}

    def as_rows(nm, a):
        return a[0].T if nm == "ssm_w_in" else a.reshape(-1, a.shape[-1])

    def from_rows(nm, r):
        return r.T[None] if nm == "ssm_w_in" else r.reshape(weights[nm].shape)

    def update(nm, grad_rows, layer=None, into=None):
        return _adamw(as_rows(nm, weights[nm]), grad_rows, as_rows(nm, moms[nm][0]), as_rows(nm, moms[nm][1]),
                      name="adamw_%s_%s" % (nm, layer), part=layer, into=into)

    def keep(nm, results):
        deltas[nm], new_m[nm], new_v[nm], grads[nm] = [from_rows(nm, r) for r in results]

    keep("ssm_w_in", update("ssm_w_in", g_win_s[:IN_PROJ_DIM // N_CHIPS]))
    keep("ssm_w_out", update("ssm_w_out", g_wout_s))
    w1_done = update("mlp_w1", g_w1_1, layer=1)
    w2_done = update("mlp_w2", g_w2_1, layer=1)
    small_names = [nm for nm in names if nm not in big_names]
    small_done = _adamw_small(
        [tuple(as_rows(nm, a) for a in (weights[nm], grads[nm], moms[nm][0], moms[nm][1])) for nm in small_names],
        name="adamw_small")
    for nm, (d_, m_, v_) in zip(small_names, small_done):
        deltas[nm], new_m[nm], new_v[nm] = [from_rows(nm, r) for r in (d_, m_, v_)]

    above = (deltas["ssm_w_in"][0, 0, 0] + deltas["ssm_w_out"][0, 0, 0] + w1_done[0][-1, -1]
             + w2_done[0][-1, -1] + small_done[0][0][0, 0])
    g_pool_w, g_w1_0, g_w2_0 = _pair_gather_halves(reduce_finish("layer0", above), name="pair_gather_layer0")
    keep("mlp_w1", update("mlp_w1", g_w1_0, layer=0, into=w1_done))
    keep("mlp_w2", update("mlp_w2", g_w2_0, layer=0, into=w2_done))
    keep("pool_w", update("pool_w", g_pool_w))

    grad_x = dx.reshape(x.shape)
    out_grads = [grads[nm].reshape(weights[nm].shape) for nm in names]
    return (loss, grad_x, *out_grads, *[deltas[nm] for nm in names],
            *[new_m[nm] for nm in names], *[new_v[nm] for nm in names])
```

```python
import functools

import jax
import jax.numpy as jnp
from jax import lax
from jax.experimental import pallas as pl
from jax.experimental.pallas import tpu as pltpu

F32 = jnp.float32
BF16 = jnp.bfloat16
MESH = pl.DeviceIdType.MESH

D_MODEL = 1024
RMS_EPS = 1e-5
POOL_WINDOWS = (2, 4, 8, 16)
POOL_GROUP = 256
POOL_HALO = 16
D_INNER = 2048
HEAD_DIM = 64
N_HEADS = 32
N_GROUPS = 4
HEADS_PER_GROUP = 8
D_STATE = 128
CONV_K = 4
CONV_HALO = 8
CHUNK = 128
CONV_DIM = 3072
IN_PROJ_DIM = 5152
D_FF = 4096
N_CHIPS = 4
N_DEV = 8

ADAM_LR = 0.001
ADAM_B1 = 0.9
ADAM_B2 = 0.999
ADAM_EPS = 1e-08
ADAM_WD = 0.01
ADAM_STEP = 10

VMEM_LIMIT = 56 * 1024 * 1024
NEG_INF = float("-inf")


def _pcall(body, **kw):
    return pl.pallas_call(body, **kw)


def _params(*sem):
    return pltpu.CompilerParams(dimension_semantics=sem, vmem_limit_bytes=VMEM_LIMIT)


def _sigmoid(v):
    return 1.0 / (1.0 + jnp.exp(-v))


def _row_spec(tb, d, nb=None, reverse=False):
    if reverse:
        return pl.BlockSpec((tb, d), lambda i: (nb - 1 - i, 0))
    return pl.BlockSpec((tb, d), lambda i: (i, 0))


def _const_spec(shape):
    return pl.BlockSpec(shape, lambda *_: tuple(0 for _ in shape))


_DIMS = {"nn": (((1,), (0,)), ((), ())),
         "nt": (((1,), (1,)), ((), ())),
         "tn": (((0,), (0,)), ((), ()))}


_MATMUL_VMEM_BUDGET = 40 * 1024 * 1024


def _matmul_tiles(m_dim, n_dim, k_dim, a_bytes, b_bytes, mn_bytes):
    tm, tn = min(m_dim, 1024), min(n_dim, 1024)
    while 2 * (tm * k_dim * a_bytes + tn * k_dim * b_bytes + tm * tn * mn_bytes) > _MATMUL_VMEM_BUDGET:
        if tm >= tn:
            tm //= 2
        else:
            tn //= 2
    return tm, tn


def _matmul(a, b, mode, *, name, out_dtype=F32, a_relu2=False, a_norm=None, add=None, relu2_grad_of=None,
            out_col_shards=1, b_col_shards=False, norm_bwd=None, loss_head=None):
    if mode == "tn":
        k_dim, m_dim = a.shape
    else:
        m_dim, k_dim = a.shape
    if b_col_shards:
        n_shards, shard_cols = b.shape[0], b.shape[2]
        n_dim = n_shards * shard_cols if mode == "nn" else b.shape[1]
    else:
        n_dim = b.shape[0] if mode == "nt" else b.shape[1]
    mn_bytes = jnp.dtype(out_dtype).itemsize
    if relu2_grad_of is not None:
        mn_bytes += relu2_grad_of.dtype.itemsize
    if add is not None:
        mn_bytes += add.dtype.itemsize
    row_epilogue = norm_bwd is not None or loss_head is not None
    if row_epilogue:
        assert out_dtype == F32 and out_col_shards == 1 and (norm_bwd is None or loss_head is None)
        mn_bytes += 4 + 4 + 2 + 8
    a_bytes = a.dtype.itemsize + (2 if a_norm is not None else 0)
    tm, tn = _matmul_tiles(m_dim, n_dim, k_dim, a_bytes, b.dtype.itemsize, mn_bytes)
    if row_epilogue:
        while tn < n_dim:
            tm, tn = tm // 2, tn * 2
    assert m_dim % tm == 0 and n_dim % tn == 0
    a_spec = (pl.BlockSpec((k_dim, tm), lambda i, j: (0, i)) if mode == "tn"
              else pl.BlockSpec((tm, k_dim), lambda i, j: (i, 0)))
    if b_col_shards and mode == "nn":
        assert shard_cols % tn == 0
        per_shard = shard_cols // tn
        b_spec = pl.BlockSpec((None, k_dim, tn), lambda i, j: (j // per_shard, 0, j % per_shard))
    elif b_col_shards:
        assert mode == "nt" and k_dim == n_shards * shard_cols
        b_spec = pl.BlockSpec((n_shards, tn, shard_cols), lambda i, j: (0, j, 0))
    else:
        b_spec = (pl.BlockSpec((tn, k_dim), lambda i, j: (j, 0)) if mode == "nt"
                  else pl.BlockSpec((k_dim, tn), lambda i, j: (0, j)))
    mn_spec = pl.BlockSpec((tm, tn), lambda i, j: (i, j))
    operands, in_specs = [a, b], [a_spec, b_spec]
    if relu2_grad_of is not None:
        operands.append(relu2_grad_of)
        in_specs.append(mn_spec)
    if add is not None:
        operands.append(add)
        in_specs.append(mn_spec)
    gain_spec = pl.BlockSpec((1, n_dim), lambda i, j: (0, 0))
    if a_norm is not None:
        assert mode in ("nn", "nt") and a.dtype == F32 and not row_epilogue and out_col_shards == 1
        operands.append(a_norm)
        in_specs.append(pl.BlockSpec((1, k_dim), lambda i, j: (0, 0)))
    if norm_bwd is not None:
        h_in, g_in, dres_in = norm_bwd
        operands += [h_in, g_in, dres_in]
        in_specs += [mn_spec, gain_spec, mn_spec]
    if loss_head is not None:
        operands += list(loss_head)
        in_specs += [gain_spec, mn_spec]
    if row_epilogue:
        out_shape = (jax.ShapeDtypeStruct((m_dim, n_dim), F32), jax.ShapeDtypeStruct((m_dim, n_dim), BF16),
                     jax.ShapeDtypeStruct((8, n_dim), F32))
        out_spec = (mn_spec, mn_spec, pl.BlockSpec((8, n_dim), lambda i, j: (0, 0)))
    elif out_col_shards == 1:
        out_shape = jax.ShapeDtypeStruct((m_dim, n_dim), out_dtype)
        out_spec = mn_spec
    else:
        n_shard = n_dim // out_col_shards
        assert n_shard % tn == 0
        per = n_shard // tn
        out_shape = jax.ShapeDtypeStruct((out_col_shards, m_dim, n_shard), out_dtype)
        out_spec = pl.BlockSpec((None, tm, tn), lambda i, j: (j // per, i, j % per))
    if a_norm is not None:
        out_shape = (out_shape, jax.ShapeDtypeStruct((m_dim, k_dim), BF16))
        out_spec = (out_spec, pl.BlockSpec((tm, k_dim), lambda i, j: (i, 0)))

    n_in = len(operands)

    def body(*refs):
        a_ref, b_ref, o_ref = refs[0], refs[1], refs[n_in]
        if a_norm is not None:
            normed_ref = refs[n_in + 1]

            @pl.when(pl.program_id(1) == 0)
            def _():
                xa = a_ref[...]
                normed_ref[...] = (xa * _rms(xa) * refs[n_in - 1][...]).astype(BF16)

            av = normed_ref[...]
        else:
            av = a_ref[...]
        if a_relu2:
            av = jnp.maximum(av, 0)
            av = av * av
        if b_col_shards and mode == "nt":
            r = None
            for s in range(n_shards):
                part = lax.dot_general(av[:, s * shard_cols:(s + 1) * shard_cols].astype(BF16),
                                       b_ref[s].astype(BF16), _DIMS[mode], preferred_element_type=F32)
                r = part if r is None else r + part
        else:
            r = lax.dot_general(av.astype(BF16), b_ref[...].astype(BF16), _DIMS[mode],
                                preferred_element_type=F32)
        nxt = 2
        if relu2_grad_of is not None:
            r = r * (2.0 * jnp.maximum(refs[nxt][...].astype(F32), 0.0))
            nxt += 1
        if add is not None:
            r = r + refs[nxt][...]
            nxt += 1
        if not row_epilogue:
            o_ref[...] = r.astype(out_dtype)
            return
        dh16_ref, small_ref = refs[n_in + 1], refs[n_in + 2]

        @pl.when(pl.program_id(0) == 0)
        def _():
            small_ref[...] = jnp.zeros_like(small_ref)

        if norm_bwd is not None:
            h_ref, g_ref, dres_ref = refs[nxt:nxt + 3]
            x, dy = h_ref[...], r
        else:
            g_ref, t_ref = refs[nxt:nxt + 2]
            x = r
        rr = _rms(x)
        xhat = x * rr
        gv = g_ref[...]
        if loss_head is not None:
            err = xhat * gv - t_ref[...]
            small_ref[1:2, :] += (0.5 / n_dim) * jnp.sum(err * err, axis=0, keepdims=True)
            dy = err * (1.0 / n_dim)
        dxhat = dy * gv
        dh = rr * (dxhat - xhat * jnp.mean(dxhat * xhat, axis=-1, keepdims=True))
        if norm_bwd is not None:
            dh = dres_ref[...] + dh
        o_ref[...] = dh
        dh16_ref[...] = dh.astype(BF16)
        small_ref[0:1, :] += jnp.sum(dy * xhat, axis=0, keepdims=True)

    if row_epilogue:
        semantics = ("arbitrary", "arbitrary")
    elif a_norm is not None:
        semantics = ("parallel", "arbitrary")
    else:
        semantics = ("parallel", "parallel")
    return _pcall(
        body, name=name, out_shape=out_shape,
        grid=(m_dim // tm, n_dim // tn),
        in_specs=in_specs, out_specs=out_spec,
        compiler_params=_params(*semantics),
    )(*operands)


def _rms(x):
    return lax.rsqrt(jnp.mean(x * x, axis=-1, keepdims=True) + RMS_EPS)


def _pool_mixed(ext, hn, t0, tb):
    t = t0 + lax.broadcasted_iota(jnp.int32, (tb, 1), 0)
    parts = []
    for gi, w in enumerate(POOL_WINDOWS):
        lanes = slice(gi * POOL_GROUP, (gi + 1) * POOL_GROUP)
        s = ext[:, lanes]
        k = 1
        while k < w:
            s = s + pltpu.roll(s, k, 0)
            k *= 2
        cnt = jnp.minimum(t + 1, w).astype(F32)
        parts.append(s[POOL_HALO:, :] / cnt - hn[:, lanes])
    return parts


def _pool_fwd(x, g, pw, pb, ps, *, tb=512):
    t_dim, d = x.shape

    def body(x_ref, g_ref, pw_ref, pb_ref, ps_ref, o_ref, ext_ref):
        i = pl.program_id(0)

        @pl.when(i == 0)
        def _():
            ext_ref[0:POOL_HALO, :] = jnp.zeros((POOL_HALO, d), F32)

        xv = x_ref[...]
        hn = xv * _rms(xv) * g_ref[...]
        ext_ref[POOL_HALO:, :] = hn
        mixed = _pool_mixed(ext_ref[...], hn, i * tb, tb)
        for gi in range(len(POOL_WINDOWS)):
            lanes = slice(gi * POOL_GROUP, (gi + 1) * POOL_GROUP)
            out = jnp.dot(mixed[gi].astype(BF16), pw_ref[gi], preferred_element_type=F32)
            o_ref[:, lanes] = xv[:, lanes] + (out + pb_ref[:, lanes]) * ps_ref[:, lanes]
        ext_ref[0:POOL_HALO, :] = hn[tb - POOL_HALO:, :]

    return _pcall(
        body, name="pool_fwd", out_shape=jax.ShapeDtypeStruct((t_dim, d), F32),
        grid=(t_dim // tb,),
        in_specs=[_row_spec(tb, d), _const_spec((1, d)), _const_spec((4, POOL_GROUP, POOL_GROUP)),
                  _const_spec((1, d)), _const_spec((1, d))],
        out_specs=_row_spec(tb, d),
        scratch_shapes=[pltpu.VMEM((POOL_HALO + tb, d), F32)],
        compiler_params=_params("arbitrary"),
    )(x, g, pw, pb, ps)


def _pool_bwd(x, g, pw, pb, ps, dh1, *, tb=512):
    t_dim, d = x.shape
    nb = t_dim // tb
    halo_per_block = tb // POOL_HALO

    def body(x_ref, xprev_ref, g_ref, pw_ref, pb_ref, ps_ref, dh1_ref,
             dx_ref, dpw_ref, small_ref, ext_ref, dext_ref):
        i = pl.program_id(0)
        blk = nb - 1 - i

        @pl.when(i == 0)
        def _():
            dpw_ref[...] = jnp.zeros_like(dpw_ref)
            small_ref[...] = jnp.zeros_like(small_ref)
            dext_ref[tb:, :] = jnp.zeros((POOL_HALO, d), F32)

        gv = g_ref[...]
        xv = x_ref[...]
        r = _rms(xv)
        xhat = xv * r
        hn = xhat * gv
        xp = xprev_ref[...]
        hprev = xp * _rms(xp) * gv * (blk > 0).astype(F32)
        ext_ref[0:POOL_HALO, :] = hprev
        ext_ref[POOL_HALO:, :] = hn
        mixed = _pool_mixed(ext_ref[...], hn, blk * tb, tb)

        dout = dh1_ref[...]
        t = blk * tb + lax.broadcasted_iota(jnp.int32, (tb, 1), 0)
        for gi, w in enumerate(POOL_WINDOWS):
            lanes = slice(gi * POOL_GROUP, (gi + 1) * POOL_GROUP)
            mb = mixed[gi].astype(BF16)
            pre = jnp.dot(mb, pw_ref[gi], preferred_element_type=F32) + pb_ref[:, lanes]
            dg_out = dout[:, lanes]
            small_ref[2:3, lanes] += jnp.sum(dg_out * pre, axis=0, keepdims=True)
            dpre = dg_out * ps_ref[:, lanes]
            small_ref[1:2, lanes] += jnp.sum(dpre, axis=0, keepdims=True)
            dpb16 = dpre.astype(BF16)
            dpw_ref[gi] += lax.dot_general(mb, dpb16, _DIMS["tn"], preferred_element_type=F32)
            dmixed = lax.dot_general(dpb16, pw_ref[gi], _DIMS["nt"], preferred_element_type=F32)
            cnt = jnp.minimum(t + 1, w).astype(F32)
            dq = dmixed / cnt
            dext_ref[0:tb, lanes] = dq
            s = dext_ref[:, lanes]
            k = 1
            while k < w:
                s = s + pltpu.roll(s, tb + POOL_HALO - k, 0)
                k *= 2
            dhn = s[0:tb, :] - dmixed
            dext_ref[tb:, lanes] = dq[0:POOL_HALO, :]
            small_ref[0:1, lanes] += jnp.sum(dhn * xhat[:, lanes], axis=0, keepdims=True)
            ext_ref[POOL_HALO:, lanes] = dhn * gv[:, lanes]
        dxhat = ext_ref[POOL_HALO:, :]
        dx_ref[...] = dout + r * (dxhat - xhat * jnp.mean(dxhat * xhat, axis=-1, keepdims=True))

    return _pcall(
        body, name="pool_bwd",
        out_shape=(jax.ShapeDtypeStruct((t_dim, d), F32),
                   jax.ShapeDtypeStruct((4, POOL_GROUP, POOL_GROUP), F32),
                   jax.ShapeDtypeStruct((8, d), F32)),
        grid=(nb,),
        in_specs=[_row_spec(tb, d, nb, True),
                  pl.BlockSpec((POOL_HALO, d),
                               lambda i: (jnp.maximum((nb - 1 - i) * halo_per_block - 1, 0), 0)),
                  _const_spec((1, d)), _const_spec((4, POOL_GROUP, POOL_GROUP)),
                  _const_spec((1, d)), _const_spec((1, d)), _row_spec(tb, d, nb, True)],
        out_specs=(_row_spec(tb, d, nb, True), _const_spec((4, POOL_GROUP, POOL_GROUP)),
                   _const_spec((8, d))),
        scratch_shapes=[pltpu.VMEM((POOL_HALO + tb, d), F32), pltpu.VMEM((tb + POOL_HALO, d), F32)],
        compiler_params=_params("arbitrary"),
    )(x, x, g, pw, pb, ps, dh1)


_CONV_CB = 1024
_STRIP = 16


def _strips(tb, fn, unroll=4):
    def step(i, carry):
        fn(pl.multiple_of(i * _STRIP, _STRIP))
        return carry
    lax.fori_loop(0, tb // _STRIP, step, 0, unroll=unroll)


def _conv_taps(ext_ref, r0, w):
    shifted = [ext_ref[CONV_HALO + r0 - sh:CONV_HALO + r0 - sh + _STRIP, :] for sh in range(CONV_K)]
    acc = shifted[0] * w[CONV_K - 1:CONV_K, :]
    for sh in range(1, CONV_K):
        acc = acc + shifted[sh] * w[CONV_K - 1 - sh:CONV_K - sh, :]
    return shifted, acc


def _conv_fwd(u, w, b, *, tb=512):
    t_dim, c = u.shape
    cb = _CONV_CB

    def body(u_ref, w_ref, b_ref, o_ref, ext_ref):
        @pl.when(pl.program_id(1) == 0)
        def _():
            ext_ref[0:CONV_HALO, :] = jnp.zeros((CONV_HALO, cb), F32)

        wv = w_ref[...]
        bv = b_ref[...]

        def fill(r0):
            ext_ref[pl.ds(CONV_HALO + r0, _STRIP), :] = u_ref[pl.ds(r0, _STRIP), :].astype(F32)

        _strips(tb, fill)
        for r0 in range(0, tb, _STRIP):
            v = _conv_taps(ext_ref, r0, wv)[1] + bv
            o_ref[r0:r0 + _STRIP, :] = (v * _sigmoid(v)).astype(BF16)
        ext_ref[0:CONV_HALO, :] = ext_ref[tb:tb + CONV_HALO, :]

    blk = pl.BlockSpec((tb, cb), lambda j, t: (t, j))
    return _pcall(
        body, name="conv_fwd", out_shape=jax.ShapeDtypeStruct((t_dim, c), BF16),
        grid=(c // cb, t_dim // tb),
        in_specs=[blk, pl.BlockSpec((CONV_K, cb), lambda j, t: (0, j)),
                  pl.BlockSpec((1, cb), lambda j, t: (0, j))],
        out_specs=blk,
        scratch_shapes=[pltpu.VMEM((CONV_HALO + tb, cb), F32)],
        compiler_params=_params("parallel", "arbitrary"),
    )(u, w, b)


def _conv_bwd_act(u, dxc, w, b, *, tb=512):
    t_dim, c = u.shape
    cb = _CONV_CB
    half = _STRIP // 2

    def body(u_ref, d_ref, w_ref, b_ref, dv_ref, dwb_ref, ext_ref, acc_ref):
        @pl.when(pl.program_id(1) == 0)
        def _():
            ext_ref[0:CONV_HALO, :] = jnp.zeros((CONV_HALO, cb), F32)
            dwb_ref[...] = jnp.zeros_like(dwb_ref)

        acc_ref[...] = jnp.zeros_like(acc_ref)
        wv = w_ref[...]
        bv = b_ref[...]

        def fill(r0):
            ext_ref[pl.ds(CONV_HALO + r0, _STRIP), :] = u_ref[pl.ds(r0, _STRIP), :].astype(F32)

        _strips(tb, fill)
        for r0 in range(0, tb, _STRIP):
            shifted, v = _conv_taps(ext_ref, r0, wv)
            v = v + bv
            sg = _sigmoid(v)
            dv = d_ref[r0:r0 + _STRIP, :].astype(F32) * (sg * (1.0 + v * (1.0 - sg)))
            dv_ref[r0:r0 + _STRIP, :] = dv.astype(BF16)
            acc_ref[CONV_K] += dv[0:half, :] + dv[half:, :]
            for sh in range(CONV_K):
                p = dv * shifted[sh]
                acc_ref[CONV_K - 1 - sh] += p[0:half, :] + p[half:, :]
        for k in range(CONV_K + 1):
            dwb_ref[k:k + 1, :] += jnp.sum(acc_ref[k], axis=0, keepdims=True)
        ext_ref[0:CONV_HALO, :] = ext_ref[tb:tb + CONV_HALO, :]

    blk = pl.BlockSpec((tb, cb), lambda j, t: (t, j))
    return _pcall(
        body, name="conv_bwd_act",
        out_shape=(jax.ShapeDtypeStruct((t_dim, c), BF16), jax.ShapeDtypeStruct((8, c), F32)),
        grid=(c // cb, t_dim // tb),
        in_specs=[blk, blk, pl.BlockSpec((CONV_K, cb), lambda j, t: (0, j)),
                  pl.BlockSpec((1, cb), lambda j, t: (0, j))],
        out_specs=(blk, pl.BlockSpec((8, cb), lambda j, t: (0, j))),
        scratch_shapes=[pltpu.VMEM((CONV_HALO + tb, cb), F32), pltpu.VMEM((CONV_K + 1, half, cb), F32)],
        compiler_params=_params("parallel", "arbitrary"),
    )(u, dxc, w, b)


def _conv_bwd_in(dv, w, *, tb=512):
    t_dim, c = dv.shape
    cb = _CONV_CB
    nb = t_dim // tb

    def body(dv_ref, w_ref, du_ref, ext_ref):
        @pl.when(pl.program_id(1) == 0)
        def _():
            ext_ref[tb:, :] = jnp.zeros((CONV_HALO, cb), F32)

        wv = w_ref[...]

        def fill(r0):
            ext_ref[pl.ds(r0, _STRIP), :] = dv_ref[pl.ds(r0, _STRIP), :].astype(F32)

        _strips(tb, fill)
        for r0 in range(0, tb, _STRIP):
            acc = ext_ref[r0:r0 + _STRIP, :] * wv[CONV_K - 1:CONV_K, :]
            for sh in range(1, CONV_K):
                acc = acc + ext_ref[r0 + sh:r0 + sh + _STRIP, :] * wv[CONV_K - 1 - sh:CONV_K - sh, :]
            du_ref[r0:r0 + _STRIP, :] = acc.astype(BF16)
        ext_ref[tb:, :] = ext_ref[0:CONV_HALO, :]

    blk = pl.BlockSpec((tb, cb), lambda j, t: (nb - 1 - t, j))
    return _pcall(
        body, name="conv_bwd_in", out_shape=jax.ShapeDtypeStruct((t_dim, c), BF16),
        grid=(c // cb, nb),
        in_specs=[blk, pl.BlockSpec((CONV_K, cb), lambda j, t: (0, j))],
        out_specs=blk,
        scratch_shapes=[pltpu.VMEM((tb + CONV_HALO, cb), F32)],
        compiler_params=_params("parallel", "arbitrary"),
    )(dv, w)


def _softplus(v):
    e = jnp.exp(-jnp.abs(v))
    w = 1.0 + e
    log1p = jnp.where(w == 1.0, e, jnp.log(w) * e / jnp.where(w == 1.0, 1.0, w - 1.0))
    return jnp.maximum(v, 0.0) + log1p


def _cumsum_rows(v):
    row = lax.broadcasted_iota(jnp.int32, v.shape, 0) & (CHUNK - 1)
    k = 1
    while k < CHUNK:
        v = v + jnp.where(row >= k, pltpu.roll(v, k, 0), 0.0)
        k *= 2
    return v


def _cumsum_lanes(v):
    col = lax.broadcasted_iota(jnp.int32, v.shape, 1) & (CHUNK - 1)
    k = 1
    while k < CHUNK:
        v = v + jnp.where(col >= k, pltpu.roll(v, k, 1), 0.0)
        k *= 2
    return v


def _rev_cumsum_rows(v):
    row = lax.broadcasted_iota(jnp.int32, v.shape, 0)
    k = 1
    while k < CHUNK:
        v = v + jnp.where(row < CHUNK - k, pltpu.roll(v, CHUNK - k, 0), 0.0)
        k *= 2
    return v


PAIR = 2 * HEAD_DIM
GROUP_LANES = HEADS_PER_GROUP * HEAD_DIM


def _head_lane_matrix():
    h = lax.broadcasted_iota(jnp.int32, (128, D_INNER), 0)
    j = lax.broadcasted_iota(jnp.int32, (128, D_INNER), 1)
    return (j // HEAD_DIM == h).astype(BF16)


def _split_bf16(v, pieces):
    out = []
    for _ in range(pieces):
        p = v.astype(BF16)
        out.append(p)
        v = v - p.astype(F32)
    return out


_EXACT_PIECES = 3


def _expand_heads(values, e3):
    lhs = jnp.concatenate([jnp.concatenate(_split_bf16(v, _EXACT_PIECES), axis=1) for v in values], axis=0)
    out = jnp.dot(lhs, e3, preferred_element_type=F32)
    rows = values[0].shape[0]
    return [out[i * rows:(i + 1) * rows, :] for i in range(len(values))]


def _reduce_heads(v, et, pieces):
    return sum(jnp.dot(p, et, preferred_element_type=F32) for p in _split_bf16(v, pieces))


def _ssd_decay(dt_raw, dt_raw_t, bias_r, bias_c, alog_r, alog_c, *, tb=1024):
    t_dim = dt_raw.shape[0]
    tb = min(tb, t_dim)
    assert t_dim % tb == 0 and tb % CHUNK == 0

    def body(dtr_ref, dtt_ref, br_ref, bc_ref, ar_ref, ac_ref, dt_ref, acs_ref, sg_ref, acst_ref):
        pre = dtr_ref[...] + br_ref[...]
        dt = _softplus(pre)
        dt_ref[...] = dt
        sg_ref[...] = _sigmoid(pre)
        acs_ref[...] = _cumsum_rows(dt * (-jnp.exp(ar_ref[...])))
        acst_ref[...] = _cumsum_lanes(_softplus(dtt_ref[...] + bc_ref[...]) * (-jnp.exp(ac_ref[...])))

    rows = pl.BlockSpec((tb, 128), lambda i: (i, 0))
    cols = pl.BlockSpec((N_HEADS, tb), lambda i: (0, i))
    sds = jax.ShapeDtypeStruct((t_dim, 128), F32)
    return _pcall(
        body, name="ssd_decay",
        out_shape=(sds, sds, sds, jax.ShapeDtypeStruct((N_HEADS, t_dim), F32)),
        grid=(t_dim // tb,),
        in_specs=[rows, cols, _const_spec((1, 128)), _const_spec((N_HEADS, 1)),
                  _const_spec((1, 128)), _const_spec((N_HEADS, 1))],
        out_specs=(rows, rows, rows, cols), compiler_params=_params("parallel"),
    )(dt_raw, dt_raw_t, bias_r, bias_c, alog_r, alog_c)


def _pair_decay(acs_slab, acs_c, h0, causal, left):
    other = pltpu.roll(acs_slab, HEAD_DIM, 1)
    col0 = jnp.where(left, acs_slab, other)
    col1 = jnp.where(left, other, acs_slab)
    l0 = jnp.exp(jnp.where(causal, col0 - acs_c[h0:h0 + 1, :], NEG_INF))
    l1 = jnp.exp(jnp.where(causal, col1 - acs_c[h0 + 1:h0 + 2, :], NEG_INF))
    return l0, l1


def _ssd_fwd(xc, dt_r, acs_r, acs_t, dskip_x, e3_mat):
    t_dim = xc.shape[0]
    nc = t_dim // CHUNK

    def body(xc_ref, dt_ref, acs_ref, acst_ref, dk_ref, e3_ref, y_ref, st_ref, state):
        @pl.when(pl.program_id(0) == 0)
        def _():
            state[...] = jnp.zeros_like(state)

        dt, acs = _expand_heads([dt_ref[...], acs_ref[...]], e3_ref[...])
        acs_c = acst_ref[...]
        st_ref[0] = state[...]
        last = acs[CHUNK - 1:CHUNK, :]
        xs32 = xc_ref[:, 0:D_INNER].astype(F32)
        xdt = xs32 * dt
        xdt16 = xdt.astype(BF16)
        xdte16 = (xdt * jnp.exp(last - acs)).astype(BF16)
        ea = jnp.exp(acs)
        cd = jnp.exp(last)
        skip = dk_ref[...] * xs32
        causal = (lax.broadcasted_iota(jnp.int32, (CHUNK, CHUNK), 0)
                  >= lax.broadcasted_iota(jnp.int32, (CHUNK, CHUNK), 1))
        left = lax.broadcasted_iota(jnp.int32, (CHUNK, PAIR), 1) < HEAD_DIM
        for g in range(N_GROUPS):
            gl = slice(g * GROUP_LANES, (g + 1) * GROUP_LANES)
            bg = xc_ref[:, D_INNER + g * D_STATE:D_INNER + (g + 1) * D_STATE]
            cg = xc_ref[:, D_INNER + (N_GROUPS + g) * D_STATE:D_INNER + (N_GROUPS + g + 1) * D_STATE]
            cb = lax.dot_general(cg, bg, _DIMS["nt"], preferred_element_type=F32)
            hprev = state[:, gl]
            ch = jnp.dot(cg, hprev.astype(BF16), preferred_element_type=F32)
            for j in range(HEADS_PER_GROUP // 2):
                pl_ = slice(g * GROUP_LANES + j * PAIR, g * GROUP_LANES + (j + 1) * PAIR)
                h0 = g * HEADS_PER_GROUP + 2 * j
                l0, l1 = _pair_decay(acs[:, pl_], acs_c, h0, causal, left)
                lhs = jnp.concatenate([(cb * l0).astype(BF16), (cb * l1).astype(BF16)], axis=1)
                xp = xdt16[:, pl_]
                zero = jnp.zeros_like(xp)
                rhs = jnp.concatenate([jnp.where(left, xp, zero), jnp.where(left, zero, xp)], axis=0)
                ydiag = jnp.dot(lhs, rhs, preferred_element_type=F32)
                y_ref[:, pl_] = ydiag + ch[:, j * PAIR:(j + 1) * PAIR] * ea[:, pl_] + skip[:, pl_]
            s_new = lax.dot_general(bg, xdte16[:, gl], _DIMS["tn"], preferred_element_type=F32)
            state[:, gl] = hprev * cd[:, gl] + s_new

    rows = lambda w: pl.BlockSpec((CHUNK, w), lambda c: (c, 0))
    return _pcall(
        body, name="ssd_fwd",
        out_shape=(jax.ShapeDtypeStruct((t_dim, D_INNER), F32),
                   jax.ShapeDtypeStruct((nc, D_STATE, D_INNER), F32)),
        grid=(nc,),
        in_specs=[rows(CONV_DIM), rows(128), rows(128), pl.BlockSpec((N_HEADS, CHUNK), lambda c: (0, c)),
                  _const_spec((1, D_INNER)), _const_spec((_EXACT_PIECES * 128, D_INNER))],
        out_specs=(rows(D_INNER), pl.BlockSpec((1, D_STATE, D_INNER), lambda c: (c, 0, 0))),
        scratch_shapes=[pltpu.VMEM((D_STATE, D_INNER), F32)],
        compiler_params=_params("arbitrary"),
    )(xc, dt_r, acs_r, acs_t, dskip_x, e3_mat)


def _ssd_bwd(xc, dt_r, acs_r, sg_r, acs_t, alog_r, dskip_x, e3_mat, et_mat, states, dy):
    t_dim = xc.shape[0]
    nc = t_dim // CHUNK

    def body(xc_ref, dt_ref, acs_ref, sg_ref, acst_ref, ar_ref, dk_ref, e3_ref, et_ref, st_ref, dy_ref,
             dxc_ref, ddt_ref, small_ref, dstate, dacs_ref, dxdt_ref, acc_x, acc_r):
        step = pl.program_id(0)

        @pl.when(step == 0)
        def _():
            dstate[...] = jnp.zeros_like(dstate)
            acc_x[...] = jnp.zeros_like(acc_x)
            acc_r[...] = jnp.zeros_like(acc_r)

        dt_r = dt_ref[...]
        a_r = -jnp.exp(ar_ref[...])
        dt, acs = _expand_heads([dt_r, acs_ref[...]], e3_ref[...])
        acs_c = acst_ref[...]
        last = acs[CHUNK - 1:CHUNK, :]
        xs32 = xc_ref[:, 0:D_INNER].astype(F32)
        xdt = xs32 * dt
        xdt16 = xdt.astype(BF16)
        dte = jnp.exp(last - acs)
        xdte = xdt * dte
        xdte16 = xdte.astype(BF16)
        cd = jnp.exp(last)
        dyv = dy_ref[...]
        dy16 = dyv.astype(BF16)
        dye = dyv * jnp.exp(acs)
        dye16 = dye.astype(BF16)
        causal = (lax.broadcasted_iota(jnp.int32, (CHUNK, CHUNK), 0)
                  >= lax.broadcasted_iota(jnp.int32, (CHUNK, CHUNK), 1))
        left = lax.broadcasted_iota(jnp.int32, (CHUNK, PAIR), 1) < HEAD_DIM
        lane_id = lax.broadcasted_iota(jnp.int32, (CHUNK, 128), 1)
        row_id = lax.broadcasted_iota(jnp.int32, (CHUNK, 128), 0)
        is_last_row = lax.broadcasted_iota(jnp.int32, (CHUNK, 1), 0) == CHUNK - 1
        dacs_cols = jnp.zeros((CHUNK, 128), F32)
        dacs_rows = jnp.zeros((CHUNK, 128), F32)
        for g in range(N_GROUPS):
            gl = slice(g * GROUP_LANES, (g + 1) * GROUP_LANES)
            b_lanes = slice(D_INNER + g * D_STATE, D_INNER + (g + 1) * D_STATE)
            c_lanes = slice(D_INNER + (N_GROUPS + g) * D_STATE, D_INNER + (N_GROUPS + g + 1) * D_STATE)
            bg = xc_ref[:, b_lanes]
            cg = xc_ref[:, c_lanes]
            cb = lax.dot_general(cg, bg, _DIMS["nt"], preferred_element_type=F32)
            hprev = st_ref[0, :, gl]
            hp16 = hprev.astype(BF16)
            dhn = dstate[:, gl]
            dhn16 = dhn.astype(BF16)
            ch = jnp.dot(cg, hp16, preferred_element_type=F32)
            gmat = jnp.dot(bg, dhn16, preferred_element_type=F32)
            gx = gmat * xdte[:, gl]
            dlast = jnp.sum(gx, axis=0, keepdims=True) + cd[:, gl] * jnp.sum(dhn * hprev, axis=0, keepdims=True)
            dacs_ref[:, gl] = dye[:, gl] * ch - gx + jnp.where(is_last_row, dlast, 0.0)
            dc_acc = lax.dot_general(dye16[:, gl], hp16, _DIMS["nt"], preferred_element_type=F32)
            db_acc = lax.dot_general(xdte16[:, gl], dhn16, _DIMS["nt"], preferred_element_type=F32)
            dstate[:, gl] = dhn * cd[:, gl] + lax.dot_general(cg, dye16[:, gl], _DIMS["tn"],
                                                             preferred_element_type=F32)
            dcb = jnp.zeros((CHUNK, CHUNK), F32)
            for j in range(HEADS_PER_GROUP // 2):
                pl_ = slice(g * GROUP_LANES + j * PAIR, g * GROUP_LANES + (j + 1) * PAIR)
                h0 = g * HEADS_PER_GROUP + 2 * j
                l0, l1 = _pair_decay(acs[:, pl_], acs_c, h0, causal, left)
                m0, m1 = cb * l0, cb * l1
                lhs = jnp.concatenate([m0.astype(BF16), m1.astype(BF16)], axis=1)
                dyp = dy16[:, pl_]
                zero = jnp.zeros_like(dyp)
                both = lax.dot_general(lhs, dyp, _DIMS["tn"], preferred_element_type=F32)
                dxdt_ref[:, pl_] = (jnp.where(left, both[0:CHUNK, :], both[CHUNK:, :])
                                    + gmat[:, j * PAIR:(j + 1) * PAIR] * dte[:, pl_])
                lhs2 = jnp.concatenate([jnp.where(left, dyp, zero), jnp.where(left, zero, dyp)], axis=0)
                dm = lax.dot_general(lhs2, xdt16[:, pl_], _DIMS["nt"], preferred_element_type=F32)
                dm0, dm1 = dm[0:CHUNK, :], dm[CHUNK:, :]
                dcb = dcb + dm0 * l0 + dm1 * l1
                ds0, ds1 = dm0 * m0, dm1 * m1
                dacs_cols = jnp.where(lane_id == h0, jnp.sum(ds0, axis=1, keepdims=True), dacs_cols)
                dacs_cols = jnp.where(lane_id == h0 + 1, jnp.sum(ds1, axis=1, keepdims=True), dacs_cols)
                dacs_rows = jnp.where(row_id == h0, jnp.sum(ds0, axis=0, keepdims=True), dacs_rows)
                dacs_rows = jnp.where(row_id == h0 + 1, jnp.sum(ds1, axis=0, keepdims=True), dacs_rows)
            dcb16 = dcb.astype(BF16)
            dxc_ref[:, c_lanes] = (dc_acc + jnp.dot(dcb16, bg, preferred_element_type=F32)).astype(BF16)
            dxc_ref[:, b_lanes] = (db_acc + lax.dot_general(dcb16, cg, _DIMS["tn"],
                                                           preferred_element_type=F32)).astype(BF16)
        dxdt = dxdt_ref[...]
        dxc_ref[:, 0:D_INNER] = (dxdt * dt + dk_ref[...] * dyv).astype(BF16)
        acc_x[0:1, :] += jnp.sum(dyv * xs32, axis=0, keepdims=True)
        et = et_ref[...]
        dacs = _reduce_heads(dacs_ref[...], et, 2) + dacs_cols - dacs_rows.T
        dadt = _rev_cumsum_rows(dacs)
        ddraw = (_reduce_heads(dxdt * xs32, et, 1) + dadt * a_r) * sg_ref[...]
        ddraw = jnp.where(lane_id < N_HEADS, ddraw, 0.0)
        ddt_ref[...] = ddraw
        acc_r[0:1, :] += jnp.where(lane_id[0:1, :] < N_HEADS,
                                   jnp.sum(dadt * dt_r, axis=0, keepdims=True) * a_r, 0.0)
        acc_r[1:2, :] += jnp.sum(ddraw, axis=0, keepdims=True)

        @pl.when(step == nc - 1)
        def _():
            dd = _reduce_heads(acc_x[...], et_ref[...], 3)
            rid = lax.broadcasted_iota(jnp.int32, (8, 128), 0)
            small_ref[...] = acc_r[...] + jnp.where(rid == 2, pltpu.roll(dd, 2, 0), 0.0)

    rev = lambda w: pl.BlockSpec((CHUNK, w), lambda c: (nc - 1 - c, 0))
    return _pcall(
        body, name="ssd_bwd",
        out_shape=(jax.ShapeDtypeStruct((t_dim, CONV_DIM), BF16),
                   jax.ShapeDtypeStruct((t_dim, 128), F32),
                   jax.ShapeDtypeStruct((8, 128), F32)),
        grid=(nc,),
        in_specs=[rev(CONV_DIM), rev(128), rev(128), rev(128),
                  pl.BlockSpec((N_HEADS, CHUNK), lambda c: (0, nc - 1 - c)),
                  _const_spec((1, 128)), _const_spec((1, D_INNER)),
                  _const_spec((_EXACT_PIECES * 128, D_INNER)), _const_spec((D_INNER, 128)),
                  pl.BlockSpec((1, D_STATE, D_INNER), lambda c: (nc - 1 - c, 0, 0)),
                  rev(D_INNER)],
        out_specs=(rev(CONV_DIM), rev(128), _const_spec((8, 128))),
        scratch_shapes=[pltpu.VMEM((D_STATE, D_INNER), F32), pltpu.VMEM((CHUNK, D_INNER), F32),
                        pltpu.VMEM((CHUNK, D_INNER), F32), pltpu.VMEM((8, D_INNER), F32),
                        pltpu.VMEM((8, 128), F32)],
        compiler_params=_params("arbitrary"),
    )(xc, dt_r, acs_r, sg_r, acs_t, alog_r, dskip_x, e3_mat, et_mat, states, dy)


_GATE_GROUP = D_INNER // N_GROUPS


def _gate_fwd(y, z, g, *, tb=256):
    t_dim = y.shape[0]

    def body(y_ref, z_ref, g_ref, o_ref):
        for gi in range(N_GROUPS):
            lanes = slice(gi * _GATE_GROUP, (gi + 1) * _GATE_GROUP)
            zv = z_ref[:, lanes].astype(F32)
            wv = y_ref[:, lanes] * (zv * _sigmoid(zv))
            o_ref[:, lanes] = (wv * _rms(wv) * g_ref[:, lanes]).astype(BF16)

    return _pcall(
        body, name="gate_fwd", out_shape=jax.ShapeDtypeStruct((t_dim, D_INNER), BF16),
        grid=(t_dim // tb,),
        in_specs=[_row_spec(tb, D_INNER), _row_spec(tb, D_INNER), _const_spec((1, D_INNER))],
        out_specs=_row_spec(tb, D_INNER), compiler_params=_params("parallel"),
    )(y, z, g)


def _gate_bwd(dyn, y, z, g, *, tb=256):
    t_dim = y.shape[0]

    def body(d_ref, y_ref, z_ref, g_ref, dy_ref, dz_ref, dg_ref):
        @pl.when(pl.program_id(0) == 0)
        def _():
            dg_ref[...] = jnp.zeros_like(dg_ref)

        for gi in range(N_GROUPS):
            lanes = slice(gi * _GATE_GROUP, (gi + 1) * _GATE_GROUP)
            zv = z_ref[:, lanes].astype(F32)
            sg = _sigmoid(zv)
            sz = zv * sg
            yv = y_ref[:, lanes]
            wv = yv * sz
            r = _rms(wv)
            what = wv * r
            dv = d_ref[:, lanes]
            dwhat = dv * g_ref[:, lanes]
            dw = r * (dwhat - what * jnp.mean(dwhat * what, axis=-1, keepdims=True))
            dg_ref[0:1, lanes] += jnp.sum(dv * what, axis=0, keepdims=True)
            dy_ref[:, lanes] = dw * sz
            dz_ref[:, lanes] = (dw * yv * (sg * (1.0 + zv * (1.0 - sg)))).astype(BF16)

    return _pcall(
        body, name="gate_bwd",
        out_shape=(jax.ShapeDtypeStruct((t_dim, D_INNER), F32),
                   jax.ShapeDtypeStruct((t_dim, D_INNER), BF16),
                   jax.ShapeDtypeStruct((8, D_INNER), F32)),
        grid=(t_dim // tb,),
        in_specs=[_row_spec(tb, D_INNER), _row_spec(tb, D_INNER), _row_spec(tb, D_INNER),
                  _const_spec((1, D_INNER))],
        out_specs=(_row_spec(tb, D_INNER), _row_spec(tb, D_INNER), _const_spec((8, D_INNER))),
        compiler_params=_params("arbitrary"),
    )(dyn, y, z, g)


_ADAM_C1 = 1.0 / (1.0 - ADAM_B1 ** ADAM_STEP)
_ADAM_C2 = 1.0 / (1.0 - ADAM_B2 ** ADAM_STEP)


def _adamw_math(w, g, m, v):
    mn = ADAM_B1 * m + (1.0 - ADAM_B1) * g
    vn = ADAM_B2 * v + (1.0 - ADAM_B2) * (g * g)
    delta = -ADAM_LR * ((mn * _ADAM_C1) / (jnp.sqrt(vn * _ADAM_C2) + ADAM_EPS) + ADAM_WD * w)
    return delta, mn, vn


def _adamw(w, g, m, v, *, name, part=None, into=None):
    r_dim, c = w.shape
    rows = r_dim if part is None else r_dim // 2
    assert g.shape == (rows, c)
    tb = max(t for t in range(8, 513, 8) if rows % t == 0)
    first = 0 if part is None else part * (rows // tb)
    n_out = 3 if part is None else 4

    def body(w_ref, g_ref, m_ref, v_ref, *rest):
        outs = rest[-n_out:]
        gv = g_ref[...]
        outs[0][...], outs[1][...], outs[2][...] = _adamw_math(w_ref[...], gv, m_ref[...], v_ref[...])
        if part is not None:
            outs[3][...] = gv

    spec = pl.BlockSpec((tb, c), lambda i: (first + i, 0))
    sds = jax.ShapeDtypeStruct((r_dim, c), F32)
    in_specs = [spec, _row_spec(tb, c), spec, spec]
    operands = [w, g, m, v]
    aliases = {}
    if into is not None:
        in_specs += [_ANY] * n_out
        operands += list(into)
        aliases = {4 + i: i for i in range(n_out)}
    outs = _pcall(
        body, name=name, out_shape=(sds,) * n_out, grid=(rows // tb,),
        in_specs=in_specs, out_specs=(spec,) * n_out, input_output_aliases=aliases,
        compiler_params=_params("parallel"),
    )(*operands)
    return tuple(outs) if part is not None else tuple(outs) + (g,)


def _adamw_small(params, *, name):
    n = len(params)

    def body(*refs):
        ins, outs = refs[:4 * n], refs[4 * n:]
        for i in range(n):
            w_ref, g_ref, m_ref, v_ref = ins[4 * i:4 * i + 4]
            res = _adamw_math(w_ref[...], g_ref[...], m_ref[...], v_ref[...])
            for o_ref, r in zip(outs[3 * i:3 * i + 3], res):
                o_ref[...] = r

    vmem = pl.BlockSpec(memory_space=pltpu.VMEM)
    flat = [a for p in params for a in p]
    outs = _pcall(
        body, name=name,
        out_shape=tuple(jax.ShapeDtypeStruct(p[0].shape, F32) for p in params for _ in range(3)),
        in_specs=[vmem] * (4 * n), out_specs=(vmem,) * (3 * n),
    )(*flat)
    return [tuple(outs[3 * i:3 * i + 3]) for i in range(n)]


def _pair_sum(grad, recv, place, *, name):
    s_dim, r_dim, c = grad.shape
    half = r_dim // 2
    tb = 256 if half % 256 == 0 else half
    per_half = half // tb

    def body(place_ref, a_ref, b_ref, o16_ref, o32_ref):
        s = a_ref[...] + b_ref[...]
        o16_ref[...] = s.astype(BF16)

        @pl.when(pl.program_id(1) == place_ref[1])
        def _():
            o32_ref[...] = s[0]

    grid_spec = pltpu.PrefetchScalarGridSpec(
        num_scalar_prefetch=1, grid=(per_half, s_dim),
        in_specs=[pl.BlockSpec((1, tb, c), lambda i, s, p: (s, p[0] * per_half + i, 0)),
                  pl.BlockSpec((1, tb, c), lambda i, s, p: (s, i, 0))],
        out_specs=(pl.BlockSpec((1, tb, c), lambda i, s, p: (s, i, 0)),
                   pl.BlockSpec((tb, c), lambda i, s, p: (i, 0))))
    return _pcall(
        body, name=name, grid_spec=grid_spec,
        out_shape=(jax.ShapeDtypeStruct((s_dim, half, c), BF16), jax.ShapeDtypeStruct((half, c), F32)),
        compiler_params=_params("parallel", "arbitrary"),
    )(place, grad, recv)


def _chip_sum(own, recv, place, *, name):
    r_dim, c = own.shape
    tb = 256 if r_dim % 256 == 0 else r_dim

    def body(place_ref, a_ref, b_ref, o_ref):
        s = a_ref[...]
        for k in range(1, N_CHIPS):
            s = s + b_ref[k].astype(F32)
        o_ref[...] = s

    grid_spec = pltpu.PrefetchScalarGridSpec(
        num_scalar_prefetch=1, grid=(r_dim // tb,),
        in_specs=[pl.BlockSpec((tb, c), lambda i, p: (i, 0)),
                  pl.BlockSpec((N_CHIPS, tb, c), lambda i, p: (0, i, 0))],
        out_specs=pl.BlockSpec((None, tb, c), lambda i, p: (p[0], i, 0)))
    return _pcall(
        body, name=name, grid_spec=grid_spec, out_shape=jax.ShapeDtypeStruct((2, r_dim, c), F32),
        compiler_params=_params("parallel"),
    )(place, own, recv)


def _position():
    return lax.axis_index("x"), lax.axis_index("y"), lax.axis_index("c")


def _chip_peer(x, y, k):
    return x ^ (k >> 1), y ^ (k & 1)


_ANY = pl.BlockSpec(memory_space=pl.ANY)
_TOKEN = jax.ShapeDtypeStruct((8, 128), F32)


def _all_gather_weights(shards):
    n = len(shards)
    hops = N_CHIPS - 1

    def body(*refs):
        srcs, outs, done = refs[:n], refs[n:2 * n], refs[2 * n]
        send_sems, recv_sems = refs[2 * n + 1:]
        x, y, c = _position()
        me = 2 * x + y
        done[...] = jnp.zeros_like(done)

        def over_ici(w, k, chip, to):
            return pltpu.make_async_remote_copy(
                src_ref=srcs[w].at[c], dst_ref=outs[w].at[chip, c],
                send_sem=send_sems.at[w, k - 1], recv_sem=recv_sems.at[w, k - 1],
                device_id=to, device_id_type=MESH)

        def over_d2d(w, k, chip, half):
            return pltpu.make_async_remote_copy(
                src_ref=outs[w].at[chip, half], dst_ref=outs[w].at[chip, half],
                send_sem=send_sems.at[w, hops + k - 1], recv_sem=recv_sems.at[w, hops + k - 1],
                device_id=(x, y, 1 - c), device_id_type=MESH)

        sends = []
        for w in range(n):
            for k in range(1, N_CHIPS):
                px, py = _chip_peer(x, y, k)
                cp = over_ici(w, k, me, (px, py, c))
                cp.start()
                sends.append(cp)
        for w in range(n):
            for k in range(1, N_CHIPS):
                px, py = _chip_peer(x, y, k)
                over_ici(w, k, 2 * px + py, (px, py, c)).wait_recv()
                cp = over_d2d(w, k, 2 * px + py, c)
                cp.start()
                sends.append(cp)
        for w in range(n):
            for k in range(1, N_CHIPS):
                px, py = _chip_peer(x, y, k)
                over_d2d(w, k, 2 * px + py, 1 - c).wait_recv()
        for cp in sends:
            cp.wait_send()

    outs = _pcall(
        body, name="gather_weights",
        out_shape=tuple(jax.ShapeDtypeStruct((N_CHIPS,) + s.shape, s.dtype) for s in shards) + (_TOKEN,),
        in_specs=[_ANY] * n, out_specs=(_ANY,) * n + (pl.BlockSpec(memory_space=pltpu.VMEM),),
        scratch_shapes=[pltpu.SemaphoreType.DMA((n, 2 * hops)),
                        pltpu.SemaphoreType.DMA((n, 2 * hops))],
    )(*shards)
    return outs[:n], outs[n][0, 0]


def _pair_copies(srcs, lands, send_sems, recv_sems):
    x, y, c = _position()
    copies = []
    for w in range(len(srcs)):
        half = srcs[w].shape[1] // 2
        copies.append(pltpu.make_async_remote_copy(
            src_ref=srcs[w].at[:, pl.ds((1 - c) * half, half), :], dst_ref=lands[w],
            send_sem=send_sems.at[w], recv_sem=recv_sems.at[w],
            device_id=(x, y, 1 - c), device_id_type=MESH))
    return copies


def _chip_copies(srcs, lands, send_sems, recv_sems):
    x, y, c = _position()
    copies = []
    for w in range(len(srcs)):
        for k in range(1, N_CHIPS):
            px, py = _chip_peer(x, y, k)
            i = w * (N_CHIPS - 1) + k - 1
            copies.append(pltpu.make_async_remote_copy(
                src_ref=srcs[w].at[2 * px + py], dst_ref=lands[w].at[k],
                send_sem=send_sems.at[i], recv_sem=recv_sems.at[i],
                device_id=(px, py, c), device_id_type=MESH))
    return copies


def _gather_copies(srcs, lands, send_sems, recv_sems):
    x, y, c = _position()
    me = 2 * x + y
    copies = []
    for w in range(len(srcs)):
        for k in range(1, N_CHIPS):
            px, py = _chip_peer(x, y, k)
            i = w * (N_CHIPS - 1) + k - 1
            copies.append(pltpu.make_async_remote_copy(
                src_ref=srcs[w].at[c], dst_ref=lands[w].at[me, c],
                send_sem=send_sems.at[i], recv_sem=recv_sems.at[i],
                device_id=(px, py, c), device_id_type=MESH))
    return copies


def _exchange(name, copies_of, n_copies, srcs, land_shapes):
    n = len(srcs)

    def body(*refs):
        copies = copies_of(refs[:n], refs[n:2 * n], refs[2 * n], refs[2 * n + 1])
        for cp in copies:
            cp.start()
        for cp in copies:
            cp.wait_recv()
        for cp in copies:
            cp.wait_send()

    return _pcall(
        body, name=name, out_shape=tuple(land_shapes),
        in_specs=[_ANY] * n, out_specs=(_ANY,) * n,
        scratch_shapes=[pltpu.SemaphoreType.DMA((n_copies,)), pltpu.SemaphoreType.DMA((n_copies,))],
    )(*srcs)


_HBM = pl.BlockSpec(memory_space=pltpu.HBM)
_SEM = pl.BlockSpec(memory_space=pltpu.SEMAPHORE)
_DATAFLOW = pltpu.SideEffectType.DATAFLOW_SIDE_EFFECTING


def _exchange_start(name, copies_of, n_copies, srcs, land_shapes):
    n = len(srcs)
    lands = [lax.empty(s.shape, s.dtype) for s in land_shapes]

    def body(*refs):
        for cp in copies_of(refs[:n], refs[n:2 * n], refs[2 * n], refs[2 * n + 1]):
            cp.start()
        refs[-1][...] = jnp.zeros_like(refs[-1])

    through = [pltpu.HBM(a.shape, a.dtype) for a in list(srcs) + lands]
    outs = _pcall(
        body, name=name,
        out_shape=(pltpu.SemaphoreType.DMA((n_copies,)), pltpu.SemaphoreType.DMA((n_copies,)),
                   *through, jax.ShapeDtypeStruct((8, 128), F32)),
        in_specs=[_HBM] * (2 * n),
        out_specs=(_SEM, _SEM, *([_HBM] * (2 * n)), pl.BlockSpec(memory_space=pltpu.VMEM)),
        input_output_aliases={i: 2 + i for i in range(2 * n)},
        compiler_params=pltpu.CompilerParams(has_side_effects=_DATAFLOW),
    )(*[pltpu.with_memory_space_constraint(a, pltpu.HBM) for a in list(srcs) + lands])
    return outs[:-1], outs[-1][0, 0]


def _exchange_wait(name, copies_of, state, after):
    send_sems, recv_sems, through = state[0], state[1], state[2:]
    n = len(through) // 2
    if after.ndim == 0:
        after = jnp.broadcast_to(after, (8, 128))
    after = pltpu.with_memory_space_constraint(after, pltpu.HBM)

    def body(*refs):
        for cp in copies_of(refs[:n], refs[n:2 * n], refs[2 * n], refs[2 * n + 1]):
            cp.wait_send()
            cp.wait_recv()

    outs = _pcall(
        body, name=name,
        out_shape=tuple(pltpu.HBM(a.shape, a.dtype) for a in through),
        in_specs=[_HBM] * (2 * n) + [_SEM, _SEM, _HBM], out_specs=tuple([_HBM] * (2 * n)),
        input_output_aliases={i: i for i in range(2 * n)},
        compiler_params=pltpu.CompilerParams(has_side_effects=_DATAFLOW),
    )(*through, send_sems, recv_sems, after)
    return outs[:n], outs[n:]


def _forward_halves(lands, *, name):
    n = len(lands)
    hops = N_CHIPS - 1

    def body(*refs):
        ins, outs, done = refs[:n], refs[n:2 * n], refs[2 * n]
        send_sems, recv_sems = refs[2 * n + 1], refs[2 * n + 2]
        x, y, c = _position()
        done[...] = jnp.zeros_like(done)
        copies = []
        for w in range(n):
            for k in range(1, N_CHIPS):
                px, py = _chip_peer(x, y, k)
                i = w * hops + k - 1
                copies.append(pltpu.make_async_remote_copy(
                    src_ref=ins[w].at[2 * px + py, c], dst_ref=outs[w].at[2 * px + py, c],
                    send_sem=send_sems.at[i], recv_sem=recv_sems.at[i],
                    device_id=(x, y, 1 - c), device_id_type=MESH))
        for cp in copies:
            cp.start()
        for cp in copies:
            cp.wait_recv()
        for cp in copies:
            cp.wait_send()

    outs = _pcall(
        body, name=name,
        out_shape=tuple(jax.ShapeDtypeStruct(a.shape, a.dtype) for a in lands) + (_TOKEN,),
        in_specs=[_ANY] * n, out_specs=(_ANY,) * n + (pl.BlockSpec(memory_space=pltpu.VMEM),),
        input_output_aliases={i: i for i in range(n)},
        scratch_shapes=[pltpu.SemaphoreType.DMA((n * hops,)), pltpu.SemaphoreType.DMA((n * hops,))],
    )(*lands)
    return outs[:n], outs[n][0, 0]


def _pair_lands(grads):
    return [jax.ShapeDtypeStruct((g.shape[0], g.shape[1] // 2, g.shape[2]), F32) for g in grads]


def _same_lands(parts):
    return [jax.ShapeDtypeStruct(p.shape, p.dtype) for p in parts]


def _pair_gather_halves(halves, *, name):
    n = len(halves)

    def body(*refs):
        ins, outs = refs[:n], refs[n:2 * n]
        send_sems, recv_sems = refs[2 * n:]
        x, y, c = _position()
        sends = []
        for w in range(n):
            cp = pltpu.make_async_remote_copy(
                src_ref=ins[w].at[c], dst_ref=outs[w].at[c],
                send_sem=send_sems.at[w], recv_sem=recv_sems.at[w],
                device_id=(x, y, 1 - c), device_id_type=MESH)
            cp.start()
            sends.append(cp)
        for cp in sends:
            cp.wait_recv()
        for cp in sends:
            cp.wait_send()

    whole = _pcall(
        body, name=name,
        out_shape=tuple(jax.ShapeDtypeStruct(h.shape, F32) for h in halves),
        in_specs=[_ANY] * n, out_specs=(_ANY,) * n,
        input_output_aliases={i: i for i in range(n)},
        scratch_shapes=[pltpu.SemaphoreType.DMA((n,)), pltpu.SemaphoreType.DMA((n,))],
    )(*halves)
    return [w.reshape(2 * w.shape[1], w.shape[2]) for w in whole]


def _all_reduce_small(packed, *, name, sum_row0):
    r_dim, c = packed.shape

    def body(src_ref, out_ref, recv_ref, send_sems, recv_sems):
        x, y, c_ = _position()
        me = 4 * x + 2 * y + c_
        recv_ref[0] = src_ref[...]
        sends = []
        for k in range(1, N_DEV):
            peer = (x ^ (k >> 2), y ^ ((k >> 1) & 1), c_ ^ (k & 1))
            cp = pltpu.make_async_remote_copy(
                src_ref=src_ref, dst_ref=recv_ref.at[k],
                send_sem=send_sems.at[k - 1], recv_sem=recv_sems.at[k - 1],
                device_id=peer, device_id_type=MESH)
            cp.start()
            sends.append(cp)
        for cp in sends:
            cp.wait_recv()
        total = recv_ref[me]
        for d in range(1, N_DEV):
            total = total + recv_ref[d ^ me]
        if sum_row0:
            row0 = jnp.sum(total[0:1, :], axis=1, keepdims=True)
            rid = lax.broadcasted_iota(jnp.int32, total.shape, 0)
            total = jnp.where(rid == 0, row0, total)
        out_ref[...] = total
        for cp in sends:
            cp.wait_send()

    return _pcall(
        body, name=name, out_shape=jax.ShapeDtypeStruct((r_dim, c), F32),
        in_specs=[pl.BlockSpec(memory_space=pltpu.VMEM)],
        out_specs=pl.BlockSpec(memory_space=pltpu.VMEM),
        scratch_shapes=[pltpu.VMEM((N_DEV, r_dim, c), F32),
                        pltpu.SemaphoreType.DMA((N_DEV - 1,)), pltpu.SemaphoreType.DMA((N_DEV - 1,))],
    )(packed)


def _pad_lanes(v, width):
    return jnp.pad(v, ((0, 0), (0, width - v.shape[1])))


def _pad_rows(v, rows):
    pad = [(0, 0)] * v.ndim
    pad[-2] = (0, rows - v.shape[-2])
    return jnp.pad(v, pad)


_IN_PROJ_SHARD_ROWS = 1312


def _rows_1024(v):
    flat = v.reshape(-1)
    pad = (-flat.shape[0]) % D_MODEL
    return jnp.pad(flat, (0, pad)).reshape(-1, D_MODEL)


def _local_step(xs, target, pw, fetch, reduce_start, reduce_midway,
                conv_w, conv_b, gate_g,
                norm_mix_g, norm_mlp_g, pool_b, pool_scale, ssm_dt_bias, ssm_a_log, ssm_d, final_g):
    bias_r = _pad_lanes(ssm_dt_bias, 128)
    alog_r = _pad_lanes(ssm_a_log, 128)
    dskip_x = jnp.repeat(ssm_d, HEAD_DIM, axis=1)
    bias_c = ssm_dt_bias.reshape(N_HEADS, 1)
    alog_c = ssm_a_log.reshape(N_HEADS, 1)
    e_mat = _head_lane_matrix()
    e3_mat = jnp.tile(e_mat, (_EXACT_PIECES, 1))

    g_mix0, g_mix1 = norm_mix_g[0:1], norm_mix_g[1:2]
    g_mlp0, g_mlp1 = norm_mlp_g[0:1], norm_mlp_g[1:2]
    fg = final_g.reshape(1, D_MODEL)

    h1 = _pool_fwd(xs, g_mix0, pw, pool_b, pool_scale)
    w1_0 = fetch("mlp0_up", h1)
    u0, hm0 = _matmul(h1, w1_0, "nn", name="mlp0_up", out_dtype=BF16, b_col_shards=True, a_norm=g_mlp0)
    w2_0 = fetch("mlp0_down", u0)
    h2 = _matmul(u0, w2_0, "nn", name="mlp0_down", a_relu2=True, add=h1)

    w_z, w_xbc, w_dt = fetch("in_proj", h2)
    xbc, hn1 = _matmul(h2, w_xbc, "nt", name="in_proj_xbc", out_dtype=BF16, a_norm=g_mix1)
    z = _matmul(hn1, w_z, "nt", name="in_proj_z", out_dtype=BF16)
    dt_raw = _matmul(hn1, w_dt, "nt", name="in_proj_dt")
    dt_raw_t = dt_raw[:, :N_HEADS].T
    xc = _conv_fwd(xbc, conv_w, conv_b)
    wout, w1_1, w2_1 = fetch("rest", xc)
    dt_r, acs_r, sg_r, acs_t = _ssd_decay(dt_raw, dt_raw_t, bias_r, bias_c, alog_r, alog_c)
    y, states = _ssd_fwd(xc, dt_r, acs_r, acs_t, dskip_x, e3_mat)
    yn = _gate_fwd(y, z, gate_g)
    h3 = _matmul(yn, wout, "nn", name="out_proj", add=h2)
    u1, hm1 = _matmul(h3, w1_1, "nn", name="mlp1_up", out_dtype=BF16, b_col_shards=True, a_norm=g_mlp1)

    dh4, dh4_16, small_final = _matmul(u1, w2_1, "nn", name="mlp1_down", a_relu2=True, add=h3,
                                       loss_head=(fg, target))

    def mlp_bwd_weights(dh_out16, hm, u, w2_i, tag):
        du = _matmul(dh_out16, w2_i, "nt", name=tag + "_du", out_dtype=BF16, relu2_grad_of=u)
        dw2 = _matmul(u, dh_out16, "tn", name=tag + "_dw2", a_relu2=True)
        dw1 = _matmul(hm, du, "tn", name=tag + "_dw1", out_col_shards=N_CHIPS)
        return du, dw1, dw2.reshape(N_CHIPS, D_FF // N_CHIPS, D_MODEL)

    def mlp_bwd_input(du, dh_out, h_in, w1_i, g_i, tag):
        return _matmul(du, w1_i, "nt", name=tag + "_dhm", b_col_shards=True, norm_bwd=(h_in, g_i, dh_out))

    du1, dw1_1, dw2_1 = mlp_bwd_weights(dh4_16, hm1, u1, w2_1, "mlp1")
    dh3, dh3_16, dg_mlp1 = mlp_bwd_input(du1, dh4, h3, w1_1, g_mlp1, "mlp1")

    dyn = _matmul(dh3_16, wout, "nt", name="out_proj_dyn")
    dwout = _matmul(yn, dh3_16, "tn", name="out_proj_dw").reshape(N_CHIPS, D_INNER // N_CHIPS, D_MODEL)
    behind = reduce_start("mlp1_out", [dw1_1, dw2_1, dwout])
    dy, dz, dg_gate = _gate_bwd(dyn, y, z, gate_g + behind)
    behind = reduce_midway("mlp1_out", dz)
    dxc, ddt_raw, small_ssd = _ssd_bwd(xc, dt_r, acs_r, sg_r, acs_t, alog_r, dskip_x + behind,
                                       e3_mat, e_mat.T, states, dy)
    dv, dconv = _conv_bwd_act(xbc, dxc, conv_w, conv_b)
    dxbc = _conv_bwd_in(dv, conv_w)
    dw_z = _matmul(dz, hn1, "tn", name="in_proj_z_dw")
    dw_xbc = _matmul(dxbc, hn1, "tn", name="in_proj_xbc_dw")
    dw_dt = _matmul(ddt_raw, hn1, "tn", name="in_proj_dt_dw")
    dwin = jnp.concatenate([dw_z, dw_xbc, dw_dt[:N_HEADS]], axis=0)
    dwin = _pad_rows(dwin.reshape(N_CHIPS, IN_PROJ_DIM // N_CHIPS, D_MODEL), _IN_PROJ_SHARD_ROWS)
    behind = reduce_start("in_proj", [dwin])
    dhn1 = _matmul(ddt_raw, w_dt, "nn", name="in_proj_dt_dh")
    dhn1 = _matmul(dz, w_z, "nn", name="in_proj_z_dh", add=dhn1)
    dh2, dh2_16, dg_mix1 = _matmul(dxbc, w_xbc, "nn", name="in_proj_xbc_dh", add=dhn1,
                                   norm_bwd=(h2, g_mix1 + behind, dh3))
    behind = reduce_midway("in_proj", dh2_16)

    du0, dw1_0, dw2_0 = mlp_bwd_weights(dh2_16, hm0, u0, w2_0, "mlp0")
    dh1, _, dg_mlp0 = mlp_bwd_input(du0, dh2, h1, w1_0, g_mlp0 + behind, "mlp0")
    dx, dpw, small_pool = _pool_bwd(xs, g_mix0, pw, pool_b, pool_scale, dh1)
    dpw = jnp.transpose(dpw.reshape(4, N_CHIPS, POOL_GROUP // N_CHIPS, POOL_GROUP), (1, 0, 2, 3))
    dpw = dpw.reshape(N_CHIPS, 4 * (POOL_GROUP // N_CHIPS), POOL_GROUP)

    big = [dpw, dw1_0, dw2_0]
    rows = [
        small_final[1:2],
        small_final[0:1],
        small_pool[0:1], dg_mix1[0:1],
        dg_mlp0[0:1], dg_mlp1[0:1],
        small_pool[1:2], small_pool[2:3],
        _pad_lanes(small_ssd[0:3], D_MODEL),
        _rows_1024(dg_gate[0:1]),
        _rows_1024(dconv[0:CONV_K]),
        _rows_1024(dconv[CONV_K:CONV_K + 1]),
    ]
    return dx, big, rows


def kernel(x, norm_mix_g, norm_mlp_g, pool_w, pool_b, pool_scale, ssm_w_in, ssm_conv_w, ssm_conv_b, ssm_dt_bias, ssm_a_log, ssm_d, ssm_norm_g, ssm_w_out, mlp_w1, mlp_w2, final_g, loss_target, m_norm_mix_g, m_norm_mlp_g, m_pool_w, m_pool_b, m_pool_scale, m_ssm_w_in, m_ssm_conv_w, m_ssm_conv_b, m_ssm_dt_bias, m_ssm_a_log, m_ssm_d, m_ssm_norm_g, m_ssm_w_out, m_mlp_w1, m_mlp_w2, m_final_g, v_norm_mix_g, v_norm_mlp_g, v_pool_w, v_pool_b, v_pool_scale, v_ssm_w_in, v_ssm_conv_w, v_ssm_conv_b, v_ssm_dt_bias, v_ssm_a_log, v_ssm_d, v_ssm_norm_g, v_ssm_w_out, v_mlp_w1, v_mlp_w2, v_final_g):
    t_dim = x.shape[1]
    xs = x[0]
    target = loss_target[0]
    my_x, my_y, my_c = _position()
    my_chip = 2 * my_x + my_y

    def halves(w):
        return w.astype(BF16).reshape((2, w.shape[0] // 2) + w.shape[1:])

    def whole(gathered, own_shard):
        g = lax.dynamic_update_index_in_dim(gathered, own_shard, my_chip, axis=0)
        return g.reshape((N_CHIPS, 2 * g.shape[2]) + g.shape[3:])

    def gather_lands(own):
        return [jax.ShapeDtypeStruct((N_CHIPS,) + s.shape, s.dtype) for s in own]

    vec_cols = CONV_DIM // N_CHIPS
    vec_own = jnp.concatenate([ssm_conv_w[0], ssm_conv_b, _pad_lanes(ssm_norm_g, vec_cols)], axis=0)
    early_own = [halves(pool_w[0]), vec_own.reshape(2, (CONV_K + 2) // 2, vec_cols)]
    early, behind_early = _all_gather_weights(early_own)
    g_pool, g_vec = [whole(g, o) for g, o in zip(early, early_own)]
    pw = jnp.transpose(g_pool, (1, 0, 2, 3)).reshape(4, POOL_GROUP, POOL_GROUP)
    conv_w = jnp.transpose(g_vec[:, 0:CONV_K, :], (1, 0, 2)).reshape(CONV_K, CONV_DIM)
    conv_b = g_vec[:, CONV_K, :].reshape(1, CONV_DIM)
    gate_g = g_vec[:, CONV_K + 1, :D_INNER // N_CHIPS].reshape(1, D_INNER)

    def behind_it(zero, ws):
        return [halves(w + zero) for w in ws]

    fetches = {}
    up_own = behind_it(behind_early, [mlp_w1[0]])
    fetches["mlp0_up"], behind_gather = _exchange_start(
        "gather_mlp0_up_start", _gather_copies, len(up_own) * (N_CHIPS - 1), up_own, gather_lands(up_own))
    down_own = behind_it(behind_gather, [mlp_w2[0]])
    fetches["mlp0_down"], behind_gather = _exchange_start(
        "gather_mlp0_down_start", _gather_copies, len(down_own) * (N_CHIPS - 1), down_own, gather_lands(down_own))
    in_own = behind_it(behind_gather, [_pad_rows(ssm_w_in[0].T, _IN_PROJ_SHARD_ROWS)])
    fetches["in_proj"], behind_gather = _exchange_start(
        "gather_in_proj_start", _gather_copies, len(in_own) * (N_CHIPS - 1), in_own, gather_lands(in_own))

    def fetch(what, after):
        if what == "mlp0_up":
            own_thru, landed = _exchange_wait("gather_mlp0_up_wait", _gather_copies, fetches[what], after)
            landed, _ = _forward_halves(landed, name="forward_mlp0_up")
            return whole(landed[0], own_thru[0])
        if what == "mlp0_down":
            own_thru, landed = _exchange_wait("gather_mlp0_down_wait", _gather_copies, fetches[what], after)
            landed, _ = _forward_halves(landed, name="forward_mlp0_down")
            return whole(landed[0], own_thru[0]).reshape(D_FF, D_MODEL)
        if what == "in_proj":
            own_thru, landed = _exchange_wait("gather_in_proj_wait", _gather_copies, fetches["in_proj"], after)
            landed, behind = _forward_halves(landed, name="forward_in_proj")
            rest = behind_it(behind, [ssm_w_out[0], mlp_w1[1], mlp_w2[1]])
            fetches["rest"], behind = _exchange_start(
                "gather_rest_start", _gather_copies, len(rest) * (N_CHIPS - 1), rest, gather_lands(rest))
            win = whole(landed[0], own_thru[0])[:, :IN_PROJ_DIM // N_CHIPS].reshape(IN_PROJ_DIM, D_MODEL)
            w_dt = _pad_rows(win[D_INNER + CONV_DIM:], 128) + behind.astype(BF16)
            return win[:D_INNER], win[D_INNER:D_INNER + CONV_DIM], w_dt
        own_thru, landed = _exchange_wait("gather_rest_wait", _gather_copies, fetches["rest"], after)
        landed, _ = _forward_halves(landed, name="forward_rest")
        g_wout, w1_1, g_w2_1 = [whole(g, o) for g, o in zip(landed, own_thru)]
        return g_wout.reshape(D_INNER, D_MODEL), w1_1, g_w2_1.reshape(D_FF, D_MODEL)

    place = jnp.stack([my_c, my_chip]).astype(jnp.int32)
    waves = {}

    def reduce_start(wave, grads):
        waves[wave] = {}
        waves[wave]["pair"], behind = _exchange_start(
            "pair_%s_start" % wave, _pair_copies, len(grads), grads, _pair_lands(grads))
        return behind

    def reduce_midway(wave, after):
        st = waves[wave]
        grads, recv = _exchange_wait("pair_%s_wait" % wave, _pair_copies, st["pair"], after)
        sums = [_pair_sum(g, r, place, name="pair_sum_%s_%d" % (wave, i))
                for i, (g, r) in enumerate(zip(grads, recv))]
        st["f32"] = [s32 for _, s32 in sums]
        b16 = [s16 for s16, _ in sums]
        st["chip"], behind = _exchange_start(
            "chip_%s_start" % wave, _chip_copies, len(b16) * (N_CHIPS - 1), b16, _same_lands(b16))
        return behind

    def reduce_finish(wave, after):
        st = waves[wave]
        _, got = _exchange_wait("chip_%s_wait" % wave, _chip_copies, st["chip"], after)
        return [_chip_sum(s32, r, place, name="chip_sum_%s_%d" % (wave, i))
                for i, (s32, r) in enumerate(zip(st["f32"], got))]

    dx, big0, rows = _local_step(xs, target, pw, fetch, reduce_start, reduce_midway,
                                 conv_w, conv_b, gate_g,
                                 norm_mix_g + behind_gather, norm_mlp_g, pool_b, pool_scale,
                                 ssm_dt_bias, ssm_a_log, ssm_d, final_g)

    behind = reduce_start("layer0", big0)
    small = jnp.concatenate(rows, axis=0)
    small = jnp.pad(small, ((0, (-small.shape[0]) % 8), (0, 0))) + behind
    small = _all_reduce_small(small, name="all_reduce_small", sum_row0=True)
    behind = reduce_midway("layer0", small)
    h_w1_1, h_w2_1, h_wout = reduce_finish("mlp1_out", behind)
    (h_win,) = reduce_finish("in_proj", behind)
    g_w1_1, g_w2_1, g_wout_s, g_win_s = _pair_gather_halves([h_w1_1, h_w2_1, h_wout, h_win],
                                                            name="pair_gather_layer1")
    loss = small[0, 0]
    g_final = small[1]
    g_norm_mix = small[2:4]
    g_norm_mlp = small[4:6]
    g_pool_b, g_pool_scale = small[6:7], small[7:8]
    g_alog, g_dtb, g_dsk = small[8:9, :N_HEADS], small[9:10, :N_HEADS], small[10:11, :N_HEADS]
    g_gate_full = small[11:13].reshape(1, D_INNER)
    g_convw_full = small[13:25].reshape(CONV_K, CONV_DIM)
    g_convb_full = small[25:28].reshape(1, CONV_DIM)
    g_gate = lax.dynamic_slice_in_dim(g_gate_full, my_chip * (D_INNER // N_CHIPS), D_INNER // N_CHIPS, axis=1)
    g_convw = lax.dynamic_slice_in_dim(g_convw_full, my_chip * (CONV_DIM // N_CHIPS), CONV_DIM // N_CHIPS, axis=1)
    g_convb = lax.dynamic_slice_in_dim(g_convb_full, my_chip * (CONV_DIM // N_CHIPS), CONV_DIM // N_CHIPS, axis=1)

    grads = {
        "norm_mix_g": g_norm_mix, "norm_mlp_g": g_norm_mlp,
        "pool_b": g_pool_b, "pool_scale": g_pool_scale,
        "ssm_conv_w": g_convw.reshape(ssm_conv_w.shape),
        "ssm_conv_b": g_convb, "ssm_dt_bias": g_dtb, "ssm_a_log": g_alog, "ssm_d": g_dsk,
        "ssm_norm_g": g_gate, "ssm_w_out": g_wout_s.reshape(ssm_w_out.shape),
        "final_g": g_final,
    }
    weights = dict(norm_mix_g=norm_mix_g, norm_mlp_g=norm_mlp_g, pool_w=pool_w, pool_b=pool_b,
                   pool_scale=pool_scale, ssm_w_in=ssm_w_in, ssm_conv_w=ssm_conv_w, ssm_conv_b=ssm_conv_b,
                   ssm_dt_bias=ssm_dt_bias, ssm_a_log=ssm_a_log, ssm_d=ssm_d, ssm_norm_g=ssm_norm_g,
                   ssm_w_out=ssm_w_out, mlp_w1=mlp_w1, mlp_w2=mlp_w2, final_g=final_g)
    moms = dict(norm_mix_g=(m_norm_mix_g, v_norm_mix_g), norm_mlp_g=(m_norm_mlp_g, v_norm_mlp_g),
                pool_w=(m_pool_w, v_pool_w), pool_b=(m_pool_b, v_pool_b),
                pool_scale=(m_pool_scale, v_pool_scale), ssm_w_in=(m_ssm_w_in, v_ssm_w_in),
                ssm_conv_w=(m_ssm_conv_w, v_ssm_conv_w), ssm_conv_b=(m_ssm_conv_b, v_ssm_conv_b),
                ssm_dt_bias=(m_ssm_dt_bias, v_ssm_dt_bias), ssm_a_log=(m_ssm_a_log, v_ssm_a_log),
                ssm_d=(m_ssm_d, v_ssm_d), ssm_norm_g=(m_ssm_norm_g, v_ssm_norm_g),
                ssm_w_out=(m_ssm_w_out, v_ssm_w_out), mlp_w1=(m_mlp_w1, v_mlp_w1),
                mlp_w2=(m_mlp_w2, v_mlp_w2), final_g=(m_final_g, v_final_g))
    names = list(weights)
    big_names = ("pool_w", "ssm_w_in", "ssm_w_out", "mlp_w1", "mlp_w2")
    deltas, new_m, new_v = {}, {}, {}

    def as_rows(nm, a):
        return a[0].T if nm == "ssm_w_in" else a.reshape(-1, a.shape[-1])

    def from_rows(nm, r):
        return r.T[None] if nm == "ssm_w_in" else r.reshape(weights[nm].shape)

    def update(nm, grad_rows, layer=None, into=None):
        return _adamw(as_rows(nm, weights[nm]), grad_rows, as_rows(nm, moms[nm][0]), as_rows(nm, moms[nm][1]),
                      name="adamw_%s_%s" % (nm, layer), part=layer, into=into)

    def keep(nm, results):
        deltas[nm], new_m[nm], new_v[nm], grads[nm] = [from_rows(nm, r) for r in results]

    keep("ssm_w_in", update("ssm_w_in", g_win_s[:IN_PROJ_DIM // N_CHIPS]))
    keep("ssm_w_out", update("ssm_w_out", g_wout_s))
    w1_done = update("mlp_w1", g_w1_1, layer=1)
    w2_done = update("mlp_w2", g_w2_1, layer=1)
    small_names = [nm for nm in names if nm not in big_names]
    small_done = _adamw_small(
        [tuple(as_rows(nm, a) for a in (weights[nm], grads[nm], moms[nm][0], moms[nm][1])) for nm in small_names],
        name="adamw_small")
    for nm, (d_, m_, v_) in zip(small_names, small_done):
        deltas[nm], new_m[nm], new_v[nm] = [from_rows(nm, r) for r in (d_, m_, v_)]

    above = (deltas["ssm_w_in"][0, 0, 0] + deltas["ssm_w_out"][0, 0, 0] + w1_done[0][-1, -1]
             + w2_done[0][-1, -1] + small_done[0][0][0, 0])
    g_pool_w, g_w1_0, g_w2_0 = _pair_gather_halves(reduce_finish("layer0", above), name="pair_gather_layer0")
    keep("mlp_w1", update("mlp_w1", g_w1_0, layer=0, into=w1_done))
    keep("mlp_w2", update("mlp_w2", g_w2_0, layer=0, into=w2_done))
    keep("pool_w", update("pool_w", g_pool_w))

    grad_x = dx.reshape(x.shape)
    out_grads = [grads[nm].reshape(weights[nm].shape) for nm in names]
    return (loss, grad_x, *out_grads, *[deltas[nm] for nm in names],
            *[new_m[nm] for nm in names], *[new_v[nm] for nm in names])
```

```python
import functools

import jax
import jax.numpy as jnp
from jax import lax
from jax.experimental import pallas as pl
from jax.experimental.pallas import tpu as pltpu

F32 = jnp.float32
BF16 = jnp.bfloat16
MESH = pl.DeviceIdType.MESH

D_MODEL = 1024
RMS_EPS = 1e-5
POOL_WINDOWS = (2, 4, 8, 16)
POOL_GROUP = 256
POOL_HALO = 16
D_INNER = 2048
HEAD_DIM = 64
N_HEADS = 32
N_GROUPS = 4
HEADS_PER_GROUP = 8
D_STATE = 128
CONV_K = 4
CONV_HALO = 8
CHUNK = 128
CONV_DIM = 3072
IN_PROJ_DIM = 5152
D_FF = 4096
N_CHIPS = 4
N_DEV = 8

ADAM_LR = 0.001
ADAM_B1 = 0.9
ADAM_B2 = 0.999
ADAM_EPS = 1e-08
ADAM_WD = 0.01
ADAM_STEP = 10

VMEM_LIMIT = 56 * 1024 * 1024
NEG_INF = float("-inf")


def _pcall(body, **kw):
    return pl.pallas_call(body, **kw)


def _params(*sem):
    return pltpu.CompilerParams(dimension_semantics=sem, vmem_limit_bytes=VMEM_LIMIT)


def _sigmoid(v):
    return 1.0 / (1.0 + jnp.exp(-v))


def _row_spec(tb, d, nb=None, reverse=False):
    if reverse:
        return pl.BlockSpec((tb, d), lambda i: (nb - 1 - i, 0))
    return pl.BlockSpec((tb, d), lambda i: (i, 0))


def _const_spec(shape):
    return pl.BlockSpec(shape, lambda *_: tuple(0 for _ in shape))


_DIMS = {"nn": (((1,), (0,)), ((), ())),
         "nt": (((1,), (1,)), ((), ())),
         "tn": (((0,), (0,)), ((), ()))}


_MATMUL_VMEM_BUDGET = 40 * 1024 * 1024


def _matmul_tiles(m_dim, n_dim, k_dim, a_bytes, b_bytes, mn_bytes):
    tm, tn = min(m_dim, 1024), min(n_dim, 1024)
    while 2 * (tm * k_dim * a_bytes + tn * k_dim * b_bytes + tm * tn * mn_bytes) > _MATMUL_VMEM_BUDGET:
        if tm >= tn:
            tm //= 2
        else:
            tn //= 2
    return tm, tn


def _matmul(a, b, mode, *, name, out_dtype=F32, a_relu2=False, a_norm=None, add=None, relu2_grad_of=None,
            out_col_shards=1, b_col_shards=False, norm_bwd=None, loss_head=None):
    if mode == "tn":
        k_dim, m_dim = a.shape
    else:
        m_dim, k_dim = a.shape
    if b_col_shards:
        n_shards, shard_cols = b.shape[0], b.shape[2]
        n_dim = n_shards * shard_cols if mode == "nn" else b.shape[1]
    else:
        n_dim = b.shape[0] if mode == "nt" else b.shape[1]
    mn_bytes = jnp.dtype(out_dtype).itemsize
    if relu2_grad_of is not None:
        mn_bytes += relu2_grad_of.dtype.itemsize
    if add is not None:
        mn_bytes += add.dtype.itemsize
    row_epilogue = norm_bwd is not None or loss_head is not None
    if row_epilogue:
        assert out_dtype == F32 and out_col_shards == 1 and (norm_bwd is None or loss_head is None)
        mn_bytes += 4 + 4 + 2 + 8
    a_bytes = a.dtype.itemsize + (2 if a_norm is not None else 0)
    tm, tn = _matmul_tiles(m_dim, n_dim, k_dim, a_bytes, b.dtype.itemsize, mn_bytes)
    if row_epilogue:
        while tn < n_dim:
            tm, tn = tm // 2, tn * 2
    assert m_dim % tm == 0 and n_dim % tn == 0
    a_spec = (pl.BlockSpec((k_dim, tm), lambda i, j: (0, i)) if mode == "tn"
              else pl.BlockSpec((tm, k_dim), lambda i, j: (i, 0)))
    if b_col_shards and mode == "nn":
        assert shard_cols % tn == 0
        per_shard = shard_cols // tn
        b_spec = pl.BlockSpec((None, k_dim, tn), lambda i, j: (j // per_shard, 0, j % per_shard))
    elif b_col_shards:
        assert mode == "nt" and k_dim == n_shards * shard_cols
        b_spec = pl.BlockSpec((n_shards, tn, shard_cols), lambda i, j: (0, j, 0))
    else:
        b_spec = (pl.BlockSpec((tn, k_dim), lambda i, j: (j, 0)) if mode == "nt"
                  else pl.BlockSpec((k_dim, tn), lambda i, j: (0, j)))
    mn_spec = pl.BlockSpec((tm, tn), lambda i, j: (i, j))
    operands, in_specs = [a, b], [a_spec, b_spec]
    if relu2_grad_of is not None:
        operands.append(relu2_grad_of)
        in_specs.append(mn_spec)
    if add is not None:
        operands.append(add)
        in_specs.append(mn_spec)
    gain_spec = pl.BlockSpec((1, n_dim), lambda i, j: (0, 0))
    if a_norm is not None:
        assert mode in ("nn", "nt") and a.dtype == F32 and not row_epilogue and out_col_shards == 1
        operands.append(a_norm)
        in_specs.append(pl.BlockSpec((1, k_dim), lambda i, j: (0, 0)))
    if norm_bwd is not None:
        h_in, g_in, dres_in = norm_bwd
        operands += [h_in, g_in, dres_in]
        in_specs += [mn_spec, gain_spec, mn_spec]
    if loss_head is not None:
        operands += list(loss_head)
        in_specs += [gain_spec, mn_spec]
    if row_epilogue:
        out_shape = (jax.ShapeDtypeStruct((m_dim, n_dim), F32), jax.ShapeDtypeStruct((m_dim, n_dim), BF16),
                     jax.ShapeDtypeStruct((8, n_dim), F32))
        out_spec = (mn_spec, mn_spec, pl.BlockSpec((8, n_dim), lambda i, j: (0, 0)))
    elif out_col_shards == 1:
        out_shape = jax.ShapeDtypeStruct((m_dim, n_dim), out_dtype)
        out_spec = mn_spec
    else:
        n_shard = n_dim // out_col_shards
        assert n_shard % tn == 0
        per = n_shard // tn
        out_shape = jax.ShapeDtypeStruct((out_col_shards, m_dim, n_shard), out_dtype)
        out_spec = pl.BlockSpec((None, tm, tn), lambda i, j: (j // per, i, j % per))
    if a_norm is not None:
        out_shape = (out_shape, jax.ShapeDtypeStruct((m_dim, k_dim), BF16))
        out_spec = (out_spec, pl.BlockSpec((tm, k_dim), lambda i, j: (i, 0)))

    n_in = len(operands)

    def body(*refs):
        a_ref, b_ref, o_ref = refs[0], refs[1], refs[n_in]
        if a_norm is not None:
            normed_ref = refs[n_in + 1]

            @pl.when(pl.program_id(1) == 0)
            def _():
                xa = a_ref[...]
                normed_ref[...] = (xa * _rms(xa) * refs[n_in - 1][...]).astype(BF16)

            av = normed_ref[...]
        else:
            av = a_ref[...]
        if a_relu2:
            av = jnp.maximum(av, 0)
            av = av * av
        if b_col_shards and mode == "nt":
            r = None
            for s in range(n_shards):
                part = lax.dot_general(av[:, s * shard_cols:(s + 1) * shard_cols].astype(BF16),
                                       b_ref[s].astype(BF16), _DIMS[mode], preferred_element_type=F32)
                r = part if r is None else r + part
        else:
            r = lax.dot_general(av.astype(BF16), b_ref[...].astype(BF16), _DIMS[mode],
                                preferred_element_type=F32)
        nxt = 2
        if relu2_grad_of is not None:
            r = r * (2.0 * jnp.maximum(refs[nxt][...].astype(F32), 0.0))
            nxt += 1
        if add is not None:
            r = r + refs[nxt][...]
            nxt += 1
        if not row_epilogue:
            o_ref[...] = r.astype(out_dtype)
            return
        dh16_ref, small_ref = refs[n_in + 1], refs[n_in + 2]

        @pl.when(pl.program_id(0) == 0)
        def _():
            small_ref[...] = jnp.zeros_like(small_ref)

        if norm_bwd is not None:
            h_ref, g_ref, dres_ref = refs[nxt:nxt + 3]
            x, dy = h_ref[...], r
        else:
            g_ref, t_ref = refs[nxt:nxt + 2]
            x = r
        rr = _rms(x)
        xhat = x * rr
        gv = g_ref[...]
        if loss_head is not None:
            err = xhat * gv - t_ref[...]
            small_ref[1:2, :] += (0.5 / n_dim) * jnp.sum(err * err, axis=0, keepdims=True)
            dy = err * (1.0 / n_dim)
        dxhat = dy * gv
        dh = rr * (dxhat - xhat * jnp.mean(dxhat * xhat, axis=-1, keepdims=True))
        if norm_bwd is not None:
            dh = dres_ref[...] + dh
        o_ref[...] = dh
        dh16_ref[...] = dh.astype(BF16)
        small_ref[0:1, :] += jnp.sum(dy * xhat, axis=0, keepdims=True)

    if row_epilogue:
        semantics = ("arbitrary", "arbitrary")
    elif a_norm is not None:
        semantics = ("parallel", "arbitrary")
    else:
        semantics = ("parallel", "parallel")
    return _pcall(
        body, name=name, out_shape=out_shape,
        grid=(m_dim // tm, n_dim // tn),
        in_specs=in_specs, out_specs=out_spec,
        compiler_params=_params(*semantics),
    )(*operands)


def _rms(x):
    return lax.rsqrt(jnp.mean(x * x, axis=-1, keepdims=True) + RMS_EPS)


def _pool_mixed(ext, hn, t0, tb):
    t = t0 + lax.broadcasted_iota(jnp.int32, (tb, 1), 0)
    parts = []
    for gi, w in enumerate(POOL_WINDOWS):
        lanes = slice(gi * POOL_GROUP, (gi + 1) * POOL_GROUP)
        s = ext[:, lanes]
        k = 1
        while k < w:
            s = s + pltpu.roll(s, k, 0)
            k *= 2
        cnt = jnp.minimum(t + 1, w).astype(F32)
        parts.append(s[POOL_HALO:, :] / cnt - hn[:, lanes])
    return parts


def _pool_fwd(x, g, pw, pb, ps, *, tb=512):
    t_dim, d = x.shape

    def body(x_ref, g_ref, pw_ref, pb_ref, ps_ref, o_ref, ext_ref):
        i = pl.program_id(0)

        @pl.when(i == 0)
        def _():
            ext_ref[0:POOL_HALO, :] = jnp.zeros((POOL_HALO, d), F32)

        xv = x_ref[...]
        hn = xv * _rms(xv) * g_ref[...]
        ext_ref[POOL_HALO:, :] = hn
        mixed = _pool_mixed(ext_ref[...], hn, i * tb, tb)
        for gi in range(len(POOL_WINDOWS)):
            lanes = slice(gi * POOL_GROUP, (gi + 1) * POOL_GROUP)
            out = jnp.dot(mixed[gi].astype(BF16), pw_ref[gi], preferred_element_type=F32)
            o_ref[:, lanes] = xv[:, lanes] + (out + pb_ref[:, lanes]) * ps_ref[:, lanes]
        ext_ref[0:POOL_HALO, :] = hn[tb - POOL_HALO:, :]

    return _pcall(
        body, name="pool_fwd", out_shape=jax.ShapeDtypeStruct((t_dim, d), F32),
        grid=(t_dim // tb,),
        in_specs=[_row_spec(tb, d), _const_spec((1, d)), _const_spec((4, POOL_GROUP, POOL_GROUP)),
                  _const_spec((1, d)), _const_spec((1, d))],
        out_specs=_row_spec(tb, d),
        scratch_shapes=[pltpu.VMEM((POOL_HALO + tb, d), F32)],
        compiler_params=_params("arbitrary"),
    )(x, g, pw, pb, ps)


def _pool_bwd(x, g, pw, pb, ps, dh1, *, tb=512):
    t_dim, d = x.shape
    nb = t_dim // tb
    halo_per_block = tb // POOL_HALO

    def body(x_ref, xprev_ref, g_ref, pw_ref, pb_ref, ps_ref, dh1_ref,
             dx_ref, dpw_ref, small_ref, ext_ref, dext_ref):
        i = pl.program_id(0)
        blk = nb - 1 - i

        @pl.when(i == 0)
        def _():
            dpw_ref[...] = jnp.zeros_like(dpw_ref)
            small_ref[...] = jnp.zeros_like(small_ref)
            dext_ref[tb:, :] = jnp.zeros((POOL_HALO, d), F32)

        gv = g_ref[...]
        xv = x_ref[...]
        r = _rms(xv)
        xhat = xv * r
        hn = xhat * gv
        xp = xprev_ref[...]
        hprev = xp * _rms(xp) * gv * (blk > 0).astype(F32)
        ext_ref[0:POOL_HALO, :] = hprev
        ext_ref[POOL_HALO:, :] = hn
        mixed = _pool_mixed(ext_ref[...], hn, blk * tb, tb)

        dout = dh1_ref[...]
        t = blk * tb + lax.broadcasted_iota(jnp.int32, (tb, 1), 0)
        for gi, w in enumerate(POOL_WINDOWS):
            lanes = slice(gi * POOL_GROUP, (gi + 1) * POOL_GROUP)
            mb = mixed[gi].astype(BF16)
            pre = jnp.dot(mb, pw_ref[gi], preferred_element_type=F32) + pb_ref[:, lanes]
            dg_out = dout[:, lanes]
            small_ref[2:3, lanes] += jnp.sum(dg_out * pre, axis=0, keepdims=True)
            dpre = dg_out * ps_ref[:, lanes]
            small_ref[1:2, lanes] += jnp.sum(dpre, axis=0, keepdims=True)
            dpb16 = dpre.astype(BF16)
            dpw_ref[gi] += lax.dot_general(mb, dpb16, _DIMS["tn"], preferred_element_type=F32)
            dmixed = lax.dot_general(dpb16, pw_ref[gi], _DIMS["nt"], preferred_element_type=F32)
            cnt = jnp.minimum(t + 1, w).astype(F32)
            dq = dmixed / cnt
            dext_ref[0:tb, lanes] = dq
            s = dext_ref[:, lanes]
            k = 1
            while k < w:
                s = s + pltpu.roll(s, tb + POOL_HALO - k, 0)
                k *= 2
            dhn = s[0:tb, :] - dmixed
            dext_ref[tb:, lanes] = dq[0:POOL_HALO, :]
            small_ref[0:1, lanes] += jnp.sum(dhn * xhat[:, lanes], axis=0, keepdims=True)
            ext_ref[POOL_HALO:, lanes] = dhn * gv[:, lanes]
        dxhat = ext_ref[POOL_HALO:, :]
        dx_ref[...] = dout + r * (dxhat - xhat * jnp.mean(dxhat * xhat, axis=-1, keepdims=True))

    return _pcall(
        body, name="pool_bwd",
        out_shape=(jax.ShapeDtypeStruct((t_dim, d), F32),
                   jax.ShapeDtypeStruct((4, POOL_GROUP, POOL_GROUP), F32),
                   jax.ShapeDtypeStruct((8, d), F32)),
        grid=(nb,),
        in_specs=[_row_spec(tb, d, nb, True),
                  pl.BlockSpec((POOL_HALO, d),
                               lambda i: (jnp.maximum((nb - 1 - i) * halo_per_block - 1, 0), 0)),
                  _const_spec((1, d)), _const_spec((4, POOL_GROUP, POOL_GROUP)),
                  _const_spec((1, d)), _const_spec((1, d)), _row_spec(tb, d, nb, True)],
        out_specs=(_row_spec(tb, d, nb, True), _const_spec((4, POOL_GROUP, POOL_GROUP)),
                   _const_spec((8, d))),
        scratch_shapes=[pltpu.VMEM((POOL_HALO + tb, d), F32), pltpu.VMEM((tb + POOL_HALO, d), F32)],
        compiler_params=_params("arbitrary"),
    )(x, x, g, pw, pb, ps, dh1)


_CONV_CB = 1024
_STRIP = 16


def _strips(tb, fn, unroll=4):
    def step(i, carry):
        fn(pl.multiple_of(i * _STRIP, _STRIP))
        return carry
    lax.fori_loop(0, tb // _STRIP, step, 0, unroll=unroll)


def _conv_taps(ext_ref, r0, w):
    shifted = [ext_ref[CONV_HALO + r0 - sh:CONV_HALO + r0 - sh + _STRIP, :] for sh in range(CONV_K)]
    acc = shifted[0] * w[CONV_K - 1:CONV_K, :]
    for sh in range(1, CONV_K):
        acc = acc + shifted[sh] * w[CONV_K - 1 - sh:CONV_K - sh, :]
    return shifted, acc


def _conv_fwd(u, w, b, *, tb=512):
    t_dim, c = u.shape
    cb = _CONV_CB

    def body(u_ref, w_ref, b_ref, o_ref, ext_ref):
        @pl.when(pl.program_id(1) == 0)
        def _():
            ext_ref[0:CONV_HALO, :] = jnp.zeros((CONV_HALO, cb), F32)

        wv = w_ref[...]
        bv = b_ref[...]

        def fill(r0):
            ext_ref[pl.ds(CONV_HALO + r0, _STRIP), :] = u_ref[pl.ds(r0, _STRIP), :].astype(F32)

        _strips(tb, fill)
        for r0 in range(0, tb, _STRIP):
            v = _conv_taps(ext_ref, r0, wv)[1] + bv
            o_ref[r0:r0 + _STRIP, :] = (v * _sigmoid(v)).astype(BF16)
        ext_ref[0:CONV_HALO, :] = ext_ref[tb:tb + CONV_HALO, :]

    blk = pl.BlockSpec((tb, cb), lambda j, t: (t, j))
    return _pcall(
        body, name="conv_fwd", out_shape=jax.ShapeDtypeStruct((t_dim, c), BF16),
        grid=(c // cb, t_dim // tb),
        in_specs=[blk, pl.BlockSpec((CONV_K, cb), lambda j, t: (0, j)),
                  pl.BlockSpec((1, cb), lambda j, t: (0, j))],
        out_specs=blk,
        scratch_shapes=[pltpu.VMEM((CONV_HALO + tb, cb), F32)],
        compiler_params=_params("parallel", "arbitrary"),
    )(u, w, b)


def _conv_bwd_act(u, dxc, w, b, *, tb=512):
    t_dim, c = u.shape
    cb = _CONV_CB
    half = _STRIP // 2

    def body(u_ref, d_ref, w_ref, b_ref, dv_ref, dwb_ref, ext_ref, acc_ref):
        @pl.when(pl.program_id(1) == 0)
        def _():
            ext_ref[0:CONV_HALO, :] = jnp.zeros((CONV_HALO, cb), F32)
            dwb_ref[...] = jnp.zeros_like(dwb_ref)

        acc_ref[...] = jnp.zeros_like(acc_ref)
        wv = w_ref[...]
        bv = b_ref[...]

        def fill(r0):
            ext_ref[pl.ds(CONV_HALO + r0, _STRIP), :] = u_ref[pl.ds(r0, _STRIP), :].astype(F32)

        _strips(tb, fill)
        for r0 in range(0, tb, _STRIP):
            shifted, v = _conv_taps(ext_ref, r0, wv)
            v = v + bv
            sg = _sigmoid(v)
            dv = d_ref[r0:r0 + _STRIP, :].astype(F32) * (sg * (1.0 + v * (1.0 - sg)))
            dv_ref[r0:r0 + _STRIP, :] = dv.astype(BF16)
            acc_ref[CONV_K] += dv[0:half, :] + dv[half:, :]
            for sh in range(CONV_K):
                p = dv * shifted[sh]
                acc_ref[CONV_K - 1 - sh] += p[0:half, :] + p[half:, :]
        for k in range(CONV_K + 1):
            dwb_ref[k:k + 1, :] += jnp.sum(acc_ref[k], axis=0, keepdims=True)
        ext_ref[0:CONV_HALO, :] = ext_ref[tb:tb + CONV_HALO, :]

    blk = pl.BlockSpec((tb, cb), lambda j, t: (t, j))
    return _pcall(
        body, name="conv_bwd_act",
        out_shape=(jax.ShapeDtypeStruct((t_dim, c), BF16), jax.ShapeDtypeStruct((8, c), F32)),
        grid=(c // cb, t_dim // tb),
        in_specs=[blk, blk, pl.BlockSpec((CONV_K, cb), lambda j, t: (0, j)),
                  pl.BlockSpec((1, cb), lambda j, t: (0, j))],
        out_specs=(blk, pl.BlockSpec((8, cb), lambda j, t: (0, j))),
        scratch_shapes=[pltpu.VMEM((CONV_HALO + tb, cb), F32), pltpu.VMEM((CONV_K + 1, half, cb), F32)],
        compiler_params=_params("parallel", "arbitrary"),
    )(u, dxc, w, b)


def _conv_bwd_in(dv, w, *, tb=512):
    t_dim, c = dv.shape
    cb = _CONV_CB
    nb = t_dim // tb

    def body(dv_ref, w_ref, du_ref, ext_ref):
        @pl.when(pl.program_id(1) == 0)
        def _():
            ext_ref[tb:, :] = jnp.zeros((CONV_HALO, cb), F32)

        wv = w_ref[...]

        def fill(r0):
            ext_ref[pl.ds(r0, _STRIP), :] = dv_ref[pl.ds(r0, _STRIP), :].astype(F32)

        _strips(tb, fill)
        for r0 in range(0, tb, _STRIP):
            acc = ext_ref[r0:r0 + _STRIP, :] * wv[CONV_K - 1:CONV_K, :]
            for sh in range(1, CONV_K):
                acc = acc + ext_ref[r0 + sh:r0 + sh + _STRIP, :] * wv[CONV_K - 1 - sh:CONV_K - sh, :]
            du_ref[r0:r0 + _STRIP, :] = acc.astype(BF16)
        ext_ref[tb:, :] = ext_ref[0:CONV_HALO, :]

    blk = pl.BlockSpec((tb, cb), lambda j, t: (nb - 1 - t, j))
    return _pcall(
        body, name="conv_bwd_in", out_shape=jax.ShapeDtypeStruct((t_dim, c), BF16),
        grid=(c // cb, nb),
        in_specs=[blk, pl.BlockSpec((CONV_K, cb), lambda j, t: (0, j))],
        out_specs=blk,
        scratch_shapes=[pltpu.VMEM((tb + CONV_HALO, cb), F32)],
        compiler_params=_params("parallel", "arbitrary"),
    )(dv, w)


def _softplus(v):
    e = jnp.exp(-jnp.abs(v))
    w = 1.0 + e
    log1p = jnp.where(w == 1.0, e, jnp.log(w) * e / jnp.where(w == 1.0, 1.0, w - 1.0))
    return jnp.maximum(v, 0.0) + log1p


def _cumsum_rows(v):
    row = lax.broadcasted_iota(jnp.int32, v.shape, 0) & (CHUNK - 1)
    k = 1
    while k < CHUNK:
        v = v + jnp.where(row >= k, pltpu.roll(v, k, 0), 0.0)
        k *= 2
    return v


def _cumsum_lanes(v):
    col = lax.broadcasted_iota(jnp.int32, v.shape, 1) & (CHUNK - 1)
    k = 1
    while k < CHUNK:
        v = v + jnp.where(col >= k, pltpu.roll(v, k, 1), 0.0)
        k *= 2
    return v


def _rev_cumsum_rows(v):
    row = lax.broadcasted_iota(jnp.int32, v.shape, 0)
    k = 1
    while k < CHUNK:
        v = v + jnp.where(row < CHUNK - k, pltpu.roll(v, CHUNK - k, 0), 0.0)
        k *= 2
    return v


PAIR = 2 * HEAD_DIM
GROUP_LANES = HEADS_PER_GROUP * HEAD_DIM


def _head_lane_matrix():
    h = lax.broadcasted_iota(jnp.int32, (128, D_INNER), 0)
    j = lax.broadcasted_iota(jnp.int32, (128, D_INNER), 1)
    return (j // HEAD_DIM == h).astype(BF16)


def _split_bf16(v, pieces):
    out = []
    for _ in range(pieces):
        p = v.astype(BF16)
        out.append(p)
        v = v - p.astype(F32)
    return out


_EXACT_PIECES = 3


def _expand_heads(values, e3):
    lhs = jnp.concatenate([jnp.concatenate(_split_bf16(v, _EXACT_PIECES), axis=1) for v in values], axis=0)
    out = jnp.dot(lhs, e3, preferred_element_type=F32)
    rows = values[0].shape[0]
    return [out[i * rows:(i + 1) * rows, :] for i in range(len(values))]


def _reduce_heads(v, et, pieces):
    return sum(jnp.dot(p, et, preferred_element_type=F32) for p in _split_bf16(v, pieces))


def _ssd_decay(dt_raw, dt_raw_t, bias_r, bias_c, alog_r, alog_c, *, tb=1024):
    t_dim = dt_raw.shape[0]
    tb = min(tb, t_dim)
    assert t_dim % tb == 0 and tb % CHUNK == 0

    def body(dtr_ref, dtt_ref, br_ref, bc_ref, ar_ref, ac_ref, dt_ref, acs_ref, sg_ref, acst_ref):
        pre = dtr_ref[...] + br_ref[...]
        dt = _softplus(pre)
        dt_ref[...] = dt
        sg_ref[...] = _sigmoid(pre)
        acs_ref[...] = _cumsum_rows(dt * (-jnp.exp(ar_ref[...])))
        acst_ref[...] = _cumsum_lanes(_softplus(dtt_ref[...] + bc_ref[...]) * (-jnp.exp(ac_ref[...])))

    rows = pl.BlockSpec((tb, 128), lambda i: (i, 0))
    cols = pl.BlockSpec((N_HEADS, tb), lambda i: (0, i))
    sds = jax.ShapeDtypeStruct((t_dim, 128), F32)
    return _pcall(
        body, name="ssd_decay",
        out_shape=(sds, sds, sds, jax.ShapeDtypeStruct((N_HEADS, t_dim), F32)),
        grid=(t_dim // tb,),
        in_specs=[rows, cols, _const_spec((1, 128)), _const_spec((N_HEADS, 1)),
                  _const_spec((1, 128)), _const_spec((N_HEADS, 1))],
        out_specs=(rows, rows, rows, cols), compiler_params=_params("parallel"),
    )(dt_raw, dt_raw_t, bias_r, bias_c, alog_r, alog_c)


def _pair_decay(acs_slab, acs_c, h0, causal, left):
    other = pltpu.roll(acs_slab, HEAD_DIM, 1)
    col0 = jnp.where(left, acs_slab, other)
    col1 = jnp.where(left, other, acs_slab)
    l0 = jnp.exp(jnp.where(causal, col0 - acs_c[h0:h0 + 1, :], NEG_INF))
    l1 = jnp.exp(jnp.where(causal, col1 - acs_c[h0 + 1:h0 + 2, :], NEG_INF))
    return l0, l1


def _ssd_fwd(xc, dt_r, acs_r, acs_t, dskip_x, e3_mat):
    t_dim = xc.shape[0]
    nc = t_dim // CHUNK

    def body(xc_ref, dt_ref, acs_ref, acst_ref, dk_ref, e3_ref, y_ref, st_ref, state):
        @pl.when(pl.program_id(0) == 0)
        def _():
            state[...] = jnp.zeros_like(state)

        dt, acs = _expand_heads([dt_ref[...], acs_ref[...]], e3_ref[...])
        acs_c = acst_ref[...]
        st_ref[0] = state[...]
        last = acs[CHUNK - 1:CHUNK, :]
        xs32 = xc_ref[:, 0:D_INNER].astype(F32)
        xdt = xs32 * dt
        xdt16 = xdt.astype(BF16)
        xdte16 = (xdt * jnp.exp(last - acs)).astype(BF16)
        ea = jnp.exp(acs)
        cd = jnp.exp(last)
        skip = dk_ref[...] * xs32
        causal = (lax.broadcasted_iota(jnp.int32, (CHUNK, CHUNK), 0)
                  >= lax.broadcasted_iota(jnp.int32, (CHUNK, CHUNK), 1))
        left = lax.broadcasted_iota(jnp.int32, (CHUNK, PAIR), 1) < HEAD_DIM
        for g in range(N_GROUPS):
            gl = slice(g * GROUP_LANES, (g + 1) * GROUP_LANES)
            bg = xc_ref[:, D_INNER + g * D_STATE:D_INNER + (g + 1) * D_STATE]
            cg = xc_ref[:, D_INNER + (N_GROUPS + g) * D_STATE:D_INNER + (N_GROUPS + g + 1) * D_STATE]
            cb = lax.dot_general(cg, bg, _DIMS["nt"], preferred_element_type=F32)
            hprev = state[:, gl]
            ch = jnp.dot(cg, hprev.astype(BF16), preferred_element_type=F32)
            for j in range(HEADS_PER_GROUP // 2):
                pl_ = slice(g * GROUP_LANES + j * PAIR, g * GROUP_LANES + (j + 1) * PAIR)
                h0 = g * HEADS_PER_GROUP + 2 * j
                l0, l1 = _pair_decay(acs[:, pl_], acs_c, h0, causal, left)
                lhs = jnp.concatenate([(cb * l0).astype(BF16), (cb * l1).astype(BF16)], axis=1)
                xp = xdt16[:, pl_]
                zero = jnp.zeros_like(xp)
                rhs = jnp.concatenate([jnp.where(left, xp, zero), jnp.where(left, zero, xp)], axis=0)
                ydiag = jnp.dot(lhs, rhs, preferred_element_type=F32)
                y_ref[:, pl_] = (ydiag + ch[:, j * PAIR:(j + 1) * PAIR] * ea[:, pl_] + skip[:, pl_]).astype(BF16)
            s_new = lax.dot_general(bg, xdte16[:, gl], _DIMS["tn"], preferred_element_type=F32)
            state[:, gl] = hprev * cd[:, gl] + s_new

    rows = lambda w: pl.BlockSpec((CHUNK, w), lambda c: (c, 0))
    return _pcall(
        body, name="ssd_fwd",
        out_shape=(jax.ShapeDtypeStruct((t_dim, D_INNER), BF16),
                   jax.ShapeDtypeStruct((nc, D_STATE, D_INNER), F32)),
        grid=(nc,),
        in_specs=[rows(CONV_DIM), rows(128), rows(128), pl.BlockSpec((N_HEADS, CHUNK), lambda c: (0, c)),
                  _const_spec((1, D_INNER)), _const_spec((_EXACT_PIECES * 128, D_INNER))],
        out_specs=(rows(D_INNER), pl.BlockSpec((1, D_STATE, D_INNER), lambda c: (c, 0, 0))),
        scratch_shapes=[pltpu.VMEM((D_STATE, D_INNER), F32)],
        compiler_params=_params("arbitrary"),
    )(xc, dt_r, acs_r, acs_t, dskip_x, e3_mat)


def _ssd_bwd(xc, dt_r, acs_r, sg_r, acs_t, alog_r, dskip_x, e3_mat, et_mat, states, dy):
    t_dim = xc.shape[0]
    nc = t_dim // CHUNK

    def body(xc_ref, dt_ref, acs_ref, sg_ref, acst_ref, ar_ref, dk_ref, e3_ref, et_ref, st_ref, dy_ref,
             dxc_ref, ddt_ref, small_ref, dstate, dacs_ref, dxdt_ref, acc_x, acc_r):
        step = pl.program_id(0)

        @pl.when(step == 0)
        def _():
            dstate[...] = jnp.zeros_like(dstate)
            acc_x[...] = jnp.zeros_like(acc_x)
            acc_r[...] = jnp.zeros_like(acc_r)

        dt_r = dt_ref[...]
        a_r = -jnp.exp(ar_ref[...])
        dt, acs = _expand_heads([dt_r, acs_ref[...]], e3_ref[...])
        acs_c = acst_ref[...]
        last = acs[CHUNK - 1:CHUNK, :]
        xs32 = xc_ref[:, 0:D_INNER].astype(F32)
        xdt = xs32 * dt
        xdt16 = xdt.astype(BF16)
        dte = jnp.exp(last - acs)
        xdte = xdt * dte
        xdte16 = xdte.astype(BF16)
        cd = jnp.exp(last)
        dy16 = dy_ref[...]
        dyv = dy16.astype(F32)
        dye = dyv * jnp.exp(acs)
        dye16 = dye.astype(BF16)
        causal = (lax.broadcasted_iota(jnp.int32, (CHUNK, CHUNK), 0)
                  >= lax.broadcasted_iota(jnp.int32, (CHUNK, CHUNK), 1))
        left = lax.broadcasted_iota(jnp.int32, (CHUNK, PAIR), 1) < HEAD_DIM
        lane_id = lax.broadcasted_iota(jnp.int32, (CHUNK, 128), 1)
        row_id = lax.broadcasted_iota(jnp.int32, (CHUNK, 128), 0)
        is_last_row = lax.broadcasted_iota(jnp.int32, (CHUNK, 1), 0) == CHUNK - 1
        dacs_cols = jnp.zeros((CHUNK, 128), F32)
        dacs_rows = jnp.zeros((CHUNK, 128), F32)
        for g in range(N_GROUPS):
            gl = slice(g * GROUP_LANES, (g + 1) * GROUP_LANES)
            b_lanes = slice(D_INNER + g * D_STATE, D_INNER + (g + 1) * D_STATE)
            c_lanes = slice(D_INNER + (N_GROUPS + g) * D_STATE, D_INNER + (N_GROUPS + g + 1) * D_STATE)
            bg = xc_ref[:, b_lanes]
            cg = xc_ref[:, c_lanes]
            cb = lax.dot_general(cg, bg, _DIMS["nt"], preferred_element_type=F32)
            hprev = st_ref[0, :, gl]
            hp16 = hprev.astype(BF16)
            dhn = dstate[:, gl]
            dhn16 = dhn.astype(BF16)
            ch = jnp.dot(cg, hp16, preferred_element_type=F32)
            gmat = jnp.dot(bg, dhn16, preferred_element_type=F32)
            gx = gmat * xdte[:, gl]
            dlast = jnp.sum(gx, axis=0, keepdims=True) + cd[:, gl] * jnp.sum(dhn * hprev, axis=0, keepdims=True)
            dacs_ref[:, gl] = dye[:, gl] * ch - gx + jnp.where(is_last_row, dlast, 0.0)
            dc_acc = lax.dot_general(dye16[:, gl], hp16, _DIMS["nt"], preferred_element_type=F32)
            db_acc = lax.dot_general(xdte16[:, gl], dhn16, _DIMS["nt"], preferred_element_type=F32)
            dstate[:, gl] = dhn * cd[:, gl] + lax.dot_general(cg, dye16[:, gl], _DIMS["tn"],
                                                             preferred_element_type=F32)
            dcb = jnp.zeros((CHUNK, CHUNK), F32)
            for j in range(HEADS_PER_GROUP // 2):
                pl_ = slice(g * GROUP_LANES + j * PAIR, g * GROUP_LANES + (j + 1) * PAIR)
                h0 = g * HEADS_PER_GROUP + 2 * j
                l0, l1 = _pair_decay(acs[:, pl_], acs_c, h0, causal, left)
                m0, m1 = cb * l0, cb * l1
                lhs = jnp.concatenate([m0.astype(BF16), m1.astype(BF16)], axis=1)
                dyp = dy16[:, pl_]
                zero = jnp.zeros_like(dyp)
                both = lax.dot_general(lhs, dyp, _DIMS["tn"], preferred_element_type=F32)
                dxdt_ref[:, pl_] = (jnp.where(left, both[0:CHUNK, :], both[CHUNK:, :])
                                    + gmat[:, j * PAIR:(j + 1) * PAIR] * dte[:, pl_])
                lhs2 = jnp.concatenate([jnp.where(left, dyp, zero), jnp.where(left, zero, dyp)], axis=0)
                dm = lax.dot_general(lhs2, xdt16[:, pl_], _DIMS["nt"], preferred_element_type=F32)
                dm0, dm1 = dm[0:CHUNK, :], dm[CHUNK:, :]
                dcb = dcb + dm0 * l0 + dm1 * l1
                ds0, ds1 = dm0 * m0, dm1 * m1
                dacs_cols = jnp.where(lane_id == h0, jnp.sum(ds0, axis=1, keepdims=True), dacs_cols)
                dacs_cols = jnp.where(lane_id == h0 + 1, jnp.sum(ds1, axis=1, keepdims=True), dacs_cols)
                dacs_rows = jnp.where(row_id == h0, jnp.sum(ds0, axis=0, keepdims=True), dacs_rows)
                dacs_rows = jnp.where(row_id == h0 + 1, jnp.sum(ds1, axis=0, keepdims=True), dacs_rows)
            dcb16 = dcb.astype(BF16)
            dxc_ref[:, c_lanes] = (dc_acc + jnp.dot(dcb16, bg, preferred_element_type=F32)).astype(BF16)
            dxc_ref[:, b_lanes] = (db_acc + lax.dot_general(dcb16, cg, _DIMS["tn"],
                                                           preferred_element_type=F32)).astype(BF16)
        dxdt = dxdt_ref[...]
        dxc_ref[:, 0:D_INNER] = (dxdt * dt + dk_ref[...] * dyv).astype(BF16)
        acc_x[0:1, :] += jnp.sum(dyv * xs32, axis=0, keepdims=True)
        et = et_ref[...]
        dacs = _reduce_heads(dacs_ref[...], et, 2) + dacs_cols - dacs_rows.T
        dadt = _rev_cumsum_rows(dacs)
        ddraw = (_reduce_heads(dxdt * xs32, et, 1) + dadt * a_r) * sg_ref[...]
        ddraw = jnp.where(lane_id < N_HEADS, ddraw, 0.0)
        ddt_ref[...] = ddraw
        acc_r[0:1, :] += jnp.where(lane_id[0:1, :] < N_HEADS,
                                   jnp.sum(dadt * dt_r, axis=0, keepdims=True) * a_r, 0.0)
        acc_r[1:2, :] += jnp.sum(ddraw, axis=0, keepdims=True)

        @pl.when(step == nc - 1)
        def _():
            dd = _reduce_heads(acc_x[...], et_ref[...], 3)
            rid = lax.broadcasted_iota(jnp.int32, (8, 128), 0)
            small_ref[...] = acc_r[...] + jnp.where(rid == 2, pltpu.roll(dd, 2, 0), 0.0)

    rev = lambda w: pl.BlockSpec((CHUNK, w), lambda c: (nc - 1 - c, 0))
    return _pcall(
        body, name="ssd_bwd",
        out_shape=(jax.ShapeDtypeStruct((t_dim, CONV_DIM), BF16),
                   jax.ShapeDtypeStruct((t_dim, 128), F32),
                   jax.ShapeDtypeStruct((8, 128), F32)),
        grid=(nc,),
        in_specs=[rev(CONV_DIM), rev(128), rev(128), rev(128),
                  pl.BlockSpec((N_HEADS, CHUNK), lambda c: (0, nc - 1 - c)),
                  _const_spec((1, 128)), _const_spec((1, D_INNER)),
                  _const_spec((_EXACT_PIECES * 128, D_INNER)), _const_spec((D_INNER, 128)),
                  pl.BlockSpec((1, D_STATE, D_INNER), lambda c: (nc - 1 - c, 0, 0)),
                  rev(D_INNER)],
        out_specs=(rev(CONV_DIM), rev(128), _const_spec((8, 128))),
        scratch_shapes=[pltpu.VMEM((D_STATE, D_INNER), F32), pltpu.VMEM((CHUNK, D_INNER), F32),
                        pltpu.VMEM((CHUNK, D_INNER), F32), pltpu.VMEM((8, D_INNER), F32),
                        pltpu.VMEM((8, 128), F32)],
        compiler_params=_params("arbitrary"),
    )(xc, dt_r, acs_r, sg_r, acs_t, alog_r, dskip_x, e3_mat, et_mat, states, dy)


_GATE_GROUP = D_INNER // N_GROUPS


def _gate_fwd(y, z, g, *, tb=256):
    t_dim = y.shape[0]

    def body(y_ref, z_ref, g_ref, o_ref):
        for gi in range(N_GROUPS):
            lanes = slice(gi * _GATE_GROUP, (gi + 1) * _GATE_GROUP)
            zv = z_ref[:, lanes].astype(F32)
            wv = y_ref[:, lanes].astype(F32) * (zv * _sigmoid(zv))
            o_ref[:, lanes] = (wv * _rms(wv) * g_ref[:, lanes]).astype(BF16)

    return _pcall(
        body, name="gate_fwd", out_shape=jax.ShapeDtypeStruct((t_dim, D_INNER), BF16),
        grid=(t_dim // tb,),
        in_specs=[_row_spec(tb, D_INNER), _row_spec(tb, D_INNER), _const_spec((1, D_INNER))],
        out_specs=_row_spec(tb, D_INNER), compiler_params=_params("parallel"),
    )(y, z, g)


def _gate_bwd(dyn, y, z, g, *, tb=256):
    t_dim = y.shape[0]

    def body(d_ref, y_ref, z_ref, g_ref, dy_ref, dz_ref, dg_ref):
        @pl.when(pl.program_id(0) == 0)
        def _():
            dg_ref[...] = jnp.zeros_like(dg_ref)

        for gi in range(N_GROUPS):
            lanes = slice(gi * _GATE_GROUP, (gi + 1) * _GATE_GROUP)
            zv = z_ref[:, lanes].astype(F32)
            sg = _sigmoid(zv)
            sz = zv * sg
            yv = y_ref[:, lanes].astype(F32)
            wv = yv * sz
            r = _rms(wv)
            what = wv * r
            dv = d_ref[:, lanes].astype(F32)
            dwhat = dv * g_ref[:, lanes]
            dw = r * (dwhat - what * jnp.mean(dwhat * what, axis=-1, keepdims=True))
            dg_ref[0:1, lanes] += jnp.sum(dv * what, axis=0, keepdims=True)
            dy_ref[:, lanes] = (dw * sz).astype(BF16)
            dz_ref[:, lanes] = (dw * yv * (sg * (1.0 + zv * (1.0 - sg)))).astype(BF16)

    return _pcall(
        body, name="gate_bwd",
        out_shape=(jax.ShapeDtypeStruct((t_dim, D_INNER), BF16),
                   jax.ShapeDtypeStruct((t_dim, D_INNER), BF16),
                   jax.ShapeDtypeStruct((8, D_INNER), F32)),
        grid=(t_dim // tb,),
        in_specs=[_row_spec(tb, D_INNER), _row_spec(tb, D_INNER), _row_spec(tb, D_INNER),
                  _const_spec((1, D_INNER))],
        out_specs=(_row_spec(tb, D_INNER), _row_spec(tb, D_INNER), _const_spec((8, D_INNER))),
        compiler_params=_params("arbitrary"),
    )(dyn, y, z, g)


_ADAM_C1 = 1.0 / (1.0 - ADAM_B1 ** ADAM_STEP)
_ADAM_C2 = 1.0 / (1.0 - ADAM_B2 ** ADAM_STEP)


def _adamw_math(w, g, m, v):
    mn = ADAM_B1 * m + (1.0 - ADAM_B1) * g
    vn = ADAM_B2 * v + (1.0 - ADAM_B2) * (g * g)
    delta = -ADAM_LR * ((mn * _ADAM_C1) / (jnp.sqrt(vn * _ADAM_C2) + ADAM_EPS) + ADAM_WD * w)
    return delta, mn, vn


def _adamw(w, g, m, v, *, name, part=None, into=None):
    r_dim, c = w.shape
    rows = r_dim if part is None else r_dim // 2
    assert g.shape == (rows, c)
    tb = max(t for t in range(8, 513, 8) if rows % t == 0)
    first = 0 if part is None else part * (rows // tb)
    n_out = 3 if part is None else 4

    def body(w_ref, g_ref, m_ref, v_ref, *rest):
        outs = rest[-n_out:]
        gv = g_ref[...]
        outs[0][...], outs[1][...], outs[2][...] = _adamw_math(w_ref[...], gv, m_ref[...], v_ref[...])
        if part is not None:
            outs[3][...] = gv

    spec = pl.BlockSpec((tb, c), lambda i: (first + i, 0))
    sds = jax.ShapeDtypeStruct((r_dim, c), F32)
    in_specs = [spec, _row_spec(tb, c), spec, spec]
    operands = [w, g, m, v]
    aliases = {}
    if into is not None:
        in_specs += [_ANY] * n_out
        operands += list(into)
        aliases = {4 + i: i for i in range(n_out)}
    outs = _pcall(
        body, name=name, out_shape=(sds,) * n_out, grid=(rows // tb,),
        in_specs=in_specs, out_specs=(spec,) * n_out, input_output_aliases=aliases,
        compiler_params=_params("parallel"),
    )(*operands)
    return tuple(outs) if part is not None else tuple(outs) + (g,)


def _adamw_small(params, *, name):
    n = len(params)

    def body(*refs):
        ins, outs = refs[:4 * n], refs[4 * n:]
        for i in range(n):
            w_ref, g_ref, m_ref, v_ref = ins[4 * i:4 * i + 4]
            res = _adamw_math(w_ref[...], g_ref[...], m_ref[...], v_ref[...])
            for o_ref, r in zip(outs[3 * i:3 * i + 3], res):
                o_ref[...] = r

    vmem = pl.BlockSpec(memory_space=pltpu.VMEM)
    flat = [a for p in params for a in p]
    outs = _pcall(
        body, name=name,
        out_shape=tuple(jax.ShapeDtypeStruct(p[0].shape, F32) for p in params for _ in range(3)),
        in_specs=[vmem] * (4 * n), out_specs=(vmem,) * (3 * n),
    )(*flat)
    return [tuple(outs[3 * i:3 * i + 3]) for i in range(n)]


def _pair_sum(grad, recv, place, *, name):
    s_dim, r_dim, c = grad.shape
    half = r_dim // 2
    tb = 256 if half % 256 == 0 else half
    per_half = half // tb

    def body(place_ref, a_ref, b_ref, o16_ref, o32_ref):
        s = a_ref[...] + b_ref[...]
        o16_ref[...] = s.astype(BF16)

        @pl.when(pl.program_id(1) == place_ref[1])
        def _():
            o32_ref[...] = s[0]

    grid_spec = pltpu.PrefetchScalarGridSpec(
        num_scalar_prefetch=1, grid=(per_half, s_dim),
        in_specs=[pl.BlockSpec((1, tb, c), lambda i, s, p: (s, p[0] * per_half + i, 0)),
                  pl.BlockSpec((1, tb, c), lambda i, s, p: (s, i, 0))],
        out_specs=(pl.BlockSpec((1, tb, c), lambda i, s, p: (s, i, 0)),
                   pl.BlockSpec((tb, c), lambda i, s, p: (i, 0))))
    return _pcall(
        body, name=name, grid_spec=grid_spec,
        out_shape=(jax.ShapeDtypeStruct((s_dim, half, c), BF16), jax.ShapeDtypeStruct((half, c), F32)),
        compiler_params=_params("parallel", "arbitrary"),
    )(place, grad, recv)


def _chip_sum(own, recv, place, *, name):
    r_dim, c = own.shape
    tb = 256 if r_dim % 256 == 0 else r_dim

    def body(place_ref, a_ref, b_ref, o_ref):
        s = a_ref[...]
        for k in range(1, N_CHIPS):
            s = s + b_ref[k].astype(F32)
        o_ref[...] = s

    grid_spec = pltpu.PrefetchScalarGridSpec(
        num_scalar_prefetch=1, grid=(r_dim // tb,),
        in_specs=[pl.BlockSpec((tb, c), lambda i, p: (i, 0)),
                  pl.BlockSpec((N_CHIPS, tb, c), lambda i, p: (0, i, 0))],
        out_specs=pl.BlockSpec((None, tb, c), lambda i, p: (p[0], i, 0)))
    return _pcall(
        body, name=name, grid_spec=grid_spec, out_shape=jax.ShapeDtypeStruct((2, r_dim, c), F32),
        compiler_params=_params("parallel"),
    )(place, own, recv)


def _position():
    return lax.axis_index("x"), lax.axis_index("y"), lax.axis_index("c")


def _chip_peer(x, y, k):
    return x ^ (k >> 1), y ^ (k & 1)


_ANY = pl.BlockSpec(memory_space=pl.ANY)
_TOKEN = jax.ShapeDtypeStruct((8, 128), F32)


def _all_gather_weights(shards):
    n = len(shards)
    hops = N_CHIPS - 1

    def body(*refs):
        srcs, outs, done = refs[:n], refs[n:2 * n], refs[2 * n]
        send_sems, recv_sems = refs[2 * n + 1:]
        x, y, c = _position()
        me = 2 * x + y
        done[...] = jnp.zeros_like(done)

        def over_ici(w, k, chip, to):
            return pltpu.make_async_remote_copy(
                src_ref=srcs[w].at[c], dst_ref=outs[w].at[chip, c],
                send_sem=send_sems.at[w, k - 1], recv_sem=recv_sems.at[w, k - 1],
                device_id=to, device_id_type=MESH)

        def over_d2d(w, k, chip, half):
            return pltpu.make_async_remote_copy(
                src_ref=outs[w].at[chip, half], dst_ref=outs[w].at[chip, half],
                send_sem=send_sems.at[w, hops + k - 1], recv_sem=recv_sems.at[w, hops + k - 1],
                device_id=(x, y, 1 - c), device_id_type=MESH)

        sends = []
        for w in range(n):
            for k in range(1, N_CHIPS):
                px, py = _chip_peer(x, y, k)
                cp = over_ici(w, k, me, (px, py, c))
                cp.start()
                sends.append(cp)
        for w in range(n):
            for k in range(1, N_CHIPS):
                px, py = _chip_peer(x, y, k)
                over_ici(w, k, 2 * px + py, (px, py, c)).wait_recv()
                cp = over_d2d(w, k, 2 * px + py, c)
                cp.start()
                sends.append(cp)
        for w in range(n):
            for k in range(1, N_CHIPS):
                px, py = _chip_peer(x, y, k)
                over_d2d(w, k, 2 * px + py, 1 - c).wait_recv()
        for cp in sends:
            cp.wait_send()

    outs = _pcall(
        body, name="gather_weights",
        out_shape=tuple(jax.ShapeDtypeStruct((N_CHIPS,) + s.shape, s.dtype) for s in shards) + (_TOKEN,),
        in_specs=[_ANY] * n, out_specs=(_ANY,) * n + (pl.BlockSpec(memory_space=pltpu.VMEM),),
        scratch_shapes=[pltpu.SemaphoreType.DMA((n, 2 * hops)),
                        pltpu.SemaphoreType.DMA((n, 2 * hops))],
    )(*shards)
    return outs[:n], outs[n][0, 0]


def _pair_copies(srcs, lands, send_sems, recv_sems):
    x, y, c = _position()
    copies = []
    for w in range(len(srcs)):
        half = srcs[w].shape[1] // 2
        copies.append(pltpu.make_async_remote_copy(
            src_ref=srcs[w].at[:, pl.ds((1 - c) * half, half), :], dst_ref=lands[w],
            send_sem=send_sems.at[w], recv_sem=recv_sems.at[w],
            device_id=(x, y, 1 - c), device_id_type=MESH))
    return copies


def _chip_copies(srcs, lands, send_sems, recv_sems):
    x, y, c = _position()
    copies = []
    for w in range(len(srcs)):
        for k in range(1, N_CHIPS):
            px, py = _chip_peer(x, y, k)
            i = w * (N_CHIPS - 1) + k - 1
            copies.append(pltpu.make_async_remote_copy(
                src_ref=srcs[w].at[2 * px + py], dst_ref=lands[w].at[k],
                send_sem=send_sems.at[i], recv_sem=recv_sems.at[i],
                device_id=(px, py, c), device_id_type=MESH))
    return copies


def _gather_copies(srcs, lands, send_sems, recv_sems):
    x, y, c = _position()
    me = 2 * x + y
    copies = []
    for w in range(len(srcs)):
        for k in range(1, N_CHIPS):
            px, py = _chip_peer(x, y, k)
            i = w * (N_CHIPS - 1) + k - 1
            copies.append(pltpu.make_async_remote_copy(
                src_ref=srcs[w].at[c], dst_ref=lands[w].at[me, c],
                send_sem=send_sems.at[i], recv_sem=recv_sems.at[i],
                device_id=(px, py, c), device_id_type=MESH))
    return copies


def _exchange(name, copies_of, n_copies, srcs, land_shapes):
    n = len(srcs)

    def body(*refs):
        copies = copies_of(refs[:n], refs[n:2 * n], refs[2 * n], refs[2 * n + 1])
        for cp in copies:
            cp.start()
        for cp in copies:
            cp.wait_recv()
        for cp in copies:
            cp.wait_send()

    return _pcall(
        body, name=name, out_shape=tuple(land_shapes),
        in_specs=[_ANY] * n, out_specs=(_ANY,) * n,
        scratch_shapes=[pltpu.SemaphoreType.DMA((n_copies,)), pltpu.SemaphoreType.DMA((n_copies,))],
    )(*srcs)


_HBM = pl.BlockSpec(memory_space=pltpu.HBM)
_SEM = pl.BlockSpec(memory_space=pltpu.SEMAPHORE)
_DATAFLOW = pltpu.SideEffectType.DATAFLOW_SIDE_EFFECTING


def _exchange_start(name, copies_of, n_copies, srcs, land_shapes):
    n = len(srcs)
    lands = [lax.empty(s.shape, s.dtype) for s in land_shapes]

    def body(*refs):
        for cp in copies_of(refs[:n], refs[n:2 * n], refs[2 * n], refs[2 * n + 1]):
            cp.start()
        refs[-1][...] = jnp.zeros_like(refs[-1])

    through = [pltpu.HBM(a.shape, a.dtype) for a in list(srcs) + lands]
    outs = _pcall(
        body, name=name,
        out_shape=(pltpu.SemaphoreType.DMA((n_copies,)), pltpu.SemaphoreType.DMA((n_copies,)),
                   *through, jax.ShapeDtypeStruct((8, 128), F32)),
        in_specs=[_HBM] * (2 * n),
        out_specs=(_SEM, _SEM, *([_HBM] * (2 * n)), pl.BlockSpec(memory_space=pltpu.VMEM)),
        input_output_aliases={i: 2 + i for i in range(2 * n)},
        compiler_params=pltpu.CompilerParams(has_side_effects=_DATAFLOW),
    )(*[pltpu.with_memory_space_constraint(a, pltpu.HBM) for a in list(srcs) + lands])
    return outs[:-1], outs[-1][0, 0]


def _exchange_wait(name, copies_of, state, after):
    send_sems, recv_sems, through = state[0], state[1], state[2:]
    n = len(through) // 2
    if after.ndim == 0:
        after = jnp.broadcast_to(after, (8, 128))
    after = pltpu.with_memory_space_constraint(after, pltpu.HBM)

    def body(*refs):
        for cp in copies_of(refs[:n], refs[n:2 * n], refs[2 * n], refs[2 * n + 1]):
            cp.wait_send()
            cp.wait_recv()

    outs = _pcall(
        body, name=name,
        out_shape=tuple(pltpu.HBM(a.shape, a.dtype) for a in through),
        in_specs=[_HBM] * (2 * n) + [_SEM, _SEM, _HBM], out_specs=tuple([_HBM] * (2 * n)),
        input_output_aliases={i: i for i in range(2 * n)},
        compiler_params=pltpu.CompilerParams(has_side_effects=_DATAFLOW),
    )(*through, send_sems, recv_sems, after)
    return outs[:n], outs[n:]


def _forward_halves(lands, *, name):
    n = len(lands)
    hops = N_CHIPS - 1

    def body(*refs):
        ins, outs, done = refs[:n], refs[n:2 * n], refs[2 * n]
        send_sems, recv_sems = refs[2 * n + 1], refs[2 * n + 2]
        x, y, c = _position()
        done[...] = jnp.zeros_like(done)
        copies = []
        for w in range(n):
            for k in range(1, N_CHIPS):
                px, py = _chip_peer(x, y, k)
                i = w * hops + k - 1
                copies.append(pltpu.make_async_remote_copy(
                    src_ref=ins[w].at[2 * px + py, c], dst_ref=outs[w].at[2 * px + py, c],
                    send_sem=send_sems.at[i], recv_sem=recv_sems.at[i],
                    device_id=(x, y, 1 - c), device_id_type=MESH))
        for cp in copies:
            cp.start()
        for cp in copies:
            cp.wait_recv()
        for cp in copies:
            cp.wait_send()

    outs = _pcall(
        body, name=name,
        out_shape=tuple(jax.ShapeDtypeStruct(a.shape, a.dtype) for a in lands) + (_TOKEN,),
        in_specs=[_ANY] * n, out_specs=(_ANY,) * n + (pl.BlockSpec(memory_space=pltpu.VMEM),),
        input_output_aliases={i: i for i in range(n)},
        scratch_shapes=[pltpu.SemaphoreType.DMA((n * hops,)), pltpu.SemaphoreType.DMA((n * hops,))],
    )(*lands)
    return outs[:n], outs[n][0, 0]


def _pair_lands(grads):
    return [jax.ShapeDtypeStruct((g.shape[0], g.shape[1] // 2, g.shape[2]), F32) for g in grads]


def _same_lands(parts):
    return [jax.ShapeDtypeStruct(p.shape, p.dtype) for p in parts]


def _pair_gather_halves(halves, *, name):
    n = len(halves)

    def body(*refs):
        ins, outs = refs[:n], refs[n:2 * n]
        send_sems, recv_sems = refs[2 * n:]
        x, y, c = _position()
        sends = []
        for w in range(n):
            cp = pltpu.make_async_remote_copy(
                src_ref=ins[w].at[c], dst_ref=outs[w].at[c],
                send_sem=send_sems.at[w], recv_sem=recv_sems.at[w],
                device_id=(x, y, 1 - c), device_id_type=MESH)
            cp.start()
            sends.append(cp)
        for cp in sends:
            cp.wait_recv()
        for cp in sends:
            cp.wait_send()

    whole = _pcall(
        body, name=name,
        out_shape=tuple(jax.ShapeDtypeStruct(h.shape, F32) for h in halves),
        in_specs=[_ANY] * n, out_specs=(_ANY,) * n,
        input_output_aliases={i: i for i in range(n)},
        scratch_shapes=[pltpu.SemaphoreType.DMA((n,)), pltpu.SemaphoreType.DMA((n,))],
    )(*halves)
    return [w.reshape(2 * w.shape[1], w.shape[2]) for w in whole]


def _all_reduce_small(packed, *, name, sum_row0):
    r_dim, c = packed.shape

    def body(src_ref, out_ref, recv_ref, send_sems, recv_sems):
        x, y, c_ = _position()
        me = 4 * x + 2 * y + c_
        recv_ref[0] = src_ref[...]
        sends = []
        for k in range(1, N_DEV):
            peer = (x ^ (k >> 2), y ^ ((k >> 1) & 1), c_ ^ (k & 1))
            cp = pltpu.make_async_remote_copy(
                src_ref=src_ref, dst_ref=recv_ref.at[k],
                send_sem=send_sems.at[k - 1], recv_sem=recv_sems.at[k - 1],
                device_id=peer, device_id_type=MESH)
            cp.start()
            sends.append(cp)
        for cp in sends:
            cp.wait_recv()
        total = recv_ref[me]
        for d in range(1, N_DEV):
            total = total + recv_ref[d ^ me]
        if sum_row0:
            row0 = jnp.sum(total[0:1, :], axis=1, keepdims=True)
            rid = lax.broadcasted_iota(jnp.int32, total.shape, 0)
            total = jnp.where(rid == 0, row0, total)
        out_ref[...] = total
        for cp in sends:
            cp.wait_send()

    return _pcall(
        body, name=name, out_shape=jax.ShapeDtypeStruct((r_dim, c), F32),
        in_specs=[pl.BlockSpec(memory_space=pltpu.VMEM)],
        out_specs=pl.BlockSpec(memory_space=pltpu.VMEM),
        scratch_shapes=[pltpu.VMEM((N_DEV, r_dim, c), F32),
                        pltpu.SemaphoreType.DMA((N_DEV - 1,)), pltpu.SemaphoreType.DMA((N_DEV - 1,))],
    )(packed)


def _pad_lanes(v, width):
    return jnp.pad(v, ((0, 0), (0, width - v.shape[1])))


def _pad_rows(v, rows):
    pad = [(0, 0)] * v.ndim
    pad[-2] = (0, rows - v.shape[-2])
    return jnp.pad(v, pad)


_IN_PROJ_SHARD_ROWS = 1312


def _rows_1024(v):
    flat = v.reshape(-1)
    pad = (-flat.shape[0]) % D_MODEL
    return jnp.pad(flat, (0, pad)).reshape(-1, D_MODEL)


def _local_step(xs, target, pw, fetch, reduce_start, reduce_midway,
                conv_w, conv_b, gate_g,
                norm_mix_g, norm_mlp_g, pool_b, pool_scale, ssm_dt_bias, ssm_a_log, ssm_d, final_g):
    bias_r = _pad_lanes(ssm_dt_bias, 128)
    alog_r = _pad_lanes(ssm_a_log, 128)
    dskip_x = jnp.repeat(ssm_d, HEAD_DIM, axis=1)
    bias_c = ssm_dt_bias.reshape(N_HEADS, 1)
    alog_c = ssm_a_log.reshape(N_HEADS, 1)
    e_mat = _head_lane_matrix()
    e3_mat = jnp.tile(e_mat, (_EXACT_PIECES, 1))

    g_mix0, g_mix1 = norm_mix_g[0:1], norm_mix_g[1:2]
    g_mlp0, g_mlp1 = norm_mlp_g[0:1], norm_mlp_g[1:2]
    fg = final_g.reshape(1, D_MODEL)

    h1 = _pool_fwd(xs, g_mix0, pw, pool_b, pool_scale)
    w1_0 = fetch("mlp0_up", h1)
    u0, hm0 = _matmul(h1, w1_0, "nn", name="mlp0_up", out_dtype=BF16, b_col_shards=True, a_norm=g_mlp0)
    w2_0 = fetch("mlp0_down", u0)
    h2 = _matmul(u0, w2_0, "nn", name="mlp0_down", a_relu2=True, add=h1)

    w_z, w_xbc, w_dt = fetch("in_proj", h2)
    xbc, hn1 = _matmul(h2, w_xbc, "nt", name="in_proj_xbc", out_dtype=BF16, a_norm=g_mix1)
    z = _matmul(hn1, w_z, "nt", name="in_proj_z", out_dtype=BF16)
    dt_raw = _matmul(hn1, w_dt, "nt", name="in_proj_dt")
    dt_raw_t = dt_raw[:, :N_HEADS].T
    xc = _conv_fwd(xbc, conv_w, conv_b)
    wout, w1_1, w2_1 = fetch("rest", xc)
    dt_r, acs_r, sg_r, acs_t = _ssd_decay(dt_raw, dt_raw_t, bias_r, bias_c, alog_r, alog_c)
    y, states = _ssd_fwd(xc, dt_r, acs_r, acs_t, dskip_x, e3_mat)
    yn = _gate_fwd(y, z, gate_g)
    h3 = _matmul(yn, wout, "nn", name="out_proj", add=h2)
    u1, hm1 = _matmul(h3, w1_1, "nn", name="mlp1_up", out_dtype=BF16, b_col_shards=True, a_norm=g_mlp1)

    dh4, dh4_16, small_final = _matmul(u1, w2_1, "nn", name="mlp1_down", a_relu2=True, add=h3,
                                       loss_head=(fg, target))

    def mlp_bwd_weights(dh_out16, hm, u, w2_i, tag):
        du = _matmul(dh_out16, w2_i, "nt", name=tag + "_du", out_dtype=BF16, relu2_grad_of=u)
        dw2 = _matmul(u, dh_out16, "tn", name=tag + "_dw2", a_relu2=True)
        dw1 = _matmul(hm, du, "tn", name=tag + "_dw1", out_col_shards=N_CHIPS)
        return du, dw1, dw2.reshape(N_CHIPS, D_FF // N_CHIPS, D_MODEL)

    def mlp_bwd_input(du, dh_out, h_in, w1_i, g_i, tag):
        return _matmul(du, w1_i, "nt", name=tag + "_dhm", b_col_shards=True, norm_bwd=(h_in, g_i, dh_out))

    du1, dw1_1, dw2_1 = mlp_bwd_weights(dh4_16, hm1, u1, w2_1, "mlp1")
    dh3, dh3_16, dg_mlp1 = mlp_bwd_input(du1, dh4, h3, w1_1, g_mlp1, "mlp1")

    dyn = _matmul(dh3_16, wout, "nt", name="out_proj_dyn", out_dtype=BF16)
    dwout = _matmul(yn, dh3_16, "tn", name="out_proj_dw").reshape(N_CHIPS, D_INNER // N_CHIPS, D_MODEL)
    behind = reduce_start("mlp1_out", [dw1_1, dw2_1, dwout])
    dy, dz, dg_gate = _gate_bwd(dyn, y, z, gate_g + behind)
    behind = reduce_midway("mlp1_out", dz)
    dxc, ddt_raw, small_ssd = _ssd_bwd(xc, dt_r, acs_r, sg_r, acs_t, alog_r, dskip_x + behind,
                                       e3_mat, e_mat.T, states, dy)
    dv, dconv = _conv_bwd_act(xbc, dxc, conv_w, conv_b)
    dxbc = _conv_bwd_in(dv, conv_w)
    dw_z = _matmul(dz, hn1, "tn", name="in_proj_z_dw")
    dw_xbc = _matmul(dxbc, hn1, "tn", name="in_proj_xbc_dw")
    dw_dt = _matmul(ddt_raw, hn1, "tn", name="in_proj_dt_dw")
    dwin = jnp.concatenate([dw_z, dw_xbc, dw_dt[:N_HEADS]], axis=0)
    dwin = _pad_rows(dwin.reshape(N_CHIPS, IN_PROJ_DIM // N_CHIPS, D_MODEL), _IN_PROJ_SHARD_ROWS)
    behind = reduce_start("in_proj", [dwin])
    dhn1 = _matmul(ddt_raw, w_dt, "nn", name="in_proj_dt_dh")
    dhn1 = _matmul(dz, w_z, "nn", name="in_proj_z_dh", add=dhn1)
    dh2, dh2_16, dg_mix1 = _matmul(dxbc, w_xbc, "nn", name="in_proj_xbc_dh", add=dhn1,
                                   norm_bwd=(h2, g_mix1 + behind, dh3))
    behind = reduce_midway("in_proj", dh2_16)

    du0, dw1_0, dw2_0 = mlp_bwd_weights(dh2_16, hm0, u0, w2_0, "mlp0")
    dh1, _, dg_mlp0 = mlp_bwd_input(du0, dh2, h1, w1_0, g_mlp0 + behind, "mlp0")
    dx, dpw, small_pool = _pool_bwd(xs, g_mix0, pw, pool_b, pool_scale, dh1)
    dpw = jnp.transpose(dpw.reshape(4, N_CHIPS, POOL_GROUP // N_CHIPS, POOL_GROUP), (1, 0, 2, 3))
    dpw = dpw.reshape(N_CHIPS, 4 * (POOL_GROUP // N_CHIPS), POOL_GROUP)

    big = [dpw, dw1_0, dw2_0]
    rows = [
        small_final[1:2],
        small_final[0:1],
        small_pool[0:1], dg_mix1[0:1],
        dg_mlp0[0:1], dg_mlp1[0:1],
        small_pool[1:2], small_pool[2:3],
        _pad_lanes(small_ssd[0:3], D_MODEL),
        _rows_1024(dg_gate[0:1]),
        _rows_1024(dconv[0:CONV_K]),
        _rows_1024(dconv[CONV_K:CONV_K + 1]),
    ]
    return dx, big, rows


def kernel(x, norm_mix_g, norm_mlp_g, pool_w, pool_b, pool_scale, ssm_w_in, ssm_conv_w, ssm_conv_b, ssm_dt_bias, ssm_a_log, ssm_d, ssm_norm_g, ssm_w_out, mlp_w1, mlp_w2, final_g, loss_target, m_norm_mix_g, m_norm_mlp_g, m_pool_w, m_pool_b, m_pool_scale, m_ssm_w_in, m_ssm_conv_w, m_ssm_conv_b, m_ssm_dt_bias, m_ssm_a_log, m_ssm_d, m_ssm_norm_g, m_ssm_w_out, m_mlp_w1, m_mlp_w2, m_final_g, v_norm_mix_g, v_norm_mlp_g, v_pool_w, v_pool_b, v_pool_scale, v_ssm_w_in, v_ssm_conv_w, v_ssm_conv_b, v_ssm_dt_bias, v_ssm_a_log, v_ssm_d, v_ssm_norm_g, v_ssm_w_out, v_mlp_w1, v_mlp_w2, v_final_g):
    t_dim = x.shape[1]
    xs = x[0]
    target = loss_target[0]
    my_x, my_y, my_c = _position()
    my_chip = 2 * my_x + my_y

    def halves(w):
        return w.astype(BF16).reshape((2, w.shape[0] // 2) + w.shape[1:])

    def whole(gathered, own_shard):
        g = lax.dynamic_update_index_in_dim(gathered, own_shard, my_chip, axis=0)
        return g.reshape((N_CHIPS, 2 * g.shape[2]) + g.shape[3:])

    def gather_lands(own):
        return [jax.ShapeDtypeStruct((N_CHIPS,) + s.shape, s.dtype) for s in own]

    vec_cols = CONV_DIM // N_CHIPS
    vec_own = jnp.concatenate([ssm_conv_w[0], ssm_conv_b, _pad_lanes(ssm_norm_g, vec_cols)], axis=0)
    early_own = [halves(pool_w[0]), vec_own.reshape(2, (CONV_K + 2) // 2, vec_cols)]
    early, behind_early = _all_gather_weights(early_own)
    g_pool, g_vec = [whole(g, o) for g, o in zip(early, early_own)]
    pw = jnp.transpose(g_pool, (1, 0, 2, 3)).reshape(4, POOL_GROUP, POOL_GROUP)
    conv_w = jnp.transpose(g_vec[:, 0:CONV_K, :], (1, 0, 2)).reshape(CONV_K, CONV_DIM)
    conv_b = g_vec[:, CONV_K, :].reshape(1, CONV_DIM)
    gate_g = g_vec[:, CONV_K + 1, :D_INNER // N_CHIPS].reshape(1, D_INNER)

    def behind_it(zero, ws):
        return [halves(w + zero) for w in ws]

    fetches = {}
    up_own = behind_it(behind_early, [mlp_w1[0]])
    fetches["mlp0_up"], behind_gather = _exchange_start(
        "gather_mlp0_up_start", _gather_copies, len(up_own) * (N_CHIPS - 1), up_own, gather_lands(up_own))
    down_own = behind_it(behind_gather, [mlp_w2[0]])
    fetches["mlp0_down"], behind_gather = _exchange_start(
        "gather_mlp0_down_start", _gather_copies, len(down_own) * (N_CHIPS - 1), down_own, gather_lands(down_own))
    in_own = behind_it(behind_gather, [_pad_rows(ssm_w_in[0].T, _IN_PROJ_SHARD_ROWS)])
    fetches["in_proj"], behind_gather = _exchange_start(
        "gather_in_proj_start", _gather_copies, len(in_own) * (N_CHIPS - 1), in_own, gather_lands(in_own))

    def fetch(what, after):
        if what == "mlp0_up":
            own_thru, landed = _exchange_wait("gather_mlp0_up_wait", _gather_copies, fetches[what], after)
            landed, _ = _forward_halves(landed, name="forward_mlp0_up")
            return whole(landed[0], own_thru[0])
        if what == "mlp0_down":
            own_thru, landed = _exchange_wait("gather_mlp0_down_wait", _gather_copies, fetches[what], after)
            landed, _ = _forward_halves(landed, name="forward_mlp0_down")
            return whole(landed[0], own_thru[0]).reshape(D_FF, D_MODEL)
        if what == "in_proj":
            own_thru, landed = _exchange_wait("gather_in_proj_wait", _gather_copies, fetches["in_proj"], after)
            landed, behind = _forward_halves(landed, name="forward_in_proj")
            rest = behind_it(behind, [ssm_w_out[0], mlp_w1[1], mlp_w2[1]])
            fetches["rest"], behind = _exchange_start(
                "gather_rest_start", _gather_copies, len(rest) * (N_CHIPS - 1), rest, gather_lands(rest))
            win = whole(landed[0], own_thru[0])[:, :IN_PROJ_DIM // N_CHIPS].reshape(IN_PROJ_DIM, D_MODEL)
            w_dt = _pad_rows(win[D_INNER + CONV_DIM:], 128) + behind.astype(BF16)
            return win[:D_INNER], win[D_INNER:D_INNER + CONV_DIM], w_dt
        own_thru, landed = _exchange_wait("gather_rest_wait", _gather_copies, fetches["rest"], after)
        landed, _ = _forward_halves(landed, name="forward_rest")
        g_wout, w1_1, g_w2_1 = [whole(g, o) for g, o in zip(landed, own_thru)]
        return g_wout.reshape(D_INNER, D_MODEL), w1_1, g_w2_1.reshape(D_FF, D_MODEL)

    place = jnp.stack([my_c, my_chip]).astype(jnp.int32)
    waves = {}

    def reduce_start(wave, grads):
        waves[wave] = {}
        waves[wave]["pair"], behind = _exchange_start(
            "pair_%s_start" % wave, _pair_copies, len(grads), grads, _pair_lands(grads))
        return behind

    def reduce_midway(wave, after):
        st = waves[wave]
        grads, recv = _exchange_wait("pair_%s_wait" % wave, _pair_copies, st["pair"], after)
        sums = [_pair_sum(g, r, place, name="pair_sum_%s_%d" % (wave, i))
                for i, (g, r) in enumerate(zip(grads, recv))]
        st["f32"] = [s32 for _, s32 in sums]
        b16 = [s16 for s16, _ in sums]
        st["chip"], behind = _exchange_start(
            "chip_%s_start" % wave, _chip_copies, len(b16) * (N_CHIPS - 1), b16, _same_lands(b16))
        return behind

    def reduce_finish(wave, after):
        st = waves[wave]
        _, got = _exchange_wait("chip_%s_wait" % wave, _chip_copies, st["chip"], after)
        return [_chip_sum(s32, r, place, name="chip_sum_%s_%d" % (wave, i))
                for i, (s32, r) in enumerate(zip(st["f32"], got))]

    dx, big0, rows = _local_step(xs, target, pw, fetch, reduce_start, reduce_midway,
                                 conv_w, conv_b, gate_g,
                                 norm_mix_g + behind_gather, norm_mlp_g, pool_b, pool_scale,
                                 ssm_dt_bias, ssm_a_log, ssm_d, final_g)

    behind = reduce_start("layer0", big0)
    small = jnp.concatenate(rows, axis=0)
    small = jnp.pad(small, ((0, (-small.shape[0]) % 8), (0, 0))) + behind
    small = _all_reduce_small(small, name="all_reduce_small", sum_row0=True)
    behind = reduce_midway("layer0", small)
    h_w1_1, h_w2_1, h_wout = reduce_finish("mlp1_out", behind)
    (h_win,) = reduce_finish("in_proj", behind)
    g_w1_1, g_w2_1, g_wout_s, g_win_s = _pair_gather_halves([h_w1_1, h_w2_1, h_wout, h_win],
                                                            name="pair_gather_layer1")
    loss = small[0, 0]
    g_final = small[1]
    g_norm_mix = small[2:4]
    g_norm_mlp = small[4:6]
    g_pool_b, g_pool_scale = small[6:7], small[7:8]
    g_alog, g_dtb, g_dsk = small[8:9, :N_HEADS], small[9:10, :N_HEADS], small[10:11, :N_HEADS]
    g_gate_full = small[11:13].reshape(1, D_INNER)
    g_convw_full = small[13:25].reshape(CONV_K, CONV_DIM)
    g_convb_full = small[25:28].reshape(1, CONV_DIM)
    g_gate = lax.dynamic_slice_in_dim(g_gate_full, my_chip * (D_INNER // N_CHIPS), D_INNER // N_CHIPS, axis=1)
    g_convw = lax.dynamic_slice_in_dim(g_convw_full, my_chip * (CONV_DIM // N_CHIPS), CONV_DIM // N_CHIPS, axis=1)
    g_convb = lax.dynamic_slice_in_dim(g_convb_full, my_chip * (CONV_DIM // N_CHIPS), CONV_DIM // N_CHIPS, axis=1)

    grads = {
        "norm_mix_g": g_norm_mix, "norm_mlp_g": g_norm_mlp,
        "pool_b": g_pool_b, "pool_scale": g_pool_scale,
        "ssm_conv_w": g_convw.reshape(ssm_conv_w.shape),
        "ssm_conv_b": g_convb, "ssm_dt_bias": g_dtb, "ssm_a_log": g_alog, "ssm_d": g_dsk,
        "ssm_norm_g": g_gate, "ssm_w_out": g_wout_s.reshape(ssm_w_out.shape),
        "final_g": g_final,
    }
    weights = dict(norm_mix_g=norm_mix_g, norm_mlp_g=norm_mlp_g, pool_w=pool_w, pool_b=pool_b,
                   pool_scale=pool_scale, ssm_w_in=ssm_w_in, ssm_conv_w=ssm_conv_w, ssm_conv_b=ssm_conv_b,
                   ssm_dt_bias=ssm_dt_bias, ssm_a_log=ssm_a_log, ssm_d=ssm_d, ssm_norm_g=ssm_norm_g,
                   ssm_w_out=ssm_w_out, mlp_w1=mlp_w1, mlp_w2=mlp_w2, final_g=final_g)
    moms = dict(norm_mix_g=(m_norm_mix_g, v_norm_mix_g), norm_mlp_g=(m_norm_mlp_g, v_norm_mlp_g),
                pool_w=(m_pool_w, v_pool_w), pool_b=(m_pool_b, v_pool_b),
                pool_scale=(m_pool_scale, v_pool_scale), ssm_w_in=(m_ssm_w_in, v_ssm_w_in),
                ssm_conv_w=(m_ssm_conv_w, v_ssm_conv_w), ssm_conv_b=(m_ssm_conv_b, v_ssm_conv_b),
                ssm_dt_bias=(m_ssm_dt_bias, v_ssm_dt_bias), ssm_a_log=(m_ssm_a_log, v_ssm_a_log),
                ssm_d=(m_ssm_d, v_ssm_d), ssm_norm_g=(m_ssm_norm_g, v_ssm_norm_g),
                ssm_w_out=(m_ssm_w_out, v_ssm_w_out), mlp_w1=(m_mlp_w1, v_mlp_w1),
                mlp_w2=(m_mlp_w2, v_mlp_w2), final_g=(m_final_g, v_final_g))
    names = list(weights)
    big_names = ("pool_w", "ssm_w_in", "ssm_w_out", "mlp_w1", "mlp_w2")
    deltas, new_m, new_v = {}, {}, {}

    def as_rows(nm, a):
        return a[0].T if nm == "ssm_w_in" else a.reshape(-1, a.shape[-1])

    def from_rows(nm, r):
        return r.T[None] if nm == "ssm_w_in" else r.reshape(weights[nm].shape)

    def update(nm, grad_rows, layer=None, into=None):
        return _adamw(as_rows(nm, weights[nm]), grad_rows, as_rows(nm, moms[nm][0]), as_rows(nm, moms[nm][1]),
                      name="adamw_%s_%s" % (nm, layer), part=layer, into=into)

    def keep(nm, results):
        deltas[nm], new_m[nm], new_v[nm], grads[nm] = [from_rows(nm, r) for r in results]

    keep("ssm_w_in", update("ssm_w_in", g_win_s[:IN_PROJ_DIM // N_CHIPS]))
    keep("ssm_w_out", update("ssm_w_out", g_wout_s))
    w1_done = update("mlp_w1", g_w1_1, layer=1)
    w2_done = update("mlp_w2", g_w2_1, layer=1)
    small_names = [nm for nm in names if nm not in big_names]
    small_done = _adamw_small(
        [tuple(as_rows(nm, a) for a in (weights[nm], grads[nm], moms[nm][0], moms[nm][1])) for nm in small_names],
        name="adamw_small")
    for nm, (d_, m_, v_) in zip(small_names, small_done):
        deltas[nm], new_m[nm], new_v[nm] = [from_rows(nm, r) for r in (d_, m_, v_)]

    above = (deltas["ssm_w_in"][0, 0, 0] + deltas["ssm_w_out"][0, 0, 0] + w1_done[0][-1, -1]
             + w2_done[0][-1, -1] + small_done[0][0][0, 0])
    g_pool_w, g_w1_0, g_w2_0 = _pair_gather_halves(reduce_finish("layer0", above), name="pair_gather_layer0")
    keep("mlp_w1", update("mlp_w1", g_w1_0, layer=0, into=w1_done))
    keep("mlp_w2", update("mlp_w2", g_w2_0, layer=0, into=w2_done))
    keep("pool_w", update("pool_w", g_pool_w))

    grad_x = dx.reshape(x.shape)
    out_grads = [grads[nm].reshape(weights[nm].shape) for nm in names]
    return (loss, grad_x, *out_grads, *[deltas[nm] for nm in names],
            *[new_m[nm] for nm in names], *[new_v[nm] for nm in names])
```

```python
import functools

import jax
import jax.numpy as jnp
from jax import lax
from jax.experimental import pallas as pl
from jax.experimental.pallas import tpu as pltpu

F32 = jnp.float32
BF16 = jnp.bfloat16
MESH = pl.DeviceIdType.MESH

D_MODEL = 1024
RMS_EPS = 1e-5
POOL_WINDOWS = (2, 4, 8, 16)
POOL_GROUP = 256
POOL_HALO = 16
D_INNER = 2048
HEAD_DIM = 64
N_HEADS = 32
N_GROUPS = 4
HEADS_PER_GROUP = 8
D_STATE = 128
CONV_K = 4
CONV_HALO = 8
CHUNK = 128
CONV_DIM = 3072
IN_PROJ_DIM = 5152
D_FF = 4096
N_CHIPS = 4
N_DEV = 8

ADAM_LR = 0.001
ADAM_B1 = 0.9
ADAM_B2 = 0.999
ADAM_EPS = 1e-08
ADAM_WD = 0.01
ADAM_STEP = 10

VMEM_LIMIT = 56 * 1024 * 1024
NEG_INF = float("-inf")


def _pcall(body, **kw):
    return pl.pallas_call(body, **kw)


def _params(*sem):
    return pltpu.CompilerParams(dimension_semantics=sem, vmem_limit_bytes=VMEM_LIMIT)


def _sigmoid(v):
    return 1.0 / (1.0 + jnp.exp(-v))


def _row_spec(tb, d, nb=None, reverse=False):
    if reverse:
        return pl.BlockSpec((tb, d), lambda i: (nb - 1 - i, 0))
    return pl.BlockSpec((tb, d), lambda i: (i, 0))


def _const_spec(shape):
    return pl.BlockSpec(shape, lambda *_: tuple(0 for _ in shape))


_DIMS = {"nn": (((1,), (0,)), ((), ())),
         "nt": (((1,), (1,)), ((), ())),
         "tn": (((0,), (0,)), ((), ()))}


_MATMUL_VMEM_BUDGET = 40 * 1024 * 1024


def _matmul_tiles(m_dim, n_dim, k_dim, a_bytes, b_bytes, mn_bytes):
    tm, tn = min(m_dim, 1024), min(n_dim, 1024)
    while 2 * (tm * k_dim * a_bytes + tn * k_dim * b_bytes + tm * tn * mn_bytes) > _MATMUL_VMEM_BUDGET:
        if tm >= tn:
            tm //= 2
        else:
            tn //= 2
    return tm, tn


def _matmul(a, b, mode, *, name, out_dtype=F32, a_relu2=False, a_norm=None, add=None, relu2_grad_of=None,
            out_col_shards=1, b_col_shards=False, norm_bwd=None, loss_head=None, more=()):
    if mode == "tn":
        k_dim, m_dim = a.shape
    else:
        m_dim, k_dim = a.shape
    if b_col_shards:
        n_shards, shard_cols = b.shape[0], b.shape[2]
        n_dim = n_shards * shard_cols if mode == "nn" else b.shape[1]
    else:
        n_dim = b.shape[0] if mode == "nt" else b.shape[1]
    mn_bytes = jnp.dtype(out_dtype).itemsize
    if relu2_grad_of is not None:
        mn_bytes += relu2_grad_of.dtype.itemsize
    if add is not None:
        mn_bytes += add.dtype.itemsize
    row_epilogue = norm_bwd is not None or loss_head is not None
    if row_epilogue:
        assert out_dtype == F32 and out_col_shards == 1 and (norm_bwd is None or loss_head is None)
        mn_bytes += 4 + 4 + 2 + 8
    a_bytes = a.dtype.itemsize + (2 if a_norm is not None else 0)
    k_all = k_dim + sum(a_i.shape[1] for a_i, _ in more)
    tm, tn = _matmul_tiles(m_dim, n_dim, k_all, a_bytes, b.dtype.itemsize, mn_bytes)
    if row_epilogue:
        while tn < n_dim:
            tm, tn = tm // 2, tn * 2
    assert m_dim % tm == 0 and n_dim % tn == 0
    a_spec = (pl.BlockSpec((k_dim, tm), lambda i, j: (0, i)) if mode == "tn"
              else pl.BlockSpec((tm, k_dim), lambda i, j: (i, 0)))
    if b_col_shards and mode == "nn":
        assert shard_cols % tn == 0
        per_shard = shard_cols // tn
        b_spec = pl.BlockSpec((None, k_dim, tn), lambda i, j: (j // per_shard, 0, j % per_shard))
    elif b_col_shards:
        assert mode == "nt" and k_dim == n_shards * shard_cols
        b_spec = pl.BlockSpec((n_shards, tn, shard_cols), lambda i, j: (0, j, 0))
    else:
        b_spec = (pl.BlockSpec((tn, k_dim), lambda i, j: (j, 0)) if mode == "nt"
                  else pl.BlockSpec((k_dim, tn), lambda i, j: (0, j)))
    mn_spec = pl.BlockSpec((tm, tn), lambda i, j: (i, j))
    operands, in_specs = [a, b], [a_spec, b_spec]
    for a_i, b_i in more:
        assert mode == "nn" and a_i.shape[0] == m_dim and b_i.shape == (a_i.shape[1], n_dim)
        operands += [a_i, b_i]
        in_specs += [pl.BlockSpec((tm, a_i.shape[1]), lambda i, j: (i, 0)),
                     pl.BlockSpec((a_i.shape[1], tn), lambda i, j: (0, j))]
    if relu2_grad_of is not None:
        operands.append(relu2_grad_of)
        in_specs.append(mn_spec)
    if add is not None:
        operands.append(add)
        in_specs.append(mn_spec)
    gain_spec = pl.BlockSpec((1, n_dim), lambda i, j: (0, 0))
    if a_norm is not None:
        assert mode in ("nn", "nt") and a.dtype == F32 and not row_epilogue and out_col_shards == 1
        operands.append(a_norm)
        in_specs.append(pl.BlockSpec((1, k_dim), lambda i, j: (0, 0)))
    if norm_bwd is not None:
        h_in, g_in, dres_in = norm_bwd
        operands += [h_in, g_in, dres_in]
        in_specs += [mn_spec, gain_spec, mn_spec]
    if loss_head is not None:
        operands += list(loss_head)
        in_specs += [gain_spec, mn_spec]
    if row_epilogue:
        out_shape = (jax.ShapeDtypeStruct((m_dim, n_dim), F32), jax.ShapeDtypeStruct((m_dim, n_dim), BF16),
                     jax.ShapeDtypeStruct((8, n_dim), F32))
        out_spec = (mn_spec, mn_spec, pl.BlockSpec((8, n_dim), lambda i, j: (0, 0)))
    elif out_col_shards == 1:
        out_shape = jax.ShapeDtypeStruct((m_dim, n_dim), out_dtype)
        out_spec = mn_spec
    else:
        n_shard = n_dim // out_col_shards
        assert n_shard % tn == 0
        per = n_shard // tn
        out_shape = jax.ShapeDtypeStruct((out_col_shards, m_dim, n_shard), out_dtype)
        out_spec = pl.BlockSpec((None, tm, tn), lambda i, j: (j // per, i, j % per))
    if a_norm is not None:
        out_shape = (out_shape, jax.ShapeDtypeStruct((m_dim, k_dim), BF16))
        out_spec = (out_spec, pl.BlockSpec((tm, k_dim), lambda i, j: (i, 0)))

    n_in = len(operands)

    def body(*refs):
        a_ref, b_ref, o_ref = refs[0], refs[1], refs[n_in]
        if a_norm is not None:
            normed_ref = refs[n_in + 1]

            @pl.when(pl.program_id(1) == 0)
            def _():
                xa = a_ref[...]
                normed_ref[...] = (xa * _rms(xa) * refs[n_in - 1][...]).astype(BF16)

            av = normed_ref[...]
        else:
            av = a_ref[...]
        if a_relu2:
            av = jnp.maximum(av, 0)
            av = av * av
        if b_col_shards and mode == "nt":
            r = None
            for s in range(n_shards):
                part = lax.dot_general(av[:, s * shard_cols:(s + 1) * shard_cols].astype(BF16),
                                       b_ref[s].astype(BF16), _DIMS[mode], preferred_element_type=F32)
                r = part if r is None else r + part
        else:
            r = lax.dot_general(av.astype(BF16), b_ref[...].astype(BF16), _DIMS[mode],
                                preferred_element_type=F32)
        nxt = 2
        for _ in more:
            r = r + jnp.dot(refs[nxt][...].astype(BF16), refs[nxt + 1][...].astype(BF16),
                            preferred_element_type=F32)
            nxt += 2
        if relu2_grad_of is not None:
            r = r * (2.0 * jnp.maximum(refs[nxt][...].astype(F32), 0.0))
            nxt += 1
        if add is not None:
            r = r + refs[nxt][...]
            nxt += 1
        if not row_epilogue:
            o_ref[...] = r.astype(out_dtype)
            return
        dh16_ref, small_ref = refs[n_in + 1], refs[n_in + 2]

        @pl.when(pl.program_id(0) == 0)
        def _():
            small_ref[...] = jnp.zeros_like(small_ref)

        if norm_bwd is not None:
            h_ref, g_ref, dres_ref = refs[nxt:nxt + 3]
            x, dy = h_ref[...], r
        else:
            g_ref, t_ref = refs[nxt:nxt + 2]
            x = r
        rr = _rms(x)
        xhat = x * rr
        gv = g_ref[...]
        if loss_head is not None:
            err = xhat * gv - t_ref[...]
            small_ref[1:2, :] += (0.5 / n_dim) * jnp.sum(err * err, axis=0, keepdims=True)
            dy = err * (1.0 / n_dim)
        dxhat = dy * gv
        dh = rr * (dxhat - xhat * jnp.mean(dxhat * xhat, axis=-1, keepdims=True))
        if norm_bwd is not None:
            dh = dres_ref[...] + dh
        o_ref[...] = dh
        dh16_ref[...] = dh.astype(BF16)
        small_ref[0:1, :] += jnp.sum(dy * xhat, axis=0, keepdims=True)

    if row_epilogue:
        semantics = ("arbitrary", "arbitrary")
    elif a_norm is not None:
        semantics = ("parallel", "arbitrary")
    else:
        semantics = ("parallel", "parallel")
    return _pcall(
        body, name=name, out_shape=out_shape,
        grid=(m_dim // tm, n_dim // tn),
        in_specs=in_specs, out_specs=out_spec,
        compiler_params=_params(*semantics),
    )(*operands)


def _rms(x):
    return lax.rsqrt(jnp.mean(x * x, axis=-1, keepdims=True) + RMS_EPS)


def _pool_mixed(ext, hn, t0, tb):
    t = t0 + lax.broadcasted_iota(jnp.int32, (tb, 1), 0)
    parts = []
    for gi, w in enumerate(POOL_WINDOWS):
        lanes = slice(gi * POOL_GROUP, (gi + 1) * POOL_GROUP)
        s = ext[:, lanes]
        k = 1
        while k < w:
            s = s + pltpu.roll(s, k, 0)
            k *= 2
        cnt = jnp.minimum(t + 1, w).astype(F32)
        parts.append(s[POOL_HALO:, :] / cnt - hn[:, lanes])
    return parts


def _pool_fwd(x, g, pw, pb, ps, *, tb=512):
    t_dim, d = x.shape

    def body(x_ref, g_ref, pw_ref, pb_ref, ps_ref, o_ref, ext_ref):
        i = pl.program_id(0)

        @pl.when(i == 0)
        def _():
            ext_ref[0:POOL_HALO, :] = jnp.zeros((POOL_HALO, d), F32)

        xv = x_ref[...]
        hn = xv * _rms(xv) * g_ref[...]
        ext_ref[POOL_HALO:, :] = hn
        mixed = _pool_mixed(ext_ref[...], hn, i * tb, tb)
        for gi in range(len(POOL_WINDOWS)):
            lanes = slice(gi * POOL_GROUP, (gi + 1) * POOL_GROUP)
            out = jnp.dot(mixed[gi].astype(BF16), pw_ref[gi], preferred_element_type=F32)
            o_ref[:, lanes] = xv[:, lanes] + (out + pb_ref[:, lanes]) * ps_ref[:, lanes]
        ext_ref[0:POOL_HALO, :] = hn[tb - POOL_HALO:, :]

    return _pcall(
        body, name="pool_fwd", out_shape=jax.ShapeDtypeStruct((t_dim, d), F32),
        grid=(t_dim // tb,),
        in_specs=[_row_spec(tb, d), _const_spec((1, d)), _const_spec((4, POOL_GROUP, POOL_GROUP)),
                  _const_spec((1, d)), _const_spec((1, d))],
        out_specs=_row_spec(tb, d),
        scratch_shapes=[pltpu.VMEM((POOL_HALO + tb, d), F32)],
        compiler_params=_params("arbitrary"),
    )(x, g, pw, pb, ps)


def _pool_bwd(x, g, pw, pb, ps, dh1, *, tb=512):
    t_dim, d = x.shape
    nb = t_dim // tb
    halo_per_block = tb // POOL_HALO

    def body(x_ref, xprev_ref, g_ref, pw_ref, pb_ref, ps_ref, dh1_ref,
             dx_ref, dpw_ref, small_ref, ext_ref, dext_ref):
        i = pl.program_id(0)
        blk = nb - 1 - i

        @pl.when(i == 0)
        def _():
            dpw_ref[...] = jnp.zeros_like(dpw_ref)
            small_ref[...] = jnp.zeros_like(small_ref)
            dext_ref[tb:, :] = jnp.zeros((POOL_HALO, d), F32)

        gv = g_ref[...]
        xv = x_ref[...]
        r = _rms(xv)
        xhat = xv * r
        hn = xhat * gv
        xp = xprev_ref[...]
        hprev = xp * _rms(xp) * gv * (blk > 0).astype(F32)
        ext_ref[0:POOL_HALO, :] = hprev
        ext_ref[POOL_HALO:, :] = hn
        mixed = _pool_mixed(ext_ref[...], hn, blk * tb, tb)

        dout = dh1_ref[...]
        t = blk * tb + lax.broadcasted_iota(jnp.int32, (tb, 1), 0)
        for gi, w in enumerate(POOL_WINDOWS):
            lanes = slice(gi * POOL_GROUP, (gi + 1) * POOL_GROUP)
            mb = mixed[gi].astype(BF16)
            pre = jnp.dot(mb, pw_ref[gi], preferred_element_type=F32) + pb_ref[:, lanes]
            dg_out = dout[:, lanes]
            small_ref[2:3, lanes] += jnp.sum(dg_out * pre, axis=0, keepdims=True)
            dpre = dg_out * ps_ref[:, lanes]
            small_ref[1:2, lanes] += jnp.sum(dpre, axis=0, keepdims=True)
            dpb16 = dpre.astype(BF16)
            dpw_ref[gi] += lax.dot_general(mb, dpb16, _DIMS["tn"], preferred_element_type=F32)
            dmixed = lax.dot_general(dpb16, pw_ref[gi], _DIMS["nt"], preferred_element_type=F32)
            cnt = jnp.minimum(t + 1, w).astype(F32)
            dq = dmixed / cnt
            dext_ref[0:tb, lanes] = dq
            s = dext_ref[:, lanes]
            k = 1
            while k < w:
                s = s + pltpu.roll(s, tb + POOL_HALO - k, 0)
                k *= 2
            dhn = s[0:tb, :] - dmixed
            dext_ref[tb:, lanes] = dq[0:POOL_HALO, :]
            small_ref[0:1, lanes] += jnp.sum(dhn * xhat[:, lanes], axis=0, keepdims=True)
            ext_ref[POOL_HALO:, lanes] = dhn * gv[:, lanes]
        dxhat = ext_ref[POOL_HALO:, :]
        dx_ref[...] = dout + r * (dxhat - xhat * jnp.mean(dxhat * xhat, axis=-1, keepdims=True))

    return _pcall(
        body, name="pool_bwd",
        out_shape=(jax.ShapeDtypeStruct((t_dim, d), F32),
                   jax.ShapeDtypeStruct((4, POOL_GROUP, POOL_GROUP), F32),
                   jax.ShapeDtypeStruct((8, d), F32)),
        grid=(nb,),
        in_specs=[_row_spec(tb, d, nb, True),
                  pl.BlockSpec((POOL_HALO, d),
                               lambda i: (jnp.maximum((nb - 1 - i) * halo_per_block - 1, 0), 0)),
                  _const_spec((1, d)), _const_spec((4, POOL_GROUP, POOL_GROUP)),
                  _const_spec((1, d)), _const_spec((1, d)), _row_spec(tb, d, nb, True)],
        out_specs=(_row_spec(tb, d, nb, True), _const_spec((4, POOL_GROUP, POOL_GROUP)),
                   _const_spec((8, d))),
        scratch_shapes=[pltpu.VMEM((POOL_HALO + tb, d), F32), pltpu.VMEM((tb + POOL_HALO, d), F32)],
        compiler_params=_params("arbitrary"),
    )(x, x, g, pw, pb, ps, dh1)


_CONV_CB = 1024
_STRIP = 16


def _strips(tb, fn, unroll=4):
    def step(i, carry):
        fn(pl.multiple_of(i * _STRIP, _STRIP))
        return carry
    lax.fori_loop(0, tb // _STRIP, step, 0, unroll=unroll)


def _conv_taps(ext_ref, r0, w):
    shifted = [ext_ref[CONV_HALO + r0 - sh:CONV_HALO + r0 - sh + _STRIP, :] for sh in range(CONV_K)]
    acc = shifted[0] * w[CONV_K - 1:CONV_K, :]
    for sh in range(1, CONV_K):
        acc = acc + shifted[sh] * w[CONV_K - 1 - sh:CONV_K - sh, :]
    return shifted, acc


def _conv_fwd(u, w, b, *, tb=512):
    t_dim, c = u.shape
    cb = _CONV_CB

    def body(u_ref, w_ref, b_ref, o_ref, ext_ref):
        @pl.when(pl.program_id(1) == 0)
        def _():
            ext_ref[0:CONV_HALO, :] = jnp.zeros((CONV_HALO, cb), F32)

        wv = w_ref[...]
        bv = b_ref[...]

        def fill(r0):
            ext_ref[pl.ds(CONV_HALO + r0, _STRIP), :] = u_ref[pl.ds(r0, _STRIP), :].astype(F32)

        _strips(tb, fill)
        for r0 in range(0, tb, _STRIP):
            v = _conv_taps(ext_ref, r0, wv)[1] + bv
            o_ref[r0:r0 + _STRIP, :] = (v * _sigmoid(v)).astype(BF16)
        ext_ref[0:CONV_HALO, :] = ext_ref[tb:tb + CONV_HALO, :]

    blk = pl.BlockSpec((tb, cb), lambda j, t: (t, j))
    return _pcall(
        body, name="conv_fwd", out_shape=jax.ShapeDtypeStruct((t_dim, c), BF16),
        grid=(c // cb, t_dim // tb),
        in_specs=[blk, pl.BlockSpec((CONV_K, cb), lambda j, t: (0, j)),
                  pl.BlockSpec((1, cb), lambda j, t: (0, j))],
        out_specs=blk,
        scratch_shapes=[pltpu.VMEM((CONV_HALO + tb, cb), F32)],
        compiler_params=_params("parallel", "arbitrary"),
    )(u, w, b)


def _conv_bwd_act(u, dxc, w, b, *, tb=512):
    t_dim, c = u.shape
    cb = _CONV_CB
    half = _STRIP // 2

    def body(u_ref, d_ref, w_ref, b_ref, dv_ref, dwb_ref, ext_ref, acc_ref):
        @pl.when(pl.program_id(1) == 0)
        def _():
            ext_ref[0:CONV_HALO, :] = jnp.zeros((CONV_HALO, cb), F32)
            dwb_ref[...] = jnp.zeros_like(dwb_ref)

        acc_ref[...] = jnp.zeros_like(acc_ref)
        wv = w_ref[...]
        bv = b_ref[...]

        def fill(r0):
            ext_ref[pl.ds(CONV_HALO + r0, _STRIP), :] = u_ref[pl.ds(r0, _STRIP), :].astype(F32)

        _strips(tb, fill)
        for r0 in range(0, tb, _STRIP):
            shifted, v = _conv_taps(ext_ref, r0, wv)
            v = v + bv
            sg = _sigmoid(v)
            dv = d_ref[r0:r0 + _STRIP, :].astype(F32) * (sg * (1.0 + v * (1.0 - sg)))
            dv_ref[r0:r0 + _STRIP, :] = dv.astype(BF16)
            acc_ref[CONV_K] += dv[0:half, :] + dv[half:, :]
            for sh in range(CONV_K):
                p = dv * shifted[sh]
                acc_ref[CONV_K - 1 - sh] += p[0:half, :] + p[half:, :]
        for k in range(CONV_K + 1):
            dwb_ref[k:k + 1, :] += jnp.sum(acc_ref[k], axis=0, keepdims=True)
        ext_ref[0:CONV_HALO, :] = ext_ref[tb:tb + CONV_HALO, :]

    blk = pl.BlockSpec((tb, cb), lambda j, t: (t, j))
    return _pcall(
        body, name="conv_bwd_act",
        out_shape=(jax.ShapeDtypeStruct((t_dim, c), BF16), jax.ShapeDtypeStruct((8, c), F32)),
        grid=(c // cb, t_dim // tb),
        in_specs=[blk, blk, pl.BlockSpec((CONV_K, cb), lambda j, t: (0, j)),
                  pl.BlockSpec((1, cb), lambda j, t: (0, j))],
        out_specs=(blk, pl.BlockSpec((8, cb), lambda j, t: (0, j))),
        scratch_shapes=[pltpu.VMEM((CONV_HALO + tb, cb), F32), pltpu.VMEM((CONV_K + 1, half, cb), F32)],
        compiler_params=_params("parallel", "arbitrary"),
    )(u, dxc, w, b)


def _conv_bwd_in(dv, w, *, tb=512):
    t_dim, c = dv.shape
    cb = _CONV_CB
    nb = t_dim // tb

    def body(dv_ref, w_ref, du_ref, ext_ref):
        @pl.when(pl.program_id(1) == 0)
        def _():
            ext_ref[tb:, :] = jnp.zeros((CONV_HALO, cb), F32)

        wv = w_ref[...]

        def fill(r0):
            ext_ref[pl.ds(r0, _STRIP), :] = dv_ref[pl.ds(r0, _STRIP), :].astype(F32)

        _strips(tb, fill)
        for r0 in range(0, tb, _STRIP):
            acc = ext_ref[r0:r0 + _STRIP, :] * wv[CONV_K - 1:CONV_K, :]
            for sh in range(1, CONV_K):
                acc = acc + ext_ref[r0 + sh:r0 + sh + _STRIP, :] * wv[CONV_K - 1 - sh:CONV_K - sh, :]
            du_ref[r0:r0 + _STRIP, :] = acc.astype(BF16)
        ext_ref[tb:, :] = ext_ref[0:CONV_HALO, :]

    blk = pl.BlockSpec((tb, cb), lambda j, t: (nb - 1 - t, j))
    return _pcall(
        body, name="conv_bwd_in", out_shape=jax.ShapeDtypeStruct((t_dim, c), BF16),
        grid=(c // cb, nb),
        in_specs=[blk, pl.BlockSpec((CONV_K, cb), lambda j, t: (0, j))],
        out_specs=blk,
        scratch_shapes=[pltpu.VMEM((tb + CONV_HALO, cb), F32)],
        compiler_params=_params("parallel", "arbitrary"),
    )(dv, w)


def _softplus(v):
    e = jnp.exp(-jnp.abs(v))
    w = 1.0 + e
    log1p = jnp.where(w == 1.0, e, jnp.log(w) * e / jnp.where(w == 1.0, 1.0, w - 1.0))
    return jnp.maximum(v, 0.0) + log1p


def _cumsum_rows(v):
    row = lax.broadcasted_iota(jnp.int32, v.shape, 0) & (CHUNK - 1)
    k = 1
    while k < CHUNK:
        v = v + jnp.where(row >= k, pltpu.roll(v, k, 0), 0.0)
        k *= 2
    return v


def _cumsum_lanes(v):
    col = lax.broadcasted_iota(jnp.int32, v.shape, 1) & (CHUNK - 1)
    k = 1
    while k < CHUNK:
        v = v + jnp.where(col >= k, pltpu.roll(v, k, 1), 0.0)
        k *= 2
    return v


def _rev_cumsum_rows(v):
    row = lax.broadcasted_iota(jnp.int32, v.shape, 0)
    k = 1
    while k < CHUNK:
        v = v + jnp.where(row < CHUNK - k, pltpu.roll(v, CHUNK - k, 0), 0.0)
        k *= 2
    return v


PAIR = 2 * HEAD_DIM
GROUP_LANES = HEADS_PER_GROUP * HEAD_DIM


def _head_lane_matrix():
    h = lax.broadcasted_iota(jnp.int32, (128, D_INNER), 0)
    j = lax.broadcasted_iota(jnp.int32, (128, D_INNER), 1)
    return (j // HEAD_DIM == h).astype(BF16)


def _split_bf16(v, pieces):
    out = []
    for _ in range(pieces):
        p = v.astype(BF16)
        out.append(p)
        v = v - p.astype(F32)
    return out


_EXACT_PIECES = 3


def _expand_heads(values, e3):
    lhs = jnp.concatenate([jnp.concatenate(_split_bf16(v, _EXACT_PIECES), axis=1) for v in values], axis=0)
    out = jnp.dot(lhs, e3, preferred_element_type=F32)
    rows = values[0].shape[0]
    return [out[i * rows:(i + 1) * rows, :] for i in range(len(values))]


def _reduce_heads(v, et, pieces):
    return sum(jnp.dot(p, et, preferred_element_type=F32) for p in _split_bf16(v, pieces))


def _ssd_decay(dt_raw, dt_raw_t, bias_r, bias_c, alog_r, alog_c, *, tb=1024):
    t_dim = dt_raw.shape[0]
    tb = min(tb, t_dim)
    assert t_dim % tb == 0 and tb % CHUNK == 0

    def body(dtr_ref, dtt_ref, br_ref, bc_ref, ar_ref, ac_ref, dt_ref, acs_ref, sg_ref, acst_ref):
        pre = dtr_ref[...] + br_ref[...]
        dt = _softplus(pre)
        dt_ref[...] = dt
        sg_ref[...] = _sigmoid(pre)
        acs_ref[...] = _cumsum_rows(dt * (-jnp.exp(ar_ref[...])))
        acst_ref[...] = _cumsum_lanes(_softplus(dtt_ref[...] + bc_ref[...]) * (-jnp.exp(ac_ref[...])))

    rows = pl.BlockSpec((tb, 128), lambda i: (i, 0))
    cols = pl.BlockSpec((N_HEADS, tb), lambda i: (0, i))
    sds = jax.ShapeDtypeStruct((t_dim, 128), F32)
    return _pcall(
        body, name="ssd_decay",
        out_shape=(sds, sds, sds, jax.ShapeDtypeStruct((N_HEADS, t_dim), F32)),
        grid=(t_dim // tb,),
        in_specs=[rows, cols, _const_spec((1, 128)), _const_spec((N_HEADS, 1)),
                  _const_spec((1, 128)), _const_spec((N_HEADS, 1))],
        out_specs=(rows, rows, rows, cols), compiler_params=_params("parallel"),
    )(dt_raw, dt_raw_t, bias_r, bias_c, alog_r, alog_c)


def _pair_decay(acs_slab, acs_c, h0, causal, left):
    other = pltpu.roll(acs_slab, HEAD_DIM, 1)
    col0 = jnp.where(left, acs_slab, other)
    col1 = jnp.where(left, other, acs_slab)
    l0 = jnp.exp(jnp.where(causal, col0 - acs_c[h0:h0 + 1, :], NEG_INF))
    l1 = jnp.exp(jnp.where(causal, col1 - acs_c[h0 + 1:h0 + 2, :], NEG_INF))
    return l0, l1


def _ssd_fwd(xc, dt_r, acs_r, acs_t, dskip_x, e3_mat):
    t_dim = xc.shape[0]
    nc = t_dim // CHUNK

    def body(xc_ref, dt_ref, acs_ref, acst_ref, dk_ref, e3_ref, y_ref, st_ref, state):
        @pl.when(pl.program_id(0) == 0)
        def _():
            state[...] = jnp.zeros_like(state)

        dt, acs = _expand_heads([dt_ref[...], acs_ref[...]], e3_ref[...])
        acs_c = acst_ref[...]
        st_ref[0] = state[...]
        last = acs[CHUNK - 1:CHUNK, :]
        xs32 = xc_ref[:, 0:D_INNER].astype(F32)
        xdt = xs32 * dt
        xdt16 = xdt.astype(BF16)
        xdte16 = (xdt * jnp.exp(last - acs)).astype(BF16)
        ea = jnp.exp(acs)
        cd = jnp.exp(last)
        skip = dk_ref[...] * xs32
        causal = (lax.broadcasted_iota(jnp.int32, (CHUNK, CHUNK), 0)
                  >= lax.broadcasted_iota(jnp.int32, (CHUNK, CHUNK), 1))
        left = lax.broadcasted_iota(jnp.int32, (CHUNK, PAIR), 1) < HEAD_DIM
        for g in range(N_GROUPS):
            gl = slice(g * GROUP_LANES, (g + 1) * GROUP_LANES)
            bg = xc_ref[:, D_INNER + g * D_STATE:D_INNER + (g + 1) * D_STATE]
            cg = xc_ref[:, D_INNER + (N_GROUPS + g) * D_STATE:D_INNER + (N_GROUPS + g + 1) * D_STATE]
            cb = lax.dot_general(cg, bg, _DIMS["nt"], preferred_element_type=F32)
            hprev = state[:, gl]
            ch = jnp.dot(cg, hprev.astype(BF16), preferred_element_type=F32)
            for j in range(HEADS_PER_GROUP // 2):
                pl_ = slice(g * GROUP_LANES + j * PAIR, g * GROUP_LANES + (j + 1) * PAIR)
                h0 = g * HEADS_PER_GROUP + 2 * j
                l0, l1 = _pair_decay(acs[:, pl_], acs_c, h0, causal, left)
                lhs = jnp.concatenate([(cb * l0).astype(BF16), (cb * l1).astype(BF16)], axis=1)
                xp = xdt16[:, pl_]
                zero = jnp.zeros_like(xp)
                rhs = jnp.concatenate([jnp.where(left, xp, zero), jnp.where(left, zero, xp)], axis=0)
                ydiag = jnp.dot(lhs, rhs, preferred_element_type=F32)
                y_ref[:, pl_] = (ydiag + ch[:, j * PAIR:(j + 1) * PAIR] * ea[:, pl_] + skip[:, pl_]).astype(BF16)
            s_new = lax.dot_general(bg, xdte16[:, gl], _DIMS["tn"], preferred_element_type=F32)
            state[:, gl] = hprev * cd[:, gl] + s_new

    rows = lambda w: pl.BlockSpec((CHUNK, w), lambda c: (c, 0))
    return _pcall(
        body, name="ssd_fwd",
        out_shape=(jax.ShapeDtypeStruct((t_dim, D_INNER), BF16),
                   jax.ShapeDtypeStruct((nc, D_STATE, D_INNER), F32)),
        grid=(nc,),
        in_specs=[rows(CONV_DIM), rows(128), rows(128), pl.BlockSpec((N_HEADS, CHUNK), lambda c: (0, c)),
                  _const_spec((1, D_INNER)), _const_spec((_EXACT_PIECES * 128, D_INNER))],
        out_specs=(rows(D_INNER), pl.BlockSpec((1, D_STATE, D_INNER), lambda c: (c, 0, 0))),
        scratch_shapes=[pltpu.VMEM((D_STATE, D_INNER), F32)],
        compiler_params=_params("arbitrary"),
    )(xc, dt_r, acs_r, acs_t, dskip_x, e3_mat)


def _ssd_bwd(xc, dt_r, acs_r, sg_r, acs_t, alog_r, dskip_x, e3_mat, et_mat, states, dy):
    t_dim = xc.shape[0]
    nc = t_dim // CHUNK

    def body(xc_ref, dt_ref, acs_ref, sg_ref, acst_ref, ar_ref, dk_ref, e3_ref, et_ref, st_ref, dy_ref,
             dxc_ref, ddt_ref, small_ref, dstate, dacs_ref, dxdt_ref, acc_x, acc_r):
        step = pl.program_id(0)

        @pl.when(step == 0)
        def _():
            dstate[...] = jnp.zeros_like(dstate)
            acc_x[...] = jnp.zeros_like(acc_x)
            acc_r[...] = jnp.zeros_like(acc_r)

        dt_r = dt_ref[...]
        a_r = -jnp.exp(ar_ref[...])
        dt, acs = _expand_heads([dt_r, acs_ref[...]], e3_ref[...])
        acs_c = acst_ref[...]
        last = acs[CHUNK - 1:CHUNK, :]
        xs32 = xc_ref[:, 0:D_INNER].astype(F32)
        xdt = xs32 * dt
        xdt16 = xdt.astype(BF16)
        dte = jnp.exp(last - acs)
        xdte = xdt * dte
        xdte16 = xdte.astype(BF16)
        cd = jnp.exp(last)
        dy16 = dy_ref[...]
        dyv = dy16.astype(F32)
        dye = dyv * jnp.exp(acs)
        dye16 = dye.astype(BF16)
        causal = (lax.broadcasted_iota(jnp.int32, (CHUNK, CHUNK), 0)
                  >= lax.broadcasted_iota(jnp.int32, (CHUNK, CHUNK), 1))
        left = lax.broadcasted_iota(jnp.int32, (CHUNK, PAIR), 1) < HEAD_DIM
        lane_id = lax.broadcasted_iota(jnp.int32, (CHUNK, 128), 1)
        row_id = lax.broadcasted_iota(jnp.int32, (CHUNK, 128), 0)
        is_last_row = lax.broadcasted_iota(jnp.int32, (CHUNK, 1), 0) == CHUNK - 1
        dacs_cols = jnp.zeros((CHUNK, 128), F32)
        dacs_rows = jnp.zeros((CHUNK, 128), F32)
        for g in range(N_GROUPS):
            gl = slice(g * GROUP_LANES, (g + 1) * GROUP_LANES)
            b_lanes = slice(D_INNER + g * D_STATE, D_INNER + (g + 1) * D_STATE)
            c_lanes = slice(D_INNER + (N_GROUPS + g) * D_STATE, D_INNER + (N_GROUPS + g + 1) * D_STATE)
            bg = xc_ref[:, b_lanes]
            cg = xc_ref[:, c_lanes]
            cb = lax.dot_general(cg, bg, _DIMS["nt"], preferred_element_type=F32)
            hprev = st_ref[0, :, gl]
            hp16 = hprev.astype(BF16)
            dhn = dstate[:, gl]
            dhn16 = dhn.astype(BF16)
            ch = jnp.dot(cg, hp16, preferred_element_type=F32)
            gmat = jnp.dot(bg, dhn16, preferred_element_type=F32)
            gx = gmat * xdte[:, gl]
            dlast = jnp.sum(gx, axis=0, keepdims=True) + cd[:, gl] * jnp.sum(dhn * hprev, axis=0, keepdims=True)
            dacs_ref[:, gl] = dye[:, gl] * ch - gx + jnp.where(is_last_row, dlast, 0.0)
            dc_acc = lax.dot_general(dye16[:, gl], hp16, _DIMS["nt"], preferred_element_type=F32)
            db_acc = lax.dot_general(xdte16[:, gl], dhn16, _DIMS["nt"], preferred_element_type=F32)
            dstate[:, gl] = dhn * cd[:, gl] + lax.dot_general(cg, dye16[:, gl], _DIMS["tn"],
                                                             preferred_element_type=F32)
            dcb = jnp.zeros((CHUNK, CHUNK), F32)
            for j in range(HEADS_PER_GROUP // 2):
                pl_ = slice(g * GROUP_LANES + j * PAIR, g * GROUP_LANES + (j + 1) * PAIR)
                h0 = g * HEADS_PER_GROUP + 2 * j
                l0, l1 = _pair_decay(acs[:, pl_], acs_c, h0, causal, left)
                m0, m1 = cb * l0, cb * l1
                lhs = jnp.concatenate([m0.astype(BF16), m1.astype(BF16)], axis=1)
                dyp = dy16[:, pl_]
                zero = jnp.zeros_like(dyp)
                both = lax.dot_general(lhs, dyp, _DIMS["tn"], preferred_element_type=F32)
                dxdt_ref[:, pl_] = (jnp.where(left, both[0:CHUNK, :], both[CHUNK:, :])
                                    + gmat[:, j * PAIR:(j + 1) * PAIR] * dte[:, pl_])
                lhs2 = jnp.concatenate([jnp.where(left, dyp, zero), jnp.where(left, zero, dyp)], axis=0)
                dm = lax.dot_general(lhs2, xdt16[:, pl_], _DIMS["nt"], preferred_element_type=F32)
                dm0, dm1 = dm[0:CHUNK, :], dm[CHUNK:, :]
                dcb = dcb + dm0 * l0 + dm1 * l1
                ds0, ds1 = dm0 * m0, dm1 * m1
                dacs_cols = jnp.where(lane_id == h0, jnp.sum(ds0, axis=1, keepdims=True), dacs_cols)
                dacs_cols = jnp.where(lane_id == h0 + 1, jnp.sum(ds1, axis=1, keepdims=True), dacs_cols)
                dacs_rows = jnp.where(row_id == h0, jnp.sum(ds0, axis=0, keepdims=True), dacs_rows)
                dacs_rows = jnp.where(row_id == h0 + 1, jnp.sum(ds1, axis=0, keepdims=True), dacs_rows)
            dcb16 = dcb.astype(BF16)
            dxc_ref[:, c_lanes] = (dc_acc + jnp.dot(dcb16, bg, preferred_element_type=F32)).astype(BF16)
            dxc_ref[:, b_lanes] = (db_acc + lax.dot_general(dcb16, cg, _DIMS["tn"],
                                                           preferred_element_type=F32)).astype(BF16)
        dxdt = dxdt_ref[...]
        dxc_ref[:, 0:D_INNER] = (dxdt * dt + dk_ref[...] * dyv).astype(BF16)
        acc_x[0:1, :] += jnp.sum(dyv * xs32, axis=0, keepdims=True)
        et = et_ref[...]
        dacs = _reduce_heads(dacs_ref[...], et, 2) + dacs_cols - dacs_rows.T
        dadt = _rev_cumsum_rows(dacs)
        ddraw = (_reduce_heads(dxdt * xs32, et, 1) + dadt * a_r) * sg_ref[...]
        ddraw = jnp.where(lane_id < N_HEADS, ddraw, 0.0)
        ddt_ref[...] = ddraw
        acc_r[0:1, :] += jnp.where(lane_id[0:1, :] < N_HEADS,
                                   jnp.sum(dadt * dt_r, axis=0, keepdims=True) * a_r, 0.0)
        acc_r[1:2, :] += jnp.sum(ddraw, axis=0, keepdims=True)

        @pl.when(step == nc - 1)
        def _():
            dd = _reduce_heads(acc_x[...], et_ref[...], 3)
            rid = lax.broadcasted_iota(jnp.int32, (8, 128), 0)
            small_ref[...] = acc_r[...] + jnp.where(rid == 2, pltpu.roll(dd, 2, 0), 0.0)

    rev = lambda w: pl.BlockSpec((CHUNK, w), lambda c: (nc - 1 - c, 0))
    return _pcall(
        body, name="ssd_bwd",
        out_shape=(jax.ShapeDtypeStruct((t_dim, CONV_DIM), BF16),
                   jax.ShapeDtypeStruct((t_dim, 128), F32),
                   jax.ShapeDtypeStruct((8, 128), F32)),
        grid=(nc,),
        in_specs=[rev(CONV_DIM), rev(128), rev(128), rev(128),
                  pl.BlockSpec((N_HEADS, CHUNK), lambda c: (0, nc - 1 - c)),
                  _const_spec((1, 128)), _const_spec((1, D_INNER)),
                  _const_spec((_EXACT_PIECES * 128, D_INNER)), _const_spec((D_INNER, 128)),
                  pl.BlockSpec((1, D_STATE, D_INNER), lambda c: (nc - 1 - c, 0, 0)),
                  rev(D_INNER)],
        out_specs=(rev(CONV_DIM), rev(128), _const_spec((8, 128))),
        scratch_shapes=[pltpu.VMEM((D_STATE, D_INNER), F32), pltpu.VMEM((CHUNK, D_INNER), F32),
                        pltpu.VMEM((CHUNK, D_INNER), F32), pltpu.VMEM((8, D_INNER), F32),
                        pltpu.VMEM((8, 128), F32)],
        compiler_params=_params("arbitrary"),
    )(xc, dt_r, acs_r, sg_r, acs_t, alog_r, dskip_x, e3_mat, et_mat, states, dy)


_GATE_GROUP = D_INNER // N_GROUPS


def _gate_fwd(y, z, g, *, tb=256):
    t_dim = y.shape[0]

    def body(y_ref, z_ref, g_ref, o_ref):
        for gi in range(N_GROUPS):
            lanes = slice(gi * _GATE_GROUP, (gi + 1) * _GATE_GROUP)
            zv = z_ref[:, lanes].astype(F32)
            wv = y_ref[:, lanes].astype(F32) * (zv * _sigmoid(zv))
            o_ref[:, lanes] = (wv * _rms(wv) * g_ref[:, lanes]).astype(BF16)

    return _pcall(
        body, name="gate_fwd", out_shape=jax.ShapeDtypeStruct((t_dim, D_INNER), BF16),
        grid=(t_dim // tb,),
        in_specs=[_row_spec(tb, D_INNER), _row_spec(tb, D_INNER), _const_spec((1, D_INNER))],
        out_specs=_row_spec(tb, D_INNER), compiler_params=_params("parallel"),
    )(y, z, g)


def _gate_bwd(dyn, y, z, g, *, tb=256):
    t_dim = y.shape[0]

    def body(d_ref, y_ref, z_ref, g_ref, dy_ref, dz_ref, dg_ref):
        @pl.when(pl.program_id(0) == 0)
        def _():
            dg_ref[...] = jnp.zeros_like(dg_ref)

        for gi in range(N_GROUPS):
            lanes = slice(gi * _GATE_GROUP, (gi + 1) * _GATE_GROUP)
            zv = z_ref[:, lanes].astype(F32)
            sg = _sigmoid(zv)
            sz = zv * sg
            yv = y_ref[:, lanes].astype(F32)
            wv = yv * sz
            r = _rms(wv)
            what = wv * r
            dv = d_ref[:, lanes].astype(F32)
            dwhat = dv * g_ref[:, lanes]
            dw = r * (dwhat - what * jnp.mean(dwhat * what, axis=-1, keepdims=True))
            dg_ref[0:1, lanes] += jnp.sum(dv * what, axis=0, keepdims=True)
            dy_ref[:, lanes] = (dw * sz).astype(BF16)
            dz_ref[:, lanes] = (dw * yv * (sg * (1.0 + zv * (1.0 - sg)))).astype(BF16)

    return _pcall(
        body, name="gate_bwd",
        out_shape=(jax.ShapeDtypeStruct((t_dim, D_INNER), BF16),
                   jax.ShapeDtypeStruct((t_dim, D_INNER), BF16),
                   jax.ShapeDtypeStruct((8, D_INNER), F32)),
        grid=(t_dim // tb,),
        in_specs=[_row_spec(tb, D_INNER), _row_spec(tb, D_INNER), _row_spec(tb, D_INNER),
                  _const_spec((1, D_INNER))],
        out_specs=(_row_spec(tb, D_INNER), _row_spec(tb, D_INNER), _const_spec((8, D_INNER))),
        compiler_params=_params("arbitrary"),
    )(dyn, y, z, g)


_ADAM_C1 = 1.0 / (1.0 - ADAM_B1 ** ADAM_STEP)
_ADAM_C2 = 1.0 / (1.0 - ADAM_B2 ** ADAM_STEP)


def _adamw_math(w, g, m, v):
    mn = ADAM_B1 * m + (1.0 - ADAM_B1) * g
    vn = ADAM_B2 * v + (1.0 - ADAM_B2) * (g * g)
    delta = -ADAM_LR * ((mn * _ADAM_C1) / (jnp.sqrt(vn * _ADAM_C2) + ADAM_EPS) + ADAM_WD * w)
    return delta, mn, vn


def _adamw(w, g, m, v, *, name, part=None, into=None):
    r_dim, c = w.shape
    rows = r_dim if part is None else r_dim // 2
    assert g.shape == (rows, c)
    tb = max(t for t in range(8, 513, 8) if rows % t == 0)
    first = 0 if part is None else part * (rows // tb)
    n_out = 3 if part is None else 4

    def body(w_ref, g_ref, m_ref, v_ref, *rest):
        outs = rest[-n_out:]
        gv = g_ref[...]
        outs[0][...], outs[1][...], outs[2][...] = _adamw_math(w_ref[...], gv, m_ref[...], v_ref[...])
        if part is not None:
            outs[3][...] = gv

    spec = pl.BlockSpec((tb, c), lambda i: (first + i, 0))
    sds = jax.ShapeDtypeStruct((r_dim, c), F32)
    in_specs = [spec, _row_spec(tb, c), spec, spec]
    operands = [w, g, m, v]
    aliases = {}
    if into is not None:
        in_specs += [_ANY] * n_out
        operands += list(into)
        aliases = {4 + i: i for i in range(n_out)}
    outs = _pcall(
        body, name=name, out_shape=(sds,) * n_out, grid=(rows // tb,),
        in_specs=in_specs, out_specs=(spec,) * n_out, input_output_aliases=aliases,
        compiler_params=_params("parallel"),
    )(*operands)
    return tuple(outs) if part is not None else tuple(outs) + (g,)


def _adamw_small(params, *, name):
    n = len(params)

    def body(*refs):
        ins, outs = refs[:4 * n], refs[4 * n:]
        for i in range(n):
            w_ref, g_ref, m_ref, v_ref = ins[4 * i:4 * i + 4]
            res = _adamw_math(w_ref[...], g_ref[...], m_ref[...], v_ref[...])
            for o_ref, r in zip(outs[3 * i:3 * i + 3], res):
                o_ref[...] = r

    vmem = pl.BlockSpec(memory_space=pltpu.VMEM)
    flat = [a for p in params for a in p]
    outs = _pcall(
        body, name=name,
        out_shape=tuple(jax.ShapeDtypeStruct(p[0].shape, F32) for p in params for _ in range(3)),
        in_specs=[vmem] * (4 * n), out_specs=(vmem,) * (3 * n),
    )(*flat)
    return [tuple(outs[3 * i:3 * i + 3]) for i in range(n)]


def _pair_sum(grad, recv, place, *, name):
    s_dim, r_dim, c = grad.shape
    half = r_dim // 2
    tb = 256 if half % 256 == 0 else half
    per_half = half // tb

    def body(place_ref, a_ref, b_ref, o16_ref, o32_ref):
        s = a_ref[...] + b_ref[...]
        o16_ref[...] = s.astype(BF16)

        @pl.when(pl.program_id(1) == place_ref[1])
        def _():
            o32_ref[...] = s[0]

    grid_spec = pltpu.PrefetchScalarGridSpec(
        num_scalar_prefetch=1, grid=(per_half, s_dim),
        in_specs=[pl.BlockSpec((1, tb, c), lambda i, s, p: (s, p[0] * per_half + i, 0)),
                  pl.BlockSpec((1, tb, c), lambda i, s, p: (s, i, 0))],
        out_specs=(pl.BlockSpec((1, tb, c), lambda i, s, p: (s, i, 0)),
                   pl.BlockSpec((tb, c), lambda i, s, p: (i, 0))))
    return _pcall(
        body, name=name, grid_spec=grid_spec,
        out_shape=(jax.ShapeDtypeStruct((s_dim, half, c), BF16), jax.ShapeDtypeStruct((half, c), F32)),
        compiler_params=_params("parallel", "arbitrary"),
    )(place, grad, recv)


def _chip_sum(own, recv, place, *, name):
    r_dim, c = own.shape
    tb = 256 if r_dim % 256 == 0 else r_dim

    def body(place_ref, a_ref, b_ref, o_ref):
        s = a_ref[...]
        for k in range(1, N_CHIPS):
            s = s + b_ref[k].astype(F32)
        o_ref[...] = s

    grid_spec = pltpu.PrefetchScalarGridSpec(
        num_scalar_prefetch=1, grid=(r_dim // tb,),
        in_specs=[pl.BlockSpec((tb, c), lambda i, p: (i, 0)),
                  pl.BlockSpec((N_CHIPS, tb, c), lambda i, p: (0, i, 0))],
        out_specs=pl.BlockSpec((None, tb, c), lambda i, p: (p[0], i, 0)))
    return _pcall(
        body, name=name, grid_spec=grid_spec, out_shape=jax.ShapeDtypeStruct((2, r_dim, c), F32),
        compiler_params=_params("parallel"),
    )(place, own, recv)


def _position():
    return lax.axis_index("x"), lax.axis_index("y"), lax.axis_index("c")


def _chip_peer(x, y, k):
    return x ^ (k >> 1), y ^ (k & 1)


_ANY = pl.BlockSpec(memory_space=pl.ANY)
_TOKEN = jax.ShapeDtypeStruct((8, 128), F32)


def _all_gather_weights(shards):
    n = len(shards)
    hops = N_CHIPS - 1

    def body(*refs):
        srcs, outs, done = refs[:n], refs[n:2 * n], refs[2 * n]
        send_sems, recv_sems = refs[2 * n + 1:]
        x, y, c = _position()
        me = 2 * x + y
        done[...] = jnp.zeros_like(done)

        def over_ici(w, k, chip, to):
            return pltpu.make_async_remote_copy(
                src_ref=srcs[w].at[c], dst_ref=outs[w].at[chip, c],
                send_sem=send_sems.at[w, k - 1], recv_sem=recv_sems.at[w, k - 1],
                device_id=to, device_id_type=MESH)

        def over_d2d(w, k, chip, half):
            return pltpu.make_async_remote_copy(
                src_ref=outs[w].at[chip, half], dst_ref=outs[w].at[chip, half],
                send_sem=send_sems.at[w, hops + k - 1], recv_sem=recv_sems.at[w, hops + k - 1],
                device_id=(x, y, 1 - c), device_id_type=MESH)

        sends = []
        for w in range(n):
            for k in range(1, N_CHIPS):
                px, py = _chip_peer(x, y, k)
                cp = over_ici(w, k, me, (px, py, c))
                cp.start()
                sends.append(cp)
        for w in range(n):
            for k in range(1, N_CHIPS):
                px, py = _chip_peer(x, y, k)
                over_ici(w, k, 2 * px + py, (px, py, c)).wait_recv()
                cp = over_d2d(w, k, 2 * px + py, c)
                cp.start()
                sends.append(cp)
        for w in range(n):
            for k in range(1, N_CHIPS):
                px, py = _chip_peer(x, y, k)
                over_d2d(w, k, 2 * px + py, 1 - c).wait_recv()
        for cp in sends:
            cp.wait_send()

    outs = _pcall(
        body, name="gather_weights",
        out_shape=tuple(jax.ShapeDtypeStruct((N_CHIPS,) + s.shape, s.dtype) for s in shards) + (_TOKEN,),
        in_specs=[_ANY] * n, out_specs=(_ANY,) * n + (pl.BlockSpec(memory_space=pltpu.VMEM),),
        scratch_shapes=[pltpu.SemaphoreType.DMA((n, 2 * hops)),
                        pltpu.SemaphoreType.DMA((n, 2 * hops))],
    )(*shards)
    return outs[:n], outs[n][0, 0]


def _pair_copies(srcs, lands, send_sems, recv_sems):
    x, y, c = _position()
    copies = []
    for w in range(len(srcs)):
        half = srcs[w].shape[1] // 2
        copies.append(pltpu.make_async_remote_copy(
            src_ref=srcs[w].at[:, pl.ds((1 - c) * half, half), :], dst_ref=lands[w],
            send_sem=send_sems.at[w], recv_sem=recv_sems.at[w],
            device_id=(x, y, 1 - c), device_id_type=MESH))
    return copies


def _chip_copies(srcs, lands, send_sems, recv_sems):
    x, y, c = _position()
    copies = []
    for w in range(len(srcs)):
        for k in range(1, N_CHIPS):
            px, py = _chip_peer(x, y, k)
            i = w * (N_CHIPS - 1) + k - 1
            copies.append(pltpu.make_async_remote_copy(
                src_ref=srcs[w].at[2 * px + py], dst_ref=lands[w].at[k],
                send_sem=send_sems.at[i], recv_sem=recv_sems.at[i],
                device_id=(px, py, c), device_id_type=MESH))
    return copies


def _gather_copies(srcs, lands, send_sems, recv_sems):
    x, y, c = _position()
    me = 2 * x + y
    copies = []
    for w in range(len(srcs)):
        for k in range(1, N_CHIPS):
            px, py = _chip_peer(x, y, k)
            i = w * (N_CHIPS - 1) + k - 1
            copies.append(pltpu.make_async_remote_copy(
                src_ref=srcs[w].at[c], dst_ref=lands[w].at[me, c],
                send_sem=send_sems.at[i], recv_sem=recv_sems.at[i],
                device_id=(px, py, c), device_id_type=MESH))
    return copies


def _exchange(name, copies_of, n_copies, srcs, land_shapes):
    n = len(srcs)

    def body(*refs):
        copies = copies_of(refs[:n], refs[n:2 * n], refs[2 * n], refs[2 * n + 1])
        for cp in copies:
            cp.start()
        for cp in copies:
            cp.wait_recv()
        for cp in copies:
            cp.wait_send()

    return _pcall(
        body, name=name, out_shape=tuple(land_shapes),
        in_specs=[_ANY] * n, out_specs=(_ANY,) * n,
        scratch_shapes=[pltpu.SemaphoreType.DMA((n_copies,)), pltpu.SemaphoreType.DMA((n_copies,))],
    )(*srcs)


_HBM = pl.BlockSpec(memory_space=pltpu.HBM)
_SEM = pl.BlockSpec(memory_space=pltpu.SEMAPHORE)
_DATAFLOW = pltpu.SideEffectType.DATAFLOW_SIDE_EFFECTING


def _exchange_start(name, copies_of, n_copies, srcs, land_shapes):
    n = len(srcs)
    lands = [lax.empty(s.shape, s.dtype) for s in land_shapes]

    def body(*refs):
        for cp in copies_of(refs[:n], refs[n:2 * n], refs[2 * n], refs[2 * n + 1]):
            cp.start()
        refs[-1][...] = jnp.zeros_like(refs[-1])

    through = [pltpu.HBM(a.shape, a.dtype) for a in list(srcs) + lands]
    outs = _pcall(
        body, name=name,
        out_shape=(pltpu.SemaphoreType.DMA((n_copies,)), pltpu.SemaphoreType.DMA((n_copies,)),
                   *through, jax.ShapeDtypeStruct((8, 128), F32)),
        in_specs=[_HBM] * (2 * n),
        out_specs=(_SEM, _SEM, *([_HBM] * (2 * n)), pl.BlockSpec(memory_space=pltpu.VMEM)),
        input_output_aliases={i: 2 + i for i in range(2 * n)},
        compiler_params=pltpu.CompilerParams(has_side_effects=_DATAFLOW),
    )(*[pltpu.with_memory_space_constraint(a, pltpu.HBM) for a in list(srcs) + lands])
    return outs[:-1], outs[-1][0, 0]


def _exchange_wait(name, copies_of, state, after):
    send_sems, recv_sems, through = state[0], state[1], state[2:]
    n = len(through) // 2
    if after.ndim == 0:
        after = jnp.broadcast_to(after, (8, 128))
    after = pltpu.with_memory_space_constraint(after, pltpu.HBM)

    def body(*refs):
        for cp in copies_of(refs[:n], refs[n:2 * n], refs[2 * n], refs[2 * n + 1]):
            cp.wait_send()
            cp.wait_recv()

    outs = _pcall(
        body, name=name,
        out_shape=tuple(pltpu.HBM(a.shape, a.dtype) for a in through),
        in_specs=[_HBM] * (2 * n) + [_SEM, _SEM, _HBM], out_specs=tuple([_HBM] * (2 * n)),
        input_output_aliases={i: i for i in range(2 * n)},
        compiler_params=pltpu.CompilerParams(has_side_effects=_DATAFLOW),
    )(*through, send_sems, recv_sems, after)
    return outs[:n], outs[n:]


def _forward_halves(lands, *, name):
    n = len(lands)
    hops = N_CHIPS - 1

    def body(*refs):
        ins, outs, done = refs[:n], refs[n:2 * n], refs[2 * n]
        send_sems, recv_sems = refs[2 * n + 1], refs[2 * n + 2]
        x, y, c = _position()
        done[...] = jnp.zeros_like(done)
        copies = []
        for w in range(n):
            for k in range(1, N_CHIPS):
                px, py = _chip_peer(x, y, k)
                i = w * hops + k - 1
                copies.append(pltpu.make_async_remote_copy(
                    src_ref=ins[w].at[2 * px + py, c], dst_ref=outs[w].at[2 * px + py, c],
                    send_sem=send_sems.at[i], recv_sem=recv_sems.at[i],
                    device_id=(x, y, 1 - c), device_id_type=MESH))
        for cp in copies:
            cp.start()
        for cp in copies:
            cp.wait_recv()
        for cp in copies:
            cp.wait_send()

    outs = _pcall(
        body, name=name,
        out_shape=tuple(jax.ShapeDtypeStruct(a.shape, a.dtype) for a in lands) + (_TOKEN,),
        in_specs=[_ANY] * n, out_specs=(_ANY,) * n + (pl.BlockSpec(memory_space=pltpu.VMEM),),
        input_output_aliases={i: i for i in range(n)},
        scratch_shapes=[pltpu.SemaphoreType.DMA((n * hops,)), pltpu.SemaphoreType.DMA((n * hops,))],
    )(*lands)
    return outs[:n], outs[n][0, 0]


def _pair_lands(grads):
    return [jax.ShapeDtypeStruct((g.shape[0], g.shape[1] // 2, g.shape[2]), F32) for g in grads]


def _same_lands(parts):
    return [jax.ShapeDtypeStruct(p.shape, p.dtype) for p in parts]


def _pair_gather_halves(halves, *, name):
    n = len(halves)

    def body(*refs):
        ins, outs = refs[:n], refs[n:2 * n]
        send_sems, recv_sems = refs[2 * n:]
        x, y, c = _position()
        sends = []
        for w in range(n):
            cp = pltpu.make_async_remote_copy(
                src_ref=ins[w].at[c], dst_ref=outs[w].at[c],
                send_sem=send_sems.at[w], recv_sem=recv_sems.at[w],
                device_id=(x, y, 1 - c), device_id_type=MESH)
            cp.start()
            sends.append(cp)
        for cp in sends:
            cp.wait_recv()
        for cp in sends:
            cp.wait_send()

    whole = _pcall(
        body, name=name,
        out_shape=tuple(jax.ShapeDtypeStruct(h.shape, F32) for h in halves),
        in_specs=[_ANY] * n, out_specs=(_ANY,) * n,
        input_output_aliases={i: i for i in range(n)},
        scratch_shapes=[pltpu.SemaphoreType.DMA((n,)), pltpu.SemaphoreType.DMA((n,))],
    )(*halves)
    return [w.reshape(2 * w.shape[1], w.shape[2]) for w in whole]


def _all_reduce_small(packed, *, name, sum_row0):
    r_dim, c = packed.shape

    def body(src_ref, out_ref, recv_ref, send_sems, recv_sems):
        x, y, c_ = _position()
        me = 4 * x + 2 * y + c_
        recv_ref[0] = src_ref[...]
        sends = []
        for k in range(1, N_DEV):
            peer = (x ^ (k >> 2), y ^ ((k >> 1) & 1), c_ ^ (k & 1))
            cp = pltpu.make_async_remote_copy(
                src_ref=src_ref, dst_ref=recv_ref.at[k],
                send_sem=send_sems.at[k - 1], recv_sem=recv_sems.at[k - 1],
                device_id=peer, device_id_type=MESH)
            cp.start()
            sends.append(cp)
        for cp in sends:
            cp.wait_recv()
        total = recv_ref[me]
        for d in range(1, N_DEV):
            total = total + recv_ref[d ^ me]
        if sum_row0:
            row0 = jnp.sum(total[0:1, :], axis=1, keepdims=True)
            rid = lax.broadcasted_iota(jnp.int32, total.shape, 0)
            total = jnp.where(rid == 0, row0, total)
        out_ref[...] = total
        for cp in sends:
            cp.wait_send()

    return _pcall(
        body, name=name, out_shape=jax.ShapeDtypeStruct((r_dim, c), F32),
        in_specs=[pl.BlockSpec(memory_space=pltpu.VMEM)],
        out_specs=pl.BlockSpec(memory_space=pltpu.VMEM),
        scratch_shapes=[pltpu.VMEM((N_DEV, r_dim, c), F32),
                        pltpu.SemaphoreType.DMA((N_DEV - 1,)), pltpu.SemaphoreType.DMA((N_DEV - 1,))],
    )(packed)


def _pad_lanes(v, width):
    return jnp.pad(v, ((0, 0), (0, width - v.shape[1])))


def _pad_rows(v, rows):
    pad = [(0, 0)] * v.ndim
    pad[-2] = (0, rows - v.shape[-2])
    return jnp.pad(v, pad)


_IN_PROJ_SHARD_ROWS = 1312


def _rows_1024(v):
    flat = v.reshape(-1)
    pad = (-flat.shape[0]) % D_MODEL
    return jnp.pad(flat, (0, pad)).reshape(-1, D_MODEL)


def _local_step(xs, target, pw, fetch, reduce_start, reduce_midway,
                conv_w, conv_b, gate_g,
                norm_mix_g, norm_mlp_g, pool_b, pool_scale, ssm_dt_bias, ssm_a_log, ssm_d, final_g):
    bias_r = _pad_lanes(ssm_dt_bias, 128)
    alog_r = _pad_lanes(ssm_a_log, 128)
    dskip_x = jnp.repeat(ssm_d, HEAD_DIM, axis=1)
    bias_c = ssm_dt_bias.reshape(N_HEADS, 1)
    alog_c = ssm_a_log.reshape(N_HEADS, 1)
    e_mat = _head_lane_matrix()
    e3_mat = jnp.tile(e_mat, (_EXACT_PIECES, 1))

    g_mix0, g_mix1 = norm_mix_g[0:1], norm_mix_g[1:2]
    g_mlp0, g_mlp1 = norm_mlp_g[0:1], norm_mlp_g[1:2]
    fg = final_g.reshape(1, D_MODEL)

    h1 = _pool_fwd(xs, g_mix0, pw, pool_b, pool_scale)
    w1_0 = fetch("mlp0_up", h1)
    u0, hm0 = _matmul(h1, w1_0, "nn", name="mlp0_up", out_dtype=BF16, b_col_shards=True, a_norm=g_mlp0)
    w2_0 = fetch("mlp0_down", u0)
    h2 = _matmul(u0, w2_0, "nn", name="mlp0_down", a_relu2=True, add=h1)

    w_z, w_xbc, w_dt = fetch("in_proj", h2)
    xbc, hn1 = _matmul(h2, w_xbc, "nt", name="in_proj_xbc", out_dtype=BF16, a_norm=g_mix1)
    z = _matmul(hn1, w_z, "nt", name="in_proj_z", out_dtype=BF16)
    dt_raw = _matmul(hn1, w_dt, "nt", name="in_proj_dt")
    dt_raw_t = dt_raw[:, :N_HEADS].T
    xc = _conv_fwd(xbc, conv_w, conv_b)
    wout, w1_1, w2_1 = fetch("rest", xc)
    dt_r, acs_r, sg_r, acs_t = _ssd_decay(dt_raw, dt_raw_t, bias_r, bias_c, alog_r, alog_c)
    y, states = _ssd_fwd(xc, dt_r, acs_r, acs_t, dskip_x, e3_mat)
    yn = _gate_fwd(y, z, gate_g)
    h3 = _matmul(yn, wout, "nn", name="out_proj", add=h2)
    u1, hm1 = _matmul(h3, w1_1, "nn", name="mlp1_up", out_dtype=BF16, b_col_shards=True, a_norm=g_mlp1)

    dh4, dh4_16, small_final = _matmul(u1, w2_1, "nn", name="mlp1_down", a_relu2=True, add=h3,
                                       loss_head=(fg, target))

    def mlp_bwd_weights(dh_out16, hm, u, w2_i, tag):
        du = _matmul(dh_out16, w2_i, "nt", name=tag + "_du", out_dtype=BF16, relu2_grad_of=u)
        dw2 = _matmul(u, dh_out16, "tn", name=tag + "_dw2", a_relu2=True)
        dw1 = _matmul(hm, du, "tn", name=tag + "_dw1", out_col_shards=N_CHIPS)
        return du, dw1, dw2.reshape(N_CHIPS, D_FF // N_CHIPS, D_MODEL)

    def mlp_bwd_input(du, dh_out, h_in, w1_i, g_i, tag):
        return _matmul(du, w1_i, "nt", name=tag + "_dhm", b_col_shards=True, norm_bwd=(h_in, g_i, dh_out))

    du1, dw1_1, dw2_1 = mlp_bwd_weights(dh4_16, hm1, u1, w2_1, "mlp1")
    dh3, dh3_16, dg_mlp1 = mlp_bwd_input(du1, dh4, h3, w1_1, g_mlp1, "mlp1")

    dyn = _matmul(dh3_16, wout, "nt", name="out_proj_dyn", out_dtype=BF16)
    dwout = _matmul(yn, dh3_16, "tn", name="out_proj_dw").reshape(N_CHIPS, D_INNER // N_CHIPS, D_MODEL)
    behind = reduce_start("mlp1_out", [dw1_1, dw2_1, dwout])
    dy, dz, dg_gate = _gate_bwd(dyn, y, z, gate_g + behind)
    behind = reduce_midway("mlp1_out", dz)
    dxc, ddt_raw, small_ssd = _ssd_bwd(xc, dt_r, acs_r, sg_r, acs_t, alog_r, dskip_x + behind,
                                       e3_mat, e_mat.T, states, dy)
    dv, dconv = _conv_bwd_act(xbc, dxc, conv_w, conv_b)
    dxbc = _conv_bwd_in(dv, conv_w)
    dw_z = _matmul(dz, hn1, "tn", name="in_proj_z_dw")
    dw_xbc = _matmul(dxbc, hn1, "tn", name="in_proj_xbc_dw")
    dw_dt = _matmul(ddt_raw, hn1, "tn", name="in_proj_dt_dw")
    dwin = jnp.concatenate([dw_z, dw_xbc, dw_dt[:N_HEADS]], axis=0)
    dwin = _pad_rows(dwin.reshape(N_CHIPS, IN_PROJ_DIM // N_CHIPS, D_MODEL), _IN_PROJ_SHARD_ROWS)
    behind = reduce_start("in_proj", [dwin])
    dh2, dh2_16, dg_mix1 = _matmul(dxbc, w_xbc, "nn", name="in_proj_dh", more=[(dz, w_z), (ddt_raw, w_dt)],
                                   norm_bwd=(h2, g_mix1 + behind, dh3))
    behind = reduce_midway("in_proj", dh2_16)

    du0, dw1_0, dw2_0 = mlp_bwd_weights(dh2_16, hm0, u0, w2_0, "mlp0")
    dh1, _, dg_mlp0 = mlp_bwd_input(du0, dh2, h1, w1_0, g_mlp0 + behind, "mlp0")
    dx, dpw, small_pool = _pool_bwd(xs, g_mix0, pw, pool_b, pool_scale, dh1)
    dpw = jnp.transpose(dpw.reshape(4, N_CHIPS, POOL_GROUP // N_CHIPS, POOL_GROUP), (1, 0, 2, 3))
    dpw = dpw.reshape(N_CHIPS, 4 * (POOL_GROUP // N_CHIPS), POOL_GROUP)

    big = [dpw, dw1_0, dw2_0]
    rows = [
        small_final[1:2],
        small_final[0:1],
        small_pool[0:1], dg_mix1[0:1],
        dg_mlp0[0:1], dg_mlp1[0:1],
        small_pool[1:2], small_pool[2:3],
        _pad_lanes(small_ssd[0:3], D_MODEL),
        _rows_1024(dg_gate[0:1]),
        _rows_1024(dconv[0:CONV_K]),
        _rows_1024(dconv[CONV_K:CONV_K + 1]),
    ]
    return dx, big, rows


def kernel(x, norm_mix_g, norm_mlp_g, pool_w, pool_b, pool_scale, ssm_w_in, ssm_conv_w, ssm_conv_b, ssm_dt_bias, ssm_a_log, ssm_d, ssm_norm_g, ssm_w_out, mlp_w1, mlp_w2, final_g, loss_target, m_norm_mix_g, m_norm_mlp_g, m_pool_w, m_pool_b, m_pool_scale, m_ssm_w_in, m_ssm_conv_w, m_ssm_conv_b, m_ssm_dt_bias, m_ssm_a_log, m_ssm_d, m_ssm_norm_g, m_ssm_w_out, m_mlp_w1, m_mlp_w2, m_final_g, v_norm_mix_g, v_norm_mlp_g, v_pool_w, v_pool_b, v_pool_scale, v_ssm_w_in, v_ssm_conv_w, v_ssm_conv_b, v_ssm_dt_bias, v_ssm_a_log, v_ssm_d, v_ssm_norm_g, v_ssm_w_out, v_mlp_w1, v_mlp_w2, v_final_g):
    t_dim = x.shape[1]
    xs = x[0]
    target = loss_target[0]
    my_x, my_y, my_c = _position()
    my_chip = 2 * my_x + my_y

    def halves(w):
        return w.astype(BF16).reshape((2, w.shape[0] // 2) + w.shape[1:])

    def whole(gathered, own_shard):
        g = lax.dynamic_update_index_in_dim(gathered, own_shard, my_chip, axis=0)
        return g.reshape((N_CHIPS, 2 * g.shape[2]) + g.shape[3:])

    def gather_lands(own):
        return [jax.ShapeDtypeStruct((N_CHIPS,) + s.shape, s.dtype) for s in own]

    vec_cols = CONV_DIM // N_CHIPS
    vec_own = jnp.concatenate([ssm_conv_w[0], ssm_conv_b, _pad_lanes(ssm_norm_g, vec_cols)], axis=0)
    early_own = [halves(pool_w[0]), vec_own.reshape(2, (CONV_K + 2) // 2, vec_cols)]
    early, behind_early = _all_gather_weights(early_own)
    g_pool, g_vec = [whole(g, o) for g, o in zip(early, early_own)]
    pw = jnp.transpose(g_pool, (1, 0, 2, 3)).reshape(4, POOL_GROUP, POOL_GROUP)
    conv_w = jnp.transpose(g_vec[:, 0:CONV_K, :], (1, 0, 2)).reshape(CONV_K, CONV_DIM)
    conv_b = g_vec[:, CONV_K, :].reshape(1, CONV_DIM)
    gate_g = g_vec[:, CONV_K + 1, :D_INNER // N_CHIPS].reshape(1, D_INNER)

    def behind_it(zero, ws):
        return [halves(w + zero) for w in ws]

    fetches = {}
    up_own = behind_it(behind_early, [mlp_w1[0]])
    fetches["mlp0_up"], behind_gather = _exchange_start(
        "gather_mlp0_up_start", _gather_copies, len(up_own) * (N_CHIPS - 1), up_own, gather_lands(up_own))
    down_own = behind_it(behind_gather, [mlp_w2[0]])
    fetches["mlp0_down"], behind_gather = _exchange_start(
        "gather_mlp0_down_start", _gather_copies, len(down_own) * (N_CHIPS - 1), down_own, gather_lands(down_own))
    in_own = behind_it(behind_gather, [_pad_rows(ssm_w_in[0].T, _IN_PROJ_SHARD_ROWS)])
    fetches["in_proj"], behind_gather = _exchange_start(
        "gather_in_proj_start", _gather_copies, len(in_own) * (N_CHIPS - 1), in_own, gather_lands(in_own))

    def fetch(what, after):
        if what == "mlp0_up":
            own_thru, landed = _exchange_wait("gather_mlp0_up_wait", _gather_copies, fetches[what], after)
            landed, _ = _forward_halves(landed, name="forward_mlp0_up")
            return whole(landed[0], own_thru[0])
        if what == "mlp0_down":
            own_thru, landed = _exchange_wait("gather_mlp0_down_wait", _gather_copies, fetches[what], after)
            landed, _ = _forward_halves(landed, name="forward_mlp0_down")
            return whole(landed[0], own_thru[0]).reshape(D_FF, D_MODEL)
        if what == "in_proj":
            own_thru, landed = _exchange_wait("gather_in_proj_wait", _gather_copies, fetches["in_proj"], after)
            landed, behind = _forward_halves(landed, name="forward_in_proj")
            rest = behind_it(behind, [ssm_w_out[0], mlp_w1[1], mlp_w2[1]])
            fetches["rest"], behind = _exchange_start(
                "gather_rest_start", _gather_copies, len(rest) * (N_CHIPS - 1), rest, gather_lands(rest))
            win = whole(landed[0], own_thru[0])[:, :IN_PROJ_DIM // N_CHIPS].reshape(IN_PROJ_DIM, D_MODEL)
            w_dt = _pad_rows(win[D_INNER + CONV_DIM:], 128) + behind.astype(BF16)
            return win[:D_INNER], win[D_INNER:D_INNER + CONV_DIM], w_dt
        own_thru, landed = _exchange_wait("gather_rest_wait", _gather_copies, fetches["rest"], after)
        landed, _ = _forward_halves(landed, name="forward_rest")
        g_wout, w1_1, g_w2_1 = [whole(g, o) for g, o in zip(landed, own_thru)]
        return g_wout.reshape(D_INNER, D_MODEL), w1_1, g_w2_1.reshape(D_FF, D_MODEL)

    place = jnp.stack([my_c, my_chip]).astype(jnp.int32)
    waves = {}

    def reduce_start(wave, grads):
        waves[wave] = {}
        waves[wave]["pair"], behind = _exchange_start(
            "pair_%s_start" % wave, _pair_copies, len(grads), grads, _pair_lands(grads))
        return behind

    def reduce_midway(wave, after):
        st = waves[wave]
        grads, recv = _exchange_wait("pair_%s_wait" % wave, _pair_copies, st["pair"], after)
        sums = [_pair_sum(g, r, place, name="pair_sum_%s_%d" % (wave, i))
                for i, (g, r) in enumerate(zip(grads, recv))]
        st["f32"] = [s32 for _, s32 in sums]
        b16 = [s16 for s16, _ in sums]
        st["chip"], behind = _exchange_start(
            "chip_%s_start" % wave, _chip_copies, len(b16) * (N_CHIPS - 1), b16, _same_lands(b16))
        return behind

    def reduce_finish(wave, after):
        st = waves[wave]
        _, got = _exchange_wait("chip_%s_wait" % wave, _chip_copies, st["chip"], after)
        return [_chip_sum(s32, r, place, name="chip_sum_%s_%d" % (wave, i))
                for i, (s32, r) in enumerate(zip(st["f32"], got))]

    dx, big0, rows = _local_step(xs, target, pw, fetch, reduce_start, reduce_midway,
                                 conv_w, conv_b, gate_g,
                                 norm_mix_g + behind_gather, norm_mlp_g, pool_b, pool_scale,
                                 ssm_dt_bias, ssm_a_log, ssm_d, final_g)

    behind = reduce_start("layer0", big0)
    small = jnp.concatenate(rows, axis=0)
    small = jnp.pad(small, ((0, (-small.shape[0]) % 8), (0, 0))) + behind
    small = _all_reduce_small(small, name="all_reduce_small", sum_row0=True)
    behind = reduce_midway("layer0", small)
    h_w1_1, h_w2_1, h_wout = reduce_finish("mlp1_out", behind)
    (h_win,) = reduce_finish("in_proj", behind)
    g_w1_1, g_w2_1, g_wout_s, g_win_s = _pair_gather_halves([h_w1_1, h_w2_1, h_wout, h_win],
                                                            name="pair_gather_layer1")
    loss = small[0, 0]
    g_final = small[1]
    g_norm_mix = small[2:4]
    g_norm_mlp = small[4:6]
    g_pool_b, g_pool_scale = small[6:7], small[7:8]
    g_alog, g_dtb, g_dsk = small[8:9, :N_HEADS], small[9:10, :N_HEADS], small[10:11, :N_HEADS]
    g_gate_full = small[11:13].reshape(1, D_INNER)
    g_convw_full = small[13:25].reshape(CONV_K, CONV_DIM)
    g_convb_full = small[25:28].reshape(1, CONV_DIM)
    g_gate = lax.dynamic_slice_in_dim(g_gate_full, my_chip * (D_INNER // N_CHIPS), D_INNER // N_CHIPS, axis=1)
    g_convw = lax.dynamic_slice_in_dim(g_convw_full, my_chip * (CONV_DIM // N_CHIPS), CONV_DIM // N_CHIPS, axis=1)
    g_convb = lax.dynamic_slice_in_dim(g_convb_full, my_chip * (CONV_DIM // N_CHIPS), CONV_DIM // N_CHIPS, axis=1)

    grads = {
        "norm_mix_g": g_norm_mix, "norm_mlp_g": g_norm_mlp,
        "pool_b": g_pool_b, "pool_scale": g_pool_scale,
        "ssm_conv_w": g_convw.reshape(ssm_conv_w.shape),
        "ssm_conv_b": g_convb, "ssm_dt_bias": g_dtb, "ssm_a_log": g_alog, "ssm_d": g_dsk,
        "ssm_norm_g": g_gate, "ssm_w_out": g_wout_s.reshape(ssm_w_out.shape),
        "final_g": g_final,
    }
    weights = dict(norm_mix_g=norm_mix_g, norm_mlp_g=norm_mlp_g, pool_w=pool_w, pool_b=pool_b,
                   pool_scale=pool_scale, ssm_w_in=ssm_w_in, ssm_conv_w=ssm_conv_w, ssm_conv_b=ssm_conv_b,
                   ssm_dt_bias=ssm_dt_bias, ssm_a_log=ssm_a_log, ssm_d=ssm_d, ssm_norm_g=ssm_norm_g,
                   ssm_w_out=ssm_w_out, mlp_w1=mlp_w1, mlp_w2=mlp_w2, final_g=final_g)
    moms = dict(norm_mix_g=(m_norm_mix_g, v_norm_mix_g), norm_mlp_g=(m_norm_mlp_g, v_norm_mlp_g),
                pool_w=(m_pool_w, v_pool_w), pool_b=(m_pool_b, v_pool_b),
                pool_scale=(m_pool_scale, v_pool_scale), ssm_w_in=(m_ssm_w_in, v_ssm_w_in),
                ssm_conv_w=(m_ssm_conv_w, v_ssm_conv_w), ssm_conv_b=(m_ssm_conv_b, v_ssm_conv_b),
                ssm_dt_bias=(m_ssm_dt_bias, v_ssm_dt_bias), ssm_a_log=(m_ssm_a_log, v_ssm_a_log),
                ssm_d=(m_ssm_d, v_ssm_d), ssm_norm_g=(m_ssm_norm_g, v_ssm_norm_g),
                ssm_w_out=(m_ssm_w_out, v_ssm_w_out), mlp_w1=(m_mlp_w1, v_mlp_w1),
                mlp_w2=(m_mlp_w2, v_mlp_w2), final_g=(m_final_g, v_final_g))
    names = list(weights)
    big_names = ("pool_w", "ssm_w_in", "ssm_w_out", "mlp_w1", "mlp_w2")
    deltas, new_m, new_v = {}, {}, {}

    def as_rows(nm, a):
        return a[0].T if nm == "ssm_w_in" else a.reshape(-1, a.shape[-1])

    def from_rows(nm, r):
        return r.T[None] if nm == "ssm_w_in" else r.reshape(weights[nm].shape)

    def update(nm, grad_rows, layer=None, into=None):
        return _adamw(as_rows(nm, weights[nm]), grad_rows, as_rows(nm, moms[nm][0]), as_rows(nm, moms[nm][1]),
                      name="adamw_%s_%s" % (nm, layer), part=layer, into=into)

    def keep(nm, results):
        deltas[nm], new_m[nm], new_v[nm], grads[nm] = [from_rows(nm, r) for r in results]

    keep("ssm_w_in", update("ssm_w_in", g_win_s[:IN_PROJ_DIM // N_CHIPS]))
    keep("ssm_w_out", update("ssm_w_out", g_wout_s))
    w1_done = update("mlp_w1", g_w1_1, layer=1)
    w2_done = update("mlp_w2", g_w2_1, layer=1)
    small_names = [nm for nm in names if nm not in big_names]
    small_done = _adamw_small(
        [tuple(as_rows(nm, a) for a in (weights[nm], grads[nm], moms[nm][0], moms[nm][1])) for nm in small_names],
        name="adamw_small")
    for nm, (d_, m_, v_) in zip(small_names, small_done):
        deltas[nm], new_m[nm], new_v[nm] = [from_rows(nm, r) for r in (d_, m_, v_)]

    above = (deltas["ssm_w_in"][0, 0, 0] + deltas["ssm_w_out"][0, 0, 0] + w1_done[0][-1, -1]
             + w2_done[0][-1, -1] + small_done[0][0][0, 0])
    g_pool_w, g_w1_0, g_w2_0 = _pair_gather_halves(reduce_finish("layer0", above), name="pair_gather_layer0")
    keep("mlp_w1", update("mlp_w1", g_w1_0, layer=0, into=w1_done))
    keep("mlp_w2", update("mlp_w2", g_w2_0, layer=0, into=w2_done))
    keep("pool_w", update("pool_w", g_pool_w))

    grad_x = dx.reshape(x.shape)
    out_grads = [grads[nm].reshape(weights[nm].shape) for nm in names]
    return (loss, grad_x, *out_grads, *[deltas[nm] for nm in names],
            *[new_m[nm] for nm in names], *[new_v[nm] for nm in names])
```

```python
import jax
import jax.numpy as jnp
from jax import lax
from jax.experimental import pallas as pl
from jax.experimental.pallas import tpu as pltpu

F32 = jnp.float32
BF16 = jnp.bfloat16
MESH = pl.DeviceIdType.MESH

D_MODEL = 1024
RMS_EPS = 1e-5
POOL_WINDOWS = (2, 4, 8, 16)
POOL_GROUP = 256
POOL_HALO = 16
D_INNER = 2048
HEAD_DIM = 64
N_HEADS = 32
N_GROUPS = 4
HEADS_PER_GROUP = 8
D_STATE = 128
CONV_K = 4
CONV_HALO = 8
CHUNK = 128
CONV_DIM = 3072
IN_PROJ_DIM = 5152
D_FF = 4096
N_CHIPS = 4
N_DEV = 8

ADAM_LR = 0.001
ADAM_B1 = 0.9
ADAM_B2 = 0.999
ADAM_EPS = 1e-08
ADAM_WD = 0.01
ADAM_STEP = 10

VMEM_LIMIT = 56 * 1024 * 1024
NEG_INF = float("-inf")


def _pcall(body, **kw):
    return pl.pallas_call(body, **kw)


def _params(*sem):
    return pltpu.CompilerParams(dimension_semantics=sem, vmem_limit_bytes=VMEM_LIMIT)


def _sigmoid(v):
    return 1.0 / (1.0 + jnp.exp(-v))


def _row_spec(tb, d, nb=None, reverse=False):
    if reverse:
        return pl.BlockSpec((tb, d), lambda i: (nb - 1 - i, 0))
    return pl.BlockSpec((tb, d), lambda i: (i, 0))


def _const_spec(shape):
    return pl.BlockSpec(shape, lambda *_: tuple(0 for _ in shape))


_DIMS = {"nn": (((1,), (0,)), ((), ())),
         "nt": (((1,), (1,)), ((), ())),
         "tn": (((0,), (0,)), ((), ()))}


_MATMUL_VMEM_BUDGET = 40 * 1024 * 1024


def _matmul_tiles(m_dim, n_dim, k_dim, a_bytes, b_bytes, mn_bytes):
    tm, tn = min(m_dim, 1024), min(n_dim, 1024)
    while 2 * (tm * k_dim * a_bytes + tn * k_dim * b_bytes + tm * tn * mn_bytes) > _MATMUL_VMEM_BUDGET:
        if tm >= tn:
            tm //= 2
        else:
            tn //= 2
    return tm, tn


def _matmul(a, b, mode, *, name, out_dtype=F32, a_relu2=False, a_norm=None, add=None, relu2_grad_of=None,
            out_col_shards=1, b_col_shards=False, norm_bwd=None, loss_head=None, more=()):
    if mode == "tn":
        k_dim, m_dim = a.shape
    else:
        m_dim, k_dim = a.shape
    if b_col_shards:
        n_shards, shard_cols = b.shape[0], b.shape[2]
        n_dim = n_shards * shard_cols if mode == "nn" else b.shape[1]
    else:
        n_dim = b.shape[0] if mode == "nt" else b.shape[1]
    mn_bytes = jnp.dtype(out_dtype).itemsize
    if relu2_grad_of is not None:
        mn_bytes += relu2_grad_of.dtype.itemsize
    if add is not None:
        mn_bytes += add.dtype.itemsize
    row_epilogue = norm_bwd is not None or loss_head is not None
    if row_epilogue:
        assert out_dtype == F32 and out_col_shards == 1 and (norm_bwd is None or loss_head is None)
        mn_bytes += 4 + 4 + 2 + 8
    a_bytes = a.dtype.itemsize + (2 if a_norm is not None else 0)
    k_all = k_dim + sum(a_i.shape[1] for a_i, _ in more)
    tm, tn = _matmul_tiles(m_dim, n_dim, k_all, a_bytes, b.dtype.itemsize, mn_bytes)
    if row_epilogue:
        while tn < n_dim:
            tm, tn = tm // 2, tn * 2
    assert m_dim % tm == 0 and n_dim % tn == 0
    a_spec = (pl.BlockSpec((k_dim, tm), lambda i, j: (0, i)) if mode == "tn"
              else pl.BlockSpec((tm, k_dim), lambda i, j: (i, 0)))
    if b_col_shards and mode == "nn":
        assert shard_cols % tn == 0
        per_shard = shard_cols // tn
        b_spec = pl.BlockSpec((None, k_dim, tn), lambda i, j: (j // per_shard, 0, j % per_shard))
    elif b_col_shards:
        assert mode == "nt" and k_dim == n_shards * shard_cols
        b_spec = pl.BlockSpec((n_shards, tn, shard_cols), lambda i, j: (0, j, 0))
    else:
        b_spec = (pl.BlockSpec((tn, k_dim), lambda i, j: (j, 0)) if mode == "nt"
                  else pl.BlockSpec((k_dim, tn), lambda i, j: (0, j)))
    mn_spec = pl.BlockSpec((tm, tn), lambda i, j: (i, j))
    operands, in_specs = [a, b], [a_spec, b_spec]
    for a_i, b_i in more:
        assert mode == "nn" and a_i.shape[0] == m_dim and b_i.shape == (a_i.shape[1], n_dim)
        operands += [a_i, b_i]
        in_specs += [pl.BlockSpec((tm, a_i.shape[1]), lambda i, j: (i, 0)),
                     pl.BlockSpec((a_i.shape[1], tn), lambda i, j: (0, j))]
    if relu2_grad_of is not None:
        operands.append(relu2_grad_of)
        in_specs.append(mn_spec)
    if add is not None:
        operands.append(add)
        in_specs.append(mn_spec)
    gain_spec = pl.BlockSpec((1, n_dim), lambda i, j: (0, 0))
    if a_norm is not None:
        assert mode in ("nn", "nt") and a.dtype == F32 and not row_epilogue and out_col_shards == 1
        operands.append(a_norm)
        in_specs.append(pl.BlockSpec((1, k_dim), lambda i, j: (0, 0)))
    if norm_bwd is not None:
        h_in, g_in, dres_in = norm_bwd
        operands += [h_in, g_in, dres_in]
        in_specs += [mn_spec, gain_spec, mn_spec]
    if loss_head is not None:
        operands += list(loss_head)
        in_specs += [gain_spec, mn_spec]
    if row_epilogue:
        out_shape = (jax.ShapeDtypeStruct((m_dim, n_dim), F32), jax.ShapeDtypeStruct((m_dim, n_dim), BF16),
                     jax.ShapeDtypeStruct((8, n_dim), F32))
        out_spec = (mn_spec, mn_spec, pl.BlockSpec((8, n_dim), lambda i, j: (0, 0)))
    elif out_col_shards == 1:
        out_shape = jax.ShapeDtypeStruct((m_dim, n_dim), out_dtype)
        out_spec = mn_spec
    else:
        n_shard = n_dim // out_col_shards
        assert n_shard % tn == 0
        per = n_shard // tn
        out_shape = jax.ShapeDtypeStruct((out_col_shards, m_dim, n_shard), out_dtype)
        out_spec = pl.BlockSpec((None, tm, tn), lambda i, j: (j // per, i, j % per))
    if a_norm is not None:
        out_shape = (out_shape, jax.ShapeDtypeStruct((m_dim, k_dim), BF16))
        out_spec = (out_spec, pl.BlockSpec((tm, k_dim), lambda i, j: (i, 0)))

    n_in = len(operands)

    def body(*refs):
        a_ref, b_ref, o_ref = refs[0], refs[1], refs[n_in]
        if a_norm is not None:
            normed_ref = refs[n_in + 1]

            @pl.when(pl.program_id(1) == 0)
            def _():
                xa = a_ref[...]
                normed_ref[...] = (xa * _rms(xa) * refs[n_in - 1][...]).astype(BF16)

            av = normed_ref[...]
        else:
            av = a_ref[...]
        if a_relu2:
            av = jnp.maximum(av, 0)
            av = av * av
        if b_col_shards and mode == "nt":
            r = None
            for s in range(n_shards):
                part = lax.dot_general(av[:, s * shard_cols:(s + 1) * shard_cols].astype(BF16),
                                       b_ref[s].astype(BF16), _DIMS[mode], preferred_element_type=F32)
                r = part if r is None else r + part
        else:
            r = lax.dot_general(av.astype(BF16), b_ref[...].astype(BF16), _DIMS[mode],
                                preferred_element_type=F32)
        nxt = 2
        for _ in more:
            r = r + jnp.dot(refs[nxt][...].astype(BF16), refs[nxt + 1][...].astype(BF16),
                            preferred_element_type=F32)
            nxt += 2
        if relu2_grad_of is not None:
            r = r * (2.0 * jnp.maximum(refs[nxt][...].astype(F32), 0.0))
            nxt += 1
        if add is not None:
            r = r + refs[nxt][...]
            nxt += 1
        if not row_epilogue:
            o_ref[...] = r.astype(out_dtype)
            return
        dh16_ref, small_ref = refs[n_in + 1], refs[n_in + 2]

        @pl.when(pl.program_id(0) == 0)
        def _():
            small_ref[...] = jnp.zeros_like(small_ref)

        if norm_bwd is not None:
            h_ref, g_ref, dres_ref = refs[nxt:nxt + 3]
            x, dy = h_ref[...], r
        else:
            g_ref, t_ref = refs[nxt:nxt + 2]
            x = r
        rr = _rms(x)
        xhat = x * rr
        gv = g_ref[...]
        if loss_head is not None:
            err = xhat * gv - t_ref[...]
            small_ref[1:2, :] += (0.5 / n_dim) * jnp.sum(err * err, axis=0, keepdims=True)
            dy = err * (1.0 / n_dim)
        dxhat = dy * gv
        dh = rr * (dxhat - xhat * jnp.mean(dxhat * xhat, axis=-1, keepdims=True))
        if norm_bwd is not None:
            dh = dres_ref[...] + dh
        o_ref[...] = dh
        dh16_ref[...] = dh.astype(BF16)
        small_ref[0:1, :] += jnp.sum(dy * xhat, axis=0, keepdims=True)

    if row_epilogue:
        semantics = ("arbitrary", "arbitrary")
    elif a_norm is not None:
        semantics = ("parallel", "arbitrary")
    else:
        semantics = ("parallel", "parallel")
    return _pcall(
        body, name=name, out_shape=out_shape,
        grid=(m_dim // tm, n_dim // tn),
        in_specs=in_specs, out_specs=out_spec,
        compiler_params=_params(*semantics),
    )(*operands)


def _rms(x):
    return lax.rsqrt(jnp.mean(x * x, axis=-1, keepdims=True) + RMS_EPS)


def _pool_mixed(ext, hn, t0, tb):
    t = t0 + lax.broadcasted_iota(jnp.int32, (tb, 1), 0)
    parts = []
    for gi, w in enumerate(POOL_WINDOWS):
        lanes = slice(gi * POOL_GROUP, (gi + 1) * POOL_GROUP)
        s = ext[:, lanes]
        k = 1
        while k < w:
            s = s + pltpu.roll(s, k, 0)
            k *= 2
        cnt = jnp.minimum(t + 1, w).astype(F32)
        parts.append(s[POOL_HALO:, :] / cnt - hn[:, lanes])
    return parts


def _pool_fwd(x, g, pw, pb, ps, *, tb=512):
    t_dim, d = x.shape

    def body(x_ref, g_ref, pw_ref, pb_ref, ps_ref, o_ref, ext_ref):
        i = pl.program_id(0)

        @pl.when(i == 0)
        def _():
            ext_ref[0:POOL_HALO, :] = jnp.zeros((POOL_HALO, d), F32)

        xv = x_ref[...]
        hn = xv * _rms(xv) * g_ref[...]
        ext_ref[POOL_HALO:, :] = hn
        mixed = _pool_mixed(ext_ref[...], hn, i * tb, tb)
        for gi in range(len(POOL_WINDOWS)):
            lanes = slice(gi * POOL_GROUP, (gi + 1) * POOL_GROUP)
            out = jnp.dot(mixed[gi].astype(BF16), pw_ref[gi], preferred_element_type=F32)
            o_ref[:, lanes] = xv[:, lanes] + (out + pb_ref[:, lanes]) * ps_ref[:, lanes]
        ext_ref[0:POOL_HALO, :] = hn[tb - POOL_HALO:, :]

    return _pcall(
        body, name="pool_fwd", out_shape=jax.ShapeDtypeStruct((t_dim, d), F32),
        grid=(t_dim // tb,),
        in_specs=[_row_spec(tb, d), _const_spec((1, d)), _const_spec((4, POOL_GROUP, POOL_GROUP)),
                  _const_spec((1, d)), _const_spec((1, d))],
        out_specs=_row_spec(tb, d),
        scratch_shapes=[pltpu.VMEM((POOL_HALO + tb, d), F32)],
        compiler_params=_params("arbitrary"),
    )(x, g, pw, pb, ps)


def _pool_bwd(x, g, pw, pb, ps, dh1, *, tb=512):
    t_dim, d = x.shape
    nb = t_dim // tb
    halo_per_block = tb // POOL_HALO

    def body(x_ref, xprev_ref, g_ref, pw_ref, pb_ref, ps_ref, dh1_ref,
             dx_ref, dpw_ref, small_ref, ext_ref, dext_ref):
        i = pl.program_id(0)
        blk = nb - 1 - i

        @pl.when(i == 0)
        def _():
            dpw_ref[...] = jnp.zeros_like(dpw_ref)
            small_ref[...] = jnp.zeros_like(small_ref)
            dext_ref[tb:, :] = jnp.zeros((POOL_HALO, d), F32)

        gv = g_ref[...]
        xv = x_ref[...]
        r = _rms(xv)
        xhat = xv * r
        hn = xhat * gv
        xp = xprev_ref[...]
        hprev = xp * _rms(xp) * gv * (blk > 0).astype(F32)
        ext_ref[0:POOL_HALO, :] = hprev
        ext_ref[POOL_HALO:, :] = hn
        mixed = _pool_mixed(ext_ref[...], hn, blk * tb, tb)

        dout = dh1_ref[...]
        t = blk * tb + lax.broadcasted_iota(jnp.int32, (tb, 1), 0)
        for gi, w in enumerate(POOL_WINDOWS):
            lanes = slice(gi * POOL_GROUP, (gi + 1) * POOL_GROUP)
            mb = mixed[gi].astype(BF16)
            pre = jnp.dot(mb, pw_ref[gi], preferred_element_type=F32) + pb_ref[:, lanes]
            dg_out = dout[:, lanes]
            small_ref[2:3, lanes] += jnp.sum(dg_out * pre, axis=0, keepdims=True)
            dpre = dg_out * ps_ref[:, lanes]
            small_ref[1:2, lanes] += jnp.sum(dpre, axis=0, keepdims=True)
            dpb16 = dpre.astype(BF16)
            dpw_ref[gi] += lax.dot_general(mb, dpb16, _DIMS["tn"], preferred_element_type=F32)
            dmixed = lax.dot_general(dpb16, pw_ref[gi], _DIMS["nt"], preferred_element_type=F32)
            cnt = jnp.minimum(t + 1, w).astype(F32)
            dq = dmixed / cnt
            dext_ref[0:tb, lanes] = dq
            s = dext_ref[:, lanes]
            k = 1
            while k < w:
                s = s + pltpu.roll(s, tb + POOL_HALO - k, 0)
                k *= 2
            dhn = s[0:tb, :] - dmixed
            dext_ref[tb:, lanes] = dq[0:POOL_HALO, :]
            small_ref[0:1, lanes] += jnp.sum(dhn * xhat[:, lanes], axis=0, keepdims=True)
            ext_ref[POOL_HALO:, lanes] = dhn * gv[:, lanes]
        dxhat = ext_ref[POOL_HALO:, :]
        dx_ref[...] = dout + r * (dxhat - xhat * jnp.mean(dxhat * xhat, axis=-1, keepdims=True))

    return _pcall(
        body, name="pool_bwd",
        out_shape=(jax.ShapeDtypeStruct((t_dim, d), F32),
                   jax.ShapeDtypeStruct((4, POOL_GROUP, POOL_GROUP), F32),
                   jax.ShapeDtypeStruct((8, d), F32)),
        grid=(nb,),
        in_specs=[_row_spec(tb, d, nb, True),
                  pl.BlockSpec((POOL_HALO, d),
                               lambda i: (jnp.maximum((nb - 1 - i) * halo_per_block - 1, 0), 0)),
                  _const_spec((1, d)), _const_spec((4, POOL_GROUP, POOL_GROUP)),
                  _const_spec((1, d)), _const_spec((1, d)), _row_spec(tb, d, nb, True)],
        out_specs=(_row_spec(tb, d, nb, True), _const_spec((4, POOL_GROUP, POOL_GROUP)),
                   _const_spec((8, d))),
        scratch_shapes=[pltpu.VMEM((POOL_HALO + tb, d), F32), pltpu.VMEM((tb + POOL_HALO, d), F32)],
        compiler_params=_params("arbitrary"),
    )(x, x, g, pw, pb, ps, dh1)


_CONV_CB = 1024
_STRIP = 16


def _strips(tb, fn, unroll=4):
    def step(i, carry):
        fn(pl.multiple_of(i * _STRIP, _STRIP))
        return carry
    lax.fori_loop(0, tb // _STRIP, step, 0, unroll=unroll)


def _conv_taps(ext_ref, r0, w):
    shifted = [ext_ref[CONV_HALO + r0 - sh:CONV_HALO + r0 - sh + _STRIP, :] for sh in range(CONV_K)]
    acc = shifted[0] * w[CONV_K - 1:CONV_K, :]
    for sh in range(1, CONV_K):
        acc = acc + shifted[sh] * w[CONV_K - 1 - sh:CONV_K - sh, :]
    return shifted, acc


def _conv_fwd(u, w, b, *, tb=512):
    t_dim, c = u.shape
    cb = _CONV_CB

    def body(u_ref, w_ref, b_ref, o_ref, ext_ref):
        @pl.when(pl.program_id(1) == 0)
        def _():
            ext_ref[0:CONV_HALO, :] = jnp.zeros((CONV_HALO, cb), F32)

        wv = w_ref[...]
        bv = b_ref[...]

        def fill(r0):
            ext_ref[pl.ds(CONV_HALO + r0, _STRIP), :] = u_ref[pl.ds(r0, _STRIP), :].astype(F32)

        _strips(tb, fill)
        for r0 in range(0, tb, _STRIP):
            v = _conv_taps(ext_ref, r0, wv)[1] + bv
            o_ref[r0:r0 + _STRIP, :] = (v * _sigmoid(v)).astype(BF16)
        ext_ref[0:CONV_HALO, :] = ext_ref[tb:tb + CONV_HALO, :]

    blk = pl.BlockSpec((tb, cb), lambda j, t: (t, j))
    return _pcall(
        body, name="conv_fwd", out_shape=jax.ShapeDtypeStruct((t_dim, c), BF16),
        grid=(c // cb, t_dim // tb),
        in_specs=[blk, pl.BlockSpec((CONV_K, cb), lambda j, t: (0, j)),
                  pl.BlockSpec((1, cb), lambda j, t: (0, j))],
        out_specs=blk,
        scratch_shapes=[pltpu.VMEM((CONV_HALO + tb, cb), F32)],
        compiler_params=_params("parallel", "arbitrary"),
    )(u, w, b)


def _conv_bwd_act(u, dxc, w, b, *, tb=512):
    t_dim, c = u.shape
    cb = _CONV_CB
    half = _STRIP // 2

    def body(u_ref, d_ref, w_ref, b_ref, dv_ref, dwb_ref, ext_ref, acc_ref):
        @pl.when(pl.program_id(1) == 0)
        def _():
            ext_ref[0:CONV_HALO, :] = jnp.zeros((CONV_HALO, cb), F32)
            dwb_ref[...] = jnp.zeros_like(dwb_ref)

        acc_ref[...] = jnp.zeros_like(acc_ref)
        wv = w_ref[...]
        bv = b_ref[...]

        def fill(r0):
            ext_ref[pl.ds(CONV_HALO + r0, _STRIP), :] = u_ref[pl.ds(r0, _STRIP), :].astype(F32)

        _strips(tb, fill)
        for r0 in range(0, tb, _STRIP):
            shifted, v = _conv_taps(ext_ref, r0, wv)
            v = v + bv
            sg = _sigmoid(v)
            dv = d_ref[r0:r0 + _STRIP, :].astype(F32) * (sg * (1.0 + v * (1.0 - sg)))
            dv_ref[r0:r0 + _STRIP, :] = dv.astype(BF16)
            acc_ref[CONV_K] += dv[0:half, :] + dv[half:, :]
            for sh in range(CONV_K):
                p = dv * shifted[sh]
                acc_ref[CONV_K - 1 - sh] += p[0:half, :] + p[half:, :]
        for k in range(CONV_K + 1):
            dwb_ref[k:k + 1, :] += jnp.sum(acc_ref[k], axis=0, keepdims=True)
        ext_ref[0:CONV_HALO, :] = ext_ref[tb:tb + CONV_HALO, :]

    blk = pl.BlockSpec((tb, cb), lambda j, t: (t, j))
    return _pcall(
        body, name="conv_bwd_act",
        out_shape=(jax.ShapeDtypeStruct((t_dim, c), BF16), jax.ShapeDtypeStruct((8, c), F32)),
        grid=(c // cb, t_dim // tb),
        in_specs=[blk, blk, pl.BlockSpec((CONV_K, cb), lambda j, t: (0, j)),
                  pl.BlockSpec((1, cb), lambda j, t: (0, j))],
        out_specs=(blk, pl.BlockSpec((8, cb), lambda j, t: (0, j))),
        scratch_shapes=[pltpu.VMEM((CONV_HALO + tb, cb), F32), pltpu.VMEM((CONV_K + 1, half, cb), F32)],
        compiler_params=_params("parallel", "arbitrary"),
    )(u, dxc, w, b)


def _conv_bwd_in(dv, w, *, tb=512):
    t_dim, c = dv.shape
    cb = _CONV_CB
    nb = t_dim // tb

    def body(dv_ref, w_ref, du_ref, ext_ref):
        @pl.when(pl.program_id(1) == 0)
        def _():
            ext_ref[tb:, :] = jnp.zeros((CONV_HALO, cb), F32)

        wv = w_ref[...]

        def fill(r0):
            ext_ref[pl.ds(r0, _STRIP), :] = dv_ref[pl.ds(r0, _STRIP), :].astype(F32)

        _strips(tb, fill)
        for r0 in range(0, tb, _STRIP):
            acc = ext_ref[r0:r0 + _STRIP, :] * wv[CONV_K - 1:CONV_K, :]
            for sh in range(1, CONV_K):
                acc = acc + ext_ref[r0 + sh:r0 + sh + _STRIP, :] * wv[CONV_K - 1 - sh:CONV_K - sh, :]
            du_ref[r0:r0 + _STRIP, :] = acc.astype(BF16)
        ext_ref[tb:, :] = ext_ref[0:CONV_HALO, :]

    blk = pl.BlockSpec((tb, cb), lambda j, t: (nb - 1 - t, j))
    return _pcall(
        body, name="conv_bwd_in", out_shape=jax.ShapeDtypeStruct((t_dim, c), BF16),
        grid=(c // cb, nb),
        in_specs=[blk, pl.BlockSpec((CONV_K, cb), lambda j, t: (0, j))],
        out_specs=blk,
        scratch_shapes=[pltpu.VMEM((tb + CONV_HALO, cb), F32)],
        compiler_params=_params("parallel", "arbitrary"),
    )(dv, w)


def _softplus(v):
    e = jnp.exp(-jnp.abs(v))
    w = 1.0 + e
    log1p = jnp.where(w == 1.0, e, jnp.log(w) * e / jnp.where(w == 1.0, 1.0, w - 1.0))
    return jnp.maximum(v, 0.0) + log1p


def _cumsum_rows(v):
    row = lax.broadcasted_iota(jnp.int32, v.shape, 0) & (CHUNK - 1)
    k = 1
    while k < CHUNK:
        v = v + jnp.where(row >= k, pltpu.roll(v, k, 0), 0.0)
        k *= 2
    return v


def _cumsum_lanes(v):
    col = lax.broadcasted_iota(jnp.int32, v.shape, 1) & (CHUNK - 1)
    k = 1
    while k < CHUNK:
        v = v + jnp.where(col >= k, pltpu.roll(v, k, 1), 0.0)
        k *= 2
    return v


def _rev_cumsum_rows(v):
    row = lax.broadcasted_iota(jnp.int32, v.shape, 0)
    k = 1
    while k < CHUNK:
        v = v + jnp.where(row < CHUNK - k, pltpu.roll(v, CHUNK - k, 0), 0.0)
        k *= 2
    return v


PAIR = 2 * HEAD_DIM
GROUP_LANES = HEADS_PER_GROUP * HEAD_DIM


def _head_lane_matrix():
    h = lax.broadcasted_iota(jnp.int32, (128, D_INNER), 0)
    j = lax.broadcasted_iota(jnp.int32, (128, D_INNER), 1)
    return (j // HEAD_DIM == h).astype(BF16)


def _split_bf16(v, pieces):
    out = []
    for _ in range(pieces):
        p = v.astype(BF16)
        out.append(p)
        v = v - p.astype(F32)
    return out


_EXACT_PIECES = 3


def _expand_heads(values, e3):
    lhs = jnp.concatenate([jnp.concatenate(_split_bf16(v, _EXACT_PIECES), axis=1) for v in values], axis=0)
    out = jnp.dot(lhs, e3, preferred_element_type=F32)
    rows = values[0].shape[0]
    return [out[i * rows:(i + 1) * rows, :] for i in range(len(values))]


def _reduce_heads(v, et, pieces):
    return sum(jnp.dot(p, et, preferred_element_type=F32) for p in _split_bf16(v, pieces))


def _ssd_decay(dt_raw, dt_raw_t, bias_r, bias_c, alog_r, alog_c, *, tb=1024):
    t_dim = dt_raw.shape[0]
    tb = min(tb, t_dim)
    assert t_dim % tb == 0 and tb % CHUNK == 0

    def body(dtr_ref, dtt_ref, br_ref, bc_ref, ar_ref, ac_ref, dt_ref, acs_ref, sg_ref, acst_ref):
        pre = dtr_ref[...] + br_ref[...]
        dt = _softplus(pre)
        dt_ref[...] = dt
        sg_ref[...] = _sigmoid(pre)
        acs_ref[...] = _cumsum_rows(dt * (-jnp.exp(ar_ref[...])))
        acst_ref[...] = _cumsum_lanes(_softplus(dtt_ref[...] + bc_ref[...]) * (-jnp.exp(ac_ref[...])))

    rows = pl.BlockSpec((tb, 128), lambda i: (i, 0))
    cols = pl.BlockSpec((N_HEADS, tb), lambda i: (0, i))
    sds = jax.ShapeDtypeStruct((t_dim, 128), F32)
    return _pcall(
        body, name="ssd_decay",
        out_shape=(sds, sds, sds, jax.ShapeDtypeStruct((N_HEADS, t_dim), F32)),
        grid=(t_dim // tb,),
        in_specs=[rows, cols, _const_spec((1, 128)), _const_spec((N_HEADS, 1)),
                  _const_spec((1, 128)), _const_spec((N_HEADS, 1))],
        out_specs=(rows, rows, rows, cols), compiler_params=_params("parallel"),
    )(dt_raw, dt_raw_t, bias_r, bias_c, alog_r, alog_c)


def _pair_decay(acs_slab, acs_c, h0, causal, left):
    other = pltpu.roll(acs_slab, HEAD_DIM, 1)
    col0 = jnp.where(left, acs_slab, other)
    col1 = jnp.where(left, other, acs_slab)
    l0 = jnp.exp(jnp.where(causal, col0 - acs_c[h0:h0 + 1, :], NEG_INF))
    l1 = jnp.exp(jnp.where(causal, col1 - acs_c[h0 + 1:h0 + 2, :], NEG_INF))
    return l0, l1


def _ssd_fwd(xc, dt_r, acs_r, acs_t, dskip_x, e3_mat):
    t_dim = xc.shape[0]
    nc = t_dim // CHUNK

    def body(xc_ref, dt_ref, acs_ref, acst_ref, dk_ref, e3_ref, y_ref, st_ref, state):
        @pl.when(pl.program_id(0) == 0)
        def _():
            state[...] = jnp.zeros_like(state)

        dt, acs = _expand_heads([dt_ref[...], acs_ref[...]], e3_ref[...])
        acs_c = acst_ref[...]
        st_ref[0] = state[...]
        last = acs[CHUNK - 1:CHUNK, :]
        xs32 = xc_ref[:, 0:D_INNER].astype(F32)
        xdt = xs32 * dt
        xdt16 = xdt.astype(BF16)
        xdte16 = (xdt * jnp.exp(last - acs)).astype(BF16)
        ea = jnp.exp(acs)
        cd = jnp.exp(last)
        skip = dk_ref[...] * xs32
        causal = (lax.broadcasted_iota(jnp.int32, (CHUNK, CHUNK), 0)
                  >= lax.broadcasted_iota(jnp.int32, (CHUNK, CHUNK), 1))
        left = lax.broadcasted_iota(jnp.int32, (CHUNK, PAIR), 1) < HEAD_DIM
        for g in range(N_GROUPS):
            gl = slice(g * GROUP_LANES, (g + 1) * GROUP_LANES)
            bg = xc_ref[:, D_INNER + g * D_STATE:D_INNER + (g + 1) * D_STATE]
            cg = xc_ref[:, D_INNER + (N_GROUPS + g) * D_STATE:D_INNER + (N_GROUPS + g + 1) * D_STATE]
            cb = lax.dot_general(cg, bg, _DIMS["nt"], preferred_element_type=F32)
            hprev = state[:, gl]
            ch = jnp.dot(cg, hprev.astype(BF16), preferred_element_type=F32)
            for j in range(HEADS_PER_GROUP // 2):
                pl_ = slice(g * GROUP_LANES + j * PAIR, g * GROUP_LANES + (j + 1) * PAIR)
                h0 = g * HEADS_PER_GROUP + 2 * j
                l0, l1 = _pair_decay(acs[:, pl_], acs_c, h0, causal, left)
                lhs = jnp.concatenate([(cb * l0).astype(BF16), (cb * l1).astype(BF16)], axis=1)
                xp = xdt16[:, pl_]
                zero = jnp.zeros_like(xp)
                rhs = jnp.concatenate([jnp.where(left, xp, zero), jnp.where(left, zero, xp)], axis=0)
                ydiag = jnp.dot(lhs, rhs, preferred_element_type=F32)
                y_ref[:, pl_] = (ydiag + ch[:, j * PAIR:(j + 1) * PAIR] * ea[:, pl_] + skip[:, pl_]).astype(BF16)
            s_new = lax.dot_general(bg, xdte16[:, gl], _DIMS["tn"], preferred_element_type=F32)
            state[:, gl] = hprev * cd[:, gl] + s_new

    rows = lambda w: pl.BlockSpec((CHUNK, w), lambda c: (c, 0))
    return _pcall(
        body, name="ssd_fwd",
        out_shape=(jax.ShapeDtypeStruct((t_dim, D_INNER), BF16),
                   jax.ShapeDtypeStruct((nc, D_STATE, D_INNER), F32)),
        grid=(nc,),
        in_specs=[rows(CONV_DIM), rows(128), rows(128), pl.BlockSpec((N_HEADS, CHUNK), lambda c: (0, c)),
                  _const_spec((1, D_INNER)), _const_spec((_EXACT_PIECES * 128, D_INNER))],
        out_specs=(rows(D_INNER), pl.BlockSpec((1, D_STATE, D_INNER), lambda c: (c, 0, 0))),
        scratch_shapes=[pltpu.VMEM((D_STATE, D_INNER), F32)],
        compiler_params=_params("arbitrary"),
    )(xc, dt_r, acs_r, acs_t, dskip_x, e3_mat)


def _ssd_bwd(xc, dt_r, acs_r, sg_r, acs_t, alog_r, dskip_x, e3_mat, et_mat, states, dy):
    t_dim = xc.shape[0]
    nc = t_dim // CHUNK

    def body(xc_ref, dt_ref, acs_ref, sg_ref, acst_ref, ar_ref, dk_ref, e3_ref, et_ref, st_ref, dy_ref,
             dxc_ref, ddt_ref, small_ref, dstate, dacs_ref, dxdt_ref, acc_x, acc_r):
        step = pl.program_id(0)

        @pl.when(step == 0)
        def _():
            dstate[...] = jnp.zeros_like(dstate)
            acc_x[...] = jnp.zeros_like(acc_x)
            acc_r[...] = jnp.zeros_like(acc_r)

        dt_r = dt_ref[...]
        a_r = -jnp.exp(ar_ref[...])
        dt, acs = _expand_heads([dt_r, acs_ref[...]], e3_ref[...])
        acs_c = acst_ref[...]
        last = acs[CHUNK - 1:CHUNK, :]
        xs32 = xc_ref[:, 0:D_INNER].astype(F32)
        xdt = xs32 * dt
        xdt16 = xdt.astype(BF16)
        dte = jnp.exp(last - acs)
        xdte = xdt * dte
        xdte16 = xdte.astype(BF16)
        cd = jnp.exp(last)
        dy16 = dy_ref[...]
        dyv = dy16.astype(F32)
        dye = dyv * jnp.exp(acs)
        dye16 = dye.astype(BF16)
        causal = (lax.broadcasted_iota(jnp.int32, (CHUNK, CHUNK), 0)
                  >= lax.broadcasted_iota(jnp.int32, (CHUNK, CHUNK), 1))
        left = lax.broadcasted_iota(jnp.int32, (CHUNK, PAIR), 1) < HEAD_DIM
        lane_id = lax.broadcasted_iota(jnp.int32, (CHUNK, 128), 1)
        row_id = lax.broadcasted_iota(jnp.int32, (CHUNK, 128), 0)
        is_last_row = lax.broadcasted_iota(jnp.int32, (CHUNK, 1), 0) == CHUNK - 1
        dacs_cols = jnp.zeros((CHUNK, 128), F32)
        dacs_rows = jnp.zeros((CHUNK, 128), F32)
        for g in range(N_GROUPS):
            gl = slice(g * GROUP_LANES, (g + 1) * GROUP_LANES)
            b_lanes = slice(D_INNER + g * D_STATE, D_INNER + (g + 1) * D_STATE)
            c_lanes = slice(D_INNER + (N_GROUPS + g) * D_STATE, D_INNER + (N_GROUPS + g + 1) * D_STATE)
            bg = xc_ref[:, b_lanes]
            cg = xc_ref[:, c_lanes]
            cb = lax.dot_general(cg, bg, _DIMS["nt"], preferred_element_type=F32)
            hprev = st_ref[0, :, gl]
            hp16 = hprev.astype(BF16)
            dhn = dstate[:, gl]
            dhn16 = dhn.astype(BF16)
            ch = jnp.dot(cg, hp16, preferred_element_type=F32)
            gmat = jnp.dot(bg, dhn16, preferred_element_type=F32)
            gx = gmat * xdte[:, gl]
            dlast = jnp.sum(gx, axis=0, keepdims=True) + cd[:, gl] * jnp.sum(dhn * hprev, axis=0, keepdims=True)
            dacs_ref[:, gl] = dye[:, gl] * ch - gx + jnp.where(is_last_row, dlast, 0.0)
            dc_acc = lax.dot_general(dye16[:, gl], hp16, _DIMS["nt"], preferred_element_type=F32)
            db_acc = lax.dot_general(xdte16[:, gl], dhn16, _DIMS["nt"], preferred_element_type=F32)
            dstate[:, gl] = dhn * cd[:, gl] + lax.dot_general(cg, dye16[:, gl], _DIMS["tn"],
                                                             preferred_element_type=F32)
            dcb = jnp.zeros((CHUNK, CHUNK), F32)
            for j in range(HEADS_PER_GROUP // 2):
                pl_ = slice(g * GROUP_LANES + j * PAIR, g * GROUP_LANES + (j + 1) * PAIR)
                h0 = g * HEADS_PER_GROUP + 2 * j
                l0, l1 = _pair_decay(acs[:, pl_], acs_c, h0, causal, left)
                m0, m1 = cb * l0, cb * l1
                lhs = jnp.concatenate([m0.astype(BF16), m1.astype(BF16)], axis=1)
                dyp = dy16[:, pl_]
                zero = jnp.zeros_like(dyp)
                both = lax.dot_general(lhs, dyp, _DIMS["tn"], preferred_element_type=F32)
                dxdt_ref[:, pl_] = (jnp.where(left, both[0:CHUNK, :], both[CHUNK:, :])
                                    + gmat[:, j * PAIR:(j + 1) * PAIR] * dte[:, pl_])
                lhs2 = jnp.concatenate([jnp.where(left, dyp, zero), jnp.where(left, zero, dyp)], axis=0)
                dm = lax.dot_general(lhs2, xdt16[:, pl_], _DIMS["nt"], preferred_element_type=F32)
                dm0, dm1 = dm[0:CHUNK, :], dm[CHUNK:, :]
                dcb = dcb + dm0 * l0 + dm1 * l1
                ds0, ds1 = dm0 * m0, dm1 * m1
                dacs_cols = jnp.where(lane_id == h0, jnp.sum(ds0, axis=1, keepdims=True), dacs_cols)
                dacs_cols = jnp.where(lane_id == h0 + 1, jnp.sum(ds1, axis=1, keepdims=True), dacs_cols)
                dacs_rows = jnp.where(row_id == h0, jnp.sum(ds0, axis=0, keepdims=True), dacs_rows)
                dacs_rows = jnp.where(row_id == h0 + 1, jnp.sum(ds1, axis=0, keepdims=True), dacs_rows)
            dcb16 = dcb.astype(BF16)
            dxc_ref[:, c_lanes] = (dc_acc + jnp.dot(dcb16, bg, preferred_element_type=F32)).astype(BF16)
            dxc_ref[:, b_lanes] = (db_acc + lax.dot_general(dcb16, cg, _DIMS["tn"],
                                                           preferred_element_type=F32)).astype(BF16)
        dxdt = dxdt_ref[...]
        dxc_ref[:, 0:D_INNER] = (dxdt * dt + dk_ref[...] * dyv).astype(BF16)
        acc_x[0:1, :] += jnp.sum(dyv * xs32, axis=0, keepdims=True)
        et = et_ref[...]
        dacs = _reduce_heads(dacs_ref[...], et, 2) + dacs_cols - dacs_rows.T
        dadt = _rev_cumsum_rows(dacs)
        ddraw = (_reduce_heads(dxdt * xs32, et, 1) + dadt * a_r) * sg_ref[...]
        ddraw = jnp.where(lane_id < N_HEADS, ddraw, 0.0)
        ddt_ref[...] = ddraw
        acc_r[0:1, :] += jnp.where(lane_id[0:1, :] < N_HEADS,
                                   jnp.sum(dadt * dt_r, axis=0, keepdims=True) * a_r, 0.0)
        acc_r[1:2, :] += jnp.sum(ddraw, axis=0, keepdims=True)

        @pl.when(step == nc - 1)
        def _():
            dd = _reduce_heads(acc_x[...], et_ref[...], 3)
            rid = lax.broadcasted_iota(jnp.int32, (8, 128), 0)
            small_ref[...] = acc_r[...] + jnp.where(rid == 2, pltpu.roll(dd, 2, 0), 0.0)

    rev = lambda w: pl.BlockSpec((CHUNK, w), lambda c: (nc - 1 - c, 0))
    return _pcall(
        body, name="ssd_bwd",
        out_shape=(jax.ShapeDtypeStruct((t_dim, CONV_DIM), BF16),
                   jax.ShapeDtypeStruct((t_dim, 128), F32),
                   jax.ShapeDtypeStruct((8, 128), F32)),
        grid=(nc,),
        in_specs=[rev(CONV_DIM), rev(128), rev(128), rev(128),
                  pl.BlockSpec((N_HEADS, CHUNK), lambda c: (0, nc - 1 - c)),
                  _const_spec((1, 128)), _const_spec((1, D_INNER)),
                  _const_spec((_EXACT_PIECES * 128, D_INNER)), _const_spec((D_INNER, 128)),
                  pl.BlockSpec((1, D_STATE, D_INNER), lambda c: (nc - 1 - c, 0, 0)),
                  rev(D_INNER)],
        out_specs=(rev(CONV_DIM), rev(128), _const_spec((8, 128))),
        scratch_shapes=[pltpu.VMEM((D_STATE, D_INNER), F32), pltpu.VMEM((CHUNK, D_INNER), F32),
                        pltpu.VMEM((CHUNK, D_INNER), F32), pltpu.VMEM((8, D_INNER), F32),
                        pltpu.VMEM((8, 128), F32)],
        compiler_params=_params("arbitrary"),
    )(xc, dt_r, acs_r, sg_r, acs_t, alog_r, dskip_x, e3_mat, et_mat, states, dy)


_GATE_GROUP = D_INNER // N_GROUPS


def _gate_fwd(y, z, g, *, tb=256):
    t_dim = y.shape[0]

    def body(y_ref, z_ref, g_ref, o_ref):
        for gi in range(N_GROUPS):
            lanes = slice(gi * _GATE_GROUP, (gi + 1) * _GATE_GROUP)
            zv = z_ref[:, lanes].astype(F32)
            wv = y_ref[:, lanes].astype(F32) * (zv * _sigmoid(zv))
            o_ref[:, lanes] = (wv * _rms(wv) * g_ref[:, lanes]).astype(BF16)

    return _pcall(
        body, name="gate_fwd", out_shape=jax.ShapeDtypeStruct((t_dim, D_INNER), BF16),
        grid=(t_dim // tb,),
        in_specs=[_row_spec(tb, D_INNER), _row_spec(tb, D_INNER), _const_spec((1, D_INNER))],
        out_specs=_row_spec(tb, D_INNER), compiler_params=_params("parallel"),
    )(y, z, g)


def _gate_bwd(dyn, y, z, g, *, tb=256):
    t_dim = y.shape[0]

    def body(d_ref, y_ref, z_ref, g_ref, dy_ref, dz_ref, dg_ref):
        @pl.when(pl.program_id(0) == 0)
        def _():
            dg_ref[...] = jnp.zeros_like(dg_ref)

        for gi in range(N_GROUPS):
            lanes = slice(gi * _GATE_GROUP, (gi + 1) * _GATE_GROUP)
            zv = z_ref[:, lanes].astype(F32)
            sg = _sigmoid(zv)
            sz = zv * sg
            yv = y_ref[:, lanes].astype(F32)
            wv = yv * sz
            r = _rms(wv)
            what = wv * r
            dv = d_ref[:, lanes].astype(F32)
            dwhat = dv * g_ref[:, lanes]
            dw = r * (dwhat - what * jnp.mean(dwhat * what, axis=-1, keepdims=True))
            dg_ref[0:1, lanes] += jnp.sum(dv * what, axis=0, keepdims=True)
            dy_ref[:, lanes] = (dw * sz).astype(BF16)
            dz_ref[:, lanes] = (dw * yv * (sg * (1.0 + zv * (1.0 - sg)))).astype(BF16)

    return _pcall(
        body, name="gate_bwd",
        out_shape=(jax.ShapeDtypeStruct((t_dim, D_INNER), BF16),
                   jax.ShapeDtypeStruct((t_dim, D_INNER), BF16),
                   jax.ShapeDtypeStruct((8, D_INNER), F32)),
        grid=(t_dim // tb,),
        in_specs=[_row_spec(tb, D_INNER), _row_spec(tb, D_INNER), _row_spec(tb, D_INNER),
                  _const_spec((1, D_INNER))],
        out_specs=(_row_spec(tb, D_INNER), _row_spec(tb, D_INNER), _const_spec((8, D_INNER))),
        compiler_params=_params("arbitrary"),
    )(dyn, y, z, g)


_ADAM_C1 = 1.0 / (1.0 - ADAM_B1 ** ADAM_STEP)
_ADAM_C2 = 1.0 / (1.0 - ADAM_B2 ** ADAM_STEP)


def _adamw_math(w, g, m, v):
    mn = ADAM_B1 * m + (1.0 - ADAM_B1) * g
    vn = ADAM_B2 * v + (1.0 - ADAM_B2) * (g * g)
    delta = -ADAM_LR * ((mn * _ADAM_C1) / (jnp.sqrt(vn * _ADAM_C2) + ADAM_EPS) + ADAM_WD * w)
    return delta, mn, vn


def _adamw(w, g, m, v, *, name, part=None, into=None):
    r_dim, c = w.shape
    rows = r_dim if part is None else r_dim // 2
    assert g.shape == (rows, c)
    tb = max(t for t in range(8, 513, 8) if rows % t == 0)
    first = 0 if part is None else part * (rows // tb)
    n_out = 3 if part is None else 4

    def body(w_ref, g_ref, m_ref, v_ref, *rest):
        outs = rest[-n_out:]
        gv = g_ref[...]
        outs[0][...], outs[1][...], outs[2][...] = _adamw_math(w_ref[...], gv, m_ref[...], v_ref[...])
        if part is not None:
            outs[3][...] = gv

    spec = pl.BlockSpec((tb, c), lambda i: (first + i, 0))
    sds = jax.ShapeDtypeStruct((r_dim, c), F32)
    in_specs = [spec, _row_spec(tb, c), spec, spec]
    operands = [w, g, m, v]
    aliases = {}
    if into is not None:
        in_specs += [_ANY] * n_out
        operands += list(into)
        aliases = {4 + i: i for i in range(n_out)}
    outs = _pcall(
        body, name=name, out_shape=(sds,) * n_out, grid=(rows // tb,),
        in_specs=in_specs, out_specs=(spec,) * n_out, input_output_aliases=aliases,
        compiler_params=_params("parallel"),
    )(*operands)
    return tuple(outs) if part is not None else tuple(outs) + (g,)


def _adamw_small(params, *, name):
    n = len(params)

    def body(*refs):
        ins, outs = refs[:4 * n], refs[4 * n:]
        for i in range(n):
            w_ref, g_ref, m_ref, v_ref = ins[4 * i:4 * i + 4]
            res = _adamw_math(w_ref[...], g_ref[...], m_ref[...], v_ref[...])
            for o_ref, r in zip(outs[3 * i:3 * i + 3], res):
                o_ref[...] = r

    vmem = pl.BlockSpec(memory_space=pltpu.VMEM)
    flat = [a for p in params for a in p]
    outs = _pcall(
        body, name=name,
        out_shape=tuple(jax.ShapeDtypeStruct(p[0].shape, F32) for p in params for _ in range(3)),
        in_specs=[vmem] * (4 * n), out_specs=(vmem,) * (3 * n),
    )(*flat)
    return [tuple(outs[3 * i:3 * i + 3]) for i in range(n)]


def _pair_sum(grad, recv, place, *, name):
    s_dim, r_dim, c = grad.shape
    half = r_dim // 2
    tb = 256 if half % 256 == 0 else half
    per_half = half // tb

    def body(place_ref, a_ref, b_ref, o16_ref, o32_ref):
        s = a_ref[...] + b_ref[...]
        o16_ref[...] = s.astype(BF16)

        @pl.when(pl.program_id(1) == place_ref[1])
        def _():
            o32_ref[...] = s[0]

    grid_spec = pltpu.PrefetchScalarGridSpec(
        num_scalar_prefetch=1, grid=(per_half, s_dim),
        in_specs=[pl.BlockSpec((1, tb, c), lambda i, s, p: (s, p[0] * per_half + i, 0)),
                  pl.BlockSpec((1, tb, c), lambda i, s, p: (s, i, 0))],
        out_specs=(pl.BlockSpec((1, tb, c), lambda i, s, p: (s, i, 0)),
                   pl.BlockSpec((tb, c), lambda i, s, p: (i, 0))))
    return _pcall(
        body, name=name, grid_spec=grid_spec,
        out_shape=(jax.ShapeDtypeStruct((s_dim, half, c), BF16), jax.ShapeDtypeStruct((half, c), F32)),
        compiler_params=_params("parallel", "arbitrary"),
    )(place, grad, recv)


def _chip_sum(own, recv, place, *, name):
    r_dim, c = own.shape
    tb = 256 if r_dim % 256 == 0 else r_dim

    def body(place_ref, a_ref, b_ref, o_ref):
        s = a_ref[...]
        for k in range(1, N_CHIPS):
            s = s + b_ref[k].astype(F32)
        o_ref[...] = s

    grid_spec = pltpu.PrefetchScalarGridSpec(
        num_scalar_prefetch=1, grid=(r_dim // tb,),
        in_specs=[pl.BlockSpec((tb, c), lambda i, p: (i, 0)),
                  pl.BlockSpec((N_CHIPS, tb, c), lambda i, p: (0, i, 0))],
        out_specs=pl.BlockSpec((None, tb, c), lambda i, p: (p[0], i, 0)))
    return _pcall(
        body, name=name, grid_spec=grid_spec, out_shape=jax.ShapeDtypeStruct((2, r_dim, c), F32),
        compiler_params=_params("parallel"),
    )(place, own, recv)


def _position():
    return lax.axis_index("x"), lax.axis_index("y"), lax.axis_index("c")


def _chip_peer(x, y, k):
    return x ^ (k >> 1), y ^ (k & 1)


_ANY = pl.BlockSpec(memory_space=pl.ANY)
_TOKEN = jax.ShapeDtypeStruct((8, 128), F32)


def _all_gather_weights(shards):
    n = len(shards)
    hops = N_CHIPS - 1

    def body(*refs):
        srcs, outs, done = refs[:n], refs[n:2 * n], refs[2 * n]
        send_sems, recv_sems = refs[2 * n + 1:]
        x, y, c = _position()
        me = 2 * x + y
        done[...] = jnp.zeros_like(done)

        def over_ici(w, k, chip, to):
            return pltpu.make_async_remote_copy(
                src_ref=srcs[w].at[c], dst_ref=outs[w].at[chip, c],
                send_sem=send_sems.at[w, k - 1], recv_sem=recv_sems.at[w, k - 1],
                device_id=to, device_id_type=MESH)

        def over_d2d(w, k, chip, half):
            return pltpu.make_async_remote_copy(
                src_ref=outs[w].at[chip, half], dst_ref=outs[w].at[chip, half],
                send_sem=send_sems.at[w, hops + k - 1], recv_sem=recv_sems.at[w, hops + k - 1],
                device_id=(x, y, 1 - c), device_id_type=MESH)

        sends = []
        for w in range(n):
            for k in range(1, N_CHIPS):
                px, py = _chip_peer(x, y, k)
                cp = over_ici(w, k, me, (px, py, c))
                cp.start()
                sends.append(cp)
        for w in range(n):
            for k in range(1, N_CHIPS):
                px, py = _chip_peer(x, y, k)
                over_ici(w, k, 2 * px + py, (px, py, c)).wait_recv()
                cp = over_d2d(w, k, 2 * px + py, c)
                cp.start()
                sends.append(cp)
        for w in range(n):
            for k in range(1, N_CHIPS):
                px, py = _chip_peer(x, y, k)
                over_d2d(w, k, 2 * px + py, 1 - c).wait_recv()
        for cp in sends:
            cp.wait_send()

    outs = _pcall(
        body, name="gather_weights",
        out_shape=tuple(jax.ShapeDtypeStruct((N_CHIPS,) + s.shape, s.dtype) for s in shards) + (_TOKEN,),
        in_specs=[_ANY] * n, out_specs=(_ANY,) * n + (pl.BlockSpec(memory_space=pltpu.VMEM),),
        scratch_shapes=[pltpu.SemaphoreType.DMA((n, 2 * hops)),
                        pltpu.SemaphoreType.DMA((n, 2 * hops))],
    )(*shards)
    return outs[:n], outs[n][0, 0]


def _pair_copies(srcs, lands, send_sems, recv_sems):
    x, y, c = _position()
    copies = []
    for w in range(len(srcs)):
        half = srcs[w].shape[1] // 2
        copies.append(pltpu.make_async_remote_copy(
            src_ref=srcs[w].at[:, pl.ds((1 - c) * half, half), :], dst_ref=lands[w],
            send_sem=send_sems.at[w], recv_sem=recv_sems.at[w],
            device_id=(x, y, 1 - c), device_id_type=MESH))
    return copies


def _chip_copies(srcs, lands, send_sems, recv_sems):
    x, y, c = _position()
    copies = []
    for w in range(len(srcs)):
        for k in range(1, N_CHIPS):
            px, py = _chip_peer(x, y, k)
            i = w * (N_CHIPS - 1) + k - 1
            copies.append(pltpu.make_async_remote_copy(
                src_ref=srcs[w].at[2 * px + py], dst_ref=lands[w].at[k],
                send_sem=send_sems.at[i], recv_sem=recv_sems.at[i],
                device_id=(px, py, c), device_id_type=MESH))
    return copies


def _gather_copies(srcs, lands, send_sems, recv_sems):
    x, y, c = _position()
    me = 2 * x + y
    copies = []
    for w in range(len(srcs)):
        for k in range(1, N_CHIPS):
            px, py = _chip_peer(x, y, k)
            i = w * (N_CHIPS - 1) + k - 1
            copies.append(pltpu.make_async_remote_copy(
                src_ref=srcs[w].at[c], dst_ref=lands[w].at[me, c],
                send_sem=send_sems.at[i], recv_sem=recv_sems.at[i],
                device_id=(px, py, c), device_id_type=MESH))
    return copies


def _exchange(name, copies_of, n_copies, srcs, land_shapes):
    n = len(srcs)

    def body(*refs):
        copies = copies_of(refs[:n], refs[n:2 * n], refs[2 * n], refs[2 * n + 1])
        for cp in copies:
            cp.start()
        for cp in copies:
            cp.wait_recv()
        for cp in copies:
            cp.wait_send()

    return _pcall(
        body, name=name, out_shape=tuple(land_shapes),
        in_specs=[_ANY] * n, out_specs=(_ANY,) * n,
        scratch_shapes=[pltpu.SemaphoreType.DMA((n_copies,)), pltpu.SemaphoreType.DMA((n_copies,))],
    )(*srcs)


_HBM = pl.BlockSpec(memory_space=pltpu.HBM)
_SEM = pl.BlockSpec(memory_space=pltpu.SEMAPHORE)
_DATAFLOW = pltpu.SideEffectType.DATAFLOW_SIDE_EFFECTING


def _exchange_start(name, copies_of, n_copies, srcs, land_shapes):
    n = len(srcs)
    lands = [lax.empty(s.shape, s.dtype) for s in land_shapes]

    def body(*refs):
        for cp in copies_of(refs[:n], refs[n:2 * n], refs[2 * n], refs[2 * n + 1]):
            cp.start()
        refs[-1][...] = jnp.zeros_like(refs[-1])

    through = [pltpu.HBM(a.shape, a.dtype) for a in list(srcs) + lands]
    outs = _pcall(
        body, name=name,
        out_shape=(pltpu.SemaphoreType.DMA((n_copies,)), pltpu.SemaphoreType.DMA((n_copies,)),
                   *through, jax.ShapeDtypeStruct((8, 128), F32)),
        in_specs=[_HBM] * (2 * n),
        out_specs=(_SEM, _SEM, *([_HBM] * (2 * n)), pl.BlockSpec(memory_space=pltpu.VMEM)),
        input_output_aliases={i: 2 + i for i in range(2 * n)},
        compiler_params=pltpu.CompilerParams(has_side_effects=_DATAFLOW),
    )(*[pltpu.with_memory_space_constraint(a, pltpu.HBM) for a in list(srcs) + lands])
    return outs[:-1], outs[-1][0, 0]


def _exchange_wait(name, copies_of, state, after):
    send_sems, recv_sems, through = state[0], state[1], state[2:]
    n = len(through) // 2
    if after.ndim == 0:
        after = jnp.broadcast_to(after, (8, 128))
    after = pltpu.with_memory_space_constraint(after, pltpu.HBM)

    def body(*refs):
        for cp in copies_of(refs[:n], refs[n:2 * n], refs[2 * n], refs[2 * n + 1]):
            cp.wait_send()
            cp.wait_recv()

    outs = _pcall(
        body, name=name,
        out_shape=tuple(pltpu.HBM(a.shape, a.dtype) for a in through),
        in_specs=[_HBM] * (2 * n) + [_SEM, _SEM, _HBM], out_specs=tuple([_HBM] * (2 * n)),
        input_output_aliases={i: i for i in range(2 * n)},
        compiler_params=pltpu.CompilerParams(has_side_effects=_DATAFLOW),
    )(*through, send_sems, recv_sems, after)
    return outs[:n], outs[n:]


def _forward_halves(lands, *, name):
    n = len(lands)
    hops = N_CHIPS - 1

    def body(*refs):
        ins, outs, done = refs[:n], refs[n:2 * n], refs[2 * n]
        send_sems, recv_sems = refs[2 * n + 1], refs[2 * n + 2]
        x, y, c = _position()
        done[...] = jnp.zeros_like(done)
        copies = []
        for w in range(n):
            for k in range(1, N_CHIPS):
                px, py = _chip_peer(x, y, k)
                i = w * hops + k - 1
                copies.append(pltpu.make_async_remote_copy(
                    src_ref=ins[w].at[2 * px + py, c], dst_ref=outs[w].at[2 * px + py, c],
                    send_sem=send_sems.at[i], recv_sem=recv_sems.at[i],
                    device_id=(x, y, 1 - c), device_id_type=MESH))
        for cp in copies:
            cp.start()
        for cp in copies:
            cp.wait_recv()
        for cp in copies:
            cp.wait_send()

    outs = _pcall(
        body, name=name,
        out_shape=tuple(jax.ShapeDtypeStruct(a.shape, a.dtype) for a in lands) + (_TOKEN,),
        in_specs=[_ANY] * n, out_specs=(_ANY,) * n + (pl.BlockSpec(memory_space=pltpu.VMEM),),
        input_output_aliases={i: i for i in range(n)},
        scratch_shapes=[pltpu.SemaphoreType.DMA((n * hops,)), pltpu.SemaphoreType.DMA((n * hops,))],
    )(*lands)
    return outs[:n], outs[n][0, 0]


def _pair_lands(grads):
    return [jax.ShapeDtypeStruct((g.shape[0], g.shape[1] // 2, g.shape[2]), F32) for g in grads]


def _same_lands(parts):
    return [jax.ShapeDtypeStruct(p.shape, p.dtype) for p in parts]


def _pair_gather_halves(halves, *, name):
    n = len(halves)

    def body(*refs):
        ins, outs = refs[:n], refs[n:2 * n]
        send_sems, recv_sems = refs[2 * n:]
        x, y, c = _position()
        sends = []
        for w in range(n):
            cp = pltpu.make_async_remote_copy(
                src_ref=ins[w].at[c], dst_ref=outs[w].at[c],
                send_sem=send_sems.at[w], recv_sem=recv_sems.at[w],
                device_id=(x, y, 1 - c), device_id_type=MESH)
            cp.start()
            sends.append(cp)
        for cp in sends:
            cp.wait_recv()
        for cp in sends:
            cp.wait_send()

    whole = _pcall(
        body, name=name,
        out_shape=tuple(jax.ShapeDtypeStruct(h.shape, F32) for h in halves),
        in_specs=[_ANY] * n, out_specs=(_ANY,) * n,
        input_output_aliases={i: i for i in range(n)},
        scratch_shapes=[pltpu.SemaphoreType.DMA((n,)), pltpu.SemaphoreType.DMA((n,))],
    )(*halves)
    return [w.reshape(2 * w.shape[1], w.shape[2]) for w in whole]


def _all_reduce_small(packed, *, name, sum_row0):
    r_dim, c = packed.shape

    def body(src_ref, out_ref, recv_ref, send_sems, recv_sems):
        x, y, c_ = _position()
        me = 4 * x + 2 * y + c_
        recv_ref[0] = src_ref[...]
        sends = []
        for k in range(1, N_DEV):
            peer = (x ^ (k >> 2), y ^ ((k >> 1) & 1), c_ ^ (k & 1))
            cp = pltpu.make_async_remote_copy(
                src_ref=src_ref, dst_ref=recv_ref.at[k],
                send_sem=send_sems.at[k - 1], recv_sem=recv_sems.at[k - 1],
                device_id=peer, device_id_type=MESH)
            cp.start()
            sends.append(cp)
        for cp in sends:
            cp.wait_recv()
        total = recv_ref[me]
        for d in range(1, N_DEV):
            total = total + recv_ref[d ^ me]
        if sum_row0:
            row0 = jnp.sum(total[0:1, :], axis=1, keepdims=True)
            rid = lax.broadcasted_iota(jnp.int32, total.shape, 0)
            total = jnp.where(rid == 0, row0, total)
        out_ref[...] = total
        for cp in sends:
            cp.wait_send()

    return _pcall(
        body, name=name, out_shape=jax.ShapeDtypeStruct((r_dim, c), F32),
        in_specs=[pl.BlockSpec(memory_space=pltpu.VMEM)],
        out_specs=pl.BlockSpec(memory_space=pltpu.VMEM),
        scratch_shapes=[pltpu.VMEM((N_DEV, r_dim, c), F32),
                        pltpu.SemaphoreType.DMA((N_DEV - 1,)), pltpu.SemaphoreType.DMA((N_DEV - 1,))],
    )(packed)


def _pad_lanes(v, width):
    return jnp.pad(v, ((0, 0), (0, width - v.shape[1])))


def _pad_rows(v, rows):
    pad = [(0, 0)] * v.ndim
    pad[-2] = (0, rows - v.shape[-2])
    return jnp.pad(v, pad)


_IN_PROJ_SHARD_ROWS = 1312


def _rows_1024(v):
    flat = v.reshape(-1)
    pad = (-flat.shape[0]) % D_MODEL
    return jnp.pad(flat, (0, pad)).reshape(-1, D_MODEL)


def _local_step(xs, target, pw, fetch, reduce_start, reduce_midway,
                conv_w, conv_b, gate_g,
                norm_mix_g, norm_mlp_g, pool_b, pool_scale, ssm_dt_bias, ssm_a_log, ssm_d, final_g):
    bias_r = _pad_lanes(ssm_dt_bias, 128)
    alog_r = _pad_lanes(ssm_a_log, 128)
    dskip_x = jnp.repeat(ssm_d, HEAD_DIM, axis=1)
    bias_c = ssm_dt_bias.reshape(N_HEADS, 1)
    alog_c = ssm_a_log.reshape(N_HEADS, 1)
    e_mat = _head_lane_matrix()
    e3_mat = jnp.tile(e_mat, (_EXACT_PIECES, 1))

    g_mix0, g_mix1 = norm_mix_g[0:1], norm_mix_g[1:2]
    g_mlp0, g_mlp1 = norm_mlp_g[0:1], norm_mlp_g[1:2]
    fg = final_g.reshape(1, D_MODEL)

    h1 = _pool_fwd(xs, g_mix0, pw, pool_b, pool_scale)
    w1_0 = fetch("mlp0_up", h1)
    u0, hm0 = _matmul(h1, w1_0, "nn", name="mlp0_up", out_dtype=BF16, b_col_shards=True, a_norm=g_mlp0)
    w2_0 = fetch("mlp0_down", u0)
    h2 = _matmul(u0, w2_0, "nn", name="mlp0_down", a_relu2=True, add=h1)

    w_z, w_xbc, w_dt = fetch("in_proj", h2)
    xbc, hn1 = _matmul(h2, w_xbc, "nt", name="in_proj_xbc", out_dtype=BF16, a_norm=g_mix1)
    z = _matmul(hn1, w_z, "nt", name="in_proj_z", out_dtype=BF16)
    dt_raw = _matmul(hn1, w_dt, "nt", name="in_proj_dt")
    dt_raw_t = dt_raw[:, :N_HEADS].T
    xc = _conv_fwd(xbc, conv_w, conv_b)
    wout, w1_1, w2_1 = fetch("rest", xc)
    dt_r, acs_r, sg_r, acs_t = _ssd_decay(dt_raw, dt_raw_t, bias_r, bias_c, alog_r, alog_c)
    y, states = _ssd_fwd(xc, dt_r, acs_r, acs_t, dskip_x, e3_mat)
    yn = _gate_fwd(y, z, gate_g)
    h3 = _matmul(yn, wout, "nn", name="out_proj", add=h2)
    u1, hm1 = _matmul(h3, w1_1, "nn", name="mlp1_up", out_dtype=BF16, b_col_shards=True, a_norm=g_mlp1)

    dh4, dh4_16, small_final = _matmul(u1, w2_1, "nn", name="mlp1_down", a_relu2=True, add=h3,
                                       loss_head=(fg, target))

    def mlp_bwd_weights(dh_out16, hm, u, w2_i, tag):
        du = _matmul(dh_out16, w2_i, "nt", name=tag + "_du", out_dtype=BF16, relu2_grad_of=u)
        dw2 = _matmul(u, dh_out16, "tn", name=tag + "_dw2", a_relu2=True)
        dw1 = _matmul(hm, du, "tn", name=tag + "_dw1", out_col_shards=N_CHIPS)
        return du, dw1, dw2.reshape(N_CHIPS, D_FF // N_CHIPS, D_MODEL)

    def mlp_bwd_input(du, dh_out, h_in, w1_i, g_i, tag):
        return _matmul(du, w1_i, "nt", name=tag + "_dhm", b_col_shards=True, norm_bwd=(h_in, g_i, dh_out))

    du1, dw1_1, dw2_1 = mlp_bwd_weights(dh4_16, hm1, u1, w2_1, "mlp1")
    dh3, dh3_16, dg_mlp1 = mlp_bwd_input(du1, dh4, h3, w1_1, g_mlp1, "mlp1")

    dyn = _matmul(dh3_16, wout, "nt", name="out_proj_dyn", out_dtype=BF16)
    dwout = _matmul(yn, dh3_16, "tn", name="out_proj_dw").reshape(N_CHIPS, D_INNER // N_CHIPS, D_MODEL)
    behind = reduce_start("mlp1_out", [dw1_1, dw2_1, dwout])
    dy, dz, dg_gate = _gate_bwd(dyn, y, z, gate_g + behind)
    behind = reduce_midway("mlp1_out", dz)
    dxc, ddt_raw, small_ssd = _ssd_bwd(xc, dt_r, acs_r, sg_r, acs_t, alog_r, dskip_x + behind,
                                       e3_mat, e_mat.T, states, dy)
    dv, dconv = _conv_bwd_act(xbc, dxc, conv_w, conv_b)
    dxbc = _conv_bwd_in(dv, conv_w)
    dw_z = _matmul(dz, hn1, "tn", name="in_proj_z_dw")
    dw_xbc = _matmul(dxbc, hn1, "tn", name="in_proj_xbc_dw")
    dw_dt = _matmul(ddt_raw, hn1, "tn", name="in_proj_dt_dw")
    dwin = jnp.concatenate([dw_z, dw_xbc, dw_dt[:N_HEADS]], axis=0)
    dwin = _pad_rows(dwin.reshape(N_CHIPS, IN_PROJ_DIM // N_CHIPS, D_MODEL), _IN_PROJ_SHARD_ROWS)
    behind = reduce_start("in_proj", [dwin])
    dh2, dh2_16, dg_mix1 = _matmul(dxbc, w_xbc, "nn", name="in_proj_dh", more=[(dz, w_z), (ddt_raw, w_dt)],
                                   norm_bwd=(h2, g_mix1 + behind, dh3))
    behind = reduce_midway("in_proj", dh2_16)

    du0, dw1_0, dw2_0 = mlp_bwd_weights(dh2_16, hm0, u0, w2_0, "mlp0")
    behind = behind + reduce_start("mlp0", [dw1_0, dw2_0])
    dh1, dh1_16, dg_mlp0 = mlp_bwd_input(du0, dh2, h1, w1_0, g_mlp0 + behind, "mlp0")
    behind = reduce_midway("mlp0", dh1_16)
    dx, dpw, small_pool = _pool_bwd(xs, g_mix0 + behind, pw, pool_b, pool_scale, dh1)
    dpw = jnp.transpose(dpw.reshape(4, N_CHIPS, POOL_GROUP // N_CHIPS, POOL_GROUP), (1, 0, 2, 3))
    dpw = dpw.reshape(N_CHIPS, 4 * (POOL_GROUP // N_CHIPS), POOL_GROUP)

    big = [dpw]
    rows = [
        small_final[1:2],
        small_final[0:1],
        small_pool[0:1], dg_mix1[0:1],
        dg_mlp0[0:1], dg_mlp1[0:1],
        small_pool[1:2], small_pool[2:3],
        _pad_lanes(small_ssd[0:3], D_MODEL),
        _rows_1024(dg_gate[0:1]),
        _rows_1024(dconv[0:CONV_K]),
        _rows_1024(dconv[CONV_K:CONV_K + 1]),
    ]
    return dx, big, rows


def kernel(x, norm_mix_g, norm_mlp_g, pool_w, pool_b, pool_scale, ssm_w_in, ssm_conv_w, ssm_conv_b, ssm_dt_bias, ssm_a_log, ssm_d, ssm_norm_g, ssm_w_out, mlp_w1, mlp_w2, final_g, loss_target, m_norm_mix_g, m_norm_mlp_g, m_pool_w, m_pool_b, m_pool_scale, m_ssm_w_in, m_ssm_conv_w, m_ssm_conv_b, m_ssm_dt_bias, m_ssm_a_log, m_ssm_d, m_ssm_norm_g, m_ssm_w_out, m_mlp_w1, m_mlp_w2, m_final_g, v_norm_mix_g, v_norm_mlp_g, v_pool_w, v_pool_b, v_pool_scale, v_ssm_w_in, v_ssm_conv_w, v_ssm_conv_b, v_ssm_dt_bias, v_ssm_a_log, v_ssm_d, v_ssm_norm_g, v_ssm_w_out, v_mlp_w1, v_mlp_w2, v_final_g):
    xs = x[0]
    target = loss_target[0]
    my_x, my_y, my_c = _position()
    my_chip = 2 * my_x + my_y

    def halves(w):
        return w.astype(BF16).reshape((2, w.shape[0] // 2) + w.shape[1:])

    def whole(gathered, own_shard):
        g = lax.dynamic_update_index_in_dim(gathered, own_shard, my_chip, axis=0)
        return g.reshape((N_CHIPS, 2 * g.shape[2]) + g.shape[3:])

    def gather_lands(own):
        return [jax.ShapeDtypeStruct((N_CHIPS,) + s.shape, s.dtype) for s in own]

    vec_cols = CONV_DIM // N_CHIPS
    vec_own = jnp.concatenate([ssm_conv_w[0], ssm_conv_b, _pad_lanes(ssm_norm_g, vec_cols)], axis=0)
    early_own = [halves(pool_w[0]), vec_own.reshape(2, (CONV_K + 2) // 2, vec_cols)]
    early, behind_early = _all_gather_weights(early_own)
    g_pool, g_vec = [whole(g, o) for g, o in zip(early, early_own)]
    pw = jnp.transpose(g_pool, (1, 0, 2, 3)).reshape(4, POOL_GROUP, POOL_GROUP)
    conv_w = jnp.transpose(g_vec[:, 0:CONV_K, :], (1, 0, 2)).reshape(CONV_K, CONV_DIM)
    conv_b = g_vec[:, CONV_K, :].reshape(1, CONV_DIM)
    gate_g = g_vec[:, CONV_K + 1, :D_INNER // N_CHIPS].reshape(1, D_INNER)

    def behind_it(zero, ws):
        return [halves(w + zero) for w in ws]

    fetches = {}
    up_own = behind_it(behind_early, [mlp_w1[0]])
    fetches["mlp0_up"], behind_gather = _exchange_start(
        "gather_mlp0_up_start", _gather_copies, len(up_own) * (N_CHIPS - 1), up_own, gather_lands(up_own))
    down_own = behind_it(behind_gather, [mlp_w2[0]])
    fetches["mlp0_down"], behind_gather = _exchange_start(
        "gather_mlp0_down_start", _gather_copies, len(down_own) * (N_CHIPS - 1), down_own, gather_lands(down_own))
    in_own = behind_it(behind_gather, [_pad_rows(ssm_w_in[0].T, _IN_PROJ_SHARD_ROWS)])
    fetches["in_proj"], behind_gather = _exchange_start(
        "gather_in_proj_start", _gather_copies, len(in_own) * (N_CHIPS - 1), in_own, gather_lands(in_own))

    def fetch(what, after):
        if what == "mlp0_up":
            own_thru, landed = _exchange_wait("gather_mlp0_up_wait", _gather_copies, fetches[what], after)
            landed, _ = _forward_halves(landed, name="forward_mlp0_up")
            return whole(landed[0], own_thru[0])
        if what == "mlp0_down":
            own_thru, landed = _exchange_wait("gather_mlp0_down_wait", _gather_copies, fetches[what], after)
            landed, _ = _forward_halves(landed, name="forward_mlp0_down")
            return whole(landed[0], own_thru[0]).reshape(D_FF, D_MODEL)
        if what == "in_proj":
            own_thru, landed = _exchange_wait("gather_in_proj_wait", _gather_copies, fetches["in_proj"], after)
            landed, behind = _forward_halves(landed, name="forward_in_proj")
            rest = behind_it(behind, [ssm_w_out[0], mlp_w1[1], mlp_w2[1]])
            fetches["rest"], behind = _exchange_start(
                "gather_rest_start", _gather_copies, len(rest) * (N_CHIPS - 1), rest, gather_lands(rest))
            win = whole(landed[0], own_thru[0])[:, :IN_PROJ_DIM // N_CHIPS].reshape(IN_PROJ_DIM, D_MODEL)
            w_dt = _pad_rows(win[D_INNER + CONV_DIM:], 128) + behind.astype(BF16)
            return win[:D_INNER], win[D_INNER:D_INNER + CONV_DIM], w_dt
        own_thru, landed = _exchange_wait("gather_rest_wait", _gather_copies, fetches["rest"], after)
        landed, _ = _forward_halves(landed, name="forward_rest")
        g_wout, w1_1, g_w2_1 = [whole(g, o) for g, o in zip(landed, own_thru)]
        return g_wout.reshape(D_INNER, D_MODEL), w1_1, g_w2_1.reshape(D_FF, D_MODEL)

    place = jnp.stack([my_c, my_chip]).astype(jnp.int32)
    waves = {}

    def reduce_start(wave, grads):
        waves[wave] = {}
        waves[wave]["pair"], behind = _exchange_start(
            "pair_%s_start" % wave, _pair_copies, len(grads), grads, _pair_lands(grads))
        return behind

    def reduce_midway(wave, after):
        st = waves[wave]
        grads, recv = _exchange_wait("pair_%s_wait" % wave, _pair_copies, st["pair"], after)
        sums = [_pair_sum(g, r, place, name="pair_sum_%s_%d" % (wave, i))
                for i, (g, r) in enumerate(zip(grads, recv))]
        st["f32"] = [s32 for _, s32 in sums]
        b16 = [s16 for s16, _ in sums]
        st["chip"], behind = _exchange_start(
            "chip_%s_start" % wave, _chip_copies, len(b16) * (N_CHIPS - 1), b16, _same_lands(b16))
        return behind

    def reduce_finish(wave, after):
        st = waves[wave]
        _, got = _exchange_wait("chip_%s_wait" % wave, _chip_copies, st["chip"], after)
        return [_chip_sum(s32, r, place, name="chip_sum_%s_%d" % (wave, i))
                for i, (s32, r) in enumerate(zip(st["f32"], got))]

    dx, big0, rows = _local_step(xs, target, pw, fetch, reduce_start, reduce_midway,
                                 conv_w, conv_b, gate_g,
                                 norm_mix_g + behind_gather, norm_mlp_g, pool_b, pool_scale,
                                 ssm_dt_bias, ssm_a_log, ssm_d, final_g)

    behind = reduce_start("pool", big0)
    small = jnp.concatenate(rows, axis=0)
    small = jnp.pad(small, ((0, (-small.shape[0]) % 8), (0, 0))) + behind
    small = _all_reduce_small(small, name="all_reduce_small", sum_row0=True)
    behind = reduce_midway("pool", small)
    h_w1_1, h_w2_1, h_wout = reduce_finish("mlp1_out", behind)
    (h_win,) = reduce_finish("in_proj", behind)
    g_w1_1, g_w2_1, g_wout_s, g_win_s = _pair_gather_halves([h_w1_1, h_w2_1, h_wout, h_win],
                                                            name="pair_gather_layer1")
    loss = small[0, 0]
    g_final = small[1]
    g_norm_mix = small[2:4]
    g_norm_mlp = small[4:6]
    g_pool_b, g_pool_scale = small[6:7], small[7:8]
    g_alog, g_dtb, g_dsk = small[8:9, :N_HEADS], small[9:10, :N_HEADS], small[10:11, :N_HEADS]
    g_gate_full = small[11:13].reshape(1, D_INNER)
    g_convw_full = small[13:25].reshape(CONV_K, CONV_DIM)
    g_convb_full = small[25:28].reshape(1, CONV_DIM)
    g_gate = lax.dynamic_slice_in_dim(g_gate_full, my_chip * (D_INNER // N_CHIPS), D_INNER // N_CHIPS, axis=1)
    g_convw = lax.dynamic_slice_in_dim(g_convw_full, my_chip * (CONV_DIM // N_CHIPS), CONV_DIM // N_CHIPS, axis=1)
    g_convb = lax.dynamic_slice_in_dim(g_convb_full, my_chip * (CONV_DIM // N_CHIPS), CONV_DIM // N_CHIPS, axis=1)

    grads = {
        "norm_mix_g": g_norm_mix, "norm_mlp_g": g_norm_mlp,
        "pool_b": g_pool_b, "pool_scale": g_pool_scale,
        "ssm_conv_w": g_convw.reshape(ssm_conv_w.shape),
        "ssm_conv_b": g_convb, "ssm_dt_bias": g_dtb, "ssm_a_log": g_alog, "ssm_d": g_dsk,
        "ssm_norm_g": g_gate, "ssm_w_out": g_wout_s.reshape(ssm_w_out.shape),
        "final_g": g_final,
    }
    weights = dict(norm_mix_g=norm_mix_g, norm_mlp_g=norm_mlp_g, pool_w=pool_w, pool_b=pool_b,
                   pool_scale=pool_scale, ssm_w_in=ssm_w_in, ssm_conv_w=ssm_conv_w, ssm_conv_b=ssm_conv_b,
                   ssm_dt_bias=ssm_dt_bias, ssm_a_log=ssm_a_log, ssm_d=ssm_d, ssm_norm_g=ssm_norm_g,
                   ssm_w_out=ssm_w_out, mlp_w1=mlp_w1, mlp_w2=mlp_w2, final_g=final_g)
    moms = dict(norm_mix_g=(m_norm_mix_g, v_norm_mix_g), norm_mlp_g=(m_norm_mlp_g, v_norm_mlp_g),
                pool_w=(m_pool_w, v_pool_w), pool_b=(m_pool_b, v_pool_b),
                pool_scale=(m_pool_scale, v_pool_scale), ssm_w_in=(m_ssm_w_in, v_ssm_w_in),
                ssm_conv_w=(m_ssm_conv_w, v_ssm_conv_w), ssm_conv_b=(m_ssm_conv_b, v_ssm_conv_b),
                ssm_dt_bias=(m_ssm_dt_bias, v_ssm_dt_bias), ssm_a_log=(m_ssm_a_log, v_ssm_a_log),
                ssm_d=(m_ssm_d, v_ssm_d), ssm_norm_g=(m_ssm_norm_g, v_ssm_norm_g),
                ssm_w_out=(m_ssm_w_out, v_ssm_w_out), mlp_w1=(m_mlp_w1, v_mlp_w1),
                mlp_w2=(m_mlp_w2, v_mlp_w2), final_g=(m_final_g, v_final_g))
    names = list(weights)
    big_names = ("pool_w", "ssm_w_in", "ssm_w_out", "mlp_w1", "mlp_w2")
    deltas, new_m, new_v = {}, {}, {}

    def as_rows(nm, a):
        return a[0].T if nm == "ssm_w_in" else a.reshape(-1, a.shape[-1])

    def from_rows(nm, r):
        return r.T[None] if nm == "ssm_w_in" else r.reshape(weights[nm].shape)

    def update(nm, grad_rows, layer=None, into=None):
        return _adamw(as_rows(nm, weights[nm]), grad_rows, as_rows(nm, moms[nm][0]), as_rows(nm, moms[nm][1]),
                      name="adamw_%s_%s" % (nm, layer), part=layer, into=into)

    def keep(nm, results):
        deltas[nm], new_m[nm], new_v[nm], grads[nm] = [from_rows(nm, r) for r in results]

    keep("ssm_w_in", update("ssm_w_in", g_win_s[:IN_PROJ_DIM // N_CHIPS]))
    keep("ssm_w_out", update("ssm_w_out", g_wout_s))
    w1_done = update("mlp_w1", g_w1_1, layer=1)
    w2_done = update("mlp_w2", g_w2_1, layer=1)
    small_names = [nm for nm in names if nm not in big_names]
    small_done = _adamw_small(
        [tuple(as_rows(nm, a) for a in (weights[nm], grads[nm], moms[nm][0], moms[nm][1])) for nm in small_names],
        name="adamw_small")
    for nm, (d_, m_, v_) in zip(small_names, small_done):
        deltas[nm], new_m[nm], new_v[nm] = [from_rows(nm, r) for r in (d_, m_, v_)]

    above = (deltas["ssm_w_in"][0, 0, 0] + deltas["ssm_w_out"][0, 0, 0] + w1_done[0][-1, -1]
             + w2_done[0][-1, -1] + small_done[0][0][0, 0])
    g_pool_w, g_w1_0, g_w2_0 = _pair_gather_halves(reduce_finish("pool", above) + reduce_finish("mlp0", above),
                                                   name="pair_gather_layer0")
    keep("mlp_w1", update("mlp_w1", g_w1_0, layer=0, into=w1_done))
    keep("mlp_w2", update("mlp_w2", g_w2_0, layer=0, into=w2_done))
    keep("pool_w", update("pool_w", g_pool_w))

    grad_x = dx.reshape(x.shape)
    out_grads = [grads[nm].reshape(weights[nm].shape) for nm in names]
    return (loss, grad_x, *out_grads, *[deltas[nm] for nm in names],
            *[new_m[nm] for nm in names], *[new_v[nm] for nm in names])
```

```python
import jax
import jax.numpy as jnp
from jax import lax
from jax.experimental import pallas as pl
from jax.experimental.pallas import tpu as pltpu

F32 = jnp.float32
BF16 = jnp.bfloat16
MESH = pl.DeviceIdType.MESH

D_MODEL = 1024
RMS_EPS = 1e-5
POOL_WINDOWS = (2, 4, 8, 16)
POOL_GROUP = 256
POOL_HALO = 16
D_INNER = 2048
HEAD_DIM = 64
N_HEADS = 32
N_GROUPS = 4
HEADS_PER_GROUP = 8
D_STATE = 128
CONV_K = 4
CONV_HALO = 8
CHUNK = 128
CONV_DIM = 3072
IN_PROJ_DIM = 5152
D_FF = 4096
N_CHIPS = 4
N_DEV = 8

ADAM_LR = 0.001
ADAM_B1 = 0.9
ADAM_B2 = 0.999
ADAM_EPS = 1e-08
ADAM_WD = 0.01
ADAM_STEP = 10

VMEM_LIMIT = 56 * 1024 * 1024
NEG_INF = float("-inf")


def _pcall(body, **kw):
    return pl.pallas_call(body, **kw)


def _params(*sem):
    return pltpu.CompilerParams(dimension_semantics=sem, vmem_limit_bytes=VMEM_LIMIT)


def _sigmoid(v):
    return 1.0 / (1.0 + jnp.exp(-v))


def _row_spec(tb, d, nb=None, reverse=False):
    if reverse:
        return pl.BlockSpec((tb, d), lambda i: (nb - 1 - i, 0))
    return pl.BlockSpec((tb, d), lambda i: (i, 0))


def _const_spec(shape):
    return pl.BlockSpec(shape, lambda *_: tuple(0 for _ in shape))


_DIMS = {"nn": (((1,), (0,)), ((), ())),
         "nt": (((1,), (1,)), ((), ())),
         "tn": (((0,), (0,)), ((), ()))}


_MATMUL_VMEM_BUDGET = 40 * 1024 * 1024


def _matmul_tiles(m_dim, n_dim, k_dim, a_bytes, b_bytes, mn_bytes):
    tm, tn = min(m_dim, 2048), min(n_dim, 1024)
    while m_dim % tm:
        tm //= 2
    while 2 * (tm * k_dim * a_bytes + tn * k_dim * b_bytes + tm * tn * mn_bytes) > _MATMUL_VMEM_BUDGET:
        if tm >= tn:
            tm //= 2
        else:
            tn //= 2
    return tm, tn


def _matmul(a, b, mode, *, name, out_dtype=F32, a_relu2=False, a_norm=None, add=None, relu2_grad_of=None,
            out_col_shards=1, b_col_shards=False, norm_bwd=None, loss_head=None, more=()):
    if mode == "tn":
        k_dim, m_dim = a.shape
    else:
        m_dim, k_dim = a.shape
    if b_col_shards:
        n_shards, shard_cols = b.shape[0], b.shape[2]
        n_dim = n_shards * shard_cols if mode == "nn" else b.shape[1]
    else:
        n_dim = b.shape[0] if mode == "nt" else b.shape[1]
    mn_bytes = jnp.dtype(out_dtype).itemsize
    if relu2_grad_of is not None:
        mn_bytes += relu2_grad_of.dtype.itemsize
    if add is not None:
        mn_bytes += add.dtype.itemsize
    row_epilogue = norm_bwd is not None or loss_head is not None
    if row_epilogue:
        assert out_dtype == F32 and out_col_shards == 1 and (norm_bwd is None or loss_head is None)
        mn_bytes += 4 + 4 + 2 + 8
    a_bytes = a.dtype.itemsize + (2 if a_norm is not None else 0)
    k_all = k_dim + sum(a_i.shape[1] for a_i, _ in more)
    tm, tn = _matmul_tiles(m_dim, n_dim, k_all, a_bytes, b.dtype.itemsize, mn_bytes)
    if row_epilogue:
        while tn < n_dim:
            tm, tn = tm // 2, tn * 2
    assert m_dim % tm == 0 and n_dim % tn == 0
    a_spec = (pl.BlockSpec((k_dim, tm), lambda i, j: (0, i)) if mode == "tn"
              else pl.BlockSpec((tm, k_dim), lambda i, j: (i, 0)))
    if b_col_shards and mode == "nn":
        assert shard_cols % tn == 0
        per_shard = shard_cols // tn
        b_spec = pl.BlockSpec((None, k_dim, tn), lambda i, j: (j // per_shard, 0, j % per_shard))
    elif b_col_shards:
        assert mode == "nt" and k_dim == n_shards * shard_cols
        b_spec = pl.BlockSpec((n_shards, tn, shard_cols), lambda i, j: (0, j, 0))
    else:
        b_spec = (pl.BlockSpec((tn, k_dim), lambda i, j: (j, 0)) if mode == "nt"
                  else pl.BlockSpec((k_dim, tn), lambda i, j: (0, j)))
    mn_spec = pl.BlockSpec((tm, tn), lambda i, j: (i, j))
    operands, in_specs = [a, b], [a_spec, b_spec]
    for a_i, b_i in more:
        assert mode == "nn" and a_i.shape[0] == m_dim and b_i.shape == (a_i.shape[1], n_dim)
        operands += [a_i, b_i]
        in_specs += [pl.BlockSpec((tm, a_i.shape[1]), lambda i, j: (i, 0)),
                     pl.BlockSpec((a_i.shape[1], tn), lambda i, j: (0, j))]
    if relu2_grad_of is not None:
        operands.append(relu2_grad_of)
        in_specs.append(mn_spec)
    if add is not None:
        operands.append(add)
        in_specs.append(mn_spec)
    gain_spec = pl.BlockSpec((1, n_dim), lambda i, j: (0, 0))
    if a_norm is not None:
        assert mode in ("nn", "nt") and a.dtype == F32 and not row_epilogue and out_col_shards == 1
        operands.append(a_norm)
        in_specs.append(pl.BlockSpec((1, k_dim), lambda i, j: (0, 0)))
    if norm_bwd is not None:
        h_in, g_in, dres_in = norm_bwd
        operands += [h_in, g_in, dres_in]
        in_specs += [mn_spec, gain_spec, mn_spec]
    if loss_head is not None:
        operands += list(loss_head)
        in_specs += [gain_spec, mn_spec]
    if row_epilogue:
        out_shape = (jax.ShapeDtypeStruct((m_dim, n_dim), F32), jax.ShapeDtypeStruct((m_dim, n_dim), BF16),
                     jax.ShapeDtypeStruct((8, n_dim), F32))
        out_spec = (mn_spec, mn_spec, pl.BlockSpec((8, n_dim), lambda i, j: (0, 0)))
    elif out_col_shards == 1:
        out_shape = jax.ShapeDtypeStruct((m_dim, n_dim), out_dtype)
        out_spec = mn_spec
    else:
        n_shard = n_dim // out_col_shards
        assert n_shard % tn == 0
        per = n_shard // tn
        out_shape = jax.ShapeDtypeStruct((out_col_shards, m_dim, n_shard), out_dtype)
        out_spec = pl.BlockSpec((None, tm, tn), lambda i, j: (j // per, i, j % per))
    if a_norm is not None:
        out_shape = (out_shape, jax.ShapeDtypeStruct((m_dim, k_dim), BF16))
        out_spec = (out_spec, pl.BlockSpec((tm, k_dim), lambda i, j: (i, 0)))

    n_in = len(operands)

    def body(*refs):
        a_ref, b_ref, o_ref = refs[0], refs[1], refs[n_in]
        if a_norm is not None:
            normed_ref = refs[n_in + 1]

            @pl.when(pl.program_id(1) == 0)
            def _():
                xa = a_ref[...]
                normed_ref[...] = (xa * _rms(xa) * refs[n_in - 1][...]).astype(BF16)

            av = normed_ref[...]
        else:
            av = a_ref[...]
        if a_relu2:
            av = jnp.maximum(av, 0)
            av = av * av
        if b_col_shards and mode == "nt":
            r = None
            for s in range(n_shards):
                part = lax.dot_general(av[:, s * shard_cols:(s + 1) * shard_cols].astype(BF16),
                                       b_ref[s].astype(BF16), _DIMS[mode], preferred_element_type=F32)
                r = part if r is None else r + part
        else:
            r = lax.dot_general(av.astype(BF16), b_ref[...].astype(BF16), _DIMS[mode],
                                preferred_element_type=F32)
        nxt = 2
        for _ in more:
            r = r + jnp.dot(refs[nxt][...].astype(BF16), refs[nxt + 1][...].astype(BF16),
                            preferred_element_type=F32)
            nxt += 2
        if relu2_grad_of is not None:
            r = r * (2.0 * jnp.maximum(refs[nxt][...].astype(F32), 0.0))
            nxt += 1
        if add is not None:
            r = r + refs[nxt][...]
            nxt += 1
        if not row_epilogue:
            o_ref[...] = r.astype(out_dtype)
            return
        dh16_ref, small_ref = refs[n_in + 1], refs[n_in + 2]

        @pl.when(pl.program_id(0) == 0)
        def _():
            small_ref[...] = jnp.zeros_like(small_ref)

        if norm_bwd is not None:
            h_ref, g_ref, dres_ref = refs[nxt:nxt + 3]
            x, dy = h_ref[...], r
        else:
            g_ref, t_ref = refs[nxt:nxt + 2]
            x = r
        rr = _rms(x)
        xhat = x * rr
        gv = g_ref[...]
        if loss_head is not None:
            err = xhat * gv - t_ref[...]
            small_ref[1:2, :] += (0.5 / n_dim) * jnp.sum(err * err, axis=0, keepdims=True)
            dy = err * (1.0 / n_dim)
        dxhat = dy * gv
        dh = rr * (dxhat - xhat * jnp.mean(dxhat * xhat, axis=-1, keepdims=True))
        if norm_bwd is not None:
            dh = dres_ref[...] + dh
        o_ref[...] = dh
        dh16_ref[...] = dh.astype(BF16)
        small_ref[0:1, :] += jnp.sum(dy * xhat, axis=0, keepdims=True)

    if row_epilogue:
        semantics = ("arbitrary", "arbitrary")
    elif a_norm is not None:
        semantics = ("parallel", "arbitrary")
    else:
        semantics = ("parallel", "parallel")
    return _pcall(
        body, name=name, out_shape=out_shape,
        grid=(m_dim // tm, n_dim // tn),
        in_specs=in_specs, out_specs=out_spec,
        compiler_params=_params(*semantics),
    )(*operands)


def _rms(x):
    return lax.rsqrt(jnp.mean(x * x, axis=-1, keepdims=True) + RMS_EPS)


def _pool_mixed(ext, hn, t0, tb):
    t = t0 + lax.broadcasted_iota(jnp.int32, (tb, 1), 0)
    parts = []
    for gi, w in enumerate(POOL_WINDOWS):
        lanes = slice(gi * POOL_GROUP, (gi + 1) * POOL_GROUP)
        s = ext[:, lanes]
        k = 1
        while k < w:
            s = s + pltpu.roll(s, k, 0)
            k *= 2
        cnt = jnp.minimum(t + 1, w).astype(F32)
        parts.append(s[POOL_HALO:, :] / cnt - hn[:, lanes])
    return parts


def _pool_fwd(x, g, pw, pb, ps, *, tb=512):
    t_dim, d = x.shape

    def body(x_ref, g_ref, pw_ref, pb_ref, ps_ref, o_ref, ext_ref):
        i = pl.program_id(0)

        @pl.when(i == 0)
        def _():
            ext_ref[0:POOL_HALO, :] = jnp.zeros((POOL_HALO, d), F32)

        xv = x_ref[...]
        hn = xv * _rms(xv) * g_ref[...]
        ext_ref[POOL_HALO:, :] = hn
        mixed = _pool_mixed(ext_ref[...], hn, i * tb, tb)
        for gi in range(len(POOL_WINDOWS)):
            lanes = slice(gi * POOL_GROUP, (gi + 1) * POOL_GROUP)
            out = jnp.dot(mixed[gi].astype(BF16), pw_ref[gi], preferred_element_type=F32)
            o_ref[:, lanes] = xv[:, lanes] + (out + pb_ref[:, lanes]) * ps_ref[:, lanes]
        ext_ref[0:POOL_HALO, :] = hn[tb - POOL_HALO:, :]

    return _pcall(
        body, name="pool_fwd", out_shape=jax.ShapeDtypeStruct((t_dim, d), F32),
        grid=(t_dim // tb,),
        in_specs=[_row_spec(tb, d), _const_spec((1, d)), _const_spec((4, POOL_GROUP, POOL_GROUP)),
                  _const_spec((1, d)), _const_spec((1, d))],
        out_specs=_row_spec(tb, d),
        scratch_shapes=[pltpu.VMEM((POOL_HALO + tb, d), F32)],
        compiler_params=_params("arbitrary"),
    )(x, g, pw, pb, ps)


def _pool_bwd(x, g, pw, pb, ps, dh1, *, tb=512):
    t_dim, d = x.shape
    nb = t_dim // tb
    halo_per_block = tb // POOL_HALO

    def body(x_ref, xprev_ref, g_ref, pw_ref, pb_ref, ps_ref, dh1_ref,
             dx_ref, dpw_ref, small_ref, ext_ref, dext_ref):
        i = pl.program_id(0)
        blk = nb - 1 - i

        @pl.when(i == 0)
        def _():
            dpw_ref[...] = jnp.zeros_like(dpw_ref)
            small_ref[...] = jnp.zeros_like(small_ref)
            dext_ref[tb:, :] = jnp.zeros((POOL_HALO, d), F32)

        gv = g_ref[...]
        xv = x_ref[...]
        r = _rms(xv)
        xhat = xv * r
        hn = xhat * gv
        xp = xprev_ref[...]
        hprev = xp * _rms(xp) * gv * (blk > 0).astype(F32)
        ext_ref[0:POOL_HALO, :] = hprev
        ext_ref[POOL_HALO:, :] = hn
        mixed = _pool_mixed(ext_ref[...], hn, blk * tb, tb)

        dout = dh1_ref[...]
        t = blk * tb + lax.broadcasted_iota(jnp.int32, (tb, 1), 0)
        for gi, w in enumerate(POOL_WINDOWS):
            lanes = slice(gi * POOL_GROUP, (gi + 1) * POOL_GROUP)
            mb = mixed[gi].astype(BF16)
            pre = jnp.dot(mb, pw_ref[gi], preferred_element_type=F32) + pb_ref[:, lanes]
            dg_out = dout[:, lanes]
            small_ref[2:3, lanes] += jnp.sum(dg_out * pre, axis=0, keepdims=True)
            dpre = dg_out * ps_ref[:, lanes]
            small_ref[1:2, lanes] += jnp.sum(dpre, axis=0, keepdims=True)
            dpb16 = dpre.astype(BF16)
            dpw_ref[gi] += lax.dot_general(mb, dpb16, _DIMS["tn"], preferred_element_type=F32)
            dmixed = lax.dot_general(dpb16, pw_ref[gi], _DIMS["nt"], preferred_element_type=F32)
            cnt = jnp.minimum(t + 1, w).astype(F32)
            dq = dmixed / cnt
            dext_ref[0:tb, lanes] = dq
            s = dext_ref[:, lanes]
            k = 1
            while k < w:
                s = s + pltpu.roll(s, tb + POOL_HALO - k, 0)
                k *= 2
            dhn = s[0:tb, :] - dmixed
            dext_ref[tb:, lanes] = dq[0:POOL_HALO, :]
            small_ref[0:1, lanes] += jnp.sum(dhn * xhat[:, lanes], axis=0, keepdims=True)
            ext_ref[POOL_HALO:, lanes] = dhn * gv[:, lanes]
        dxhat = ext_ref[POOL_HALO:, :]
        dx_ref[...] = dout + r * (dxhat - xhat * jnp.mean(dxhat * xhat, axis=-1, keepdims=True))

    return _pcall(
        body, name="pool_bwd",
        out_shape=(jax.ShapeDtypeStruct((t_dim, d), F32),
                   jax.ShapeDtypeStruct((4, POOL_GROUP, POOL_GROUP), F32),
                   jax.ShapeDtypeStruct((8, d), F32)),
        grid=(nb,),
        in_specs=[_row_spec(tb, d, nb, True),
                  pl.BlockSpec((POOL_HALO, d),
                               lambda i: (jnp.maximum((nb - 1 - i) * halo_per_block - 1, 0), 0)),
                  _const_spec((1, d)), _const_spec((4, POOL_GROUP, POOL_GROUP)),
                  _const_spec((1, d)), _const_spec((1, d)), _row_spec(tb, d, nb, True)],
        out_specs=(_row_spec(tb, d, nb, True), _const_spec((4, POOL_GROUP, POOL_GROUP)),
                   _const_spec((8, d))),
        scratch_shapes=[pltpu.VMEM((POOL_HALO + tb, d), F32), pltpu.VMEM((tb + POOL_HALO, d), F32)],
        compiler_params=_params("arbitrary"),
    )(x, x, g, pw, pb, ps, dh1)


_CONV_CB = 1024
_STRIP = 16


def _strips(tb, fn, unroll=4):
    def step(i, carry):
        fn(pl.multiple_of(i * _STRIP, _STRIP))
        return carry
    lax.fori_loop(0, tb // _STRIP, step, 0, unroll=unroll)


def _conv_taps(ext_ref, r0, w):
    shifted = [ext_ref[CONV_HALO + r0 - sh:CONV_HALO + r0 - sh + _STRIP, :] for sh in range(CONV_K)]
    acc = shifted[0] * w[CONV_K - 1:CONV_K, :]
    for sh in range(1, CONV_K):
        acc = acc + shifted[sh] * w[CONV_K - 1 - sh:CONV_K - sh, :]
    return shifted, acc


def _conv_fwd(u, w, b, *, tb=512):
    t_dim, c = u.shape
    cb = _CONV_CB

    def body(u_ref, w_ref, b_ref, o_ref, ext_ref):
        @pl.when(pl.program_id(1) == 0)
        def _():
            ext_ref[0:CONV_HALO, :] = jnp.zeros((CONV_HALO, cb), F32)

        wv = w_ref[...]
        bv = b_ref[...]

        def fill(r0):
            ext_ref[pl.ds(CONV_HALO + r0, _STRIP), :] = u_ref[pl.ds(r0, _STRIP), :].astype(F32)

        _strips(tb, fill)
        for r0 in range(0, tb, _STRIP):
            v = _conv_taps(ext_ref, r0, wv)[1] + bv
            o_ref[r0:r0 + _STRIP, :] = (v * _sigmoid(v)).astype(BF16)
        ext_ref[0:CONV_HALO, :] = ext_ref[tb:tb + CONV_HALO, :]

    blk = pl.BlockSpec((tb, cb), lambda j, t: (t, j))
    return _pcall(
        body, name="conv_fwd", out_shape=jax.ShapeDtypeStruct((t_dim, c), BF16),
        grid=(c // cb, t_dim // tb),
        in_specs=[blk, pl.BlockSpec((CONV_K, cb), lambda j, t: (0, j)),
                  pl.BlockSpec((1, cb), lambda j, t: (0, j))],
        out_specs=blk,
        scratch_shapes=[pltpu.VMEM((CONV_HALO + tb, cb), F32)],
        compiler_params=_params("parallel", "arbitrary"),
    )(u, w, b)


def _conv_bwd_act(u, dxc, w, b, *, tb=512):
    t_dim, c = u.shape
    cb = _CONV_CB
    half = _STRIP // 2

    def body(u_ref, d_ref, w_ref, b_ref, dv_ref, dwb_ref, ext_ref, acc_ref):
        @pl.when(pl.program_id(1) == 0)
        def _():
            ext_ref[0:CONV_HALO, :] = jnp.zeros((CONV_HALO, cb), F32)
            dwb_ref[...] = jnp.zeros_like(dwb_ref)

        acc_ref[...] = jnp.zeros_like(acc_ref)
        wv = w_ref[...]
        bv = b_ref[...]

        def fill(r0):
            ext_ref[pl.ds(CONV_HALO + r0, _STRIP), :] = u_ref[pl.ds(r0, _STRIP), :].astype(F32)

        _strips(tb, fill)
        for r0 in range(0, tb, _STRIP):
            shifted, v = _conv_taps(ext_ref, r0, wv)
            v = v + bv
            sg = _sigmoid(v)
            dv = d_ref[r0:r0 + _STRIP, :].astype(F32) * (sg * (1.0 + v * (1.0 - sg)))
            dv_ref[r0:r0 + _STRIP, :] = dv.astype(BF16)
            acc_ref[CONV_K] += dv[0:half, :] + dv[half:, :]
            for sh in range(CONV_K):
                p = dv * shifted[sh]
                acc_ref[CONV_K - 1 - sh] += p[0:half, :] + p[half:, :]
        for k in range(CONV_K + 1):
            dwb_ref[k:k + 1, :] += jnp.sum(acc_ref[k], axis=0, keepdims=True)
        ext_ref[0:CONV_HALO, :] = ext_ref[tb:tb + CONV_HALO, :]

    blk = pl.BlockSpec((tb, cb), lambda j, t: (t, j))
    return _pcall(
        body, name="conv_bwd_act",
        out_shape=(jax.ShapeDtypeStruct((t_dim, c), BF16), jax.ShapeDtypeStruct((8, c), F32)),
        grid=(c // cb, t_dim // tb),
        in_specs=[blk, blk, pl.BlockSpec((CONV_K, cb), lambda j, t: (0, j)),
                  pl.BlockSpec((1, cb), lambda j, t: (0, j))],
        out_specs=(blk, pl.BlockSpec((8, cb), lambda j, t: (0, j))),
        scratch_shapes=[pltpu.VMEM((CONV_HALO + tb, cb), F32), pltpu.VMEM((CONV_K + 1, half, cb), F32)],
        compiler_params=_params("parallel", "arbitrary"),
    )(u, dxc, w, b)


def _conv_bwd_in(dv, w, *, tb=512):
    t_dim, c = dv.shape
    cb = _CONV_CB
    nb = t_dim // tb

    def body(dv_ref, w_ref, du_ref, ext_ref):
        @pl.when(pl.program_id(1) == 0)
        def _():
            ext_ref[tb:, :] = jnp.zeros((CONV_HALO, cb), F32)

        wv = w_ref[...]

        def fill(r0):
            ext_ref[pl.ds(r0, _STRIP), :] = dv_ref[pl.ds(r0, _STRIP), :].astype(F32)

        _strips(tb, fill)
        for r0 in range(0, tb, _STRIP):
            acc = ext_ref[r0:r0 + _STRIP, :] * wv[CONV_K - 1:CONV_K, :]
            for sh in range(1, CONV_K):
                acc = acc + ext_ref[r0 + sh:r0 + sh + _STRIP, :] * wv[CONV_K - 1 - sh:CONV_K - sh, :]
            du_ref[r0:r0 + _STRIP, :] = acc.astype(BF16)
        ext_ref[tb:, :] = ext_ref[0:CONV_HALO, :]

    blk = pl.BlockSpec((tb, cb), lambda j, t: (nb - 1 - t, j))
    return _pcall(
        body, name="conv_bwd_in", out_shape=jax.ShapeDtypeStruct((t_dim, c), BF16),
        grid=(c // cb, nb),
        in_specs=[blk, pl.BlockSpec((CONV_K, cb), lambda j, t: (0, j))],
        out_specs=blk,
        scratch_shapes=[pltpu.VMEM((tb + CONV_HALO, cb), F32)],
        compiler_params=_params("parallel", "arbitrary"),
    )(dv, w)


def _softplus(v):
    e = jnp.exp(-jnp.abs(v))
    w = 1.0 + e
    log1p = jnp.where(w == 1.0, e, jnp.log(w) * e / jnp.where(w == 1.0, 1.0, w - 1.0))
    return jnp.maximum(v, 0.0) + log1p


def _cumsum_rows(v):
    row = lax.broadcasted_iota(jnp.int32, v.shape, 0) & (CHUNK - 1)
    k = 1
    while k < CHUNK:
        v = v + jnp.where(row >= k, pltpu.roll(v, k, 0), 0.0)
        k *= 2
    return v


def _cumsum_lanes(v):
    col = lax.broadcasted_iota(jnp.int32, v.shape, 1) & (CHUNK - 1)
    k = 1
    while k < CHUNK:
        v = v + jnp.where(col >= k, pltpu.roll(v, k, 1), 0.0)
        k *= 2
    return v


def _rev_cumsum_rows(v):
    row = lax.broadcasted_iota(jnp.int32, v.shape, 0)
    k = 1
    while k < CHUNK:
        v = v + jnp.where(row < CHUNK - k, pltpu.roll(v, CHUNK - k, 0), 0.0)
        k *= 2
    return v


PAIR = 2 * HEAD_DIM
GROUP_LANES = HEADS_PER_GROUP * HEAD_DIM


def _head_lane_matrix():
    h = lax.broadcasted_iota(jnp.int32, (128, D_INNER), 0)
    j = lax.broadcasted_iota(jnp.int32, (128, D_INNER), 1)
    return (j // HEAD_DIM == h).astype(BF16)


def _split_bf16(v, pieces):
    out = []
    for _ in range(pieces):
        p = v.astype(BF16)
        out.append(p)
        v = v - p.astype(F32)
    return out


_EXACT_PIECES = 3


def _expand_heads(values, e3):
    lhs = jnp.concatenate([jnp.concatenate(_split_bf16(v, _EXACT_PIECES), axis=1) for v in values], axis=0)
    out = jnp.dot(lhs, e3, preferred_element_type=F32)
    rows = values[0].shape[0]
    return [out[i * rows:(i + 1) * rows, :] for i in range(len(values))]


def _reduce_heads(v, et, pieces):
    return sum(jnp.dot(p, et, preferred_element_type=F32) for p in _split_bf16(v, pieces))


def _ssd_decay(dt_raw, dt_raw_t, bias_r, bias_c, alog_r, alog_c, *, tb=1024):
    t_dim = dt_raw.shape[0]
    tb = min(tb, t_dim)
    assert t_dim % tb == 0 and tb % CHUNK == 0

    def body(dtr_ref, dtt_ref, br_ref, bc_ref, ar_ref, ac_ref, dt_ref, acs_ref, sg_ref, acst_ref):
        pre = dtr_ref[...] + br_ref[...]
        dt = _softplus(pre)
        dt_ref[...] = dt
        sg_ref[...] = _sigmoid(pre)
        acs_ref[...] = _cumsum_rows(dt * (-jnp.exp(ar_ref[...])))
        acst_ref[...] = _cumsum_lanes(_softplus(dtt_ref[...] + bc_ref[...]) * (-jnp.exp(ac_ref[...])))

    rows = pl.BlockSpec((tb, 128), lambda i: (i, 0))
    cols = pl.BlockSpec((N_HEADS, tb), lambda i: (0, i))
    sds = jax.ShapeDtypeStruct((t_dim, 128), F32)
    return _pcall(
        body, name="ssd_decay",
        out_shape=(sds, sds, sds, jax.ShapeDtypeStruct((N_HEADS, t_dim), F32)),
        grid=(t_dim // tb,),
        in_specs=[rows, cols, _const_spec((1, 128)), _const_spec((N_HEADS, 1)),
                  _const_spec((1, 128)), _const_spec((N_HEADS, 1))],
        out_specs=(rows, rows, rows, cols), compiler_params=_params("parallel"),
    )(dt_raw, dt_raw_t, bias_r, bias_c, alog_r, alog_c)


def _pair_decay(acs_slab, acs_c, h0, causal, left):
    other = pltpu.roll(acs_slab, HEAD_DIM, 1)
    col0 = jnp.where(left, acs_slab, other)
    col1 = jnp.where(left, other, acs_slab)
    l0 = jnp.exp(jnp.where(causal, col0 - acs_c[h0:h0 + 1, :], NEG_INF))
    l1 = jnp.exp(jnp.where(causal, col1 - acs_c[h0 + 1:h0 + 2, :], NEG_INF))
    return l0, l1


def _ssd_fwd(xc, dt_r, acs_r, acs_t, dskip_x, e3_mat):
    t_dim = xc.shape[0]
    nc = t_dim // CHUNK

    def body(xc_ref, dt_ref, acs_ref, acst_ref, dk_ref, e3_ref, y_ref, st_ref, state):
        @pl.when(pl.program_id(0) == 0)
        def _():
            state[...] = jnp.zeros_like(state)

        dt, acs = _expand_heads([dt_ref[...], acs_ref[...]], e3_ref[...])
        acs_c = acst_ref[...]
        st_ref[0] = state[...]
        last = acs[CHUNK - 1:CHUNK, :]
        xs32 = xc_ref[:, 0:D_INNER].astype(F32)
        xdt = xs32 * dt
        xdt16 = xdt.astype(BF16)
        xdte16 = (xdt * jnp.exp(last - acs)).astype(BF16)
        ea = jnp.exp(acs)
        cd = jnp.exp(last)
        skip = dk_ref[...] * xs32
        causal = (lax.broadcasted_iota(jnp.int32, (CHUNK, CHUNK), 0)
                  >= lax.broadcasted_iota(jnp.int32, (CHUNK, CHUNK), 1))
        left = lax.broadcasted_iota(jnp.int32, (CHUNK, PAIR), 1) < HEAD_DIM
        for g in range(N_GROUPS):
            gl = slice(g * GROUP_LANES, (g + 1) * GROUP_LANES)
            bg = xc_ref[:, D_INNER + g * D_STATE:D_INNER + (g + 1) * D_STATE]
            cg = xc_ref[:, D_INNER + (N_GROUPS + g) * D_STATE:D_INNER + (N_GROUPS + g + 1) * D_STATE]
            cb = lax.dot_general(cg, bg, _DIMS["nt"], preferred_element_type=F32)
            hprev = state[:, gl]
            ch = jnp.dot(cg, hprev.astype(BF16), preferred_element_type=F32)
            for j in range(HEADS_PER_GROUP // 2):
                pl_ = slice(g * GROUP_LANES + j * PAIR, g * GROUP_LANES + (j + 1) * PAIR)
                h0 = g * HEADS_PER_GROUP + 2 * j
                l0, l1 = _pair_decay(acs[:, pl_], acs_c, h0, causal, left)
                lhs = jnp.concatenate([(cb * l0).astype(BF16), (cb * l1).astype(BF16)], axis=1)
                xp = xdt16[:, pl_]
                zero = jnp.zeros_like(xp)
                rhs = jnp.concatenate([jnp.where(left, xp, zero), jnp.where(left, zero, xp)], axis=0)
                ydiag = jnp.dot(lhs, rhs, preferred_element_type=F32)
                y_ref[:, pl_] = (ydiag + ch[:, j * PAIR:(j + 1) * PAIR] * ea[:, pl_] + skip[:, pl_]).astype(BF16)
            s_new = lax.dot_general(bg, xdte16[:, gl], _DIMS["tn"], preferred_element_type=F32)
            state[:, gl] = hprev * cd[:, gl] + s_new

    rows = lambda w: pl.BlockSpec((CHUNK, w), lambda c: (c, 0))
    return _pcall(
        body, name="ssd_fwd",
        out_shape=(jax.ShapeDtypeStruct((t_dim, D_INNER), BF16),
                   jax.ShapeDtypeStruct((nc, D_STATE, D_INNER), F32)),
        grid=(nc,),
        in_specs=[rows(CONV_DIM), rows(128), rows(128), pl.BlockSpec((N_HEADS, CHUNK), lambda c: (0, c)),
                  _const_spec((1, D_INNER)), _const_spec((_EXACT_PIECES * 128, D_INNER))],
        out_specs=(rows(D_INNER), pl.BlockSpec((1, D_STATE, D_INNER), lambda c: (c, 0, 0))),
        scratch_shapes=[pltpu.VMEM((D_STATE, D_INNER), F32)],
        compiler_params=_params("arbitrary"),
    )(xc, dt_r, acs_r, acs_t, dskip_x, e3_mat)


def _ssd_bwd(xc, dt_r, acs_r, sg_r, acs_t, alog_r, dskip_x, e3_mat, et_mat, states, dy):
    t_dim = xc.shape[0]
    nc = t_dim // CHUNK

    def body(xc_ref, dt_ref, acs_ref, sg_ref, acst_ref, ar_ref, dk_ref, e3_ref, et_ref, st_ref, dy_ref,
             dxc_ref, ddt_ref, small_ref, dstate, dacs_ref, dxdt_ref, acc_x, acc_r):
        step = pl.program_id(0)

        @pl.when(step == 0)
        def _():
            dstate[...] = jnp.zeros_like(dstate)
            acc_x[...] = jnp.zeros_like(acc_x)
            acc_r[...] = jnp.zeros_like(acc_r)

        dt_r = dt_ref[...]
        a_r = -jnp.exp(ar_ref[...])
        dt, acs = _expand_heads([dt_r, acs_ref[...]], e3_ref[...])
        acs_c = acst_ref[...]
        last = acs[CHUNK - 1:CHUNK, :]
        xs32 = xc_ref[:, 0:D_INNER].astype(F32)
        xdt = xs32 * dt
        xdt16 = xdt.astype(BF16)
        dte = jnp.exp(last - acs)
        xdte = xdt * dte
        xdte16 = xdte.astype(BF16)
        cd = jnp.exp(last)
        dy16 = dy_ref[...]
        dyv = dy16.astype(F32)
        dye = dyv * jnp.exp(acs)
        dye16 = dye.astype(BF16)
        causal = (lax.broadcasted_iota(jnp.int32, (CHUNK, CHUNK), 0)
                  >= lax.broadcasted_iota(jnp.int32, (CHUNK, CHUNK), 1))
        left = lax.broadcasted_iota(jnp.int32, (CHUNK, PAIR), 1) < HEAD_DIM
        lane_id = lax.broadcasted_iota(jnp.int32, (CHUNK, 128), 1)
        row_id = lax.broadcasted_iota(jnp.int32, (CHUNK, 128), 0)
        is_last_row = lax.broadcasted_iota(jnp.int32, (CHUNK, 1), 0) == CHUNK - 1
        dacs_cols = jnp.zeros((CHUNK, 128), F32)
        dacs_rows = jnp.zeros((CHUNK, 128), F32)
        for g in range(N_GROUPS):
            gl = slice(g * GROUP_LANES, (g + 1) * GROUP_LANES)
            b_lanes = slice(D_INNER + g * D_STATE, D_INNER + (g + 1) * D_STATE)
            c_lanes = slice(D_INNER + (N_GROUPS + g) * D_STATE, D_INNER + (N_GROUPS + g + 1) * D_STATE)
            bg = xc_ref[:, b_lanes]
            cg = xc_ref[:, c_lanes]
            cb = lax.dot_general(cg, bg, _DIMS["nt"], preferred_element_type=F32)
            hprev = st_ref[0, :, gl]
            hp16 = hprev.astype(BF16)
            dhn = dstate[:, gl]
            dhn16 = dhn.astype(BF16)
            ch = jnp.dot(cg, hp16, preferred_element_type=F32)
            gmat = jnp.dot(bg, dhn16, preferred_element_type=F32)
            gx = gmat * xdte[:, gl]
            dlast = jnp.sum(gx, axis=0, keepdims=True) + cd[:, gl] * jnp.sum(dhn * hprev, axis=0, keepdims=True)
            dacs_ref[:, gl] = dye[:, gl] * ch - gx + jnp.where(is_last_row, dlast, 0.0)
            dc_acc = lax.dot_general(dye16[:, gl], hp16, _DIMS["nt"], preferred_element_type=F32)
            db_acc = lax.dot_general(xdte16[:, gl], dhn16, _DIMS["nt"], preferred_element_type=F32)
            dstate[:, gl] = dhn * cd[:, gl] + lax.dot_general(cg, dye16[:, gl], _DIMS["tn"],
                                                             preferred_element_type=F32)
            dcb = jnp.zeros((CHUNK, CHUNK), F32)
            for j in range(HEADS_PER_GROUP // 2):
                pl_ = slice(g * GROUP_LANES + j * PAIR, g * GROUP_LANES + (j + 1) * PAIR)
                h0 = g * HEADS_PER_GROUP + 2 * j
                l0, l1 = _pair_decay(acs[:, pl_], acs_c, h0, causal, left)
                m0, m1 = cb * l0, cb * l1
                lhs = jnp.concatenate([m0.astype(BF16), m1.astype(BF16)], axis=1)
                dyp = dy16[:, pl_]
                zero = jnp.zeros_like(dyp)
                both = lax.dot_general(lhs, dyp, _DIMS["tn"], preferred_element_type=F32)
                dxdt_ref[:, pl_] = (jnp.where(left, both[0:CHUNK, :], both[CHUNK:, :])
                                    + gmat[:, j * PAIR:(j + 1) * PAIR] * dte[:, pl_])
                lhs2 = jnp.concatenate([jnp.where(left, dyp, zero), jnp.where(left, zero, dyp)], axis=0)
                dm = lax.dot_general(lhs2, xdt16[:, pl_], _DIMS["nt"], preferred_element_type=F32)
                dm0, dm1 = dm[0:CHUNK, :], dm[CHUNK:, :]
                dcb = dcb + dm0 * l0 + dm1 * l1
                ds0, ds1 = dm0 * m0, dm1 * m1
                dacs_cols = jnp.where(lane_id == h0, jnp.sum(ds0, axis=1, keepdims=True), dacs_cols)
                dacs_cols = jnp.where(lane_id == h0 + 1, jnp.sum(ds1, axis=1, keepdims=True), dacs_cols)
                dacs_rows = jnp.where(row_id == h0, jnp.sum(ds0, axis=0, keepdims=True), dacs_rows)
                dacs_rows = jnp.where(row_id == h0 + 1, jnp.sum(ds1, axis=0, keepdims=True), dacs_rows)
            dcb16 = dcb.astype(BF16)
            dxc_ref[:, c_lanes] = (dc_acc + jnp.dot(dcb16, bg, preferred_element_type=F32)).astype(BF16)
            dxc_ref[:, b_lanes] = (db_acc + lax.dot_general(dcb16, cg, _DIMS["tn"],
                                                           preferred_element_type=F32)).astype(BF16)
        dxdt = dxdt_ref[...]
        dxc_ref[:, 0:D_INNER] = (dxdt * dt + dk_ref[...] * dyv).astype(BF16)
        acc_x[0:1, :] += jnp.sum(dyv * xs32, axis=0, keepdims=True)
        et = et_ref[...]
        dacs = _reduce_heads(dacs_ref[...], et, 2) + dacs_cols - dacs_rows.T
        dadt = _rev_cumsum_rows(dacs)
        ddraw = (_reduce_heads(dxdt * xs32, et, 1) + dadt * a_r) * sg_ref[...]
        ddraw = jnp.where(lane_id < N_HEADS, ddraw, 0.0)
        ddt_ref[...] = ddraw
        acc_r[0:1, :] += jnp.where(lane_id[0:1, :] < N_HEADS,
                                   jnp.sum(dadt * dt_r, axis=0, keepdims=True) * a_r, 0.0)
        acc_r[1:2, :] += jnp.sum(ddraw, axis=0, keepdims=True)

        @pl.when(step == nc - 1)
        def _():
            dd = _reduce_heads(acc_x[...], et_ref[...], 3)
            rid = lax.broadcasted_iota(jnp.int32, (8, 128), 0)
            small_ref[...] = acc_r[...] + jnp.where(rid == 2, pltpu.roll(dd, 2, 0), 0.0)

    rev = lambda w: pl.BlockSpec((CHUNK, w), lambda c: (nc - 1 - c, 0))
    return _pcall(
        body, name="ssd_bwd",
        out_shape=(jax.ShapeDtypeStruct((t_dim, CONV_DIM), BF16),
                   jax.ShapeDtypeStruct((t_dim, 128), F32),
                   jax.ShapeDtypeStruct((8, 128), F32)),
        grid=(nc,),
        in_specs=[rev(CONV_DIM), rev(128), rev(128), rev(128),
                  pl.BlockSpec((N_HEADS, CHUNK), lambda c: (0, nc - 1 - c)),
                  _const_spec((1, 128)), _const_spec((1, D_INNER)),
                  _const_spec((_EXACT_PIECES * 128, D_INNER)), _const_spec((D_INNER, 128)),
                  pl.BlockSpec((1, D_STATE, D_INNER), lambda c: (nc - 1 - c, 0, 0)),
                  rev(D_INNER)],
        out_specs=(rev(CONV_DIM), rev(128), _const_spec((8, 128))),
        scratch_shapes=[pltpu.VMEM((D_STATE, D_INNER), F32), pltpu.VMEM((CHUNK, D_INNER), F32),
                        pltpu.VMEM((CHUNK, D_INNER), F32), pltpu.VMEM((8, D_INNER), F32),
                        pltpu.VMEM((8, 128), F32)],
        compiler_params=_params("arbitrary"),
    )(xc, dt_r, acs_r, sg_r, acs_t, alog_r, dskip_x, e3_mat, et_mat, states, dy)


_GATE_GROUP = D_INNER // N_GROUPS


def _gate_fwd(y, z, g, *, tb=256):
    t_dim = y.shape[0]

    def body(y_ref, z_ref, g_ref, o_ref):
        for gi in range(N_GROUPS):
            lanes = slice(gi * _GATE_GROUP, (gi + 1) * _GATE_GROUP)
            zv = z_ref[:, lanes].astype(F32)
            wv = y_ref[:, lanes].astype(F32) * (zv * _sigmoid(zv))
            o_ref[:, lanes] = (wv * _rms(wv) * g_ref[:, lanes]).astype(BF16)

    return _pcall(
        body, name="gate_fwd", out_shape=jax.ShapeDtypeStruct((t_dim, D_INNER), BF16),
        grid=(t_dim // tb,),
        in_specs=[_row_spec(tb, D_INNER), _row_spec(tb, D_INNER), _const_spec((1, D_INNER))],
        out_specs=_row_spec(tb, D_INNER), compiler_params=_params("parallel"),
    )(y, z, g)


def _gate_bwd(dyn, y, z, g, *, tb=256):
    t_dim = y.shape[0]

    def body(d_ref, y_ref, z_ref, g_ref, dy_ref, dz_ref, dg_ref):
        @pl.when(pl.program_id(0) == 0)
        def _():
            dg_ref[...] = jnp.zeros_like(dg_ref)

        for gi in range(N_GROUPS):
            lanes = slice(gi * _GATE_GROUP, (gi + 1) * _GATE_GROUP)
            zv = z_ref[:, lanes].astype(F32)
            sg = _sigmoid(zv)
            sz = zv * sg
            yv = y_ref[:, lanes].astype(F32)
            wv = yv * sz
            r = _rms(wv)
            what = wv * r
            dv = d_ref[:, lanes].astype(F32)
            dwhat = dv * g_ref[:, lanes]
            dw = r * (dwhat - what * jnp.mean(dwhat * what, axis=-1, keepdims=True))
            dg_ref[0:1, lanes] += jnp.sum(dv * what, axis=0, keepdims=True)
            dy_ref[:, lanes] = (dw * sz).astype(BF16)
            dz_ref[:, lanes] = (dw * yv * (sg * (1.0 + zv * (1.0 - sg)))).astype(BF16)

    return _pcall(
        body, name="gate_bwd",
        out_shape=(jax.ShapeDtypeStruct((t_dim, D_INNER), BF16),
                   jax.ShapeDtypeStruct((t_dim, D_INNER), BF16),
                   jax.ShapeDtypeStruct((8, D_INNER), F32)),
        grid=(t_dim // tb,),
        in_specs=[_row_spec(tb, D_INNER), _row_spec(tb, D_INNER), _row_spec(tb, D_INNER),
                  _const_spec((1, D_INNER))],
        out_specs=(_row_spec(tb, D_INNER), _row_spec(tb, D_INNER), _const_spec((8, D_INNER))),
        compiler_params=_params("arbitrary"),
    )(dyn, y, z, g)


_ADAM_C1 = 1.0 / (1.0 - ADAM_B1 ** ADAM_STEP)
_ADAM_C2 = 1.0 / (1.0 - ADAM_B2 ** ADAM_STEP)


def _adamw_math(w, g, m, v):
    mn = ADAM_B1 * m + (1.0 - ADAM_B1) * g
    vn = ADAM_B2 * v + (1.0 - ADAM_B2) * (g * g)
    delta = -ADAM_LR * ((mn * _ADAM_C1) / (jnp.sqrt(vn * _ADAM_C2) + ADAM_EPS) + ADAM_WD * w)
    return delta, mn, vn


def _adamw(w, g, m, v, *, name, part=None, into=None):
    r_dim, c = w.shape
    rows = r_dim if part is None else r_dim // 2
    assert g.shape == (rows, c)
    tb = max(t for t in range(8, 513, 8) if rows % t == 0)
    first = 0 if part is None else part * (rows // tb)
    n_out = 3 if part is None else 4

    def body(w_ref, g_ref, m_ref, v_ref, *rest):
        outs = rest[-n_out:]
        gv = g_ref[...]
        outs[0][...], outs[1][...], outs[2][...] = _adamw_math(w_ref[...], gv, m_ref[...], v_ref[...])
        if part is not None:
            outs[3][...] = gv

    spec = pl.BlockSpec((tb, c), lambda i: (first + i, 0))
    sds = jax.ShapeDtypeStruct((r_dim, c), F32)
    in_specs = [spec, _row_spec(tb, c), spec, spec]
    operands = [w, g, m, v]
    aliases = {}
    if into is not None:
        in_specs += [_ANY] * n_out
        operands += list(into)
        aliases = {4 + i: i for i in range(n_out)}
    outs = _pcall(
        body, name=name, out_shape=(sds,) * n_out, grid=(rows // tb,),
        in_specs=in_specs, out_specs=(spec,) * n_out, input_output_aliases=aliases,
        compiler_params=_params("parallel"),
    )(*operands)
    return tuple(outs) if part is not None else tuple(outs) + (g,)


def _adamw_small(params, *, name):
    n = len(params)

    def body(*refs):
        ins, outs = refs[:4 * n], refs[4 * n:]
        for i in range(n):
            w_ref, g_ref, m_ref, v_ref = ins[4 * i:4 * i + 4]
            res = _adamw_math(w_ref[...], g_ref[...], m_ref[...], v_ref[...])
            for o_ref, r in zip(outs[3 * i:3 * i + 3], res):
                o_ref[...] = r

    vmem = pl.BlockSpec(memory_space=pltpu.VMEM)
    flat = [a for p in params for a in p]
    outs = _pcall(
        body, name=name,
        out_shape=tuple(jax.ShapeDtypeStruct(p[0].shape, F32) for p in params for _ in range(3)),
        in_specs=[vmem] * (4 * n), out_specs=(vmem,) * (3 * n),
    )(*flat)
    return [tuple(outs[3 * i:3 * i + 3]) for i in range(n)]


def _pair_sum(grad, recv, place, *, name):
    s_dim, r_dim, c = grad.shape
    half = r_dim // 2
    tb = 256 if half % 256 == 0 else half
    per_half = half // tb

    def body(place_ref, a_ref, b_ref, o16_ref, o32_ref):
        s = a_ref[...] + b_ref[...]
        o16_ref[...] = s.astype(BF16)

        @pl.when(pl.program_id(1) == place_ref[1])
        def _():
            o32_ref[...] = s[0]

    grid_spec = pltpu.PrefetchScalarGridSpec(
        num_scalar_prefetch=1, grid=(per_half, s_dim),
        in_specs=[pl.BlockSpec((1, tb, c), lambda i, s, p: (s, p[0] * per_half + i, 0)),
                  pl.BlockSpec((1, tb, c), lambda i, s, p: (s, i, 0))],
        out_specs=(pl.BlockSpec((1, tb, c), lambda i, s, p: (s, i, 0)),
                   pl.BlockSpec((tb, c), lambda i, s, p: (i, 0))))
    return _pcall(
        body, name=name, grid_spec=grid_spec,
        out_shape=(jax.ShapeDtypeStruct((s_dim, half, c), BF16), jax.ShapeDtypeStruct((half, c), F32)),
        compiler_params=_params("parallel", "arbitrary"),
    )(place, grad, recv)


def _chip_sum(own, recv, place, *, name):
    r_dim, c = own.shape
    tb = 256 if r_dim % 256 == 0 else r_dim

    def body(place_ref, a_ref, b_ref, o_ref):
        s = a_ref[...]
        for k in range(1, N_CHIPS):
            s = s + b_ref[k].astype(F32)
        o_ref[...] = s

    grid_spec = pltpu.PrefetchScalarGridSpec(
        num_scalar_prefetch=1, grid=(r_dim // tb,),
        in_specs=[pl.BlockSpec((tb, c), lambda i, p: (i, 0)),
                  pl.BlockSpec((N_CHIPS, tb, c), lambda i, p: (0, i, 0))],
        out_specs=pl.BlockSpec((None, tb, c), lambda i, p: (p[0], i, 0)))
    return _pcall(
        body, name=name, grid_spec=grid_spec, out_shape=jax.ShapeDtypeStruct((2, r_dim, c), F32),
        compiler_params=_params("parallel"),
    )(place, own, recv)


def _position():
    return lax.axis_index("x"), lax.axis_index("y"), lax.axis_index("c")


def _chip_peer(x, y, k):
    return x ^ (k >> 1), y ^ (k & 1)


_ANY = pl.BlockSpec(memory_space=pl.ANY)
_TOKEN = jax.ShapeDtypeStruct((8, 128), F32)


def _all_gather_weights(shards):
    n = len(shards)
    hops = N_CHIPS - 1

    def body(*refs):
        srcs, outs, done = refs[:n], refs[n:2 * n], refs[2 * n]
        send_sems, recv_sems = refs[2 * n + 1:]
        x, y, c = _position()
        me = 2 * x + y
        done[...] = jnp.zeros_like(done)

        def over_ici(w, k, chip, to):
            return pltpu.make_async_remote_copy(
                src_ref=srcs[w].at[c], dst_ref=outs[w].at[chip, c],
                send_sem=send_sems.at[w, k - 1], recv_sem=recv_sems.at[w, k - 1],
                device_id=to, device_id_type=MESH)

        def over_d2d(w, k, chip, half):
            return pltpu.make_async_remote_copy(
                src_ref=outs[w].at[chip, half], dst_ref=outs[w].at[chip, half],
                send_sem=send_sems.at[w, hops + k - 1], recv_sem=recv_sems.at[w, hops + k - 1],
                device_id=(x, y, 1 - c), device_id_type=MESH)

        sends = []
        for w in range(n):
            for k in range(1, N_CHIPS):
                px, py = _chip_peer(x, y, k)
                cp = over_ici(w, k, me, (px, py, c))
                cp.start()
                sends.append(cp)
        for w in range(n):
            for k in range(1, N_CHIPS):
                px, py = _chip_peer(x, y, k)
                over_ici(w, k, 2 * px + py, (px, py, c)).wait_recv()
                cp = over_d2d(w, k, 2 * px + py, c)
                cp.start()
                sends.append(cp)
        for w in range(n):
            for k in range(1, N_CHIPS):
                px, py = _chip_peer(x, y, k)
                over_d2d(w, k, 2 * px + py, 1 - c).wait_recv()
        for cp in sends:
            cp.wait_send()

    outs = _pcall(
        body, name="gather_weights",
        out_shape=tuple(jax.ShapeDtypeStruct((N_CHIPS,) + s.shape, s.dtype) for s in shards) + (_TOKEN,),
        in_specs=[_ANY] * n, out_specs=(_ANY,) * n + (pl.BlockSpec(memory_space=pltpu.VMEM),),
        scratch_shapes=[pltpu.SemaphoreType.DMA((n, 2 * hops)),
                        pltpu.SemaphoreType.DMA((n, 2 * hops))],
    )(*shards)
    return outs[:n], outs[n][0, 0]


def _pair_copies(srcs, lands, send_sems, recv_sems):
    x, y, c = _position()
    copies = []
    for w in range(len(srcs)):
        half = srcs[w].shape[1] // 2
        copies.append(pltpu.make_async_remote_copy(
            src_ref=srcs[w].at[:, pl.ds((1 - c) * half, half), :], dst_ref=lands[w],
            send_sem=send_sems.at[w], recv_sem=recv_sems.at[w],
            device_id=(x, y, 1 - c), device_id_type=MESH))
    return copies


def _chip_copies(srcs, lands, send_sems, recv_sems):
    x, y, c = _position()
    copies = []
    for w in range(len(srcs)):
        for k in range(1, N_CHIPS):
            px, py = _chip_peer(x, y, k)
            i = w * (N_CHIPS - 1) + k - 1
            copies.append(pltpu.make_async_remote_copy(
                src_ref=srcs[w].at[2 * px + py], dst_ref=lands[w].at[k],
                send_sem=send_sems.at[i], recv_sem=recv_sems.at[i],
                device_id=(px, py, c), device_id_type=MESH))
    return copies


def _gather_copies(srcs, lands, send_sems, recv_sems):
    x, y, c = _position()
    me = 2 * x + y
    copies = []
    for w in range(len(srcs)):
        for k in range(1, N_CHIPS):
            px, py = _chip_peer(x, y, k)
            i = w * (N_CHIPS - 1) + k - 1
            copies.append(pltpu.make_async_remote_copy(
                src_ref=srcs[w].at[c], dst_ref=lands[w].at[me, c],
                send_sem=send_sems.at[i], recv_sem=recv_sems.at[i],
                device_id=(px, py, c), device_id_type=MESH))
    return copies


def _exchange(name, copies_of, n_copies, srcs, land_shapes):
    n = len(srcs)

    def body(*refs):
        copies = copies_of(refs[:n], refs[n:2 * n], refs[2 * n], refs[2 * n + 1])
        for cp in copies:
            cp.start()
        for cp in copies:
            cp.wait_recv()
        for cp in copies:
            cp.wait_send()

    return _pcall(
        body, name=name, out_shape=tuple(land_shapes),
        in_specs=[_ANY] * n, out_specs=(_ANY,) * n,
        scratch_shapes=[pltpu.SemaphoreType.DMA((n_copies,)), pltpu.SemaphoreType.DMA((n_copies,))],
    )(*srcs)


_HBM = pl.BlockSpec(memory_space=pltpu.HBM)
_SEM = pl.BlockSpec(memory_space=pltpu.SEMAPHORE)
_DATAFLOW = pltpu.SideEffectType.DATAFLOW_SIDE_EFFECTING


def _exchange_start(name, copies_of, n_copies, srcs, land_shapes):
    n = len(srcs)
    lands = [lax.empty(s.shape, s.dtype) for s in land_shapes]

    def body(*refs):
        for cp in copies_of(refs[:n], refs[n:2 * n], refs[2 * n], refs[2 * n + 1]):
            cp.start()
        refs[-1][...] = jnp.zeros_like(refs[-1])

    through = [pltpu.HBM(a.shape, a.dtype) for a in list(srcs) + lands]
    outs = _pcall(
        body, name=name,
        out_shape=(pltpu.SemaphoreType.DMA((n_copies,)), pltpu.SemaphoreType.DMA((n_copies,)),
                   *through, jax.ShapeDtypeStruct((8, 128), F32)),
        in_specs=[_HBM] * (2 * n),
        out_specs=(_SEM, _SEM, *([_HBM] * (2 * n)), pl.BlockSpec(memory_space=pltpu.VMEM)),
        input_output_aliases={i: 2 + i for i in range(2 * n)},
        compiler_params=pltpu.CompilerParams(has_side_effects=_DATAFLOW),
    )(*[pltpu.with_memory_space_constraint(a, pltpu.HBM) for a in list(srcs) + lands])
    return outs[:-1], outs[-1][0, 0]


def _exchange_wait(name, copies_of, state, after):
    send_sems, recv_sems, through = state[0], state[1], state[2:]
    n = len(through) // 2
    if after.ndim == 0:
        after = jnp.broadcast_to(after, (8, 128))
    after = pltpu.with_memory_space_constraint(after, pltpu.HBM)

    def body(*refs):
        for cp in copies_of(refs[:n], refs[n:2 * n], refs[2 * n], refs[2 * n + 1]):
            cp.wait_send()
            cp.wait_recv()

    outs = _pcall(
        body, name=name,
        out_shape=tuple(pltpu.HBM(a.shape, a.dtype) for a in through),
        in_specs=[_HBM] * (2 * n) + [_SEM, _SEM, _HBM], out_specs=tuple([_HBM] * (2 * n)),
        input_output_aliases={i: i for i in range(2 * n)},
        compiler_params=pltpu.CompilerParams(has_side_effects=_DATAFLOW),
    )(*through, send_sems, recv_sems, after)
    return outs[:n], outs[n:]


def _forward_halves(lands, *, name):
    n = len(lands)
    hops = N_CHIPS - 1

    def body(*refs):
        ins, outs, done = refs[:n], refs[n:2 * n], refs[2 * n]
        send_sems, recv_sems = refs[2 * n + 1], refs[2 * n + 2]
        x, y, c = _position()
        done[...] = jnp.zeros_like(done)
        copies = []
        for w in range(n):
            for k in range(1, N_CHIPS):
                px, py = _chip_peer(x, y, k)
                i = w * hops + k - 1
                copies.append(pltpu.make_async_remote_copy(
                    src_ref=ins[w].at[2 * px + py, c], dst_ref=outs[w].at[2 * px + py, c],
                    send_sem=send_sems.at[i], recv_sem=recv_sems.at[i],
                    device_id=(x, y, 1 - c), device_id_type=MESH))
        for cp in copies:
            cp.start()
        for cp in copies:
            cp.wait_recv()
        for cp in copies:
            cp.wait_send()

    outs = _pcall(
        body, name=name,
        out_shape=tuple(jax.ShapeDtypeStruct(a.shape, a.dtype) for a in lands) + (_TOKEN,),
        in_specs=[_ANY] * n, out_specs=(_ANY,) * n + (pl.BlockSpec(memory_space=pltpu.VMEM),),
        input_output_aliases={i: i for i in range(n)},
        scratch_shapes=[pltpu.SemaphoreType.DMA((n * hops,)), pltpu.SemaphoreType.DMA((n * hops,))],
    )(*lands)
    return outs[:n], outs[n][0, 0]


def _pair_lands(grads):
    return [jax.ShapeDtypeStruct((g.shape[0], g.shape[1] // 2, g.shape[2]), F32) for g in grads]


def _same_lands(parts):
    return [jax.ShapeDtypeStruct(p.shape, p.dtype) for p in parts]


def _pair_gather_halves(halves, *, name):
    n = len(halves)

    def body(*refs):
        ins, outs = refs[:n], refs[n:2 * n]
        send_sems, recv_sems = refs[2 * n:]
        x, y, c = _position()
        sends = []
        for w in range(n):
            cp = pltpu.make_async_remote_copy(
                src_ref=ins[w].at[c], dst_ref=outs[w].at[c],
                send_sem=send_sems.at[w], recv_sem=recv_sems.at[w],
                device_id=(x, y, 1 - c), device_id_type=MESH)
            cp.start()
            sends.append(cp)
        for cp in sends:
            cp.wait_recv()
        for cp in sends:
            cp.wait_send()

    whole = _pcall(
        body, name=name,
        out_shape=tuple(jax.ShapeDtypeStruct(h.shape, F32) for h in halves),
        in_specs=[_ANY] * n, out_specs=(_ANY,) * n,
        input_output_aliases={i: i for i in range(n)},
        scratch_shapes=[pltpu.SemaphoreType.DMA((n,)), pltpu.SemaphoreType.DMA((n,))],
    )(*halves)
    return [w.reshape(2 * w.shape[1], w.shape[2]) for w in whole]


def _all_reduce_small(packed, *, name, sum_row0):
    r_dim, c = packed.shape

    def body(src_ref, out_ref, recv_ref, send_sems, recv_sems):
        x, y, c_ = _position()
        me = 4 * x + 2 * y + c_
        recv_ref[0] = src_ref[...]
        sends = []
        for k in range(1, N_DEV):
            peer = (x ^ (k >> 2), y ^ ((k >> 1) & 1), c_ ^ (k & 1))
            cp = pltpu.make_async_remote_copy(
                src_ref=src_ref, dst_ref=recv_ref.at[k],
                send_sem=send_sems.at[k - 1], recv_sem=recv_sems.at[k - 1],
                device_id=peer, device_id_type=MESH)
            cp.start()
            sends.append(cp)
        for cp in sends:
            cp.wait_recv()
        total = recv_ref[me]
        for d in range(1, N_DEV):
            total = total + recv_ref[d ^ me]
        if sum_row0:
            row0 = jnp.sum(total[0:1, :], axis=1, keepdims=True)
            rid = lax.broadcasted_iota(jnp.int32, total.shape, 0)
            total = jnp.where(rid == 0, row0, total)
        out_ref[...] = total
        for cp in sends:
            cp.wait_send()

    return _pcall(
        body, name=name, out_shape=jax.ShapeDtypeStruct((r_dim, c), F32),
        in_specs=[pl.BlockSpec(memory_space=pltpu.VMEM)],
        out_specs=pl.BlockSpec(memory_space=pltpu.VMEM),
        scratch_shapes=[pltpu.VMEM((N_DEV, r_dim, c), F32),
                        pltpu.SemaphoreType.DMA((N_DEV - 1,)), pltpu.SemaphoreType.DMA((N_DEV - 1,))],
    )(packed)


def _pad_lanes(v, width):
    return jnp.pad(v, ((0, 0), (0, width - v.shape[1])))


def _pad_rows(v, rows):
    pad = [(0, 0)] * v.ndim
    pad[-2] = (0, rows - v.shape[-2])
    return jnp.pad(v, pad)


_IN_PROJ_SHARD_ROWS = 1312


def _rows_1024(v):
    flat = v.reshape(-1)
    pad = (-flat.shape[0]) % D_MODEL
    return jnp.pad(flat, (0, pad)).reshape(-1, D_MODEL)


def _local_step(xs, target, pw, fetch, reduce_start, reduce_midway,
                conv_w, conv_b, gate_g,
                norm_mix_g, norm_mlp_g, pool_b, pool_scale, ssm_dt_bias, ssm_a_log, ssm_d, final_g):
    bias_r = _pad_lanes(ssm_dt_bias, 128)
    alog_r = _pad_lanes(ssm_a_log, 128)
    dskip_x = jnp.repeat(ssm_d, HEAD_DIM, axis=1)
    bias_c = ssm_dt_bias.reshape(N_HEADS, 1)
    alog_c = ssm_a_log.reshape(N_HEADS, 1)
    e_mat = _head_lane_matrix()
    e3_mat = jnp.tile(e_mat, (_EXACT_PIECES, 1))

    g_mix0, g_mix1 = norm_mix_g[0:1], norm_mix_g[1:2]
    g_mlp0, g_mlp1 = norm_mlp_g[0:1], norm_mlp_g[1:2]
    fg = final_g.reshape(1, D_MODEL)

    h1 = _pool_fwd(xs, g_mix0, pw, pool_b, pool_scale)
    w1_0 = fetch("mlp0_up", h1)
    u0, hm0 = _matmul(h1, w1_0, "nn", name="mlp0_up", out_dtype=BF16, b_col_shards=True, a_norm=g_mlp0)
    w2_0 = fetch("mlp0_down", u0)
    h2 = _matmul(u0, w2_0, "nn", name="mlp0_down", a_relu2=True, add=h1)

    w_z, w_xbc, w_dt = fetch("in_proj", h2)
    xbc, hn1 = _matmul(h2, w_xbc, "nt", name="in_proj_xbc", out_dtype=BF16, a_norm=g_mix1)
    z = _matmul(hn1, w_z, "nt", name="in_proj_z", out_dtype=BF16)
    dt_raw = _matmul(hn1, w_dt, "nt", name="in_proj_dt")
    dt_raw_t = dt_raw[:, :N_HEADS].T
    xc = _conv_fwd(xbc, conv_w, conv_b)
    wout, w1_1, w2_1 = fetch("rest", xc)
    dt_r, acs_r, sg_r, acs_t = _ssd_decay(dt_raw, dt_raw_t, bias_r, bias_c, alog_r, alog_c)
    y, states = _ssd_fwd(xc, dt_r, acs_r, acs_t, dskip_x, e3_mat)
    yn = _gate_fwd(y, z, gate_g)
    h3 = _matmul(yn, wout, "nn", name="out_proj", add=h2)
    u1, hm1 = _matmul(h3, w1_1, "nn", name="mlp1_up", out_dtype=BF16, b_col_shards=True, a_norm=g_mlp1)

    dh4, dh4_16, small_final = _matmul(u1, w2_1, "nn", name="mlp1_down", a_relu2=True, add=h3,
                                       loss_head=(fg, target))

    def mlp_bwd_weights(dh_out16, hm, u, w2_i, tag):
        du = _matmul(dh_out16, w2_i, "nt", name=tag + "_du", out_dtype=BF16, relu2_grad_of=u)
        dw2 = _matmul(u, dh_out16, "tn", name=tag + "_dw2", a_relu2=True)
        dw1 = _matmul(hm, du, "tn", name=tag + "_dw1", out_col_shards=N_CHIPS)
        return du, dw1, dw2.reshape(N_CHIPS, D_FF // N_CHIPS, D_MODEL)

    def mlp_bwd_input(du, dh_out, h_in, w1_i, g_i, tag):
        return _matmul(du, w1_i, "nt", name=tag + "_dhm", b_col_shards=True, norm_bwd=(h_in, g_i, dh_out))

    du1, dw1_1, dw2_1 = mlp_bwd_weights(dh4_16, hm1, u1, w2_1, "mlp1")
    dh3, dh3_16, dg_mlp1 = mlp_bwd_input(du1, dh4, h3, w1_1, g_mlp1, "mlp1")

    dyn = _matmul(dh3_16, wout, "nt", name="out_proj_dyn", out_dtype=BF16)
    dwout = _matmul(yn, dh3_16, "tn", name="out_proj_dw").reshape(N_CHIPS, D_INNER // N_CHIPS, D_MODEL)
    behind = reduce_start("mlp1_out", [dw1_1, dw2_1, dwout])
    dy, dz, dg_gate = _gate_bwd(dyn, y, z, gate_g + behind)
    behind = reduce_midway("mlp1_out", dz)
    dxc, ddt_raw, small_ssd = _ssd_bwd(xc, dt_r, acs_r, sg_r, acs_t, alog_r, dskip_x + behind,
                                       e3_mat, e_mat.T, states, dy)
    dv, dconv = _conv_bwd_act(xbc, dxc, conv_w, conv_b)
    dxbc = _conv_bwd_in(dv, conv_w)
    dw_z = _matmul(dz, hn1, "tn", name="in_proj_z_dw")
    dw_xbc = _matmul(dxbc, hn1, "tn", name="in_proj_xbc_dw")
    dw_dt = _matmul(ddt_raw, hn1, "tn", name="in_proj_dt_dw")
    dwin = jnp.concatenate([dw_z, dw_xbc, dw_dt[:N_HEADS]], axis=0)
    dwin = _pad_rows(dwin.reshape(N_CHIPS, IN_PROJ_DIM // N_CHIPS, D_MODEL), _IN_PROJ_SHARD_ROWS)
    behind = reduce_start("in_proj", [dwin])
    dh2, dh2_16, dg_mix1 = _matmul(dxbc, w_xbc, "nn", name="in_proj_dh", more=[(dz, w_z), (ddt_raw, w_dt)],
                                   norm_bwd=(h2, g_mix1 + behind, dh3))
    behind = reduce_midway("in_proj", dh2_16)

    du0, dw1_0, dw2_0 = mlp_bwd_weights(dh2_16, hm0, u0, w2_0, "mlp0")
    dh1, _, dg_mlp0 = mlp_bwd_input(du0, dh2, h1, w1_0, g_mlp0 + behind, "mlp0")
    dx, dpw, small_pool = _pool_bwd(xs, g_mix0, pw, pool_b, pool_scale, dh1)
    dpw = jnp.transpose(dpw.reshape(4, N_CHIPS, POOL_GROUP // N_CHIPS, POOL_GROUP), (1, 0, 2, 3))
    dpw = dpw.reshape(N_CHIPS, 4 * (POOL_GROUP // N_CHIPS), POOL_GROUP)

    big = [dpw, dw1_0, dw2_0]
    rows = [
        small_final[1:2],
        small_final[0:1],
        small_pool[0:1], dg_mix1[0:1],
        dg_mlp0[0:1], dg_mlp1[0:1],
        small_pool[1:2], small_pool[2:3],
        _pad_lanes(small_ssd[0:3], D_MODEL),
        _rows_1024(dg_gate[0:1]),
        _rows_1024(dconv[0:CONV_K]),
        _rows_1024(dconv[CONV_K:CONV_K + 1]),
    ]
    return dx, big, rows


def kernel(x, norm_mix_g, norm_mlp_g, pool_w, pool_b, pool_scale, ssm_w_in, ssm_conv_w, ssm_conv_b, ssm_dt_bias, ssm_a_log, ssm_d, ssm_norm_g, ssm_w_out, mlp_w1, mlp_w2, final_g, loss_target, m_norm_mix_g, m_norm_mlp_g, m_pool_w, m_pool_b, m_pool_scale, m_ssm_w_in, m_ssm_conv_w, m_ssm_conv_b, m_ssm_dt_bias, m_ssm_a_log, m_ssm_d, m_ssm_norm_g, m_ssm_w_out, m_mlp_w1, m_mlp_w2, m_final_g, v_norm_mix_g, v_norm_mlp_g, v_pool_w, v_pool_b, v_pool_scale, v_ssm_w_in, v_ssm_conv_w, v_ssm_conv_b, v_ssm_dt_bias, v_ssm_a_log, v_ssm_d, v_ssm_norm_g, v_ssm_w_out, v_mlp_w1, v_mlp_w2, v_final_g):
    xs = x[0]
    target = loss_target[0]
    my_x, my_y, my_c = _position()
    my_chip = 2 * my_x + my_y

    def halves(w):
        return w.astype(BF16).reshape((2, w.shape[0] // 2) + w.shape[1:])

    def whole(gathered, own_shard):
        g = lax.dynamic_update_index_in_dim(gathered, own_shard, my_chip, axis=0)
        return g.reshape((N_CHIPS, 2 * g.shape[2]) + g.shape[3:])

    def gather_lands(own):
        return [jax.ShapeDtypeStruct((N_CHIPS,) + s.shape, s.dtype) for s in own]

    vec_cols = CONV_DIM // N_CHIPS
    vec_own = jnp.concatenate([ssm_conv_w[0], ssm_conv_b, _pad_lanes(ssm_norm_g, vec_cols)], axis=0)
    early_own = [halves(pool_w[0]), vec_own.reshape(2, (CONV_K + 2) // 2, vec_cols)]
    early, behind_early = _all_gather_weights(early_own)
    g_pool, g_vec = [whole(g, o) for g, o in zip(early, early_own)]
    pw = jnp.transpose(g_pool, (1, 0, 2, 3)).reshape(4, POOL_GROUP, POOL_GROUP)
    conv_w = jnp.transpose(g_vec[:, 0:CONV_K, :], (1, 0, 2)).reshape(CONV_K, CONV_DIM)
    conv_b = g_vec[:, CONV_K, :].reshape(1, CONV_DIM)
    gate_g = g_vec[:, CONV_K + 1, :D_INNER // N_CHIPS].reshape(1, D_INNER)

    def behind_it(zero, ws):
        return [halves(w + zero) for w in ws]

    fetches = {}
    up_own = behind_it(behind_early, [mlp_w1[0]])
    fetches["mlp0_up"], behind_gather = _exchange_start(
        "gather_mlp0_up_start", _gather_copies, len(up_own) * (N_CHIPS - 1), up_own, gather_lands(up_own))
    down_own = behind_it(behind_gather, [mlp_w2[0]])
    fetches["mlp0_down"], behind_gather = _exchange_start(
        "gather_mlp0_down_start", _gather_copies, len(down_own) * (N_CHIPS - 1), down_own, gather_lands(down_own))
    in_own = behind_it(behind_gather, [_pad_rows(ssm_w_in[0].T, _IN_PROJ_SHARD_ROWS)])
    fetches["in_proj"], behind_gather = _exchange_start(
        "gather_in_proj_start", _gather_copies, len(in_own) * (N_CHIPS - 1), in_own, gather_lands(in_own))

    def fetch(what, after):
        if what == "mlp0_up":
            own_thru, landed = _exchange_wait("gather_mlp0_up_wait", _gather_copies, fetches[what], after)
            landed, _ = _forward_halves(landed, name="forward_mlp0_up")
            return whole(landed[0], own_thru[0])
        if what == "mlp0_down":
            own_thru, landed = _exchange_wait("gather_mlp0_down_wait", _gather_copies, fetches[what], after)
            landed, _ = _forward_halves(landed, name="forward_mlp0_down")
            return whole(landed[0], own_thru[0]).reshape(D_FF, D_MODEL)
        if what == "in_proj":
            own_thru, landed = _exchange_wait("gather_in_proj_wait", _gather_copies, fetches["in_proj"], after)
            landed, behind = _forward_halves(landed, name="forward_in_proj")
            rest = behind_it(behind, [ssm_w_out[0], mlp_w1[1], mlp_w2[1]])
            fetches["rest"], behind = _exchange_start(
                "gather_rest_start", _gather_copies, len(rest) * (N_CHIPS - 1), rest, gather_lands(rest))
            win = whole(landed[0], own_thru[0])[:, :IN_PROJ_DIM // N_CHIPS].reshape(IN_PROJ_DIM, D_MODEL)
            w_dt = _pad_rows(win[D_INNER + CONV_DIM:], 128) + behind.astype(BF16)
            return win[:D_INNER], win[D_INNER:D_INNER + CONV_DIM], w_dt
        own_thru, landed = _exchange_wait("gather_rest_wait", _gather_copies, fetches["rest"], after)
        landed, _ = _forward_halves(landed, name="forward_rest")
        g_wout, w1_1, g_w2_1 = [whole(g, o) for g, o in zip(landed, own_thru)]
        return g_wout.reshape(D_INNER, D_MODEL), w1_1, g_w2_1.reshape(D_FF, D_MODEL)

    place = jnp.stack([my_c, my_chip]).astype(jnp.int32)
    waves = {}

    def reduce_start(wave, grads):
        waves[wave] = {}
        waves[wave]["pair"], behind = _exchange_start(
            "pair_%s_start" % wave, _pair_copies, len(grads), grads, _pair_lands(grads))
        return behind

    def reduce_midway(wave, after):
        st = waves[wave]
        grads, recv = _exchange_wait("pair_%s_wait" % wave, _pair_copies, st["pair"], after)
        sums = [_pair_sum(g, r, place, name="pair_sum_%s_%d" % (wave, i))
                for i, (g, r) in enumerate(zip(grads, recv))]
        st["f32"] = [s32 for _, s32 in sums]
        b16 = [s16 for s16, _ in sums]
        st["chip"], behind = _exchange_start(
            "chip_%s_start" % wave, _chip_copies, len(b16) * (N_CHIPS - 1), b16, _same_lands(b16))
        return behind

    def reduce_finish(wave, after):
        st = waves[wave]
        _, got = _exchange_wait("chip_%s_wait" % wave, _chip_copies, st["chip"], after)
        return [_chip_sum(s32, r, place, name="chip_sum_%s_%d" % (wave, i))
                for i, (s32, r) in enumerate(zip(st["f32"], got))]

    dx, big0, rows = _local_step(xs, target, pw, fetch, reduce_start, reduce_midway,
                                 conv_w, conv_b, gate_g,
                                 norm_mix_g + behind_gather, norm_mlp_g, pool_b, pool_scale,
                                 ssm_dt_bias, ssm_a_log, ssm_d, final_g)

    behind = reduce_start("layer0", big0)
    small = jnp.concatenate(rows, axis=0)
    small = jnp.pad(small, ((0, (-small.shape[0]) % 8), (0, 0))) + behind
    small = _all_reduce_small(small, name="all_reduce_small", sum_row0=True)
    behind = reduce_midway("layer0", small)
    h_w1_1, h_w2_1, h_wout = reduce_finish("mlp1_out", behind)
    (h_win,) = reduce_finish("in_proj", behind)
    g_w1_1, g_w2_1, g_wout_s, g_win_s = _pair_gather_halves([h_w1_1, h_w2_1, h_wout, h_win],
                                                            name="pair_gather_layer1")
    loss = small[0, 0]
    g_final = small[1]
    g_norm_mix = small[2:4]
    g_norm_mlp = small[4:6]
    g_pool_b, g_pool_scale = small[6:7], small[7:8]
    g_alog, g_dtb, g_dsk = small[8:9, :N_HEADS], small[9:10, :N_HEADS], small[10:11, :N_HEADS]
    g_gate_full = small[11:13].reshape(1, D_INNER)
    g_convw_full = small[13:25].reshape(CONV_K, CONV_DIM)
    g_convb_full = small[25:28].reshape(1, CONV_DIM)
    g_gate = lax.dynamic_slice_in_dim(g_gate_full, my_chip * (D_INNER // N_CHIPS), D_INNER // N_CHIPS, axis=1)
    g_convw = lax.dynamic_slice_in_dim(g_convw_full, my_chip * (CONV_DIM // N_CHIPS), CONV_DIM // N_CHIPS, axis=1)
    g_convb = lax.dynamic_slice_in_dim(g_convb_full, my_chip * (CONV_DIM // N_CHIPS), CONV_DIM // N_CHIPS, axis=1)

    grads = {
        "norm_mix_g": g_norm_mix, "norm_mlp_g": g_norm_mlp,
        "pool_b": g_pool_b, "pool_scale": g_pool_scale,
        "ssm_conv_w": g_convw.reshape(ssm_conv_w.shape),
        "ssm_conv_b": g_convb, "ssm_dt_bias": g_dtb, "ssm_a_log": g_alog, "ssm_d": g_dsk,
        "ssm_norm_g": g_gate, "ssm_w_out": g_wout_s.reshape(ssm_w_out.shape),
        "final_g": g_final,
    }
    weights = dict(norm_mix_g=norm_mix_g, norm_mlp_g=norm_mlp_g, pool_w=pool_w, pool_b=pool_b,
                   pool_scale=pool_scale, ssm_w_in=ssm_w_in, ssm_conv_w=ssm_conv_w, ssm_conv_b=ssm_conv_b,
                   ssm_dt_bias=ssm_dt_bias, ssm_a_log=ssm_a_log, ssm_d=ssm_d, ssm_norm_g=ssm_norm_g,
                   ssm_w_out=ssm_w_out, mlp_w1=mlp_w1, mlp_w2=mlp_w2, final_g=final_g)
    moms = dict(norm_mix_g=(m_norm_mix_g, v_norm_mix_g), norm_mlp_g=(m_norm_mlp_g, v_norm_mlp_g),
                pool_w=(m_pool_w, v_pool_w), pool_b=(m_pool_b, v_pool_b),
                pool_scale=(m_pool_scale, v_pool_scale), ssm_w_in=(m_ssm_w_in, v_ssm_w_in),
                ssm_conv_w=(m_ssm_conv_w, v_ssm_conv_w), ssm_conv_b=(m_ssm_conv_b, v_ssm_conv_b),
                ssm_dt_bias=(m_ssm_dt_bias, v_ssm_dt_bias), ssm_a_log=(m_ssm_a_log, v_ssm_a_log),
                ssm_d=(m_ssm_d, v_ssm_d), ssm_norm_g=(m_ssm_norm_g, v_ssm_norm_g),
                ssm_w_out=(m_ssm_w_out, v_ssm_w_out), mlp_w1=(m_mlp_w1, v_mlp_w1),
                mlp_w2=(m_mlp_w2, v_mlp_w2), final_g=(m_final_g, v_final_g))
    names = list(weights)
    big_names = ("pool_w", "ssm_w_in", "ssm_w_out", "mlp_w1", "mlp_w2")
    deltas, new_m, new_v = {}, {}, {}

    def as_rows(nm, a):
        return a[0].T if nm == "ssm_w_in" else a.reshape(-1, a.shape[-1])

    def from_rows(nm, r):
        return r.T[None] if nm == "ssm_w_in" else r.reshape(weights[nm].shape)

    def update(nm, grad_rows, layer=None, into=None):
        return _adamw(as_rows(nm, weights[nm]), grad_rows, as_rows(nm, moms[nm][0]), as_rows(nm, moms[nm][1]),
                      name="adamw_%s_%s" % (nm, layer), part=layer, into=into)

    def keep(nm, results):
        deltas[nm], new_m[nm], new_v[nm], grads[nm] = [from_rows(nm, r) for r in results]

    keep("ssm_w_in", update("ssm_w_in", g_win_s[:IN_PROJ_DIM // N_CHIPS]))
    keep("ssm_w_out", update("ssm_w_out", g_wout_s))
    w1_done = update("mlp_w1", g_w1_1, layer=1)
    w2_done = update("mlp_w2", g_w2_1, layer=1)
    small_names = [nm for nm in names if nm not in big_names]
    small_done = _adamw_small(
        [tuple(as_rows(nm, a) for a in (weights[nm], grads[nm], moms[nm][0], moms[nm][1])) for nm in small_names],
        name="adamw_small")
    for nm, (d_, m_, v_) in zip(small_names, small_done):
        deltas[nm], new_m[nm], new_v[nm] = [from_rows(nm, r) for r in (d_, m_, v_)]

    above = (deltas["ssm_w_in"][0, 0, 0] + deltas["ssm_w_out"][0, 0, 0] + w1_done[0][-1, -1]
             + w2_done[0][-1, -1] + small_done[0][0][0, 0])
    g_pool_w, g_w1_0, g_w2_0 = _pair_gather_halves(reduce_finish("layer0", above), name="pair_gather_layer0")
    keep("mlp_w1", update("mlp_w1", g_w1_0, layer=0, into=w1_done))
    keep("mlp_w2", update("mlp_w2", g_w2_0, layer=0, into=w2_done))
    keep("pool_w", update("pool_w", g_pool_w))

    grad_x = dx.reshape(x.shape)
    out_grads = [grads[nm].reshape(weights[nm].shape) for nm in names]
    return (loss, grad_x, *out_grads, *[deltas[nm] for nm in names],
            *[new_m[nm] for nm in names], *[new_v[nm] for nm in names])
```

```python
import jax
import jax.numpy as jnp
from jax import lax
from jax.experimental import pallas as pl
from jax.experimental.pallas import tpu as pltpu

F32 = jnp.float32
BF16 = jnp.bfloat16
MESH = pl.DeviceIdType.MESH

D_MODEL = 1024
RMS_EPS = 1e-5
POOL_WINDOWS = (2, 4, 8, 16)
POOL_GROUP = 256
POOL_HALO = 16
D_INNER = 2048
HEAD_DIM = 64
N_HEADS = 32
N_GROUPS = 4
HEADS_PER_GROUP = 8
D_STATE = 128
CONV_K = 4
CONV_HALO = 8
CHUNK = 128
CONV_DIM = 3072
IN_PROJ_DIM = 5152
D_FF = 4096
N_CHIPS = 4
N_DEV = 8

ADAM_LR = 0.001
ADAM_B1 = 0.9
ADAM_B2 = 0.999
ADAM_EPS = 1e-08
ADAM_WD = 0.01
ADAM_STEP = 10

VMEM_LIMIT = 56 * 1024 * 1024
NEG_INF = float("-inf")


def _pcall(body, **kw):
    return pl.pallas_call(body, **kw)


def _params(*sem):
    return pltpu.CompilerParams(dimension_semantics=sem, vmem_limit_bytes=VMEM_LIMIT)


def _sigmoid(v):
    return 1.0 / (1.0 + jnp.exp(-v))


def _row_spec(tb, d, nb=None, reverse=False):
    if reverse:
        return pl.BlockSpec((tb, d), lambda i: (nb - 1 - i, 0))
    return pl.BlockSpec((tb, d), lambda i: (i, 0))


def _const_spec(shape):
    return pl.BlockSpec(shape, lambda *_: tuple(0 for _ in shape))


_DIMS = {"nn": (((1,), (0,)), ((), ())),
         "nt": (((1,), (1,)), ((), ())),
         "tn": (((0,), (0,)), ((), ()))}


_MATMUL_VMEM_BUDGET = 40 * 1024 * 1024


def _matmul_tiles(m_dim, n_dim, k_dim, a_bytes, b_bytes, mn_bytes):
    tm, tn = min(m_dim, 2048), min(n_dim, 1024)
    while m_dim % tm:
        tm //= 2
    while 2 * (tm * k_dim * a_bytes + tn * k_dim * b_bytes + tm * tn * mn_bytes) > _MATMUL_VMEM_BUDGET:
        if tm >= tn:
            tm //= 2
        else:
            tn //= 2
    return tm, tn


def _matmul(a, b, mode, *, name, out_dtype=F32, a_relu2=False, a_norm=None, add=None, relu2_grad_of=None,
            out_col_shards=1, b_col_shards=False, norm_bwd=None, loss_head=None, more=()):
    if mode == "tn":
        k_dim, m_dim = a.shape
    else:
        m_dim, k_dim = a.shape
    if b_col_shards:
        n_shards, shard_cols = b.shape[0], b.shape[2]
        n_dim = n_shards * shard_cols if mode == "nn" else b.shape[1]
    else:
        n_dim = b.shape[0] if mode == "nt" else b.shape[1]
    mn_bytes = jnp.dtype(out_dtype).itemsize
    if relu2_grad_of is not None:
        mn_bytes += relu2_grad_of.dtype.itemsize
    if add is not None:
        mn_bytes += add.dtype.itemsize
    row_epilogue = norm_bwd is not None or loss_head is not None
    if row_epilogue:
        assert out_dtype == F32 and out_col_shards == 1 and (norm_bwd is None or loss_head is None)
        mn_bytes += 4 + 4 + 2 + 8
    a_bytes = a.dtype.itemsize + (2 if a_norm is not None else 0)
    k_all = k_dim + sum(a_i.shape[1] for a_i, _ in more)
    tm, tn = _matmul_tiles(m_dim, n_dim, k_all, a_bytes, b.dtype.itemsize, mn_bytes)
    if row_epilogue:
        while tn < n_dim:
            tm, tn = tm // 2, tn * 2
    assert m_dim % tm == 0 and n_dim % tn == 0
    a_spec = (pl.BlockSpec((k_dim, tm), lambda i, j: (0, i)) if mode == "tn"
              else pl.BlockSpec((tm, k_dim), lambda i, j: (i, 0)))
    if b_col_shards and mode == "nn":
        assert shard_cols % tn == 0
        per_shard = shard_cols // tn
        b_spec = pl.BlockSpec((None, k_dim, tn), lambda i, j: (j // per_shard, 0, j % per_shard))
    elif b_col_shards:
        assert mode == "nt" and k_dim == n_shards * shard_cols
        b_spec = pl.BlockSpec((n_shards, tn, shard_cols), lambda i, j: (0, j, 0))
    else:
        b_spec = (pl.BlockSpec((tn, k_dim), lambda i, j: (j, 0)) if mode == "nt"
                  else pl.BlockSpec((k_dim, tn), lambda i, j: (0, j)))
    mn_spec = pl.BlockSpec((tm, tn), lambda i, j: (i, j))
    operands, in_specs = [a, b], [a_spec, b_spec]
    for a_i, b_i in more:
        assert mode == "nn" and a_i.shape[0] == m_dim and b_i.shape == (a_i.shape[1], n_dim)
        operands += [a_i, b_i]
        in_specs += [pl.BlockSpec((tm, a_i.shape[1]), lambda i, j: (i, 0)),
                     pl.BlockSpec((a_i.shape[1], tn), lambda i, j: (0, j))]
    if relu2_grad_of is not None:
        operands.append(relu2_grad_of)
        in_specs.append(mn_spec)
    if add is not None:
        operands.append(add)
        in_specs.append(mn_spec)
    gain_spec = pl.BlockSpec((1, n_dim), lambda i, j: (0, 0))
    if a_norm is not None:
        assert mode in ("nn", "nt") and a.dtype == F32 and not row_epilogue and out_col_shards == 1
        operands.append(a_norm)
        in_specs.append(pl.BlockSpec((1, k_dim), lambda i, j: (0, 0)))
    if norm_bwd is not None:
        h_in, g_in, dres_in = norm_bwd
        operands += [h_in, g_in, dres_in]
        in_specs += [mn_spec, gain_spec, mn_spec]
    if loss_head is not None:
        operands += list(loss_head)
        in_specs += [gain_spec, mn_spec]
    if row_epilogue:
        out_shape = (jax.ShapeDtypeStruct((m_dim, n_dim), F32), jax.ShapeDtypeStruct((m_dim, n_dim), BF16),
                     jax.ShapeDtypeStruct((8, n_dim), F32))
        out_spec = (mn_spec, mn_spec, pl.BlockSpec((8, n_dim), lambda i, j: (0, 0)))
    elif out_col_shards == 1:
        out_shape = jax.ShapeDtypeStruct((m_dim, n_dim), out_dtype)
        out_spec = mn_spec
    else:
        n_shard = n_dim // out_col_shards
        assert n_shard % tn == 0
        per = n_shard // tn
        out_shape = jax.ShapeDtypeStruct((out_col_shards, m_dim, n_shard), out_dtype)
        out_spec = pl.BlockSpec((None, tm, tn), lambda i, j: (j // per, i, j % per))
    if a_norm is not None:
        out_shape = (out_shape, jax.ShapeDtypeStruct((m_dim, k_dim), BF16))
        out_spec = (out_spec, pl.BlockSpec((tm, k_dim), lambda i, j: (i, 0)))

    n_in = len(operands)

    def body(*refs):
        a_ref, b_ref, o_ref = refs[0], refs[1], refs[n_in]
        if a_norm is not None:
            normed_ref = refs[n_in + 1]

            @pl.when(pl.program_id(1) == 0)
            def _():
                xa = a_ref[...]
                normed_ref[...] = (xa * _rms(xa) * refs[n_in - 1][...]).astype(BF16)

            av = normed_ref[...]
        else:
            av = a_ref[...]
        if a_relu2:
            av = jnp.maximum(av, 0)
            av = av * av
        if b_col_shards and mode == "nt":
            r = None
            for s in range(n_shards):
                part = lax.dot_general(av[:, s * shard_cols:(s + 1) * shard_cols].astype(BF16),
                                       b_ref[s].astype(BF16), _DIMS[mode], preferred_element_type=F32)
                r = part if r is None else r + part
        else:
            r = lax.dot_general(av.astype(BF16), b_ref[...].astype(BF16), _DIMS[mode],
                                preferred_element_type=F32)
        nxt = 2
        for _ in more:
            r = r + jnp.dot(refs[nxt][...].astype(BF16), refs[nxt + 1][...].astype(BF16),
                            preferred_element_type=F32)
            nxt += 2
        if relu2_grad_of is not None:
            r = r * (2.0 * jnp.maximum(refs[nxt][...].astype(F32), 0.0))
            nxt += 1
        if add is not None:
            r = r + refs[nxt][...]
            nxt += 1
        if not row_epilogue:
            o_ref[...] = r.astype(out_dtype)
            return
        dh16_ref, small_ref = refs[n_in + 1], refs[n_in + 2]

        @pl.when(pl.program_id(0) == 0)
        def _():
            small_ref[...] = jnp.zeros_like(small_ref)

        if norm_bwd is not None:
            h_ref, g_ref, dres_ref = refs[nxt:nxt + 3]
            x, dy = h_ref[...], r
        else:
            g_ref, t_ref = refs[nxt:nxt + 2]
            x = r
        rr = _rms(x)
        xhat = x * rr
        gv = g_ref[...]
        if loss_head is not None:
            err = xhat * gv - t_ref[...]
            small_ref[1:2, :] += (0.5 / n_dim) * jnp.sum(err * err, axis=0, keepdims=True)
            dy = err * (1.0 / n_dim)
        dxhat = dy * gv
        dh = rr * (dxhat - xhat * jnp.mean(dxhat * xhat, axis=-1, keepdims=True))
        if norm_bwd is not None:
            dh = dres_ref[...] + dh
        o_ref[...] = dh
        dh16_ref[...] = dh.astype(BF16)
        small_ref[0:1, :] += jnp.sum(dy * xhat, axis=0, keepdims=True)

    if row_epilogue:
        semantics = ("arbitrary", "arbitrary")
    elif a_norm is not None:
        semantics = ("parallel", "arbitrary")
    else:
        semantics = ("parallel", "parallel")
    return _pcall(
        body, name=name, out_shape=out_shape,
        grid=(m_dim // tm, n_dim // tn),
        in_specs=in_specs, out_specs=out_spec,
        compiler_params=_params(*semantics),
    )(*operands)


def _rms(x):
    return lax.rsqrt(jnp.mean(x * x, axis=-1, keepdims=True) + RMS_EPS)


def _pool_mixed(ext, hn, t0, tb):
    t = t0 + lax.broadcasted_iota(jnp.int32, (tb, 1), 0)
    parts = []
    for gi, w in enumerate(POOL_WINDOWS):
        lanes = slice(gi * POOL_GROUP, (gi + 1) * POOL_GROUP)
        s = ext[:, lanes]
        k = 1
        while k < w:
            s = s + pltpu.roll(s, k, 0)
            k *= 2
        cnt = jnp.minimum(t + 1, w).astype(F32)
        parts.append(s[POOL_HALO:, :] / cnt - hn[:, lanes])
    return parts


def _pool_fwd(x, g, pw, pb, ps, *, tb=512):
    t_dim, d = x.shape

    def body(x_ref, g_ref, pw_ref, pb_ref, ps_ref, o_ref, ext_ref):
        i = pl.program_id(0)

        @pl.when(i == 0)
        def _():
            ext_ref[0:POOL_HALO, :] = jnp.zeros((POOL_HALO, d), F32)

        xv = x_ref[...]
        hn = xv * _rms(xv) * g_ref[...]
        ext_ref[POOL_HALO:, :] = hn
        mixed = _pool_mixed(ext_ref[...], hn, i * tb, tb)
        for gi in range(len(POOL_WINDOWS)):
            lanes = slice(gi * POOL_GROUP, (gi + 1) * POOL_GROUP)
            out = jnp.dot(mixed[gi].astype(BF16), pw_ref[gi], preferred_element_type=F32)
            o_ref[:, lanes] = xv[:, lanes] + (out + pb_ref[:, lanes]) * ps_ref[:, lanes]
        ext_ref[0:POOL_HALO, :] = hn[tb - POOL_HALO:, :]

    return _pcall(
        body, name="pool_fwd", out_shape=jax.ShapeDtypeStruct((t_dim, d), F32),
        grid=(t_dim // tb,),
        in_specs=[_row_spec(tb, d), _const_spec((1, d)), _const_spec((4, POOL_GROUP, POOL_GROUP)),
                  _const_spec((1, d)), _const_spec((1, d))],
        out_specs=_row_spec(tb, d),
        scratch_shapes=[pltpu.VMEM((POOL_HALO + tb, d), F32)],
        compiler_params=_params("arbitrary"),
    )(x, g, pw, pb, ps)


def _pool_bwd(x, g, pw, pb, ps, dh1, *, tb=512):
    t_dim, d = x.shape
    nb = t_dim // tb
    halo_per_block = tb // POOL_HALO

    def body(x_ref, xprev_ref, g_ref, pw_ref, pb_ref, ps_ref, dh1_ref,
             dx_ref, dpw_ref, small_ref, ext_ref, dext_ref):
        i = pl.program_id(0)
        blk = nb - 1 - i

        @pl.when(i == 0)
        def _():
            dpw_ref[...] = jnp.zeros_like(dpw_ref)
            small_ref[...] = jnp.zeros_like(small_ref)
            dext_ref[tb:, :] = jnp.zeros((POOL_HALO, d), F32)

        gv = g_ref[...]
        xv = x_ref[...]
        r = _rms(xv)
        xhat = xv * r
        hn = xhat * gv
        xp = xprev_ref[...]
        hprev = xp * _rms(xp) * gv * (blk > 0).astype(F32)
        ext_ref[0:POOL_HALO, :] = hprev
        ext_ref[POOL_HALO:, :] = hn
        mixed = _pool_mixed(ext_ref[...], hn, blk * tb, tb)

        dout = dh1_ref[...]
        t = blk * tb + lax.broadcasted_iota(jnp.int32, (tb, 1), 0)
        for gi, w in enumerate(POOL_WINDOWS):
            lanes = slice(gi * POOL_GROUP, (gi + 1) * POOL_GROUP)
            mb = mixed[gi].astype(BF16)
            pre = jnp.dot(mb, pw_ref[gi], preferred_element_type=F32) + pb_ref[:, lanes]
            dg_out = dout[:, lanes]
            small_ref[2:3, lanes] += jnp.sum(dg_out * pre, axis=0, keepdims=True)
            dpre = dg_out * ps_ref[:, lanes]
            small_ref[1:2, lanes] += jnp.sum(dpre, axis=0, keepdims=True)
            dpb16 = dpre.astype(BF16)
            dpw_ref[gi] += lax.dot_general(mb, dpb16, _DIMS["tn"], preferred_element_type=F32)
            dmixed = lax.dot_general(dpb16, pw_ref[gi], _DIMS["nt"], preferred_element_type=F32)
            cnt = jnp.minimum(t + 1, w).astype(F32)
            dq = dmixed / cnt
            dext_ref[0:tb, lanes] = dq
            s = dext_ref[:, lanes]
            k = 1
            while k < w:
                s = s + pltpu.roll(s, tb + POOL_HALO - k, 0)
                k *= 2
            dhn = s[0:tb, :] - dmixed
            dext_ref[tb:, lanes] = dq[0:POOL_HALO, :]
            small_ref[0:1, lanes] += jnp.sum(dhn * xhat[:, lanes], axis=0, keepdims=True)
            ext_ref[POOL_HALO:, lanes] = dhn * gv[:, lanes]
        dxhat = ext_ref[POOL_HALO:, :]
        dx_ref[...] = dout + r * (dxhat - xhat * jnp.mean(dxhat * xhat, axis=-1, keepdims=True))

    return _pcall(
        body, name="pool_bwd",
        out_shape=(jax.ShapeDtypeStruct((t_dim, d), F32),
                   jax.ShapeDtypeStruct((4, POOL_GROUP, POOL_GROUP), F32),
                   jax.ShapeDtypeStruct((8, d), F32)),
        grid=(nb,),
        in_specs=[_row_spec(tb, d, nb, True),
                  pl.BlockSpec((POOL_HALO, d),
                               lambda i: (jnp.maximum((nb - 1 - i) * halo_per_block - 1, 0), 0)),
                  _const_spec((1, d)), _const_spec((4, POOL_GROUP, POOL_GROUP)),
                  _const_spec((1, d)), _const_spec((1, d)), _row_spec(tb, d, nb, True)],
        out_specs=(_row_spec(tb, d, nb, True), _const_spec((4, POOL_GROUP, POOL_GROUP)),
                   _const_spec((8, d))),
        scratch_shapes=[pltpu.VMEM((POOL_HALO + tb, d), F32), pltpu.VMEM((tb + POOL_HALO, d), F32)],
        compiler_params=_params("arbitrary"),
    )(x, x, g, pw, pb, ps, dh1)


_CONV_CB = 1024
_STRIP = 16


def _strips(tb, fn, unroll=4):
    def step(i, carry):
        fn(pl.multiple_of(i * _STRIP, _STRIP))
        return carry
    lax.fori_loop(0, tb // _STRIP, step, 0, unroll=unroll)


def _conv_taps(ext_ref, r0, w):
    shifted = [ext_ref[CONV_HALO + r0 - sh:CONV_HALO + r0 - sh + _STRIP, :] for sh in range(CONV_K)]
    acc = shifted[0] * w[CONV_K - 1:CONV_K, :]
    for sh in range(1, CONV_K):
        acc = acc + shifted[sh] * w[CONV_K - 1 - sh:CONV_K - sh, :]
    return shifted, acc


def _conv_fwd(u, w, b, *, tb=512):
    t_dim, c = u.shape
    cb = _CONV_CB

    def body(u_ref, w_ref, b_ref, o_ref, ext_ref):
        @pl.when(pl.program_id(1) == 0)
        def _():
            ext_ref[0:CONV_HALO, :] = jnp.zeros((CONV_HALO, cb), F32)

        wv = w_ref[...]
        bv = b_ref[...]

        def fill(r0):
            ext_ref[pl.ds(CONV_HALO + r0, _STRIP), :] = u_ref[pl.ds(r0, _STRIP), :].astype(F32)

        _strips(tb, fill)
        for r0 in range(0, tb, _STRIP):
            v = _conv_taps(ext_ref, r0, wv)[1] + bv
            o_ref[r0:r0 + _STRIP, :] = (v * _sigmoid(v)).astype(BF16)
        ext_ref[0:CONV_HALO, :] = ext_ref[tb:tb + CONV_HALO, :]

    blk = pl.BlockSpec((tb, cb), lambda j, t: (t, j))
    return _pcall(
        body, name="conv_fwd", out_shape=jax.ShapeDtypeStruct((t_dim, c), BF16),
        grid=(c // cb, t_dim // tb),
        in_specs=[blk, pl.BlockSpec((CONV_K, cb), lambda j, t: (0, j)),
                  pl.BlockSpec((1, cb), lambda j, t: (0, j))],
        out_specs=blk,
        scratch_shapes=[pltpu.VMEM((CONV_HALO + tb, cb), F32)],
        compiler_params=_params("parallel", "arbitrary"),
    )(u, w, b)


def _conv_bwd_act(u, dxc, w, b, *, tb=512):
    t_dim, c = u.shape
    cb = _CONV_CB
    half = _STRIP // 2

    def body(u_ref, d_ref, w_ref, b_ref, dv_ref, dwb_ref, ext_ref, acc_ref):
        @pl.when(pl.program_id(1) == 0)
        def _():
            ext_ref[0:CONV_HALO, :] = jnp.zeros((CONV_HALO, cb), F32)
            dwb_ref[...] = jnp.zeros_like(dwb_ref)

        acc_ref[...] = jnp.zeros_like(acc_ref)
        wv = w_ref[...]
        bv = b_ref[...]

        def fill(r0):
            ext_ref[pl.ds(CONV_HALO + r0, _STRIP), :] = u_ref[pl.ds(r0, _STRIP), :].astype(F32)

        _strips(tb, fill)
        for r0 in range(0, tb, _STRIP):
            shifted, v = _conv_taps(ext_ref, r0, wv)
            v = v + bv
            sg = _sigmoid(v)
            dv = d_ref[r0:r0 + _STRIP, :].astype(F32) * (sg * (1.0 + v * (1.0 - sg)))
            dv_ref[r0:r0 + _STRIP, :] = dv.astype(BF16)
            acc_ref[CONV_K] += dv[0:half, :] + dv[half:, :]
            for sh in range(CONV_K):
                p = dv * shifted[sh]
                acc_ref[CONV_K - 1 - sh] += p[0:half, :] + p[half:, :]
        for k in range(CONV_K + 1):
            dwb_ref[k:k + 1, :] += jnp.sum(acc_ref[k], axis=0, keepdims=True)
        ext_ref[0:CONV_HALO, :] = ext_ref[tb:tb + CONV_HALO, :]

    blk = pl.BlockSpec((tb, cb), lambda j, t: (t, j))
    return _pcall(
        body, name="conv_bwd_act",
        out_shape=(jax.ShapeDtypeStruct((t_dim, c), BF16), jax.ShapeDtypeStruct((8, c), F32)),
        grid=(c // cb, t_dim // tb),
        in_specs=[blk, blk, pl.BlockSpec((CONV_K, cb), lambda j, t: (0, j)),
                  pl.BlockSpec((1, cb), lambda j, t: (0, j))],
        out_specs=(blk, pl.BlockSpec((8, cb), lambda j, t: (0, j))),
        scratch_shapes=[pltpu.VMEM((CONV_HALO + tb, cb), F32), pltpu.VMEM((CONV_K + 1, half, cb), F32)],
        compiler_params=_params("parallel", "arbitrary"),
    )(u, dxc, w, b)


def _conv_bwd_in(dv, w, *, tb=512):
    t_dim, c = dv.shape
    cb = _CONV_CB
    nb = t_dim // tb

    def body(dv_ref, w_ref, du_ref, ext_ref):
        @pl.when(pl.program_id(1) == 0)
        def _():
            ext_ref[tb:, :] = jnp.zeros((CONV_HALO, cb), F32)

        wv = w_ref[...]

        def fill(r0):
            ext_ref[pl.ds(r0, _STRIP), :] = dv_ref[pl.ds(r0, _STRIP), :].astype(F32)

        _strips(tb, fill)
        for r0 in range(0, tb, _STRIP):
            acc = ext_ref[r0:r0 + _STRIP, :] * wv[CONV_K - 1:CONV_K, :]
            for sh in range(1, CONV_K):
                acc = acc + ext_ref[r0 + sh:r0 + sh + _STRIP, :] * wv[CONV_K - 1 - sh:CONV_K - sh, :]
            du_ref[r0:r0 + _STRIP, :] = acc.astype(BF16)
        ext_ref[tb:, :] = ext_ref[0:CONV_HALO, :]

    blk = pl.BlockSpec((tb, cb), lambda j, t: (nb - 1 - t, j))
    return _pcall(
        body, name="conv_bwd_in", out_shape=jax.ShapeDtypeStruct((t_dim, c), BF16),
        grid=(c // cb, nb),
        in_specs=[blk, pl.BlockSpec((CONV_K, cb), lambda j, t: (0, j))],
        out_specs=blk,
        scratch_shapes=[pltpu.VMEM((tb + CONV_HALO, cb), F32)],
        compiler_params=_params("parallel", "arbitrary"),
    )(dv, w)


def _softplus(v):
    e = jnp.exp(-jnp.abs(v))
    w = 1.0 + e
    log1p = jnp.where(w == 1.0, e, jnp.log(w) * e / jnp.where(w == 1.0, 1.0, w - 1.0))
    return jnp.maximum(v, 0.0) + log1p


def _cumsum_rows(v):
    row = lax.broadcasted_iota(jnp.int32, v.shape, 0) & (CHUNK - 1)
    k = 1
    while k < CHUNK:
        v = v + jnp.where(row >= k, pltpu.roll(v, k, 0), 0.0)
        k *= 2
    return v


def _cumsum_lanes(v):
    col = lax.broadcasted_iota(jnp.int32, v.shape, 1) & (CHUNK - 1)
    k = 1
    while k < CHUNK:
        v = v + jnp.where(col >= k, pltpu.roll(v, k, 1), 0.0)
        k *= 2
    return v


def _rev_cumsum_rows(v):
    row = lax.broadcasted_iota(jnp.int32, v.shape, 0)
    k = 1
    while k < CHUNK:
        v = v + jnp.where(row < CHUNK - k, pltpu.roll(v, CHUNK - k, 0), 0.0)
        k *= 2
    return v


PAIR = 2 * HEAD_DIM
GROUP_LANES = HEADS_PER_GROUP * HEAD_DIM


def _head_lane_matrix():
    h = lax.broadcasted_iota(jnp.int32, (128, D_INNER), 0)
    j = lax.broadcasted_iota(jnp.int32, (128, D_INNER), 1)
    return (j // HEAD_DIM == h).astype(BF16)


def _split_bf16(v, pieces):
    out = []
    for _ in range(pieces):
        p = v.astype(BF16)
        out.append(p)
        v = v - p.astype(F32)
    return out


_EXACT_PIECES = 3


def _expand_heads(values, e3):
    lhs = jnp.concatenate([jnp.concatenate(_split_bf16(v, _EXACT_PIECES), axis=1) for v in values], axis=0)
    out = jnp.dot(lhs, e3, preferred_element_type=F32)
    rows = values[0].shape[0]
    return [out[i * rows:(i + 1) * rows, :] for i in range(len(values))]


def _reduce_heads(v, et, pieces):
    return sum(jnp.dot(p, et, preferred_element_type=F32) for p in _split_bf16(v, pieces))


def _ssd_decay(dt_raw, dt_raw_t, bias_r, bias_c, alog_r, alog_c, *, tb=1024):
    t_dim = dt_raw.shape[0]
    tb = min(tb, t_dim)
    assert t_dim % tb == 0 and tb % CHUNK == 0

    def body(dtr_ref, dtt_ref, br_ref, bc_ref, ar_ref, ac_ref, dt_ref, acs_ref, sg_ref, acst_ref):
        pre = dtr_ref[...] + br_ref[...]
        dt = _softplus(pre)
        dt_ref[...] = dt
        sg_ref[...] = _sigmoid(pre)
        acs_ref[...] = _cumsum_rows(dt * (-jnp.exp(ar_ref[...])))
        acst_ref[...] = _cumsum_lanes(_softplus(dtt_ref[...] + bc_ref[...]) * (-jnp.exp(ac_ref[...])))

    rows = pl.BlockSpec((tb, 128), lambda i: (i, 0))
    cols = pl.BlockSpec((N_HEADS, tb), lambda i: (0, i))
    sds = jax.ShapeDtypeStruct((t_dim, 128), F32)
    return _pcall(
        body, name="ssd_decay",
        out_shape=(sds, sds, sds, jax.ShapeDtypeStruct((N_HEADS, t_dim), F32)),
        grid=(t_dim // tb,),
        in_specs=[rows, cols, _const_spec((1, 128)), _const_spec((N_HEADS, 1)),
                  _const_spec((1, 128)), _const_spec((N_HEADS, 1))],
        out_specs=(rows, rows, rows, cols), compiler_params=_params("parallel"),
    )(dt_raw, dt_raw_t, bias_r, bias_c, alog_r, alog_c)


def _pair_decay(acs_slab, acs_c, h0, causal, left):
    other = pltpu.roll(acs_slab, HEAD_DIM, 1)
    col0 = jnp.where(left, acs_slab, other)
    col1 = jnp.where(left, other, acs_slab)
    l0 = jnp.exp(jnp.where(causal, col0 - acs_c[h0:h0 + 1, :], NEG_INF))
    l1 = jnp.exp(jnp.where(causal, col1 - acs_c[h0 + 1:h0 + 2, :], NEG_INF))
    return l0, l1


def _ssd_fwd(xc, dt_r, acs_r, acs_t, dskip_x, e3_mat):
    t_dim = xc.shape[0]
    nc = t_dim // CHUNK

    def body(xc_ref, dt_ref, acs_ref, acst_ref, dk_ref, e3_ref, y_ref, st_ref, state):
        @pl.when(pl.program_id(0) == 0)
        def _():
            state[...] = jnp.zeros_like(state)

        dt, acs = _expand_heads([dt_ref[...], acs_ref[...]], e3_ref[...])
        acs_c = acst_ref[...]
        st_ref[0] = state[...]
        last = acs[CHUNK - 1:CHUNK, :]
        xs32 = xc_ref[:, 0:D_INNER].astype(F32)
        xdt = xs32 * dt
        xdt16 = xdt.astype(BF16)
        xdte16 = (xdt * jnp.exp(last - acs)).astype(BF16)
        ea = jnp.exp(acs)
        cd = jnp.exp(last)
        skip = dk_ref[...] * xs32
        causal = (lax.broadcasted_iota(jnp.int32, (CHUNK, CHUNK), 0)
                  >= lax.broadcasted_iota(jnp.int32, (CHUNK, CHUNK), 1))
        left = lax.broadcasted_iota(jnp.int32, (CHUNK, PAIR), 1) < HEAD_DIM
        for g in range(N_GROUPS):
            gl = slice(g * GROUP_LANES, (g + 1) * GROUP_LANES)
            bg = xc_ref[:, D_INNER + g * D_STATE:D_INNER + (g + 1) * D_STATE]
            cg = xc_ref[:, D_INNER + (N_GROUPS + g) * D_STATE:D_INNER + (N_GROUPS + g + 1) * D_STATE]
            cb = lax.dot_general(cg, bg, _DIMS["nt"], preferred_element_type=F32)
            hprev = state[:, gl]
            ch = jnp.dot(cg, hprev.astype(BF16), preferred_element_type=F32)
            for j in range(HEADS_PER_GROUP // 2):
                pl_ = slice(g * GROUP_LANES + j * PAIR, g * GROUP_LANES + (j + 1) * PAIR)
                h0 = g * HEADS_PER_GROUP + 2 * j
                l0, l1 = _pair_decay(acs[:, pl_], acs_c, h0, causal, left)
                lhs = jnp.concatenate([(cb * l0).astype(BF16), (cb * l1).astype(BF16)], axis=1)
                xp = xdt16[:, pl_]
                zero = jnp.zeros_like(xp)
                rhs = jnp.concatenate([jnp.where(left, xp, zero), jnp.where(left, zero, xp)], axis=0)
                ydiag = jnp.dot(lhs, rhs, preferred_element_type=F32)
                y_ref[:, pl_] = (ydiag + ch[:, j * PAIR:(j + 1) * PAIR] * ea[:, pl_] + skip[:, pl_]).astype(BF16)
            s_new = lax.dot_general(bg, xdte16[:, gl], _DIMS["tn"], preferred_element_type=F32)
            state[:, gl] = hprev * cd[:, gl] + s_new

    rows = lambda w: pl.BlockSpec((CHUNK, w), lambda c: (c, 0))
    return _pcall(
        body, name="ssd_fwd",
        out_shape=(jax.ShapeDtypeStruct((t_dim, D_INNER), BF16),
                   jax.ShapeDtypeStruct((nc, D_STATE, D_INNER), F32)),
        grid=(nc,),
        in_specs=[rows(CONV_DIM), rows(128), rows(128), pl.BlockSpec((N_HEADS, CHUNK), lambda c: (0, c)),
                  _const_spec((1, D_INNER)), _const_spec((_EXACT_PIECES * 128, D_INNER))],
        out_specs=(rows(D_INNER), pl.BlockSpec((1, D_STATE, D_INNER), lambda c: (c, 0, 0))),
        scratch_shapes=[pltpu.VMEM((D_STATE, D_INNER), F32)],
        compiler_params=_params("arbitrary"),
    )(xc, dt_r, acs_r, acs_t, dskip_x, e3_mat)


def _ssd_bwd(xc, dt_r, acs_r, sg_r, acs_t, alog_r, dskip_x, e3_mat, et_mat, states, dy):
    t_dim = xc.shape[0]
    nc = t_dim // CHUNK

    def body(xc_ref, dt_ref, acs_ref, sg_ref, acst_ref, ar_ref, dk_ref, e3_ref, et_ref, st_ref, dy_ref,
             dxc_ref, ddt_ref, small_ref, dstate, dacs_ref, dxdt_ref, acc_x, acc_r):
        step = pl.program_id(0)

        @pl.when(step == 0)
        def _():
            dstate[...] = jnp.zeros_like(dstate)
            acc_x[...] = jnp.zeros_like(acc_x)
            acc_r[...] = jnp.zeros_like(acc_r)

        dt_r = dt_ref[...]
        a_r = -jnp.exp(ar_ref[...])
        dt, acs = _expand_heads([dt_r, acs_ref[...]], e3_ref[...])
        acs_c = acst_ref[...]
        last = acs[CHUNK - 1:CHUNK, :]
        xs32 = xc_ref[:, 0:D_INNER].astype(F32)
        xdt = xs32 * dt
        xdt16 = xdt.astype(BF16)
        dte = jnp.exp(last - acs)
        xdte = xdt * dte
        xdte16 = xdte.astype(BF16)
        cd = jnp.exp(last)
        dy16 = dy_ref[...]
        dyv = dy16.astype(F32)
        dye = dyv * jnp.exp(acs)
        dye16 = dye.astype(BF16)
        causal = (lax.broadcasted_iota(jnp.int32, (CHUNK, CHUNK), 0)
                  >= lax.broadcasted_iota(jnp.int32, (CHUNK, CHUNK), 1))
        left = lax.broadcasted_iota(jnp.int32, (CHUNK, PAIR), 1) < HEAD_DIM
        lane_id = lax.broadcasted_iota(jnp.int32, (CHUNK, 128), 1)
        row_id = lax.broadcasted_iota(jnp.int32, (CHUNK, 128), 0)
        is_last_row = lax.broadcasted_iota(jnp.int32, (CHUNK, 1), 0) == CHUNK - 1
        dacs_cols = jnp.zeros((CHUNK, 128), F32)
        dacs_rows = jnp.zeros((CHUNK, 128), F32)
        for g in range(N_GROUPS):
            gl = slice(g * GROUP_LANES, (g + 1) * GROUP_LANES)
            b_lanes = slice(D_INNER + g * D_STATE, D_INNER + (g + 1) * D_STATE)
            c_lanes = slice(D_INNER + (N_GROUPS + g) * D_STATE, D_INNER + (N_GROUPS + g + 1) * D_STATE)
            bg = xc_ref[:, b_lanes]
            cg = xc_ref[:, c_lanes]
            cb = lax.dot_general(cg, bg, _DIMS["nt"], preferred_element_type=F32)
            hprev = st_ref[0, :, gl]
            hp16 = hprev.astype(BF16)
            dhn = dstate[:, gl]
            dhn16 = dhn.astype(BF16)
            ch = jnp.dot(cg, hp16, preferred_element_type=F32)
            gmat = jnp.dot(bg, dhn16, preferred_element_type=F32)
            gx = gmat * xdte[:, gl]
            dlast = jnp.sum(gx, axis=0, keepdims=True) + cd[:, gl] * jnp.sum(dhn * hprev, axis=0, keepdims=True)
            dacs_ref[:, gl] = dye[:, gl] * ch - gx + jnp.where(is_last_row, dlast, 0.0)
            dc_acc = lax.dot_general(dye16[:, gl], hp16, _DIMS["nt"], preferred_element_type=F32)
            db_acc = lax.dot_general(xdte16[:, gl], dhn16, _DIMS["nt"], preferred_element_type=F32)
            dstate[:, gl] = dhn * cd[:, gl] + lax.dot_general(cg, dye16[:, gl], _DIMS["tn"],
                                                             preferred_element_type=F32)
            dcb = jnp.zeros((CHUNK, CHUNK), F32)
            for j in range(HEADS_PER_GROUP // 2):
                pl_ = slice(g * GROUP_LANES + j * PAIR, g * GROUP_LANES + (j + 1) * PAIR)
                h0 = g * HEADS_PER_GROUP + 2 * j
                l0, l1 = _pair_decay(acs[:, pl_], acs_c, h0, causal, left)
                m0, m1 = cb * l0, cb * l1
                lhs = jnp.concatenate([m0.astype(BF16), m1.astype(BF16)], axis=1)
                dyp = dy16[:, pl_]
                zero = jnp.zeros_like(dyp)
                both = lax.dot_general(lhs, dyp, _DIMS["tn"], preferred_element_type=F32)
                dxdt_ref[:, pl_] = (jnp.where(left, both[0:CHUNK, :], both[CHUNK:, :])
                                    + gmat[:, j * PAIR:(j + 1) * PAIR] * dte[:, pl_])
                lhs2 = jnp.concatenate([jnp.where(left, dyp, zero), jnp.where(left, zero, dyp)], axis=0)
                dm = lax.dot_general(lhs2, xdt16[:, pl_], _DIMS["nt"], preferred_element_type=F32)
                dm0, dm1 = dm[0:CHUNK, :], dm[CHUNK:, :]
                dcb = dcb + dm0 * l0 + dm1 * l1
                ds0, ds1 = dm0 * m0, dm1 * m1
                dacs_cols = jnp.where(lane_id == h0, jnp.sum(ds0, axis=1, keepdims=True), dacs_cols)
                dacs_cols = jnp.where(lane_id == h0 + 1, jnp.sum(ds1, axis=1, keepdims=True), dacs_cols)
                dacs_rows = jnp.where(row_id == h0, jnp.sum(ds0, axis=0, keepdims=True), dacs_rows)
                dacs_rows = jnp.where(row_id == h0 + 1, jnp.sum(ds1, axis=0, keepdims=True), dacs_rows)
            dcb16 = dcb.astype(BF16)
            dxc_ref[:, c_lanes] = (dc_acc + jnp.dot(dcb16, bg, preferred_element_type=F32)).astype(BF16)
            dxc_ref[:, b_lanes] = (db_acc + lax.dot_general(dcb16, cg, _DIMS["tn"],
                                                           preferred_element_type=F32)).astype(BF16)
        dxdt = dxdt_ref[...]
        dxc_ref[:, 0:D_INNER] = (dxdt * dt + dk_ref[...] * dyv).astype(BF16)
        acc_x[0:1, :] += jnp.sum(dyv * xs32, axis=0, keepdims=True)
        et = et_ref[...]
        dacs = _reduce_heads(dacs_ref[...], et, 2) + dacs_cols - dacs_rows.T
        dadt = _rev_cumsum_rows(dacs)
        ddraw = (_reduce_heads(dxdt * xs32, et, 1) + dadt * a_r) * sg_ref[...]
        ddraw = jnp.where(lane_id < N_HEADS, ddraw, 0.0)
        ddt_ref[...] = ddraw
        acc_r[0:1, :] += jnp.where(lane_id[0:1, :] < N_HEADS,
                                   jnp.sum(dadt * dt_r, axis=0, keepdims=True) * a_r, 0.0)
        acc_r[1:2, :] += jnp.sum(ddraw, axis=0, keepdims=True)

        @pl.when(step == nc - 1)
        def _():
            dd = _reduce_heads(acc_x[...], et_ref[...], 3)
            rid = lax.broadcasted_iota(jnp.int32, (8, 128), 0)
            small_ref[...] = acc_r[...] + jnp.where(rid == 2, pltpu.roll(dd, 2, 0), 0.0)

    rev = lambda w: pl.BlockSpec((CHUNK, w), lambda c: (nc - 1 - c, 0))
    return _pcall(
        body, name="ssd_bwd",
        out_shape=(jax.ShapeDtypeStruct((t_dim, CONV_DIM), BF16),
                   jax.ShapeDtypeStruct((t_dim, 128), F32),
                   jax.ShapeDtypeStruct((8, 128), F32)),
        grid=(nc,),
        in_specs=[rev(CONV_DIM), rev(128), rev(128), rev(128),
                  pl.BlockSpec((N_HEADS, CHUNK), lambda c: (0, nc - 1 - c)),
                  _const_spec((1, 128)), _const_spec((1, D_INNER)),
                  _const_spec((_EXACT_PIECES * 128, D_INNER)), _const_spec((D_INNER, 128)),
                  pl.BlockSpec((1, D_STATE, D_INNER), lambda c: (nc - 1 - c, 0, 0)),
                  rev(D_INNER)],
        out_specs=(rev(CONV_DIM), rev(128), _const_spec((8, 128))),
        scratch_shapes=[pltpu.VMEM((D_STATE, D_INNER), F32), pltpu.VMEM((CHUNK, D_INNER), F32),
                        pltpu.VMEM((CHUNK, D_INNER), F32), pltpu.VMEM((8, D_INNER), F32),
                        pltpu.VMEM((8, 128), F32)],
        compiler_params=_params("arbitrary"),
    )(xc, dt_r, acs_r, sg_r, acs_t, alog_r, dskip_x, e3_mat, et_mat, states, dy)


_GATE_GROUP = D_INNER // N_GROUPS


def _gate_fwd(y, z, g, *, tb=512):
    t_dim = y.shape[0]

    def body(y_ref, z_ref, g_ref, o_ref):
        for gi in range(N_GROUPS):
            lanes = slice(gi * _GATE_GROUP, (gi + 1) * _GATE_GROUP)
            zv = z_ref[:, lanes].astype(F32)
            wv = y_ref[:, lanes].astype(F32) * (zv * _sigmoid(zv))
            o_ref[:, lanes] = (wv * _rms(wv) * g_ref[:, lanes]).astype(BF16)

    return _pcall(
        body, name="gate_fwd", out_shape=jax.ShapeDtypeStruct((t_dim, D_INNER), BF16),
        grid=(t_dim // tb,),
        in_specs=[_row_spec(tb, D_INNER), _row_spec(tb, D_INNER), _const_spec((1, D_INNER))],
        out_specs=_row_spec(tb, D_INNER), compiler_params=_params("parallel"),
    )(y, z, g)


def _gate_bwd(dyn, y, z, g, *, tb=512):
    t_dim = y.shape[0]

    def body(d_ref, y_ref, z_ref, g_ref, dy_ref, dz_ref, dg_ref):
        @pl.when(pl.program_id(0) == 0)
        def _():
            dg_ref[...] = jnp.zeros_like(dg_ref)

        for gi in range(N_GROUPS):
            lanes = slice(gi * _GATE_GROUP, (gi + 1) * _GATE_GROUP)
            zv = z_ref[:, lanes].astype(F32)
            sg = _sigmoid(zv)
            sz = zv * sg
            yv = y_ref[:, lanes].astype(F32)
            wv = yv * sz
            r = _rms(wv)
            what = wv * r
            dv = d_ref[:, lanes].astype(F32)
            dwhat = dv * g_ref[:, lanes]
            dw = r * (dwhat - what * jnp.mean(dwhat * what, axis=-1, keepdims=True))
            dg_ref[0:1, lanes] += jnp.sum(dv * what, axis=0, keepdims=True)
            dy_ref[:, lanes] = (dw * sz).astype(BF16)
            dz_ref[:, lanes] = (dw * yv * (sg * (1.0 + zv * (1.0 - sg)))).astype(BF16)

    return _pcall(
        body, name="gate_bwd",
        out_shape=(jax.ShapeDtypeStruct((t_dim, D_INNER), BF16),
                   jax.ShapeDtypeStruct((t_dim, D_INNER), BF16),
                   jax.ShapeDtypeStruct((8, D_INNER), F32)),
        grid=(t_dim // tb,),
        in_specs=[_row_spec(tb, D_INNER), _row_spec(tb, D_INNER), _row_spec(tb, D_INNER),
                  _const_spec((1, D_INNER))],
        out_specs=(_row_spec(tb, D_INNER), _row_spec(tb, D_INNER), _const_spec((8, D_INNER))),
        compiler_params=_params("arbitrary"),
    )(dyn, y, z, g)


_ADAM_C1 = 1.0 / (1.0 - ADAM_B1 ** ADAM_STEP)
_ADAM_C2 = 1.0 / (1.0 - ADAM_B2 ** ADAM_STEP)


def _adamw_math(w, g, m, v):
    mn = ADAM_B1 * m + (1.0 - ADAM_B1) * g
    vn = ADAM_B2 * v + (1.0 - ADAM_B2) * (g * g)
    delta = -ADAM_LR * ((mn * _ADAM_C1) / (jnp.sqrt(vn * _ADAM_C2) + ADAM_EPS) + ADAM_WD * w)
    return delta, mn, vn


def _adamw(w, g, m, v, *, name, part=None, into=None):
    r_dim, c = w.shape
    rows = r_dim if part is None else r_dim // 2
    assert g.shape == (rows, c)
    tb = max(t for t in range(8, 513, 8) if rows % t == 0)
    first = 0 if part is None else part * (rows // tb)
    n_out = 3 if part is None else 4

    def body(w_ref, g_ref, m_ref, v_ref, *rest):
        outs = rest[-n_out:]
        gv = g_ref[...]
        outs[0][...], outs[1][...], outs[2][...] = _adamw_math(w_ref[...], gv, m_ref[...], v_ref[...])
        if part is not None:
            outs[3][...] = gv

    spec = pl.BlockSpec((tb, c), lambda i: (first + i, 0))
    sds = jax.ShapeDtypeStruct((r_dim, c), F32)
    in_specs = [spec, _row_spec(tb, c), spec, spec]
    operands = [w, g, m, v]
    aliases = {}
    if into is not None:
        in_specs += [_ANY] * n_out
        operands += list(into)
        aliases = {4 + i: i for i in range(n_out)}
    outs = _pcall(
        body, name=name, out_shape=(sds,) * n_out, grid=(rows // tb,),
        in_specs=in_specs, out_specs=(spec,) * n_out, input_output_aliases=aliases,
        compiler_params=_params("parallel"),
    )(*operands)
    return tuple(outs) if part is not None else tuple(outs) + (g,)


def _adamw_small(params, *, name):
    n = len(params)

    def body(*refs):
        ins, outs = refs[:4 * n], refs[4 * n:]
        for i in range(n):
            w_ref, g_ref, m_ref, v_ref = ins[4 * i:4 * i + 4]
            res = _adamw_math(w_ref[...], g_ref[...], m_ref[...], v_ref[...])
            for o_ref, r in zip(outs[3 * i:3 * i + 3], res):
                o_ref[...] = r

    vmem = pl.BlockSpec(memory_space=pltpu.VMEM)
    flat = [a for p in params for a in p]
    outs = _pcall(
        body, name=name,
        out_shape=tuple(jax.ShapeDtypeStruct(p[0].shape, F32) for p in params for _ in range(3)),
        in_specs=[vmem] * (4 * n), out_specs=(vmem,) * (3 * n),
    )(*flat)
    return [tuple(outs[3 * i:3 * i + 3]) for i in range(n)]


def _pair_sum(grad, recv, place, *, name):
    s_dim, r_dim, c = grad.shape
    half = r_dim // 2
    tb = half if half <= 1024 else 256
    per_half = half // tb

    def body(place_ref, a_ref, b_ref, o16_ref, o32_ref):
        s = a_ref[...] + b_ref[...]
        o16_ref[...] = s.astype(BF16)

        @pl.when(pl.program_id(1) == place_ref[1])
        def _():
            o32_ref[...] = s[0]

    grid_spec = pltpu.PrefetchScalarGridSpec(
        num_scalar_prefetch=1, grid=(per_half, s_dim),
        in_specs=[pl.BlockSpec((1, tb, c), lambda i, s, p: (s, p[0] * per_half + i, 0)),
                  pl.BlockSpec((1, tb, c), lambda i, s, p: (s, i, 0))],
        out_specs=(pl.BlockSpec((1, tb, c), lambda i, s, p: (s, i, 0)),
                   pl.BlockSpec((tb, c), lambda i, s, p: (i, 0))))
    return _pcall(
        body, name=name, grid_spec=grid_spec,
        out_shape=(jax.ShapeDtypeStruct((s_dim, half, c), BF16), jax.ShapeDtypeStruct((half, c), F32)),
        compiler_params=_params("parallel", "arbitrary"),
    )(place, grad, recv)


def _chip_sum(own, recv, place, *, name):
    r_dim, c = own.shape
    tb = r_dim if r_dim <= 1024 else 256

    def body(place_ref, a_ref, b_ref, o_ref):
        s = a_ref[...]
        for k in range(1, N_CHIPS):
            s = s + b_ref[k].astype(F32)
        o_ref[...] = s

    grid_spec = pltpu.PrefetchScalarGridSpec(
        num_scalar_prefetch=1, grid=(r_dim // tb,),
        in_specs=[pl.BlockSpec((tb, c), lambda i, p: (i, 0)),
                  pl.BlockSpec((N_CHIPS, tb, c), lambda i, p: (0, i, 0))],
        out_specs=pl.BlockSpec((None, tb, c), lambda i, p: (p[0], i, 0)))
    return _pcall(
        body, name=name, grid_spec=grid_spec, out_shape=jax.ShapeDtypeStruct((2, r_dim, c), F32),
        compiler_params=_params("parallel"),
    )(place, own, recv)


def _position():
    return lax.axis_index("x"), lax.axis_index("y"), lax.axis_index("c")


def _chip_peer(x, y, k):
    return x ^ (k >> 1), y ^ (k & 1)


_ANY = pl.BlockSpec(memory_space=pl.ANY)
_TOKEN = jax.ShapeDtypeStruct((8, 128), F32)


def _all_gather_weights(shards):
    n = len(shards)
    hops = N_CHIPS - 1

    def body(*refs):
        srcs, outs, done = refs[:n], refs[n:2 * n], refs[2 * n]
        send_sems, recv_sems = refs[2 * n + 1:]
        x, y, c = _position()
        me = 2 * x + y
        done[...] = jnp.zeros_like(done)

        def over_ici(w, k, chip, to):
            return pltpu.make_async_remote_copy(
                src_ref=srcs[w].at[c], dst_ref=outs[w].at[chip, c],
                send_sem=send_sems.at[w, k - 1], recv_sem=recv_sems.at[w, k - 1],
                device_id=to, device_id_type=MESH)

        def over_d2d(w, k, chip, half):
            return pltpu.make_async_remote_copy(
                src_ref=outs[w].at[chip, half], dst_ref=outs[w].at[chip, half],
                send_sem=send_sems.at[w, hops + k - 1], recv_sem=recv_sems.at[w, hops + k - 1],
                device_id=(x, y, 1 - c), device_id_type=MESH)

        sends = []
        for w in range(n):
            for k in range(1, N_CHIPS):
                px, py = _chip_peer(x, y, k)
                cp = over_ici(w, k, me, (px, py, c))
                cp.start()
                sends.append(cp)
        for w in range(n):
            for k in range(1, N_CHIPS):
                px, py = _chip_peer(x, y, k)
                over_ici(w, k, 2 * px + py, (px, py, c)).wait_recv()
                cp = over_d2d(w, k, 2 * px + py, c)
                cp.start()
                sends.append(cp)
        for w in range(n):
            for k in range(1, N_CHIPS):
                px, py = _chip_peer(x, y, k)
                over_d2d(w, k, 2 * px + py, 1 - c).wait_recv()
        for cp in sends:
            cp.wait_send()

    outs = _pcall(
        body, name="gather_weights",
        out_shape=tuple(jax.ShapeDtypeStruct((N_CHIPS,) + s.shape, s.dtype) for s in shards) + (_TOKEN,),
        in_specs=[_ANY] * n, out_specs=(_ANY,) * n + (pl.BlockSpec(memory_space=pltpu.VMEM),),
        scratch_shapes=[pltpu.SemaphoreType.DMA((n, 2 * hops)),
                        pltpu.SemaphoreType.DMA((n, 2 * hops))],
    )(*shards)
    return outs[:n], outs[n][0, 0]


def _pair_copies(srcs, lands, send_sems, recv_sems):
    x, y, c = _position()
    copies = []
    for w in range(len(srcs)):
        half = srcs[w].shape[1] // 2
        copies.append(pltpu.make_async_remote_copy(
            src_ref=srcs[w].at[:, pl.ds((1 - c) * half, half), :], dst_ref=lands[w],
            send_sem=send_sems.at[w], recv_sem=recv_sems.at[w],
            device_id=(x, y, 1 - c), device_id_type=MESH))
    return copies


def _chip_copies(srcs, lands, send_sems, recv_sems):
    x, y, c = _position()
    copies = []
    for w in range(len(srcs)):
        for k in range(1, N_CHIPS):
            px, py = _chip_peer(x, y, k)
            i = w * (N_CHIPS - 1) + k - 1
            copies.append(pltpu.make_async_remote_copy(
                src_ref=srcs[w].at[2 * px + py], dst_ref=lands[w].at[k],
                send_sem=send_sems.at[i], recv_sem=recv_sems.at[i],
                device_id=(px, py, c), device_id_type=MESH))
    return copies


def _gather_copies(srcs, lands, send_sems, recv_sems):
    x, y, c = _position()
    me = 2 * x + y
    copies = []
    for w in range(len(srcs)):
        for k in range(1, N_CHIPS):
            px, py = _chip_peer(x, y, k)
            i = w * (N_CHIPS - 1) + k - 1
            copies.append(pltpu.make_async_remote_copy(
                src_ref=srcs[w].at[c], dst_ref=lands[w].at[me, c],
                send_sem=send_sems.at[i], recv_sem=recv_sems.at[i],
                device_id=(px, py, c), device_id_type=MESH))
    return copies


def _exchange(name, copies_of, n_copies, srcs, land_shapes):
    n = len(srcs)

    def body(*refs):
        copies = copies_of(refs[:n], refs[n:2 * n], refs[2 * n], refs[2 * n + 1])
        for cp in copies:
            cp.start()
        for cp in copies:
            cp.wait_recv()
        for cp in copies:
            cp.wait_send()

    return _pcall(
        body, name=name, out_shape=tuple(land_shapes),
        in_specs=[_ANY] * n, out_specs=(_ANY,) * n,
        scratch_shapes=[pltpu.SemaphoreType.DMA((n_copies,)), pltpu.SemaphoreType.DMA((n_copies,))],
    )(*srcs)


_HBM = pl.BlockSpec(memory_space=pltpu.HBM)
_SEM = pl.BlockSpec(memory_space=pltpu.SEMAPHORE)
_DATAFLOW = pltpu.SideEffectType.DATAFLOW_SIDE_EFFECTING


def _exchange_start(name, copies_of, n_copies, srcs, land_shapes):
    n = len(srcs)
    lands = [lax.empty(s.shape, s.dtype) for s in land_shapes]

    def body(*refs):
        for cp in copies_of(refs[:n], refs[n:2 * n], refs[2 * n], refs[2 * n + 1]):
            cp.start()
        refs[-1][...] = jnp.zeros_like(refs[-1])

    through = [pltpu.HBM(a.shape, a.dtype) for a in list(srcs) + lands]
    outs = _pcall(
        body, name=name,
        out_shape=(pltpu.SemaphoreType.DMA((n_copies,)), pltpu.SemaphoreType.DMA((n_copies,)),
                   *through, jax.ShapeDtypeStruct((8, 128), F32)),
        in_specs=[_HBM] * (2 * n),
        out_specs=(_SEM, _SEM, *([_HBM] * (2 * n)), pl.BlockSpec(memory_space=pltpu.VMEM)),
        input_output_aliases={i: 2 + i for i in range(2 * n)},
        compiler_params=pltpu.CompilerParams(has_side_effects=_DATAFLOW),
    )(*[pltpu.with_memory_space_constraint(a, pltpu.HBM) for a in list(srcs) + lands])
    return outs[:-1], outs[-1][0, 0]


def _exchange_wait(name, copies_of, state, after):
    send_sems, recv_sems, through = state[0], state[1], state[2:]
    n = len(through) // 2
    if after.ndim == 0:
        after = jnp.broadcast_to(after, (8, 128))
    after = pltpu.with_memory_space_constraint(after, pltpu.HBM)

    def body(*refs):
        for cp in copies_of(refs[:n], refs[n:2 * n], refs[2 * n], refs[2 * n + 1]):
            cp.wait_send()
            cp.wait_recv()

    outs = _pcall(
        body, name=name,
        out_shape=tuple(pltpu.HBM(a.shape, a.dtype) for a in through),
        in_specs=[_HBM] * (2 * n) + [_SEM, _SEM, _HBM], out_specs=tuple([_HBM] * (2 * n)),
        input_output_aliases={i: i for i in range(2 * n)},
        compiler_params=pltpu.CompilerParams(has_side_effects=_DATAFLOW),
    )(*through, send_sems, recv_sems, after)
    return outs[:n], outs[n:]


def _forward_halves(lands, *, name):
    n = len(lands)
    hops = N_CHIPS - 1

    def body(*refs):
        ins, outs, done = refs[:n], refs[n:2 * n], refs[2 * n]
        send_sems, recv_sems = refs[2 * n + 1], refs[2 * n + 2]
        x, y, c = _position()
        done[...] = jnp.zeros_like(done)
        copies = []
        for w in range(n):
            for k in range(1, N_CHIPS):
                px, py = _chip_peer(x, y, k)
                i = w * hops + k - 1
                copies.append(pltpu.make_async_remote_copy(
                    src_ref=ins[w].at[2 * px + py, c], dst_ref=outs[w].at[2 * px + py, c],
                    send_sem=send_sems.at[i], recv_sem=recv_sems.at[i],
                    device_id=(x, y, 1 - c), device_id_type=MESH))
        for cp in copies:
            cp.start()
        for cp in copies:
            cp.wait_recv()
        for cp in copies:
            cp.wait_send()

    outs = _pcall(
        body, name=name,
        out_shape=tuple(jax.ShapeDtypeStruct(a.shape, a.dtype) for a in lands) + (_TOKEN,),
        in_specs=[_ANY] * n, out_specs=(_ANY,) * n + (pl.BlockSpec(memory_space=pltpu.VMEM),),
        input_output_aliases={i: i for i in range(n)},
        scratch_shapes=[pltpu.SemaphoreType.DMA((n * hops,)), pltpu.SemaphoreType.DMA((n * hops,))],
    )(*lands)
    return outs[:n], outs[n][0, 0]


def _pair_lands(grads):
    return [jax.ShapeDtypeStruct((g.shape[0], g.shape[1] // 2, g.shape[2]), F32) for g in grads]


def _same_lands(parts):
    return [jax.ShapeDtypeStruct(p.shape, p.dtype) for p in parts]


def _pair_gather_halves(halves, *, name):
    n = len(halves)

    def body(*refs):
        ins, outs = refs[:n], refs[n:2 * n]
        send_sems, recv_sems = refs[2 * n:]
        x, y, c = _position()
        sends = []
        for w in range(n):
            cp = pltpu.make_async_remote_copy(
                src_ref=ins[w].at[c], dst_ref=outs[w].at[c],
                send_sem=send_sems.at[w], recv_sem=recv_sems.at[w],
                device_id=(x, y, 1 - c), device_id_type=MESH)
            cp.start()
            sends.append(cp)
        for cp in sends:
            cp.wait_recv()
        for cp in sends:
            cp.wait_send()

    whole = _pcall(
        body, name=name,
        out_shape=tuple(jax.ShapeDtypeStruct(h.shape, F32) for h in halves),
        in_specs=[_ANY] * n, out_specs=(_ANY,) * n,
        input_output_aliases={i: i for i in range(n)},
        scratch_shapes=[pltpu.SemaphoreType.DMA((n,)), pltpu.SemaphoreType.DMA((n,))],
    )(*halves)
    return [w.reshape(2 * w.shape[1], w.shape[2]) for w in whole]


def _all_reduce_small(packed, *, name, sum_row0):
    r_dim, c = packed.shape

    def body(src_ref, out_ref, recv_ref, send_sems, recv_sems):
        x, y, c_ = _position()
        me = 4 * x + 2 * y + c_
        recv_ref[0] = src_ref[...]
        sends = []
        for k in range(1, N_DEV):
            peer = (x ^ (k >> 2), y ^ ((k >> 1) & 1), c_ ^ (k & 1))
            cp = pltpu.make_async_remote_copy(
                src_ref=src_ref, dst_ref=recv_ref.at[k],
                send_sem=send_sems.at[k - 1], recv_sem=recv_sems.at[k - 1],
                device_id=peer, device_id_type=MESH)
            cp.start()
            sends.append(cp)
        for cp in sends:
            cp.wait_recv()
        total = recv_ref[me]
        for d in range(1, N_DEV):
            total = total + recv_ref[d ^ me]
        if sum_row0:
            row0 = jnp.sum(total[0:1, :], axis=1, keepdims=True)
            rid = lax.broadcasted_iota(jnp.int32, total.shape, 0)
            total = jnp.where(rid == 0, row0, total)
        out_ref[...] = total
        for cp in sends:
            cp.wait_send()

    return _pcall(
        body, name=name, out_shape=jax.ShapeDtypeStruct((r_dim, c), F32),
        in_specs=[pl.BlockSpec(memory_space=pltpu.VMEM)],
        out_specs=pl.BlockSpec(memory_space=pltpu.VMEM),
        scratch_shapes=[pltpu.VMEM((N_DEV, r_dim, c), F32),
                        pltpu.SemaphoreType.DMA((N_DEV - 1,)), pltpu.SemaphoreType.DMA((N_DEV - 1,))],
    )(packed)


def _pad_lanes(v, width):
    return jnp.pad(v, ((0, 0), (0, width - v.shape[1])))


def _pad_rows(v, rows):
    pad = [(0, 0)] * v.ndim
    pad[-2] = (0, rows - v.shape[-2])
    return jnp.pad(v, pad)


_IN_PROJ_SHARD_ROWS = 1312


def _rows_1024(v):
    flat = v.reshape(-1)
    pad = (-flat.shape[0]) % D_MODEL
    return jnp.pad(flat, (0, pad)).reshape(-1, D_MODEL)


def _local_step(xs, target, pw, fetch, reduce_start, reduce_midway,
                conv_w, conv_b, gate_g,
                norm_mix_g, norm_mlp_g, pool_b, pool_scale, ssm_dt_bias, ssm_a_log, ssm_d, final_g):
    bias_r = _pad_lanes(ssm_dt_bias, 128)
    alog_r = _pad_lanes(ssm_a_log, 128)
    dskip_x = jnp.repeat(ssm_d, HEAD_DIM, axis=1)
    bias_c = ssm_dt_bias.reshape(N_HEADS, 1)
    alog_c = ssm_a_log.reshape(N_HEADS, 1)
    e_mat = _head_lane_matrix()
    e3_mat = jnp.tile(e_mat, (_EXACT_PIECES, 1))

    g_mix0, g_mix1 = norm_mix_g[0:1], norm_mix_g[1:2]
    g_mlp0, g_mlp1 = norm_mlp_g[0:1], norm_mlp_g[1:2]
    fg = final_g.reshape(1, D_MODEL)

    h1 = _pool_fwd(xs, g_mix0, pw, pool_b, pool_scale)
    w1_0 = fetch("mlp0_up", h1)
    u0, hm0 = _matmul(h1, w1_0, "nn", name="mlp0_up", out_dtype=BF16, b_col_shards=True, a_norm=g_mlp0)
    w2_0 = fetch("mlp0_down", u0)
    h2 = _matmul(u0, w2_0, "nn", name="mlp0_down", a_relu2=True, add=h1)

    w_z, w_xbc, w_dt = fetch("in_proj", h2)
    xbc, hn1 = _matmul(h2, w_xbc, "nt", name="in_proj_xbc", out_dtype=BF16, a_norm=g_mix1)
    z = _matmul(hn1, w_z, "nt", name="in_proj_z", out_dtype=BF16)
    dt_raw = _matmul(hn1, w_dt, "nt", name="in_proj_dt")
    dt_raw_t = dt_raw[:, :N_HEADS].T
    xc = _conv_fwd(xbc, conv_w, conv_b)
    wout, w1_1, w2_1 = fetch("rest", xc)
    dt_r, acs_r, sg_r, acs_t = _ssd_decay(dt_raw, dt_raw_t, bias_r, bias_c, alog_r, alog_c)
    y, states = _ssd_fwd(xc, dt_r, acs_r, acs_t, dskip_x, e3_mat)
    yn = _gate_fwd(y, z, gate_g)
    h3 = _matmul(yn, wout, "nn", name="out_proj", add=h2)
    u1, hm1 = _matmul(h3, w1_1, "nn", name="mlp1_up", out_dtype=BF16, b_col_shards=True, a_norm=g_mlp1)

    dh4, dh4_16, small_final = _matmul(u1, w2_1, "nn", name="mlp1_down", a_relu2=True, add=h3,
                                       loss_head=(fg, target))

    def mlp_bwd_weights(dh_out16, hm, u, w2_i, tag):
        du = _matmul(dh_out16, w2_i, "nt", name=tag + "_du", out_dtype=BF16, relu2_grad_of=u)
        dw2 = _matmul(u, dh_out16, "tn", name=tag + "_dw2", a_relu2=True)
        dw1 = _matmul(hm, du, "tn", name=tag + "_dw1", out_col_shards=N_CHIPS)
        return du, dw1, dw2.reshape(N_CHIPS, D_FF // N_CHIPS, D_MODEL)

    def mlp_bwd_input(du, dh_out, h_in, w1_i, g_i, tag):
        return _matmul(du, w1_i, "nt", name=tag + "_dhm", b_col_shards=True, norm_bwd=(h_in, g_i, dh_out))

    du1, dw1_1, dw2_1 = mlp_bwd_weights(dh4_16, hm1, u1, w2_1, "mlp1")
    dh3, dh3_16, dg_mlp1 = mlp_bwd_input(du1, dh4, h3, w1_1, g_mlp1, "mlp1")

    dyn = _matmul(dh3_16, wout, "nt", name="out_proj_dyn", out_dtype=BF16)
    dwout = _matmul(yn, dh3_16, "tn", name="out_proj_dw").reshape(N_CHIPS, D_INNER // N_CHIPS, D_MODEL)
    behind = reduce_start("mlp1_out", [dw1_1, dw2_1, dwout])
    dy, dz, dg_gate = _gate_bwd(dyn, y, z, gate_g + behind)
    behind = reduce_midway("mlp1_out", dz)
    dxc, ddt_raw, small_ssd = _ssd_bwd(xc, dt_r, acs_r, sg_r, acs_t, alog_r, dskip_x + behind,
                                       e3_mat, e_mat.T, states, dy)
    dv, dconv = _conv_bwd_act(xbc, dxc, conv_w, conv_b)
    dxbc = _conv_bwd_in(dv, conv_w)
    dw_z = _matmul(dz, hn1, "tn", name="in_proj_z_dw")
    dw_xbc = _matmul(dxbc, hn1, "tn", name="in_proj_xbc_dw")
    dw_dt = _matmul(ddt_raw, hn1, "tn", name="in_proj_dt_dw")
    dwin = jnp.concatenate([dw_z, dw_xbc, dw_dt[:N_HEADS]], axis=0)
    dwin = _pad_rows(dwin.reshape(N_CHIPS, IN_PROJ_DIM // N_CHIPS, D_MODEL), _IN_PROJ_SHARD_ROWS)
    behind = reduce_start("in_proj", [dwin])
    dh2, dh2_16, dg_mix1 = _matmul(dxbc, w_xbc, "nn", name="in_proj_dh", more=[(dz, w_z), (ddt_raw, w_dt)],
                                   norm_bwd=(h2, g_mix1 + behind, dh3))
    behind = reduce_midway("in_proj", dh2_16)

    du0, dw1_0, dw2_0 = mlp_bwd_weights(dh2_16, hm0, u0, w2_0, "mlp0")
    dh1, _, dg_mlp0 = mlp_bwd_input(du0, dh2, h1, w1_0, g_mlp0 + behind, "mlp0")
    dx, dpw, small_pool = _pool_bwd(xs, g_mix0, pw, pool_b, pool_scale, dh1)
    dpw = jnp.transpose(dpw.reshape(4, N_CHIPS, POOL_GROUP // N_CHIPS, POOL_GROUP), (1, 0, 2, 3))
    dpw = dpw.reshape(N_CHIPS, 4 * (POOL_GROUP // N_CHIPS), POOL_GROUP)

    big = [dpw, dw1_0, dw2_0]
    rows = [
        small_final[1:2],
        small_final[0:1],
        small_pool[0:1], dg_mix1[0:1],
        dg_mlp0[0:1], dg_mlp1[0:1],
        small_pool[1:2], small_pool[2:3],
        _pad_lanes(small_ssd[0:3], D_MODEL),
        _rows_1024(dg_gate[0:1]),
        _rows_1024(dconv[0:CONV_K]),
        _rows_1024(dconv[CONV_K:CONV_K + 1]),
    ]
    return dx, big, rows


def kernel(x, norm_mix_g, norm_mlp_g, pool_w, pool_b, pool_scale, ssm_w_in, ssm_conv_w, ssm_conv_b, ssm_dt_bias, ssm_a_log, ssm_d, ssm_norm_g, ssm_w_out, mlp_w1, mlp_w2, final_g, loss_target, m_norm_mix_g, m_norm_mlp_g, m_pool_w, m_pool_b, m_pool_scale, m_ssm_w_in, m_ssm_conv_w, m_ssm_conv_b, m_ssm_dt_bias, m_ssm_a_log, m_ssm_d, m_ssm_norm_g, m_ssm_w_out, m_mlp_w1, m_mlp_w2, m_final_g, v_norm_mix_g, v_norm_mlp_g, v_pool_w, v_pool_b, v_pool_scale, v_ssm_w_in, v_ssm_conv_w, v_ssm_conv_b, v_ssm_dt_bias, v_ssm_a_log, v_ssm_d, v_ssm_norm_g, v_ssm_w_out, v_mlp_w1, v_mlp_w2, v_final_g):
    xs = x[0]
    target = loss_target[0]
    my_x, my_y, my_c = _position()
    my_chip = 2 * my_x + my_y

    def halves(w):
        return w.astype(BF16).reshape((2, w.shape[0] // 2) + w.shape[1:])

    def whole(gathered, own_shard):
        g = lax.dynamic_update_index_in_dim(gathered, own_shard, my_chip, axis=0)
        return g.reshape((N_CHIPS, 2 * g.shape[2]) + g.shape[3:])

    def gather_lands(own):
        return [jax.ShapeDtypeStruct((N_CHIPS,) + s.shape, s.dtype) for s in own]

    vec_cols = CONV_DIM // N_CHIPS
    vec_own = jnp.concatenate([ssm_conv_w[0], ssm_conv_b, _pad_lanes(ssm_norm_g, vec_cols)], axis=0)
    early_own = [halves(pool_w[0]), vec_own.reshape(2, (CONV_K + 2) // 2, vec_cols)]
    early, behind_early = _all_gather_weights(early_own)
    g_pool, g_vec = [whole(g, o) for g, o in zip(early, early_own)]
    pw = jnp.transpose(g_pool, (1, 0, 2, 3)).reshape(4, POOL_GROUP, POOL_GROUP)
    conv_w = jnp.transpose(g_vec[:, 0:CONV_K, :], (1, 0, 2)).reshape(CONV_K, CONV_DIM)
    conv_b = g_vec[:, CONV_K, :].reshape(1, CONV_DIM)
    gate_g = g_vec[:, CONV_K + 1, :D_INNER // N_CHIPS].reshape(1, D_INNER)

    def behind_it(zero, ws):
        return [halves(w + zero) for w in ws]

    fetches = {}
    up_own = behind_it(behind_early, [mlp_w1[0]])
    fetches["mlp0_up"], behind_gather = _exchange_start(
        "gather_mlp0_up_start", _gather_copies, len(up_own) * (N_CHIPS - 1), up_own, gather_lands(up_own))
    down_own = behind_it(behind_gather, [mlp_w2[0]])
    fetches["mlp0_down"], behind_gather = _exchange_start(
        "gather_mlp0_down_start", _gather_copies, len(down_own) * (N_CHIPS - 1), down_own, gather_lands(down_own))
    in_own = behind_it(behind_gather, [_pad_rows(ssm_w_in[0].T, _IN_PROJ_SHARD_ROWS)])
    fetches["in_proj"], behind_gather = _exchange_start(
        "gather_in_proj_start", _gather_copies, len(in_own) * (N_CHIPS - 1), in_own, gather_lands(in_own))

    def fetch(what, after):
        if what == "mlp0_up":
            own_thru, landed = _exchange_wait("gather_mlp0_up_wait", _gather_copies, fetches[what], after)
            landed, _ = _forward_halves(landed, name="forward_mlp0_up")
            return whole(landed[0], own_thru[0])
        if what == "mlp0_down":
            own_thru, landed = _exchange_wait("gather_mlp0_down_wait", _gather_copies, fetches[what], after)
            landed, _ = _forward_halves(landed, name="forward_mlp0_down")
            return whole(landed[0], own_thru[0]).reshape(D_FF, D_MODEL)
        if what == "in_proj":
            own_thru, landed = _exchange_wait("gather_in_proj_wait", _gather_copies, fetches["in_proj"], after)
            landed, behind = _forward_halves(landed, name="forward_in_proj")
            rest = behind_it(behind, [ssm_w_out[0], mlp_w1[1], mlp_w2[1]])
            fetches["rest"], behind = _exchange_start(
                "gather_rest_start", _gather_copies, len(rest) * (N_CHIPS - 1), rest, gather_lands(rest))
            win = whole(landed[0], own_thru[0])[:, :IN_PROJ_DIM // N_CHIPS].reshape(IN_PROJ_DIM, D_MODEL)
            w_dt = _pad_rows(win[D_INNER + CONV_DIM:], 128) + behind.astype(BF16)
            return win[:D_INNER], win[D_INNER:D_INNER + CONV_DIM], w_dt
        own_thru, landed = _exchange_wait("gather_rest_wait", _gather_copies, fetches["rest"], after)
        landed, _ = _forward_halves(landed, name="forward_rest")
        g_wout, w1_1, g_w2_1 = [whole(g, o) for g, o in zip(landed, own_thru)]
        return g_wout.reshape(D_INNER, D_MODEL), w1_1, g_w2_1.reshape(D_FF, D_MODEL)

    place = jnp.stack([my_c, my_chip]).astype(jnp.int32)
    waves = {}

    def reduce_start(wave, grads):
        waves[wave] = {}
        waves[wave]["pair"], behind = _exchange_start(
            "pair_%s_start" % wave, _pair_copies, len(grads), grads, _pair_lands(grads))
        return behind

    def reduce_midway(wave, after):
        st = waves[wave]
        grads, recv = _exchange_wait("pair_%s_wait" % wave, _pair_copies, st["pair"], after)
        sums = [_pair_sum(g, r, place, name="pair_sum_%s_%d" % (wave, i))
                for i, (g, r) in enumerate(zip(grads, recv))]
        st["f32"] = [s32 for _, s32 in sums]
        b16 = [s16 for s16, _ in sums]
        st["chip"], behind = _exchange_start(
            "chip_%s_start" % wave, _chip_copies, len(b16) * (N_CHIPS - 1), b16, _same_lands(b16))
        return behind

    def reduce_finish(wave, after):
        st = waves[wave]
        _, got = _exchange_wait("chip_%s_wait" % wave, _chip_copies, st["chip"], after)
        return [_chip_sum(s32, r, place, name="chip_sum_%s_%d" % (wave, i))
                for i, (s32, r) in enumerate(zip(st["f32"], got))]

    dx, big0, rows = _local_step(xs, target, pw, fetch, reduce_start, reduce_midway,
                                 conv_w, conv_b, gate_g,
                                 norm_mix_g + behind_gather, norm_mlp_g, pool_b, pool_scale,
                                 ssm_dt_bias, ssm_a_log, ssm_d, final_g)

    behind = reduce_start("layer0", big0)
    small = jnp.concatenate(rows, axis=0)
    small = jnp.pad(small, ((0, (-small.shape[0]) % 8), (0, 0))) + behind
    small = _all_reduce_small(small, name="all_reduce_small", sum_row0=True)
    behind = reduce_midway("layer0", small)
    h_w1_1, h_w2_1, h_wout = reduce_finish("mlp1_out", behind)
    (h_win,) = reduce_finish("in_proj", behind)
    g_w1_1, g_w2_1, g_wout_s, g_win_s = _pair_gather_halves([h_w1_1, h_w2_1, h_wout, h_win],
                                                            name="pair_gather_layer1")
    loss = small[0, 0]
    g_final = small[1]
    g_norm_mix = small[2:4]
    g_norm_mlp = small[4:6]
    g_pool_b, g_pool_scale = small[6:7], small[7:8]
    g_alog, g_dtb, g_dsk = small[8:9, :N_HEADS], small[9:10, :N_HEADS], small[10:11, :N_HEADS]
    g_gate_full = small[11:13].reshape(1, D_INNER)
    g_convw_full = small[13:25].reshape(CONV_K, CONV_DIM)
    g_convb_full = small[25:28].reshape(1, CONV_DIM)
    g_gate = lax.dynamic_slice_in_dim(g_gate_full, my_chip * (D_INNER // N_CHIPS), D_INNER // N_CHIPS, axis=1)
    g_convw = lax.dynamic_slice_in_dim(g_convw_full, my_chip * (CONV_DIM // N_CHIPS), CONV_DIM // N_CHIPS, axis=1)
    g_convb = lax.dynamic_slice_in_dim(g_convb_full, my_chip * (CONV_DIM // N_CHIPS), CONV_DIM // N_CHIPS, axis=1)

    grads = {
        "norm_mix_g": g_norm_mix, "norm_mlp_g": g_norm_mlp,
        "pool_b": g_pool_b, "pool_scale": g_pool_scale,
        "ssm_conv_w": g_convw.reshape(ssm_conv_w.shape),
        "ssm_conv_b": g_convb, "ssm_dt_bias": g_dtb, "ssm_a_log": g_alog, "ssm_d": g_dsk,
        "ssm_norm_g": g_gate, "ssm_w_out": g_wout_s.reshape(ssm_w_out.shape),
        "final_g": g_final,
    }
    weights = dict(norm_mix_g=norm_mix_g, norm_mlp_g=norm_mlp_g, pool_w=pool_w, pool_b=pool_b,
                   pool_scale=pool_scale, ssm_w_in=ssm_w_in, ssm_conv_w=ssm_conv_w, ssm_conv_b=ssm_conv_b,
                   ssm_dt_bias=ssm_dt_bias, ssm_a_log=ssm_a_log, ssm_d=ssm_d, ssm_norm_g=ssm_norm_g,
                   ssm_w_out=ssm_w_out, mlp_w1=mlp_w1, mlp_w2=mlp_w2, final_g=final_g)
    moms = dict(norm_mix_g=(m_norm_mix_g, v_norm_mix_g), norm_mlp_g=(m_norm_mlp_g, v_norm_mlp_g),
                pool_w=(m_pool_w, v_pool_w), pool_b=(m_pool_b, v_pool_b),
                pool_scale=(m_pool_scale, v_pool_scale), ssm_w_in=(m_ssm_w_in, v_ssm_w_in),
                ssm_conv_w=(m_ssm_conv_w, v_ssm_conv_w), ssm_conv_b=(m_ssm_conv_b, v_ssm_conv_b),
                ssm_dt_bias=(m_ssm_dt_bias, v_ssm_dt_bias), ssm_a_log=(m_ssm_a_log, v_ssm_a_log),
                ssm_d=(m_ssm_d, v_ssm_d), ssm_norm_g=(m_ssm_norm_g, v_ssm_norm_g),
                ssm_w_out=(m_ssm_w_out, v_ssm_w_out), mlp_w1=(m_mlp_w1, v_mlp_w1),
                mlp_w2=(m_mlp_w2, v_mlp_w2), final_g=(m_final_g, v_final_g))
    names = list(weights)
    big_names = ("pool_w", "ssm_w_in", "ssm_w_out", "mlp_w1", "mlp_w2")
    deltas, new_m, new_v = {}, {}, {}

    def as_rows(nm, a):
        return a[0].T if nm == "ssm_w_in" else a.reshape(-1, a.shape[-1])

    def from_rows(nm, r):
        return r.T[None] if nm == "ssm_w_in" else r.reshape(weights[nm].shape)

    def update(nm, grad_rows, layer=None, into=None):
        return _adamw(as_rows(nm, weights[nm]), grad_rows, as_rows(nm, moms[nm][0]), as_rows(nm, moms[nm][1]),
                      name="adamw_%s_%s" % (nm, layer), part=layer, into=into)

    def keep(nm, results):
        deltas[nm], new_m[nm], new_v[nm], grads[nm] = [from_rows(nm, r) for r in results]

    keep("ssm_w_in", update("ssm_w_in", g_win_s[:IN_PROJ_DIM // N_CHIPS]))
    keep("ssm_w_out", update("ssm_w_out", g_wout_s))
    w1_done = update("mlp_w1", g_w1_1, layer=1)
    w2_done = update("mlp_w2", g_w2_1, layer=1)
    small_names = [nm for nm in names if nm not in big_names]
    small_done = _adamw_small(
        [tuple(as_rows(nm, a) for a in (weights[nm], grads[nm], moms[nm][0], moms[nm][1])) for nm in small_names],
        name="adamw_small")
    for nm, (d_, m_, v_) in zip(small_names, small_done):
        deltas[nm], new_m[nm], new_v[nm] = [from_rows(nm, r) for r in (d_, m_, v_)]

    above = (deltas["ssm_w_in"][0, 0, 0] + deltas["ssm_w_out"][0, 0, 0] + w1_done[0][-1, -1]
             + w2_done[0][-1, -1] + small_done[0][0][0, 0])
    g_pool_w, g_w1_0, g_w2_0 = _pair_gather_halves(reduce_finish("layer0", above), name="pair_gather_layer0")
    keep("mlp_w1", update("mlp_w1", g_w1_0, layer=0, into=w1_done))
    keep("mlp_w2", update("mlp_w2", g_w2_0, layer=0, into=w2_done))
    keep("pool_w", update("pool_w", g_pool_w))

    grad_x = dx.reshape(x.shape)
    out_grads = [grads[nm].reshape(weights[nm].shape) for nm in names]
    return (loss, grad_x, *out_grads, *[deltas[nm] for nm in names],
            *[new_m[nm] for nm in names], *[new_v[nm] for nm in names])
```

```python
import jax
import jax.numpy as jnp
from jax import lax
from jax.experimental import pallas as pl
from jax.experimental.pallas import tpu as pltpu

F32 = jnp.float32
BF16 = jnp.bfloat16
MESH = pl.DeviceIdType.MESH

D_MODEL = 1024
RMS_EPS = 1e-5
POOL_WINDOWS = (2, 4, 8, 16)
POOL_GROUP = 256
POOL_HALO = 16
D_INNER = 2048
HEAD_DIM = 64
N_HEADS = 32
N_GROUPS = 4
HEADS_PER_GROUP = 8
D_STATE = 128
CONV_K = 4
CONV_HALO = 8
CHUNK = 128
CONV_DIM = 3072
IN_PROJ_DIM = 5152
D_FF = 4096
N_CHIPS = 4
N_DEV = 8

ADAM_LR = 0.001
ADAM_B1 = 0.9
ADAM_B2 = 0.999
ADAM_EPS = 1e-08
ADAM_WD = 0.01
ADAM_STEP = 10

VMEM_LIMIT = 56 * 1024 * 1024
NEG_INF = float("-inf")


def _pcall(body, **kw):
    return pl.pallas_call(body, **kw)


def _params(*sem):
    return pltpu.CompilerParams(dimension_semantics=sem, vmem_limit_bytes=VMEM_LIMIT)


def _sigmoid(v):
    return 1.0 / (1.0 + jnp.exp(-v))


def _row_spec(tb, d, nb=None, reverse=False):
    if reverse:
        return pl.BlockSpec((tb, d), lambda i: (nb - 1 - i, 0))
    return pl.BlockSpec((tb, d), lambda i: (i, 0))


def _const_spec(shape):
    return pl.BlockSpec(shape, lambda *_: tuple(0 for _ in shape))


_DIMS = {"nn": (((1,), (0,)), ((), ())),
         "nt": (((1,), (1,)), ((), ())),
         "tn": (((0,), (0,)), ((), ()))}


_MATMUL_VMEM_BUDGET = 40 * 1024 * 1024


def _matmul_tiles(m_dim, n_dim, k_dim, a_bytes, b_bytes, mn_bytes):
    tm, tn = min(m_dim, 2048), min(n_dim, 1024)
    while m_dim % tm:
        tm //= 2
    while 2 * (tm * k_dim * a_bytes + tn * k_dim * b_bytes + tm * tn * mn_bytes) > _MATMUL_VMEM_BUDGET:
        if tm >= tn:
            tm //= 2
        else:
            tn //= 2
    return tm, tn


def _matmul(a, b, mode, *, name, out_dtype=F32, a_relu2=False, a_norm=None, add=None, relu2_grad_of=None,
            out_col_shards=1, b_col_shards=False, norm_bwd=None, loss_head=None, more=()):
    if mode == "tn":
        k_dim, m_dim = a.shape
    else:
        m_dim, k_dim = a.shape
    if b_col_shards:
        n_shards, shard_cols = b.shape[0], b.shape[2]
        n_dim = n_shards * shard_cols if mode == "nn" else b.shape[1]
    else:
        n_dim = b.shape[0] if mode == "nt" else b.shape[1]
    mn_bytes = jnp.dtype(out_dtype).itemsize
    if relu2_grad_of is not None:
        mn_bytes += relu2_grad_of.dtype.itemsize
    if add is not None:
        mn_bytes += add.dtype.itemsize
    row_epilogue = norm_bwd is not None or loss_head is not None
    if row_epilogue:
        assert out_dtype == F32 and out_col_shards == 1 and (norm_bwd is None or loss_head is None)
        mn_bytes += 4 + 4 + 2 + 8
    a_bytes = a.dtype.itemsize + (2 if a_norm is not None else 0)
    k_all = k_dim + sum(a_i.shape[1] for a_i, _ in more)
    tm, tn = _matmul_tiles(m_dim, n_dim, k_all, a_bytes, b.dtype.itemsize, mn_bytes)
    if row_epilogue:
        while tn < n_dim:
            tm, tn = tm // 2, tn * 2
    assert m_dim % tm == 0 and n_dim % tn == 0
    a_spec = (pl.BlockSpec((k_dim, tm), lambda i, j: (0, i)) if mode == "tn"
              else pl.BlockSpec((tm, k_dim), lambda i, j: (i, 0)))
    if b_col_shards and mode == "nn":
        assert shard_cols % tn == 0
        per_shard = shard_cols // tn
        b_spec = pl.BlockSpec((None, k_dim, tn), lambda i, j: (j // per_shard, 0, j % per_shard))
    elif b_col_shards:
        assert mode == "nt" and k_dim == n_shards * shard_cols
        b_spec = pl.BlockSpec((n_shards, tn, shard_cols), lambda i, j: (0, j, 0))
    else:
        b_spec = (pl.BlockSpec((tn, k_dim), lambda i, j: (j, 0)) if mode == "nt"
                  else pl.BlockSpec((k_dim, tn), lambda i, j: (0, j)))
    mn_spec = pl.BlockSpec((tm, tn), lambda i, j: (i, j))
    operands, in_specs = [a, b], [a_spec, b_spec]
    for a_i, b_i in more:
        assert mode == "nn" and a_i.shape[0] == m_dim and b_i.shape == (a_i.shape[1], n_dim)
        operands += [a_i, b_i]
        in_specs += [pl.BlockSpec((tm, a_i.shape[1]), lambda i, j: (i, 0)),
                     pl.BlockSpec((a_i.shape[1], tn), lambda i, j: (0, j))]
    if relu2_grad_of is not None:
        operands.append(relu2_grad_of)
        in_specs.append(mn_spec)
    if add is not None:
        operands.append(add)
        in_specs.append(mn_spec)
    gain_spec = pl.BlockSpec((1, n_dim), lambda i, j: (0, 0))
    if a_norm is not None:
        assert mode in ("nn", "nt") and a.dtype == F32 and not row_epilogue and out_col_shards == 1
        operands.append(a_norm)
        in_specs.append(pl.BlockSpec((1, k_dim), lambda i, j: (0, 0)))
    if norm_bwd is not None:
        h_in, g_in, dres_in = norm_bwd
        operands += [h_in, g_in, dres_in]
        in_specs += [mn_spec, gain_spec, mn_spec]
    if loss_head is not None:
        operands += list(loss_head)
        in_specs += [gain_spec, mn_spec]
    if row_epilogue:
        out_shape = (jax.ShapeDtypeStruct((m_dim, n_dim), F32), jax.ShapeDtypeStruct((m_dim, n_dim), BF16),
                     jax.ShapeDtypeStruct((8, n_dim), F32))
        out_spec = (mn_spec, mn_spec, pl.BlockSpec((8, n_dim), lambda i, j: (0, 0)))
    elif out_col_shards == 1:
        out_shape = jax.ShapeDtypeStruct((m_dim, n_dim), out_dtype)
        out_spec = mn_spec
    else:
        n_shard = n_dim // out_col_shards
        assert n_shard % tn == 0
        per = n_shard // tn
        out_shape = jax.ShapeDtypeStruct((out_col_shards, m_dim, n_shard), out_dtype)
        out_spec = pl.BlockSpec((None, tm, tn), lambda i, j: (j // per, i, j % per))
    if a_norm is not None:
        out_shape = (out_shape, jax.ShapeDtypeStruct((m_dim, k_dim), BF16))
        out_spec = (out_spec, pl.BlockSpec((tm, k_dim), lambda i, j: (i, 0)))

    n_in = len(operands)

    def body(*refs):
        a_ref, b_ref, o_ref = refs[0], refs[1], refs[n_in]
        if a_norm is not None:
            normed_ref = refs[n_in + 1]

            @pl.when(pl.program_id(1) == 0)
            def _():
                xa = a_ref[...]
                normed_ref[...] = (xa * _rms(xa) * refs[n_in - 1][...]).astype(BF16)

            av = normed_ref[...]
        else:
            av = a_ref[...]
        if a_relu2:
            av = jnp.maximum(av, 0)
            av = av * av
        if b_col_shards and mode == "nt":
            r = None
            for s in range(n_shards):
                part = lax.dot_general(av[:, s * shard_cols:(s + 1) * shard_cols].astype(BF16),
                                       b_ref[s].astype(BF16), _DIMS[mode], preferred_element_type=F32)
                r = part if r is None else r + part
        else:
            r = lax.dot_general(av.astype(BF16), b_ref[...].astype(BF16), _DIMS[mode],
                                preferred_element_type=F32)
        nxt = 2
        for _ in more:
            r = r + jnp.dot(refs[nxt][...].astype(BF16), refs[nxt + 1][...].astype(BF16),
                            preferred_element_type=F32)
            nxt += 2
        if relu2_grad_of is not None:
            r = r * (2.0 * jnp.maximum(refs[nxt][...].astype(F32), 0.0))
            nxt += 1
        if add is not None:
            r = r + refs[nxt][...]
            nxt += 1
        if not row_epilogue:
            o_ref[...] = r.astype(out_dtype)
            return
        dh16_ref, small_ref = refs[n_in + 1], refs[n_in + 2]

        @pl.when(pl.program_id(0) == 0)
        def _():
            small_ref[...] = jnp.zeros_like(small_ref)

        if norm_bwd is not None:
            h_ref, g_ref, dres_ref = refs[nxt:nxt + 3]
            x, dy = h_ref[...], r
        else:
            g_ref, t_ref = refs[nxt:nxt + 2]
            x = r
        rr = _rms(x)
        xhat = x * rr
        gv = g_ref[...]
        if loss_head is not None:
            err = xhat * gv - t_ref[...]
            small_ref[1:2, :] += (0.5 / n_dim) * jnp.sum(err * err, axis=0, keepdims=True)
            dy = err * (1.0 / n_dim)
        dxhat = dy * gv
        dh = rr * (dxhat - xhat * jnp.mean(dxhat * xhat, axis=-1, keepdims=True))
        if norm_bwd is not None:
            dh = dres_ref[...] + dh
        o_ref[...] = dh
        dh16_ref[...] = dh.astype(BF16)
        small_ref[0:1, :] += jnp.sum(dy * xhat, axis=0, keepdims=True)

    if row_epilogue:
        semantics = ("arbitrary", "arbitrary")
    elif a_norm is not None:
        semantics = ("parallel", "arbitrary")
    else:
        semantics = ("parallel", "parallel")
    return _pcall(
        body, name=name, out_shape=out_shape,
        grid=(m_dim // tm, n_dim // tn),
        in_specs=in_specs, out_specs=out_spec,
        compiler_params=_params(*semantics),
    )(*operands)


def _rms(x):
    return lax.rsqrt(jnp.mean(x * x, axis=-1, keepdims=True) + RMS_EPS)


def _pool_mixed(ext, hn, t0, tb):
    t = t0 + lax.broadcasted_iota(jnp.int32, (tb, 1), 0)
    parts = []
    for gi, w in enumerate(POOL_WINDOWS):
        lanes = slice(gi * POOL_GROUP, (gi + 1) * POOL_GROUP)
        s = ext[:, lanes]
        k = 1
        while k < w:
            s = s + pltpu.roll(s, k, 0)
            k *= 2
        cnt = jnp.minimum(t + 1, w).astype(F32)
        parts.append(s[POOL_HALO:, :] / cnt - hn[:, lanes])
    return parts


def _pool_fwd(x, g, pw, pb, ps, *, tb=512):
    t_dim, d = x.shape

    def body(x_ref, g_ref, pw_ref, pb_ref, ps_ref, o_ref, ext_ref):
        i = pl.program_id(0)

        @pl.when(i == 0)
        def _():
            ext_ref[0:POOL_HALO, :] = jnp.zeros((POOL_HALO, d), F32)

        xv = x_ref[...]
        hn = xv * _rms(xv) * g_ref[...]
        ext_ref[POOL_HALO:, :] = hn
        mixed = _pool_mixed(ext_ref[...], hn, i * tb, tb)
        for gi in range(len(POOL_WINDOWS)):
            lanes = slice(gi * POOL_GROUP, (gi + 1) * POOL_GROUP)
            out = jnp.dot(mixed[gi].astype(BF16), pw_ref[gi], preferred_element_type=F32)
            o_ref[:, lanes] = xv[:, lanes] + (out + pb_ref[:, lanes]) * ps_ref[:, lanes]
        ext_ref[0:POOL_HALO, :] = hn[tb - POOL_HALO:, :]

    return _pcall(
        body, name="pool_fwd", out_shape=jax.ShapeDtypeStruct((t_dim, d), F32),
        grid=(t_dim // tb,),
        in_specs=[_row_spec(tb, d), _const_spec((1, d)), _const_spec((4, POOL_GROUP, POOL_GROUP)),
                  _const_spec((1, d)), _const_spec((1, d))],
        out_specs=_row_spec(tb, d),
        scratch_shapes=[pltpu.VMEM((POOL_HALO + tb, d), F32)],
        compiler_params=_params("arbitrary"),
    )(x, g, pw, pb, ps)


def _pool_bwd(x, g, pw, pb, ps, dh1, *, tb=512):
    t_dim, d = x.shape
    nb = t_dim // tb
    halo_per_block = tb // POOL_HALO

    def body(x_ref, xprev_ref, g_ref, pw_ref, pb_ref, ps_ref, dh1_ref,
             dx_ref, dpw_ref, small_ref, ext_ref, dext_ref):
        i = pl.program_id(0)
        blk = nb - 1 - i

        @pl.when(i == 0)
        def _():
            dpw_ref[...] = jnp.zeros_like(dpw_ref)
            small_ref[...] = jnp.zeros_like(small_ref)
            dext_ref[tb:, :] = jnp.zeros((POOL_HALO, d), F32)

        gv = g_ref[...]
        xv = x_ref[...]
        r = _rms(xv)
        xhat = xv * r
        hn = xhat * gv
        xp = xprev_ref[...]
        hprev = xp * _rms(xp) * gv * (blk > 0).astype(F32)
        ext_ref[0:POOL_HALO, :] = hprev
        ext_ref[POOL_HALO:, :] = hn
        mixed = _pool_mixed(ext_ref[...], hn, blk * tb, tb)

        dout = dh1_ref[...]
        t = blk * tb + lax.broadcasted_iota(jnp.int32, (tb, 1), 0)
        for gi, w in enumerate(POOL_WINDOWS):
            lanes = slice(gi * POOL_GROUP, (gi + 1) * POOL_GROUP)
            mb = mixed[gi].astype(BF16)
            pre = jnp.dot(mb, pw_ref[gi], preferred_element_type=F32) + pb_ref[:, lanes]
            dg_out = dout[:, lanes]
            small_ref[2:3, lanes] += jnp.sum(dg_out * pre, axis=0, keepdims=True)
            dpre = dg_out * ps_ref[:, lanes]
            small_ref[1:2, lanes] += jnp.sum(dpre, axis=0, keepdims=True)
            dpb16 = dpre.astype(BF16)
            dpw_ref[gi] += lax.dot_general(mb, dpb16, _DIMS["tn"], preferred_element_type=F32)
            dmixed = lax.dot_general(dpb16, pw_ref[gi], _DIMS["nt"], preferred_element_type=F32)
            cnt = jnp.minimum(t + 1, w).astype(F32)
            dq = dmixed / cnt
            dext_ref[0:tb, lanes] = dq
            s = dext_ref[:, lanes]
            k = 1
            while k < w:
                s = s + pltpu.roll(s, tb + POOL_HALO - k, 0)
                k *= 2
            dhn = s[0:tb, :] - dmixed
            dext_ref[tb:, lanes] = dq[0:POOL_HALO, :]
            small_ref[0:1, lanes] += jnp.sum(dhn * xhat[:, lanes], axis=0, keepdims=True)
            ext_ref[POOL_HALO:, lanes] = dhn * gv[:, lanes]
        dxhat = ext_ref[POOL_HALO:, :]
        dx_ref[...] = dout + r * (dxhat - xhat * jnp.mean(dxhat * xhat, axis=-1, keepdims=True))

    return _pcall(
        body, name="pool_bwd",
        out_shape=(jax.ShapeDtypeStruct((t_dim, d), F32),
                   jax.ShapeDtypeStruct((4, POOL_GROUP, POOL_GROUP), F32),
                   jax.ShapeDtypeStruct((8, d), F32)),
        grid=(nb,),
        in_specs=[_row_spec(tb, d, nb, True),
                  pl.BlockSpec((POOL_HALO, d),
                               lambda i: (jnp.maximum((nb - 1 - i) * halo_per_block - 1, 0), 0)),
                  _const_spec((1, d)), _const_spec((4, POOL_GROUP, POOL_GROUP)),
                  _const_spec((1, d)), _const_spec((1, d)), _row_spec(tb, d, nb, True)],
        out_specs=(_row_spec(tb, d, nb, True), _const_spec((4, POOL_GROUP, POOL_GROUP)),
                   _const_spec((8, d))),
        scratch_shapes=[pltpu.VMEM((POOL_HALO + tb, d), F32), pltpu.VMEM((tb + POOL_HALO, d), F32)],
        compiler_params=_params("arbitrary"),
    )(x, x, g, pw, pb, ps, dh1)


_CONV_CB = 1024
_STRIP = 16


def _strips(tb, fn, unroll=4):
    def step(i, carry):
        fn(pl.multiple_of(i * _STRIP, _STRIP))
        return carry
    lax.fori_loop(0, tb // _STRIP, step, 0, unroll=unroll)


def _conv_taps(ext_ref, r0, w):
    shifted = [ext_ref[CONV_HALO + r0 - sh:CONV_HALO + r0 - sh + _STRIP, :] for sh in range(CONV_K)]
    acc = shifted[0] * w[CONV_K - 1:CONV_K, :]
    for sh in range(1, CONV_K):
        acc = acc + shifted[sh] * w[CONV_K - 1 - sh:CONV_K - sh, :]
    return shifted, acc


def _conv_fwd(u, w, b, *, tb=1024):
    t_dim, c = u.shape
    cb = _CONV_CB

    def body(u_ref, w_ref, b_ref, o_ref, ext_ref):
        @pl.when(pl.program_id(1) == 0)
        def _():
            ext_ref[0:CONV_HALO, :] = jnp.zeros((CONV_HALO, cb), F32)

        wv = w_ref[...]
        bv = b_ref[...]

        def fill(r0):
            ext_ref[pl.ds(CONV_HALO + r0, _STRIP), :] = u_ref[pl.ds(r0, _STRIP), :].astype(F32)

        _strips(tb, fill)
        for r0 in range(0, tb, _STRIP):
            v = _conv_taps(ext_ref, r0, wv)[1] + bv
            o_ref[r0:r0 + _STRIP, :] = (v * _sigmoid(v)).astype(BF16)
        ext_ref[0:CONV_HALO, :] = ext_ref[tb:tb + CONV_HALO, :]

    blk = pl.BlockSpec((tb, cb), lambda j, t: (t, j))
    return _pcall(
        body, name="conv_fwd", out_shape=jax.ShapeDtypeStruct((t_dim, c), BF16),
        grid=(c // cb, t_dim // tb),
        in_specs=[blk, pl.BlockSpec((CONV_K, cb), lambda j, t: (0, j)),
                  pl.BlockSpec((1, cb), lambda j, t: (0, j))],
        out_specs=blk,
        scratch_shapes=[pltpu.VMEM((CONV_HALO + tb, cb), F32)],
        compiler_params=_params("parallel", "arbitrary"),
    )(u, w, b)


def _conv_bwd_act(u, dxc, w, b, *, tb=1024):
    t_dim, c = u.shape
    cb = _CONV_CB
    half = _STRIP // 2

    def body(u_ref, d_ref, w_ref, b_ref, dv_ref, dwb_ref, ext_ref, acc_ref):
        @pl.when(pl.program_id(1) == 0)
        def _():
            ext_ref[0:CONV_HALO, :] = jnp.zeros((CONV_HALO, cb), F32)
            dwb_ref[...] = jnp.zeros_like(dwb_ref)

        acc_ref[...] = jnp.zeros_like(acc_ref)
        wv = w_ref[...]
        bv = b_ref[...]

        def fill(r0):
            ext_ref[pl.ds(CONV_HALO + r0, _STRIP), :] = u_ref[pl.ds(r0, _STRIP), :].astype(F32)

        _strips(tb, fill)
        for r0 in range(0, tb, _STRIP):
            shifted, v = _conv_taps(ext_ref, r0, wv)
            v = v + bv
            sg = _sigmoid(v)
            dv = d_ref[r0:r0 + _STRIP, :].astype(F32) * (sg * (1.0 + v * (1.0 - sg)))
            dv_ref[r0:r0 + _STRIP, :] = dv.astype(BF16)
            acc_ref[CONV_K] += dv[0:half, :] + dv[half:, :]
            for sh in range(CONV_K):
                p = dv * shifted[sh]
                acc_ref[CONV_K - 1 - sh] += p[0:half, :] + p[half:, :]
        for k in range(CONV_K + 1):
            dwb_ref[k:k + 1, :] += jnp.sum(acc_ref[k], axis=0, keepdims=True)
        ext_ref[0:CONV_HALO, :] = ext_ref[tb:tb + CONV_HALO, :]

    blk = pl.BlockSpec((tb, cb), lambda j, t: (t, j))
    return _pcall(
        body, name="conv_bwd_act",
        out_shape=(jax.ShapeDtypeStruct((t_dim, c), BF16), jax.ShapeDtypeStruct((8, c), F32)),
        grid=(c // cb, t_dim // tb),
        in_specs=[blk, blk, pl.BlockSpec((CONV_K, cb), lambda j, t: (0, j)),
                  pl.BlockSpec((1, cb), lambda j, t: (0, j))],
        out_specs=(blk, pl.BlockSpec((8, cb), lambda j, t: (0, j))),
        scratch_shapes=[pltpu.VMEM((CONV_HALO + tb, cb), F32), pltpu.VMEM((CONV_K + 1, half, cb), F32)],
        compiler_params=_params("parallel", "arbitrary"),
    )(u, dxc, w, b)


def _conv_bwd_in(dv, w, *, tb=1024):
    t_dim, c = dv.shape
    cb = _CONV_CB
    nb = t_dim // tb

    def body(dv_ref, w_ref, du_ref, ext_ref):
        @pl.when(pl.program_id(1) == 0)
        def _():
            ext_ref[tb:, :] = jnp.zeros((CONV_HALO, cb), F32)

        wv = w_ref[...]

        def fill(r0):
            ext_ref[pl.ds(r0, _STRIP), :] = dv_ref[pl.ds(r0, _STRIP), :].astype(F32)

        _strips(tb, fill)
        for r0 in range(0, tb, _STRIP):
            acc = ext_ref[r0:r0 + _STRIP, :] * wv[CONV_K - 1:CONV_K, :]
            for sh in range(1, CONV_K):
                acc = acc + ext_ref[r0 + sh:r0 + sh + _STRIP, :] * wv[CONV_K - 1 - sh:CONV_K - sh, :]
            du_ref[r0:r0 + _STRIP, :] = acc.astype(BF16)
        ext_ref[tb:, :] = ext_ref[0:CONV_HALO, :]

    blk = pl.BlockSpec((tb, cb), lambda j, t: (nb - 1 - t, j))
    return _pcall(
        body, name="conv_bwd_in", out_shape=jax.ShapeDtypeStruct((t_dim, c), BF16),
        grid=(c // cb, nb),
        in_specs=[blk, pl.BlockSpec((CONV_K, cb), lambda j, t: (0, j))],
        out_specs=blk,
        scratch_shapes=[pltpu.VMEM((tb + CONV_HALO, cb), F32)],
        compiler_params=_params("parallel", "arbitrary"),
    )(dv, w)


def _softplus(v):
    e = jnp.exp(-jnp.abs(v))
    w = 1.0 + e
    log1p = jnp.where(w == 1.0, e, jnp.log(w) * e / jnp.where(w == 1.0, 1.0, w - 1.0))
    return jnp.maximum(v, 0.0) + log1p


def _cumsum_rows(v):
    row = lax.broadcasted_iota(jnp.int32, v.shape, 0) & (CHUNK - 1)
    k = 1
    while k < CHUNK:
        v = v + jnp.where(row >= k, pltpu.roll(v, k, 0), 0.0)
        k *= 2
    return v


def _cumsum_lanes(v):
    col = lax.broadcasted_iota(jnp.int32, v.shape, 1) & (CHUNK - 1)
    k = 1
    while k < CHUNK:
        v = v + jnp.where(col >= k, pltpu.roll(v, k, 1), 0.0)
        k *= 2
    return v


def _rev_cumsum_rows(v):
    row = lax.broadcasted_iota(jnp.int32, v.shape, 0)
    k = 1
    while k < CHUNK:
        v = v + jnp.where(row < CHUNK - k, pltpu.roll(v, CHUNK - k, 0), 0.0)
        k *= 2
    return v


PAIR = 2 * HEAD_DIM
GROUP_LANES = HEADS_PER_GROUP * HEAD_DIM


def _head_lane_matrix():
    h = lax.broadcasted_iota(jnp.int32, (128, D_INNER), 0)
    j = lax.broadcasted_iota(jnp.int32, (128, D_INNER), 1)
    return (j // HEAD_DIM == h).astype(BF16)


def _split_bf16(v, pieces):
    out = []
    for _ in range(pieces):
        p = v.astype(BF16)
        out.append(p)
        v = v - p.astype(F32)
    return out


_EXACT_PIECES = 3


def _expand_heads(values, e3):
    lhs = jnp.concatenate([jnp.concatenate(_split_bf16(v, _EXACT_PIECES), axis=1) for v in values], axis=0)
    out = jnp.dot(lhs, e3, preferred_element_type=F32)
    rows = values[0].shape[0]
    return [out[i * rows:(i + 1) * rows, :] for i in range(len(values))]


def _reduce_heads(v, et, pieces):
    return sum(jnp.dot(p, et, preferred_element_type=F32) for p in _split_bf16(v, pieces))


def _ssd_decay(dt_raw, dt_raw_t, bias_r, bias_c, alog_r, alog_c, *, tb=1024):
    t_dim = dt_raw.shape[0]
    tb = min(tb, t_dim)
    assert t_dim % tb == 0 and tb % CHUNK == 0

    def body(dtr_ref, dtt_ref, br_ref, bc_ref, ar_ref, ac_ref, dt_ref, acs_ref, sg_ref, acst_ref):
        pre = dtr_ref[...] + br_ref[...]
        dt = _softplus(pre)
        dt_ref[...] = dt
        sg_ref[...] = _sigmoid(pre)
        acs_ref[...] = _cumsum_rows(dt * (-jnp.exp(ar_ref[...])))
        acst_ref[...] = _cumsum_lanes(_softplus(dtt_ref[...] + bc_ref[...]) * (-jnp.exp(ac_ref[...])))

    rows = pl.BlockSpec((tb, 128), lambda i: (i, 0))
    cols = pl.BlockSpec((N_HEADS, tb), lambda i: (0, i))
    sds = jax.ShapeDtypeStruct((t_dim, 128), F32)
    return _pcall(
        body, name="ssd_decay",
        out_shape=(sds, sds, sds, jax.ShapeDtypeStruct((N_HEADS, t_dim), F32)),
        grid=(t_dim // tb,),
        in_specs=[rows, cols, _const_spec((1, 128)), _const_spec((N_HEADS, 1)),
                  _const_spec((1, 128)), _const_spec((N_HEADS, 1))],
        out_specs=(rows, rows, rows, cols), compiler_params=_params("parallel"),
    )(dt_raw, dt_raw_t, bias_r, bias_c, alog_r, alog_c)


def _pair_decay(acs_slab, acs_c, h0, causal, left):
    other = pltpu.roll(acs_slab, HEAD_DIM, 1)
    col0 = jnp.where(left, acs_slab, other)
    col1 = jnp.where(left, other, acs_slab)
    l0 = jnp.exp(jnp.where(causal, col0 - acs_c[h0:h0 + 1, :], NEG_INF))
    l1 = jnp.exp(jnp.where(causal, col1 - acs_c[h0 + 1:h0 + 2, :], NEG_INF))
    return l0, l1


def _ssd_fwd(xc, dt_r, acs_r, acs_t, dskip_x, e3_mat):
    t_dim = xc.shape[0]
    nc = t_dim // CHUNK

    def body(xc_ref, dt_ref, acs_ref, acst_ref, dk_ref, e3_ref, y_ref, st_ref, state):
        @pl.when(pl.program_id(0) == 0)
        def _():
            state[...] = jnp.zeros_like(state)

        dt, acs = _expand_heads([dt_ref[...], acs_ref[...]], e3_ref[...])
        acs_c = acst_ref[...]
        st_ref[0] = state[...]
        last = acs[CHUNK - 1:CHUNK, :]
        xs32 = xc_ref[:, 0:D_INNER].astype(F32)
        xdt = xs32 * dt
        xdt16 = xdt.astype(BF16)
        xdte16 = (xdt * jnp.exp(last - acs)).astype(BF16)
        ea = jnp.exp(acs)
        cd = jnp.exp(last)
        skip = dk_ref[...] * xs32
        causal = (lax.broadcasted_iota(jnp.int32, (CHUNK, CHUNK), 0)
                  >= lax.broadcasted_iota(jnp.int32, (CHUNK, CHUNK), 1))
        left = lax.broadcasted_iota(jnp.int32, (CHUNK, PAIR), 1) < HEAD_DIM
        for g in range(N_GROUPS):
            gl = slice(g * GROUP_LANES, (g + 1) * GROUP_LANES)
            bg = xc_ref[:, D_INNER + g * D_STATE:D_INNER + (g + 1) * D_STATE]
            cg = xc_ref[:, D_INNER + (N_GROUPS + g) * D_STATE:D_INNER + (N_GROUPS + g + 1) * D_STATE]
            cb = lax.dot_general(cg, bg, _DIMS["nt"], preferred_element_type=F32)
            hprev = state[:, gl]
            ch = jnp.dot(cg, hprev.astype(BF16), preferred_element_type=F32)
            for j in range(HEADS_PER_GROUP // 2):
                pl_ = slice(g * GROUP_LANES + j * PAIR, g * GROUP_LANES + (j + 1) * PAIR)
                h0 = g * HEADS_PER_GROUP + 2 * j
                l0, l1 = _pair_decay(acs[:, pl_], acs_c, h0, causal, left)
                lhs = jnp.concatenate([(cb * l0).astype(BF16), (cb * l1).astype(BF16)], axis=1)
                xp = xdt16[:, pl_]
                zero = jnp.zeros_like(xp)
                rhs = jnp.concatenate([jnp.where(left, xp, zero), jnp.where(left, zero, xp)], axis=0)
                ydiag = jnp.dot(lhs, rhs, preferred_element_type=F32)
                y_ref[:, pl_] = (ydiag + ch[:, j * PAIR:(j + 1) * PAIR] * ea[:, pl_] + skip[:, pl_]).astype(BF16)
            s_new = lax.dot_general(bg, xdte16[:, gl], _DIMS["tn"], preferred_element_type=F32)
            state[:, gl] = hprev * cd[:, gl] + s_new

    rows = lambda w: pl.BlockSpec((CHUNK, w), lambda c: (c, 0))
    return _pcall(
        body, name="ssd_fwd",
        out_shape=(jax.ShapeDtypeStruct((t_dim, D_INNER), BF16),
                   jax.ShapeDtypeStruct((nc, D_STATE, D_INNER), F32)),
        grid=(nc,),
        in_specs=[rows(CONV_DIM), rows(128), rows(128), pl.BlockSpec((N_HEADS, CHUNK), lambda c: (0, c)),
                  _const_spec((1, D_INNER)), _const_spec((_EXACT_PIECES * 128, D_INNER))],
        out_specs=(rows(D_INNER), pl.BlockSpec((1, D_STATE, D_INNER), lambda c: (c, 0, 0))),
        scratch_shapes=[pltpu.VMEM((D_STATE, D_INNER), F32)],
        compiler_params=_params("arbitrary"),
    )(xc, dt_r, acs_r, acs_t, dskip_x, e3_mat)


def _ssd_bwd(xc, dt_r, acs_r, sg_r, acs_t, alog_r, dskip_x, e3_mat, et_mat, states, dy):
    t_dim = xc.shape[0]
    nc = t_dim // CHUNK

    def body(xc_ref, dt_ref, acs_ref, sg_ref, acst_ref, ar_ref, dk_ref, e3_ref, et_ref, st_ref, dy_ref,
             dxc_ref, ddt_ref, small_ref, dstate, dacs_ref, dxdt_ref, acc_x, acc_r):
        step = pl.program_id(0)

        @pl.when(step == 0)
        def _():
            dstate[...] = jnp.zeros_like(dstate)
            acc_x[...] = jnp.zeros_like(acc_x)
            acc_r[...] = jnp.zeros_like(acc_r)

        dt_r = dt_ref[...]
        a_r = -jnp.exp(ar_ref[...])
        dt, acs = _expand_heads([dt_r, acs_ref[...]], e3_ref[...])
        acs_c = acst_ref[...]
        last = acs[CHUNK - 1:CHUNK, :]
        xs32 = xc_ref[:, 0:D_INNER].astype(F32)
        xdt = xs32 * dt
        xdt16 = xdt.astype(BF16)
        dte = jnp.exp(last - acs)
        xdte = xdt * dte
        xdte16 = xdte.astype(BF16)
        cd = jnp.exp(last)
        dy16 = dy_ref[...]
        dyv = dy16.astype(F32)
        dye = dyv * jnp.exp(acs)
        dye16 = dye.astype(BF16)
        causal = (lax.broadcasted_iota(jnp.int32, (CHUNK, CHUNK), 0)
                  >= lax.broadcasted_iota(jnp.int32, (CHUNK, CHUNK), 1))
        left = lax.broadcasted_iota(jnp.int32, (CHUNK, PAIR), 1) < HEAD_DIM
        lane_id = lax.broadcasted_iota(jnp.int32, (CHUNK, 128), 1)
        row_id = lax.broadcasted_iota(jnp.int32, (CHUNK, 128), 0)
        is_last_row = lax.broadcasted_iota(jnp.int32, (CHUNK, 1), 0) == CHUNK - 1
        dacs_cols = jnp.zeros((CHUNK, 128), F32)
        dacs_rows = jnp.zeros((CHUNK, 128), F32)
        for g in range(N_GROUPS):
            gl = slice(g * GROUP_LANES, (g + 1) * GROUP_LANES)
            b_lanes = slice(D_INNER + g * D_STATE, D_INNER + (g + 1) * D_STATE)
            c_lanes = slice(D_INNER + (N_GROUPS + g) * D_STATE, D_INNER + (N_GROUPS + g + 1) * D_STATE)
            bg = xc_ref[:, b_lanes]
            cg = xc_ref[:, c_lanes]
            cb = lax.dot_general(cg, bg, _DIMS["nt"], preferred_element_type=F32)
            hprev = st_ref[0, :, gl]
            hp16 = hprev.astype(BF16)
            dhn = dstate[:, gl]
            dhn16 = dhn.astype(BF16)
            ch = jnp.dot(cg, hp16, preferred_element_type=F32)
            gmat = jnp.dot(bg, dhn16, preferred_element_type=F32)
            gx = gmat * xdte[:, gl]
            dlast = jnp.sum(gx, axis=0, keepdims=True) + cd[:, gl] * jnp.sum(dhn * hprev, axis=0, keepdims=True)
            dacs_ref[:, gl] = dye[:, gl] * ch - gx + jnp.where(is_last_row, dlast, 0.0)
            dc_acc = lax.dot_general(dye16[:, gl], hp16, _DIMS["nt"], preferred_element_type=F32)
            db_acc = lax.dot_general(xdte16[:, gl], dhn16, _DIMS["nt"], preferred_element_type=F32)
            dstate[:, gl] = dhn * cd[:, gl] + lax.dot_general(cg, dye16[:, gl], _DIMS["tn"],
                                                             preferred_element_type=F32)
            dcb = jnp.zeros((CHUNK, CHUNK), F32)
            for j in range(HEADS_PER_GROUP // 2):
                pl_ = slice(g * GROUP_LANES + j * PAIR, g * GROUP_LANES + (j + 1) * PAIR)
                h0 = g * HEADS_PER_GROUP + 2 * j
                l0, l1 = _pair_decay(acs[:, pl_], acs_c, h0, causal, left)
                m0, m1 = cb * l0, cb * l1
                lhs = jnp.concatenate([m0.astype(BF16), m1.astype(BF16)], axis=1)
                dyp = dy16[:, pl_]
                zero = jnp.zeros_like(dyp)
                both = lax.dot_general(lhs, dyp, _DIMS["tn"], preferred_element_type=F32)
                dxdt_ref[:, pl_] = (jnp.where(left, both[0:CHUNK, :], both[CHUNK:, :])
                                    + gmat[:, j * PAIR:(j + 1) * PAIR] * dte[:, pl_])
                lhs2 = jnp.concatenate([jnp.where(left, dyp, zero), jnp.where(left, zero, dyp)], axis=0)
                dm = lax.dot_general(lhs2, xdt16[:, pl_], _DIMS["nt"], preferred_element_type=F32)
                dm0, dm1 = dm[0:CHUNK, :], dm[CHUNK:, :]
                dcb = dcb + dm0 * l0 + dm1 * l1
                ds0, ds1 = dm0 * m0, dm1 * m1
                dacs_cols = jnp.where(lane_id == h0, jnp.sum(ds0, axis=1, keepdims=True), dacs_cols)
                dacs_cols = jnp.where(lane_id == h0 + 1, jnp.sum(ds1, axis=1, keepdims=True), dacs_cols)
                dacs_rows = jnp.where(row_id == h0, jnp.sum(ds0, axis=0, keepdims=True), dacs_rows)
                dacs_rows = jnp.where(row_id == h0 + 1, jnp.sum(ds1, axis=0, keepdims=True), dacs_rows)
            dcb16 = dcb.astype(BF16)
            dxc_ref[:, c_lanes] = (dc_acc + jnp.dot(dcb16, bg, preferred_element_type=F32)).astype(BF16)
            dxc_ref[:, b_lanes] = (db_acc + lax.dot_general(dcb16, cg, _DIMS["tn"],
                                                           preferred_element_type=F32)).astype(BF16)
        dxdt = dxdt_ref[...]
        dxc_ref[:, 0:D_INNER] = (dxdt * dt + dk_ref[...] * dyv).astype(BF16)
        acc_x[0:1, :] += jnp.sum(dyv * xs32, axis=0, keepdims=True)
        et = et_ref[...]
        dacs = _reduce_heads(dacs_ref[...], et, 2) + dacs_cols - dacs_rows.T
        dadt = _rev_cumsum_rows(dacs)
        ddraw = (_reduce_heads(dxdt * xs32, et, 1) + dadt * a_r) * sg_ref[...]
        ddraw = jnp.where(lane_id < N_HEADS, ddraw, 0.0)
        ddt_ref[...] = ddraw
        acc_r[0:1, :] += jnp.where(lane_id[0:1, :] < N_HEADS,
                                   jnp.sum(dadt * dt_r, axis=0, keepdims=True) * a_r, 0.0)
        acc_r[1:2, :] += jnp.sum(ddraw, axis=0, keepdims=True)

        @pl.when(step == nc - 1)
        def _():
            dd = _reduce_heads(acc_x[...], et_ref[...], 3)
            rid = lax.broadcasted_iota(jnp.int32, (8, 128), 0)
            small_ref[...] = acc_r[...] + jnp.where(rid == 2, pltpu.roll(dd, 2, 0), 0.0)

    rev = lambda w: pl.BlockSpec((CHUNK, w), lambda c: (nc - 1 - c, 0))
    return _pcall(
        body, name="ssd_bwd",
        out_shape=(jax.ShapeDtypeStruct((t_dim, CONV_DIM), BF16),
                   jax.ShapeDtypeStruct((t_dim, 128), F32),
                   jax.ShapeDtypeStruct((8, 128), F32)),
        grid=(nc,),
        in_specs=[rev(CONV_DIM), rev(128), rev(128), rev(128),
                  pl.BlockSpec((N_HEADS, CHUNK), lambda c: (0, nc - 1 - c)),
                  _const_spec((1, 128)), _const_spec((1, D_INNER)),
                  _const_spec((_EXACT_PIECES * 128, D_INNER)), _const_spec((D_INNER, 128)),
                  pl.BlockSpec((1, D_STATE, D_INNER), lambda c: (nc - 1 - c, 0, 0)),
                  rev(D_INNER)],
        out_specs=(rev(CONV_DIM), rev(128), _const_spec((8, 128))),
        scratch_shapes=[pltpu.VMEM((D_STATE, D_INNER), F32), pltpu.VMEM((CHUNK, D_INNER), F32),
                        pltpu.VMEM((CHUNK, D_INNER), F32), pltpu.VMEM((8, D_INNER), F32),
                        pltpu.VMEM((8, 128), F32)],
        compiler_params=_params("arbitrary"),
    )(xc, dt_r, acs_r, sg_r, acs_t, alog_r, dskip_x, e3_mat, et_mat, states, dy)


_GATE_GROUP = D_INNER // N_GROUPS


def _gate_fwd(y, z, g, *, tb=512):
    t_dim = y.shape[0]

    def body(y_ref, z_ref, g_ref, o_ref):
        for gi in range(N_GROUPS):
            lanes = slice(gi * _GATE_GROUP, (gi + 1) * _GATE_GROUP)
            zv = z_ref[:, lanes].astype(F32)
            wv = y_ref[:, lanes].astype(F32) * (zv * _sigmoid(zv))
            o_ref[:, lanes] = (wv * _rms(wv) * g_ref[:, lanes]).astype(BF16)

    return _pcall(
        body, name="gate_fwd", out_shape=jax.ShapeDtypeStruct((t_dim, D_INNER), BF16),
        grid=(t_dim // tb,),
        in_specs=[_row_spec(tb, D_INNER), _row_spec(tb, D_INNER), _const_spec((1, D_INNER))],
        out_specs=_row_spec(tb, D_INNER), compiler_params=_params("parallel"),
    )(y, z, g)


def _gate_bwd(dyn, y, z, g, *, tb=512):
    t_dim = y.shape[0]

    def body(d_ref, y_ref, z_ref, g_ref, dy_ref, dz_ref, dg_ref):
        @pl.when(pl.program_id(0) == 0)
        def _():
            dg_ref[...] = jnp.zeros_like(dg_ref)

        for gi in range(N_GROUPS):
            lanes = slice(gi * _GATE_GROUP, (gi + 1) * _GATE_GROUP)
            zv = z_ref[:, lanes].astype(F32)
            sg = _sigmoid(zv)
            sz = zv * sg
            yv = y_ref[:, lanes].astype(F32)
            wv = yv * sz
            r = _rms(wv)
            what = wv * r
            dv = d_ref[:, lanes].astype(F32)
            dwhat = dv * g_ref[:, lanes]
            dw = r * (dwhat - what * jnp.mean(dwhat * what, axis=-1, keepdims=True))
            dg_ref[0:1, lanes] += jnp.sum(dv * what, axis=0, keepdims=True)
            dy_ref[:, lanes] = (dw * sz).astype(BF16)
            dz_ref[:, lanes] = (dw * yv * (sg * (1.0 + zv * (1.0 - sg)))).astype(BF16)

    return _pcall(
        body, name="gate_bwd",
        out_shape=(jax.ShapeDtypeStruct((t_dim, D_INNER), BF16),
                   jax.ShapeDtypeStruct((t_dim, D_INNER), BF16),
                   jax.ShapeDtypeStruct((8, D_INNER), F32)),
        grid=(t_dim // tb,),
        in_specs=[_row_spec(tb, D_INNER), _row_spec(tb, D_INNER), _row_spec(tb, D_INNER),
                  _const_spec((1, D_INNER))],
        out_specs=(_row_spec(tb, D_INNER), _row_spec(tb, D_INNER), _const_spec((8, D_INNER))),
        compiler_params=_params("arbitrary"),
    )(dyn, y, z, g)


_ADAM_C1 = 1.0 / (1.0 - ADAM_B1 ** ADAM_STEP)
_ADAM_C2 = 1.0 / (1.0 - ADAM_B2 ** ADAM_STEP)


def _adamw_math(w, g, m, v):
    mn = ADAM_B1 * m + (1.0 - ADAM_B1) * g
    vn = ADAM_B2 * v + (1.0 - ADAM_B2) * (g * g)
    delta = -ADAM_LR * ((mn * _ADAM_C1) / (jnp.sqrt(vn * _ADAM_C2) + ADAM_EPS) + ADAM_WD * w)
    return delta, mn, vn


def _adamw(w, g, m, v, *, name, part=None, into=None):
    r_dim, c = w.shape
    rows = r_dim if part is None else r_dim // 2
    assert g.shape == (rows, c)
    tb = max(t for t in range(8, 513, 8) if rows % t == 0)
    first = 0 if part is None else part * (rows // tb)
    n_out = 3 if part is None else 4

    def body(w_ref, g_ref, m_ref, v_ref, *rest):
        outs = rest[-n_out:]
        gv = g_ref[...]
        outs[0][...], outs[1][...], outs[2][...] = _adamw_math(w_ref[...], gv, m_ref[...], v_ref[...])
        if part is not None:
            outs[3][...] = gv

    spec = pl.BlockSpec((tb, c), lambda i: (first + i, 0))
    sds = jax.ShapeDtypeStruct((r_dim, c), F32)
    in_specs = [spec, _row_spec(tb, c), spec, spec]
    operands = [w, g, m, v]
    aliases = {}
    if into is not None:
        in_specs += [_ANY] * n_out
        operands += list(into)
        aliases = {4 + i: i for i in range(n_out)}
    outs = _pcall(
        body, name=name, out_shape=(sds,) * n_out, grid=(rows // tb,),
        in_specs=in_specs, out_specs=(spec,) * n_out, input_output_aliases=aliases,
        compiler_params=_params("parallel"),
    )(*operands)
    return tuple(outs) if part is not None else tuple(outs) + (g,)


def _adamw_small(params, *, name):
    n = len(params)

    def body(*refs):
        ins, outs = refs[:4 * n], refs[4 * n:]
        for i in range(n):
            w_ref, g_ref, m_ref, v_ref = ins[4 * i:4 * i + 4]
            res = _adamw_math(w_ref[...], g_ref[...], m_ref[...], v_ref[...])
            for o_ref, r in zip(outs[3 * i:3 * i + 3], res):
                o_ref[...] = r

    vmem = pl.BlockSpec(memory_space=pltpu.VMEM)
    flat = [a for p in params for a in p]
    outs = _pcall(
        body, name=name,
        out_shape=tuple(jax.ShapeDtypeStruct(p[0].shape, F32) for p in params for _ in range(3)),
        in_specs=[vmem] * (4 * n), out_specs=(vmem,) * (3 * n),
    )(*flat)
    return [tuple(outs[3 * i:3 * i + 3]) for i in range(n)]


def _pair_sum(grad, recv, place, *, name):
    s_dim, r_dim, c = grad.shape
    half = r_dim // 2
    tb = half if half <= 1024 else 256
    per_half = half // tb

    def body(place_ref, a_ref, b_ref, o16_ref, o32_ref):
        s = a_ref[...] + b_ref[...]
        o16_ref[...] = s.astype(BF16)

        @pl.when(pl.program_id(1) == place_ref[1])
        def _():
            o32_ref[...] = s[0]

    grid_spec = pltpu.PrefetchScalarGridSpec(
        num_scalar_prefetch=1, grid=(per_half, s_dim),
        in_specs=[pl.BlockSpec((1, tb, c), lambda i, s, p: (s, p[0] * per_half + i, 0)),
                  pl.BlockSpec((1, tb, c), lambda i, s, p: (s, i, 0))],
        out_specs=(pl.BlockSpec((1, tb, c), lambda i, s, p: (s, i, 0)),
                   pl.BlockSpec((tb, c), lambda i, s, p: (i, 0))))
    return _pcall(
        body, name=name, grid_spec=grid_spec,
        out_shape=(jax.ShapeDtypeStruct((s_dim, half, c), BF16), jax.ShapeDtypeStruct((half, c), F32)),
        compiler_params=_params("parallel", "arbitrary"),
    )(place, grad, recv)


def _chip_sum(own, recv, place, *, name):
    r_dim, c = own.shape
    tb = r_dim if r_dim <= 1024 else 256

    def body(place_ref, a_ref, b_ref, o_ref):
        s = a_ref[...]
        for k in range(1, N_CHIPS):
            s = s + b_ref[k].astype(F32)
        o_ref[...] = s

    grid_spec = pltpu.PrefetchScalarGridSpec(
        num_scalar_prefetch=1, grid=(r_dim // tb,),
        in_specs=[pl.BlockSpec((tb, c), lambda i, p: (i, 0)),
                  pl.BlockSpec((N_CHIPS, tb, c), lambda i, p: (0, i, 0))],
        out_specs=pl.BlockSpec((None, tb, c), lambda i, p: (p[0], i, 0)))
    return _pcall(
        body, name=name, grid_spec=grid_spec, out_shape=jax.ShapeDtypeStruct((2, r_dim, c), F32),
        compiler_params=_params("parallel"),
    )(place, own, recv)


def _position():
    return lax.axis_index("x"), lax.axis_index("y"), lax.axis_index("c")


def _chip_peer(x, y, k):
    return x ^ (k >> 1), y ^ (k & 1)


_ANY = pl.BlockSpec(memory_space=pl.ANY)
_TOKEN = jax.ShapeDtypeStruct((8, 128), F32)


def _all_gather_weights(shards):
    n = len(shards)
    hops = N_CHIPS - 1

    def body(*refs):
        srcs, outs, done = refs[:n], refs[n:2 * n], refs[2 * n]
        send_sems, recv_sems = refs[2 * n + 1:]
        x, y, c = _position()
        me = 2 * x + y
        done[...] = jnp.zeros_like(done)

        def over_ici(w, k, chip, to):
            return pltpu.make_async_remote_copy(
                src_ref=srcs[w].at[c], dst_ref=outs[w].at[chip, c],
                send_sem=send_sems.at[w, k - 1], recv_sem=recv_sems.at[w, k - 1],
                device_id=to, device_id_type=MESH)

        def over_d2d(w, k, chip, half):
            return pltpu.make_async_remote_copy(
                src_ref=outs[w].at[chip, half], dst_ref=outs[w].at[chip, half],
                send_sem=send_sems.at[w, hops + k - 1], recv_sem=recv_sems.at[w, hops + k - 1],
                device_id=(x, y, 1 - c), device_id_type=MESH)

        sends = []
        for w in range(n):
            for k in range(1, N_CHIPS):
                px, py = _chip_peer(x, y, k)
                cp = over_ici(w, k, me, (px, py, c))
                cp.start()
                sends.append(cp)
        for w in range(n):
            for k in range(1, N_CHIPS):
                px, py = _chip_peer(x, y, k)
                over_ici(w, k, 2 * px + py, (px, py, c)).wait_recv()
                cp = over_d2d(w, k, 2 * px + py, c)
                cp.start()
                sends.append(cp)
        for w in range(n):
            for k in range(1, N_CHIPS):
                px, py = _chip_peer(x, y, k)
                over_d2d(w, k, 2 * px + py, 1 - c).wait_recv()
        for cp in sends:
            cp.wait_send()

    outs = _pcall(
        body, name="gather_weights",
        out_shape=tuple(jax.ShapeDtypeStruct((N_CHIPS,) + s.shape, s.dtype) for s in shards) + (_TOKEN,),
        in_specs=[_ANY] * n, out_specs=(_ANY,) * n + (pl.BlockSpec(memory_space=pltpu.VMEM),),
        scratch_shapes=[pltpu.SemaphoreType.DMA((n, 2 * hops)),
                        pltpu.SemaphoreType.DMA((n, 2 * hops))],
    )(*shards)
    return outs[:n], outs[n][0, 0]


def _pair_copies(srcs, lands, send_sems, recv_sems):
    x, y, c = _position()
    copies = []
    for w in range(len(srcs)):
        half = srcs[w].shape[1] // 2
        copies.append(pltpu.make_async_remote_copy(
            src_ref=srcs[w].at[:, pl.ds((1 - c) * half, half), :], dst_ref=lands[w],
            send_sem=send_sems.at[w], recv_sem=recv_sems.at[w],
            device_id=(x, y, 1 - c), device_id_type=MESH))
    return copies


def _chip_copies(srcs, lands, send_sems, recv_sems):
    x, y, c = _position()
    copies = []
    for w in range(len(srcs)):
        for k in range(1, N_CHIPS):
            px, py = _chip_peer(x, y, k)
            i = w * (N_CHIPS - 1) + k - 1
            copies.append(pltpu.make_async_remote_copy(
                src_ref=srcs[w].at[2 * px + py], dst_ref=lands[w].at[k],
                send_sem=send_sems.at[i], recv_sem=recv_sems.at[i],
                device_id=(px, py, c), device_id_type=MESH))
    return copies


def _gather_copies(srcs, lands, send_sems, recv_sems):
    x, y, c = _position()
    me = 2 * x + y
    copies = []
    for w in range(len(srcs)):
        for k in range(1, N_CHIPS):
            px, py = _chip_peer(x, y, k)
            i = w * (N_CHIPS - 1) + k - 1
            copies.append(pltpu.make_async_remote_copy(
                src_ref=srcs[w].at[c], dst_ref=lands[w].at[me, c],
                send_sem=send_sems.at[i], recv_sem=recv_sems.at[i],
                device_id=(px, py, c), device_id_type=MESH))
    return copies


def _exchange(name, copies_of, n_copies, srcs, land_shapes):
    n = len(srcs)

    def body(*refs):
        copies = copies_of(refs[:n], refs[n:2 * n], refs[2 * n], refs[2 * n + 1])
        for cp in copies:
            cp.start()
        for cp in copies:
            cp.wait_recv()
        for cp in copies:
            cp.wait_send()

    return _pcall(
        body, name=name, out_shape=tuple(land_shapes),
        in_specs=[_ANY] * n, out_specs=(_ANY,) * n,
        scratch_shapes=[pltpu.SemaphoreType.DMA((n_copies,)), pltpu.SemaphoreType.DMA((n_copies,))],
    )(*srcs)


_HBM = pl.BlockSpec(memory_space=pltpu.HBM)
_SEM = pl.BlockSpec(memory_space=pltpu.SEMAPHORE)
_DATAFLOW = pltpu.SideEffectType.DATAFLOW_SIDE_EFFECTING


def _exchange_start(name, copies_of, n_copies, srcs, land_shapes):
    n = len(srcs)
    lands = [lax.empty(s.shape, s.dtype) for s in land_shapes]

    def body(*refs):
        for cp in copies_of(refs[:n], refs[n:2 * n], refs[2 * n], refs[2 * n + 1]):
            cp.start()
        refs[-1][...] = jnp.zeros_like(refs[-1])

    through = [pltpu.HBM(a.shape, a.dtype) for a in list(srcs) + lands]
    outs = _pcall(
        body, name=name,
        out_shape=(pltpu.SemaphoreType.DMA((n_copies,)), pltpu.SemaphoreType.DMA((n_copies,)),
                   *through, jax.ShapeDtypeStruct((8, 128), F32)),
        in_specs=[_HBM] * (2 * n),
        out_specs=(_SEM, _SEM, *([_HBM] * (2 * n)), pl.BlockSpec(memory_space=pltpu.VMEM)),
        input_output_aliases={i: 2 + i for i in range(2 * n)},
        compiler_params=pltpu.CompilerParams(has_side_effects=_DATAFLOW),
    )(*[pltpu.with_memory_space_constraint(a, pltpu.HBM) for a in list(srcs) + lands])
    return outs[:-1], outs[-1][0, 0]


def _exchange_wait(name, copies_of, state, after):
    send_sems, recv_sems, through = state[0], state[1], state[2:]
    n = len(through) // 2
    if after.ndim == 0:
        after = jnp.broadcast_to(after, (8, 128))
    after = pltpu.with_memory_space_constraint(after, pltpu.HBM)

    def body(*refs):
        for cp in copies_of(refs[:n], refs[n:2 * n], refs[2 * n], refs[2 * n + 1]):
            cp.wait_send()
            cp.wait_recv()

    outs = _pcall(
        body, name=name,
        out_shape=tuple(pltpu.HBM(a.shape, a.dtype) for a in through),
        in_specs=[_HBM] * (2 * n) + [_SEM, _SEM, _HBM], out_specs=tuple([_HBM] * (2 * n)),
        input_output_aliases={i: i for i in range(2 * n)},
        compiler_params=pltpu.CompilerParams(has_side_effects=_DATAFLOW),
    )(*through, send_sems, recv_sems, after)
    return outs[:n], outs[n:]


def _forward_halves(lands, *, name):
    n = len(lands)
    hops = N_CHIPS - 1

    def body(*refs):
        ins, outs, done = refs[:n], refs[n:2 * n], refs[2 * n]
        send_sems, recv_sems = refs[2 * n + 1], refs[2 * n + 2]
        x, y, c = _position()
        done[...] = jnp.zeros_like(done)
        copies = []
        for w in range(n):
            for k in range(1, N_CHIPS):
                px, py = _chip_peer(x, y, k)
                i = w * hops + k - 1
                copies.append(pltpu.make_async_remote_copy(
                    src_ref=ins[w].at[2 * px + py, c], dst_ref=outs[w].at[2 * px + py, c],
                    send_sem=send_sems.at[i], recv_sem=recv_sems.at[i],
                    device_id=(x, y, 1 - c), device_id_type=MESH))
        for cp in copies:
            cp.start()
        for cp in copies:
            cp.wait_recv()
        for cp in copies:
            cp.wait_send()

    outs = _pcall(
        body, name=name,
        out_shape=tuple(jax.ShapeDtypeStruct(a.shape, a.dtype) for a in lands) + (_TOKEN,),
        in_specs=[_ANY] * n, out_specs=(_ANY,) * n + (pl.BlockSpec(memory_space=pltpu.VMEM),),
        input_output_aliases={i: i for i in range(n)},
        scratch_shapes=[pltpu.SemaphoreType.DMA((n * hops,)), pltpu.SemaphoreType.DMA((n * hops,))],
    )(*lands)
    return outs[:n], outs[n][0, 0]


def _pair_lands(grads):
    return [jax.ShapeDtypeStruct((g.shape[0], g.shape[1] // 2, g.shape[2]), F32) for g in grads]


def _same_lands(parts):
    return [jax.ShapeDtypeStruct(p.shape, p.dtype) for p in parts]


def _pair_gather_halves(halves, *, name):
    n = len(halves)

    def body(*refs):
        ins, outs = refs[:n], refs[n:2 * n]
        send_sems, recv_sems = refs[2 * n:]
        x, y, c = _position()
        sends = []
        for w in range(n):
            cp = pltpu.make_async_remote_copy(
                src_ref=ins[w].at[c], dst_ref=outs[w].at[c],
                send_sem=send_sems.at[w], recv_sem=recv_sems.at[w],
                device_id=(x, y, 1 - c), device_id_type=MESH)
            cp.start()
            sends.append(cp)
        for cp in sends:
            cp.wait_recv()
        for cp in sends:
            cp.wait_send()

    whole = _pcall(
        body, name=name,
        out_shape=tuple(jax.ShapeDtypeStruct(h.shape, F32) for h in halves),
        in_specs=[_ANY] * n, out_specs=(_ANY,) * n,
        input_output_aliases={i: i for i in range(n)},
        scratch_shapes=[pltpu.SemaphoreType.DMA((n,)), pltpu.SemaphoreType.DMA((n,))],
    )(*halves)
    return [w.reshape(2 * w.shape[1], w.shape[2]) for w in whole]


def _all_reduce_small(packed, *, name, sum_row0):
    r_dim, c = packed.shape

    def body(src_ref, out_ref, recv_ref, send_sems, recv_sems):
        x, y, c_ = _position()
        me = 4 * x + 2 * y + c_
        recv_ref[0] = src_ref[...]
        sends = []
        for k in range(1, N_DEV):
            peer = (x ^ (k >> 2), y ^ ((k >> 1) & 1), c_ ^ (k & 1))
            cp = pltpu.make_async_remote_copy(
                src_ref=src_ref, dst_ref=recv_ref.at[k],
                send_sem=send_sems.at[k - 1], recv_sem=recv_sems.at[k - 1],
                device_id=peer, device_id_type=MESH)
            cp.start()
            sends.append(cp)
        for cp in sends:
            cp.wait_recv()
        total = recv_ref[me]
        for d in range(1, N_DEV):
            total = total + recv_ref[d ^ me]
        if sum_row0:
            row0 = jnp.sum(total[0:1, :], axis=1, keepdims=True)
            rid = lax.broadcasted_iota(jnp.int32, total.shape, 0)
            total = jnp.where(rid == 0, row0, total)
        out_ref[...] = total
        for cp in sends:
            cp.wait_send()

    return _pcall(
        body, name=name, out_shape=jax.ShapeDtypeStruct((r_dim, c), F32),
        in_specs=[pl.BlockSpec(memory_space=pltpu.VMEM)],
        out_specs=pl.BlockSpec(memory_space=pltpu.VMEM),
        scratch_shapes=[pltpu.VMEM((N_DEV, r_dim, c), F32),
                        pltpu.SemaphoreType.DMA((N_DEV - 1,)), pltpu.SemaphoreType.DMA((N_DEV - 1,))],
    )(packed)


def _pad_lanes(v, width):
    return jnp.pad(v, ((0, 0), (0, width - v.shape[1])))


def _pad_rows(v, rows):
    pad = [(0, 0)] * v.ndim
    pad[-2] = (0, rows - v.shape[-2])
    return jnp.pad(v, pad)


_IN_PROJ_SHARD_ROWS = 1312


def _rows_1024(v):
    flat = v.reshape(-1)
    pad = (-flat.shape[0]) % D_MODEL
    return jnp.pad(flat, (0, pad)).reshape(-1, D_MODEL)


def _local_step(xs, target, pw, fetch, reduce_start, reduce_midway,
                conv_w, conv_b, gate_g,
                norm_mix_g, norm_mlp_g, pool_b, pool_scale, ssm_dt_bias, ssm_a_log, ssm_d, final_g):
    bias_r = _pad_lanes(ssm_dt_bias, 128)
    alog_r = _pad_lanes(ssm_a_log, 128)
    dskip_x = jnp.repeat(ssm_d, HEAD_DIM, axis=1)
    bias_c = ssm_dt_bias.reshape(N_HEADS, 1)
    alog_c = ssm_a_log.reshape(N_HEADS, 1)
    e_mat = _head_lane_matrix()
    e3_mat = jnp.tile(e_mat, (_EXACT_PIECES, 1))

    g_mix0, g_mix1 = norm_mix_g[0:1], norm_mix_g[1:2]
    g_mlp0, g_mlp1 = norm_mlp_g[0:1], norm_mlp_g[1:2]
    fg = final_g.reshape(1, D_MODEL)

    h1 = _pool_fwd(xs, g_mix0, pw, pool_b, pool_scale)
    w1_0 = fetch("mlp0_up", h1)
    u0, hm0 = _matmul(h1, w1_0, "nn", name="mlp0_up", out_dtype=BF16, b_col_shards=True, a_norm=g_mlp0)
    w2_0 = fetch("mlp0_down", u0)
    h2 = _matmul(u0, w2_0, "nn", name="mlp0_down", a_relu2=True, add=h1)

    w_z, w_xbc, w_dt = fetch("in_proj", h2)
    xbc, hn1 = _matmul(h2, w_xbc, "nt", name="in_proj_xbc", out_dtype=BF16, a_norm=g_mix1)
    z = _matmul(hn1, w_z, "nt", name="in_proj_z", out_dtype=BF16)
    dt_raw = _matmul(hn1, w_dt, "nt", name="in_proj_dt")
    dt_raw_t = dt_raw[:, :N_HEADS].T
    xc = _conv_fwd(xbc, conv_w, conv_b)
    wout, w1_1, w2_1 = fetch("rest", xc)
    dt_r, acs_r, sg_r, acs_t = _ssd_decay(dt_raw, dt_raw_t, bias_r, bias_c, alog_r, alog_c)
    y, states = _ssd_fwd(xc, dt_r, acs_r, acs_t, dskip_x, e3_mat)
    yn = _gate_fwd(y, z, gate_g)
    h3 = _matmul(yn, wout, "nn", name="out_proj", add=h2)
    u1, hm1 = _matmul(h3, w1_1, "nn", name="mlp1_up", out_dtype=BF16, b_col_shards=True, a_norm=g_mlp1)

    dh4, dh4_16, small_final = _matmul(u1, w2_1, "nn", name="mlp1_down", a_relu2=True, add=h3,
                                       loss_head=(fg, target))

    def mlp_bwd_weights(dh_out16, hm, u, w2_i, tag):
        du = _matmul(dh_out16, w2_i, "nt", name=tag + "_du", out_dtype=BF16, relu2_grad_of=u)
        dw2 = _matmul(u, dh_out16, "tn", name=tag + "_dw2", a_relu2=True)
        dw1 = _matmul(hm, du, "tn", name=tag + "_dw1", out_col_shards=N_CHIPS)
        return du, dw1, dw2.reshape(N_CHIPS, D_FF // N_CHIPS, D_MODEL)

    def mlp_bwd_input(du, dh_out, h_in, w1_i, g_i, tag):
        return _matmul(du, w1_i, "nt", name=tag + "_dhm", b_col_shards=True, norm_bwd=(h_in, g_i, dh_out))

    du1, dw1_1, dw2_1 = mlp_bwd_weights(dh4_16, hm1, u1, w2_1, "mlp1")
    dh3, dh3_16, dg_mlp1 = mlp_bwd_input(du1, dh4, h3, w1_1, g_mlp1, "mlp1")

    dyn = _matmul(dh3_16, wout, "nt", name="out_proj_dyn", out_dtype=BF16)
    dwout = _matmul(yn, dh3_16, "tn", name="out_proj_dw").reshape(N_CHIPS, D_INNER // N_CHIPS, D_MODEL)
    behind = reduce_start("mlp1_out", [dw1_1, dw2_1, dwout])
    dy, dz, dg_gate = _gate_bwd(dyn, y, z, gate_g + behind)
    behind = reduce_midway("mlp1_out", dz)
    dxc, ddt_raw, small_ssd = _ssd_bwd(xc, dt_r, acs_r, sg_r, acs_t, alog_r, dskip_x + behind,
                                       e3_mat, e_mat.T, states, dy)
    dv, dconv = _conv_bwd_act(xbc, dxc, conv_w, conv_b)
    dxbc = _conv_bwd_in(dv, conv_w)
    dw_z = _matmul(dz, hn1, "tn", name="in_proj_z_dw")
    dw_xbc = _matmul(dxbc, hn1, "tn", name="in_proj_xbc_dw")
    dw_dt = _matmul(ddt_raw, hn1, "tn", name="in_proj_dt_dw")
    dwin = jnp.concatenate([dw_z, dw_xbc, dw_dt[:N_HEADS]], axis=0)
    dwin = _pad_rows(dwin.reshape(N_CHIPS, IN_PROJ_DIM // N_CHIPS, D_MODEL), _IN_PROJ_SHARD_ROWS)
    behind = reduce_start("in_proj", [dwin])
    dh2, dh2_16, dg_mix1 = _matmul(dxbc, w_xbc, "nn", name="in_proj_dh", more=[(dz, w_z), (ddt_raw, w_dt)],
                                   norm_bwd=(h2, g_mix1 + behind, dh3))
    behind = reduce_midway("in_proj", dh2_16)

    du0, dw1_0, dw2_0 = mlp_bwd_weights(dh2_16, hm0, u0, w2_0, "mlp0")
    dh1, _, dg_mlp0 = mlp_bwd_input(du0, dh2, h1, w1_0, g_mlp0 + behind, "mlp0")
    dx, dpw, small_pool = _pool_bwd(xs, g_mix0, pw, pool_b, pool_scale, dh1)
    dpw = jnp.transpose(dpw.reshape(4, N_CHIPS, POOL_GROUP // N_CHIPS, POOL_GROUP), (1, 0, 2, 3))
    dpw = dpw.reshape(N_CHIPS, 4 * (POOL_GROUP // N_CHIPS), POOL_GROUP)

    big = [dpw, dw1_0, dw2_0]
    rows = [
        small_final[1:2],
        small_final[0:1],
        small_pool[0:1], dg_mix1[0:1],
        dg_mlp0[0:1], dg_mlp1[0:1],
        small_pool[1:2], small_pool[2:3],
        _pad_lanes(small_ssd[0:3], D_MODEL),
        _rows_1024(dg_gate[0:1]),
        _rows_1024(dconv[0:CONV_K]),
        _rows_1024(dconv[CONV_K:CONV_K + 1]),
    ]
    return dx, big, rows


def kernel(x, norm_mix_g, norm_mlp_g, pool_w, pool_b, pool_scale, ssm_w_in, ssm_conv_w, ssm_conv_b, ssm_dt_bias, ssm_a_log, ssm_d, ssm_norm_g, ssm_w_out, mlp_w1, mlp_w2, final_g, loss_target, m_norm_mix_g, m_norm_mlp_g, m_pool_w, m_pool_b, m_pool_scale, m_ssm_w_in, m_ssm_conv_w, m_ssm_conv_b, m_ssm_dt_bias, m_ssm_a_log, m_ssm_d, m_ssm_norm_g, m_ssm_w_out, m_mlp_w1, m_mlp_w2, m_final_g, v_norm_mix_g, v_norm_mlp_g, v_pool_w, v_pool_b, v_pool_scale, v_ssm_w_in, v_ssm_conv_w, v_ssm_conv_b, v_ssm_dt_bias, v_ssm_a_log, v_ssm_d, v_ssm_norm_g, v_ssm_w_out, v_mlp_w1, v_mlp_w2, v_final_g):
    xs = x[0]
    target = loss_target[0]
    my_x, my_y, my_c = _position()
    my_chip = 2 * my_x + my_y

    def halves(w):
        return w.astype(BF16).reshape((2, w.shape[0] // 2) + w.shape[1:])

    def whole(gathered, own_shard):
        g = lax.dynamic_update_index_in_dim(gathered, own_shard, my_chip, axis=0)
        return g.reshape((N_CHIPS, 2 * g.shape[2]) + g.shape[3:])

    def gather_lands(own):
        return [jax.ShapeDtypeStruct((N_CHIPS,) + s.shape, s.dtype) for s in own]

    vec_cols = CONV_DIM // N_CHIPS
    vec_own = jnp.concatenate([ssm_conv_w[0], ssm_conv_b, _pad_lanes(ssm_norm_g, vec_cols)], axis=0)
    early_own = [halves(pool_w[0]), vec_own.reshape(2, (CONV_K + 2) // 2, vec_cols)]
    early, behind_early = _all_gather_weights(early_own)
    g_pool, g_vec = [whole(g, o) for g, o in zip(early, early_own)]
    pw = jnp.transpose(g_pool, (1, 0, 2, 3)).reshape(4, POOL_GROUP, POOL_GROUP)
    conv_w = jnp.transpose(g_vec[:, 0:CONV_K, :], (1, 0, 2)).reshape(CONV_K, CONV_DIM)
    conv_b = g_vec[:, CONV_K, :].reshape(1, CONV_DIM)
    gate_g = g_vec[:, CONV_K + 1, :D_INNER // N_CHIPS].reshape(1, D_INNER)

    def behind_it(zero, ws):
        return [halves(w + zero) for w in ws]

    fetches = {}
    up_own = behind_it(behind_early, [mlp_w1[0]])
    fetches["mlp0_up"], behind_gather = _exchange_start(
        "gather_mlp0_up_start", _gather_copies, len(up_own) * (N_CHIPS - 1), up_own, gather_lands(up_own))
    down_own = behind_it(behind_gather, [mlp_w2[0]])
    fetches["mlp0_down"], behind_gather = _exchange_start(
        "gather_mlp0_down_start", _gather_copies, len(down_own) * (N_CHIPS - 1), down_own, gather_lands(down_own))
    in_own = behind_it(behind_gather, [_pad_rows(ssm_w_in[0].T, _IN_PROJ_SHARD_ROWS)])
    fetches["in_proj"], behind_gather = _exchange_start(
        "gather_in_proj_start", _gather_copies, len(in_own) * (N_CHIPS - 1), in_own, gather_lands(in_own))

    def fetch(what, after):
        if what == "mlp0_up":
            own_thru, landed = _exchange_wait("gather_mlp0_up_wait", _gather_copies, fetches[what], after)
            landed, _ = _forward_halves(landed, name="forward_mlp0_up")
            return whole(landed[0], own_thru[0])
        if what == "mlp0_down":
            own_thru, landed = _exchange_wait("gather_mlp0_down_wait", _gather_copies, fetches[what], after)
            landed, _ = _forward_halves(landed, name="forward_mlp0_down")
            return whole(landed[0], own_thru[0]).reshape(D_FF, D_MODEL)
        if what == "in_proj":
            own_thru, landed = _exchange_wait("gather_in_proj_wait", _gather_copies, fetches["in_proj"], after)
            landed, behind = _forward_halves(landed, name="forward_in_proj")
            rest = behind_it(behind, [ssm_w_out[0], mlp_w1[1], mlp_w2[1]])
            fetches["rest"], behind = _exchange_start(
                "gather_rest_start", _gather_copies, len(rest) * (N_CHIPS - 1), rest, gather_lands(rest))
            win = whole(landed[0], own_thru[0])[:, :IN_PROJ_DIM // N_CHIPS].reshape(IN_PROJ_DIM, D_MODEL)
            w_dt = _pad_rows(win[D_INNER + CONV_DIM:], 128) + behind.astype(BF16)
            return win[:D_INNER], win[D_INNER:D_INNER + CONV_DIM], w_dt
        own_thru, landed = _exchange_wait("gather_rest_wait", _gather_copies, fetches["rest"], after)
        landed, _ = _forward_halves(landed, name="forward_rest")
        g_wout, w1_1, g_w2_1 = [whole(g, o) for g, o in zip(landed, own_thru)]
        return g_wout.reshape(D_INNER, D_MODEL), w1_1, g_w2_1.reshape(D_FF, D_MODEL)

    place = jnp.stack([my_c, my_chip]).astype(jnp.int32)
    waves = {}

    def reduce_start(wave, grads):
        waves[wave] = {}
        waves[wave]["pair"], behind = _exchange_start(
            "pair_%s_start" % wave, _pair_copies, len(grads), grads, _pair_lands(grads))
        return behind

    def reduce_midway(wave, after):
        st = waves[wave]
        grads, recv = _exchange_wait("pair_%s_wait" % wave, _pair_copies, st["pair"], after)
        sums = [_pair_sum(g, r, place, name="pair_sum_%s_%d" % (wave, i))
                for i, (g, r) in enumerate(zip(grads, recv))]
        st["f32"] = [s32 for _, s32 in sums]
        b16 = [s16 for s16, _ in sums]
        st["chip"], behind = _exchange_start(
            "chip_%s_start" % wave, _chip_copies, len(b16) * (N_CHIPS - 1), b16, _same_lands(b16))
        return behind

    def reduce_finish(wave, after):
        st = waves[wave]
        _, got = _exchange_wait("chip_%s_wait" % wave, _chip_copies, st["chip"], after)
        return [_chip_sum(s32, r, place, name="chip_sum_%s_%d" % (wave, i))
                for i, (s32, r) in enumerate(zip(st["f32"], got))]

    dx, big0, rows = _local_step(xs, target, pw, fetch, reduce_start, reduce_midway,
                                 conv_w, conv_b, gate_g,
                                 norm_mix_g + behind_gather, norm_mlp_g, pool_b, pool_scale,
                                 ssm_dt_bias, ssm_a_log, ssm_d, final_g)

    behind = reduce_start("layer0", big0)
    small = jnp.concatenate(rows, axis=0)
    small = jnp.pad(small, ((0, (-small.shape[0]) % 8), (0, 0))) + behind
    small = _all_reduce_small(small, name="all_reduce_small", sum_row0=True)
    behind = reduce_midway("layer0", small)
    h_w1_1, h_w2_1, h_wout = reduce_finish("mlp1_out", behind)
    (h_win,) = reduce_finish("in_proj", behind)
    g_w1_1, g_w2_1, g_wout_s, g_win_s = _pair_gather_halves([h_w1_1, h_w2_1, h_wout, h_win],
                                                            name="pair_gather_layer1")
    loss = small[0, 0]
    g_final = small[1]
    g_norm_mix = small[2:4]
    g_norm_mlp = small[4:6]
    g_pool_b, g_pool_scale = small[6:7], small[7:8]
    g_alog, g_dtb, g_dsk = small[8:9, :N_HEADS], small[9:10, :N_HEADS], small[10:11, :N_HEADS]
    g_gate_full = small[11:13].reshape(1, D_INNER)
    g_convw_full = small[13:25].reshape(CONV_K, CONV_DIM)
    g_convb_full = small[25:28].reshape(1, CONV_DIM)
    g_gate = lax.dynamic_slice_in_dim(g_gate_full, my_chip * (D_INNER // N_CHIPS), D_INNER // N_CHIPS, axis=1)
    g_convw = lax.dynamic_slice_in_dim(g_convw_full, my_chip * (CONV_DIM // N_CHIPS), CONV_DIM // N_CHIPS, axis=1)
    g_convb = lax.dynamic_slice_in_dim(g_convb_full, my_chip * (CONV_DIM // N_CHIPS), CONV_DIM // N_CHIPS, axis=1)

    grads = {
        "norm_mix_g": g_norm_mix, "norm_mlp_g": g_norm_mlp,
        "pool_b": g_pool_b, "pool_scale": g_pool_scale,
        "ssm_conv_w": g_convw.reshape(ssm_conv_w.shape),
        "ssm_conv_b": g_convb, "ssm_dt_bias": g_dtb, "ssm_a_log": g_alog, "ssm_d": g_dsk,
        "ssm_norm_g": g_gate, "ssm_w_out": g_wout_s.reshape(ssm_w_out.shape),
        "final_g": g_final,
    }
    weights = dict(norm_mix_g=norm_mix_g, norm_mlp_g=norm_mlp_g, pool_w=pool_w, pool_b=pool_b,
                   pool_scale=pool_scale, ssm_w_in=ssm_w_in, ssm_conv_w=ssm_conv_w, ssm_conv_b=ssm_conv_b,
                   ssm_dt_bias=ssm_dt_bias, ssm_a_log=ssm_a_log, ssm_d=ssm_d, ssm_norm_g=ssm_norm_g,
                   ssm_w_out=ssm_w_out, mlp_w1=mlp_w1, mlp_w2=mlp_w2, final_g=final_g)
    moms = dict(norm_mix_g=(m_norm_mix_g, v_norm_mix_g), norm_mlp_g=(m_norm_mlp_g, v_norm_mlp_g),
                pool_w=(m_pool_w, v_pool_w), pool_b=(m_pool_b, v_pool_b),
                pool_scale=(m_pool_scale, v_pool_scale), ssm_w_in=(m_ssm_w_in, v_ssm_w_in),
                ssm_conv_w=(m_ssm_conv_w, v_ssm_conv_w), ssm_conv_b=(m_ssm_conv_b, v_ssm_conv_b),
                ssm_dt_bias=(m_ssm_dt_bias, v_ssm_dt_bias), ssm_a_log=(m_ssm_a_log, v_ssm_a_log),
                ssm_d=(m_ssm_d, v_ssm_d), ssm_norm_g=(m_ssm_norm_g, v_ssm_norm_g),
                ssm_w_out=(m_ssm_w_out, v_ssm_w_out), mlp_w1=(m_mlp_w1, v_mlp_w1),
                mlp_w2=(m_mlp_w2, v_mlp_w2), final_g=(m_final_g, v_final_g))
    names = list(weights)
    big_names = ("pool_w", "ssm_w_in", "ssm_w_out", "mlp_w1", "mlp_w2")
    deltas, new_m, new_v = {}, {}, {}

    def as_rows(nm, a):
        return a[0].T if nm == "ssm_w_in" else a.reshape(-1, a.shape[-1])

    def from_rows(nm, r):
        return r.T[None] if nm == "ssm_w_in" else r.reshape(weights[nm].shape)

    def update(nm, grad_rows, layer=None, into=None):
        return _adamw(as_rows(nm, weights[nm]), grad_rows, as_rows(nm, moms[nm][0]), as_rows(nm, moms[nm][1]),
                      name="adamw_%s_%s" % (nm, layer), part=layer, into=into)

    def keep(nm, results):
        deltas[nm], new_m[nm], new_v[nm], grads[nm] = [from_rows(nm, r) for r in results]

    keep("ssm_w_in", update("ssm_w_in", g_win_s[:IN_PROJ_DIM // N_CHIPS]))
    keep("ssm_w_out", update("ssm_w_out", g_wout_s))
    w1_done = update("mlp_w1", g_w1_1, layer=1)
    w2_done = update("mlp_w2", g_w2_1, layer=1)
    small_names = [nm for nm in names if nm not in big_names]
    small_done = _adamw_small(
        [tuple(as_rows(nm, a) for a in (weights[nm], grads[nm], moms[nm][0], moms[nm][1])) for nm in small_names],
        name="adamw_small")
    for nm, (d_, m_, v_) in zip(small_names, small_done):
        deltas[nm], new_m[nm], new_v[nm] = [from_rows(nm, r) for r in (d_, m_, v_)]

    above = (deltas["ssm_w_in"][0, 0, 0] + deltas["ssm_w_out"][0, 0, 0] + w1_done[0][-1, -1]
             + w2_done[0][-1, -1] + small_done[0][0][0, 0])
    g_pool_w, g_w1_0, g_w2_0 = _pair_gather_halves(reduce_finish("layer0", above), name="pair_gather_layer0")
    keep("mlp_w1", update("mlp_w1", g_w1_0, layer=0, into=w1_done))
    keep("mlp_w2", update("mlp_w2", g_w2_0, layer=0, into=w2_done))
    keep("pool_w", update("pool_w", g_pool_w))

    grad_x = dx.reshape(x.shape)
    out_grads = [grads[nm].reshape(weights[nm].shape) for nm in names]
    return (loss, grad_x, *out_grads, *[deltas[nm] for nm in names],
            *[new_m[nm] for nm in names], *[new_v[nm] for nm in names])
```
